```python
import math
import jax, jax.numpy as jnp
from jax import lax
import numpy as np

D_MODEL = 1024
BATCH = 8
SEQ = 4096
DEPTH = 1

D_RNN = 1280
RNN_BLOCK = 64
N_RNN_BLOCKS = D_RNN // RNN_BLOCK
CONV_WIDTH = 4
LRU_C = 8.0
N_HEADS_MLA = 16
QK_NOPE = 64
QK_ROPE = 32
V_HEAD = 64
Q_LORA = 384
KV_LORA = 256
ROPE_THETA = 10000.0
Q_BLOCK = 128
D_FF = 2816
FFN_CONV = 3
EPS = 1e-6
D_IN = D_RNN + Q_LORA + KV_LORA + QK_ROPE + 2 * D_MODEL

kernel_name = "hybrid_rglru_mla_convffn_adaln"


def rmsnorm(x, g):
    xf = x.astype(jnp.float32)
    y = xf * lax.rsqrt(jnp.mean(xf * xf, axis=-1, keepdims=True) + EPS)
    return (y * g.astype(jnp.float32)).astype(x.dtype)


def causal_dwconv(x, w, b):
    K = w.shape[0]
    S = x.shape[1]
    xp = jnp.pad(x, ((0, 0), (K - 1, 0), (0, 0)))
    y = xp[:, 0:S] * w[0]
    for k in range(1, K):
        y = y + xp[:, k:k + S] * w[k]
    return y + b


def rotary(x, positions):
    half = QK_ROPE // 2
    inv_freq = ROPE_THETA ** (-jnp.arange(half, dtype=jnp.float32) / half)
    ang = positions.astype(jnp.float32)[..., None] * inv_freq
    ang = ang.reshape(ang.shape[:2] + (1,) * (x.ndim - 3) + (half,))
    cos, sin = jnp.cos(ang), jnp.sin(ang)
    xf = x.astype(jnp.float32)
    x1, x2 = xf[..., :half], xf[..., half:]
    out = jnp.concatenate([x1 * cos - x2 * sin, x2 * cos + x1 * sin], axis=-1)
    return out.astype(x.dtype)


def rg_lru(xc, positions, w_a, b_a, w_x, b_x, lru_param):
    B, S, W = xc.shape
    xb = xc.reshape(B, S, N_RNN_BLOCKS, RNN_BLOCK)
    r = jax.nn.sigmoid(jnp.einsum('bsnd,nde->bsne', xb, w_a).reshape(B, S, W) + b_a)
    i = jax.nn.sigmoid(jnp.einsum('bsnd,nde->bsne', xb, w_x).reshape(B, S, W) + b_x)
    log_a = -LRU_C * r.astype(jnp.float32) * jax.nn.softplus(-lru_param.astype(jnp.float32))
    a = jnp.exp(log_a)
    mult = jnp.sqrt(-jnp.expm1(2.0 * log_a))
    reset = (positions == 0)[..., None]
    a = jnp.where(reset, 0.0, a)
    mult = jnp.where(reset, 1.0, mult)
    bterm = mult * (i * xc).astype(jnp.float32)

    def combine(lhs, rhs):
        a1, b1 = lhs
        a2, b2 = rhs
        return a1 * a2, a2 * b1 + b2

    _, h = lax.associative_scan(combine, (a, bterm), axis=1)
    return h.astype(xc.dtype)


def causal_block_attention(q, k, v):
    B, S, H, Dk = q.shape
    Dv = v.shape[-1]
    nb = S // Q_BLOCK
    scale = 1.0 / math.sqrt(Dk)
    qb = q.reshape(B, nb, Q_BLOCK, H, Dk).transpose(1, 0, 2, 3, 4)
    kpos = jnp.arange(S)

    def one_block(args):
        qi, blk = args
        s = jnp.einsum('bqhd,bkhd->bhqk', qi, k).astype(jnp.float32) * scale
        qpos = blk * Q_BLOCK + jnp.arange(Q_BLOCK)
        mask = kpos[None, :] <= qpos[:, None]
        s = jnp.where(mask[None, None], s, -jnp.inf)
        p = jax.nn.softmax(s, axis=-1).astype(v.dtype)
        return jnp.einsum('bhqk,bkhd->bqhd', p, v)

    o = lax.map(one_block, (qb, jnp.arange(nb)))
    return o.transpose(1, 0, 2, 3, 4).reshape(B, S, H * Dv)


def _fwd_setup_inputs(seed: int = 0) -> dict:
    key = jax.random.key(seed)
    ks = jax.random.split(key, 32)

    def nrm(k, shape, fan_in, mult=1.0):
        return jax.random.normal(k, shape, jnp.float32) * (mult * fan_in ** -0.5)

    def gain(k, shape):
        return 1.0 + 0.02 * jax.random.normal(k, shape, jnp.float32)

    def bias(k, shape):
        return 0.01 * jax.random.normal(k, shape, jnp.float32)

    L = DEPTH
    a0 = jax.random.uniform(ks[10], (L, D_RNN), jnp.float32, 0.9, 0.999)
    s0 = a0 ** (1.0 / LRU_C)
    lru_param = jnp.log(s0) - jnp.log1p(-s0)
    positions = (jnp.arange(SEQ, dtype=jnp.int32)[None, :]
                 + jax.random.randint(ks[31], (BATCH, 1), 0, SEQ, dtype=jnp.int32))
    return {
        "x": jax.random.normal(ks[0], (BATCH, SEQ, D_MODEL), jnp.float32),
        "c": jax.random.normal(ks[1], (BATCH, D_MODEL), jnp.float32),
        "positions": positions,
        "w_ada": nrm(ks[2], (L, D_MODEL, 6 * D_MODEL), D_MODEL, 0.5),
        "b_ada": bias(ks[3], (L, 6 * D_MODEL)),
        "norm1_g": gain(ks[4], (L, D_MODEL)),
        "w_in": nrm(ks[5], (L, D_MODEL, D_IN), D_MODEL),
        "conv_w": nrm(ks[6], (L, CONV_WIDTH, D_RNN), CONV_WIDTH),
        "conv_b": bias(ks[7], (L, D_RNN)),
        "w_gate_a": nrm(ks[8], (L, N_RNN_BLOCKS, RNN_BLOCK, RNN_BLOCK), RNN_BLOCK),
        "b_gate_a": bias(ks[9], (L, D_RNN)),
        "w_gate_x": nrm(ks[11], (L, N_RNN_BLOCKS, RNN_BLOCK, RNN_BLOCK), RNN_BLOCK),
        "b_gate_x": bias(ks[12], (L, D_RNN)),
        "lru_param": lru_param,
        "q_norm_g": gain(ks[13], (L, Q_LORA)),
        "w_uq": nrm(ks[14], (L, Q_LORA, N_HEADS_MLA * (QK_NOPE + QK_ROPE)), Q_LORA),
        "kv_norm_g": gain(ks[15], (L, KV_LORA)),
        "w_ukv": nrm(ks[16], (L, KV_LORA, N_HEADS_MLA * (QK_NOPE + V_HEAD)), KV_LORA),
        "w_proj_rnn": nrm(ks[17], (L, D_RNN, D_MODEL), D_RNN),
        "w_proj_mla": nrm(ks[18], (L, N_HEADS_MLA * V_HEAD, D_MODEL), N_HEADS_MLA * V_HEAD),
        "w_out": nrm(ks[19], (L, D_MODEL, D_MODEL), D_MODEL),
        "norm2_g": gain(ks[20], (L, D_MODEL)),
        "w_up": nrm(ks[21], (L, D_MODEL, 2 * D_FF), D_MODEL),
        "ffn_conv_w": nrm(ks[22], (L, FFN_CONV, 2 * D_FF), FFN_CONV),
        "ffn_conv_b": bias(ks[23], (L, 2 * D_FF)),
        "w_down": nrm(ks[24], (L, D_FF, D_MODEL), D_FF),
        "final_g": gain(ks[25], (D_MODEL,)),
    }


def _fwd_reference(x, c, positions, w_ada, b_ada, norm1_g, w_in, conv_w, conv_b,
              w_gate_a, b_gate_a, w_gate_x, b_gate_x, lru_param, q_norm_g, w_uq,
              kv_norm_g, w_ukv, w_proj_rnn, w_proj_mla, w_out, norm2_g, w_up,
              ffn_conv_w, ffn_conv_b, w_down, final_g):
    B, S, D = x.shape
    H = N_HEADS_MLA
    c_act = jax.nn.silu(c)
    for l in range(DEPTH):
        mod = c_act @ w_ada[l] + b_ada[l]
        shift1, scale1, gate1, shift2, scale2, gate2 = [m[:, None, :] for m in jnp.split(mod, 6, axis=-1)]

        h = rmsnorm(x, norm1_g[l]) * (1.0 + scale1) + shift1
        proj = h @ w_in[l]
        o0 = D_RNN
        o1 = o0 + Q_LORA
        o2 = o1 + KV_LORA
        o3 = o2 + QK_ROPE
        o4 = o3 + D_MODEL
        x_rnn, q_lat, kv_lat, k_rope, g_rnn, g_mla = (
            proj[..., :o0], proj[..., o0:o1], proj[..., o1:o2],
            proj[..., o2:o3], proj[..., o3:o4], proj[..., o4:])

        xc = causal_dwconv(x_rnn, conv_w[l], conv_b[l])
        y_rnn = rg_lru(xc, positions, w_gate_a[l], b_gate_a[l], w_gate_x[l], b_gate_x[l], lru_param[l])

        q = (rmsnorm(q_lat, q_norm_g[l]) @ w_uq[l]).reshape(B, S, H, QK_NOPE + QK_ROPE)
        q_rope = rotary(q[..., QK_NOPE:], positions)
        q = jnp.concatenate([q[..., :QK_NOPE], q_rope], axis=-1)
        kv = (rmsnorm(kv_lat, kv_norm_g[l]) @ w_ukv[l]).reshape(B, S, H, QK_NOPE + V_HEAD)
        k_nope, v = kv[..., :QK_NOPE], kv[..., QK_NOPE:]
        k_r = jnp.broadcast_to(rotary(k_rope, positions)[:, :, None, :], (B, S, H, QK_ROPE))
        k = jnp.concatenate([k_nope, k_r], axis=-1)
        y_mla = causal_block_attention(q, k, v)

        merged = (jax.nn.sigmoid(g_rnn) * (y_rnn @ w_proj_rnn[l])
                  + jax.nn.sigmoid(g_mla) * (y_mla @ w_proj_mla[l]))
        x = x + gate1 * (merged @ w_out[l])

        h2 = rmsnorm(x, norm2_g[l]) * (1.0 + scale2) + shift2
        u = causal_dwconv(h2 @ w_up[l], ffn_conv_w[l], ffn_conv_b[l])
        u_gate, u_val = u[..., :D_FF], u[..., D_FF:]
        x = x + gate2 * ((jax.nn.silu(u_gate) * u_val) @ w_down[l])

    return rmsnorm(x, final_g)


import jax as _jax
import jax.numpy as _jnp

TWIN_FORMAT = 'train_step'
FWD_PARAMS = ['x', 'c', 'positions', 'w_ada', 'b_ada', 'norm1_g', 'w_in', 'conv_w', 'conv_b', 'w_gate_a', 'b_gate_a', 'w_gate_x', 'b_gate_x', 'lru_param', 'q_norm_g', 'w_uq', 'kv_norm_g', 'w_ukv', 'w_proj_rnn', 'w_proj_mla', 'w_out', 'norm2_g', 'w_up', 'ffn_conv_w', 'ffn_conv_b', 'w_down', 'final_g']
TWIN_WEIGHTS = ['w_ada', 'b_ada', 'norm1_g', 'w_in', 'conv_w', 'conv_b', 'w_gate_a', 'b_gate_a', 'w_gate_x', 'b_gate_x', 'lru_param', 'q_norm_g', 'w_uq', 'kv_norm_g', 'w_ukv', 'w_proj_rnn', 'w_proj_mla', 'w_out', 'norm2_g', 'w_up', 'ffn_conv_w', 'ffn_conv_b', 'w_down', 'final_g']
TWIN_DIFF_INPUT = 'x'
TWIN_INPUTS = ['x', 'c', 'positions', 'w_ada', 'b_ada', 'norm1_g', 'w_in', 'conv_w', 'conv_b', 'w_gate_a', 'b_gate_a', 'w_gate_x', 'b_gate_x', 'lru_param', 'q_norm_g', 'w_uq', 'kv_norm_g', 'w_ukv', 'w_proj_rnn', 'w_proj_mla', 'w_out', 'norm2_g', 'w_up', 'ffn_conv_w', 'ffn_conv_b', 'w_down', 'final_g', 'loss_target', 'm_w_ada', 'm_b_ada', 'm_norm1_g', 'm_w_in', 'm_conv_w', 'm_conv_b', 'm_w_gate_a', 'm_b_gate_a', 'm_w_gate_x', 'm_b_gate_x', 'm_lru_param', 'm_q_norm_g', 'm_w_uq', 'm_kv_norm_g', 'm_w_ukv', 'm_w_proj_rnn', 'm_w_proj_mla', 'm_w_out', 'm_norm2_g', 'm_w_up', 'm_ffn_conv_w', 'm_ffn_conv_b', 'm_w_down', 'm_final_g', 'v_w_ada', 'v_b_ada', 'v_norm1_g', 'v_w_in', 'v_conv_w', 'v_conv_b', 'v_w_gate_a', 'v_b_gate_a', 'v_w_gate_x', 'v_b_gate_x', 'v_lru_param', 'v_q_norm_g', 'v_w_uq', 'v_kv_norm_g', 'v_w_ukv', 'v_w_proj_rnn', 'v_w_proj_mla', 'v_w_out', 'v_norm2_g', 'v_w_up', 'v_ffn_conv_w', 'v_ffn_conv_b', 'v_w_down', 'v_final_g']
TWIN_OUTPUTS = ['loss', 'grad_x', 'grad_w_ada', 'grad_b_ada', 'grad_norm1_g', 'grad_w_in', 'grad_conv_w', 'grad_conv_b', 'grad_w_gate_a', 'grad_b_gate_a', 'grad_w_gate_x', 'grad_b_gate_x', 'grad_lru_param', 'grad_q_norm_g', 'grad_w_uq', 'grad_kv_norm_g', 'grad_w_ukv', 'grad_w_proj_rnn', 'grad_w_proj_mla', 'grad_w_out', 'grad_norm2_g', 'grad_w_up', 'grad_ffn_conv_w', 'grad_ffn_conv_b', 'grad_w_down', 'grad_final_g', 'delta_w_ada', 'delta_b_ada', 'delta_norm1_g', 'delta_w_in', 'delta_conv_w', 'delta_conv_b', 'delta_w_gate_a', 'delta_b_gate_a', 'delta_w_gate_x', 'delta_b_gate_x', 'delta_lru_param', 'delta_q_norm_g', 'delta_w_uq', 'delta_kv_norm_g', 'delta_w_ukv', 'delta_w_proj_rnn', 'delta_w_proj_mla', 'delta_w_out', 'delta_norm2_g', 'delta_w_up', 'delta_ffn_conv_w', 'delta_ffn_conv_b', 'delta_w_down', 'delta_final_g', 'new_m_w_ada', 'new_m_b_ada', 'new_m_norm1_g', 'new_m_w_in', 'new_m_conv_w', 'new_m_conv_b', 'new_m_w_gate_a', 'new_m_b_gate_a', 'new_m_w_gate_x', 'new_m_b_gate_x', 'new_m_lru_param', 'new_m_q_norm_g', 'new_m_w_uq', 'new_m_kv_norm_g', 'new_m_w_ukv', 'new_m_w_proj_rnn', 'new_m_w_proj_mla', 'new_m_w_out', 'new_m_norm2_g', 'new_m_w_up', 'new_m_ffn_conv_w', 'new_m_ffn_conv_b', 'new_m_w_down', 'new_m_final_g', 'new_v_w_ada', 'new_v_b_ada', 'new_v_norm1_g', 'new_v_w_in', 'new_v_conv_w', 'new_v_conv_b', 'new_v_w_gate_a', 'new_v_b_gate_a', 'new_v_w_gate_x', 'new_v_b_gate_x', 'new_v_lru_param', 'new_v_q_norm_g', 'new_v_w_uq', 'new_v_kv_norm_g', 'new_v_w_ukv', 'new_v_w_proj_rnn', 'new_v_w_proj_mla', 'new_v_w_out', 'new_v_norm2_g', 'new_v_w_up', 'new_v_ffn_conv_w', 'new_v_ffn_conv_b', 'new_v_w_down', 'new_v_final_g']
TWIN_LEAF_KINDS = {'loss': 'loss', 'grad_x': 'grad_x', 'grad_w_ada': 'grad_w', 'grad_b_ada': 'grad_w', 'grad_norm1_g': 'grad_w', 'grad_w_in': 'grad_w', 'grad_conv_w': 'grad_w', 'grad_conv_b': 'grad_w', 'grad_w_gate_a': 'grad_w', 'grad_b_gate_a': 'grad_w', 'grad_w_gate_x': 'grad_w', 'grad_b_gate_x': 'grad_w', 'grad_lru_param': 'grad_w', 'grad_q_norm_g': 'grad_w', 'grad_w_uq': 'grad_w', 'grad_kv_norm_g': 'grad_w', 'grad_w_ukv': 'grad_w', 'grad_w_proj_rnn': 'grad_w', 'grad_w_proj_mla': 'grad_w', 'grad_w_out': 'grad_w', 'grad_norm2_g': 'grad_w', 'grad_w_up': 'grad_w', 'grad_ffn_conv_w': 'grad_w', 'grad_ffn_conv_b': 'grad_w', 'grad_w_down': 'grad_w', 'grad_final_g': 'grad_w', 'delta_w_ada': 'delta_w', 'delta_b_ada': 'delta_w', 'delta_norm1_g': 'delta_w', 'delta_w_in': 'delta_w', 'delta_conv_w': 'delta_w', 'delta_conv_b': 'delta_w', 'delta_w_gate_a': 'delta_w', 'delta_b_gate_a': 'delta_w', 'delta_w_gate_x': 'delta_w', 'delta_b_gate_x': 'delta_w', 'delta_lru_param': 'delta_w', 'delta_q_norm_g': 'delta_w', 'delta_w_uq': 'delta_w', 'delta_kv_norm_g': 'delta_w', 'delta_w_ukv': 'delta_w', 'delta_w_proj_rnn': 'delta_w', 'delta_w_proj_mla': 'delta_w', 'delta_w_out': 'delta_w', 'delta_norm2_g': 'delta_w', 'delta_w_up': 'delta_w', 'delta_ffn_conv_w': 'delta_w', 'delta_ffn_conv_b': 'delta_w', 'delta_w_down': 'delta_w', 'delta_final_g': 'delta_w', 'new_m_w_ada': 'new_m', 'new_m_b_ada': 'new_m', 'new_m_norm1_g': 'new_m', 'new_m_w_in': 'new_m', 'new_m_conv_w': 'new_m', 'new_m_conv_b': 'new_m', 'new_m_w_gate_a': 'new_m', 'new_m_b_gate_a': 'new_m', 'new_m_w_gate_x': 'new_m', 'new_m_b_gate_x': 'new_m', 'new_m_lru_param': 'new_m', 'new_m_q_norm_g': 'new_m', 'new_m_w_uq': 'new_m', 'new_m_kv_norm_g': 'new_m', 'new_m_w_ukv': 'new_m', 'new_m_w_proj_rnn': 'new_m', 'new_m_w_proj_mla': 'new_m', 'new_m_w_out': 'new_m', 'new_m_norm2_g': 'new_m', 'new_m_w_up': 'new_m', 'new_m_ffn_conv_w': 'new_m', 'new_m_ffn_conv_b': 'new_m', 'new_m_w_down': 'new_m', 'new_m_final_g': 'new_m', 'new_v_w_ada': 'new_v', 'new_v_b_ada': 'new_v', 'new_v_norm1_g': 'new_v', 'new_v_w_in': 'new_v', 'new_v_conv_w': 'new_v', 'new_v_conv_b': 'new_v', 'new_v_w_gate_a': 'new_v', 'new_v_b_gate_a': 'new_v', 'new_v_w_gate_x': 'new_v', 'new_v_b_gate_x': 'new_v', 'new_v_lru_param': 'new_v', 'new_v_q_norm_g': 'new_v', 'new_v_w_uq': 'new_v', 'new_v_kv_norm_g': 'new_v', 'new_v_w_ukv': 'new_v', 'new_v_w_proj_rnn': 'new_v', 'new_v_w_proj_mla': 'new_v', 'new_v_w_out': 'new_v', 'new_v_norm2_g': 'new_v', 'new_v_w_up': 'new_v', 'new_v_ffn_conv_w': 'new_v', 'new_v_ffn_conv_b': 'new_v', 'new_v_w_down': 'new_v', 'new_v_final_g': 'new_v'}


def _forward(args):
    return _fwd_reference(*[args[k] for k in FWD_PARAMS])


def _output_shape():
    out = _jax.eval_shape(lambda: _forward(_fwd_setup_inputs(0)))
    return out.shape, out.dtype

N_MICROBATCH = 1
ADAM_LR = 0.001
ADAM_B1 = 0.9
ADAM_B2 = 0.999
ADAM_EPS = 1e-08
ADAM_WD = 0.01
ADAM_STEP = 10
PER_EXAMPLE_BATCH_AXIS = {'x': 0, 'c': 0, 'positions': 0, 'loss_target': 0}
SHARED_INPUTS = []
_WEIGHT_DTYPES = {'w_ada': _jnp.float32, 'b_ada': _jnp.float32, 'norm1_g': _jnp.float32, 'w_in': _jnp.float32, 'conv_w': _jnp.float32, 'conv_b': _jnp.float32, 'w_gate_a': _jnp.float32, 'b_gate_a': _jnp.float32, 'w_gate_x': _jnp.float32, 'b_gate_x': _jnp.float32, 'lru_param': _jnp.float32, 'q_norm_g': _jnp.float32, 'w_uq': _jnp.float32, 'kv_norm_g': _jnp.float32, 'w_ukv': _jnp.float32, 'w_proj_rnn': _jnp.float32, 'w_proj_mla': _jnp.float32, 'w_out': _jnp.float32, 'norm2_g': _jnp.float32, 'w_up': _jnp.float32, 'ffn_conv_w': _jnp.float32, 'ffn_conv_b': _jnp.float32, 'w_down': _jnp.float32, 'final_g': _jnp.float32}
MOMENT_SCALE = {'w_ada': 9.785805e-02, 'b_ada': 1.589993e-01, 'norm1_g': 2.950328e-02, 'w_in': 4.363157e-02, 'conv_w': 7.856375e-02, 'conv_b': 2.532264e-01, 'w_gate_a': 9.810760e-03, 'b_gate_a': 1.337748e-02, 'w_gate_x': 1.864887e-02, 'b_gate_x': 2.634271e-02, 'lru_param': 3.612669e-02, 'q_norm_g': 8.878610e-03, 'w_uq': 4.315614e-03, 'kv_norm_g': 2.268722e-02, 'w_ukv': 7.974577e-03, 'w_proj_rnn': 8.121148e-02, 'w_proj_mla': 1.058762e-02, 'w_out': 7.812412e-02, 'norm2_g': 5.548143e-02, 'w_up': 2.380161e-02, 'ffn_conv_w': 2.360201e-02, 'ffn_conv_b': 2.097115e-02, 'w_down': 3.866672e-02, 'final_g': 3.213481e+01}


def _to_microbatches(a, axis):
    t = _jnp.moveaxis(a, axis, 0)
    t = t.reshape((N_MICROBATCH, t.shape[0] // N_MICROBATCH) + t.shape[1:])
    return _jnp.moveaxis(t, 1, axis + 1)


def setup_inputs(seed: int = 0) -> dict:
    inp = _fwd_setup_inputs(seed)
    key = _jax.random.fold_in(_jax.random.key(seed), 7919)
    shape, _ = _output_shape()
    out = dict(inp)
    out["loss_target"] = _jax.random.normal(_jax.random.fold_in(key, 0), shape, _jnp.float32)
    for i, name in enumerate(TWIN_WEIGHTS):
        w = inp[name].astype(_jnp.float32)
        if MOMENT_SCALE is None:
            s = _jnp.sqrt(_jnp.mean(_jnp.square(w)) + 1e-30)
        else:
            s = MOMENT_SCALE[name]
        km, kv = _jax.random.split(_jax.random.fold_in(key, i + 1))
        out[name] = w
        out["m_" + name] = s * _jax.random.normal(km, w.shape, _jnp.float32)
        out["v_" + name] = (s * s) * _jax.random.uniform(kv, w.shape, _jnp.float32, 0.5, 1.5)
    if N_MICROBATCH > 1:
        for name, axis in PER_EXAMPLE_BATCH_AXIS.items():
            out[name] = _to_microbatches(out[name], axis)
    return {'x': out['x'], 'c': out['c'], 'positions': out['positions'], 'w_ada': out['w_ada'], 'b_ada': out['b_ada'], 'norm1_g': out['norm1_g'], 'w_in': out['w_in'], 'conv_w': out['conv_w'], 'conv_b': out['conv_b'], 'w_gate_a': out['w_gate_a'], 'b_gate_a': out['b_gate_a'], 'w_gate_x': out['w_gate_x'], 'b_gate_x': out['b_gate_x'], 'lru_param': out['lru_param'], 'q_norm_g': out['q_norm_g'], 'w_uq': out['w_uq'], 'kv_norm_g': out['kv_norm_g'], 'w_ukv': out['w_ukv'], 'w_proj_rnn': out['w_proj_rnn'], 'w_proj_mla': out['w_proj_mla'], 'w_out': out['w_out'], 'norm2_g': out['norm2_g'], 'w_up': out['w_up'], 'ffn_conv_w': out['ffn_conv_w'], 'ffn_conv_b': out['ffn_conv_b'], 'w_down': out['w_down'], 'final_g': out['final_g'], 'loss_target': out['loss_target'], 'm_w_ada': out['m_w_ada'], 'm_b_ada': out['m_b_ada'], 'm_norm1_g': out['m_norm1_g'], 'm_w_in': out['m_w_in'], 'm_conv_w': out['m_conv_w'], 'm_conv_b': out['m_conv_b'], 'm_w_gate_a': out['m_w_gate_a'], 'm_b_gate_a': out['m_b_gate_a'], 'm_w_gate_x': out['m_w_gate_x'], 'm_b_gate_x': out['m_b_gate_x'], 'm_lru_param': out['m_lru_param'], 'm_q_norm_g': out['m_q_norm_g'], 'm_w_uq': out['m_w_uq'], 'm_kv_norm_g': out['m_kv_norm_g'], 'm_w_ukv': out['m_w_ukv'], 'm_w_proj_rnn': out['m_w_proj_rnn'], 'm_w_proj_mla': out['m_w_proj_mla'], 'm_w_out': out['m_w_out'], 'm_norm2_g': out['m_norm2_g'], 'm_w_up': out['m_w_up'], 'm_ffn_conv_w': out['m_ffn_conv_w'], 'm_ffn_conv_b': out['m_ffn_conv_b'], 'm_w_down': out['m_w_down'], 'm_final_g': out['m_final_g'], 'v_w_ada': out['v_w_ada'], 'v_b_ada': out['v_b_ada'], 'v_norm1_g': out['v_norm1_g'], 'v_w_in': out['v_w_in'], 'v_conv_w': out['v_conv_w'], 'v_conv_b': out['v_conv_b'], 'v_w_gate_a': out['v_w_gate_a'], 'v_b_gate_a': out['v_b_gate_a'], 'v_w_gate_x': out['v_w_gate_x'], 'v_b_gate_x': out['v_b_gate_x'], 'v_lru_param': out['v_lru_param'], 'v_q_norm_g': out['v_q_norm_g'], 'v_w_uq': out['v_w_uq'], 'v_kv_norm_g': out['v_kv_norm_g'], 'v_w_ukv': out['v_w_ukv'], 'v_w_proj_rnn': out['v_w_proj_rnn'], 'v_w_proj_mla': out['v_w_proj_mla'], 'v_w_out': out['v_w_out'], 'v_norm2_g': out['v_norm2_g'], 'v_w_up': out['v_w_up'], 'v_ffn_conv_w': out['v_ffn_conv_w'], 'v_ffn_conv_b': out['v_ffn_conv_b'], 'v_w_down': out['v_w_down'], 'v_final_g': out['v_final_g']}


def _loss(weights, diff, rest, loss_target):
    with _jax.named_scope("forward"):
        args = {**rest, TWIN_DIFF_INPUT: diff, **{k: w.astype(_WEIGHT_DTYPES[k]) for k, w in weights.items()}}
        y = _forward(args)
    with _jax.named_scope("loss_head"):
        err = _jnp.square(y.astype(_jnp.float32) - loss_target)
        return 0.5 * _jnp.sum(_jnp.mean(err, axis=-1)) if err.ndim else 0.5 * err


def _adamw(w, g, m, v):
    m = ADAM_B1 * m + (1.0 - ADAM_B1) * g
    v = ADAM_B2 * v + (1.0 - ADAM_B2) * _jnp.square(g)
    m_hat = m / (1.0 - ADAM_B1 ** ADAM_STEP)
    v_hat = v / (1.0 - ADAM_B2 ** ADAM_STEP)
    delta = -ADAM_LR * (m_hat / (_jnp.sqrt(v_hat) + ADAM_EPS) + ADAM_WD * w)
    return delta, m, v


def reference(x, c, positions, w_ada, b_ada, norm1_g, w_in, conv_w, conv_b, w_gate_a, b_gate_a, w_gate_x, b_gate_x, lru_param, q_norm_g, w_uq, kv_norm_g, w_ukv, w_proj_rnn, w_proj_mla, w_out, norm2_g, w_up, ffn_conv_w, ffn_conv_b, w_down, final_g, loss_target, m_w_ada, m_b_ada, m_norm1_g, m_w_in, m_conv_w, m_conv_b, m_w_gate_a, m_b_gate_a, m_w_gate_x, m_b_gate_x, m_lru_param, m_q_norm_g, m_w_uq, m_kv_norm_g, m_w_ukv, m_w_proj_rnn, m_w_proj_mla, m_w_out, m_norm2_g, m_w_up, m_ffn_conv_w, m_ffn_conv_b, m_w_down, m_final_g, v_w_ada, v_b_ada, v_norm1_g, v_w_in, v_conv_w, v_conv_b, v_w_gate_a, v_b_gate_a, v_w_gate_x, v_b_gate_x, v_lru_param, v_q_norm_g, v_w_uq, v_kv_norm_g, v_w_ukv, v_w_proj_rnn, v_w_proj_mla, v_w_out, v_norm2_g, v_w_up, v_ffn_conv_w, v_ffn_conv_b, v_w_down, v_final_g):
    given = dict(x=x, c=c, positions=positions, w_ada=w_ada, b_ada=b_ada, norm1_g=norm1_g, w_in=w_in, conv_w=conv_w, conv_b=conv_b, w_gate_a=w_gate_a, b_gate_a=b_gate_a, w_gate_x=w_gate_x, b_gate_x=b_gate_x, lru_param=lru_param, q_norm_g=q_norm_g, w_uq=w_uq, kv_norm_g=kv_norm_g, w_ukv=w_ukv, w_proj_rnn=w_proj_rnn, w_proj_mla=w_proj_mla, w_out=w_out, norm2_g=norm2_g, w_up=w_up, ffn_conv_w=ffn_conv_w, ffn_conv_b=ffn_conv_b, w_down=w_down, final_g=final_g, loss_target=loss_target, m_w_ada=m_w_ada, m_b_ada=m_b_ada, m_norm1_g=m_norm1_g, m_w_in=m_w_in, m_conv_w=m_conv_w, m_conv_b=m_conv_b, m_w_gate_a=m_w_gate_a, m_b_gate_a=m_b_gate_a, m_w_gate_x=m_w_gate_x, m_b_gate_x=m_b_gate_x, m_lru_param=m_lru_param, m_q_norm_g=m_q_norm_g, m_w_uq=m_w_uq, m_kv_norm_g=m_kv_norm_g, m_w_ukv=m_w_ukv, m_w_proj_rnn=m_w_proj_rnn, m_w_proj_mla=m_w_proj_mla, m_w_out=m_w_out, m_norm2_g=m_norm2_g, m_w_up=m_w_up, m_ffn_conv_w=m_ffn_conv_w, m_ffn_conv_b=m_ffn_conv_b, m_w_down=m_w_down, m_final_g=m_final_g, v_w_ada=v_w_ada, v_b_ada=v_b_ada, v_norm1_g=v_norm1_g, v_w_in=v_w_in, v_conv_w=v_conv_w, v_conv_b=v_conv_b, v_w_gate_a=v_w_gate_a, v_b_gate_a=v_b_gate_a, v_w_gate_x=v_w_gate_x, v_b_gate_x=v_b_gate_x, v_lru_param=v_lru_param, v_q_norm_g=v_q_norm_g, v_w_uq=v_w_uq, v_kv_norm_g=v_kv_norm_g, v_w_ukv=v_w_ukv, v_w_proj_rnn=v_w_proj_rnn, v_w_proj_mla=v_w_proj_mla, v_w_out=v_w_out, v_norm2_g=v_norm2_g, v_w_up=v_w_up, v_ffn_conv_w=v_ffn_conv_w, v_ffn_conv_b=v_ffn_conv_b, v_w_down=v_w_down, v_final_g=v_final_g)
    weights = {n: given[n] for n in TWIN_WEIGHTS}
    shared = {n: given[n] for n in SHARED_INPUTS}
    per_example = {n: given[n] for n in ['x', 'c', 'positions']}
    grad_fn = _jax.value_and_grad(_loss, argnums=(0, 1))

    def one_microbatch(ex, loss_target):
        ex = dict(ex)
        diff = ex.pop(TWIN_DIFF_INPUT)
        return grad_fn(weights, diff, {**shared, **ex}, loss_target)

    if N_MICROBATCH == 1:
        loss, (grad_w, grad_x) = one_microbatch(per_example, given["loss_target"])
    else:
        def body(carry, xs):
            loss_sum, grad_sum = carry
            l_k, (gw_k, gx_k) = one_microbatch(xs[0], xs[1])
            with _jax.named_scope("update"):
                return (loss_sum + l_k, _jax.tree.map(_jnp.add, grad_sum, gw_k)), gx_k

        init = (_jnp.zeros((), _jnp.float32), _jax.tree.map(_jnp.zeros_like, weights))
        (loss, grad_w), grad_x = _jax.lax.scan(body, init, (per_example, given["loss_target"]))
    with _jax.named_scope("update"):
        delta_w, new_m, new_v = {}, {}, {}
        for n in TWIN_WEIGHTS:
            delta_w[n], new_m[n], new_v[n] = _adamw(weights[n], grad_w[n], given["m_" + n], given["v_" + n])
    return (loss, grad_x, *[grad_w[n] for n in TWIN_WEIGHTS], *[delta_w[n] for n in TWIN_WEIGHTS],
            *[new_m[n] for n in TWIN_WEIGHTS], *[new_v[n] for n in TWIN_WEIGHTS])
```

```python
import functools
import math

import jax
import jax.numpy as jnp
from jax import lax
from jax.experimental import pallas as pl
from jax.experimental.pallas import tpu as pltpu

F32 = jnp.float32
BF16 = jnp.bfloat16

EPS = 1e-6
LRU_C = 8.0
N_HEADS = 16
QK_NOPE = 64
QK_ROPE = 32
HEAD_PAD = 128
ROPE_THETA = 10000.0
ADAM_LR = 0.001
ADAM_B1 = 0.9
ADAM_B2 = 0.999
ADAM_EPS = 1e-08
ADAM_WD = 0.01
ADAM_STEP = 10

LANE = 128
VMEM_LIMIT = 48 * 1024 * 1024
PACK_COLS = 1024
PACK_ROW_UNIT = 32
MESH = pl.DeviceIdType.MESH

NN = (((1,), (0,)), ((), ()))
NT = (((1,), (1,)), ((), ()))


def _cparams(sem):
    return pltpu.CompilerParams(dimension_semantics=sem, vmem_limit_bytes=VMEM_LIMIT)


def _div_tile(n, cap, unit):
    best = None
    d = unit
    while d <= min(n, cap):
        if n % d == 0:
            best = d
        d += unit
    return n if best is None else best


def _mm(name, a, b, *, ta=False, tb=False, add=None, out_dtype=F32):
    if ta:
        kdim, m = a.shape
    else:
        m, kdim = a.shape
    if tb:
        n, kb = b.shape
    else:
        kb, n = b.shape
    assert kdim == kb, (name, a.shape, b.shape)
    tm = _div_tile(m, 512, 8 if not ta else LANE)
    tn = _div_tile(n, 512, LANE)
    tk = _div_tile(kdim, 512, LANE)
    nk = kdim // tk
    a_spec = pl.BlockSpec((tk, tm), lambda i, j, k: (k, i)) if ta else pl.BlockSpec((tm, tk), lambda i, j, k: (i, k))
    b_spec = pl.BlockSpec((tn, tk), lambda i, j, k: (j, k)) if tb else pl.BlockSpec((tk, tn), lambda i, j, k: (k, j))
    o_spec = pl.BlockSpec((tm, tn), lambda i, j, k: (i, j))
    has_add = add is not None

    def body(*refs):
        if has_add:
            a_ref, b_ref, c_ref, o_ref, acc = refs
        else:
            a_ref, b_ref, o_ref, acc = refs
        k = pl.program_id(2)

        @pl.when(k == 0)
        def _():
            if has_add:
                acc[...] = c_ref[...].astype(F32)
            else:
                acc[...] = jnp.zeros(acc.shape, F32)

        av = a_ref[...]
        av = av.astype(F32).T.astype(BF16) if ta else av.astype(BF16)
        bv = b_ref[...].astype(BF16)
        acc[...] += lax.dot_general(av, bv, NT if tb else NN, preferred_element_type=F32)

        @pl.when(k == nk - 1)
        def _():
            o_ref[...] = acc[...].astype(o_ref.dtype)

    ins = [a, b] + ([add] if has_add else [])
    specs = [a_spec, b_spec] + ([o_spec] if has_add else [])
    return pl.pallas_call(
        body, name=name, grid=(m // tm, n // tn, nk), in_specs=specs, out_specs=o_spec,
        out_shape=jax.ShapeDtypeStruct((m, n), out_dtype),
        scratch_shapes=[pltpu.VMEM((tm, tn), F32)],
        compiler_params=_cparams(("parallel", "parallel", "arbitrary")),
    )(*ins)


_IMAPS = {
    "row": lambda i: (i, 0),
    "col": lambda i: (0, i),
    "full": lambda i: (0, 0),
    "acc": lambda i: (0, 0),
}


def _tiled(name, fn, n, ins, outs):
    ni = len(ins)
    is_acc = [k == "acc" for *_, k in outs]

    def body(*refs):
        vals = fn(*[r[...] for r in refs[:ni]])
        orefs = refs[ni:]
        if any(is_acc):
            @pl.when(pl.program_id(0) == 0)
            def _():
                for r, a in zip(orefs, is_acc):
                    if a:
                        r[...] = jnp.zeros(r.shape, r.dtype)
        for r, v, a in zip(orefs, vals, is_acc):
            if a:
                r[...] += v.astype(r.dtype)
            else:
                r[...] = v.astype(r.dtype)

    res = pl.pallas_call(
        body, name=name, grid=(n,),
        in_specs=[pl.BlockSpec(bs, _IMAPS[k]) for _, bs, k in ins],
        out_specs=[pl.BlockSpec(bs, _IMAPS[k]) for _, _, bs, k in outs],
        out_shape=[jax.ShapeDtypeStruct(s, d) for s, d, _, _ in outs],
        compiler_params=_cparams(("arbitrary",)),
    )(*[a for a, _, _ in ins])
    return tuple(res)


def _vjp_of(fn, nin, diff):
    def g(*args):
        ins, cots = args[:nin], args[nin:]

        def f(*d):
            full = list(ins)
            for i, v in zip(diff, d):
                full[i] = v
            return fn(*full)

        outs, vjp = jax.vjp(f, *[ins[i] for i in diff])
        return vjp(tuple(c.astype(o.dtype) for c, o in zip(cots, outs)))
    return g


def _shift_rows(x, k, fill, up=False):
    n = x.shape[0]
    rows = lax.broadcasted_iota(jnp.int32, x.shape, 0)
    if up:
        return jnp.where(rows < n - k, pltpu.roll(x, n - k, 0), fill)
    return jnp.where(rows >= k, pltpu.roll(x, k, 0), fill)


@functools.partial(jax.custom_vjp, nondiff_argnums=(1,))
def _delay(x, k):
    return _shift_rows(x, k, 0.0)


def _delay_fwd(x, k):
    return _shift_rows(x, k, 0.0), None


def _delay_bwd(k, _, g):
    return (_shift_rows(g, k, 0.0, up=True),)


_delay.defvjp(_delay_fwd, _delay_bwd)


@functools.partial(jax.custom_vjp, nondiff_argnums=(1,))
def _lane_roll(x, s):
    return pltpu.roll(x, s, 1)


def _lane_roll_fwd(x, s):
    return pltpu.roll(x, s, 1), None


def _lane_roll_bwd(s, _, g):
    return (pltpu.roll(g, g.shape[1] - s, 1),)


_lane_roll.defvjp(_lane_roll_fwd, _lane_roll_bwd)


@jax.custom_vjp
def _bdot(x, w):
    return lax.dot_general(x.astype(BF16), w.astype(BF16), NN, preferred_element_type=F32)


def _bdot_fwd(x, w):
    return _bdot(x, w), (x, w)


def _bdot_bwd(res, g):
    x, w = res
    gb = g.astype(BF16)
    dx = lax.dot_general(gb, w.astype(BF16), NT, preferred_element_type=F32)
    dw = lax.dot_general(x.T.astype(BF16), gb, NN, preferred_element_type=F32)
    return dx, dw


_bdot.defvjp(_bdot_fwd, _bdot_bwd)


def _sigmoid(x):
    return 0.5 * (jnp.tanh(0.5 * x) + 1.0)


def _silu(x):
    return x * _sigmoid(x)


def _rms(x, g):
    return x * lax.rsqrt(jnp.mean(x * x, axis=-1, keepdims=True) + EPS) * g


def _causal_conv(x, w, b):
    kw = w.shape[0]
    tap = lax.broadcasted_iota(jnp.int32, w.shape, 0)
    y = b
    for k in range(kw):
        d = kw - 1 - k
        wk = jnp.sum(jnp.where(tap == k, w, 0.0), axis=0, keepdims=True)
        y = y + wk * (x if d == 0 else _delay(x, d))
    return y


def _rotate(x, cos_f, sin_a, sin_b):
    reps = x.shape[1] // LANE
    if reps > 1:
        cos_f, sin_a, sin_b = (jnp.tile(t, (1, reps)) for t in (cos_f, sin_a, sin_b))
    n = x.shape[1]
    half = QK_ROPE // 2
    return x * cos_f + _lane_roll(x, n - half) * sin_a + _lane_roll(x, half) * sin_b


def _softplus_neg(l):
    u = jnp.exp(-jnp.abs(l))
    log1p_u = jnp.where(u < 0.01, u * (1.0 - u * (0.5 - u * (1.0 / 3.0))), jnp.log(1.0 + u))
    return jnp.maximum(-l, 0.0) + log1p_u


def _f_ln(x, g, scale, shift):
    return (_rms(x, g) * (1.0 + scale) + shift,)


def _f_qkv(qkv, cos_f, sin_a, sin_b, qg, kvg):
    nq, nkv = qg.shape[1], kvg.shape[1]
    qn = _rms(qkv[:, :nq], qg)
    kvn = _rms(qkv[:, nq:nq + nkv], kvg)
    kr = _rotate(qkv[:, nq + nkv:], cos_f, sin_a, sin_b)
    return qn, kvn, kr


def _f_qkv_bwd(qkv, cos_f, sin_a, sin_b, qg, kvg, dqn, dkvn, dkr):
    nq, nkv = qg.shape[1], kvg.shape[1]
    _, vjp_q = jax.vjp(_rms, qkv[:, :nq], qg)
    _, vjp_kv = jax.vjp(_rms, qkv[:, nq:nq + nkv], kvg)
    _, vjp_r = jax.vjp(lambda t: _rotate(t, cos_f, sin_a, sin_b), qkv[:, nq + nkv:])
    dq_lat, dqg = vjp_q(dqn)
    dkv_lat, dkvg = vjp_kv(dkvn)
    (dkr_pre,) = vjp_r(dkr)
    return jnp.concatenate([dq_lat, dkv_lat, dkr_pre], axis=1), dqg, dkvg


def _f_rotq(q, cos_f, sin_a, sin_b):
    return (_rotate(q, cos_f, sin_a, sin_b),)


def _merge(g_rnn, g_mla, p_rnn, p_mla):
    return _sigmoid(g_rnn) * p_rnn + _sigmoid(g_mla) * p_mla


def _f_merge(g, p_rnn, p_mla):
    d = p_rnn.shape[1]
    return (_merge(g[:, :d], g[:, d:], p_rnn, p_mla),)


def _f_merge_bwd(g, p_rnn, p_mla, dm):
    d = p_rnn.shape[1]
    _, vjp = jax.vjp(_merge, g[:, :d], g[:, d:], p_rnn, p_mla)
    dg_rnn, dg_mla, dp_rnn, dp_mla = vjp(dm)
    return jnp.concatenate([dg_rnn, dg_mla], axis=1), dp_rnn, dp_mla


def _f_res_ln(x, o, gate, g2, scale, shift):
    x1 = x + gate * o
    return x1, _rms(x1, g2) * (1.0 + scale) + shift


def _ffn(u_gate, u_val, cw_gate, cw_val, cb_gate, cb_val):
    return _silu(_causal_conv(u_gate, cw_gate, cb_gate)) * _causal_conv(u_val, cw_val, cb_val)


def _halves(*arrays):
    out = []
    for a in arrays:
        half = a.shape[1] // 2
        out += [a[:, :half], a[:, half:]]
    return out


def _f_ffn(u, cw, cb):
    ug, uv, cwg, cwv, cbg, cbv = _halves(u, cw, cb)
    return (_ffn(ug, uv, cwg, cwv, cbg, cbv),)


def _f_ffn_bwd(u, cw, cb, dact):
    _, vjp = jax.vjp(_ffn, *_halves(u, cw, cb))
    dug, duv, dcwg, dcwv, dcbg, dcbv = vjp(dact)
    cat = lambda a, b: jnp.concatenate([a, b], axis=1)
    return cat(dug, duv), cat(dcwg, dcwv), cat(dcbg, dcbv)


def _f_loss(x1, f, tgt, gate, fg):
    y = _rms(x1 + gate * f, fg)
    err = (y - tgt) * (y - tgt)
    return 0.5 * jnp.sum(jnp.mean(err, axis=-1, keepdims=True), axis=0, keepdims=True)


def _f_loss_and_grads(x1, f, tgt, gate, fg):
    loss, vjp = jax.vjp(lambda a, b, c, d: _f_loss(a, b, tgt, c, d), x1, f, gate, fg)
    dx1, df, dgate, dfg = vjp(jnp.ones((1, 1), F32))
    return dx1, df, jnp.broadcast_to(loss, (1, LANE)), dgate, dfg


def _f_lru_coeffs(xr, cw, cb, wa, ba, wx, bx, lru, reset):
    xc = _causal_conv(xr, cw, cb)
    r = _sigmoid(_bdot(xc, wa) + ba)
    i = _sigmoid(_bdot(xc, wx) + bx)
    log_a = (-LRU_C) * r * _softplus_neg(lru)
    a = jnp.exp(log_a)
    mult = jnp.sqrt(-jnp.tanh(log_a) * (1.0 + a * a))
    is_reset = reset > 0.5
    a = jnp.where(is_reset, 0.0, a)
    mult = jnp.where(is_reset, 1.0, mult)
    return a, mult * (i * xc)


def _scan(a, b, up=False):
    n = a.shape[0]
    k = 1
    while k < n:
        b = b + a * _shift_rows(b, k, 0.0, up)
        if 2 * k < n:
            a = a * _shift_rows(a, k, 1.0, up)
        k *= 2
    return b


def _f_lru_fwd(xr, cw, cb, wa, ba, wx, bx, lru, reset):
    a, b = _f_lru_coeffs(xr, cw, cb, wa, ba, wx, bx, lru, reset)
    h = _scan(a, b)
    return h, h


def _f_lru_bwd(xr, cw, cb, wa, ba, wx, bx, lru, reset, h, dh):
    (a, _), vjp = jax.vjp(lambda *p: _f_lru_coeffs(*p, reset), xr, cw, cb, wa, ba, wx, bx, lru)
    g = _scan(_shift_rows(a, 1, 0.0, up=True), dh, up=True)
    return vjp((g * _shift_rows(h, 1, 0.0), g))


def _attn_tile(s):
    return 512 if s >= 1024 else s // 2


def _keys(kv, kr):
    lane = lax.broadcasted_iota(jnp.int32, kv.shape, 1)
    return jnp.where(lane < QK_NOPE, kv, kr)


def _masked_scores(q, kc, qi, ki, t, scale):
    s = lax.dot_general(q, kc, NT, preferred_element_type=F32) * scale
    rows = qi * t + lax.broadcasted_iota(jnp.int32, s.shape, 0)
    cols = ki * t + lax.broadcasted_iota(jnp.int32, s.shape, 1)
    return jnp.where(cols <= rows, s, -jnp.inf)


def _attn_fwd(q, kv, kr):
    s_len = q.shape[0]
    t = _attn_tile(s_len)
    nb = s_len // t
    scale = 1.0 / math.sqrt(QK_NOPE + QK_ROPE)

    def body(q_ref, kv_ref, kr_ref, o_ref, lse_ref, m_s, l_s, acc_s):
        qi, ki = pl.program_id(1), pl.program_id(2)

        @pl.when(ki == 0)
        def _():
            m_s[...] = jnp.full(m_s.shape, -jnp.inf, F32)
            l_s[...] = jnp.zeros(l_s.shape, F32)
            acc_s[...] = jnp.zeros(acc_s.shape, F32)

        @pl.when(ki <= qi)
        def _():
            kvv = kv_ref[...]
            s = _masked_scores(q_ref[...], _keys(kvv, kr_ref[...]), qi, ki, t, scale)
            m_old = m_s[...]
            m_new = jnp.maximum(m_old, jnp.max(s, axis=-1, keepdims=True))
            alpha = jnp.exp(m_old - m_new)
            p = jnp.exp(s - m_new)
            l_s[...] = alpha * l_s[...] + jnp.sum(p, axis=-1, keepdims=True)
            acc_s[...] = alpha * acc_s[...] + lax.dot_general(p.astype(BF16), kvv, NN, preferred_element_type=F32)
            m_s[...] = m_new

        @pl.when(ki == qi)
        def _():
            lane = lax.broadcasted_iota(jnp.int32, acc_s.shape, 1)
            o_ref[...] = jnp.where(lane >= QK_NOPE, acc_s[...] / l_s[...], 0.0).astype(o_ref.dtype)
            lse_ref[0] = m_s[...] + jnp.log(l_s[...])

    return pl.pallas_call(
        body, name="attn_fwd", grid=(N_HEADS, nb, nb),
        in_specs=[pl.BlockSpec((t, HEAD_PAD), lambda h, i, j: (i, h)),
                  pl.BlockSpec((t, HEAD_PAD), lambda h, i, j: (jnp.minimum(i, j), h)),
                  pl.BlockSpec((t, HEAD_PAD), lambda h, i, j: (jnp.minimum(i, j), 0))],
        out_specs=[pl.BlockSpec((t, HEAD_PAD), lambda h, i, j: (i, h)),
                   pl.BlockSpec((1, t, 1), lambda h, i, j: (h, i, 0))],
        out_shape=[jax.ShapeDtypeStruct((s_len, N_HEADS * HEAD_PAD), BF16),
                   jax.ShapeDtypeStruct((N_HEADS, s_len, 1), F32)],
        scratch_shapes=[pltpu.VMEM((t, 1), F32), pltpu.VMEM((t, 1), F32), pltpu.VMEM((t, HEAD_PAD), F32)],
        compiler_params=_cparams(("arbitrary", "arbitrary", "arbitrary")),
    )(q, kv, kr)


def _attn_bwd(q, kv, kr, o, lse, do):
    s_len = q.shape[0]
    t = _attn_tile(s_len)
    nb = s_len // t
    scale = 1.0 / math.sqrt(QK_NOPE + QK_ROPE)

    def body(q_ref, kv_ref, kr_ref, o_ref, lse_ref, do_ref, dq_ref, dkv_ref, dkr_ref, dk_s, dv_s):
        h, kb, qb = pl.program_id(0), pl.program_id(1), pl.program_id(2)

        @pl.when(jnp.logical_and(h == 0, jnp.logical_and(kb == 0, qb == 0)))
        def _():
            dkr_ref[...] = jnp.zeros(dkr_ref.shape, F32)

        @pl.when(jnp.logical_and(kb == 0, qb == 0))
        def _():
            dq_ref[...] = jnp.zeros(dq_ref.shape, F32)

        @pl.when(qb == kb)
        def _():
            dk_s[...] = jnp.zeros(dk_s.shape, F32)
            dv_s[...] = jnp.zeros(dv_s.shape, F32)

        @pl.when(qb >= kb)
        def _():
            qv, kvv, dov = q_ref[...], kv_ref[...], do_ref[...]
            kc = _keys(kvv, kr_ref[...])
            s = _masked_scores(qv, kc, qb, kb, t, scale)
            p = jnp.exp(s - lse_ref[0])
            delta = jnp.sum(dov.astype(F32) * o_ref[...].astype(F32), axis=-1, keepdims=True)
            dp = lax.dot_general(dov, kvv, NT, preferred_element_type=F32)
            ds = (p * (dp - delta) * scale).astype(BF16)
            dv_s[...] += lax.dot_general(p.T.astype(BF16), dov, NN, preferred_element_type=F32)
            dk_s[...] += lax.dot_general(ds.astype(F32).T.astype(BF16), qv, NN, preferred_element_type=F32)
            rows = pl.ds(pl.multiple_of(qb * t, t), t)
            dq_ref[rows, :] += lax.dot_general(ds, kc, NN, preferred_element_type=F32)

        @pl.when(qb == nb - 1)
        def _():
            lane = lax.broadcasted_iota(jnp.int32, dk_s.shape, 1)
            dkv_ref[...] = jnp.where(lane < QK_NOPE, dk_s[...], dv_s[...])
            rows = pl.ds(pl.multiple_of(kb * t, t), t)
            dkr_ref[rows, :] += jnp.where(lane >= QK_NOPE, dk_s[...], 0.0)

    wide = N_HEADS * HEAD_PAD
    qmap = lambda h, k, i: (jnp.maximum(i, k), h)
    return pl.pallas_call(
        body, name="attn_bwd", grid=(N_HEADS, nb, nb),
        in_specs=[pl.BlockSpec((t, HEAD_PAD), qmap),
                  pl.BlockSpec((t, HEAD_PAD), lambda h, k, i: (k, h)),
                  pl.BlockSpec((t, HEAD_PAD), lambda h, k, i: (k, 0)),
                  pl.BlockSpec((t, HEAD_PAD), qmap),
                  pl.BlockSpec((1, t, 1), lambda h, k, i: (h, jnp.maximum(i, k), 0)),
                  pl.BlockSpec((t, HEAD_PAD), qmap)],
        out_specs=[pl.BlockSpec((s_len, HEAD_PAD), lambda h, k, i: (0, h)),
                   pl.BlockSpec((t, HEAD_PAD), lambda h, k, i: (k, h)),
                   pl.BlockSpec((s_len, HEAD_PAD), lambda h, k, i: (0, 0))],
        out_shape=[jax.ShapeDtypeStruct((s_len, wide), F32),
                   jax.ShapeDtypeStruct((s_len, wide), F32),
                   jax.ShapeDtypeStruct((s_len, HEAD_PAD), F32)],
        scratch_shapes=[pltpu.VMEM((t, HEAD_PAD), F32), pltpu.VMEM((t, HEAD_PAD), F32)],
        compiler_params=_cparams(("arbitrary", "arbitrary", "arbitrary")),
    )(q, kv, kr, o, lse, do)


def _adamw(name, w, g, m, v):
    rows, cols = w.shape
    tr = _div_tile(rows, max(8, (2 * 1024 * 1024) // (4 * cols)), 8)

    def body(w_ref, g_ref, m_ref, v_ref, d_ref, nm_ref, nv_ref):
        gv = g_ref[...]
        nm = ADAM_B1 * m_ref[...] + (1.0 - ADAM_B1) * gv
        nv = ADAM_B2 * v_ref[...] + (1.0 - ADAM_B2) * jnp.square(gv)
        m_hat = nm / (1.0 - ADAM_B1 ** ADAM_STEP)
        v_hat = nv / (1.0 - ADAM_B2 ** ADAM_STEP)
        d_ref[...] = -ADAM_LR * (m_hat / (jnp.sqrt(v_hat) + ADAM_EPS) + ADAM_WD * w_ref[...])
        nm_ref[...] = nm
        nv_ref[...] = nv

    spec = pl.BlockSpec((tr, cols), lambda i: (i, 0))
    return pl.pallas_call(
        body, name=name, grid=(rows // tr,), in_specs=[spec] * 4, out_specs=[spec] * 3,
        out_shape=[jax.ShapeDtypeStruct((rows, cols), F32)] * 3,
        compiler_params=_cparams(("parallel",)),
    )(w, g, m, v)


def _exchange(name, src, masks, *, distinct, local=True):
    bits = 0
    for m in masks:
        bits |= m
    nslots = {7: 8, 6: 4, 1: 2}[bits]
    nm = len(masks)
    block = src.shape[1:] if distinct else src.shape

    def slot_of(x, y, c):
        return {7: 4 * x + 2 * y + c, 6: 2 * x + y, 1: c}[bits]

    def body(src_ref, out_ref, send_sems, recv_sems, local_sem):
        x, y, c = lax.axis_index("x"), lax.axis_index("y"), lax.axis_index("c")
        mine = slot_of(x, y, c)
        if local:
            own = pltpu.make_async_copy(src_ref.at[mine] if distinct else src_ref, out_ref.at[mine], local_sem)
            own.start()
        copies = []
        for i, m in enumerate(masks):
            px = 1 - x if m & 4 else x
            py = 1 - y if m & 2 else y
            pc = 1 - c if m & 1 else c
            theirs = slot_of(px, py, pc)
            copies.append((
                pltpu.make_async_remote_copy(
                    src_ref=src_ref.at[theirs] if distinct else src_ref, dst_ref=out_ref.at[mine],
                    send_sem=send_sems.at[i], recv_sem=recv_sems.at[i],
                    device_id=(px, py, pc), device_id_type=MESH),
                pltpu.make_async_remote_copy(
                    src_ref=src_ref.at[theirs] if distinct else src_ref, dst_ref=out_ref.at[theirs],
                    send_sem=send_sems.at[i], recv_sem=recv_sems.at[i],
                    device_id=(px, py, pc), device_id_type=MESH)))
        for send, _ in copies:
            send.start()
        for _, arrival in copies:
            arrival.wait_recv()
        for send, _ in copies:
            send.wait_send()
        if local:
            own.wait()

    return pl.pallas_call(
        body, name=name,
        in_specs=[pl.BlockSpec(memory_space=pl.ANY)], out_specs=pl.BlockSpec(memory_space=pl.ANY),
        out_shape=jax.ShapeDtypeStruct((nslots,) + tuple(block), src.dtype),
        scratch_shapes=[pltpu.SemaphoreType.DMA((nm,)), pltpu.SemaphoreType.DMA((nm,)), pltpu.SemaphoreType.DMA],
    )(src)


ALL7 = (1, 2, 3, 4, 5, 6, 7)
CHIPS = (2, 4, 6)
SIBLING = (1,)


def _unshard(seg, kind):
    n, r, c = seg.shape
    if kind == "col":
        return seg.transpose(1, 0, 2).reshape(r, n * c)
    return seg.reshape(n * r, c)


def _reshard(full, kind, n=4):
    r, c = full.shape
    if kind == "col":
        return full.reshape(r, n, c // n).transpose(1, 0, 2).reshape(n, -1)
    return full.reshape(n, -1)


def _pad_rows(flat, rows):
    n, ln = flat.shape
    return jnp.pad(flat, ((0, 0), (0, rows * PACK_COLS - ln))).reshape(n, rows, PACK_COLS)


def _block_diag_pairs(w):
    n2, bs, _ = w.shape
    eye = jnp.eye(2, dtype=w.dtype)
    z = w.reshape(n2 // 2, 2, bs, 1, bs) * eye[None, :, None, :, None]
    return z.reshape(n2 // 2, 2 * bs, 2 * bs).transpose(1, 0, 2).reshape(2 * bs, n2 * bs)


def _block_diag_pairs_t(d, bs=64):
    n = d.shape[1] // (2 * bs)
    z = d.reshape(2 * bs, n, 2 * bs).transpose(1, 0, 2).reshape(n, 2, bs, 2, bs)
    return jnp.stack([z[:, 0, :, 0, :], z[:, 1, :, 1, :]], axis=1).reshape(2 * n, bs, bs)


def _interleave(w, half_tile=LANE):
    r, c = w.shape
    return w.reshape(r, 2, c // (2 * half_tile), half_tile).transpose(0, 2, 1, 3).reshape(r, c)


def _deinterleave(w, half_tile=LANE):
    r, c = w.shape
    return w.reshape(r, c // (2 * half_tile), 2, half_tile).transpose(0, 2, 1, 3).reshape(r, c)


BIG = (("w_in", "col"), ("w_uq", "col"), ("w_ukv", "col"), ("w_proj_rnn", "row"), ("w_proj_mla", "row"),
       ("w_out", "row"), ("w_up", "col"), ("w_down", "row"))
CONVS = (("conv_w", "col"), ("ffn_conv_w", "col"))
SMALL = ("b_ada", "norm1_g", "conv_b", "w_gate_a", "b_gate_a", "w_gate_x", "b_gate_x", "lru_param",
         "q_norm_g", "kv_norm_g", "norm2_g", "ffn_conv_b", "final_g")
WEIGHTS = ("w_ada", "b_ada", "norm1_g", "w_in", "conv_w", "conv_b", "w_gate_a", "b_gate_a", "w_gate_x",
           "b_gate_x", "lru_param", "q_norm_g", "w_uq", "kv_norm_g", "w_ukv", "w_proj_rnn", "w_proj_mla",
           "w_out", "norm2_g", "w_up", "ffn_conv_w", "ffn_conv_b", "w_down", "final_g")


def _step(x, c, positions, w, m_in, v_in, loss_target):
    s_len, d = x.shape[1], x.shape[2]
    x2d = x[0]
    tgt = loss_target[0]
    xi, yi, ci = lax.axis_index("x"), lax.axis_index("y"), lax.axis_index("c")
    chip = 2 * xi + yi
    me = 2 * chip + ci
    tile = min(256, s_len)
    nt = s_len // tile

    local2d = {k: w[k][0] for k, _ in BIG + CONVS}
    flat = jnp.concatenate([local2d[k].reshape(-1) for k, _ in BIG])
    n_big = flat.shape[0]
    n_grad = n_big + sum(local2d[k].size for k, _ in CONVS)
    half_rows = -(-n_grad // (2 * PACK_COLS * PACK_ROW_UNIT)) * PACK_ROW_UNIT
    packed = _pad_rows(flat.astype(BF16)[None], 2 * half_rows).reshape(2, half_rows, PACK_COLS)
    mine = lax.dynamic_index_in_dim(packed, ci, 0, keepdims=False)
    g1 = _exchange("gather_w_chips", mine, CHIPS, distinct=False)
    g2 = _exchange("gather_w_sibling", g1, SIBLING, distinct=False)
    shards = g2.transpose(1, 0, 2, 3).reshape(4, -1)
    full = {}
    off = 0
    for k, kind in BIG:
        r, cc = local2d[k].shape
        full[k] = _unshard(shards[:, off:off + r * cc].reshape(4, r, cc), kind)
        off += r * cc
    conv_flat = jnp.concatenate([local2d[k].reshape(-1) for k, _ in CONVS])
    conv_rows = -(-conv_flat.shape[0] // PACK_COLS)
    conv_all = _exchange("gather_conv_w", _pad_rows(conv_flat[None], conv_rows)[0], CHIPS, distinct=False)
    conv_all = conv_all.reshape(4, -1)
    off = 0
    for k, kind in CONVS:
        r, cc = local2d[k].shape
        full[k] = _unshard(conv_all[:, off:off + r * cc].reshape(4, r, cc), kind)
        off += r * cc

    d_rnn = w["conv_b"].shape[1]
    n_q, n_kv = w["q_norm_g"].shape[1], w["kv_norm_g"].shape[1]
    w_in = full["w_in"]
    o1, o2, o3 = d_rnn + n_q, d_rnn + n_q + n_kv, d_rnn + n_q + n_kv + QK_ROPE
    w_rnn = w_in[:, :d_rnn]
    zpad = lambda n: jnp.zeros((d, n), BF16)
    w_qkv = jnp.concatenate([w_in[:, d_rnn:o2], zpad(QK_NOPE), w_in[:, o2:o3], zpad(LANE - QK_NOPE - QK_ROPE)], axis=1)
    w_g = w_in[:, o3:]
    hd = QK_NOPE + QK_ROPE
    w_uq = jnp.pad(full["w_uq"].reshape(n_q, N_HEADS, hd), ((0, 0), (0, 0), (0, HEAD_PAD - hd))).reshape(n_q, -1)
    w_ukv = full["w_ukv"]
    w_pr = full["w_proj_rnn"]
    v_head = w_ukv.shape[1] // N_HEADS - QK_NOPE
    w_pm = jnp.pad(full["w_proj_mla"].reshape(N_HEADS, v_head, d), ((0, 0), (HEAD_PAD - v_head, 0), (0, 0))).reshape(-1, d)
    w_out = full["w_out"]
    w_up = _interleave(full["w_up"])
    w_down = full["w_down"]
    ffn_cw = _interleave(full["ffn_conv_w"])
    ffn_cb = _interleave(w["ffn_conv_b"])
    conv_w, conv_b = full["conv_w"], w["conv_b"]
    wa_bd = _block_diag_pairs(w["w_gate_a"][0])
    wx_bd = _block_diag_pairs(w["w_gate_x"][0])

    c_all = _exchange("gather_c", c, ALL7, distinct=False).reshape(8, d)
    c_rows = 128
    (c_act,) = _tiled("silu_c", lambda v: (_silu(v),), 1, [(jnp.pad(c_all, ((0, c_rows - 8), (0, 0))), (c_rows, d), "full")],
                      [((c_rows, d), F32, (c_rows, d), "full")])
    w_ada = w["w_ada"][0]
    n_mod = w_ada.shape[1]
    b_loc = lax.dynamic_slice_in_dim(w["b_ada"], chip * n_mod, n_mod, axis=1)
    mod_loc = _mm("ada_fwd", c_act, w_ada, add=jnp.broadcast_to(b_loc, (c_rows, n_mod)))
    mod_all = _exchange("gather_mod", mod_loc[:8], CHIPS, distinct=False)
    mod = lax.dynamic_index_in_dim(mod_all, me, 1, keepdims=False).reshape(1, -1)
    shift1, scale1, gate1, shift2, scale2, gate2 = [mod[:, i * d:(i + 1) * d] for i in range(6)]

    half = QK_ROPE // 2
    inv_freq = ROPE_THETA ** (-jnp.arange(half, dtype=F32) / half)
    ang = positions[0].astype(F32)[:, None] * inv_freq
    cos, sin = jnp.cos(ang), jnp.sin(ang)
    one, zero = jnp.ones((s_len, QK_NOPE), F32), jnp.zeros((s_len, half), F32)
    tail = jnp.zeros((s_len, LANE - QK_NOPE - QK_ROPE), F32)
    cos_f = jnp.concatenate([one, cos, cos, tail + 1.0], axis=1)
    sin_a = jnp.concatenate([one * 0.0, -sin, zero, tail], axis=1)
    sin_b = jnp.concatenate([one * 0.0, zero, sin, tail], axis=1)
    reset = (positions[0] == 0).astype(F32)[:, None]
    tabs = [(cos_f, (tile, LANE), "row"), (sin_a, (tile, LANE), "row"), (sin_b, (tile, LANE), "row")]

    def rowspec(a):
        return (a, (tile, a.shape[1]), "row")

    def full2(a):
        return (a, a.shape, "full")

    def rowout(cols, dt):
        return ((s_len, cols), dt, (tile, cols), "row")

    def accout(a):
        return (a.shape, F32, a.shape, "acc")

    norm1_g, norm2_g, final_g = w["norm1_g"], w["norm2_g"], w["final_g"].reshape(1, d)
    ln1_in = [rowspec(x2d), full2(norm1_g), full2(scale1), full2(shift1)]
    (h1,) = _tiled("ln1", _f_ln, nt, ln1_in, [rowout(d, BF16)])
    x_rnn = _mm("in_rnn", h1, w_rnn)
    qkv = _mm("in_qkv", h1, w_qkv)
    gates = _mm("in_gates", h1, w_g)

    ct = LANE
    n_ct = d_rnn // ct
    colspec = lambda a, width=ct: (a, (a.shape[0], width), "col")
    lru_in = [colspec(x_rnn), colspec(conv_w), colspec(conv_b), colspec(wa_bd), colspec(w["b_gate_a"]),
              colspec(wx_bd), colspec(w["b_gate_x"]), colspec(w["lru_param"]), full2(reset)]
    y_rnn, h_rnn = _tiled("lru_fwd", _f_lru_fwd, n_ct, lru_in,
                          [((s_len, d_rnn), BF16, (s_len, ct), "col"), ((s_len, d_rnn), F32, (s_len, ct), "col")])

    qkv_in = [rowspec(qkv)] + tabs + [full2(w["q_norm_g"]), full2(w["kv_norm_g"])]
    qn, kvn, kr = _tiled("qkv_norm", _f_qkv, nt, qkv_in, [rowout(n_q, BF16), rowout(n_kv, BF16), rowout(LANE, BF16)])
    q_pre = _mm("up_q", qn, w_uq)
    kv = _mm("up_kv", kvn, w_ukv, out_dtype=BF16)
    (q_cat,) = _tiled("rot_q", _f_rotq, nt, [rowspec(q_pre)] + tabs, [rowout(q_pre.shape[1], BF16)])
    o_mla, lse = _attn_fwd(q_cat, kv, kr)

    p_rnn = _mm("proj_rnn", y_rnn, w_pr)
    p_mla = _mm("proj_mla", o_mla, w_pm)
    merge_in = [rowspec(gates), rowspec(p_rnn), rowspec(p_mla)]
    (merged,) = _tiled("merge", _f_merge, nt, merge_in, [rowout(d, BF16)])
    o_tok = _mm("out_proj", merged, w_out)
    res_in = [rowspec(x2d), rowspec(o_tok), full2(gate1), full2(norm2_g), full2(scale2), full2(shift2)]
    x1, h2 = _tiled("res_ln2", _f_res_ln, nt, res_in, [rowout(d, F32), rowout(d, BF16)])
    u_pre = _mm("ffn_up", h2, w_up)
    d_ff2 = u_pre.shape[1]
    n_ft = d_ff2 // (2 * LANE)
    ffn_in = [colspec(u_pre, 2 * LANE), colspec(ffn_cw, 2 * LANE), colspec(ffn_cb, 2 * LANE)]
    (act,) = _tiled("ffn_conv", _f_ffn, n_ft, ffn_in, [((s_len, d_ff2 // 2), BF16, (s_len, LANE), "col")])
    f_tok = _mm("ffn_down", act, w_down)

    loss_in = [rowspec(x1), rowspec(f_tok), rowspec(tgt), full2(gate2), full2(final_g)]
    dx1, df, loss_row, d_gate2, d_final_g = _tiled(
        "loss", _f_loss_and_grads, nt, loss_in,
        [rowout(d, F32), rowout(d, BF16), ((1, LANE), F32, (1, LANE), "acc"), accout(gate2), accout(final_g)])
    loss = lax.psum(loss_row[0, 0], ("x", "y", "c"))

    d_act = _mm("ffn_down_dx", df, w_down, tb=True)
    g_w_down = _mm("ffn_down_dw", act, df, ta=True)
    du, g_ffn_cw, g_ffn_cb = _tiled(
        "ffn_conv_bwd", _f_ffn_bwd, n_ft, ffn_in + [colspec(d_act)],
        [((s_len, d_ff2), BF16, (s_len, 2 * LANE), "col"), (ffn_cw.shape, F32, (ffn_cw.shape[0], 2 * LANE), "col"),
         (ffn_cb.shape, F32, (1, 2 * LANE), "col")])
    dh2 = _mm("ffn_up_dx", du, w_up, tb=True)
    g_w_up = _mm("ffn_up_dw", h2, du, ta=True)

    res_bwd = _vjp_of(_f_res_ln, 6, (0, 1, 2, 3, 4, 5))
    dx_res, do_tok, d_gate1, g_norm2, d_scale2, d_shift2 = _tiled(
        "res_ln2_bwd", res_bwd, nt, res_in + [rowspec(dx1), rowspec(dh2)],
        [rowout(d, F32), rowout(d, BF16), accout(gate1), accout(norm2_g), accout(scale2), accout(shift2)])
    d_merged = _mm("out_proj_dx", do_tok, w_out, tb=True)
    g_w_out = _mm("out_proj_dw", merged, do_tok, ta=True)
    d_gates, dp_rnn, dp_mla = _tiled(
        "merge_bwd", _f_merge_bwd, nt, merge_in + [rowspec(d_merged)],
        [rowout(gates.shape[1], BF16), rowout(d, BF16), rowout(d, BF16)])
    dy_rnn = _mm("proj_rnn_dx", dp_rnn, w_pr, tb=True)
    g_w_pr = _mm("proj_rnn_dw", y_rnn, dp_rnn, ta=True)
    do_mla = _mm("proj_mla_dx", dp_mla, w_pm, tb=True, out_dtype=BF16)
    g_w_pm = _mm("proj_mla_dw", o_mla, dp_mla, ta=True)

    dq_cat, dkv, dkr = _attn_bwd(q_cat, kv, kr, o_mla, lse, do_mla)
    rot_bwd = _vjp_of(_f_rotq, 4, (0,))
    (dq_pre,) = _tiled("rot_q_bwd", rot_bwd, nt, [rowspec(q_pre)] + tabs + [rowspec(dq_cat)],
                       [rowout(q_pre.shape[1], BF16)])
    dqn = _mm("up_q_dx", dq_pre, w_uq, tb=True)
    g_w_uq = _mm("up_q_dw", qn, dq_pre, ta=True)
    dkv_b = dkv.astype(BF16)
    dkvn = _mm("up_kv_dx", dkv_b, w_ukv, tb=True)
    g_w_ukv = _mm("up_kv_dw", kvn, dkv_b, ta=True)
    dqkv, g_q_norm, g_kv_norm = _tiled(
        "qkv_norm_bwd", _f_qkv_bwd, nt, qkv_in + [rowspec(dqn), rowspec(dkvn), rowspec(dkr)],
        [rowout(qkv.shape[1], BF16), accout(w["q_norm_g"]), accout(w["kv_norm_g"])])

    lru_out = [((s_len, d_rnn), BF16, (s_len, ct), "col")]
    for a in (conv_w, conv_b, wa_bd, w["b_gate_a"], wx_bd, w["b_gate_x"], w["lru_param"]):
        lru_out.append((a.shape, F32, (a.shape[0], ct), "col"))
    dx_rnn, g_conv_w, g_conv_b, g_wa_bd, g_b_a, g_wx_bd, g_b_x, g_lru = _tiled(
        "lru_bwd", _f_lru_bwd, n_ct, lru_in + [colspec(h_rnn), colspec(dy_rnn)], lru_out)

    dh1 = _mm("in_gates_dx", d_gates, w_g, tb=True)
    dh1 = _mm("in_qkv_dx", dqkv, w_qkv, tb=True, add=dh1)
    dh1 = _mm("in_rnn_dx", dx_rnn, w_rnn, tb=True, add=dh1)
    g_w_rnn = _mm("in_rnn_dw", h1, dx_rnn, ta=True)
    g_w_qkv = _mm("in_qkv_dw", h1, dqkv, ta=True)
    g_w_g = _mm("in_gates_dw", h1, d_gates, ta=True)

    ln_bwd = _vjp_of(_f_ln, 4, (0, 1, 2, 3))

    def ln1_bwd(xv, gv, sc, sh, dxr, dh):
        dx, dg, dsc, dsh = ln_bwd(xv, gv, sc, sh, dh)
        return dx + dxr, dg, dsc, dsh

    grad_x, g_norm1, d_scale1, d_shift1 = _tiled(
        "ln1_bwd", ln1_bwd, nt, ln1_in + [rowspec(dx_res), rowspec(dh1)],
        [rowout(d, F32), accout(norm1_g), accout(scale1), accout(shift1)])

    dmod = jnp.concatenate([d_shift1, d_scale1, d_gate1, d_shift2, d_scale2, d_gate2], axis=1)
    dmod_all = _exchange("gather_dmod", dmod, ALL7, distinct=False).reshape(8, -1)
    dmod_loc = lax.dynamic_slice_in_dim(dmod_all, chip * n_mod, n_mod, axis=1)
    g_w_ada = _mm("ada_dw", c_act, jnp.pad(dmod_loc, ((0, c_rows - 8), (0, 0))), ta=True)

    g_full = {
        "w_in": jnp.concatenate([g_w_rnn, g_w_qkv[:, :n_q + n_kv],
                                 g_w_qkv[:, n_q + n_kv + QK_NOPE:n_q + n_kv + QK_NOPE + QK_ROPE], g_w_g], axis=1),
        "w_uq": g_w_uq.reshape(n_q, N_HEADS, HEAD_PAD)[:, :, :hd].reshape(n_q, -1),
        "w_ukv": g_w_ukv,
        "w_proj_rnn": g_w_pr,
        "w_proj_mla": g_w_pm.reshape(N_HEADS, HEAD_PAD, d)[:, HEAD_PAD - v_head:, :].reshape(-1, d),
        "w_out": g_w_out,
        "w_up": _deinterleave(g_w_up),
        "w_down": g_w_down,
        "conv_w": g_conv_w,
        "ffn_conv_w": _deinterleave(g_ffn_cw),
    }
    g_small = {
        "b_ada": dmod, "norm1_g": g_norm1, "conv_b": g_conv_b,
        "w_gate_a": _block_diag_pairs_t(g_wa_bd)[None], "b_gate_a": g_b_a,
        "w_gate_x": _block_diag_pairs_t(g_wx_bd)[None], "b_gate_x": g_b_x, "lru_param": g_lru,
        "q_norm_g": g_q_norm, "kv_norm_g": g_kv_norm, "norm2_g": g_norm2,
        "ffn_conv_b": _deinterleave(g_ffn_cb), "final_g": d_final_g.reshape(w["final_g"].shape),
    }

    big = jnp.concatenate([_reshard(g_full[k], kind) for k, kind in BIG + CONVS], axis=1)
    big = _pad_rows(big, 2 * half_rows).reshape(4, 2, half_rows, PACK_COLS)
    small_flat = jnp.concatenate([g_small[k].reshape(-1) for k in SMALL])
    n_small = small_flat.shape[0]
    small_rows = -(-n_small // (8 * PACK_COLS * PACK_ROW_UNIT)) * PACK_ROW_UNIT
    small = _pad_rows(small_flat[None], 8 * small_rows).reshape(4, 2, small_rows, PACK_COLS)
    chunks = jnp.concatenate([big, small], axis=2)
    rows = half_rows + small_rows
    theirs = lax.dynamic_index_in_dim(chunks, 1 - ci, 1, keepdims=False)
    ours = lax.dynamic_index_in_dim(chunks, ci, 1, keepdims=False)
    pair = _exchange("reduce_sibling", theirs, SIBLING, distinct=False, local=False)
    from_sib = lax.dynamic_index_in_dim(pair, 1 - ci, 0, keepdims=False)
    rt = _div_tile(4 * rows, 256, 16)
    flat2 = lambda a: a.reshape(-1, PACK_COLS)
    (chip_sum,) = _tiled("reduce_pair", lambda a, b: (a + b,), 4 * rows // rt,
                         [(flat2(ours), (rt, PACK_COLS), "row"), (flat2(from_sib), (rt, PACK_COLS), "row")],
                         [((4 * rows, PACK_COLS), BF16, (rt, PACK_COLS), "row")])
    quad = _exchange("reduce_chips", chip_sum.reshape(4, rows, PACK_COLS), CHIPS, distinct=True)
    rt2 = _div_tile(rows, 256, 16)

    def sum4(a):
        return (((a[0].astype(F32) + a[1].astype(F32)) + a[2].astype(F32)) + a[3].astype(F32),)

    def body_sum4(q_ref, o_ref):
        o_ref[...] = sum4(q_ref[...])[0]

    reduced = pl.pallas_call(
        body_sum4, name="reduce_quad", grid=(rows // rt2,),
        in_specs=[pl.BlockSpec((4, rt2, PACK_COLS), lambda i: (0, i, 0))],
        out_specs=pl.BlockSpec((rt2, PACK_COLS), lambda i: (i, 0)),
        out_shape=jax.ShapeDtypeStruct((rows, PACK_COLS), F32),
        compiler_params=_cparams(("parallel",)),
    )(quad)
    halves = _exchange("share_sibling", reduced[:half_rows], SIBLING, distinct=False)
    eighths = _exchange("share_small", reduced[half_rows:], ALL7, distinct=False)
    shard_grad = halves.reshape(-1)
    small_grad = eighths.reshape(-1)

    grads = {}
    off = 0
    for k, _ in BIG + CONVS:
        shp = w[k].shape
        n = local2d[k].size
        grads[k] = shard_grad[off:off + n].reshape(shp)
        off += n
    off = 0
    for k in SMALL:
        shp = w[k].shape
        n = w[k].size
        grads[k] = small_grad[off:off + n].reshape(shp)
        off += n
    grads["w_ada"] = g_w_ada[None]

    delta, new_m, new_v = {}, {}, {}
    for k in WEIGHTS:
        shp = w[k].shape
        two_d = (-1, shp[-1]) if len(shp) > 1 else (1, -1)
        dk, mk, vk = _adamw("adamw_" + k, w[k].reshape(two_d), grads[k].reshape(two_d),
                            m_in[k].reshape(two_d), v_in[k].reshape(two_d))
        delta[k], new_m[k], new_v[k] = dk.reshape(shp), mk.reshape(shp), vk.reshape(shp)

    return (loss, grad_x[None], *[grads[k] for k in WEIGHTS], *[delta[k] for k in WEIGHTS],
            *[new_m[k] for k in WEIGHTS], *[new_v[k] for k in WEIGHTS])


def kernel(x, c, positions, w_ada, b_ada, norm1_g, w_in, conv_w, conv_b, w_gate_a, b_gate_a, w_gate_x, b_gate_x, lru_param, q_norm_g, w_uq, kv_norm_g, w_ukv, w_proj_rnn, w_proj_mla, w_out, norm2_g, w_up, ffn_conv_w, ffn_conv_b, w_down, final_g, loss_target, m_w_ada, m_b_ada, m_norm1_g, m_w_in, m_conv_w, m_conv_b, m_w_gate_a, m_b_gate_a, m_w_gate_x, m_b_gate_x, m_lru_param, m_q_norm_g, m_w_uq, m_kv_norm_g, m_w_ukv, m_w_proj_rnn, m_w_proj_mla, m_w_out, m_norm2_g, m_w_up, m_ffn_conv_w, m_ffn_conv_b, m_w_down, m_final_g, v_w_ada, v_b_ada, v_norm1_g, v_w_in, v_conv_w, v_conv_b, v_w_gate_a, v_b_gate_a, v_w_gate_x, v_b_gate_x, v_lru_param, v_q_norm_g, v_w_uq, v_kv_norm_g, v_w_ukv, v_w_proj_rnn, v_w_proj_mla, v_w_out, v_norm2_g, v_w_up, v_ffn_conv_w, v_ffn_conv_b, v_w_down, v_final_g):
    given = dict(locals())
    w = {k: given[k] for k in WEIGHTS}
    m_in = {k: given["m_" + k] for k in WEIGHTS}
    v_in = {k: given["v_" + k] for k in WEIGHTS}
    return _step(x, c, positions, w, m_in, v_in, loss_target)
```

```python
import functools
import math

import jax
import jax.numpy as jnp
from jax import lax
from jax.experimental import pallas as pl
from jax.experimental.pallas import tpu as pltpu

F32 = jnp.float32
BF16 = jnp.bfloat16

EPS = 1e-6
LRU_C = 8.0
N_HEADS = 16
QK_NOPE = 64
QK_ROPE = 32
HEAD_PAD = 128
ROPE_THETA = 10000.0
ADAM_LR = 0.001
ADAM_B1 = 0.9
ADAM_B2 = 0.999
ADAM_EPS = 1e-08
ADAM_WD = 0.01
ADAM_STEP = 10

LANE = 128
VMEM_LIMIT = 48 * 1024 * 1024
MM_TILE_M = MM_TILE_N = MM_TILE_K = 1408
PACK_COLS = 1024
PACK_ROW_UNIT = 32
MESH = pl.DeviceIdType.MESH

NN = (((1,), (0,)), ((), ()))
NT = (((1,), (1,)), ((), ()))


def _cparams(sem):
    return pltpu.CompilerParams(dimension_semantics=sem, vmem_limit_bytes=VMEM_LIMIT)


def _div_tile(n, cap, unit):
    best = None
    d = unit
    while d <= min(n, cap):
        if n % d == 0:
            best = d
        d += unit
    return n if best is None else best


def _mm(name, a, b, *, ta=False, tb=False, add=None, out_dtype=F32):
    if ta:
        kdim, m = a.shape
    else:
        m, kdim = a.shape
    if tb:
        n, kb = b.shape
    else:
        kb, n = b.shape
    assert kdim == kb, (name, a.shape, b.shape)
    tm = _div_tile(m, MM_TILE_M, 8 if not ta else LANE)
    tn = _div_tile(n, MM_TILE_N, LANE)
    tk = _div_tile(kdim, MM_TILE_K, LANE)
    nk = kdim // tk
    a_spec = pl.BlockSpec((tk, tm), lambda i, j, k: (k, i)) if ta else pl.BlockSpec((tm, tk), lambda i, j, k: (i, k))
    b_spec = pl.BlockSpec((tn, tk), lambda i, j, k: (j, k)) if tb else pl.BlockSpec((tk, tn), lambda i, j, k: (k, j))
    o_spec = pl.BlockSpec((tm, tn), lambda i, j, k: (i, j))
    has_add = add is not None
    dims = ((((0,) if ta else (1,)), ((1,) if tb else (0,))), ((), ()))

    def body(*refs):
        a_ref, b_ref = refs[0], refs[1]
        c_ref = refs[2] if has_add else None
        o_ref = refs[3] if has_add else refs[2]
        prod = lax.dot_general(a_ref[...].astype(BF16), b_ref[...].astype(BF16), dims, preferred_element_type=F32)
        if nk == 1:
            o_ref[...] = (prod + c_ref[...].astype(F32) if has_add else prod).astype(o_ref.dtype)
            return
        acc = refs[-1]
        k = pl.program_id(2)

        @pl.when(k == 0)
        def _():
            acc[...] = prod + c_ref[...].astype(F32) if has_add else prod

        @pl.when(jnp.logical_and(k > 0, k < nk - 1))
        def _():
            acc[...] += prod

        @pl.when(k == nk - 1)
        def _():
            o_ref[...] = (acc[...] + prod).astype(o_ref.dtype)

    ins = [a, b] + ([add] if has_add else [])
    specs = [a_spec, b_spec] + ([o_spec] if has_add else [])
    return pl.pallas_call(
        body, name=name, grid=(m // tm, n // tn, nk), in_specs=specs, out_specs=o_spec,
        out_shape=jax.ShapeDtypeStruct((m, n), out_dtype),
        scratch_shapes=[pltpu.VMEM((tm, tn), F32)] if nk > 1 else [],
        compiler_params=_cparams(("parallel", "parallel", "arbitrary")),
    )(*ins)


_IMAPS = {
    "row": lambda i: (i, 0),
    "col": lambda i: (0, i),
    "full": lambda i: (0, 0),
    "acc": lambda i: (0, 0),
}


def _tiled(name, fn, n, ins, outs):
    ni = len(ins)
    is_acc = [k == "acc" for *_, k in outs]

    def body(*refs):
        vals = fn(*[r[...] for r in refs[:ni]])
        orefs = refs[ni:]
        if any(is_acc):
            @pl.when(pl.program_id(0) == 0)
            def _():
                for r, a in zip(orefs, is_acc):
                    if a:
                        r[...] = jnp.zeros(r.shape, r.dtype)
        for r, v, a in zip(orefs, vals, is_acc):
            if a:
                r[...] += v.astype(r.dtype)
            else:
                r[...] = v.astype(r.dtype)

    res = pl.pallas_call(
        body, name=name, grid=(n,),
        in_specs=[pl.BlockSpec(bs, _IMAPS[k]) for _, bs, k in ins],
        out_specs=[pl.BlockSpec(bs, _IMAPS[k]) for _, _, bs, k in outs],
        out_shape=[jax.ShapeDtypeStruct(s, d) for s, d, _, _ in outs],
        compiler_params=_cparams(("arbitrary",)),
    )(*[a for a, _, _ in ins])
    return tuple(res)


def _vjp_of(fn, nin, diff):
    def g(*args):
        ins, cots = args[:nin], args[nin:]

        def f(*d):
            full = list(ins)
            for i, v in zip(diff, d):
                full[i] = v
            return fn(*full)

        outs, vjp = jax.vjp(f, *[ins[i] for i in diff])
        return vjp(tuple(c.astype(o.dtype) for c, o in zip(cots, outs)))
    return g


def _shift_rows(x, k, fill, up=False):
    n = x.shape[0]
    rows = lax.broadcasted_iota(jnp.int32, x.shape, 0)
    if up:
        return jnp.where(rows < n - k, pltpu.roll(x, n - k, 0), fill)
    return jnp.where(rows >= k, pltpu.roll(x, k, 0), fill)


@functools.partial(jax.custom_vjp, nondiff_argnums=(1,))
def _delay(x, k):
    return _shift_rows(x, k, 0.0)


def _delay_fwd(x, k):
    return _shift_rows(x, k, 0.0), None


def _delay_bwd(k, _, g):
    return (_shift_rows(g, k, 0.0, up=True),)


_delay.defvjp(_delay_fwd, _delay_bwd)


@functools.partial(jax.custom_vjp, nondiff_argnums=(1,))
def _lane_roll(x, s):
    return pltpu.roll(x, s, 1)


def _lane_roll_fwd(x, s):
    return pltpu.roll(x, s, 1), None


def _lane_roll_bwd(s, _, g):
    return (pltpu.roll(g, g.shape[1] - s, 1),)


_lane_roll.defvjp(_lane_roll_fwd, _lane_roll_bwd)


@jax.custom_vjp
def _bdot(x, w):
    return lax.dot_general(x.astype(BF16), w.astype(BF16), NN, preferred_element_type=F32)


def _bdot_fwd(x, w):
    return _bdot(x, w), (x, w)


def _bdot_bwd(res, g):
    x, w = res
    gb = g.astype(BF16)
    dx = lax.dot_general(gb, w.astype(BF16), NT, preferred_element_type=F32)
    dw = lax.dot_general(x.T.astype(BF16), gb, NN, preferred_element_type=F32)
    return dx, dw


_bdot.defvjp(_bdot_fwd, _bdot_bwd)


def _sigmoid(x):
    return 0.5 * (jnp.tanh(0.5 * x) + 1.0)


def _silu(x):
    return x * _sigmoid(x)


def _rms(x, g):
    return x * lax.rsqrt(jnp.mean(x * x, axis=-1, keepdims=True) + EPS) * g


def _causal_conv(x, w, b):
    kw = w.shape[0]
    tap = lax.broadcasted_iota(jnp.int32, w.shape, 0)
    y = b
    for k in range(kw):
        d = kw - 1 - k
        wk = jnp.sum(jnp.where(tap == k, w, 0.0), axis=0, keepdims=True)
        y = y + wk * (x if d == 0 else _delay(x, d))
    return y


def _rotate(x, cos_f, sin_a, sin_b):
    reps = x.shape[1] // LANE
    if reps > 1:
        cos_f, sin_a, sin_b = (jnp.tile(t, (1, reps)) for t in (cos_f, sin_a, sin_b))
    n = x.shape[1]
    half = QK_ROPE // 2
    return x * cos_f + _lane_roll(x, n - half) * sin_a + _lane_roll(x, half) * sin_b


def _softplus_neg(l):
    u = jnp.exp(-jnp.abs(l))
    log1p_u = jnp.where(u < 0.01, u * (1.0 - u * (0.5 - u * (1.0 / 3.0))), jnp.log(1.0 + u))
    return jnp.maximum(-l, 0.0) + log1p_u


def _f_ln(x, g, scale, shift):
    return (_rms(x, g) * (1.0 + scale) + shift,)


def _f_qkv(qkv, cos_f, sin_a, sin_b, qg, kvg):
    nq, nkv = qg.shape[1], kvg.shape[1]
    qn = _rms(qkv[:, :nq], qg)
    kvn = _rms(qkv[:, nq:nq + nkv], kvg)
    kr = _rotate(qkv[:, nq + nkv:], cos_f, sin_a, sin_b)
    return qn, kvn, kr


def _f_qkv_bwd(qkv, cos_f, sin_a, sin_b, qg, kvg, dqn, dkvn, dkr):
    nq, nkv = qg.shape[1], kvg.shape[1]
    _, vjp_q = jax.vjp(_rms, qkv[:, :nq], qg)
    _, vjp_kv = jax.vjp(_rms, qkv[:, nq:nq + nkv], kvg)
    _, vjp_r = jax.vjp(lambda t: _rotate(t, cos_f, sin_a, sin_b), qkv[:, nq + nkv:])
    dq_lat, dqg = vjp_q(dqn)
    dkv_lat, dkvg = vjp_kv(dkvn)
    (dkr_pre,) = vjp_r(dkr)
    return jnp.concatenate([dq_lat, dkv_lat, dkr_pre], axis=1), dqg, dkvg


def _f_rotq(q, cos_f, sin_a, sin_b):
    return (_rotate(q, cos_f, sin_a, sin_b),)


def _merge(g_rnn, g_mla, p_rnn, p_mla):
    return _sigmoid(g_rnn) * p_rnn + _sigmoid(g_mla) * p_mla


def _f_merge(g, p_rnn, p_mla):
    d = p_rnn.shape[1]
    return (_merge(g[:, :d], g[:, d:], p_rnn, p_mla),)


def _f_merge_bwd(g, p_rnn, p_mla, dm):
    d = p_rnn.shape[1]
    _, vjp = jax.vjp(_merge, g[:, :d], g[:, d:], p_rnn, p_mla)
    dg_rnn, dg_mla, dp_rnn, dp_mla = vjp(dm)
    return jnp.concatenate([dg_rnn, dg_mla], axis=1), dp_rnn, dp_mla


def _f_res_ln(x, o, gate, g2, scale, shift):
    x1 = x + gate * o
    return x1, _rms(x1, g2) * (1.0 + scale) + shift


def _ffn(u_gate, u_val, cw_gate, cw_val, cb_gate, cb_val):
    return _silu(_causal_conv(u_gate, cw_gate, cb_gate)) * _causal_conv(u_val, cw_val, cb_val)


def _halves(*arrays):
    out = []
    for a in arrays:
        half = a.shape[1] // 2
        out += [a[:, :half], a[:, half:]]
    return out


def _f_ffn(u, cw, cb):
    ug, uv, cwg, cwv, cbg, cbv = _halves(u, cw, cb)
    return (_ffn(ug, uv, cwg, cwv, cbg, cbv),)


def _f_ffn_bwd(u, cw, cb, dact):
    _, vjp = jax.vjp(_ffn, *_halves(u, cw, cb))
    dug, duv, dcwg, dcwv, dcbg, dcbv = vjp(dact)
    cat = lambda a, b: jnp.concatenate([a, b], axis=1)
    return cat(dug, duv), cat(dcwg, dcwv), cat(dcbg, dcbv)


def _f_loss(x1, f, tgt, gate, fg):
    y = _rms(x1 + gate * f, fg)
    err = (y - tgt) * (y - tgt)
    return 0.5 * jnp.sum(jnp.mean(err, axis=-1, keepdims=True), axis=0, keepdims=True)


def _f_loss_and_grads(x1, f, tgt, gate, fg):
    loss, vjp = jax.vjp(lambda a, b, c, d: _f_loss(a, b, tgt, c, d), x1, f, gate, fg)
    dx1, df, dgate, dfg = vjp(jnp.ones((1, 1), F32))
    return dx1, df, jnp.broadcast_to(loss, (1, LANE)), dgate, dfg


def _f_lru_coeffs(xr, cw, cb, wa, ba, wx, bx, lru, reset):
    xc = _causal_conv(xr, cw, cb)
    r = _sigmoid(_bdot(xc, wa) + ba)
    i = _sigmoid(_bdot(xc, wx) + bx)
    log_a = (-LRU_C) * r * _softplus_neg(lru)
    a = jnp.exp(log_a)
    mult = jnp.sqrt(-jnp.tanh(log_a) * (1.0 + a * a))
    is_reset = reset > 0.5
    a = jnp.where(is_reset, 0.0, a)
    mult = jnp.where(is_reset, 1.0, mult)
    return a, mult * (i * xc)


def _scan(a, b, up=False):
    n = a.shape[0]
    k = 1
    while k < n:
        b = b + a * _shift_rows(b, k, 0.0, up)
        if 2 * k < n:
            a = a * _shift_rows(a, k, 1.0, up)
        k *= 2
    return b


def _f_lru_fwd(xr, cw, cb, wa, ba, wx, bx, lru, reset):
    a, b = _f_lru_coeffs(xr, cw, cb, wa, ba, wx, bx, lru, reset)
    h = _scan(a, b)
    return h, h


def _f_lru_bwd(xr, cw, cb, wa, ba, wx, bx, lru, reset, h, dh):
    (a, _), vjp = jax.vjp(lambda *p: _f_lru_coeffs(*p, reset), xr, cw, cb, wa, ba, wx, bx, lru)
    g = _scan(_shift_rows(a, 1, 0.0, up=True), dh, up=True)
    return vjp((g * _shift_rows(h, 1, 0.0), g))


def _attn_tile(s):
    return 512 if s >= 1024 else s // 2


def _keys(kv, kr):
    lane = lax.broadcasted_iota(jnp.int32, kv.shape, 1)
    return jnp.where(lane < QK_NOPE, kv, kr)


def _masked_scores(q, kc, qi, ki, t, scale):
    s = lax.dot_general(q, kc, NT, preferred_element_type=F32) * scale
    rows = qi * t + lax.broadcasted_iota(jnp.int32, s.shape, 0)
    cols = ki * t + lax.broadcasted_iota(jnp.int32, s.shape, 1)
    return jnp.where(cols <= rows, s, -jnp.inf)


def _attn_fwd(q, kv, kr):
    s_len = q.shape[0]
    t = _attn_tile(s_len)
    nb = s_len // t
    scale = 1.0 / math.sqrt(QK_NOPE + QK_ROPE)

    def body(q_ref, kv_ref, kr_ref, o_ref, lse_ref, m_s, l_s, acc_s):
        qi, ki = pl.program_id(1), pl.program_id(2)

        @pl.when(ki == 0)
        def _():
            m_s[...] = jnp.full(m_s.shape, -jnp.inf, F32)
            l_s[...] = jnp.zeros(l_s.shape, F32)
            acc_s[...] = jnp.zeros(acc_s.shape, F32)

        @pl.when(ki <= qi)
        def _():
            kvv = kv_ref[...]
            s = _masked_scores(q_ref[...], _keys(kvv, kr_ref[...]), qi, ki, t, scale)
            m_old = m_s[...]
            m_new = jnp.maximum(m_old, jnp.max(s, axis=-1, keepdims=True))
            alpha = jnp.exp(m_old - m_new)
            p = jnp.exp(s - m_new)
            l_s[...] = alpha * l_s[...] + jnp.sum(p, axis=-1, keepdims=True)
            acc_s[...] = alpha * acc_s[...] + lax.dot_general(p.astype(BF16), kvv, NN, preferred_element_type=F32)
            m_s[...] = m_new

        @pl.when(ki == qi)
        def _():
            lane = lax.broadcasted_iota(jnp.int32, acc_s.shape, 1)
            o_ref[...] = jnp.where(lane >= QK_NOPE, acc_s[...] / l_s[...], 0.0).astype(o_ref.dtype)
            lse_ref[0] = m_s[...] + jnp.log(l_s[...])

    return pl.pallas_call(
        body, name="attn_fwd", grid=(N_HEADS, nb, nb),
        in_specs=[pl.BlockSpec((t, HEAD_PAD), lambda h, i, j: (i, h)),
                  pl.BlockSpec((t, HEAD_PAD), lambda h, i, j: (jnp.minimum(i, j), h)),
                  pl.BlockSpec((t, HEAD_PAD), lambda h, i, j: (jnp.minimum(i, j), 0))],
        out_specs=[pl.BlockSpec((t, HEAD_PAD), lambda h, i, j: (i, h)),
                   pl.BlockSpec((1, t, 1), lambda h, i, j: (h, i, 0))],
        out_shape=[jax.ShapeDtypeStruct((s_len, N_HEADS * HEAD_PAD), BF16),
                   jax.ShapeDtypeStruct((N_HEADS, s_len, 1), F32)],
        scratch_shapes=[pltpu.VMEM((t, 1), F32), pltpu.VMEM((t, 1), F32), pltpu.VMEM((t, HEAD_PAD), F32)],
        compiler_params=_cparams(("arbitrary", "arbitrary", "arbitrary")),
    )(q, kv, kr)


def _attn_bwd(q, kv, kr, o, lse, do):
    s_len = q.shape[0]
    t = _attn_tile(s_len)
    nb = s_len // t
    scale = 1.0 / math.sqrt(QK_NOPE + QK_ROPE)

    def body(q_ref, kv_ref, kr_ref, o_ref, lse_ref, do_ref, dq_ref, dkv_ref, dkr_ref, dk_s, dv_s):
        h, kb, qb = pl.program_id(0), pl.program_id(1), pl.program_id(2)

        @pl.when(jnp.logical_and(h == 0, jnp.logical_and(kb == 0, qb == 0)))
        def _():
            dkr_ref[...] = jnp.zeros(dkr_ref.shape, F32)

        @pl.when(jnp.logical_and(kb == 0, qb == 0))
        def _():
            dq_ref[...] = jnp.zeros(dq_ref.shape, F32)

        @pl.when(qb == kb)
        def _():
            dk_s[...] = jnp.zeros(dk_s.shape, F32)
            dv_s[...] = jnp.zeros(dv_s.shape, F32)

        @pl.when(qb >= kb)
        def _():
            qv, kvv, dov = q_ref[...], kv_ref[...], do_ref[...]
            kc = _keys(kvv, kr_ref[...])
            s = _masked_scores(qv, kc, qb, kb, t, scale)
            p = jnp.exp(s - lse_ref[0])
            delta = jnp.sum(dov.astype(F32) * o_ref[...].astype(F32), axis=-1, keepdims=True)
            dp = lax.dot_general(dov, kvv, NT, preferred_element_type=F32)
            ds = (p * (dp - delta) * scale).astype(BF16)
            dv_s[...] += lax.dot_general(p.T.astype(BF16), dov, NN, preferred_element_type=F32)
            dk_s[...] += lax.dot_general(ds.astype(F32).T.astype(BF16), qv, NN, preferred_element_type=F32)
            rows = pl.ds(pl.multiple_of(qb * t, t), t)
            dq_ref[rows, :] += lax.dot_general(ds, kc, NN, preferred_element_type=F32)

        @pl.when(qb == nb - 1)
        def _():
            lane = lax.broadcasted_iota(jnp.int32, dk_s.shape, 1)
            dkv_ref[...] = jnp.where(lane < QK_NOPE, dk_s[...], dv_s[...])
            rows = pl.ds(pl.multiple_of(kb * t, t), t)
            dkr_ref[rows, :] += jnp.where(lane >= QK_NOPE, dk_s[...], 0.0)

    wide = N_HEADS * HEAD_PAD
    qmap = lambda h, k, i: (jnp.maximum(i, k), h)
    return pl.pallas_call(
        body, name="attn_bwd", grid=(N_HEADS, nb, nb),
        in_specs=[pl.BlockSpec((t, HEAD_PAD), qmap),
                  pl.BlockSpec((t, HEAD_PAD), lambda h, k, i: (k, h)),
                  pl.BlockSpec((t, HEAD_PAD), lambda h, k, i: (k, 0)),
                  pl.BlockSpec((t, HEAD_PAD), qmap),
                  pl.BlockSpec((1, t, 1), lambda h, k, i: (h, jnp.maximum(i, k), 0)),
                  pl.BlockSpec((t, HEAD_PAD), qmap)],
        out_specs=[pl.BlockSpec((s_len, HEAD_PAD), lambda h, k, i: (0, h)),
                   pl.BlockSpec((t, HEAD_PAD), lambda h, k, i: (k, h)),
                   pl.BlockSpec((s_len, HEAD_PAD), lambda h, k, i: (0, 0))],
        out_shape=[jax.ShapeDtypeStruct((s_len, wide), F32),
                   jax.ShapeDtypeStruct((s_len, wide), F32),
                   jax.ShapeDtypeStruct((s_len, HEAD_PAD), F32)],
        scratch_shapes=[pltpu.VMEM((t, HEAD_PAD), F32), pltpu.VMEM((t, HEAD_PAD), F32)],
        compiler_params=_cparams(("arbitrary", "arbitrary", "arbitrary")),
    )(q, kv, kr, o, lse, do)


def _adamw(name, w, g, m, v):
    rows, cols = w.shape
    tr = _div_tile(rows, max(8, (2 * 1024 * 1024) // (4 * cols)), 8)

    def body(w_ref, g_ref, m_ref, v_ref, d_ref, nm_ref, nv_ref):
        gv = g_ref[...]
        nm = ADAM_B1 * m_ref[...] + (1.0 - ADAM_B1) * gv
        nv = ADAM_B2 * v_ref[...] + (1.0 - ADAM_B2) * jnp.square(gv)
        m_hat = nm / (1.0 - ADAM_B1 ** ADAM_STEP)
        v_hat = nv / (1.0 - ADAM_B2 ** ADAM_STEP)
        d_ref[...] = -ADAM_LR * (m_hat / (jnp.sqrt(v_hat) + ADAM_EPS) + ADAM_WD * w_ref[...])
        nm_ref[...] = nm
        nv_ref[...] = nv

    spec = pl.BlockSpec((tr, cols), lambda i: (i, 0))
    return pl.pallas_call(
        body, name=name, grid=(rows // tr,), in_specs=[spec] * 4, out_specs=[spec] * 3,
        out_shape=[jax.ShapeDtypeStruct((rows, cols), F32)] * 3,
        compiler_params=_cparams(("parallel",)),
    )(w, g, m, v)


ALL7 = (1, 2, 3, 4, 5, 6, 7)
CHIPS = (2, 4, 6)


def _all_gather(name, src, masks):
    bits = 0
    for m in masks:
        bits |= m
    nslots = {7: 8, 6: 4}[bits]
    nm = len(masks)

    def slot_of(x, y, c):
        return {7: 4 * x + 2 * y + c, 6: 2 * x + y}[bits]

    def body(src_ref, out_ref, send_sems, recv_sems, local_sem):
        x, y, c = lax.axis_index("x"), lax.axis_index("y"), lax.axis_index("c")
        mine = slot_of(x, y, c)
        own = pltpu.make_async_copy(src_ref, out_ref.at[mine], local_sem)
        own.start()
        copies = []
        for i, m in enumerate(masks):
            peer = _peer(x, y, c, m)
            copies.append((
                pltpu.make_async_remote_copy(
                    src_ref=src_ref, dst_ref=out_ref.at[mine], send_sem=send_sems.at[i], recv_sem=recv_sems.at[i],
                    device_id=peer, device_id_type=MESH),
                pltpu.make_async_remote_copy(
                    src_ref=src_ref, dst_ref=out_ref.at[slot_of(*peer)], send_sem=send_sems.at[i],
                    recv_sem=recv_sems.at[i], device_id=peer, device_id_type=MESH)))
        for send, _ in copies:
            send.start()
        for _, arrival in copies:
            arrival.wait_recv()
        for send, _ in copies:
            send.wait_send()
        own.wait()

    return pl.pallas_call(
        body, name=name,
        in_specs=[pl.BlockSpec(memory_space=pl.ANY)], out_specs=pl.BlockSpec(memory_space=pl.ANY),
        out_shape=jax.ShapeDtypeStruct((nslots,) + tuple(src.shape), src.dtype),
        scratch_shapes=[pltpu.SemaphoreType.DMA((nm,)), pltpu.SemaphoreType.DMA((nm,)), pltpu.SemaphoreType.DMA],
    )(src)


def _peer(x, y, c, m):
    return (1 - x if m & 4 else x, 1 - y if m & 2 else y, 1 - c if m & 1 else c)


def _comm_call(name, emit, srcs, out_shapes, n_sems):
    n = len(srcs)

    def body(*refs):
        src_refs, out_refs = refs[:n], refs[n:n + len(out_shapes)]
        send_sems, recv_sems = refs[-2], refs[-1]

        def copy(src, dst, i, peer):
            return pltpu.make_async_remote_copy(src_ref=src, dst_ref=dst, send_sem=send_sems.at[i],
                                                recv_sem=recv_sems.at[i], device_id=peer, device_id_type=MESH)

        emit(lax.axis_index("x"), lax.axis_index("y"), lax.axis_index("c"), src_refs, out_refs, copy)

    hbm = pl.BlockSpec(memory_space=pl.ANY)
    return pl.pallas_call(
        body, name=name, in_specs=[hbm] * n, out_specs=[hbm] * len(out_shapes), out_shape=out_shapes,
        scratch_shapes=[pltpu.SemaphoreType.DMA((n_sems,)), pltpu.SemaphoreType.DMA((n_sems,))],
    )(*srcs)


def _gather_weights(halves):
    n = len(halves)

    def emit(x, y, c, srcs, outs, copy):
        chip = 2 * x + y
        sib = (x, y, 1 - c)
        first, relay, landed, relayed = [], [], [], []
        for j, m in enumerate(CHIPS):
            px, py, _ = _peer(x, y, c, m)
            theirs = 2 * px + py
            for k in range(n):
                i = 6 * k + j
                first.append(copy(srcs[k].at[c], outs[k].at[chip, c], i, (px, py, c)))
                landed.append(copy(srcs[k].at[c], outs[k].at[theirs, c], i, (px, py, c)))
                relay.append(copy(outs[k].at[theirs, c], outs[k].at[theirs, c], i + 3, sib))
                relayed.append(copy(outs[k].at[theirs, 1 - c], outs[k].at[theirs, 1 - c], i + 3, sib))
        for cp in first:
            cp.start()
        for arrival, onward in zip(landed, relay):
            arrival.wait_recv()
            onward.start()
        for arrival in relayed:
            arrival.wait_recv()
        for cp in first + relay:
            cp.wait_send()

    shapes = [jax.ShapeDtypeStruct((4,) + h.shape, h.dtype) for h in halves]
    return _comm_call("gather_weights", emit, halves, shapes, 6 * n)


def _pair_exchange(chunks):
    def emit(x, y, c, srcs, outs, copy):
        sib = (x, y, 1 - c)
        sends = [copy(s.at[:, 1 - c], o, k, sib) for k, (s, o) in enumerate(zip(srcs, outs))]
        for cp in sends:
            cp.start()
        for cp in sends:
            cp.wait_recv()
        for cp in sends:
            cp.wait_send()

    shapes = [jax.ShapeDtypeStruct((4,) + g.shape[2:], g.dtype) for g in chunks]
    return _comm_call("reduce_pair_exchange", emit, chunks, shapes, len(chunks))


def _chips_alltoall(parts):
    def emit(x, y, c, srcs, outs, copy):
        chip = 2 * x + y
        sends, arrivals = [], []
        for j, m in enumerate(CHIPS):
            px, py, _ = _peer(x, y, c, m)
            theirs = 2 * px + py
            for k, (s, o) in enumerate(zip(srcs, outs)):
                sends.append(copy(s.at[theirs], o.at[chip], 3 * k + j, (px, py, c)))
                arrivals.append(copy(s.at[theirs], o.at[theirs], 3 * k + j, (px, py, c)))
        for cp in sends:
            cp.start()
        for cp in arrivals:
            cp.wait_recv()
        for cp in sends:
            cp.wait_send()

    shapes = [jax.ShapeDtypeStruct(p.shape, p.dtype) for p in parts]
    return _comm_call("reduce_chips_exchange", emit, parts, shapes, 3 * len(parts))


def _share_sibling(parts):
    def emit(x, y, c, srcs, outs, copy):
        sib = (x, y, 1 - c)
        sends = [copy(s, o.at[c], k, sib) for k, (s, o) in enumerate(zip(srcs, outs))]
        arrivals = [copy(s, o.at[1 - c], k, sib) for k, (s, o) in enumerate(zip(srcs, outs))]
        for cp in sends:
            cp.start()
        for cp in arrivals:
            cp.wait_recv()
        for cp in sends:
            cp.wait_send()

    shapes = [jax.ShapeDtypeStruct((2,) + p.shape, p.dtype) for p in parts]
    return _comm_call("share_sibling", emit, parts, shapes, len(parts))


def _reduce_pair(name, a, b):
    rows = a.shape[0] * a.shape[1]
    cols = a.shape[2]
    rt = _div_tile(rows, max(16, (1 << 20) // (4 * cols)), 16)
    spec = (rt, cols)
    (out,) = _tiled(name, lambda u, v: (u + v,), rows // rt,
                    [(a.reshape(rows, cols), spec, "row"), (b.reshape(rows, cols), spec, "row")],
                    [((rows, cols), BF16, spec, "row")])
    return out.reshape(a.shape)


def _reduce_quad(name, q):
    _, h, cols = q.shape
    rt = _div_tile(h, max(16, (1 << 20) // (4 * cols)), 16)

    def body(q_ref, o_ref):
        v = q_ref[...].astype(F32)
        o_ref[...] = ((v[0] + v[1]) + v[2]) + v[3]

    return pl.pallas_call(
        body, name=name, grid=(h // rt,),
        in_specs=[pl.BlockSpec((4, rt, cols), lambda i: (0, i, 0))],
        out_specs=pl.BlockSpec((rt, cols), lambda i: (i, 0)),
        out_shape=jax.ShapeDtypeStruct((h, cols), F32),
        compiler_params=_cparams(("parallel",)),
    )(q)


def _unshard(seg, kind):
    n, r, c = seg.shape
    if kind == "col":
        return seg.transpose(1, 0, 2).reshape(r, n * c)
    return seg.reshape(n * r, c)


def _pad_rows(flat, rows):
    n, ln = flat.shape
    return jnp.pad(flat, ((0, 0), (0, rows * PACK_COLS - ln))).reshape(n, rows, PACK_COLS)


def _block_diag_pairs(w):
    n2, bs, _ = w.shape
    eye = jnp.eye(2, dtype=w.dtype)
    z = w.reshape(n2 // 2, 2, bs, 1, bs) * eye[None, :, None, :, None]
    return z.reshape(n2 // 2, 2 * bs, 2 * bs).transpose(1, 0, 2).reshape(2 * bs, n2 * bs)


def _block_diag_pairs_t(d, bs=64):
    n = d.shape[1] // (2 * bs)
    z = d.reshape(2 * bs, n, 2 * bs).transpose(1, 0, 2).reshape(n, 2, bs, 2, bs)
    return jnp.stack([z[:, 0, :, 0, :], z[:, 1, :, 1, :]], axis=1).reshape(2 * n, bs, bs)


def _interleave(w, half_tile=LANE):
    r, c = w.shape
    return w.reshape(r, 2, c // (2 * half_tile), half_tile).transpose(0, 2, 1, 3).reshape(r, c)


def _deinterleave(w, half_tile=LANE):
    r, c = w.shape
    return w.reshape(r, c // (2 * half_tile), 2, half_tile).transpose(0, 2, 1, 3).reshape(r, c)


BIG = (("w_in", "col"), ("w_uq", "col"), ("w_ukv", "col"), ("w_proj_rnn", "row"), ("w_proj_mla", "row"),
       ("w_out", "row"), ("w_up", "col"), ("w_down", "row"))
CONVS = (("conv_w", "col"), ("ffn_conv_w", "col"))
SMALL = ("b_ada", "norm1_g", "conv_b", "w_gate_a", "b_gate_a", "w_gate_x", "b_gate_x", "lru_param",
         "q_norm_g", "kv_norm_g", "norm2_g", "ffn_conv_b", "final_g")
WEIGHTS = ("w_ada", "b_ada", "norm1_g", "w_in", "conv_w", "conv_b", "w_gate_a", "b_gate_a", "w_gate_x",
           "b_gate_x", "lru_param", "q_norm_g", "w_uq", "kv_norm_g", "w_ukv", "w_proj_rnn", "w_proj_mla",
           "w_out", "norm2_g", "w_up", "ffn_conv_w", "ffn_conv_b", "w_down", "final_g")


def _step(x, c, positions, w, m_in, v_in, loss_target):
    s_len, d = x.shape[1], x.shape[2]
    x2d = x[0]
    tgt = loss_target[0]
    xi, yi, ci = lax.axis_index("x"), lax.axis_index("y"), lax.axis_index("c")
    chip = 2 * xi + yi
    me = 2 * chip + ci
    tile = min(256, s_len)
    nt = s_len // tile

    local2d = {k: w[k][0] for k, _ in BIG + CONVS}
    halves_bf = [local2d[k].astype(BF16).reshape(2, local2d[k].shape[0] // 2, local2d[k].shape[1]) for k, _ in BIG]
    full = {}
    for (k, kind), g, hb in zip(BIG, _gather_weights(halves_bf), halves_bf):
        g = lax.dynamic_update_index_in_dim(g, hb[None], chip, 0)
        full[k] = _unshard(g.reshape((4,) + local2d[k].shape), kind)
    conv_flat = jnp.concatenate([local2d[k].reshape(-1) for k, _ in CONVS])
    conv_rows = -(-conv_flat.shape[0] // PACK_COLS)
    conv_all = _all_gather("gather_conv_w", _pad_rows(conv_flat[None], conv_rows)[0], CHIPS)
    conv_all = conv_all.reshape(4, -1)
    off = 0
    for k, kind in CONVS:
        r, cc = local2d[k].shape
        full[k] = _unshard(conv_all[:, off:off + r * cc].reshape(4, r, cc), kind)
        off += r * cc

    d_rnn = w["conv_b"].shape[1]
    n_q, n_kv = w["q_norm_g"].shape[1], w["kv_norm_g"].shape[1]
    w_in = full["w_in"]
    o1, o2, o3 = d_rnn + n_q, d_rnn + n_q + n_kv, d_rnn + n_q + n_kv + QK_ROPE
    w_rnn = w_in[:, :d_rnn]
    zpad = lambda n: jnp.zeros((d, n), BF16)
    w_qkv = jnp.concatenate([w_in[:, d_rnn:o2], zpad(QK_NOPE), w_in[:, o2:o3], zpad(LANE - QK_NOPE - QK_ROPE)], axis=1)
    w_g = w_in[:, o3:]
    hd = QK_NOPE + QK_ROPE
    w_uq = jnp.pad(full["w_uq"].reshape(n_q, N_HEADS, hd), ((0, 0), (0, 0), (0, HEAD_PAD - hd))).reshape(n_q, -1)
    w_ukv = full["w_ukv"]
    w_pr = full["w_proj_rnn"]
    v_head = w_ukv.shape[1] // N_HEADS - QK_NOPE
    w_pm = jnp.pad(full["w_proj_mla"].reshape(N_HEADS, v_head, d), ((0, 0), (HEAD_PAD - v_head, 0), (0, 0))).reshape(-1, d)
    w_out = full["w_out"]
    w_up = _interleave(full["w_up"])
    w_down = full["w_down"]
    ffn_cw = _interleave(full["ffn_conv_w"])
    ffn_cb = _interleave(w["ffn_conv_b"])
    conv_w, conv_b = full["conv_w"], w["conv_b"]
    wa_bd = _block_diag_pairs(w["w_gate_a"][0])
    wx_bd = _block_diag_pairs(w["w_gate_x"][0])

    c_all = _all_gather("gather_c", c, ALL7).reshape(8, d)
    c_rows = 128
    (c_act,) = _tiled("silu_c", lambda v: (_silu(v),), 1, [(jnp.pad(c_all, ((0, c_rows - 8), (0, 0))), (c_rows, d), "full")],
                      [((c_rows, d), F32, (c_rows, d), "full")])
    w_ada = w["w_ada"][0]
    n_mod = w_ada.shape[1]
    b_loc = lax.dynamic_slice_in_dim(w["b_ada"], chip * n_mod, n_mod, axis=1)
    mod_loc = _mm("ada_fwd", c_act, w_ada, add=jnp.broadcast_to(b_loc, (c_rows, n_mod)))
    mod_all = _all_gather("gather_mod", mod_loc[:8], CHIPS)
    mod = lax.dynamic_index_in_dim(mod_all, me, 1, keepdims=False).reshape(1, -1)
    shift1, scale1, gate1, shift2, scale2, gate2 = [mod[:, i * d:(i + 1) * d] for i in range(6)]

    half = QK_ROPE // 2
    inv_freq = ROPE_THETA ** (-jnp.arange(half, dtype=F32) / half)
    ang = positions[0].astype(F32)[:, None] * inv_freq
    cos, sin = jnp.cos(ang), jnp.sin(ang)
    one, zero = jnp.ones((s_len, QK_NOPE), F32), jnp.zeros((s_len, half), F32)
    tail = jnp.zeros((s_len, LANE - QK_NOPE - QK_ROPE), F32)
    cos_f = jnp.concatenate([one, cos, cos, tail + 1.0], axis=1)
    sin_a = jnp.concatenate([one * 0.0, -sin, zero, tail], axis=1)
    sin_b = jnp.concatenate([one * 0.0, zero, sin, tail], axis=1)
    reset = (positions[0] == 0).astype(F32)[:, None]
    tabs = [(cos_f, (tile, LANE), "row"), (sin_a, (tile, LANE), "row"), (sin_b, (tile, LANE), "row")]

    def rowspec(a):
        return (a, (tile, a.shape[1]), "row")

    def full2(a):
        return (a, a.shape, "full")

    def rowout(cols, dt):
        return ((s_len, cols), dt, (tile, cols), "row")

    def accout(a):
        return (a.shape, F32, a.shape, "acc")

    norm1_g, norm2_g, final_g = w["norm1_g"], w["norm2_g"], w["final_g"].reshape(1, d)
    ln1_in = [rowspec(x2d), full2(norm1_g), full2(scale1), full2(shift1)]
    (h1,) = _tiled("ln1", _f_ln, nt, ln1_in, [rowout(d, BF16)])
    x_rnn = _mm("in_rnn", h1, w_rnn)
    qkv = _mm("in_qkv", h1, w_qkv)
    gates = _mm("in_gates", h1, w_g)

    ct = LANE
    n_ct = d_rnn // ct
    colspec = lambda a, width=ct: (a, (a.shape[0], width), "col")
    lru_in = [colspec(x_rnn), colspec(conv_w), colspec(conv_b), colspec(wa_bd), colspec(w["b_gate_a"]),
              colspec(wx_bd), colspec(w["b_gate_x"]), colspec(w["lru_param"]), full2(reset)]
    y_rnn, h_rnn = _tiled("lru_fwd", _f_lru_fwd, n_ct, lru_in,
                          [((s_len, d_rnn), BF16, (s_len, ct), "col"), ((s_len, d_rnn), F32, (s_len, ct), "col")])

    qkv_in = [rowspec(qkv)] + tabs + [full2(w["q_norm_g"]), full2(w["kv_norm_g"])]
    qn, kvn, kr = _tiled("qkv_norm", _f_qkv, nt, qkv_in, [rowout(n_q, BF16), rowout(n_kv, BF16), rowout(LANE, BF16)])
    q_pre = _mm("up_q", qn, w_uq)
    kv = _mm("up_kv", kvn, w_ukv, out_dtype=BF16)
    (q_cat,) = _tiled("rot_q", _f_rotq, nt, [rowspec(q_pre)] + tabs, [rowout(q_pre.shape[1], BF16)])
    o_mla, lse = _attn_fwd(q_cat, kv, kr)

    p_rnn = _mm("proj_rnn", y_rnn, w_pr)
    p_mla = _mm("proj_mla", o_mla, w_pm)
    merge_in = [rowspec(gates), rowspec(p_rnn), rowspec(p_mla)]
    (merged,) = _tiled("merge", _f_merge, nt, merge_in, [rowout(d, BF16)])
    o_tok = _mm("out_proj", merged, w_out)
    res_in = [rowspec(x2d), rowspec(o_tok), full2(gate1), full2(norm2_g), full2(scale2), full2(shift2)]
    x1, h2 = _tiled("res_ln2", _f_res_ln, nt, res_in, [rowout(d, F32), rowout(d, BF16)])
    u_pre = _mm("ffn_up", h2, w_up)
    d_ff2 = u_pre.shape[1]
    n_ft = d_ff2 // (2 * LANE)
    ffn_in = [colspec(u_pre, 2 * LANE), colspec(ffn_cw, 2 * LANE), colspec(ffn_cb, 2 * LANE)]
    (act,) = _tiled("ffn_conv", _f_ffn, n_ft, ffn_in, [((s_len, d_ff2 // 2), BF16, (s_len, LANE), "col")])
    f_tok = _mm("ffn_down", act, w_down)

    loss_in = [rowspec(x1), rowspec(f_tok), rowspec(tgt), full2(gate2), full2(final_g)]
    dx1, df, loss_row, d_gate2, d_final_g = _tiled(
        "loss", _f_loss_and_grads, nt, loss_in,
        [rowout(d, F32), rowout(d, BF16), ((1, LANE), F32, (1, LANE), "acc"), accout(gate2), accout(final_g)])
    loss = lax.psum(loss_row[0, 0], ("x", "y", "c"))

    d_act = _mm("ffn_down_dx", df, w_down, tb=True)
    g_w_down = _mm("ffn_down_dw", act, df, ta=True)
    du, g_ffn_cw, g_ffn_cb = _tiled(
        "ffn_conv_bwd", _f_ffn_bwd, n_ft, ffn_in + [colspec(d_act)],
        [((s_len, d_ff2), BF16, (s_len, 2 * LANE), "col"), (ffn_cw.shape, F32, (ffn_cw.shape[0], 2 * LANE), "col"),
         (ffn_cb.shape, F32, (1, 2 * LANE), "col")])
    dh2 = _mm("ffn_up_dx", du, w_up, tb=True)
    g_w_up = _mm("ffn_up_dw", h2, du, ta=True)

    res_bwd = _vjp_of(_f_res_ln, 6, (0, 1, 2, 3, 4, 5))
    dx_res, do_tok, d_gate1, g_norm2, d_scale2, d_shift2 = _tiled(
        "res_ln2_bwd", res_bwd, nt, res_in + [rowspec(dx1), rowspec(dh2)],
        [rowout(d, F32), rowout(d, BF16), accout(gate1), accout(norm2_g), accout(scale2), accout(shift2)])
    d_merged = _mm("out_proj_dx", do_tok, w_out, tb=True)
    g_w_out = _mm("out_proj_dw", merged, do_tok, ta=True)
    d_gates, dp_rnn, dp_mla = _tiled(
        "merge_bwd", _f_merge_bwd, nt, merge_in + [rowspec(d_merged)],
        [rowout(gates.shape[1], BF16), rowout(d, BF16), rowout(d, BF16)])
    dy_rnn = _mm("proj_rnn_dx", dp_rnn, w_pr, tb=True)
    g_w_pr = _mm("proj_rnn_dw", y_rnn, dp_rnn, ta=True)
    do_mla = _mm("proj_mla_dx", dp_mla, w_pm, tb=True, out_dtype=BF16)
    g_w_pm = _mm("proj_mla_dw", o_mla, dp_mla, ta=True)

    dq_cat, dkv, dkr = _attn_bwd(q_cat, kv, kr, o_mla, lse, do_mla)
    rot_bwd = _vjp_of(_f_rotq, 4, (0,))
    (dq_pre,) = _tiled("rot_q_bwd", rot_bwd, nt, [rowspec(q_pre)] + tabs + [rowspec(dq_cat)],
                       [rowout(q_pre.shape[1], BF16)])
    dqn = _mm("up_q_dx", dq_pre, w_uq, tb=True)
    g_w_uq = _mm("up_q_dw", qn, dq_pre, ta=True)
    dkv_b = dkv.astype(BF16)
    dkvn = _mm("up_kv_dx", dkv_b, w_ukv, tb=True)
    g_w_ukv = _mm("up_kv_dw", kvn, dkv_b, ta=True)
    dqkv, g_q_norm, g_kv_norm = _tiled(
        "qkv_norm_bwd", _f_qkv_bwd, nt, qkv_in + [rowspec(dqn), rowspec(dkvn), rowspec(dkr)],
        [rowout(qkv.shape[1], BF16), accout(w["q_norm_g"]), accout(w["kv_norm_g"])])

    lru_out = [((s_len, d_rnn), BF16, (s_len, ct), "col")]
    for a in (conv_w, conv_b, wa_bd, w["b_gate_a"], wx_bd, w["b_gate_x"], w["lru_param"]):
        lru_out.append((a.shape, F32, (a.shape[0], ct), "col"))
    dx_rnn, g_conv_w, g_conv_b, g_wa_bd, g_b_a, g_wx_bd, g_b_x, g_lru = _tiled(
        "lru_bwd", _f_lru_bwd, n_ct, lru_in + [colspec(h_rnn), colspec(dy_rnn)], lru_out)

    dh1 = _mm("in_gates_dx", d_gates, w_g, tb=True)
    dh1 = _mm("in_qkv_dx", dqkv, w_qkv, tb=True, add=dh1)
    dh1 = _mm("in_rnn_dx", dx_rnn, w_rnn, tb=True, add=dh1)
    g_w_rnn = _mm("in_rnn_dw", h1, dx_rnn, ta=True)
    g_w_qkv = _mm("in_qkv_dw", h1, dqkv, ta=True)
    g_w_g = _mm("in_gates_dw", h1, d_gates, ta=True)

    ln_bwd = _vjp_of(_f_ln, 4, (0, 1, 2, 3))

    def ln1_bwd(xv, gv, sc, sh, dxr, dh):
        dx, dg, dsc, dsh = ln_bwd(xv, gv, sc, sh, dh)
        return dx + dxr, dg, dsc, dsh

    grad_x, g_norm1, d_scale1, d_shift1 = _tiled(
        "ln1_bwd", ln1_bwd, nt, ln1_in + [rowspec(dx_res), rowspec(dh1)],
        [rowout(d, F32), accout(norm1_g), accout(scale1), accout(shift1)])

    dmod = jnp.concatenate([d_shift1, d_scale1, d_gate1, d_shift2, d_scale2, d_gate2], axis=1)
    dmod_all = _all_gather("gather_dmod", dmod, ALL7).reshape(8, -1)
    dmod_loc = lax.dynamic_slice_in_dim(dmod_all, chip * n_mod, n_mod, axis=1)
    g_w_ada = _mm("ada_dw", c_act, jnp.pad(dmod_loc, ((0, c_rows - 8), (0, 0))), ta=True)

    g_full = {
        "w_in": jnp.concatenate([g_w_rnn, g_w_qkv[:, :n_q + n_kv],
                                 g_w_qkv[:, n_q + n_kv + QK_NOPE:n_q + n_kv + QK_NOPE + QK_ROPE], g_w_g], axis=1),
        "w_uq": g_w_uq.reshape(n_q, N_HEADS, HEAD_PAD)[:, :, :hd].reshape(n_q, -1),
        "w_ukv": g_w_ukv,
        "w_proj_rnn": g_w_pr,
        "w_proj_mla": g_w_pm.reshape(N_HEADS, HEAD_PAD, d)[:, HEAD_PAD - v_head:, :].reshape(-1, d),
        "w_out": g_w_out,
        "w_up": _deinterleave(g_w_up),
        "w_down": g_w_down,
        "conv_w": g_conv_w,
        "ffn_conv_w": _deinterleave(g_ffn_cw),
    }
    g_small = {
        "b_ada": dmod, "norm1_g": g_norm1, "conv_b": g_conv_b,
        "w_gate_a": _block_diag_pairs_t(g_wa_bd)[None], "b_gate_a": g_b_a,
        "w_gate_x": _block_diag_pairs_t(g_wx_bd)[None], "b_gate_x": g_b_x, "lru_param": g_lru,
        "q_norm_g": g_q_norm, "kv_norm_g": g_kv_norm, "norm2_g": g_norm2,
        "ffn_conv_b": _deinterleave(g_ffn_cb), "final_g": d_final_g.reshape(w["final_g"].shape),
    }

    chunks = []
    for k, kind in BIG:
        r, cc = local2d[k].shape
        gk = g_full[k]
        if kind == "col":
            gk = gk.reshape(r, 4, cc).transpose(1, 0, 2)
        chunks.append(gk.reshape(4, 2, r // 2, cc))
    small_flat = jnp.concatenate([g_small[k].reshape(-1) for k in SMALL] + [g_full[k].reshape(-1) for k, _ in CONVS])
    small_rows = -(-small_flat.shape[0] // (8 * PACK_COLS * PACK_ROW_UNIT)) * PACK_ROW_UNIT
    chunks.append(_pad_rows(small_flat[None], 8 * small_rows).reshape(4, 2, small_rows, PACK_COLS))
    pair_sums = []
    for k, (ck, from_sib) in enumerate(zip(chunks, _pair_exchange(chunks))):
        ours = lax.dynamic_index_in_dim(ck, ci, 1, keepdims=False)
        pair_sums.append(_reduce_pair("reduce_pair_%d" % k, ours, from_sib))
    reduced = []
    for k, (quad, ps) in enumerate(zip(_chips_alltoall(pair_sums), pair_sums)):
        quad = lax.dynamic_update_index_in_dim(quad, lax.dynamic_index_in_dim(ps, chip, 0, keepdims=True), chip, 0)
        reduced.append(_reduce_quad("reduce_quad_%d" % k, quad))
    grads = {}
    for (k, _), both, own in zip(BIG, _share_sibling(reduced[:-1]), reduced):
        grads[k] = lax.dynamic_update_index_in_dim(both, own[None], ci, 0).reshape(w[k].shape)
    small_grad = _all_gather("share_small", reduced[-1], ALL7).reshape(-1)
    off = 0
    for k in SMALL:
        grads[k] = small_grad[off:off + w[k].size].reshape(w[k].shape)
        off += w[k].size
    for k, _ in CONVS:
        r, cc = local2d[k].shape
        whole = small_grad[off:off + 4 * r * cc].reshape(r, 4 * cc)
        grads[k] = lax.dynamic_slice_in_dim(whole, chip * cc, cc, axis=1)[None]
        off += 4 * r * cc
    grads["w_ada"] = g_w_ada[None]

    delta, new_m, new_v = {}, {}, {}
    for k in WEIGHTS:
        shp = w[k].shape
        two_d = (-1, shp[-1]) if len(shp) > 1 else (1, -1)
        dk, mk, vk = _adamw("adamw_" + k, w[k].reshape(two_d), grads[k].reshape(two_d),
                            m_in[k].reshape(two_d), v_in[k].reshape(two_d))
        delta[k], new_m[k], new_v[k] = dk.reshape(shp), mk.reshape(shp), vk.reshape(shp)

    return (loss, grad_x[None], *[grads[k] for k in WEIGHTS], *[delta[k] for k in WEIGHTS],
            *[new_m[k] for k in WEIGHTS], *[new_v[k] for k in WEIGHTS])


def kernel(x, c, positions, w_ada, b_ada, norm1_g, w_in, conv_w, conv_b, w_gate_a, b_gate_a, w_gate_x, b_gate_x, lru_param, q_norm_g, w_uq, kv_norm_g, w_ukv, w_proj_rnn, w_proj_mla, w_out, norm2_g, w_up, ffn_conv_w, ffn_conv_b, w_down, final_g, loss_target, m_w_ada, m_b_ada, m_norm1_g, m_w_in, m_conv_w, m_conv_b, m_w_gate_a, m_b_gate_a, m_w_gate_x, m_b_gate_x, m_lru_param, m_q_norm_g, m_w_uq, m_kv_norm_g, m_w_ukv, m_w_proj_rnn, m_w_proj_mla, m_w_out, m_norm2_g, m_w_up, m_ffn_conv_w, m_ffn_conv_b, m_w_down, m_final_g, v_w_ada, v_b_ada, v_norm1_g, v_w_in, v_conv_w, v_conv_b, v_w_gate_a, v_b_gate_a, v_w_gate_x, v_b_gate_x, v_lru_param, v_q_norm_g, v_w_uq, v_kv_norm_g, v_w_ukv, v_w_proj_rnn, v_w_proj_mla, v_w_out, v_norm2_g, v_w_up, v_ffn_conv_w, v_ffn_conv_b, v_w_down, v_final_g):
    given = dict(locals())
    w = {k: given[k] for k in WEIGHTS}
    m_in = {k: given["m_" + k] for k in WEIGHTS}
    v_in = {k: given["v_" + k] for k in WEIGHTS}
    return _step(x, c, positions, w, m_in, v_in, loss_target)
```

```python
import functools
import math

import jax
import jax.numpy as jnp
from jax import lax
from jax.experimental import pallas as pl
from jax.experimental.pallas import tpu as pltpu

F32 = jnp.float32
BF16 = jnp.bfloat16

EPS = 1e-6
LRU_C = 8.0
N_HEADS = 16
QK_NOPE = 64
QK_ROPE = 32
HEAD_PAD = 128
ROPE_THETA = 10000.0
ADAM_LR = 0.001
ADAM_B1 = 0.9
ADAM_B2 = 0.999
ADAM_EPS = 1e-08
ADAM_WD = 0.01
ADAM_STEP = 10

LANE = 128
VMEM_LIMIT = 48 * 1024 * 1024
MM_TILE_M = MM_TILE_N = MM_TILE_K = 1408
PACK_COLS = 1024
PACK_ROW_UNIT = 32
MESH = pl.DeviceIdType.MESH

NN = (((1,), (0,)), ((), ()))
NT = (((1,), (1,)), ((), ()))


def _cparams(sem):
    return pltpu.CompilerParams(dimension_semantics=sem, vmem_limit_bytes=VMEM_LIMIT)


def _div_tile(n, cap, unit):
    best = None
    d = unit
    while d <= min(n, cap):
        if n % d == 0:
            best = d
        d += unit
    return n if best is None else best


def _mm(name, a, b, *, ta=False, tb=False, add=None, out_dtype=F32):
    if ta:
        kdim, m = a.shape
    else:
        m, kdim = a.shape
    if tb:
        n, kb = b.shape
    else:
        kb, n = b.shape
    assert kdim == kb, (name, a.shape, b.shape)
    tm = _div_tile(m, MM_TILE_M, 8 if not ta else LANE)
    tn = _div_tile(n, MM_TILE_N, LANE)
    tk = _div_tile(kdim, MM_TILE_K, LANE)
    nk = kdim // tk
    a_spec = pl.BlockSpec((tk, tm), lambda i, j, k: (k, i)) if ta else pl.BlockSpec((tm, tk), lambda i, j, k: (i, k))
    b_spec = pl.BlockSpec((tn, tk), lambda i, j, k: (j, k)) if tb else pl.BlockSpec((tk, tn), lambda i, j, k: (k, j))
    o_spec = pl.BlockSpec((tm, tn), lambda i, j, k: (i, j))
    has_add = add is not None
    dims = ((((0,) if ta else (1,)), ((1,) if tb else (0,))), ((), ()))

    def body(*refs):
        a_ref, b_ref = refs[0], refs[1]
        c_ref = refs[2] if has_add else None
        o_ref = refs[3] if has_add else refs[2]
        prod = lax.dot_general(a_ref[...].astype(BF16), b_ref[...].astype(BF16), dims, preferred_element_type=F32)
        if nk == 1:
            o_ref[...] = (prod + c_ref[...].astype(F32) if has_add else prod).astype(o_ref.dtype)
            return
        acc = refs[-1]
        k = pl.program_id(2)

        @pl.when(k == 0)
        def _():
            acc[...] = prod + c_ref[...].astype(F32) if has_add else prod

        @pl.when(jnp.logical_and(k > 0, k < nk - 1))
        def _():
            acc[...] += prod

        @pl.when(k == nk - 1)
        def _():
            o_ref[...] = (acc[...] + prod).astype(o_ref.dtype)

    ins = [a, b] + ([add] if has_add else [])
    specs = [a_spec, b_spec] + ([o_spec] if has_add else [])
    return pl.pallas_call(
        body, name=name, grid=(m // tm, n // tn, nk), in_specs=specs, out_specs=o_spec,
        out_shape=jax.ShapeDtypeStruct((m, n), out_dtype),
        scratch_shapes=[pltpu.VMEM((tm, tn), F32)] if nk > 1 else [],
        compiler_params=_cparams(("parallel", "parallel", "arbitrary")),
    )(*ins)


_IMAPS = {
    "row": lambda i: (i, 0),
    "col": lambda i: (0, i),
    "full": lambda i: (0, 0),
    "acc": lambda i: (0, 0),
}


def _tiled(name, fn, n, ins, outs):
    ni = len(ins)
    is_acc = [k == "acc" for *_, k in outs]

    def body(*refs):
        vals = fn(*[r[...] for r in refs[:ni]])
        orefs = refs[ni:]
        if any(is_acc):
            @pl.when(pl.program_id(0) == 0)
            def _():
                for r, a in zip(orefs, is_acc):
                    if a:
                        r[...] = jnp.zeros(r.shape, r.dtype)
        for r, v, a in zip(orefs, vals, is_acc):
            if a:
                r[...] += v.astype(r.dtype)
            else:
                r[...] = v.astype(r.dtype)

    res = pl.pallas_call(
        body, name=name, grid=(n,),
        in_specs=[pl.BlockSpec(bs, _IMAPS[k]) for _, bs, k in ins],
        out_specs=[pl.BlockSpec(bs, _IMAPS[k]) for _, _, bs, k in outs],
        out_shape=[jax.ShapeDtypeStruct(s, d) for s, d, _, _ in outs],
        compiler_params=_cparams(("arbitrary",)),
    )(*[a for a, _, _ in ins])
    return tuple(res)


def _vjp_of(fn, nin, diff):
    def g(*args):
        ins, cots = args[:nin], args[nin:]

        def f(*d):
            full = list(ins)
            for i, v in zip(diff, d):
                full[i] = v
            return fn(*full)

        outs, vjp = jax.vjp(f, *[ins[i] for i in diff])
        return vjp(tuple(c.astype(o.dtype) for c, o in zip(cots, outs)))
    return g


def _shift_rows(x, k, fill, up=False):
    n = x.shape[0]
    rows = lax.broadcasted_iota(jnp.int32, x.shape, 0)
    if up:
        return jnp.where(rows < n - k, pltpu.roll(x, n - k, 0), fill)
    return jnp.where(rows >= k, pltpu.roll(x, k, 0), fill)


@functools.partial(jax.custom_vjp, nondiff_argnums=(1,))
def _delay(x, k):
    return _shift_rows(x, k, 0.0)


def _delay_fwd(x, k):
    return _shift_rows(x, k, 0.0), None


def _delay_bwd(k, _, g):
    return (_shift_rows(g, k, 0.0, up=True),)


_delay.defvjp(_delay_fwd, _delay_bwd)


@functools.partial(jax.custom_vjp, nondiff_argnums=(1,))
def _lane_roll(x, s):
    return pltpu.roll(x, s, 1)


def _lane_roll_fwd(x, s):
    return pltpu.roll(x, s, 1), None


def _lane_roll_bwd(s, _, g):
    return (pltpu.roll(g, g.shape[1] - s, 1),)


_lane_roll.defvjp(_lane_roll_fwd, _lane_roll_bwd)


@jax.custom_vjp
def _bdot(x, w):
    return lax.dot_general(x.astype(BF16), w.astype(BF16), NN, preferred_element_type=F32)


def _bdot_fwd(x, w):
    return _bdot(x, w), (x, w)


def _bdot_bwd(res, g):
    x, w = res
    gb = g.astype(BF16)
    dx = lax.dot_general(gb, w.astype(BF16), NT, preferred_element_type=F32)
    dw = lax.dot_general(x.T.astype(BF16), gb, NN, preferred_element_type=F32)
    return dx, dw


_bdot.defvjp(_bdot_fwd, _bdot_bwd)


def _sigmoid(x):
    return 0.5 * (jnp.tanh(0.5 * x) + 1.0)


def _silu(x):
    return x * _sigmoid(x)


def _rms(x, g):
    return x * lax.rsqrt(jnp.mean(x * x, axis=-1, keepdims=True) + EPS) * g


def _causal_conv(x, w, b):
    kw = w.shape[0]
    tap = lax.broadcasted_iota(jnp.int32, w.shape, 0)
    y = b
    for k in range(kw):
        d = kw - 1 - k
        wk = jnp.sum(jnp.where(tap == k, w, 0.0), axis=0, keepdims=True)
        y = y + wk * (x if d == 0 else _delay(x, d))
    return y


def _rotate(x, cos_f, sin_a, sin_b):
    reps = x.shape[1] // LANE
    if reps > 1:
        cos_f, sin_a, sin_b = (jnp.tile(t, (1, reps)) for t in (cos_f, sin_a, sin_b))
    n = x.shape[1]
    half = QK_ROPE // 2
    return x * cos_f + _lane_roll(x, n - half) * sin_a + _lane_roll(x, half) * sin_b


def _softplus_neg(l):
    u = jnp.exp(-jnp.abs(l))
    log1p_u = jnp.where(u < 0.01, u * (1.0 - u * (0.5 - u * (1.0 / 3.0))), jnp.log(1.0 + u))
    return jnp.maximum(-l, 0.0) + log1p_u


def _f_ln(x, g, scale, shift):
    return (_rms(x, g) * (1.0 + scale) + shift,)


def _f_qkv(qkv, cos_f, sin_a, sin_b, qg, kvg):
    nq, nkv = qg.shape[1], kvg.shape[1]
    qn = _rms(qkv[:, :nq], qg)
    kvn = _rms(qkv[:, nq:nq + nkv], kvg)
    kr = _rotate(qkv[:, nq + nkv:], cos_f, sin_a, sin_b)
    return qn, kvn, kr


def _f_qkv_bwd(qkv, cos_f, sin_a, sin_b, qg, kvg, dqn, dkvn, dkr):
    nq, nkv = qg.shape[1], kvg.shape[1]
    _, vjp_q = jax.vjp(_rms, qkv[:, :nq], qg)
    _, vjp_kv = jax.vjp(_rms, qkv[:, nq:nq + nkv], kvg)
    _, vjp_r = jax.vjp(lambda t: _rotate(t, cos_f, sin_a, sin_b), qkv[:, nq + nkv:])
    dq_lat, dqg = vjp_q(dqn)
    dkv_lat, dkvg = vjp_kv(dkvn)
    (dkr_pre,) = vjp_r(dkr)
    return jnp.concatenate([dq_lat, dkv_lat, dkr_pre], axis=1), dqg, dkvg


def _f_rotq(q, cos_f, sin_a, sin_b):
    return (_rotate(q, cos_f, sin_a, sin_b),)


def _merge(g_rnn, g_mla, p_rnn, p_mla):
    return _sigmoid(g_rnn) * p_rnn + _sigmoid(g_mla) * p_mla


def _f_merge(g, p_rnn, p_mla):
    d = p_rnn.shape[1]
    return (_merge(g[:, :d], g[:, d:], p_rnn, p_mla),)


def _f_merge_bwd(g, p_rnn, p_mla, dm):
    d = p_rnn.shape[1]
    _, vjp = jax.vjp(_merge, g[:, :d], g[:, d:], p_rnn, p_mla)
    dg_rnn, dg_mla, dp_rnn, dp_mla = vjp(dm)
    return jnp.concatenate([dg_rnn, dg_mla], axis=1), dp_rnn, dp_mla


def _f_res_ln(x, o, gate, g2, scale, shift):
    x1 = x + gate * o
    return x1, _rms(x1, g2) * (1.0 + scale) + shift


def _ffn(u_gate, u_val, cw_gate, cw_val, cb_gate, cb_val):
    return _silu(_causal_conv(u_gate, cw_gate, cb_gate)) * _causal_conv(u_val, cw_val, cb_val)


def _halves(*arrays):
    out = []
    for a in arrays:
        half = a.shape[1] // 2
        out += [a[:, :half], a[:, half:]]
    return out


def _f_ffn(u, cw, cb):
    ug, uv, cwg, cwv, cbg, cbv = _halves(u, cw, cb)
    return (_ffn(ug, uv, cwg, cwv, cbg, cbv),)


def _f_ffn_bwd(u, cw, cb, dact):
    _, vjp = jax.vjp(_ffn, *_halves(u, cw, cb))
    dug, duv, dcwg, dcwv, dcbg, dcbv = vjp(dact)
    cat = lambda a, b: jnp.concatenate([a, b], axis=1)
    return cat(dug, duv), cat(dcwg, dcwv), cat(dcbg, dcbv)


def _f_loss(x1, f, tgt, gate, fg):
    y = _rms(x1 + gate * f, fg)
    err = (y - tgt) * (y - tgt)
    return 0.5 * jnp.sum(jnp.mean(err, axis=-1, keepdims=True), axis=0, keepdims=True)


def _f_loss_and_grads(x1, f, tgt, gate, fg):
    loss, vjp = jax.vjp(lambda a, b, c, d: _f_loss(a, b, tgt, c, d), x1, f, gate, fg)
    dx1, df, dgate, dfg = vjp(jnp.ones((1, 1), F32))
    return dx1, df, jnp.broadcast_to(loss, (1, LANE)), dgate, dfg


def _f_lru_coeffs(xr, cw, cb, wa, ba, wx, bx, lru, reset):
    xc = _causal_conv(xr, cw, cb)
    r = _sigmoid(_bdot(xc, wa) + ba)
    i = _sigmoid(_bdot(xc, wx) + bx)
    log_a = (-LRU_C) * r * _softplus_neg(lru)
    a = jnp.exp(log_a)
    mult = jnp.sqrt(-jnp.tanh(log_a) * (1.0 + a * a))
    is_reset = reset > 0.5
    a = jnp.where(is_reset, 0.0, a)
    mult = jnp.where(is_reset, 1.0, mult)
    return a, mult * (i * xc)


def _scan(a, b, up=False):
    n = a.shape[0]
    k = 1
    while k < n:
        b = b + a * _shift_rows(b, k, 0.0, up)
        if 2 * k < n:
            a = a * _shift_rows(a, k, 1.0, up)
        k *= 2
    return b


def _f_lru_fwd(xr, cw, cb, wa, ba, wx, bx, lru, reset):
    a, b = _f_lru_coeffs(xr, cw, cb, wa, ba, wx, bx, lru, reset)
    h = _scan(a, b)
    return h, h


def _f_lru_bwd(xr, cw, cb, wa, ba, wx, bx, lru, reset, h, dh):
    (a, _), vjp = jax.vjp(lambda *p: _f_lru_coeffs(*p, reset), xr, cw, cb, wa, ba, wx, bx, lru)
    g = _scan(_shift_rows(a, 1, 0.0, up=True), dh, up=True)
    return vjp((g * _shift_rows(h, 1, 0.0), g))


def _attn_tile(s):
    return 1024 if s >= 2048 else s // 2


def _keys(kv, kr):
    lane = lax.broadcasted_iota(jnp.int32, kv.shape, 1)
    return jnp.where(lane < QK_NOPE, kv, kr)


ATTN_HEADS_PER_STEP = 2


def _scores(q, kc, scale, diagonal):
    s = lax.dot_general(q, kc, NT, preferred_element_type=F32) * scale
    if not diagonal:
        return s
    rows = lax.broadcasted_iota(jnp.int32, s.shape, 0)
    cols = lax.broadcasted_iota(jnp.int32, s.shape, 1)
    return jnp.where(cols <= rows, s, -jnp.inf)


def _causal_pairs(nb, k_major):
    if k_major:
        pairs = [(qb, kb) for kb in range(nb) for qb in range(kb, nb)]
    else:
        pairs = [(qb, kb) for qb in range(nb) for kb in range(qb + 1)]
    return jnp.array([p[0] for p in pairs], jnp.int32), jnp.array([p[1] for p in pairs], jnp.int32)


def _attn_fwd(q, kv, kr):
    s_len = q.shape[0]
    t = _attn_tile(s_len)
    nb = s_len // t
    hp = ATTN_HEADS_PER_STEP
    wide = hp * HEAD_PAD
    scale = 1.0 / math.sqrt(QK_NOPE + QK_ROPE)
    q_tab, k_tab = _causal_pairs(nb, k_major=False)

    def body(qt, kt, q_ref, kv_ref, kr_ref, o_ref, lse_ref, m_s, l_s, acc_s):
        pair = pl.program_id(1)
        qi, ki = qt[pair], kt[pair]

        @pl.when(ki == 0)
        def _():
            m_s[...] = jnp.full(m_s.shape, -jnp.inf, F32)
            l_s[...] = jnp.zeros(l_s.shape, F32)
            acc_s[...] = jnp.zeros(acc_s.shape, F32)

        def step(diagonal):
            krv = kr_ref[...]
            for h in range(hp):
                lanes = slice(h * HEAD_PAD, (h + 1) * HEAD_PAD)
                kvv = kv_ref[:, lanes]
                s = _scores(q_ref[:, lanes], _keys(kvv, krv), scale, diagonal)
                m_old = m_s[h]
                m_new = jnp.maximum(m_old, jnp.max(s, axis=-1, keepdims=True))
                alpha = jnp.exp(m_old - m_new)
                p = jnp.exp(s - m_new)
                l_s[h] = alpha * l_s[h] + jnp.sum(p, axis=-1, keepdims=True)
                acc_s[:, lanes] = alpha * acc_s[:, lanes] + lax.dot_general(
                    p.astype(BF16), kvv, NN, preferred_element_type=F32)
                m_s[h] = m_new

        @pl.when(ki < qi)
        def _():
            step(False)

        @pl.when(ki == qi)
        def _():
            step(True)
            lane = lax.broadcasted_iota(jnp.int32, (t, HEAD_PAD), 1)
            for h in range(hp):
                lanes = slice(h * HEAD_PAD, (h + 1) * HEAD_PAD)
                o_ref[:, lanes] = jnp.where(lane >= QK_NOPE, acc_s[:, lanes] / l_s[h], 0.0).astype(o_ref.dtype)
                lse_ref[h] = m_s[h] + jnp.log(l_s[h])

    grid_spec = pltpu.PrefetchScalarGridSpec(
        num_scalar_prefetch=2, grid=(N_HEADS // hp, q_tab.shape[0]),
        in_specs=[pl.BlockSpec((t, wide), lambda h, p, qt, kt: (qt[p], h)),
                  pl.BlockSpec((t, wide), lambda h, p, qt, kt: (kt[p], h)),
                  pl.BlockSpec((t, HEAD_PAD), lambda h, p, qt, kt: (kt[p], 0))],
        out_specs=[pl.BlockSpec((t, wide), lambda h, p, qt, kt: (qt[p], h)),
                   pl.BlockSpec((hp, t, 1), lambda h, p, qt, kt: (h, qt[p], 0))],
        scratch_shapes=[pltpu.VMEM((hp, t, 1), F32), pltpu.VMEM((hp, t, 1), F32), pltpu.VMEM((t, wide), F32)])
    return pl.pallas_call(
        body, name="attn_fwd", grid_spec=grid_spec,
        out_shape=[jax.ShapeDtypeStruct((s_len, N_HEADS * HEAD_PAD), BF16),
                   jax.ShapeDtypeStruct((N_HEADS, s_len, 1), F32)],
        compiler_params=_cparams(("arbitrary", "arbitrary")),
    )(q_tab, k_tab, q, kv, kr)


def _attn_bwd(q, kv, kr, o, lse, do):
    s_len = q.shape[0]
    t = _attn_tile(s_len)
    nb = s_len // t
    hp = ATTN_HEADS_PER_STEP
    wide = hp * HEAD_PAD
    scale = 1.0 / math.sqrt(QK_NOPE + QK_ROPE)
    q_tab, k_tab = _causal_pairs(nb, k_major=True)

    def body(qt, kt, q_ref, kv_ref, kr_ref, o_ref, lse_ref, do_ref, dq_ref, dkv_ref, dkr_ref, dk_s, dv_s):
        g, pair = pl.program_id(0), pl.program_id(1)
        qb, kb = qt[pair], kt[pair]

        @pl.when(jnp.logical_and(g == 0, pair == 0))
        def _():
            dkr_ref[...] = jnp.zeros(dkr_ref.shape, F32)

        @pl.when(pair == 0)
        def _():
            dq_ref[...] = jnp.zeros(dq_ref.shape, F32)

        @pl.when(qb == kb)
        def _():
            dk_s[...] = jnp.zeros(dk_s.shape, F32)
            dv_s[...] = jnp.zeros(dv_s.shape, F32)

        def step(diagonal):
            krv = kr_ref[...]
            rows = pl.ds(pl.multiple_of(qb * t, t), t)
            for h in range(hp):
                lanes = slice(h * HEAD_PAD, (h + 1) * HEAD_PAD)
                qv, kvv, dov = q_ref[:, lanes], kv_ref[:, lanes], do_ref[:, lanes]
                kc = _keys(kvv, krv)
                p = jnp.exp(_scores(qv, kc, scale, diagonal) - lse_ref[h])
                delta = jnp.sum(dov.astype(F32) * o_ref[:, lanes].astype(F32), axis=-1, keepdims=True)
                dp = lax.dot_general(dov, kvv, NT, preferred_element_type=F32)
                ds = p * (dp - delta) * scale
                dv_s[:, lanes] += lax.dot_general(p.T.astype(BF16), dov, NN, preferred_element_type=F32)
                dk_s[:, lanes] += lax.dot_general(ds.T.astype(BF16), qv, NN, preferred_element_type=F32)
                dq_ref[rows, lanes] += lax.dot_general(ds.astype(BF16), kc, NN, preferred_element_type=F32)

        @pl.when(qb > kb)
        def _():
            step(False)

        @pl.when(qb == kb)
        def _():
            step(True)

        @pl.when(qb == nb - 1)
        def _():
            lane = lax.broadcasted_iota(jnp.int32, (t, HEAD_PAD), 1)
            rows = pl.ds(pl.multiple_of(kb * t, t), t)
            for h in range(hp):
                lanes = slice(h * HEAD_PAD, (h + 1) * HEAD_PAD)
                dkv_ref[:, lanes] = jnp.where(lane < QK_NOPE, dk_s[:, lanes], dv_s[:, lanes])
                dkr_ref[rows, :] += jnp.where(lane >= QK_NOPE, dk_s[:, lanes], 0.0)

    all_lanes = N_HEADS * HEAD_PAD
    qmap = lambda h, p, qt, kt: (qt[p], h)
    kmap = lambda h, p, qt, kt: (kt[p], h)
    grid_spec = pltpu.PrefetchScalarGridSpec(
        num_scalar_prefetch=2, grid=(N_HEADS // hp, q_tab.shape[0]),
        in_specs=[pl.BlockSpec((t, wide), qmap),
                  pl.BlockSpec((t, wide), kmap),
                  pl.BlockSpec((t, HEAD_PAD), lambda h, p, qt, kt: (kt[p], 0)),
                  pl.BlockSpec((t, wide), qmap),
                  pl.BlockSpec((hp, t, 1), lambda h, p, qt, kt: (h, qt[p], 0)),
                  pl.BlockSpec((t, wide), qmap)],
        out_specs=[pl.BlockSpec((s_len, wide), lambda h, p, qt, kt: (0, h)),
                   pl.BlockSpec((t, wide), kmap),
                   pl.BlockSpec((s_len, HEAD_PAD), lambda h, p, qt, kt: (0, 0))],
        scratch_shapes=[pltpu.VMEM((t, wide), F32), pltpu.VMEM((t, wide), F32)])
    return pl.pallas_call(
        body, name="attn_bwd", grid_spec=grid_spec,
        out_shape=[jax.ShapeDtypeStruct((s_len, all_lanes), F32),
                   jax.ShapeDtypeStruct((s_len, all_lanes), F32),
                   jax.ShapeDtypeStruct((s_len, HEAD_PAD), F32)],
        compiler_params=_cparams(("arbitrary", "arbitrary")),
    )(q_tab, k_tab, q, kv, kr, o, lse, do)


def _adamw(name, w, g, m, v):
    rows, cols = w.shape
    tr = _div_tile(rows, max(8, (2 * 1024 * 1024) // (4 * cols)), 8)

    def body(w_ref, g_ref, m_ref, v_ref, d_ref, nm_ref, nv_ref):
        gv = g_ref[...]
        nm = ADAM_B1 * m_ref[...] + (1.0 - ADAM_B1) * gv
        nv = ADAM_B2 * v_ref[...] + (1.0 - ADAM_B2) * jnp.square(gv)
        m_hat = nm / (1.0 - ADAM_B1 ** ADAM_STEP)
        v_hat = nv / (1.0 - ADAM_B2 ** ADAM_STEP)
        d_ref[...] = -ADAM_LR * (m_hat / (jnp.sqrt(v_hat) + ADAM_EPS) + ADAM_WD * w_ref[...])
        nm_ref[...] = nm
        nv_ref[...] = nv

    spec = pl.BlockSpec((tr, cols), lambda i: (i, 0))
    return pl.pallas_call(
        body, name=name, grid=(rows // tr,), in_specs=[spec] * 4, out_specs=[spec] * 3,
        out_shape=[jax.ShapeDtypeStruct((rows, cols), F32)] * 3,
        compiler_params=_cparams(("parallel",)),
    )(w, g, m, v)


ALL7 = (1, 2, 3, 4, 5, 6, 7)
CHIPS = (2, 4, 6)


def _all_gather(name, src, masks):
    bits = 0
    for m in masks:
        bits |= m
    nslots = {7: 8, 6: 4}[bits]
    nm = len(masks)

    def slot_of(x, y, c):
        return {7: 4 * x + 2 * y + c, 6: 2 * x + y}[bits]

    def body(src_ref, out_ref, send_sems, recv_sems, local_sem):
        x, y, c = lax.axis_index("x"), lax.axis_index("y"), lax.axis_index("c")
        mine = slot_of(x, y, c)
        own = pltpu.make_async_copy(src_ref, out_ref.at[mine], local_sem)
        own.start()
        copies = []
        for i, m in enumerate(masks):
            peer = _peer(x, y, c, m)
            copies.append((
                pltpu.make_async_remote_copy(
                    src_ref=src_ref, dst_ref=out_ref.at[mine], send_sem=send_sems.at[i], recv_sem=recv_sems.at[i],
                    device_id=peer, device_id_type=MESH),
                pltpu.make_async_remote_copy(
                    src_ref=src_ref, dst_ref=out_ref.at[slot_of(*peer)], send_sem=send_sems.at[i],
                    recv_sem=recv_sems.at[i], device_id=peer, device_id_type=MESH)))
        for send, _ in copies:
            send.start()
        for _, arrival in copies:
            arrival.wait_recv()
        for send, _ in copies:
            send.wait_send()
        own.wait()

    return pl.pallas_call(
        body, name=name,
        in_specs=[pl.BlockSpec(memory_space=pl.ANY)], out_specs=pl.BlockSpec(memory_space=pl.ANY),
        out_shape=jax.ShapeDtypeStruct((nslots,) + tuple(src.shape), src.dtype),
        scratch_shapes=[pltpu.SemaphoreType.DMA((nm,)), pltpu.SemaphoreType.DMA((nm,)), pltpu.SemaphoreType.DMA],
    )(src)


def _peer(x, y, c, m):
    return (1 - x if m & 4 else x, 1 - y if m & 2 else y, 1 - c if m & 1 else c)


def _comm_call(name, emit, srcs, out_shapes, n_sems):
    n = len(srcs)

    def body(*refs):
        src_refs, out_refs = refs[:n], refs[n:n + len(out_shapes)]
        send_sems, recv_sems = refs[-2], refs[-1]

        def copy(src, dst, i, peer):
            return pltpu.make_async_remote_copy(src_ref=src, dst_ref=dst, send_sem=send_sems.at[i],
                                                recv_sem=recv_sems.at[i], device_id=peer, device_id_type=MESH)

        emit(lax.axis_index("x"), lax.axis_index("y"), lax.axis_index("c"), src_refs, out_refs, copy)

    hbm = pl.BlockSpec(memory_space=pl.ANY)
    return pl.pallas_call(
        body, name=name, in_specs=[hbm] * n, out_specs=[hbm] * len(out_shapes), out_shape=out_shapes,
        scratch_shapes=[pltpu.SemaphoreType.DMA((n_sems,)), pltpu.SemaphoreType.DMA((n_sems,))],
    )(*srcs)


def _gather_weights(halves):
    n = len(halves)

    def emit(x, y, c, srcs, outs, copy):
        chip = 2 * x + y
        sib = (x, y, 1 - c)
        first, relay, landed, relayed = [], [], [], []
        for j, m in enumerate(CHIPS):
            px, py, _ = _peer(x, y, c, m)
            theirs = 2 * px + py
            for k in range(n):
                i = 6 * k + j
                first.append(copy(srcs[k].at[c], outs[k].at[chip, c], i, (px, py, c)))
                landed.append(copy(srcs[k].at[c], outs[k].at[theirs, c], i, (px, py, c)))
                relay.append(copy(outs[k].at[theirs, c], outs[k].at[theirs, c], i + 3, sib))
                relayed.append(copy(outs[k].at[theirs, 1 - c], outs[k].at[theirs, 1 - c], i + 3, sib))
        for cp in first:
            cp.start()
        for arrival, onward in zip(landed, relay):
            arrival.wait_recv()
            onward.start()
        for arrival in relayed:
            arrival.wait_recv()
        for cp in first + relay:
            cp.wait_send()

    shapes = [jax.ShapeDtypeStruct((4,) + h.shape, h.dtype) for h in halves]
    return _comm_call("gather_weights", emit, halves, shapes, 6 * n)


def _pair_exchange(chunks):
    def emit(x, y, c, srcs, outs, copy):
        sib = (x, y, 1 - c)
        sends = [copy(s.at[:, 1 - c], o, k, sib) for k, (s, o) in enumerate(zip(srcs, outs))]
        for cp in sends:
            cp.start()
        for cp in sends:
            cp.wait_recv()
        for cp in sends:
            cp.wait_send()

    shapes = [jax.ShapeDtypeStruct((4,) + g.shape[2:], g.dtype) for g in chunks]
    return _comm_call("reduce_pair_exchange", emit, chunks, shapes, len(chunks))


def _chips_alltoall(parts):
    def emit(x, y, c, srcs, outs, copy):
        chip = 2 * x + y
        sends, arrivals = [], []
        for j, m in enumerate(CHIPS):
            px, py, _ = _peer(x, y, c, m)
            theirs = 2 * px + py
            for k, (s, o) in enumerate(zip(srcs, outs)):
                sends.append(copy(s.at[theirs], o.at[chip], 3 * k + j, (px, py, c)))
                arrivals.append(copy(s.at[theirs], o.at[theirs], 3 * k + j, (px, py, c)))
        for cp in sends:
            cp.start()
        for cp in arrivals:
            cp.wait_recv()
        for cp in sends:
            cp.wait_send()

    shapes = [jax.ShapeDtypeStruct(p.shape, p.dtype) for p in parts]
    return _comm_call("reduce_chips_exchange", emit, parts, shapes, 3 * len(parts))


def _share_sibling(parts):
    def emit(x, y, c, srcs, outs, copy):
        sib = (x, y, 1 - c)
        sends = [copy(s, o.at[c], k, sib) for k, (s, o) in enumerate(zip(srcs, outs))]
        arrivals = [copy(s, o.at[1 - c], k, sib) for k, (s, o) in enumerate(zip(srcs, outs))]
        for cp in sends:
            cp.start()
        for cp in arrivals:
            cp.wait_recv()
        for cp in sends:
            cp.wait_send()

    shapes = [jax.ShapeDtypeStruct((2,) + p.shape, p.dtype) for p in parts]
    return _comm_call("share_sibling", emit, parts, shapes, len(parts))


def _reduce_pair(name, a, b):
    rows = a.shape[0] * a.shape[1]
    cols = a.shape[2]
    rt = _div_tile(rows, max(16, (1 << 20) // (4 * cols)), 16)
    spec = (rt, cols)
    (out,) = _tiled(name, lambda u, v: (u + v,), rows // rt,
                    [(a.reshape(rows, cols), spec, "row"), (b.reshape(rows, cols), spec, "row")],
                    [((rows, cols), BF16, spec, "row")])
    return out.reshape(a.shape)


def _reduce_quad(name, q):
    _, h, cols = q.shape
    rt = _div_tile(h, max(16, (1 << 20) // (4 * cols)), 16)

    def body(q_ref, o_ref):
        v = q_ref[...].astype(F32)
        o_ref[...] = ((v[0] + v[1]) + v[2]) + v[3]

    return pl.pallas_call(
        body, name=name, grid=(h // rt,),
        in_specs=[pl.BlockSpec((4, rt, cols), lambda i: (0, i, 0))],
        out_specs=pl.BlockSpec((rt, cols), lambda i: (i, 0)),
        out_shape=jax.ShapeDtypeStruct((h, cols), F32),
        compiler_params=_cparams(("parallel",)),
    )(q)


def _unshard(seg, kind):
    n, r, c = seg.shape
    if kind == "col":
        return seg.transpose(1, 0, 2).reshape(r, n * c)
    return seg.reshape(n * r, c)


def _pad_rows(flat, rows):
    n, ln = flat.shape
    return jnp.pad(flat, ((0, 0), (0, rows * PACK_COLS - ln))).reshape(n, rows, PACK_COLS)


def _block_diag_pairs(w):
    n2, bs, _ = w.shape
    eye = jnp.eye(2, dtype=w.dtype)
    z = w.reshape(n2 // 2, 2, bs, 1, bs) * eye[None, :, None, :, None]
    return z.reshape(n2 // 2, 2 * bs, 2 * bs).transpose(1, 0, 2).reshape(2 * bs, n2 * bs)


def _block_diag_pairs_t(d, bs=64):
    n = d.shape[1] // (2 * bs)
    z = d.reshape(2 * bs, n, 2 * bs).transpose(1, 0, 2).reshape(n, 2, bs, 2, bs)
    return jnp.stack([z[:, 0, :, 0, :], z[:, 1, :, 1, :]], axis=1).reshape(2 * n, bs, bs)


def _interleave(w, half_tile=LANE):
    r, c = w.shape
    return w.reshape(r, 2, c // (2 * half_tile), half_tile).transpose(0, 2, 1, 3).reshape(r, c)


def _deinterleave(w, half_tile=LANE):
    r, c = w.shape
    return w.reshape(r, c // (2 * half_tile), 2, half_tile).transpose(0, 2, 1, 3).reshape(r, c)


BIG = (("w_in", "col"), ("w_uq", "col"), ("w_ukv", "col"), ("w_proj_rnn", "row"), ("w_proj_mla", "row"),
       ("w_out", "row"), ("w_up", "col"), ("w_down", "row"))
CONVS = (("conv_w", "col"), ("ffn_conv_w", "col"))
SMALL = ("b_ada", "norm1_g", "conv_b", "w_gate_a", "b_gate_a", "w_gate_x", "b_gate_x", "lru_param",
         "q_norm_g", "kv_norm_g", "norm2_g", "ffn_conv_b", "final_g")
WEIGHTS = ("w_ada", "b_ada", "norm1_g", "w_in", "conv_w", "conv_b", "w_gate_a", "b_gate_a", "w_gate_x",
           "b_gate_x", "lru_param", "q_norm_g", "w_uq", "kv_norm_g", "w_ukv", "w_proj_rnn", "w_proj_mla",
           "w_out", "norm2_g", "w_up", "ffn_conv_w", "ffn_conv_b", "w_down", "final_g")


def _step(x, c, positions, w, m_in, v_in, loss_target):
    s_len, d = x.shape[1], x.shape[2]
    x2d = x[0]
    tgt = loss_target[0]
    xi, yi, ci = lax.axis_index("x"), lax.axis_index("y"), lax.axis_index("c")
    chip = 2 * xi + yi
    me = 2 * chip + ci
    tile = min(256, s_len)
    nt = s_len // tile

    local2d = {k: w[k][0] for k, _ in BIG + CONVS}
    halves_bf = [local2d[k].astype(BF16).reshape(2, local2d[k].shape[0] // 2, local2d[k].shape[1]) for k, _ in BIG]
    full = {}
    for (k, kind), g, hb in zip(BIG, _gather_weights(halves_bf), halves_bf):
        g = lax.dynamic_update_index_in_dim(g, hb[None], chip, 0)
        full[k] = _unshard(g.reshape((4,) + local2d[k].shape), kind)
    conv_flat = jnp.concatenate([local2d[k].reshape(-1) for k, _ in CONVS])
    conv_rows = -(-conv_flat.shape[0] // PACK_COLS)
    conv_all = _all_gather("gather_conv_w", _pad_rows(conv_flat[None], conv_rows)[0], CHIPS)
    conv_all = conv_all.reshape(4, -1)
    off = 0
    for k, kind in CONVS:
        r, cc = local2d[k].shape
        full[k] = _unshard(conv_all[:, off:off + r * cc].reshape(4, r, cc), kind)
        off += r * cc

    d_rnn = w["conv_b"].shape[1]
    n_q, n_kv = w["q_norm_g"].shape[1], w["kv_norm_g"].shape[1]
    w_in = full["w_in"]
    o1, o2, o3 = d_rnn + n_q, d_rnn + n_q + n_kv, d_rnn + n_q + n_kv + QK_ROPE
    w_rnn = w_in[:, :d_rnn]
    zpad = lambda n: jnp.zeros((d, n), BF16)
    w_qkv = jnp.concatenate([w_in[:, d_rnn:o2], zpad(QK_NOPE), w_in[:, o2:o3], zpad(LANE - QK_NOPE - QK_ROPE)], axis=1)
    w_g = w_in[:, o3:]
    hd = QK_NOPE + QK_ROPE
    w_uq = jnp.pad(full["w_uq"].reshape(n_q, N_HEADS, hd), ((0, 0), (0, 0), (0, HEAD_PAD - hd))).reshape(n_q, -1)
    w_ukv = full["w_ukv"]
    w_pr = full["w_proj_rnn"]
    v_head = w_ukv.shape[1] // N_HEADS - QK_NOPE
    w_pm = jnp.pad(full["w_proj_mla"].reshape(N_HEADS, v_head, d), ((0, 0), (HEAD_PAD - v_head, 0), (0, 0))).reshape(-1, d)
    w_out = full["w_out"]
    w_up = _interleave(full["w_up"])
    w_down = full["w_down"]
    ffn_cw = _interleave(full["ffn_conv_w"])
    ffn_cb = _interleave(w["ffn_conv_b"])
    conv_w, conv_b = full["conv_w"], w["conv_b"]
    wa_bd = _block_diag_pairs(w["w_gate_a"][0])
    wx_bd = _block_diag_pairs(w["w_gate_x"][0])

    c_all = _all_gather("gather_c", c, ALL7).reshape(8, d)
    c_rows = 128
    (c_act,) = _tiled("silu_c", lambda v: (_silu(v),), 1, [(jnp.pad(c_all, ((0, c_rows - 8), (0, 0))), (c_rows, d), "full")],
                      [((c_rows, d), F32, (c_rows, d), "full")])
    w_ada = w["w_ada"][0]
    n_mod = w_ada.shape[1]
    b_loc = lax.dynamic_slice_in_dim(w["b_ada"], chip * n_mod, n_mod, axis=1)
    mod_loc = _mm("ada_fwd", c_act, w_ada, add=jnp.broadcast_to(b_loc, (c_rows, n_mod)))
    mod_all = _all_gather("gather_mod", mod_loc[:8], CHIPS)
    mod = lax.dynamic_index_in_dim(mod_all, me, 1, keepdims=False).reshape(1, -1)
    shift1, scale1, gate1, shift2, scale2, gate2 = [mod[:, i * d:(i + 1) * d] for i in range(6)]

    half = QK_ROPE // 2
    inv_freq = ROPE_THETA ** (-jnp.arange(half, dtype=F32) / half)
    ang = positions[0].astype(F32)[:, None] * inv_freq
    cos, sin = jnp.cos(ang), jnp.sin(ang)
    one, zero = jnp.ones((s_len, QK_NOPE), F32), jnp.zeros((s_len, half), F32)
    tail = jnp.zeros((s_len, LANE - QK_NOPE - QK_ROPE), F32)
    cos_f = jnp.concatenate([one, cos, cos, tail + 1.0], axis=1)
    sin_a = jnp.concatenate([one * 0.0, -sin, zero, tail], axis=1)
    sin_b = jnp.concatenate([one * 0.0, zero, sin, tail], axis=1)
    reset = (positions[0] == 0).astype(F32)[:, None]
    tabs = [(cos_f, (tile, LANE), "row"), (sin_a, (tile, LANE), "row"), (sin_b, (tile, LANE), "row")]

    def rowspec(a):
        return (a, (tile, a.shape[1]), "row")

    def full2(a):
        return (a, a.shape, "full")

    def rowout(cols, dt):
        return ((s_len, cols), dt, (tile, cols), "row")

    def accout(a):
        return (a.shape, F32, a.shape, "acc")

    norm1_g, norm2_g, final_g = w["norm1_g"], w["norm2_g"], w["final_g"].reshape(1, d)
    ln1_in = [rowspec(x2d), full2(norm1_g), full2(scale1), full2(shift1)]
    (h1,) = _tiled("ln1", _f_ln, nt, ln1_in, [rowout(d, BF16)])
    x_rnn = _mm("in_rnn", h1, w_rnn)
    qkv = _mm("in_qkv", h1, w_qkv)
    gates = _mm("in_gates", h1, w_g)

    ct = LANE
    n_ct = d_rnn // ct
    colspec = lambda a, width=ct: (a, (a.shape[0], width), "col")
    lru_in = [colspec(x_rnn), colspec(conv_w), colspec(conv_b), colspec(wa_bd), colspec(w["b_gate_a"]),
              colspec(wx_bd), colspec(w["b_gate_x"]), colspec(w["lru_param"]), full2(reset)]
    y_rnn, h_rnn = _tiled("lru_fwd", _f_lru_fwd, n_ct, lru_in,
                          [((s_len, d_rnn), BF16, (s_len, ct), "col"), ((s_len, d_rnn), F32, (s_len, ct), "col")])

    qkv_in = [rowspec(qkv)] + tabs + [full2(w["q_norm_g"]), full2(w["kv_norm_g"])]
    qn, kvn, kr = _tiled("qkv_norm", _f_qkv, nt, qkv_in, [rowout(n_q, BF16), rowout(n_kv, BF16), rowout(LANE, BF16)])
    q_pre = _mm("up_q", qn, w_uq)
    kv = _mm("up_kv", kvn, w_ukv, out_dtype=BF16)
    (q_cat,) = _tiled("rot_q", _f_rotq, nt, [rowspec(q_pre)] + tabs, [rowout(q_pre.shape[1], BF16)])
    o_mla, lse = _attn_fwd(q_cat, kv, kr)

    p_rnn = _mm("proj_rnn", y_rnn, w_pr)
    p_mla = _mm("proj_mla", o_mla, w_pm)
    merge_in = [rowspec(gates), rowspec(p_rnn), rowspec(p_mla)]
    (merged,) = _tiled("merge", _f_merge, nt, merge_in, [rowout(d, BF16)])
    o_tok = _mm("out_proj", merged, w_out)
    res_in = [rowspec(x2d), rowspec(o_tok), full2(gate1), full2(norm2_g), full2(scale2), full2(shift2)]
    x1, h2 = _tiled("res_ln2", _f_res_ln, nt, res_in, [rowout(d, F32), rowout(d, BF16)])
    u_pre = _mm("ffn_up", h2, w_up)
    d_ff2 = u_pre.shape[1]
    n_ft = d_ff2 // (2 * LANE)
    ffn_in = [colspec(u_pre, 2 * LANE), colspec(ffn_cw, 2 * LANE), colspec(ffn_cb, 2 * LANE)]
    (act,) = _tiled("ffn_conv", _f_ffn, n_ft, ffn_in, [((s_len, d_ff2 // 2), BF16, (s_len, LANE), "col")])
    f_tok = _mm("ffn_down", act, w_down)

    loss_in = [rowspec(x1), rowspec(f_tok), rowspec(tgt), full2(gate2), full2(final_g)]
    dx1, df, loss_row, d_gate2, d_final_g = _tiled(
        "loss", _f_loss_and_grads, nt, loss_in,
        [rowout(d, F32), rowout(d, BF16), ((1, LANE), F32, (1, LANE), "acc"), accout(gate2), accout(final_g)])
    loss = lax.psum(loss_row[0, 0], ("x", "y", "c"))

    d_act = _mm("ffn_down_dx", df, w_down, tb=True)
    g_w_down = _mm("ffn_down_dw", act, df, ta=True)
    du, g_ffn_cw, g_ffn_cb = _tiled(
        "ffn_conv_bwd", _f_ffn_bwd, n_ft, ffn_in + [colspec(d_act)],
        [((s_len, d_ff2), BF16, (s_len, 2 * LANE), "col"), (ffn_cw.shape, F32, (ffn_cw.shape[0], 2 * LANE), "col"),
         (ffn_cb.shape, F32, (1, 2 * LANE), "col")])
    dh2 = _mm("ffn_up_dx", du, w_up, tb=True)
    g_w_up = _mm("ffn_up_dw", h2, du, ta=True)

    res_bwd = _vjp_of(_f_res_ln, 6, (0, 1, 2, 3, 4, 5))
    dx_res, do_tok, d_gate1, g_norm2, d_scale2, d_shift2 = _tiled(
        "res_ln2_bwd", res_bwd, nt, res_in + [rowspec(dx1), rowspec(dh2)],
        [rowout(d, F32), rowout(d, BF16), accout(gate1), accout(norm2_g), accout(scale2), accout(shift2)])
    d_merged = _mm("out_proj_dx", do_tok, w_out, tb=True)
    g_w_out = _mm("out_proj_dw", merged, do_tok, ta=True)
    d_gates, dp_rnn, dp_mla = _tiled(
        "merge_bwd", _f_merge_bwd, nt, merge_in + [rowspec(d_merged)],
        [rowout(gates.shape[1], BF16), rowout(d, BF16), rowout(d, BF16)])
    dy_rnn = _mm("proj_rnn_dx", dp_rnn, w_pr, tb=True)
    g_w_pr = _mm("proj_rnn_dw", y_rnn, dp_rnn, ta=True)
    do_mla = _mm("proj_mla_dx", dp_mla, w_pm, tb=True, out_dtype=BF16)
    g_w_pm = _mm("proj_mla_dw", o_mla, dp_mla, ta=True)

    dq_cat, dkv, dkr = _attn_bwd(q_cat, kv, kr, o_mla, lse, do_mla)
    rot_bwd = _vjp_of(_f_rotq, 4, (0,))
    (dq_pre,) = _tiled("rot_q_bwd", rot_bwd, nt, [rowspec(q_pre)] + tabs + [rowspec(dq_cat)],
                       [rowout(q_pre.shape[1], BF16)])
    dqn = _mm("up_q_dx", dq_pre, w_uq, tb=True)
    g_w_uq = _mm("up_q_dw", qn, dq_pre, ta=True)
    dkv_b = dkv.astype(BF16)
    dkvn = _mm("up_kv_dx", dkv_b, w_ukv, tb=True)
    g_w_ukv = _mm("up_kv_dw", kvn, dkv_b, ta=True)
    dqkv, g_q_norm, g_kv_norm = _tiled(
        "qkv_norm_bwd", _f_qkv_bwd, nt, qkv_in + [rowspec(dqn), rowspec(dkvn), rowspec(dkr)],
        [rowout(qkv.shape[1], BF16), accout(w["q_norm_g"]), accout(w["kv_norm_g"])])

    lru_out = [((s_len, d_rnn), BF16, (s_len, ct), "col")]
    for a in (conv_w, conv_b, wa_bd, w["b_gate_a"], wx_bd, w["b_gate_x"], w["lru_param"]):
        lru_out.append((a.shape, F32, (a.shape[0], ct), "col"))
    dx_rnn, g_conv_w, g_conv_b, g_wa_bd, g_b_a, g_wx_bd, g_b_x, g_lru = _tiled(
        "lru_bwd", _f_lru_bwd, n_ct, lru_in + [colspec(h_rnn), colspec(dy_rnn)], lru_out)

    dh1 = _mm("in_gates_dx", d_gates, w_g, tb=True)
    dh1 = _mm("in_qkv_dx", dqkv, w_qkv, tb=True, add=dh1)
    dh1 = _mm("in_rnn_dx", dx_rnn, w_rnn, tb=True, add=dh1)
    g_w_rnn = _mm("in_rnn_dw", h1, dx_rnn, ta=True)
    g_w_qkv = _mm("in_qkv_dw", h1, dqkv, ta=True)
    g_w_g = _mm("in_gates_dw", h1, d_gates, ta=True)

    ln_bwd = _vjp_of(_f_ln, 4, (0, 1, 2, 3))

    def ln1_bwd(xv, gv, sc, sh, dxr, dh):
        dx, dg, dsc, dsh = ln_bwd(xv, gv, sc, sh, dh)
        return dx + dxr, dg, dsc, dsh

    grad_x, g_norm1, d_scale1, d_shift1 = _tiled(
        "ln1_bwd", ln1_bwd, nt, ln1_in + [rowspec(dx_res), rowspec(dh1)],
        [rowout(d, F32), accout(norm1_g), accout(scale1), accout(shift1)])

    dmod = jnp.concatenate([d_shift1, d_scale1, d_gate1, d_shift2, d_scale2, d_gate2], axis=1)
    dmod_all = _all_gather("gather_dmod", dmod, ALL7).reshape(8, -1)
    dmod_loc = lax.dynamic_slice_in_dim(dmod_all, chip * n_mod, n_mod, axis=1)
    g_w_ada = _mm("ada_dw", c_act, jnp.pad(dmod_loc, ((0, c_rows - 8), (0, 0))), ta=True)

    g_full = {
        "w_in": jnp.concatenate([g_w_rnn, g_w_qkv[:, :n_q + n_kv],
                                 g_w_qkv[:, n_q + n_kv + QK_NOPE:n_q + n_kv + QK_NOPE + QK_ROPE], g_w_g], axis=1),
        "w_uq": g_w_uq.reshape(n_q, N_HEADS, HEAD_PAD)[:, :, :hd].reshape(n_q, -1),
        "w_ukv": g_w_ukv,
        "w_proj_rnn": g_w_pr,
        "w_proj_mla": g_w_pm.reshape(N_HEADS, HEAD_PAD, d)[:, HEAD_PAD - v_head:, :].reshape(-1, d),
        "w_out": g_w_out,
        "w_up": _deinterleave(g_w_up),
        "w_down": g_w_down,
        "conv_w": g_conv_w,
        "ffn_conv_w": _deinterleave(g_ffn_cw),
    }
    g_small = {
        "b_ada": dmod, "norm1_g": g_norm1, "conv_b": g_conv_b,
        "w_gate_a": _block_diag_pairs_t(g_wa_bd)[None], "b_gate_a": g_b_a,
        "w_gate_x": _block_diag_pairs_t(g_wx_bd)[None], "b_gate_x": g_b_x, "lru_param": g_lru,
        "q_norm_g": g_q_norm, "kv_norm_g": g_kv_norm, "norm2_g": g_norm2,
        "ffn_conv_b": _deinterleave(g_ffn_cb), "final_g": d_final_g.reshape(w["final_g"].shape),
    }

    chunks = []
    for k, kind in BIG:
        r, cc = local2d[k].shape
        gk = g_full[k]
        if kind == "col":
            gk = gk.reshape(r, 4, cc).transpose(1, 0, 2)
        chunks.append(gk.reshape(4, 2, r // 2, cc))
    small_flat = jnp.concatenate([g_small[k].reshape(-1) for k in SMALL] + [g_full[k].reshape(-1) for k, _ in CONVS])
    small_rows = -(-small_flat.shape[0] // (8 * PACK_COLS * PACK_ROW_UNIT)) * PACK_ROW_UNIT
    chunks.append(_pad_rows(small_flat[None], 8 * small_rows).reshape(4, 2, small_rows, PACK_COLS))
    pair_sums = []
    for k, (ck, from_sib) in enumerate(zip(chunks, _pair_exchange(chunks))):
        ours = lax.dynamic_index_in_dim(ck, ci, 1, keepdims=False)
        pair_sums.append(_reduce_pair("reduce_pair_%d" % k, ours, from_sib))
    reduced = []
    for k, (quad, ps) in enumerate(zip(_chips_alltoall(pair_sums), pair_sums)):
        quad = lax.dynamic_update_index_in_dim(quad, lax.dynamic_index_in_dim(ps, chip, 0, keepdims=True), chip, 0)
        reduced.append(_reduce_quad("reduce_quad_%d" % k, quad))
    grads = {}
    for (k, _), both, own in zip(BIG, _share_sibling(reduced[:-1]), reduced):
        grads[k] = lax.dynamic_update_index_in_dim(both, own[None], ci, 0).reshape(w[k].shape)
    small_grad = _all_gather("share_small", reduced[-1], ALL7).reshape(-1)
    off = 0
    for k in SMALL:
        grads[k] = small_grad[off:off + w[k].size].reshape(w[k].shape)
        off += w[k].size
    for k, _ in CONVS:
        r, cc = local2d[k].shape
        whole = small_grad[off:off + 4 * r * cc].reshape(r, 4 * cc)
        grads[k] = lax.dynamic_slice_in_dim(whole, chip * cc, cc, axis=1)[None]
        off += 4 * r * cc
    grads["w_ada"] = g_w_ada[None]

    delta, new_m, new_v = {}, {}, {}
    for k in WEIGHTS:
        shp = w[k].shape
        two_d = (-1, shp[-1]) if len(shp) > 1 else (1, -1)
        dk, mk, vk = _adamw("adamw_" + k, w[k].reshape(two_d), grads[k].reshape(two_d),
                            m_in[k].reshape(two_d), v_in[k].reshape(two_d))
        delta[k], new_m[k], new_v[k] = dk.reshape(shp), mk.reshape(shp), vk.reshape(shp)

    return (loss, grad_x[None], *[grads[k] for k in WEIGHTS], *[delta[k] for k in WEIGHTS],
            *[new_m[k] for k in WEIGHTS], *[new_v[k] for k in WEIGHTS])


def kernel(x, c, positions, w_ada, b_ada, norm1_g, w_in, conv_w, conv_b, w_gate_a, b_gate_a, w_gate_x, b_gate_x, lru_param, q_norm_g, w_uq, kv_norm_g, w_ukv, w_proj_rnn, w_proj_mla, w_out, norm2_g, w_up, ffn_conv_w, ffn_conv_b, w_down, final_g, loss_target, m_w_ada, m_b_ada, m_norm1_g, m_w_in, m_conv_w, m_conv_b, m_w_gate_a, m_b_gate_a, m_w_gate_x, m_b_gate_x, m_lru_param, m_q_norm_g, m_w_uq, m_kv_norm_g, m_w_ukv, m_w_proj_rnn, m_w_proj_mla, m_w_out, m_norm2_g, m_w_up, m_ffn_conv_w, m_ffn_conv_b, m_w_down, m_final_g, v_w_ada, v_b_ada, v_norm1_g, v_w_in, v_conv_w, v_conv_b, v_w_gate_a, v_b_gate_a, v_w_gate_x, v_b_gate_x, v_lru_param, v_q_norm_g, v_w_uq, v_kv_norm_g, v_w_ukv, v_w_proj_rnn, v_w_proj_mla, v_w_out, v_norm2_g, v_w_up, v_ffn_conv_w, v_ffn_conv_b, v_w_down, v_final_g):
    given = dict(locals())
    w = {k: given[k] for k in WEIGHTS}
    m_in = {k: given["m_" + k] for k in WEIGHTS}
    v_in = {k: given["v_" + k] for k in WEIGHTS}
    return _step(x, c, positions, w, m_in, v_in, loss_target)
```

```python
import functools
import math

import jax
import jax.numpy as jnp
from jax import lax
from jax.experimental import pallas as pl
from jax.experimental.pallas import tpu as pltpu

F32 = jnp.float32
BF16 = jnp.bfloat16

EPS = 1e-6
LRU_C = 8.0
N_HEADS = 16
QK_NOPE = 64
QK_ROPE = 32
HEAD_PAD = 128
ROPE_THETA = 10000.0
ADAM_LR = 0.001
ADAM_B1 = 0.9
ADAM_B2 = 0.999
ADAM_EPS = 1e-08
ADAM_WD = 0.01
ADAM_STEP = 10

LANE = 128
SUBLANES = 8
VMEM_LIMIT = 48 * 1024 * 1024
MM_TILE_M = MM_TILE_N = MM_TILE_K = 1408
PACK_COLS = 1024
PACK_ROW_UNIT = 32
MESH = pl.DeviceIdType.MESH

NN = (((1,), (0,)), ((), ()))
NT = (((1,), (1,)), ((), ()))


def _cparams(sem):
    return pltpu.CompilerParams(dimension_semantics=sem, vmem_limit_bytes=VMEM_LIMIT)


def _div_tile(n, cap, unit):
    best = None
    d = unit
    while d <= min(n, cap):
        if n % d == 0:
            best = d
        d += unit
    return n if best is None else best


def _mm(name, a, b, *, ta=False, tb=False, add=None, out_dtype=F32):
    if ta:
        kdim, m = a.shape
    else:
        m, kdim = a.shape
    if tb:
        n, kb = b.shape
    else:
        kb, n = b.shape
    assert kdim == kb, (name, a.shape, b.shape)
    tm = _div_tile(m, MM_TILE_M, 8 if not ta else LANE)
    tn = _div_tile(n, MM_TILE_N, LANE)
    tk = _div_tile(kdim, MM_TILE_K, LANE)
    nk = kdim // tk
    a_spec = pl.BlockSpec((tk, tm), lambda i, j, k: (k, i)) if ta else pl.BlockSpec((tm, tk), lambda i, j, k: (i, k))
    b_spec = pl.BlockSpec((tn, tk), lambda i, j, k: (j, k)) if tb else pl.BlockSpec((tk, tn), lambda i, j, k: (k, j))
    o_spec = pl.BlockSpec((tm, tn), lambda i, j, k: (i, j))
    has_add = add is not None
    dims = ((((0,) if ta else (1,)), ((1,) if tb else (0,))), ((), ()))

    def body(*refs):
        a_ref, b_ref = refs[0], refs[1]
        c_ref = refs[2] if has_add else None
        o_ref = refs[3] if has_add else refs[2]
        prod = lax.dot_general(a_ref[...].astype(BF16), b_ref[...].astype(BF16), dims, preferred_element_type=F32)
        if nk == 1:
            o_ref[...] = (prod + c_ref[...].astype(F32) if has_add else prod).astype(o_ref.dtype)
            return
        acc = refs[-1]
        k = pl.program_id(2)

        @pl.when(k == 0)
        def _():
            acc[...] = prod + c_ref[...].astype(F32) if has_add else prod

        @pl.when(jnp.logical_and(k > 0, k < nk - 1))
        def _():
            acc[...] += prod

        @pl.when(k == nk - 1)
        def _():
            o_ref[...] = (acc[...] + prod).astype(o_ref.dtype)

    ins = [a, b] + ([add] if has_add else [])
    specs = [a_spec, b_spec] + ([o_spec] if has_add else [])
    return pl.pallas_call(
        body, name=name, grid=(m // tm, n // tn, nk), in_specs=specs, out_specs=o_spec,
        out_shape=jax.ShapeDtypeStruct((m, n), out_dtype),
        scratch_shapes=[pltpu.VMEM((tm, tn), F32)] if nk > 1 else [],
        compiler_params=_cparams(("parallel", "parallel", "arbitrary")),
    )(*ins)


_IMAPS = {
    "row": lambda i: (i, 0),
    "col": lambda i: (0, i),
    "full": lambda i: (0, 0),
    "acc": lambda i: (0, 0),
}


def _tiled(name, fn, n, ins, outs):
    ni = len(ins)
    is_acc = [k == "acc" for *_, k in outs]

    def body(*refs):
        vals = fn(*[r[...] for r in refs[:ni]])
        orefs = refs[ni:]
        if any(is_acc):
            @pl.when(pl.program_id(0) == 0)
            def _():
                for r, a in zip(orefs, is_acc):
                    if a:
                        r[...] = jnp.zeros(r.shape, r.dtype)
        for r, v, a in zip(orefs, vals, is_acc):
            if a:
                r[...] += v.astype(r.dtype)
            else:
                r[...] = v.astype(r.dtype)

    res = pl.pallas_call(
        body, name=name, grid=(n,),
        in_specs=[pl.BlockSpec(bs, _IMAPS[k]) for _, bs, k in ins],
        out_specs=[pl.BlockSpec(bs, _IMAPS[k]) for _, _, bs, k in outs],
        out_shape=[jax.ShapeDtypeStruct(s, d) for s, d, _, _ in outs],
        compiler_params=_cparams(("arbitrary",)),
    )(*[a for a, _, _ in ins])
    return tuple(res)


def _vjp_of(fn, nin, diff):
    def g(*args):
        ins, cots = args[:nin], args[nin:]

        def f(*d):
            full = list(ins)
            for i, v in zip(diff, d):
                full[i] = v
            return fn(*full)

        outs, vjp = jax.vjp(f, *[ins[i] for i in diff])
        return vjp(tuple(c.astype(o.dtype) for c, o in zip(cots, outs)))
    return g


def _shift_rows(x, k, fill, up=False):
    n = x.shape[0]
    if k % SUBLANES == 0:
        pad = jnp.full((k,) + x.shape[1:], fill, x.dtype)
        return jnp.concatenate([x[k:], pad], axis=0) if up else jnp.concatenate([pad, x[:n - k]], axis=0)
    rows = lax.broadcasted_iota(jnp.int32, x.shape, 0)
    if up:
        return jnp.where(rows < n - k, pltpu.roll(x, n - k, 0), fill)
    return jnp.where(rows >= k, pltpu.roll(x, k, 0), fill)


@functools.partial(jax.custom_vjp, nondiff_argnums=(1,))
def _delay(x, k):
    return _shift_rows(x, k, 0.0)


def _delay_fwd(x, k):
    return _shift_rows(x, k, 0.0), None


def _delay_bwd(k, _, g):
    return (_shift_rows(g, k, 0.0, up=True),)


_delay.defvjp(_delay_fwd, _delay_bwd)


@functools.partial(jax.custom_vjp, nondiff_argnums=(1,))
def _lane_roll(x, s):
    return pltpu.roll(x, s, 1)


def _lane_roll_fwd(x, s):
    return pltpu.roll(x, s, 1), None


def _lane_roll_bwd(s, _, g):
    return (pltpu.roll(g, g.shape[1] - s, 1),)


_lane_roll.defvjp(_lane_roll_fwd, _lane_roll_bwd)


@jax.custom_vjp
def _bdot(x, w):
    return lax.dot_general(x.astype(BF16), w.astype(BF16), NN, preferred_element_type=F32)


def _bdot_fwd(x, w):
    return _bdot(x, w), (x, w)


def _bdot_bwd(res, g):
    x, w = res
    gb = g.astype(BF16)
    dx = lax.dot_general(gb, w.astype(BF16), NT, preferred_element_type=F32)
    dw = lax.dot_general(x.T.astype(BF16), gb, NN, preferred_element_type=F32)
    return dx, dw


_bdot.defvjp(_bdot_fwd, _bdot_bwd)


def _sigmoid(x):
    return 0.5 * (jnp.tanh(0.5 * x) + 1.0)


def _silu(x):
    return x * _sigmoid(x)


def _rms(x, g):
    return x * lax.rsqrt(jnp.mean(x * x, axis=-1, keepdims=True) + EPS) * g


def _causal_conv(x, w, b):
    kw = w.shape[0]
    tap = lax.broadcasted_iota(jnp.int32, w.shape, 0)
    y = b
    for k in range(kw):
        d = kw - 1 - k
        wk = jnp.sum(jnp.where(tap == k, w, 0.0), axis=0, keepdims=True)
        y = y + wk * (x if d == 0 else _delay(x, d))
    return y


def _rotate(x, cos_f, sin_a, sin_b):
    reps = x.shape[1] // LANE
    if reps > 1:
        cos_f, sin_a, sin_b = (jnp.tile(t, (1, reps)) for t in (cos_f, sin_a, sin_b))
    n = x.shape[1]
    half = QK_ROPE // 2
    return x * cos_f + _lane_roll(x, n - half) * sin_a + _lane_roll(x, half) * sin_b


def _softplus_neg(l):
    u = jnp.exp(-jnp.abs(l))
    log1p_u = jnp.where(u < 0.01, u * (1.0 - u * (0.5 - u * (1.0 / 3.0))), jnp.log(1.0 + u))
    return jnp.maximum(-l, 0.0) + log1p_u


def _f_ln(x, g, scale, shift):
    return (_rms(x, g) * (1.0 + scale) + shift,)


def _f_qkv(qkv, cos_f, sin_a, sin_b, qg, kvg):
    nq, nkv = qg.shape[1], kvg.shape[1]
    qn = _rms(qkv[:, :nq], qg)
    kvn = _rms(qkv[:, nq:nq + nkv], kvg)
    kr = _rotate(qkv[:, nq + nkv:], cos_f, sin_a, sin_b)
    return qn, kvn, kr


def _f_qkv_bwd(qkv, cos_f, sin_a, sin_b, qg, kvg, dqn, dkvn, dkr):
    nq, nkv = qg.shape[1], kvg.shape[1]
    _, vjp_q = jax.vjp(_rms, qkv[:, :nq], qg)
    _, vjp_kv = jax.vjp(_rms, qkv[:, nq:nq + nkv], kvg)
    _, vjp_r = jax.vjp(lambda t: _rotate(t, cos_f, sin_a, sin_b), qkv[:, nq + nkv:])
    dq_lat, dqg = vjp_q(dqn)
    dkv_lat, dkvg = vjp_kv(dkvn)
    (dkr_pre,) = vjp_r(dkr)
    return jnp.concatenate([dq_lat, dkv_lat, dkr_pre], axis=1), dqg, dkvg


def _f_rotq(q, cos_f, sin_a, sin_b):
    return (_rotate(q, cos_f, sin_a, sin_b),)


def _merge(g_rnn, g_mla, p_rnn, p_mla):
    return _sigmoid(g_rnn) * p_rnn + _sigmoid(g_mla) * p_mla


def _f_merge(g, p_rnn, p_mla):
    d = p_rnn.shape[1]
    return (_merge(g[:, :d], g[:, d:], p_rnn, p_mla),)


def _f_merge_bwd(g, p_rnn, p_mla, dm):
    d = p_rnn.shape[1]
    _, vjp = jax.vjp(_merge, g[:, :d], g[:, d:], p_rnn, p_mla)
    dg_rnn, dg_mla, dp_rnn, dp_mla = vjp(dm)
    return jnp.concatenate([dg_rnn, dg_mla], axis=1), dp_rnn, dp_mla


def _f_res_ln(x, o, gate, g2, scale, shift):
    x1 = x + gate * o
    return x1, _rms(x1, g2) * (1.0 + scale) + shift


def _f_ffn(u_gate, u_val, cw_gate, cw_val, cb_gate, cb_val):
    return (_silu(_causal_conv(u_gate, cw_gate, cb_gate)) * _causal_conv(u_val, cw_val, cb_val),)


def _f_loss(x1, f, tgt, gate, fg):
    y = _rms(x1 + gate * f, fg)
    err = (y - tgt) * (y - tgt)
    return 0.5 * jnp.sum(jnp.mean(err, axis=-1, keepdims=True), axis=0, keepdims=True)


def _f_loss_and_grads(x1, f, tgt, gate, fg):
    loss, vjp = jax.vjp(lambda a, b, c, d: _f_loss(a, b, tgt, c, d), x1, f, gate, fg)
    dx1, df, dgate, dfg = vjp(jnp.ones((1, 1), F32))
    return dx1, df, jnp.broadcast_to(loss, (1, LANE)), dgate, dfg


def _f_lru_coeffs(xr, cw, cb, wa, ba, wx, bx, lru, reset):
    xc = _causal_conv(xr, cw, cb)
    r = _sigmoid(_bdot(xc, wa) + ba)
    i = _sigmoid(_bdot(xc, wx) + bx)
    log_a = (-LRU_C) * r * _softplus_neg(lru)
    a = jnp.exp(log_a)
    mult = jnp.sqrt(-jnp.tanh(log_a) * (1.0 + a * a))
    is_reset = reset > 0.5
    a = jnp.where(is_reset, 0.0, a)
    mult = jnp.where(is_reset, 1.0, mult)
    return a, mult * (i * xc)


def _scan(a, b, up=False):
    n = a.shape[0]
    k = 1
    while k < n:
        b = b + a * _shift_rows(b, k, 0.0, up)
        if 2 * k < n:
            a = a * _shift_rows(a, k, 1.0, up)
        k *= 2
    return b


def _f_lru_fwd(xr, cw, cb, wa, ba, wx, bx, lru, reset):
    a, b = _f_lru_coeffs(xr, cw, cb, wa, ba, wx, bx, lru, reset)
    h = _scan(a, b)
    return h, h


def _f_lru_bwd(xr, cw, cb, wa, ba, wx, bx, lru, reset, h, dh):
    (a, _), vjp = jax.vjp(lambda *p: _f_lru_coeffs(*p, reset), xr, cw, cb, wa, ba, wx, bx, lru)
    g = _scan(_shift_rows(a, 1, 0.0, up=True), dh, up=True)
    return vjp((g * _shift_rows(h, 1, 0.0), g))


def _attn_tile(s):
    return 1024 if s >= 2048 else s // 2


def _keys(kv, kr):
    lane = lax.broadcasted_iota(jnp.int32, kv.shape, 1)
    return jnp.where(lane < QK_NOPE, kv, kr)


ATTN_HEADS_PER_STEP = 2


def _scores(q, kc, scale, diagonal):
    s = lax.dot_general(q, kc, NT, preferred_element_type=F32) * scale
    if not diagonal:
        return s
    rows = lax.broadcasted_iota(jnp.int32, s.shape, 0)
    cols = lax.broadcasted_iota(jnp.int32, s.shape, 1)
    return jnp.where(cols <= rows, s, -jnp.inf)


def _causal_pairs(nb, k_major):
    if k_major:
        pairs = [(qb, kb) for kb in range(nb) for qb in range(kb, nb)]
    else:
        pairs = [(qb, kb) for qb in range(nb) for kb in range(qb + 1)]
    return jnp.array([p[0] for p in pairs], jnp.int32), jnp.array([p[1] for p in pairs], jnp.int32)


def _attn_fwd(q, kv, kr):
    s_len = q.shape[0]
    t = _attn_tile(s_len)
    nb = s_len // t
    hp = ATTN_HEADS_PER_STEP
    wide = hp * HEAD_PAD
    scale = 1.0 / math.sqrt(QK_NOPE + QK_ROPE)
    q_tab, k_tab = _causal_pairs(nb, k_major=False)

    def body(qt, kt, q_ref, kv_ref, kr_ref, o_ref, lse_ref, m_s, l_s, acc_s):
        pair = pl.program_id(1)
        qi, ki = qt[pair], kt[pair]

        @pl.when(ki == 0)
        def _():
            m_s[...] = jnp.full(m_s.shape, -jnp.inf, F32)
            l_s[...] = jnp.zeros(l_s.shape, F32)
            acc_s[...] = jnp.zeros(acc_s.shape, F32)

        def step(diagonal):
            krv = kr_ref[...]
            for h in range(hp):
                lanes = slice(h * HEAD_PAD, (h + 1) * HEAD_PAD)
                kvv = kv_ref[:, lanes]
                s = _scores(q_ref[:, lanes], _keys(kvv, krv), scale, diagonal)
                m_old = m_s[h]
                m_new = jnp.maximum(m_old, jnp.max(s, axis=-1, keepdims=True))
                alpha = jnp.exp(m_old - m_new)
                p = jnp.exp(s - m_new)
                l_s[h] = alpha * l_s[h] + jnp.sum(p, axis=-1, keepdims=True)
                acc_s[:, lanes] = alpha * acc_s[:, lanes] + lax.dot_general(
                    p.astype(BF16), kvv, NN, preferred_element_type=F32)
                m_s[h] = m_new

        @pl.when(ki < qi)
        def _():
            step(False)

        @pl.when(ki == qi)
        def _():
            step(True)
            lane = lax.broadcasted_iota(jnp.int32, (t, HEAD_PAD), 1)
            for h in range(hp):
                lanes = slice(h * HEAD_PAD, (h + 1) * HEAD_PAD)
                o_ref[:, lanes] = jnp.where(lane >= QK_NOPE, acc_s[:, lanes] / l_s[h], 0.0).astype(o_ref.dtype)
                lse_ref[h] = m_s[h] + jnp.log(l_s[h])

    grid_spec = pltpu.PrefetchScalarGridSpec(
        num_scalar_prefetch=2, grid=(N_HEADS // hp, q_tab.shape[0]),
        in_specs=[pl.BlockSpec((t, wide), lambda h, p, qt, kt: (qt[p], h)),
                  pl.BlockSpec((t, wide), lambda h, p, qt, kt: (kt[p], h)),
                  pl.BlockSpec((t, HEAD_PAD), lambda h, p, qt, kt: (kt[p], 0))],
        out_specs=[pl.BlockSpec((t, wide), lambda h, p, qt, kt: (qt[p], h)),
                   pl.BlockSpec((hp, t, 1), lambda h, p, qt, kt: (h, qt[p], 0))],
        scratch_shapes=[pltpu.VMEM((hp, t, 1), F32), pltpu.VMEM((hp, t, 1), F32), pltpu.VMEM((t, wide), F32)])
    return pl.pallas_call(
        body, name="attn_fwd", grid_spec=grid_spec,
        out_shape=[jax.ShapeDtypeStruct((s_len, N_HEADS * HEAD_PAD), BF16),
                   jax.ShapeDtypeStruct((N_HEADS, s_len, 1), F32)],
        compiler_params=_cparams(("arbitrary", "arbitrary")),
    )(q_tab, k_tab, q, kv, kr)


def _attn_bwd(q, kv, kr, o, lse, do):
    s_len = q.shape[0]
    t = _attn_tile(s_len)
    nb = s_len // t
    hp = ATTN_HEADS_PER_STEP
    wide = hp * HEAD_PAD
    scale = 1.0 / math.sqrt(QK_NOPE + QK_ROPE)
    q_tab, k_tab = _causal_pairs(nb, k_major=True)

    def body(qt, kt, q_ref, kv_ref, kr_ref, o_ref, lse_ref, do_ref, dq_ref, dkv_ref, dkr_ref, dk_s, dv_s):
        g, pair = pl.program_id(0), pl.program_id(1)
        qb, kb = qt[pair], kt[pair]

        @pl.when(jnp.logical_and(g == 0, pair == 0))
        def _():
            dkr_ref[...] = jnp.zeros(dkr_ref.shape, F32)

        @pl.when(pair == 0)
        def _():
            dq_ref[...] = jnp.zeros(dq_ref.shape, F32)

        @pl.when(qb == kb)
        def _():
            dk_s[...] = jnp.zeros(dk_s.shape, F32)
            dv_s[...] = jnp.zeros(dv_s.shape, F32)

        def step(diagonal):
            krv = kr_ref[...]
            rows = pl.ds(pl.multiple_of(qb * t, t), t)
            for h in range(hp):
                lanes = slice(h * HEAD_PAD, (h + 1) * HEAD_PAD)
                qv, kvv, dov = q_ref[:, lanes], kv_ref[:, lanes], do_ref[:, lanes]
                kc = _keys(kvv, krv)
                p = jnp.exp(_scores(qv, kc, scale, diagonal) - lse_ref[h])
                delta = jnp.sum(dov.astype(F32) * o_ref[:, lanes].astype(F32), axis=-1, keepdims=True)
                dp = lax.dot_general(dov, kvv, NT, preferred_element_type=F32)
                ds = p * (dp - delta) * scale
                dv_s[:, lanes] += lax.dot_general(p.T.astype(BF16), dov, NN, preferred_element_type=F32)
                dk_s[:, lanes] += lax.dot_general(ds.T.astype(BF16), qv, NN, preferred_element_type=F32)
                dq_ref[rows, lanes] += lax.dot_general(ds.astype(BF16), kc, NN, preferred_element_type=F32)

        @pl.when(qb > kb)
        def _():
            step(False)

        @pl.when(qb == kb)
        def _():
            step(True)

        @pl.when(qb == nb - 1)
        def _():
            lane = lax.broadcasted_iota(jnp.int32, (t, HEAD_PAD), 1)
            rows = pl.ds(pl.multiple_of(kb * t, t), t)
            for h in range(hp):
                lanes = slice(h * HEAD_PAD, (h + 1) * HEAD_PAD)
                dkv_ref[:, lanes] = jnp.where(lane < QK_NOPE, dk_s[:, lanes], dv_s[:, lanes])
                dkr_ref[rows, :] += jnp.where(lane >= QK_NOPE, dk_s[:, lanes], 0.0)

    all_lanes = N_HEADS * HEAD_PAD
    qmap = lambda h, p, qt, kt: (qt[p], h)
    kmap = lambda h, p, qt, kt: (kt[p], h)
    grid_spec = pltpu.PrefetchScalarGridSpec(
        num_scalar_prefetch=2, grid=(N_HEADS // hp, q_tab.shape[0]),
        in_specs=[pl.BlockSpec((t, wide), qmap),
                  pl.BlockSpec((t, wide), kmap),
                  pl.BlockSpec((t, HEAD_PAD), lambda h, p, qt, kt: (kt[p], 0)),
                  pl.BlockSpec((t, wide), qmap),
                  pl.BlockSpec((hp, t, 1), lambda h, p, qt, kt: (h, qt[p], 0)),
                  pl.BlockSpec((t, wide), qmap)],
        out_specs=[pl.BlockSpec((s_len, wide), lambda h, p, qt, kt: (0, h)),
                   pl.BlockSpec((t, wide), kmap),
                   pl.BlockSpec((s_len, HEAD_PAD), lambda h, p, qt, kt: (0, 0))],
        scratch_shapes=[pltpu.VMEM((t, wide), F32), pltpu.VMEM((t, wide), F32)])
    return pl.pallas_call(
        body, name="attn_bwd", grid_spec=grid_spec,
        out_shape=[jax.ShapeDtypeStruct((s_len, all_lanes), F32),
                   jax.ShapeDtypeStruct((s_len, all_lanes), F32),
                   jax.ShapeDtypeStruct((s_len, HEAD_PAD), F32)],
        compiler_params=_cparams(("arbitrary", "arbitrary")),
    )(q_tab, k_tab, q, kv, kr, o, lse, do)


def _adamw(name, w, g, m, v):
    rows, cols = w.shape
    tr = _div_tile(rows, max(8, (2 * 1024 * 1024) // (4 * cols)), 8)

    def body(w_ref, g_ref, m_ref, v_ref, d_ref, nm_ref, nv_ref):
        gv = g_ref[...]
        nm = ADAM_B1 * m_ref[...] + (1.0 - ADAM_B1) * gv
        nv = ADAM_B2 * v_ref[...] + (1.0 - ADAM_B2) * jnp.square(gv)
        m_hat = nm / (1.0 - ADAM_B1 ** ADAM_STEP)
        v_hat = nv / (1.0 - ADAM_B2 ** ADAM_STEP)
        d_ref[...] = -ADAM_LR * (m_hat / (jnp.sqrt(v_hat) + ADAM_EPS) + ADAM_WD * w_ref[...])
        nm_ref[...] = nm
        nv_ref[...] = nv

    spec = pl.BlockSpec((tr, cols), lambda i: (i, 0))
    return pl.pallas_call(
        body, name=name, grid=(rows // tr,), in_specs=[spec] * 4, out_specs=[spec] * 3,
        out_shape=[jax.ShapeDtypeStruct((rows, cols), F32)] * 3,
        compiler_params=_cparams(("parallel",)),
    )(w, g, m, v)


ALL7 = (1, 2, 3, 4, 5, 6, 7)
CHIPS = (2, 4, 6)


def _all_gather(name, src, masks):
    bits = 0
    for m in masks:
        bits |= m
    nslots = {7: 8, 6: 4}[bits]
    nm = len(masks)

    def slot_of(x, y, c):
        return {7: 4 * x + 2 * y + c, 6: 2 * x + y}[bits]

    def body(src_ref, out_ref, send_sems, recv_sems, local_sem):
        x, y, c = lax.axis_index("x"), lax.axis_index("y"), lax.axis_index("c")
        mine = slot_of(x, y, c)
        own = pltpu.make_async_copy(src_ref, out_ref.at[mine], local_sem)
        own.start()
        copies = []
        for i, m in enumerate(masks):
            peer = _peer(x, y, c, m)
            copies.append((
                pltpu.make_async_remote_copy(
                    src_ref=src_ref, dst_ref=out_ref.at[mine], send_sem=send_sems.at[i], recv_sem=recv_sems.at[i],
                    device_id=peer, device_id_type=MESH),
                pltpu.make_async_remote_copy(
                    src_ref=src_ref, dst_ref=out_ref.at[slot_of(*peer)], send_sem=send_sems.at[i],
                    recv_sem=recv_sems.at[i], device_id=peer, device_id_type=MESH)))
        for send, _ in copies:
            send.start()
        for _, arrival in copies:
            arrival.wait_recv()
        for send, _ in copies:
            send.wait_send()
        own.wait()

    return pl.pallas_call(
        body, name=name,
        in_specs=[pl.BlockSpec(memory_space=pl.ANY)], out_specs=pl.BlockSpec(memory_space=pl.ANY),
        out_shape=jax.ShapeDtypeStruct((nslots,) + tuple(src.shape), src.dtype),
        scratch_shapes=[pltpu.SemaphoreType.DMA((nm,)), pltpu.SemaphoreType.DMA((nm,)), pltpu.SemaphoreType.DMA],
    )(src)


def _peer(x, y, c, m):
    return (1 - x if m & 4 else x, 1 - y if m & 2 else y, 1 - c if m & 1 else c)


def _comm_call(name, emit, srcs, out_shapes, n_sems):
    n = len(srcs)

    def body(*refs):
        src_refs, out_refs = refs[:n], refs[n:n + len(out_shapes)]
        send_sems, recv_sems = refs[-2], refs[-1]

        def copy(src, dst, i, peer):
            return pltpu.make_async_remote_copy(src_ref=src, dst_ref=dst, send_sem=send_sems.at[i],
                                                recv_sem=recv_sems.at[i], device_id=peer, device_id_type=MESH)

        emit(lax.axis_index("x"), lax.axis_index("y"), lax.axis_index("c"), src_refs, out_refs, copy)

    hbm = pl.BlockSpec(memory_space=pl.ANY)
    return pl.pallas_call(
        body, name=name, in_specs=[hbm] * n, out_specs=[hbm] * len(out_shapes), out_shape=out_shapes,
        scratch_shapes=[pltpu.SemaphoreType.DMA((n_sems,)), pltpu.SemaphoreType.DMA((n_sems,))],
    )(*srcs)


def _gather_weights(halves):
    n = len(halves)

    def emit(x, y, c, srcs, outs, copy):
        chip = 2 * x + y
        sib = (x, y, 1 - c)
        first, relay, landed, relayed = [], [], [], []
        for j, m in enumerate(CHIPS):
            px, py, _ = _peer(x, y, c, m)
            theirs = 2 * px + py
            for k in range(n):
                i = 6 * k + j
                first.append(copy(srcs[k].at[c], outs[k].at[chip, c], i, (px, py, c)))
                landed.append(copy(srcs[k].at[c], outs[k].at[theirs, c], i, (px, py, c)))
                relay.append(copy(outs[k].at[theirs, c], outs[k].at[theirs, c], i + 3, sib))
                relayed.append(copy(outs[k].at[theirs, 1 - c], outs[k].at[theirs, 1 - c], i + 3, sib))
        for cp in first:
            cp.start()
        for arrival, onward in zip(landed, relay):
            arrival.wait_recv()
            onward.start()
        for arrival in relayed:
            arrival.wait_recv()
        for cp in first + relay:
            cp.wait_send()

    shapes = [jax.ShapeDtypeStruct((4,) + h.shape, h.dtype) for h in halves]
    return _comm_call("gather_weights", emit, halves, shapes, 6 * n)


def _pair_exchange(chunks):
    def emit(x, y, c, srcs, outs, copy):
        sib = (x, y, 1 - c)
        sends = [copy(s.at[:, 1 - c], o, k, sib) for k, (s, o) in enumerate(zip(srcs, outs))]
        for cp in sends:
            cp.start()
        for cp in sends:
            cp.wait_recv()
        for cp in sends:
            cp.wait_send()

    shapes = [jax.ShapeDtypeStruct((4,) + g.shape[2:], g.dtype) for g in chunks]
    return _comm_call("reduce_pair_exchange", emit, chunks, shapes, len(chunks))


def _chips_alltoall(parts):
    def emit(x, y, c, srcs, outs, copy):
        chip = 2 * x + y
        sends, arrivals = [], []
        for j, m in enumerate(CHIPS):
            px, py, _ = _peer(x, y, c, m)
            theirs = 2 * px + py
            for k, (s, o) in enumerate(zip(srcs, outs)):
                sends.append(copy(s.at[theirs], o.at[chip], 3 * k + j, (px, py, c)))
                arrivals.append(copy(s.at[theirs], o.at[theirs], 3 * k + j, (px, py, c)))
        for cp in sends:
            cp.start()
        for cp in arrivals:
            cp.wait_recv()
        for cp in sends:
            cp.wait_send()

    shapes = [jax.ShapeDtypeStruct(p.shape, p.dtype) for p in parts]
    return _comm_call("reduce_chips_exchange", emit, parts, shapes, 3 * len(parts))


def _share_sibling(parts):
    def emit(x, y, c, srcs, outs, copy):
        sib = (x, y, 1 - c)
        sends = [copy(s, o.at[c], k, sib) for k, (s, o) in enumerate(zip(srcs, outs))]
        arrivals = [copy(s, o.at[1 - c], k, sib) for k, (s, o) in enumerate(zip(srcs, outs))]
        for cp in sends:
            cp.start()
        for cp in arrivals:
            cp.wait_recv()
        for cp in sends:
            cp.wait_send()

    shapes = [jax.ShapeDtypeStruct((2,) + p.shape, p.dtype) for p in parts]
    return _comm_call("share_sibling", emit, parts, shapes, len(parts))


def _reduce_pair(name, a, b):
    rows = a.shape[0] * a.shape[1]
    cols = a.shape[2]
    rt = _div_tile(rows, max(16, (1 << 20) // (4 * cols)), 16)
    spec = (rt, cols)
    (out,) = _tiled(name, lambda u, v: (u + v,), rows // rt,
                    [(a.reshape(rows, cols), spec, "row"), (b.reshape(rows, cols), spec, "row")],
                    [((rows, cols), BF16, spec, "row")])
    return out.reshape(a.shape)


def _reduce_quad(name, q):
    _, h, cols = q.shape
    rt = _div_tile(h, max(16, (1 << 20) // (4 * cols)), 16)

    def body(q_ref, o_ref):
        v = q_ref[...].astype(F32)
        o_ref[...] = ((v[0] + v[1]) + v[2]) + v[3]

    return pl.pallas_call(
        body, name=name, grid=(h // rt,),
        in_specs=[pl.BlockSpec((4, rt, cols), lambda i: (0, i, 0))],
        out_specs=pl.BlockSpec((rt, cols), lambda i: (i, 0)),
        out_shape=jax.ShapeDtypeStruct((h, cols), F32),
        compiler_params=_cparams(("parallel",)),
    )(q)


def _unshard(seg, kind):
    n, r, c = seg.shape
    if kind == "col":
        return seg.transpose(1, 0, 2).reshape(r, n * c)
    return seg.reshape(n * r, c)


def _pad_rows(flat, rows):
    n, ln = flat.shape
    return jnp.pad(flat, ((0, 0), (0, rows * PACK_COLS - ln))).reshape(n, rows, PACK_COLS)


def _block_diag_pairs(w):
    n2, bs, _ = w.shape
    eye = jnp.eye(2, dtype=w.dtype)
    z = w.reshape(n2 // 2, 2, bs, 1, bs) * eye[None, :, None, :, None]
    return z.reshape(n2 // 2, 2 * bs, 2 * bs).transpose(1, 0, 2).reshape(2 * bs, n2 * bs)


def _block_diag_pairs_t(d, bs=64):
    n = d.shape[1] // (2 * bs)
    z = d.reshape(2 * bs, n, 2 * bs).transpose(1, 0, 2).reshape(n, 2, bs, 2, bs)
    return jnp.stack([z[:, 0, :, 0, :], z[:, 1, :, 1, :]], axis=1).reshape(2 * n, bs, bs)


BIG = (("w_in", "col"), ("w_uq", "col"), ("w_ukv", "col"), ("w_proj_rnn", "row"), ("w_proj_mla", "row"),
       ("w_out", "row"), ("w_up", "col"), ("w_down", "row"))
CONVS = (("conv_w", "col"), ("ffn_conv_w", "col"))
SMALL = ("b_ada", "norm1_g", "conv_b", "w_gate_a", "b_gate_a", "w_gate_x", "b_gate_x", "lru_param",
         "q_norm_g", "kv_norm_g", "norm2_g", "ffn_conv_b", "final_g")
WEIGHTS = ("w_ada", "b_ada", "norm1_g", "w_in", "conv_w", "conv_b", "w_gate_a", "b_gate_a", "w_gate_x",
           "b_gate_x", "lru_param", "q_norm_g", "w_uq", "kv_norm_g", "w_ukv", "w_proj_rnn", "w_proj_mla",
           "w_out", "norm2_g", "w_up", "ffn_conv_w", "ffn_conv_b", "w_down", "final_g")


def _step(x, c, positions, w, m_in, v_in, loss_target):
    s_len, d = x.shape[1], x.shape[2]
    x2d = x[0]
    tgt = loss_target[0]
    xi, yi, ci = lax.axis_index("x"), lax.axis_index("y"), lax.axis_index("c")
    chip = 2 * xi + yi
    me = 2 * chip + ci
    tile = min(256, s_len)
    nt = s_len // tile

    local2d = {k: w[k][0] for k, _ in BIG + CONVS}
    halves_bf = [local2d[k].astype(BF16).reshape(2, local2d[k].shape[0] // 2, local2d[k].shape[1]) for k, _ in BIG]
    full = {}
    for (k, kind), g, hb in zip(BIG, _gather_weights(halves_bf), halves_bf):
        g = lax.dynamic_update_index_in_dim(g, hb[None], chip, 0).reshape((4,) + local2d[k].shape)
        if k == "w_up":
            full["w_up_gate"], full["w_up_val"] = _unshard(g[:2], kind), _unshard(g[2:], kind)
        else:
            full[k] = _unshard(g, kind)
    conv_flat = jnp.concatenate([local2d[k].reshape(-1) for k, _ in CONVS])
    conv_rows = -(-conv_flat.shape[0] // PACK_COLS)
    conv_all = _all_gather("gather_conv_w", _pad_rows(conv_flat[None], conv_rows)[0], CHIPS)
    conv_all = conv_all.reshape(4, -1)
    off = 0
    for k, kind in CONVS:
        r, cc = local2d[k].shape
        full[k] = _unshard(conv_all[:, off:off + r * cc].reshape(4, r, cc), kind)
        off += r * cc

    d_rnn = w["conv_b"].shape[1]
    n_q, n_kv = w["q_norm_g"].shape[1], w["kv_norm_g"].shape[1]
    w_in = full["w_in"]
    o1, o2, o3 = d_rnn + n_q, d_rnn + n_q + n_kv, d_rnn + n_q + n_kv + QK_ROPE
    w_rnn = w_in[:, :d_rnn]
    zpad = lambda n: jnp.zeros((d, n), BF16)
    w_qkv = jnp.concatenate([w_in[:, d_rnn:o2], zpad(QK_NOPE), w_in[:, o2:o3], zpad(LANE - QK_NOPE - QK_ROPE)], axis=1)
    w_g = w_in[:, o3:]
    hd = QK_NOPE + QK_ROPE
    w_uq = jnp.pad(full["w_uq"].reshape(n_q, N_HEADS, hd), ((0, 0), (0, 0), (0, HEAD_PAD - hd))).reshape(n_q, -1)
    w_ukv = full["w_ukv"]
    w_pr = full["w_proj_rnn"]
    v_head = w_ukv.shape[1] // N_HEADS - QK_NOPE
    w_pm = jnp.pad(full["w_proj_mla"].reshape(N_HEADS, v_head, d), ((0, 0), (HEAD_PAD - v_head, 0), (0, 0))).reshape(-1, d)
    w_out = full["w_out"]
    w_up_gate, w_up_val = full["w_up_gate"], full["w_up_val"]
    w_down = full["w_down"]
    d_ff = w_down.shape[0]
    ffn_cw_gate, ffn_cw_val = full["ffn_conv_w"][:, :d_ff], full["ffn_conv_w"][:, d_ff:]
    ffn_cb_gate, ffn_cb_val = w["ffn_conv_b"][:, :d_ff], w["ffn_conv_b"][:, d_ff:]
    conv_w, conv_b = full["conv_w"], w["conv_b"]
    wa_bd = _block_diag_pairs(w["w_gate_a"][0])
    wx_bd = _block_diag_pairs(w["w_gate_x"][0])

    c_all = _all_gather("gather_c", c, ALL7).reshape(8, d)
    c_rows = 128
    (c_act,) = _tiled("silu_c", lambda v: (_silu(v),), 1, [(jnp.pad(c_all, ((0, c_rows - 8), (0, 0))), (c_rows, d), "full")],
                      [((c_rows, d), F32, (c_rows, d), "full")])
    w_ada = w["w_ada"][0]
    n_mod = w_ada.shape[1]
    b_loc = lax.dynamic_slice_in_dim(w["b_ada"], chip * n_mod, n_mod, axis=1)
    mod_loc = _mm("ada_fwd", c_act, w_ada, add=jnp.broadcast_to(b_loc, (c_rows, n_mod)))
    mod_all = _all_gather("gather_mod", mod_loc[:8], CHIPS)
    mod = lax.dynamic_index_in_dim(mod_all, me, 1, keepdims=False).reshape(1, -1)
    shift1, scale1, gate1, shift2, scale2, gate2 = [mod[:, i * d:(i + 1) * d] for i in range(6)]

    half = QK_ROPE // 2
    inv_freq = ROPE_THETA ** (-jnp.arange(half, dtype=F32) / half)
    ang = positions[0].astype(F32)[:, None] * inv_freq
    cos, sin = jnp.cos(ang), jnp.sin(ang)
    one, zero = jnp.ones((s_len, QK_NOPE), F32), jnp.zeros((s_len, half), F32)
    tail = jnp.zeros((s_len, LANE - QK_NOPE - QK_ROPE), F32)
    cos_f = jnp.concatenate([one, cos, cos, tail + 1.0], axis=1)
    sin_a = jnp.concatenate([one * 0.0, -sin, zero, tail], axis=1)
    sin_b = jnp.concatenate([one * 0.0, zero, sin, tail], axis=1)
    reset = (positions[0] == 0).astype(F32)[:, None]
    tabs = [(cos_f, (tile, LANE), "row"), (sin_a, (tile, LANE), "row"), (sin_b, (tile, LANE), "row")]

    def rowspec(a):
        return (a, (tile, a.shape[1]), "row")

    def full2(a):
        return (a, a.shape, "full")

    def rowout(cols, dt):
        return ((s_len, cols), dt, (tile, cols), "row")

    def accout(a):
        return (a.shape, F32, a.shape, "acc")

    norm1_g, norm2_g, final_g = w["norm1_g"], w["norm2_g"], w["final_g"].reshape(1, d)
    ln1_in = [rowspec(x2d), full2(norm1_g), full2(scale1), full2(shift1)]
    (h1,) = _tiled("ln1", _f_ln, nt, ln1_in, [rowout(d, BF16)])
    x_rnn = _mm("in_rnn", h1, w_rnn)
    qkv = _mm("in_qkv", h1, w_qkv)
    gates = _mm("in_gates", h1, w_g)

    ct = LANE
    n_ct = d_rnn // ct
    colspec = lambda a, width=ct: (a, (a.shape[0], width), "col")
    lru_in = [colspec(x_rnn), colspec(conv_w), colspec(conv_b), colspec(wa_bd), colspec(w["b_gate_a"]),
              colspec(wx_bd), colspec(w["b_gate_x"]), colspec(w["lru_param"]), full2(reset)]
    y_rnn, h_rnn = _tiled("lru_fwd", _f_lru_fwd, n_ct, lru_in,
                          [((s_len, d_rnn), BF16, (s_len, ct), "col"), ((s_len, d_rnn), F32, (s_len, ct), "col")])

    qkv_in = [rowspec(qkv)] + tabs + [full2(w["q_norm_g"]), full2(w["kv_norm_g"])]
    qn, kvn, kr = _tiled("qkv_norm", _f_qkv, nt, qkv_in, [rowout(n_q, BF16), rowout(n_kv, BF16), rowout(LANE, BF16)])
    q_pre = _mm("up_q", qn, w_uq)
    kv = _mm("up_kv", kvn, w_ukv, out_dtype=BF16)
    (q_cat,) = _tiled("rot_q", _f_rotq, nt, [rowspec(q_pre)] + tabs, [rowout(q_pre.shape[1], BF16)])
    o_mla, lse = _attn_fwd(q_cat, kv, kr)

    p_rnn = _mm("proj_rnn", y_rnn, w_pr)
    p_mla = _mm("proj_mla", o_mla, w_pm)
    merge_in = [rowspec(gates), rowspec(p_rnn), rowspec(p_mla)]
    (merged,) = _tiled("merge", _f_merge, nt, merge_in, [rowout(d, BF16)])
    o_tok = _mm("out_proj", merged, w_out)
    res_in = [rowspec(x2d), rowspec(o_tok), full2(gate1), full2(norm2_g), full2(scale2), full2(shift2)]
    x1, h2 = _tiled("res_ln2", _f_res_ln, nt, res_in, [rowout(d, F32), rowout(d, BF16)])
    u_gate = _mm("ffn_up_gate", h2, w_up_gate)
    u_val = _mm("ffn_up_val", h2, w_up_val)
    n_ft = d_ff // LANE
    ffn_in = [colspec(a) for a in (u_gate, u_val, ffn_cw_gate, ffn_cw_val, ffn_cb_gate, ffn_cb_val)]
    (act,) = _tiled("ffn_conv", _f_ffn, n_ft, ffn_in, [((s_len, d_ff), BF16, (s_len, LANE), "col")])
    f_tok = _mm("ffn_down", act, w_down)

    loss_in = [rowspec(x1), rowspec(f_tok), rowspec(tgt), full2(gate2), full2(final_g)]
    dx1, df, loss_row, d_gate2, d_final_g = _tiled(
        "loss", _f_loss_and_grads, nt, loss_in,
        [rowout(d, F32), rowout(d, BF16), ((1, LANE), F32, (1, LANE), "acc"), accout(gate2), accout(final_g)])
    loss = lax.psum(loss_row[0, 0], ("x", "y", "c"))

    d_act = _mm("ffn_down_dx", df, w_down, tb=True)
    g_w_down = _mm("ffn_down_dw", act, df, ta=True)
    taps = ffn_cw_gate.shape[0]
    du_gate, du_val, g_cw_gate, g_cw_val, g_cb_gate, g_cb_val = _tiled(
        "ffn_conv_bwd", _vjp_of(_f_ffn, 6, (0, 1, 2, 3, 4, 5)), n_ft, ffn_in + [colspec(d_act)],
        [((s_len, d_ff), BF16, (s_len, LANE), "col")] * 2 + [((taps, d_ff), F32, (taps, LANE), "col")] * 2
        + [((1, d_ff), F32, (1, LANE), "col")] * 2)
    dh2 = _mm("ffn_up_gate_dx", du_gate, w_up_gate, tb=True)
    dh2 = _mm("ffn_up_val_dx", du_val, w_up_val, tb=True, add=dh2)
    g_w_up = jnp.concatenate([_mm("ffn_up_gate_dw", h2, du_gate, ta=True), _mm("ffn_up_val_dw", h2, du_val, ta=True)], axis=1)
    g_ffn_cw = jnp.concatenate([g_cw_gate, g_cw_val], axis=1)
    g_ffn_cb = jnp.concatenate([g_cb_gate, g_cb_val], axis=1)

    res_bwd = _vjp_of(_f_res_ln, 6, (0, 1, 2, 3, 4, 5))
    dx_res, do_tok, d_gate1, g_norm2, d_scale2, d_shift2 = _tiled(
        "res_ln2_bwd", res_bwd, nt, res_in + [rowspec(dx1), rowspec(dh2)],
        [rowout(d, F32), rowout(d, BF16), accout(gate1), accout(norm2_g), accout(scale2), accout(shift2)])
    d_merged = _mm("out_proj_dx", do_tok, w_out, tb=True)
    g_w_out = _mm("out_proj_dw", merged, do_tok, ta=True)
    d_gates, dp_rnn, dp_mla = _tiled(
        "merge_bwd", _f_merge_bwd, nt, merge_in + [rowspec(d_merged)],
        [rowout(gates.shape[1], BF16), rowout(d, BF16), rowout(d, BF16)])
    dy_rnn = _mm("proj_rnn_dx", dp_rnn, w_pr, tb=True)
    g_w_pr = _mm("proj_rnn_dw", y_rnn, dp_rnn, ta=True)
    do_mla = _mm("proj_mla_dx", dp_mla, w_pm, tb=True, out_dtype=BF16)
    g_w_pm = _mm("proj_mla_dw", o_mla, dp_mla, ta=True)

    dq_cat, dkv, dkr = _attn_bwd(q_cat, kv, kr, o_mla, lse, do_mla)
    rot_bwd = _vjp_of(_f_rotq, 4, (0,))
    (dq_pre,) = _tiled("rot_q_bwd", rot_bwd, nt, [rowspec(q_pre)] + tabs + [rowspec(dq_cat)],
                       [rowout(q_pre.shape[1], BF16)])
    dqn = _mm("up_q_dx", dq_pre, w_uq, tb=True)
    g_w_uq = _mm("up_q_dw", qn, dq_pre, ta=True)
    dkv_b = dkv.astype(BF16)
    dkvn = _mm("up_kv_dx", dkv_b, w_ukv, tb=True)
    g_w_ukv = _mm("up_kv_dw", kvn, dkv_b, ta=True)
    dqkv, g_q_norm, g_kv_norm = _tiled(
        "qkv_norm_bwd", _f_qkv_bwd, nt, qkv_in + [rowspec(dqn), rowspec(dkvn), rowspec(dkr)],
        [rowout(qkv.shape[1], BF16), accout(w["q_norm_g"]), accout(w["kv_norm_g"])])

    lru_out = [((s_len, d_rnn), BF16, (s_len, ct), "col")]
    for a in (conv_w, conv_b, wa_bd, w["b_gate_a"], wx_bd, w["b_gate_x"], w["lru_param"]):
        lru_out.append((a.shape, F32, (a.shape[0], ct), "col"))
    dx_rnn, g_conv_w, g_conv_b, g_wa_bd, g_b_a, g_wx_bd, g_b_x, g_lru = _tiled(
        "lru_bwd", _f_lru_bwd, n_ct, lru_in + [colspec(h_rnn), colspec(dy_rnn)], lru_out)

    dh1 = _mm("in_gates_dx", d_gates, w_g, tb=True)
    dh1 = _mm("in_qkv_dx", dqkv, w_qkv, tb=True, add=dh1)
    dh1 = _mm("in_rnn_dx", dx_rnn, w_rnn, tb=True, add=dh1)
    g_w_rnn = _mm("in_rnn_dw", h1, dx_rnn, ta=True)
    g_w_qkv = _mm("in_qkv_dw", h1, dqkv, ta=True)
    g_w_g = _mm("in_gates_dw", h1, d_gates, ta=True)

    ln_bwd = _vjp_of(_f_ln, 4, (0, 1, 2, 3))

    def ln1_bwd(xv, gv, sc, sh, dxr, dh):
        dx, dg, dsc, dsh = ln_bwd(xv, gv, sc, sh, dh)
        return dx + dxr, dg, dsc, dsh

    grad_x, g_norm1, d_scale1, d_shift1 = _tiled(
        "ln1_bwd", ln1_bwd, nt, ln1_in + [rowspec(dx_res), rowspec(dh1)],
        [rowout(d, F32), accout(norm1_g), accout(scale1), accout(shift1)])

    dmod = jnp.concatenate([d_shift1, d_scale1, d_gate1, d_shift2, d_scale2, d_gate2], axis=1)
    dmod_all = _all_gather("gather_dmod", dmod, ALL7).reshape(8, -1)
    dmod_loc = lax.dynamic_slice_in_dim(dmod_all, chip * n_mod, n_mod, axis=1)
    g_w_ada = _mm("ada_dw", c_act, jnp.pad(dmod_loc, ((0, c_rows - 8), (0, 0))), ta=True)

    g_full = {
        "w_in": jnp.concatenate([g_w_rnn, g_w_qkv[:, :n_q + n_kv],
                                 g_w_qkv[:, n_q + n_kv + QK_NOPE:n_q + n_kv + QK_NOPE + QK_ROPE], g_w_g], axis=1),
        "w_uq": g_w_uq.reshape(n_q, N_HEADS, HEAD_PAD)[:, :, :hd].reshape(n_q, -1),
        "w_ukv": g_w_ukv,
        "w_proj_rnn": g_w_pr,
        "w_proj_mla": g_w_pm.reshape(N_HEADS, HEAD_PAD, d)[:, HEAD_PAD - v_head:, :].reshape(-1, d),
        "w_out": g_w_out,
        "w_up": g_w_up,
        "w_down": g_w_down,
        "conv_w": g_conv_w,
        "ffn_conv_w": g_ffn_cw,
    }
    g_small = {
        "b_ada": dmod, "norm1_g": g_norm1, "conv_b": g_conv_b,
        "w_gate_a": _block_diag_pairs_t(g_wa_bd)[None], "b_gate_a": g_b_a,
        "w_gate_x": _block_diag_pairs_t(g_wx_bd)[None], "b_gate_x": g_b_x, "lru_param": g_lru,
        "q_norm_g": g_q_norm, "kv_norm_g": g_kv_norm, "norm2_g": g_norm2,
        "ffn_conv_b": g_ffn_cb, "final_g": d_final_g.reshape(w["final_g"].shape),
    }

    chunks = []
    for k, kind in BIG:
        r, cc = local2d[k].shape
        gk = g_full[k]
        if kind == "col":
            gk = gk.reshape(r, 4, cc).transpose(1, 0, 2)
        chunks.append(gk.reshape(4, 2, r // 2, cc))
    small_flat = jnp.concatenate([g_small[k].reshape(-1) for k in SMALL] + [g_full[k].reshape(-1) for k, _ in CONVS])
    small_rows = -(-small_flat.shape[0] // (8 * PACK_COLS * PACK_ROW_UNIT)) * PACK_ROW_UNIT
    chunks.append(_pad_rows(small_flat[None], 8 * small_rows).reshape(4, 2, small_rows, PACK_COLS))
    pair_sums = []
    for k, (ck, from_sib) in enumerate(zip(chunks, _pair_exchange(chunks))):
        ours = lax.dynamic_index_in_dim(ck, ci, 1, keepdims=False)
        pair_sums.append(_reduce_pair("reduce_pair_%d" % k, ours, from_sib))
    reduced = []
    for k, (quad, ps) in enumerate(zip(_chips_alltoall(pair_sums), pair_sums)):
        quad = lax.dynamic_update_index_in_dim(quad, lax.dynamic_index_in_dim(ps, chip, 0, keepdims=True), chip, 0)
        reduced.append(_reduce_quad("reduce_quad_%d" % k, quad))
    grads = {}
    for (k, _), both, own in zip(BIG, _share_sibling(reduced[:-1]), reduced):
        grads[k] = lax.dynamic_update_index_in_dim(both, own[None], ci, 0).reshape(w[k].shape)
    small_grad = _all_gather("share_small", reduced[-1], ALL7).reshape(-1)
    off = 0
    for k in SMALL:
        grads[k] = small_grad[off:off + w[k].size].reshape(w[k].shape)
        off += w[k].size
    for k, _ in CONVS:
        r, cc = local2d[k].shape
        whole = small_grad[off:off + 4 * r * cc].reshape(r, 4 * cc)
        grads[k] = lax.dynamic_slice_in_dim(whole, chip * cc, cc, axis=1)[None]
        off += 4 * r * cc
    grads["w_ada"] = g_w_ada[None]

    delta, new_m, new_v = {}, {}, {}
    for k in WEIGHTS:
        shp = w[k].shape
        two_d = (-1, shp[-1]) if len(shp) > 1 else (1, -1)
        dk, mk, vk = _adamw("adamw_" + k, w[k].reshape(two_d), grads[k].reshape(two_d),
                            m_in[k].reshape(two_d), v_in[k].reshape(two_d))
        delta[k], new_m[k], new_v[k] = dk.reshape(shp), mk.reshape(shp), vk.reshape(shp)

    return (loss, grad_x[None], *[grads[k] for k in WEIGHTS], *[delta[k] for k in WEIGHTS],
            *[new_m[k] for k in WEIGHTS], *[new_v[k] for k in WEIGHTS])


def kernel(x, c, positions, w_ada, b_ada, norm1_g, w_in, conv_w, conv_b, w_gate_a, b_gate_a, w_gate_x, b_gate_x, lru_param, q_norm_g, w_uq, kv_norm_g, w_ukv, w_proj_rnn, w_proj_mla, w_out, norm2_g, w_up, ffn_conv_w, ffn_conv_b, w_down, final_g, loss_target, m_w_ada, m_b_ada, m_norm1_g, m_w_in, m_conv_w, m_conv_b, m_w_gate_a, m_b_gate_a, m_w_gate_x, m_b_gate_x, m_lru_param, m_q_norm_g, m_w_uq, m_kv_norm_g, m_w_ukv, m_w_proj_rnn, m_w_proj_mla, m_w_out, m_norm2_g, m_w_up, m_ffn_conv_w, m_ffn_conv_b, m_w_down, m_final_g, v_w_ada, v_b_ada, v_norm1_g, v_w_in, v_conv_w, v_conv_b, v_w_gate_a, v_b_gate_a, v_w_gate_x, v_b_gate_x, v_lru_param, v_q_norm_g, v_w_uq, v_kv_norm_g, v_w_ukv, v_w_proj_rnn, v_w_proj_mla, v_w_out, v_norm2_g, v_w_up, v_ffn_conv_w, v_ffn_conv_b, v_w_down, v_final_g):
    given = dict(locals())
    w = {k: given[k] for k in WEIGHTS}
    m_in = {k: given["m_" + k] for k in WEIGHTS}
    v_in = {k: given["v_" + k] for k in WEIGHTS}
    return _step(x, c, positions, w, m_in, v_in, loss_target)
```

```python
import functools
import math

import jax
import jax.numpy as jnp
from jax import lax
from jax.experimental import pallas as pl
from jax.experimental.pallas import tpu as pltpu

F32 = jnp.float32
BF16 = jnp.bfloat16

EPS = 1e-6
LRU_C = 8.0
N_HEADS = 16
QK_NOPE = 64
QK_ROPE = 32
HEAD_PAD = 128
ROPE_THETA = 10000.0
ADAM_LR = 0.001
ADAM_B1 = 0.9
ADAM_B2 = 0.999
ADAM_EPS = 1e-08
ADAM_WD = 0.01
ADAM_STEP = 10

LANE = 128
SUBLANES = 8
VMEM_LIMIT = 48 * 1024 * 1024
MM_TILE_M = MM_TILE_N = MM_TILE_K = 1408
PACK_COLS = 1024
PACK_ROW_UNIT = 32
MESH = pl.DeviceIdType.MESH

NN = (((1,), (0,)), ((), ()))
NT = (((1,), (1,)), ((), ()))


def _cparams(sem):
    return pltpu.CompilerParams(dimension_semantics=sem, vmem_limit_bytes=VMEM_LIMIT)


def _div_tile(n, cap, unit):
    best = None
    d = unit
    while d <= min(n, cap):
        if n % d == 0:
            best = d
        d += unit
    return n if best is None else best


def _mm(name, a, b, *, ta=False, tb=False, add=None, out_dtype=F32):
    if ta:
        kdim, m = a.shape
    else:
        m, kdim = a.shape
    if tb:
        n, kb = b.shape
    else:
        kb, n = b.shape
    assert kdim == kb, (name, a.shape, b.shape)
    tm = _div_tile(m, MM_TILE_M, 8 if not ta else LANE)
    tn = _div_tile(n, MM_TILE_N, LANE)
    tk = _div_tile(kdim, MM_TILE_K, LANE)
    nk = kdim // tk
    a_spec = pl.BlockSpec((tk, tm), lambda i, j, k: (k, i)) if ta else pl.BlockSpec((tm, tk), lambda i, j, k: (i, k))
    b_spec = pl.BlockSpec((tn, tk), lambda i, j, k: (j, k)) if tb else pl.BlockSpec((tk, tn), lambda i, j, k: (k, j))
    o_spec = pl.BlockSpec((tm, tn), lambda i, j, k: (i, j))
    has_add = add is not None
    dims = ((((0,) if ta else (1,)), ((1,) if tb else (0,))), ((), ()))

    def body(*refs):
        a_ref, b_ref = refs[0], refs[1]
        c_ref = refs[2] if has_add else None
        o_ref = refs[3] if has_add else refs[2]
        prod = lax.dot_general(a_ref[...].astype(BF16), b_ref[...].astype(BF16), dims, preferred_element_type=F32)
        if nk == 1:
            o_ref[...] = (prod + c_ref[...].astype(F32) if has_add else prod).astype(o_ref.dtype)
            return
        acc = refs[-1]
        k = pl.program_id(2)

        @pl.when(k == 0)
        def _():
            acc[...] = prod + c_ref[...].astype(F32) if has_add else prod

        @pl.when(jnp.logical_and(k > 0, k < nk - 1))
        def _():
            acc[...] += prod

        @pl.when(k == nk - 1)
        def _():
            o_ref[...] = (acc[...] + prod).astype(o_ref.dtype)

    ins = [a, b] + ([add] if has_add else [])
    specs = [a_spec, b_spec] + ([o_spec] if has_add else [])
    return pl.pallas_call(
        body, name=name, grid=(m // tm, n // tn, nk), in_specs=specs, out_specs=o_spec,
        out_shape=jax.ShapeDtypeStruct((m, n), out_dtype),
        scratch_shapes=[pltpu.VMEM((tm, tn), F32)] if nk > 1 else [],
        compiler_params=_cparams(("parallel", "parallel", "arbitrary")),
    )(*ins)


_IMAPS = {
    "row": lambda i: (i, 0),
    "col": lambda i: (0, i),
    "full": lambda i: (0, 0),
    "acc": lambda i: (0, 0),
}


def _tiled(name, fn, n, ins, outs):
    ni = len(ins)
    is_acc = [k == "acc" for *_, k in outs]

    def body(*refs):
        vals = fn(*[r[...] for r in refs[:ni]])
        orefs = refs[ni:]
        if any(is_acc):
            @pl.when(pl.program_id(0) == 0)
            def _():
                for r, a in zip(orefs, is_acc):
                    if a:
                        r[...] = jnp.zeros(r.shape, r.dtype)
        for r, v, a in zip(orefs, vals, is_acc):
            if a:
                r[...] += v.astype(r.dtype)
            else:
                r[...] = v.astype(r.dtype)

    res = pl.pallas_call(
        body, name=name, grid=(n,),
        in_specs=[pl.BlockSpec(bs, _IMAPS[k]) for _, bs, k in ins],
        out_specs=[pl.BlockSpec(bs, _IMAPS[k]) for _, _, bs, k in outs],
        out_shape=[jax.ShapeDtypeStruct(s, d) for s, d, _, _ in outs],
        compiler_params=_cparams(("arbitrary",)),
    )(*[a for a, _, _ in ins])
    return tuple(res)


def _vjp_of(fn, nin, diff):
    def g(*args):
        ins, cots = args[:nin], args[nin:]

        def f(*d):
            full = list(ins)
            for i, v in zip(diff, d):
                full[i] = v
            return fn(*full)

        outs, vjp = jax.vjp(f, *[ins[i] for i in diff])
        return vjp(tuple(c.astype(o.dtype) for c, o in zip(cots, outs)))
    return g


def _shift_rows(x, k, fill, up=False):
    n = x.shape[0]
    if k % SUBLANES == 0:
        pad = jnp.full((k,) + x.shape[1:], fill, x.dtype)
        return jnp.concatenate([x[k:], pad], axis=0) if up else jnp.concatenate([pad, x[:n - k]], axis=0)
    rows = lax.broadcasted_iota(jnp.int32, x.shape, 0)
    if up:
        return jnp.where(rows < n - k, pltpu.roll(x, n - k, 0), fill)
    return jnp.where(rows >= k, pltpu.roll(x, k, 0), fill)


@functools.partial(jax.custom_vjp, nondiff_argnums=(1,))
def _delay(x, k):
    return _shift_rows(x, k, 0.0)


def _delay_fwd(x, k):
    return _shift_rows(x, k, 0.0), None


def _delay_bwd(k, _, g):
    return (_shift_rows(g, k, 0.0, up=True),)


_delay.defvjp(_delay_fwd, _delay_bwd)


@functools.partial(jax.custom_vjp, nondiff_argnums=(1,))
def _lane_roll(x, s):
    return pltpu.roll(x, s, 1)


def _lane_roll_fwd(x, s):
    return pltpu.roll(x, s, 1), None


def _lane_roll_bwd(s, _, g):
    return (pltpu.roll(g, g.shape[1] - s, 1),)


_lane_roll.defvjp(_lane_roll_fwd, _lane_roll_bwd)


@jax.custom_vjp
def _bdot(x, w):
    return lax.dot_general(x.astype(BF16), w.astype(BF16), NN, preferred_element_type=F32)


def _bdot_fwd(x, w):
    return _bdot(x, w), (x, w)


def _bdot_bwd(res, g):
    x, w = res
    gb = g.astype(BF16)
    dx = lax.dot_general(gb, w.astype(BF16), NT, preferred_element_type=F32)
    dw = lax.dot_general(x.T.astype(BF16), gb, NN, preferred_element_type=F32)
    return dx, dw


_bdot.defvjp(_bdot_fwd, _bdot_bwd)


def _sigmoid(x):
    return 0.5 * (jnp.tanh(0.5 * x) + 1.0)


def _silu(x):
    return x * _sigmoid(x)


def _rms(x, g):
    return x * lax.rsqrt(jnp.mean(x * x, axis=-1, keepdims=True) + EPS) * g


def _causal_conv(x, w, b):
    kw = w.shape[0]
    tap = lax.broadcasted_iota(jnp.int32, w.shape, 0)
    y = b
    for k in range(kw):
        d = kw - 1 - k
        wk = jnp.sum(jnp.where(tap == k, w, 0.0), axis=0, keepdims=True)
        y = y + wk * (x if d == 0 else _delay(x, d))
    return y


def _rotate(x, cos_f, sin_a, sin_b):
    reps = x.shape[1] // LANE
    if reps > 1:
        cos_f, sin_a, sin_b = (jnp.tile(t, (1, reps)) for t in (cos_f, sin_a, sin_b))
    n = x.shape[1]
    half = QK_ROPE // 2
    return x * cos_f + _lane_roll(x, n - half) * sin_a + _lane_roll(x, half) * sin_b


def _softplus_neg(l):
    u = jnp.exp(-jnp.abs(l))
    log1p_u = jnp.where(u < 0.01, u * (1.0 - u * (0.5 - u * (1.0 / 3.0))), jnp.log(1.0 + u))
    return jnp.maximum(-l, 0.0) + log1p_u


def _f_ln(x, g, scale, shift):
    return (_rms(x, g) * (1.0 + scale) + shift,)


def _f_qkv(qkv, cos_f, sin_a, sin_b, qg, kvg):
    nq, nkv = qg.shape[1], kvg.shape[1]
    qn = _rms(qkv[:, :nq], qg)
    kvn = _rms(qkv[:, nq:nq + nkv], kvg)
    kr = _rotate(qkv[:, nq + nkv:], cos_f, sin_a, sin_b)
    return qn, kvn, kr


def _f_qkv_bwd(qkv, cos_f, sin_a, sin_b, qg, kvg, dqn, dkvn, dkr):
    nq, nkv = qg.shape[1], kvg.shape[1]
    _, vjp_q = jax.vjp(_rms, qkv[:, :nq], qg)
    _, vjp_kv = jax.vjp(_rms, qkv[:, nq:nq + nkv], kvg)
    _, vjp_r = jax.vjp(lambda t: _rotate(t, cos_f, sin_a, sin_b), qkv[:, nq + nkv:])
    dq_lat, dqg = vjp_q(dqn)
    dkv_lat, dkvg = vjp_kv(dkvn)
    (dkr_pre,) = vjp_r(dkr)
    return jnp.concatenate([dq_lat, dkv_lat, dkr_pre], axis=1), dqg, dkvg


def _f_rotq(q, cos_f, sin_a, sin_b):
    return (_rotate(q, cos_f, sin_a, sin_b),)


def _merge(g_rnn, g_mla, p_rnn, p_mla):
    return _sigmoid(g_rnn) * p_rnn + _sigmoid(g_mla) * p_mla


def _f_merge(g, p_rnn, p_mla):
    d = p_rnn.shape[1]
    return (_merge(g[:, :d], g[:, d:], p_rnn, p_mla),)


def _f_merge_bwd(g, p_rnn, p_mla, dm):
    d = p_rnn.shape[1]
    _, vjp = jax.vjp(_merge, g[:, :d], g[:, d:], p_rnn, p_mla)
    dg_rnn, dg_mla, dp_rnn, dp_mla = vjp(dm)
    return jnp.concatenate([dg_rnn, dg_mla], axis=1), dp_rnn, dp_mla


def _f_res_ln(x, o, gate, g2, scale, shift):
    x1 = x + gate * o
    return x1, _rms(x1, g2) * (1.0 + scale) + shift


def _f_ffn(u_gate, u_val, cw_gate, cw_val, cb_gate, cb_val):
    return (_silu(_causal_conv(u_gate, cw_gate, cb_gate)) * _causal_conv(u_val, cw_val, cb_val),)


def _f_loss(x1, f, tgt, gate, fg):
    y = _rms(x1 + gate * f, fg)
    err = (y - tgt) * (y - tgt)
    return 0.5 * jnp.sum(jnp.mean(err, axis=-1, keepdims=True), axis=0, keepdims=True)


def _f_loss_and_grads(x1, f, tgt, gate, fg):
    loss, vjp = jax.vjp(lambda a, b, c, d: _f_loss(a, b, tgt, c, d), x1, f, gate, fg)
    dx1, df, dgate, dfg = vjp(jnp.ones((1, 1), F32))
    return dx1, df, jnp.broadcast_to(loss, (1, LANE)), dgate, dfg


def _f_lru_coeffs(xr, cw, cb, wa, ba, wx, bx, lru, reset):
    xc = _causal_conv(xr, cw, cb)
    r = _sigmoid(_bdot(xc, wa) + ba)
    i = _sigmoid(_bdot(xc, wx) + bx)
    log_a = (-LRU_C) * r * _softplus_neg(lru)
    a = jnp.exp(log_a)
    mult = jnp.sqrt(-jnp.tanh(log_a) * (1.0 + a * a))
    is_reset = reset > 0.5
    a = jnp.where(is_reset, 0.0, a)
    mult = jnp.where(is_reset, 1.0, mult)
    return a, mult * (i * xc)


def _scan(a, b, up=False):
    n = a.shape[0]
    k = 1
    while k < n:
        b = b + a * _shift_rows(b, k, 0.0, up)
        if 2 * k < n:
            a = a * _shift_rows(a, k, 1.0, up)
        k *= 2
    return b


def _f_lru_fwd(xr, cw, cb, wa, ba, wx, bx, lru, reset):
    a, b = _f_lru_coeffs(xr, cw, cb, wa, ba, wx, bx, lru, reset)
    h = _scan(a, b)
    return h, h


def _f_lru_bwd(xr, cw, cb, wa, ba, wx, bx, lru, reset, h, dh):
    (a, _), vjp = jax.vjp(lambda *p: _f_lru_coeffs(*p, reset), xr, cw, cb, wa, ba, wx, bx, lru)
    g = _scan(_shift_rows(a, 1, 0.0, up=True), dh, up=True)
    return vjp((g * _shift_rows(h, 1, 0.0), g))


def _attn_tile(s):
    return 1024 if s >= 2048 else s // 2


def _keys(kv, kr):
    lane = lax.broadcasted_iota(jnp.int32, kv.shape, 1)
    return jnp.where(lane < QK_NOPE, kv, kr)


ATTN_HEADS_PER_STEP = 2


def _scores(q, kc, scale, diagonal):
    s = lax.dot_general(q, kc, NT, preferred_element_type=F32) * scale
    if not diagonal:
        return s
    rows = lax.broadcasted_iota(jnp.int32, s.shape, 0)
    cols = lax.broadcasted_iota(jnp.int32, s.shape, 1)
    return jnp.where(cols <= rows, s, -jnp.inf)


def _causal_pairs(nb, k_major):
    if k_major:
        pairs = [(qb, kb) for kb in range(nb) for qb in range(kb, nb)]
    else:
        pairs = [(qb, kb) for qb in range(nb) for kb in range(qb + 1)]
    return jnp.array([p[0] for p in pairs], jnp.int32), jnp.array([p[1] for p in pairs], jnp.int32)


def _attn_fwd(q, kv, kr):
    s_len = q.shape[0]
    t = _attn_tile(s_len)
    nb = s_len // t
    hp = ATTN_HEADS_PER_STEP
    wide = hp * HEAD_PAD
    scale = 1.0 / math.sqrt(QK_NOPE + QK_ROPE)
    q_tab, k_tab = _causal_pairs(nb, k_major=False)

    def body(qt, kt, q_ref, kv_ref, kr_ref, o_ref, lse_ref, m_s, l_s, acc_s):
        pair = pl.program_id(1)
        qi, ki = qt[pair], kt[pair]

        @pl.when(ki == 0)
        def _():
            m_s[...] = jnp.full(m_s.shape, -jnp.inf, F32)
            l_s[...] = jnp.zeros(l_s.shape, F32)
            acc_s[...] = jnp.zeros(acc_s.shape, F32)

        def step(diagonal):
            krv = kr_ref[...]
            for h in range(hp):
                lanes = slice(h * HEAD_PAD, (h + 1) * HEAD_PAD)
                kvv = kv_ref[:, lanes]
                s = _scores(q_ref[:, lanes], _keys(kvv, krv), scale, diagonal)
                m_old = m_s[h]
                m_new = jnp.maximum(m_old, jnp.max(s, axis=-1, keepdims=True))
                alpha = jnp.exp(m_old - m_new)
                p = jnp.exp(s - m_new)
                l_s[h] = alpha * l_s[h] + jnp.sum(p, axis=-1, keepdims=True)
                acc_s[:, lanes] = alpha * acc_s[:, lanes] + lax.dot_general(
                    p.astype(BF16), kvv, NN, preferred_element_type=F32)
                m_s[h] = m_new

        @pl.when(ki < qi)
        def _():
            step(False)

        @pl.when(ki == qi)
        def _():
            step(True)
            lane = lax.broadcasted_iota(jnp.int32, (t, HEAD_PAD), 1)
            for h in range(hp):
                lanes = slice(h * HEAD_PAD, (h + 1) * HEAD_PAD)
                o_ref[:, lanes] = jnp.where(lane >= QK_NOPE, acc_s[:, lanes] / l_s[h], 0.0).astype(o_ref.dtype)
                lse_ref[h] = m_s[h] + jnp.log(l_s[h])

    grid_spec = pltpu.PrefetchScalarGridSpec(
        num_scalar_prefetch=2, grid=(N_HEADS // hp, q_tab.shape[0]),
        in_specs=[pl.BlockSpec((t, wide), lambda h, p, qt, kt: (qt[p], h)),
                  pl.BlockSpec((t, wide), lambda h, p, qt, kt: (kt[p], h)),
                  pl.BlockSpec((t, HEAD_PAD), lambda h, p, qt, kt: (kt[p], 0))],
        out_specs=[pl.BlockSpec((t, wide), lambda h, p, qt, kt: (qt[p], h)),
                   pl.BlockSpec((hp, t, 1), lambda h, p, qt, kt: (h, qt[p], 0))],
        scratch_shapes=[pltpu.VMEM((hp, t, 1), F32), pltpu.VMEM((hp, t, 1), F32), pltpu.VMEM((t, wide), F32)])
    return pl.pallas_call(
        body, name="attn_fwd", grid_spec=grid_spec,
        out_shape=[jax.ShapeDtypeStruct((s_len, N_HEADS * HEAD_PAD), BF16),
                   jax.ShapeDtypeStruct((N_HEADS, s_len, 1), F32)],
        compiler_params=_cparams(("arbitrary", "arbitrary")),
    )(q_tab, k_tab, q, kv, kr)


def _attn_bwd(q, kv, kr, o, lse, do):
    s_len = q.shape[0]
    t = _attn_tile(s_len)
    nb = s_len // t
    hp = ATTN_HEADS_PER_STEP
    wide = hp * HEAD_PAD
    scale = 1.0 / math.sqrt(QK_NOPE + QK_ROPE)
    q_tab, k_tab = _causal_pairs(nb, k_major=True)

    def body(qt, kt, q_ref, kv_ref, kr_ref, o_ref, lse_ref, do_ref, dq_ref, dkv_ref, dkr_ref, dk_s, dv_s):
        g, pair = pl.program_id(0), pl.program_id(1)
        qb, kb = qt[pair], kt[pair]

        @pl.when(jnp.logical_and(g == 0, pair == 0))
        def _():
            dkr_ref[...] = jnp.zeros(dkr_ref.shape, F32)

        @pl.when(pair == 0)
        def _():
            dq_ref[...] = jnp.zeros(dq_ref.shape, F32)

        @pl.when(qb == kb)
        def _():
            dk_s[...] = jnp.zeros(dk_s.shape, F32)
            dv_s[...] = jnp.zeros(dv_s.shape, F32)

        def step(diagonal):
            krv = kr_ref[...]
            rows = pl.ds(pl.multiple_of(qb * t, t), t)
            for h in range(hp):
                lanes = slice(h * HEAD_PAD, (h + 1) * HEAD_PAD)
                qv, kvv, dov = q_ref[:, lanes], kv_ref[:, lanes], do_ref[:, lanes]
                kc = _keys(kvv, krv)
                p = jnp.exp(_scores(qv, kc, scale, diagonal) - lse_ref[h])
                delta = jnp.sum(dov.astype(F32) * o_ref[:, lanes].astype(F32), axis=-1, keepdims=True)
                dp = lax.dot_general(dov, kvv, NT, preferred_element_type=F32)
                ds = p * (dp - delta) * scale
                dv_s[:, lanes] += lax.dot_general(p.T.astype(BF16), dov, NN, preferred_element_type=F32)
                dk_s[:, lanes] += lax.dot_general(ds.T.astype(BF16), qv, NN, preferred_element_type=F32)
                dq_ref[rows, lanes] += lax.dot_general(ds.astype(BF16), kc, NN, preferred_element_type=F32)

        @pl.when(qb > kb)
        def _():
            step(False)

        @pl.when(qb == kb)
        def _():
            step(True)

        @pl.when(qb == nb - 1)
        def _():
            lane = lax.broadcasted_iota(jnp.int32, (t, HEAD_PAD), 1)
            rows = pl.ds(pl.multiple_of(kb * t, t), t)
            for h in range(hp):
                lanes = slice(h * HEAD_PAD, (h + 1) * HEAD_PAD)
                dkv_ref[:, lanes] = jnp.where(lane < QK_NOPE, dk_s[:, lanes], dv_s[:, lanes])
                dkr_ref[rows, :] += jnp.where(lane >= QK_NOPE, dk_s[:, lanes], 0.0)

    all_lanes = N_HEADS * HEAD_PAD
    qmap = lambda h, p, qt, kt: (qt[p], h)
    kmap = lambda h, p, qt, kt: (kt[p], h)
    grid_spec = pltpu.PrefetchScalarGridSpec(
        num_scalar_prefetch=2, grid=(N_HEADS // hp, q_tab.shape[0]),
        in_specs=[pl.BlockSpec((t, wide), qmap),
                  pl.BlockSpec((t, wide), kmap),
                  pl.BlockSpec((t, HEAD_PAD), lambda h, p, qt, kt: (kt[p], 0)),
                  pl.BlockSpec((t, wide), qmap),
                  pl.BlockSpec((hp, t, 1), lambda h, p, qt, kt: (h, qt[p], 0)),
                  pl.BlockSpec((t, wide), qmap)],
        out_specs=[pl.BlockSpec((s_len, wide), lambda h, p, qt, kt: (0, h)),
                   pl.BlockSpec((t, wide), kmap),
                   pl.BlockSpec((s_len, HEAD_PAD), lambda h, p, qt, kt: (0, 0))],
        scratch_shapes=[pltpu.VMEM((t, wide), F32), pltpu.VMEM((t, wide), F32)])
    return pl.pallas_call(
        body, name="attn_bwd", grid_spec=grid_spec,
        out_shape=[jax.ShapeDtypeStruct((s_len, all_lanes), F32),
                   jax.ShapeDtypeStruct((s_len, all_lanes), F32),
                   jax.ShapeDtypeStruct((s_len, HEAD_PAD), F32)],
        compiler_params=_cparams(("arbitrary", "arbitrary")),
    )(q_tab, k_tab, q, kv, kr, o, lse, do)


def _adamw(name, w, g, m, v):
    rows, cols = w.shape
    tr = _div_tile(rows, max(8, (2 * 1024 * 1024) // (4 * cols)), 8)

    def body(w_ref, g_ref, m_ref, v_ref, d_ref, nm_ref, nv_ref):
        gv = g_ref[...]
        nm = ADAM_B1 * m_ref[...] + (1.0 - ADAM_B1) * gv
        nv = ADAM_B2 * v_ref[...] + (1.0 - ADAM_B2) * jnp.square(gv)
        m_hat = nm / (1.0 - ADAM_B1 ** ADAM_STEP)
        v_hat = nv / (1.0 - ADAM_B2 ** ADAM_STEP)
        d_ref[...] = -ADAM_LR * (m_hat / (jnp.sqrt(v_hat) + ADAM_EPS) + ADAM_WD * w_ref[...])
        nm_ref[...] = nm
        nv_ref[...] = nv

    spec = pl.BlockSpec((tr, cols), lambda i: (i, 0))
    return pl.pallas_call(
        body, name=name, grid=(rows // tr,), in_specs=[spec] * 4, out_specs=[spec] * 3,
        out_shape=[jax.ShapeDtypeStruct((rows, cols), F32)] * 3,
        compiler_params=_cparams(("parallel",)),
    )(w, g, m, v)


ALL7 = (1, 2, 3, 4, 5, 6, 7)
CHIPS = (2, 4, 6)


def _all_gather(name, src, masks):
    bits = 0
    for m in masks:
        bits |= m
    nslots = {7: 8, 6: 4}[bits]
    nm = len(masks)

    def slot_of(x, y, c):
        return {7: 4 * x + 2 * y + c, 6: 2 * x + y}[bits]

    def body(src_ref, out_ref, send_sems, recv_sems, local_sem):
        x, y, c = lax.axis_index("x"), lax.axis_index("y"), lax.axis_index("c")
        mine = slot_of(x, y, c)
        own = pltpu.make_async_copy(src_ref, out_ref.at[mine], local_sem)
        own.start()
        copies = []
        for i, m in enumerate(masks):
            peer = _peer(x, y, c, m)
            copies.append((
                pltpu.make_async_remote_copy(
                    src_ref=src_ref, dst_ref=out_ref.at[mine], send_sem=send_sems.at[i], recv_sem=recv_sems.at[i],
                    device_id=peer, device_id_type=MESH),
                pltpu.make_async_remote_copy(
                    src_ref=src_ref, dst_ref=out_ref.at[slot_of(*peer)], send_sem=send_sems.at[i],
                    recv_sem=recv_sems.at[i], device_id=peer, device_id_type=MESH)))
        for send, _ in copies:
            send.start()
        for _, arrival in copies:
            arrival.wait_recv()
        for send, _ in copies:
            send.wait_send()
        own.wait()

    return pl.pallas_call(
        body, name=name,
        in_specs=[pl.BlockSpec(memory_space=pl.ANY)], out_specs=pl.BlockSpec(memory_space=pl.ANY),
        out_shape=jax.ShapeDtypeStruct((nslots,) + tuple(src.shape), src.dtype),
        scratch_shapes=[pltpu.SemaphoreType.DMA((nm,)), pltpu.SemaphoreType.DMA((nm,)), pltpu.SemaphoreType.DMA],
    )(src)


def _peer(x, y, c, m):
    return (1 - x if m & 4 else x, 1 - y if m & 2 else y, 1 - c if m & 1 else c)


def _comm_call(name, emit, srcs, out_shapes, n_sems, in_place=False):
    n = len(srcs)

    def body(*refs):
        src_refs, out_refs = refs[:n], refs[n:n + len(out_shapes)]
        send_sems, recv_sems = refs[-2], refs[-1]

        def copy(src, dst, i, peer):
            return pltpu.make_async_remote_copy(src_ref=src, dst_ref=dst, send_sem=send_sems.at[i],
                                                recv_sem=recv_sems.at[i], device_id=peer, device_id_type=MESH)

        emit(lax.axis_index("x"), lax.axis_index("y"), lax.axis_index("c"), src_refs, out_refs, copy)

    hbm = pl.BlockSpec(memory_space=pl.ANY)
    return pl.pallas_call(
        body, name=name, in_specs=[hbm] * n, out_specs=[hbm] * len(out_shapes), out_shape=out_shapes,
        scratch_shapes=[pltpu.SemaphoreType.DMA((n_sems,)), pltpu.SemaphoreType.DMA((n_sems,))],
        input_output_aliases={i: i for i in range(n)} if in_place else {},
    )(*srcs)


HBM_SPEC = pl.BlockSpec(memory_space=pltpu.HBM)
SEM_SPEC = pl.BlockSpec(memory_space=pltpu.SEMAPHORE)
DATAFLOW = pltpu.SideEffectType.DATAFLOW_SIDE_EFFECTING


def _chip_copies(srcs, lands, send_sems, recv_sems, mode):
    x, y, c = lax.axis_index("x"), lax.axis_index("y"), lax.axis_index("c")
    chip = 2 * x + y
    sends, arrivals = [], []
    for j, m in enumerate(CHIPS):
        px, py, _ = _peer(x, y, c, m)
        theirs = 2 * px + py
        for k, (s, l) in enumerate(zip(srcs, lands)):
            if mode == "gather":
                src, dst, got = s.at[c], l.at[chip, c], l.at[theirs, c]
            else:
                src, dst, got = s.at[theirs], l.at[chip], l.at[theirs]
            for to, group in ((dst, sends), (got, arrivals)):
                group.append(pltpu.make_async_remote_copy(
                    src_ref=src, dst_ref=to, send_sem=send_sems.at[3 * k + j], recv_sem=recv_sems.at[3 * k + j],
                    device_id=(px, py, c), device_id_type=MESH))
    return sends, arrivals


def _split_start(name, srcs, land_shapes, mode, after):
    n = len(srcs)

    def body(*refs):
        sends, _ = _chip_copies(refs[:n], refs[n:2 * n], refs[2 * n + 1], refs[2 * n + 2], mode)
        for cp in sends:
            cp.start()
        token = refs[-1]
        token[...] = jnp.zeros(token.shape, token.dtype)

    hbm = lambda a: pltpu.with_memory_space_constraint(a, pltpu.HBM)
    lands = [hbm(lax.empty(s.shape, s.dtype)) for s in land_shapes]
    bufs = [pltpu.HBM(a.shape, a.dtype) for a in list(srcs) + lands]
    res = pl.pallas_call(
        body, name=name,
        out_shape=(pltpu.SemaphoreType.DMA((3 * n,)), pltpu.SemaphoreType.DMA((3 * n,)), *bufs,
                   jax.ShapeDtypeStruct((SUBLANES, LANE), F32)),
        in_specs=[HBM_SPEC] * (2 * n) + [pl.BlockSpec(memory_space=pl.ANY)],
        out_specs=[SEM_SPEC, SEM_SPEC] + [HBM_SPEC] * (2 * n) + [pl.BlockSpec(memory_space=pltpu.VMEM)],
        input_output_aliases={i: 2 + i for i in range(2 * n)},
        compiler_params=pltpu.CompilerParams(has_side_effects=DATAFLOW),
    )(*[hbm(s) for s in srcs], *lands, after)
    return res[0], res[1], res[2:2 + n], res[2 + n:2 + 2 * n], res[-1]


def _split_wait(name, send_sems, recv_sems, srcs, lands, mode, after):
    n = len(srcs)

    def body(*refs):
        sends, arrivals = _chip_copies(refs[:n], refs[n:2 * n], refs[2 * n], refs[2 * n + 1], mode)
        for cp in sends:
            cp.wait_send()
        for cp in arrivals:
            cp.wait_recv()

    res = pl.pallas_call(
        body, name=name,
        out_shape=tuple(pltpu.HBM(a.shape, a.dtype) for a in list(srcs) + list(lands)),
        in_specs=[HBM_SPEC] * (2 * n) + [SEM_SPEC, SEM_SPEC, pl.BlockSpec(memory_space=pl.ANY)],
        out_specs=[HBM_SPEC] * (2 * n),
        input_output_aliases={i: i for i in range(2 * n)},
        compiler_params=pltpu.CompilerParams(has_side_effects=DATAFLOW),
    )(*srcs, *lands, send_sems, recv_sems, after)
    return res[n:]


def _relay_sibling(lands):
    def emit(x, y, c, srcs, outs, copy):
        sib = (x, y, 1 - c)
        sends, arrivals = [], []
        for j, m in enumerate(CHIPS):
            px, py, _ = _peer(x, y, c, m)
            theirs = 2 * px + py
            for k, (s, o) in enumerate(zip(srcs, outs)):
                sends.append(copy(s.at[theirs, c], o.at[theirs, c], 3 * k + j, sib))
                arrivals.append(copy(s.at[theirs, c], o.at[theirs, 1 - c], 3 * k + j, sib))
        for cp in sends:
            cp.start()
        for cp in arrivals:
            cp.wait_recv()
        for cp in sends:
            cp.wait_send()

    shapes = [jax.ShapeDtypeStruct(l.shape, l.dtype) for l in lands]
    return _comm_call("relay_weights", emit, lands, shapes, 3 * len(lands), in_place=True)


def _gather_weights(halves):
    n = len(halves)

    def emit(x, y, c, srcs, outs, copy):
        chip = 2 * x + y
        sib = (x, y, 1 - c)
        first, relay, landed, relayed = [], [], [], []
        for j, m in enumerate(CHIPS):
            px, py, _ = _peer(x, y, c, m)
            theirs = 2 * px + py
            for k in range(n):
                i = 6 * k + j
                first.append(copy(srcs[k].at[c], outs[k].at[chip, c], i, (px, py, c)))
                landed.append(copy(srcs[k].at[c], outs[k].at[theirs, c], i, (px, py, c)))
                relay.append(copy(outs[k].at[theirs, c], outs[k].at[theirs, c], i + 3, sib))
                relayed.append(copy(outs[k].at[theirs, 1 - c], outs[k].at[theirs, 1 - c], i + 3, sib))
        for cp in first:
            cp.start()
        for arrival, onward in zip(landed, relay):
            arrival.wait_recv()
            onward.start()
        for arrival in relayed:
            arrival.wait_recv()
        for cp in first + relay:
            cp.wait_send()

    shapes = [jax.ShapeDtypeStruct((4,) + h.shape, h.dtype) for h in halves]
    return _comm_call("gather_weights", emit, halves, shapes, 6 * n)


def _pair_exchange(name, chunks):
    def emit(x, y, c, srcs, outs, copy):
        sib = (x, y, 1 - c)
        sends = [copy(s.at[:, 1 - c], o, k, sib) for k, (s, o) in enumerate(zip(srcs, outs))]
        for cp in sends:
            cp.start()
        for cp in sends:
            cp.wait_recv()
        for cp in sends:
            cp.wait_send()

    shapes = [jax.ShapeDtypeStruct((4,) + g.shape[2:], g.dtype) for g in chunks]
    return _comm_call(name, emit, chunks, shapes, len(chunks))


def _chips_alltoall(parts):
    def emit(x, y, c, srcs, outs, copy):
        chip = 2 * x + y
        sends, arrivals = [], []
        for j, m in enumerate(CHIPS):
            px, py, _ = _peer(x, y, c, m)
            theirs = 2 * px + py
            for k, (s, o) in enumerate(zip(srcs, outs)):
                sends.append(copy(s.at[theirs], o.at[chip], 3 * k + j, (px, py, c)))
                arrivals.append(copy(s.at[theirs], o.at[theirs], 3 * k + j, (px, py, c)))
        for cp in sends:
            cp.start()
        for cp in arrivals:
            cp.wait_recv()
        for cp in sends:
            cp.wait_send()

    shapes = [jax.ShapeDtypeStruct(p.shape, p.dtype) for p in parts]
    return _comm_call("reduce_chips_exchange", emit, parts, shapes, 3 * len(parts))


def _share_sibling(parts):
    def emit(x, y, c, srcs, outs, copy):
        sib = (x, y, 1 - c)
        sends = [copy(s, o.at[c], k, sib) for k, (s, o) in enumerate(zip(srcs, outs))]
        arrivals = [copy(s, o.at[1 - c], k, sib) for k, (s, o) in enumerate(zip(srcs, outs))]
        for cp in sends:
            cp.start()
        for cp in arrivals:
            cp.wait_recv()
        for cp in sends:
            cp.wait_send()

    shapes = [jax.ShapeDtypeStruct((2,) + p.shape, p.dtype) for p in parts]
    return _comm_call("share_sibling", emit, parts, shapes, len(parts))


def _reduce_pair(name, a, b):
    rows = a.shape[0] * a.shape[1]
    cols = a.shape[2]
    rt = _div_tile(rows, max(16, (1 << 20) // (4 * cols)), 16)
    spec = (rt, cols)
    (out,) = _tiled(name, lambda u, v: (u + v,), rows // rt,
                    [(a.reshape(rows, cols), spec, "row"), (b.reshape(rows, cols), spec, "row")],
                    [((rows, cols), BF16, spec, "row")])
    return out.reshape(a.shape)


def _reduce_quad(name, q):
    _, h, cols = q.shape
    rt = _div_tile(h, max(16, (1 << 20) // (4 * cols)), 16)

    def body(q_ref, o_ref):
        v = q_ref[...].astype(F32)
        o_ref[...] = ((v[0] + v[1]) + v[2]) + v[3]

    return pl.pallas_call(
        body, name=name, grid=(h // rt,),
        in_specs=[pl.BlockSpec((4, rt, cols), lambda i: (0, i, 0))],
        out_specs=pl.BlockSpec((rt, cols), lambda i: (i, 0)),
        out_shape=jax.ShapeDtypeStruct((h, cols), F32),
        compiler_params=_cparams(("parallel",)),
    )(q)


def _unshard(seg, kind):
    n, r, c = seg.shape
    if kind == "col":
        return seg.transpose(1, 0, 2).reshape(r, n * c)
    return seg.reshape(n * r, c)


def _pad_rows(flat, rows):
    n, ln = flat.shape
    return jnp.pad(flat, ((0, 0), (0, rows * PACK_COLS - ln))).reshape(n, rows, PACK_COLS)


def _block_diag_pairs(w):
    n2, bs, _ = w.shape
    eye = jnp.eye(2, dtype=w.dtype)
    z = w.reshape(n2 // 2, 2, bs, 1, bs) * eye[None, :, None, :, None]
    return z.reshape(n2 // 2, 2 * bs, 2 * bs).transpose(1, 0, 2).reshape(2 * bs, n2 * bs)


def _block_diag_pairs_t(d, bs=64):
    n = d.shape[1] // (2 * bs)
    z = d.reshape(2 * bs, n, 2 * bs).transpose(1, 0, 2).reshape(n, 2, bs, 2, bs)
    return jnp.stack([z[:, 0, :, 0, :], z[:, 1, :, 1, :]], axis=1).reshape(2 * n, bs, bs)


BIG = (("w_in", "col"), ("w_uq", "col"), ("w_ukv", "col"), ("w_proj_rnn", "row"), ("w_proj_mla", "row"),
       ("w_out", "row"), ("w_up", "col"), ("w_down", "row"))
FIRST_USED = ("w_in", "w_uq", "w_ukv")
CONVS = (("conv_w", "col"), ("ffn_conv_w", "col"))
SMALL = ("b_ada", "norm1_g", "conv_b", "w_gate_a", "b_gate_a", "w_gate_x", "b_gate_x", "lru_param",
         "q_norm_g", "kv_norm_g", "norm2_g", "ffn_conv_b", "final_g")
WEIGHTS = ("w_ada", "b_ada", "norm1_g", "w_in", "conv_w", "conv_b", "w_gate_a", "b_gate_a", "w_gate_x",
           "b_gate_x", "lru_param", "q_norm_g", "w_uq", "kv_norm_g", "w_ukv", "w_proj_rnn", "w_proj_mla",
           "w_out", "norm2_g", "w_up", "ffn_conv_w", "ffn_conv_b", "w_down", "final_g")


def _step(x, c, positions, w, m_in, v_in, loss_target):
    s_len, d = x.shape[1], x.shape[2]
    x2d = x[0]
    tgt = loss_target[0]
    xi, yi, ci = lax.axis_index("x"), lax.axis_index("y"), lax.axis_index("c")
    chip = 2 * xi + yi
    me = 2 * chip + ci
    tile = min(256, s_len)
    nt = s_len // tile

    local2d = {k: w[k][0] for k, _ in BIG + CONVS}
    kinds = dict(BIG)
    halves_bf = {k: local2d[k].astype(BF16).reshape(2, local2d[k].shape[0] // 2, local2d[k].shape[1]) for k, _ in BIG}
    first_names = [k for k, _ in BIG if k in FIRST_USED]
    later_names = [k for k, _ in BIG if k not in FIRST_USED]
    full = {}

    def assemble(k, g):
        g = lax.dynamic_update_index_in_dim(g, halves_bf[k][None], chip, 0).reshape((4,) + local2d[k].shape)
        if k == "w_up":
            full["w_up_gate"], full["w_up_val"] = _unshard(g[:2], kinds[k]), _unshard(g[2:], kinds[k])
        else:
            full[k] = _unshard(g, kinds[k])

    first_got = _gather_weights([halves_bf[k] for k in first_names])
    for k, g in zip(first_names, first_got):
        assemble(k, g)
    later_flight = _split_start(
        "gather_later_start", [halves_bf[k] for k in later_names],
        [jax.ShapeDtypeStruct((4,) + halves_bf[k].shape, BF16) for k in later_names], "gather", after=first_got[0])
    conv_flat = jnp.concatenate([local2d[k].reshape(-1) for k, _ in CONVS])
    conv_rows = -(-conv_flat.shape[0] // PACK_COLS)
    conv_all = _all_gather("gather_conv_w", _pad_rows(conv_flat[None], conv_rows)[0], CHIPS)
    conv_all = conv_all.reshape(4, -1)
    off = 0
    for k, kind in CONVS:
        r, cc = local2d[k].shape
        full[k] = _unshard(conv_all[:, off:off + r * cc].reshape(4, r, cc), kind)
        off += r * cc

    d_rnn = w["conv_b"].shape[1]
    n_q, n_kv = w["q_norm_g"].shape[1], w["kv_norm_g"].shape[1]
    w_in = full["w_in"]
    o1, o2, o3 = d_rnn + n_q, d_rnn + n_q + n_kv, d_rnn + n_q + n_kv + QK_ROPE
    w_rnn = w_in[:, :d_rnn]
    zpad = lambda n: jnp.zeros((d, n), BF16)
    w_qkv = jnp.concatenate([w_in[:, d_rnn:o2], zpad(QK_NOPE), w_in[:, o2:o3], zpad(LANE - QK_NOPE - QK_ROPE)], axis=1)
    w_g = w_in[:, o3:]
    hd = QK_NOPE + QK_ROPE
    w_uq = jnp.pad(full["w_uq"].reshape(n_q, N_HEADS, hd), ((0, 0), (0, 0), (0, HEAD_PAD - hd))).reshape(n_q, -1)
    w_ukv = full["w_ukv"]
    v_head = w_ukv.shape[1] // N_HEADS - QK_NOPE
    d_ff = w["ffn_conv_b"].shape[1] // 2
    ffn_cw_gate, ffn_cw_val = full["ffn_conv_w"][:, :d_ff], full["ffn_conv_w"][:, d_ff:]
    ffn_cb_gate, ffn_cb_val = w["ffn_conv_b"][:, :d_ff], w["ffn_conv_b"][:, d_ff:]
    conv_w, conv_b = full["conv_w"], w["conv_b"]
    wa_bd = _block_diag_pairs(w["w_gate_a"][0])
    wx_bd = _block_diag_pairs(w["w_gate_x"][0])

    c_all = _all_gather("gather_c", c, ALL7).reshape(8, d)
    c_rows = 128
    (c_act,) = _tiled("silu_c", lambda v: (_silu(v),), 1, [(jnp.pad(c_all, ((0, c_rows - 8), (0, 0))), (c_rows, d), "full")],
                      [((c_rows, d), F32, (c_rows, d), "full")])
    w_ada = w["w_ada"][0]
    n_mod = w_ada.shape[1]
    b_loc = lax.dynamic_slice_in_dim(w["b_ada"], chip * n_mod, n_mod, axis=1)
    mod_loc = _mm("ada_fwd", c_act, w_ada, add=jnp.broadcast_to(b_loc, (c_rows, n_mod)))
    mod_all = _all_gather("gather_mod", mod_loc[:8], CHIPS)
    mod = lax.dynamic_index_in_dim(mod_all, me, 1, keepdims=False).reshape(1, -1)
    shift1, scale1, gate1, shift2, scale2, gate2 = [mod[:, i * d:(i + 1) * d] for i in range(6)]

    half = QK_ROPE // 2
    inv_freq = ROPE_THETA ** (-jnp.arange(half, dtype=F32) / half)
    ang = positions[0].astype(F32)[:, None] * inv_freq
    cos, sin = jnp.cos(ang), jnp.sin(ang)
    one, zero = jnp.ones((s_len, QK_NOPE), F32), jnp.zeros((s_len, half), F32)
    tail = jnp.zeros((s_len, LANE - QK_NOPE - QK_ROPE), F32)
    cos_f = jnp.concatenate([one, cos, cos, tail + 1.0], axis=1)
    sin_a = jnp.concatenate([one * 0.0, -sin, zero, tail], axis=1)
    sin_b = jnp.concatenate([one * 0.0, zero, sin, tail], axis=1)
    reset = (positions[0] == 0).astype(F32)[:, None]
    tabs = [(cos_f, (tile, LANE), "row"), (sin_a, (tile, LANE), "row"), (sin_b, (tile, LANE), "row")]

    def rowspec(a):
        return (a, (tile, a.shape[1]), "row")

    def full2(a):
        return (a, a.shape, "full")

    def rowout(cols, dt):
        return ((s_len, cols), dt, (tile, cols), "row")

    def accout(a):
        return (a.shape, F32, a.shape, "acc")

    norm1_g = w["norm1_g"] + later_flight[4][:1, :1]
    norm2_g, final_g = w["norm2_g"], w["final_g"].reshape(1, d)
    ln1_in = [rowspec(x2d), full2(norm1_g), full2(scale1), full2(shift1)]
    (h1,) = _tiled("ln1", _f_ln, nt, ln1_in, [rowout(d, BF16)])
    x_rnn = _mm("in_rnn", h1, w_rnn)
    qkv = _mm("in_qkv", h1, w_qkv)
    gates = _mm("in_gates", h1, w_g)

    ct = LANE
    n_ct = d_rnn // ct
    colspec = lambda a, width=ct: (a, (a.shape[0], width), "col")
    lru_in = [colspec(x_rnn), colspec(conv_w), colspec(conv_b), colspec(wa_bd), colspec(w["b_gate_a"]),
              colspec(wx_bd), colspec(w["b_gate_x"]), colspec(w["lru_param"]), full2(reset)]
    y_rnn, h_rnn = _tiled("lru_fwd", _f_lru_fwd, n_ct, lru_in,
                          [((s_len, d_rnn), BF16, (s_len, ct), "col"), ((s_len, d_rnn), F32, (s_len, ct), "col")])

    qkv_in = [rowspec(qkv)] + tabs + [full2(w["q_norm_g"]), full2(w["kv_norm_g"])]
    qn, kvn, kr = _tiled("qkv_norm", _f_qkv, nt, qkv_in, [rowout(n_q, BF16), rowout(n_kv, BF16), rowout(LANE, BF16)])
    q_pre = _mm("up_q", qn, w_uq)
    kv = _mm("up_kv", kvn, w_ukv, out_dtype=BF16)
    (q_cat,) = _tiled("rot_q", _f_rotq, nt, [rowspec(q_pre)] + tabs, [rowout(q_pre.shape[1], BF16)])
    o_mla, lse = _attn_fwd(q_cat, kv, kr)

    send_sems, recv_sems, flown, landed, _ = later_flight
    landed = _split_wait("gather_later_wait", send_sems, recv_sems, flown, landed, "gather", after=o_mla)
    for k, g in zip(later_names, _relay_sibling(landed)):
        assemble(k, g)
    w_pr = full["w_proj_rnn"]
    w_pm = jnp.pad(full["w_proj_mla"].reshape(N_HEADS, v_head, d), ((0, 0), (HEAD_PAD - v_head, 0), (0, 0))).reshape(-1, d)
    w_out = full["w_out"]
    w_up_gate, w_up_val = full["w_up_gate"], full["w_up_val"]
    w_down = full["w_down"]

    p_rnn = _mm("proj_rnn", y_rnn, w_pr)
    p_mla = _mm("proj_mla", o_mla, w_pm)
    merge_in = [rowspec(gates), rowspec(p_rnn), rowspec(p_mla)]
    (merged,) = _tiled("merge", _f_merge, nt, merge_in, [rowout(d, BF16)])
    o_tok = _mm("out_proj", merged, w_out)
    res_in = [rowspec(x2d), rowspec(o_tok), full2(gate1), full2(norm2_g), full2(scale2), full2(shift2)]
    x1, h2 = _tiled("res_ln2", _f_res_ln, nt, res_in, [rowout(d, F32), rowout(d, BF16)])
    u_gate = _mm("ffn_up_gate", h2, w_up_gate)
    u_val = _mm("ffn_up_val", h2, w_up_val)
    n_ft = d_ff // LANE
    ffn_in = [colspec(a) for a in (u_gate, u_val, ffn_cw_gate, ffn_cw_val, ffn_cb_gate, ffn_cb_val)]
    (act,) = _tiled("ffn_conv", _f_ffn, n_ft, ffn_in, [((s_len, d_ff), BF16, (s_len, LANE), "col")])
    f_tok = _mm("ffn_down", act, w_down)

    loss_in = [rowspec(x1), rowspec(f_tok), rowspec(tgt), full2(gate2), full2(final_g)]
    dx1, df, loss_row, d_gate2, d_final_g = _tiled(
        "loss", _f_loss_and_grads, nt, loss_in,
        [rowout(d, F32), rowout(d, BF16), ((1, LANE), F32, (1, LANE), "acc"), accout(gate2), accout(final_g)])
    loss = lax.psum(loss_row[0, 0], ("x", "y", "c"))

    d_act = _mm("ffn_down_dx", df, w_down, tb=True)
    g_w_down = _mm("ffn_down_dw", act, df, ta=True)
    taps = ffn_cw_gate.shape[0]
    du_gate, du_val, g_cw_gate, g_cw_val, g_cb_gate, g_cb_val = _tiled(
        "ffn_conv_bwd", _vjp_of(_f_ffn, 6, (0, 1, 2, 3, 4, 5)), n_ft, ffn_in + [colspec(d_act)],
        [((s_len, d_ff), BF16, (s_len, LANE), "col")] * 2 + [((taps, d_ff), F32, (taps, LANE), "col")] * 2
        + [((1, d_ff), F32, (1, LANE), "col")] * 2)
    dh2 = _mm("ffn_up_gate_dx", du_gate, w_up_gate, tb=True)
    dh2 = _mm("ffn_up_val_dx", du_val, w_up_val, tb=True, add=dh2)
    g_w_up = jnp.concatenate([_mm("ffn_up_gate_dw", h2, du_gate, ta=True), _mm("ffn_up_val_dw", h2, du_val, ta=True)], axis=1)
    g_ffn_cw = jnp.concatenate([g_cw_gate, g_cw_val], axis=1)
    g_ffn_cb = jnp.concatenate([g_cb_gate, g_cb_val], axis=1)

    res_bwd = _vjp_of(_f_res_ln, 6, (0, 1, 2, 3, 4, 5))
    dx_res, do_tok, d_gate1, g_norm2, d_scale2, d_shift2 = _tiled(
        "res_ln2_bwd", res_bwd, nt, res_in + [rowspec(dx1), rowspec(dh2)],
        [rowout(d, F32), rowout(d, BF16), accout(gate1), accout(norm2_g), accout(scale2), accout(shift2)])
    d_merged = _mm("out_proj_dx", do_tok, w_out, tb=True)
    g_w_out = _mm("out_proj_dw", merged, do_tok, ta=True)
    d_gates, dp_rnn, dp_mla = _tiled(
        "merge_bwd", _f_merge_bwd, nt, merge_in + [rowspec(d_merged)],
        [rowout(gates.shape[1], BF16), rowout(d, BF16), rowout(d, BF16)])
    dy_rnn = _mm("proj_rnn_dx", dp_rnn, w_pr, tb=True)
    g_w_pr = _mm("proj_rnn_dw", y_rnn, dp_rnn, ta=True)
    do_mla = _mm("proj_mla_dx", dp_mla, w_pm, tb=True, out_dtype=BF16)
    g_w_pm = _mm("proj_mla_dw", o_mla, dp_mla, ta=True)

    def chunked(k, gk):
        r, cc = local2d[k].shape
        if kinds[k] == "col":
            gk = gk.reshape(r, 4, cc).transpose(1, 0, 2)
        return gk.reshape(4, 2, r // 2, cc)

    def pair_sums(tag, names, chunks):
        out = []
        for k, ck, from_sib in zip(names, chunks, _pair_exchange("reduce_pair_exchange_" + tag, chunks)):
            ours = lax.dynamic_index_in_dim(ck, ci, 1, keepdims=False)
            out.append(_reduce_pair("reduce_pair_" + k, ours, from_sib))
        return out

    g_later = {
        "w_proj_rnn": g_w_pr,
        "w_proj_mla": g_w_pm.reshape(N_HEADS, HEAD_PAD, d)[:, HEAD_PAD - v_head:, :].reshape(-1, d),
        "w_out": g_w_out, "w_up": g_w_up, "w_down": g_w_down,
    }
    sums_ready = pair_sums("ready", later_names, [chunked(k, g_later[k]) for k in later_names])
    ready_flight = _split_start(
        "reduce_ready_start", sums_ready, [jax.ShapeDtypeStruct(s.shape, s.dtype) for s in sums_ready], "alltoall",
        after=sums_ready[0])
    lse = lse + ready_flight[4][0, 0]

    dq_cat, dkv, dkr = _attn_bwd(q_cat, kv, kr, o_mla, lse, do_mla)
    rot_bwd = _vjp_of(_f_rotq, 4, (0,))
    (dq_pre,) = _tiled("rot_q_bwd", rot_bwd, nt, [rowspec(q_pre)] + tabs + [rowspec(dq_cat)],
                       [rowout(q_pre.shape[1], BF16)])
    dqn = _mm("up_q_dx", dq_pre, w_uq, tb=True)
    g_w_uq = _mm("up_q_dw", qn, dq_pre, ta=True)
    dkv_b = dkv.astype(BF16)
    dkvn = _mm("up_kv_dx", dkv_b, w_ukv, tb=True)
    g_w_ukv = _mm("up_kv_dw", kvn, dkv_b, ta=True)
    dqkv, g_q_norm, g_kv_norm = _tiled(
        "qkv_norm_bwd", _f_qkv_bwd, nt, qkv_in + [rowspec(dqn), rowspec(dkvn), rowspec(dkr)],
        [rowout(qkv.shape[1], BF16), accout(w["q_norm_g"]), accout(w["kv_norm_g"])])

    lru_out = [((s_len, d_rnn), BF16, (s_len, ct), "col")]
    for a in (conv_w, conv_b, wa_bd, w["b_gate_a"], wx_bd, w["b_gate_x"], w["lru_param"]):
        lru_out.append((a.shape, F32, (a.shape[0], ct), "col"))
    dx_rnn, g_conv_w, g_conv_b, g_wa_bd, g_b_a, g_wx_bd, g_b_x, g_lru = _tiled(
        "lru_bwd", _f_lru_bwd, n_ct, lru_in + [colspec(h_rnn), colspec(dy_rnn)], lru_out)

    dh1 = _mm("in_gates_dx", d_gates, w_g, tb=True)
    dh1 = _mm("in_qkv_dx", dqkv, w_qkv, tb=True, add=dh1)
    dh1 = _mm("in_rnn_dx", dx_rnn, w_rnn, tb=True, add=dh1)
    g_w_rnn = _mm("in_rnn_dw", h1, dx_rnn, ta=True)
    g_w_qkv = _mm("in_qkv_dw", h1, dqkv, ta=True)
    g_w_g = _mm("in_gates_dw", h1, d_gates, ta=True)

    ln_bwd = _vjp_of(_f_ln, 4, (0, 1, 2, 3))

    def ln1_bwd(xv, gv, sc, sh, dxr, dh):
        dx, dg, dsc, dsh = ln_bwd(xv, gv, sc, sh, dh)
        return dx + dxr, dg, dsc, dsh

    grad_x, g_norm1, d_scale1, d_shift1 = _tiled(
        "ln1_bwd", ln1_bwd, nt, ln1_in + [rowspec(dx_res), rowspec(dh1)],
        [rowout(d, F32), accout(norm1_g), accout(scale1), accout(shift1)])

    dmod = jnp.concatenate([d_shift1, d_scale1, d_gate1, d_shift2, d_scale2, d_gate2], axis=1)
    dmod_all = _all_gather("gather_dmod", dmod, ALL7).reshape(8, -1)
    dmod_loc = lax.dynamic_slice_in_dim(dmod_all, chip * n_mod, n_mod, axis=1)
    g_w_ada = _mm("ada_dw", c_act, jnp.pad(dmod_loc, ((0, c_rows - 8), (0, 0))), ta=True)

    g_full = {
        "w_in": jnp.concatenate([g_w_rnn, g_w_qkv[:, :n_q + n_kv],
                                 g_w_qkv[:, n_q + n_kv + QK_NOPE:n_q + n_kv + QK_NOPE + QK_ROPE], g_w_g], axis=1),
        "w_uq": g_w_uq.reshape(n_q, N_HEADS, HEAD_PAD)[:, :, :hd].reshape(n_q, -1),
        "w_ukv": g_w_ukv,
        "conv_w": g_conv_w,
        "ffn_conv_w": g_ffn_cw,
    }
    g_small = {
        "b_ada": dmod, "norm1_g": g_norm1, "conv_b": g_conv_b,
        "w_gate_a": _block_diag_pairs_t(g_wa_bd)[None], "b_gate_a": g_b_a,
        "w_gate_x": _block_diag_pairs_t(g_wx_bd)[None], "b_gate_x": g_b_x, "lru_param": g_lru,
        "q_norm_g": g_q_norm, "kv_norm_g": g_kv_norm, "norm2_g": g_norm2,
        "ffn_conv_b": g_ffn_cb, "final_g": d_final_g.reshape(w["final_g"].shape),
    }

    small_flat = jnp.concatenate([g_small[k].reshape(-1) for k in SMALL] + [g_full[k].reshape(-1) for k, _ in CONVS])
    small_rows = -(-small_flat.shape[0] // (8 * PACK_COLS * PACK_ROW_UNIT)) * PACK_ROW_UNIT
    last_names = first_names + ["small"]
    last_chunks = [chunked(k, g_full[k]) for k in first_names]
    last_chunks.append(_pad_rows(small_flat[None], 8 * small_rows).reshape(4, 2, small_rows, PACK_COLS))
    sums_last = pair_sums("last", last_names, last_chunks)
    send_sems, recv_sems, flown, landed, _ = ready_flight
    quads_ready = _split_wait("reduce_ready_wait", send_sems, recv_sems, flown, landed, "alltoall", after=grad_x)
    reduced = {}
    for k, quad, ps in zip(later_names + last_names, list(quads_ready) + list(_chips_alltoall(sums_last)),
                           sums_ready + sums_last):
        quad = lax.dynamic_update_index_in_dim(quad, lax.dynamic_index_in_dim(ps, chip, 0, keepdims=True), chip, 0)
        reduced[k] = _reduce_quad("reduce_quad_" + k, quad)
    grads = {}
    for (k, _), both in zip(BIG, _share_sibling([reduced[k] for k, _ in BIG])):
        grads[k] = lax.dynamic_update_index_in_dim(both, reduced[k][None], ci, 0).reshape(w[k].shape)
    small_grad = _all_gather("share_small", reduced["small"], ALL7).reshape(-1)
    off = 0
    for k in SMALL:
        grads[k] = small_grad[off:off + w[k].size].reshape(w[k].shape)
        off += w[k].size
    for k, _ in CONVS:
        r, cc = local2d[k].shape
        whole = small_grad[off:off + 4 * r * cc].reshape(r, 4 * cc)
        grads[k] = lax.dynamic_slice_in_dim(whole, chip * cc, cc, axis=1)[None]
        off += 4 * r * cc
    grads["w_ada"] = g_w_ada[None]

    delta, new_m, new_v = {}, {}, {}
    for k in WEIGHTS:
        shp = w[k].shape
        two_d = (-1, shp[-1]) if len(shp) > 1 else (1, -1)
        dk, mk, vk = _adamw("adamw_" + k, w[k].reshape(two_d), grads[k].reshape(two_d),
                            m_in[k].reshape(two_d), v_in[k].reshape(two_d))
        delta[k], new_m[k], new_v[k] = dk.reshape(shp), mk.reshape(shp), vk.reshape(shp)

    return (loss, grad_x[None], *[grads[k] for k in WEIGHTS], *[delta[k] for k in WEIGHTS],
            *[new_m[k] for k in WEIGHTS], *[new_v[k] for k in WEIGHTS])


def kernel(x, c, positions, w_ada, b_ada, norm1_g, w_in, conv_w, conv_b, w_gate_a, b_gate_a, w_gate_x, b_gate_x, lru_param, q_norm_g, w_uq, kv_norm_g, w_ukv, w_proj_rnn, w_proj_mla, w_out, norm2_g, w_up, ffn_conv_w, ffn_conv_b, w_down, final_g, loss_target, m_w_ada, m_b_ada, m_norm1_g, m_w_in, m_conv_w, m_conv_b, m_w_gate_a, m_b_gate_a, m_w_gate_x, m_b_gate_x, m_lru_param, m_q_norm_g, m_w_uq, m_kv_norm_g, m_w_ukv, m_w_proj_rnn, m_w_proj_mla, m_w_out, m_norm2_g, m_w_up, m_ffn_conv_w, m_ffn_conv_b, m_w_down, m_final_g, v_w_ada, v_b_ada, v_norm1_g, v_w_in, v_conv_w, v_conv_b, v_w_gate_a, v_b_gate_a, v_w_gate_x, v_b_gate_x, v_lru_param, v_q_norm_g, v_w_uq, v_kv_norm_g, v_w_ukv, v_w_proj_rnn, v_w_proj_mla, v_w_out, v_norm2_g, v_w_up, v_ffn_conv_w, v_ffn_conv_b, v_w_down, v_final_g):
    given = dict(locals())
    w = {k: given[k] for k in WEIGHTS}
    m_in = {k: given["m_" + k] for k in WEIGHTS}
    v_in = {k: given["v_" + k] for k in WEIGHTS}
    return _step(x, c, positions, w, m_in, v_in, loss_target)
```

```python
import functools
import math

import jax
import jax.numpy as jnp
from jax import lax
from jax.experimental import pallas as pl
from jax.experimental.pallas import tpu as pltpu

F32 = jnp.float32
BF16 = jnp.bfloat16

EPS = 1e-6
LRU_C = 8.0
N_HEADS = 16
QK_NOPE = 64
QK_ROPE = 32
HEAD_PAD = 128
ROPE_THETA = 10000.0
ADAM_LR = 0.001
ADAM_B1 = 0.9
ADAM_B2 = 0.999
ADAM_EPS = 1e-08
ADAM_WD = 0.01
ADAM_STEP = 10

LANE = 128
SUBLANES = 8
VMEM_LIMIT = 48 * 1024 * 1024
MM_TILE_M = MM_TILE_N = MM_TILE_K = 1408
PACK_COLS = 1024
PACK_ROW_UNIT = 32
MESH = pl.DeviceIdType.MESH

NN = (((1,), (0,)), ((), ()))
NT = (((1,), (1,)), ((), ()))


def _cparams(sem):
    return pltpu.CompilerParams(dimension_semantics=sem, vmem_limit_bytes=VMEM_LIMIT)


def _div_tile(n, cap, unit):
    best = None
    d = unit
    while d <= min(n, cap):
        if n % d == 0:
            best = d
        d += unit
    return n if best is None else best


def _mm(name, a, b, *, ta=False, tb=False, add=None, out_dtype=F32):
    if ta:
        kdim, m = a.shape
    else:
        m, kdim = a.shape
    if tb:
        n, kb = b.shape
    else:
        kb, n = b.shape
    assert kdim == kb, (name, a.shape, b.shape)
    tm = _div_tile(m, MM_TILE_M, 8 if not ta else LANE)
    tn = _div_tile(n, MM_TILE_N, LANE)
    tk = _div_tile(kdim, MM_TILE_K, LANE)
    nk = kdim // tk
    a_spec = pl.BlockSpec((tk, tm), lambda i, j, k: (k, i)) if ta else pl.BlockSpec((tm, tk), lambda i, j, k: (i, k))
    b_spec = pl.BlockSpec((tn, tk), lambda i, j, k: (j, k)) if tb else pl.BlockSpec((tk, tn), lambda i, j, k: (k, j))
    o_spec = pl.BlockSpec((tm, tn), lambda i, j, k: (i, j))
    has_add = add is not None
    dims = ((((0,) if ta else (1,)), ((1,) if tb else (0,))), ((), ()))

    def body(*refs):
        a_ref, b_ref = refs[0], refs[1]
        c_ref = refs[2] if has_add else None
        o_ref = refs[3] if has_add else refs[2]
        prod = lax.dot_general(a_ref[...].astype(BF16), b_ref[...].astype(BF16), dims, preferred_element_type=F32)
        if nk == 1:
            o_ref[...] = (prod + c_ref[...].astype(F32) if has_add else prod).astype(o_ref.dtype)
            return
        acc = refs[-1]
        k = pl.program_id(2)

        @pl.when(k == 0)
        def _():
            acc[...] = prod + c_ref[...].astype(F32) if has_add else prod

        @pl.when(jnp.logical_and(k > 0, k < nk - 1))
        def _():
            acc[...] += prod

        @pl.when(k == nk - 1)
        def _():
            o_ref[...] = (acc[...] + prod).astype(o_ref.dtype)

    ins = [a, b] + ([add] if has_add else [])
    specs = [a_spec, b_spec] + ([o_spec] if has_add else [])
    return pl.pallas_call(
        body, name=name, grid=(m // tm, n // tn, nk), in_specs=specs, out_specs=o_spec,
        out_shape=jax.ShapeDtypeStruct((m, n), out_dtype),
        scratch_shapes=[pltpu.VMEM((tm, tn), F32)] if nk > 1 else [],
        compiler_params=_cparams(("parallel", "parallel", "arbitrary")),
    )(*ins)


_IMAPS = {
    "row": lambda i: (i, 0),
    "col": lambda i: (0, i),
    "full": lambda i: (0, 0),
    "acc": lambda i: (0, 0),
}


def _tiled(name, fn, n, ins, outs):
    ni = len(ins)
    is_acc = [k == "acc" for *_, k in outs]

    def body(*refs):
        vals = fn(*[r[...] for r in refs[:ni]])
        orefs = refs[ni:]
        if any(is_acc):
            @pl.when(pl.program_id(0) == 0)
            def _():
                for r, a in zip(orefs, is_acc):
                    if a:
                        r[...] = jnp.zeros(r.shape, r.dtype)
        for r, v, a in zip(orefs, vals, is_acc):
            if a:
                r[...] += v.astype(r.dtype)
            else:
                r[...] = v.astype(r.dtype)

    res = pl.pallas_call(
        body, name=name, grid=(n,),
        in_specs=[pl.BlockSpec(bs, _IMAPS[k]) for _, bs, k in ins],
        out_specs=[pl.BlockSpec(bs, _IMAPS[k]) for _, _, bs, k in outs],
        out_shape=[jax.ShapeDtypeStruct(s, d) for s, d, _, _ in outs],
        compiler_params=_cparams(("arbitrary",)),
    )(*[a for a, _, _ in ins])
    return tuple(res)


def _vjp_of(fn, nin, diff):
    def g(*args):
        ins, cots = args[:nin], args[nin:]

        def f(*d):
            full = list(ins)
            for i, v in zip(diff, d):
                full[i] = v
            return fn(*full)

        outs, vjp = jax.vjp(f, *[ins[i] for i in diff])
        return vjp(tuple(c.astype(o.dtype) for c, o in zip(cots, outs)))
    return g


def _shift_rows(x, k, fill, up=False):
    n = x.shape[0]
    if k % SUBLANES == 0:
        pad = jnp.full((k,) + x.shape[1:], fill, x.dtype)
        return jnp.concatenate([x[k:], pad], axis=0) if up else jnp.concatenate([pad, x[:n - k]], axis=0)
    rows = lax.broadcasted_iota(jnp.int32, x.shape, 0)
    if up:
        return jnp.where(rows < n - k, pltpu.roll(x, n - k, 0), fill)
    return jnp.where(rows >= k, pltpu.roll(x, k, 0), fill)


@functools.partial(jax.custom_vjp, nondiff_argnums=(1,))
def _delay(x, k):
    return _shift_rows(x, k, 0.0)


def _delay_fwd(x, k):
    return _shift_rows(x, k, 0.0), None


def _delay_bwd(k, _, g):
    return (_shift_rows(g, k, 0.0, up=True),)


_delay.defvjp(_delay_fwd, _delay_bwd)


@functools.partial(jax.custom_vjp, nondiff_argnums=(1,))
def _lane_roll(x, s):
    return pltpu.roll(x, s, 1)


def _lane_roll_fwd(x, s):
    return pltpu.roll(x, s, 1), None


def _lane_roll_bwd(s, _, g):
    return (pltpu.roll(g, g.shape[1] - s, 1),)


_lane_roll.defvjp(_lane_roll_fwd, _lane_roll_bwd)


@jax.custom_vjp
def _bdot(x, w):
    return lax.dot_general(x.astype(BF16), w.astype(BF16), NN, preferred_element_type=F32)


def _bdot_fwd(x, w):
    return _bdot(x, w), (x, w)


def _bdot_bwd(res, g):
    x, w = res
    gb = g.astype(BF16)
    dx = lax.dot_general(gb, w.astype(BF16), NT, preferred_element_type=F32)
    dw = lax.dot_general(x.T.astype(BF16), gb, NN, preferred_element_type=F32)
    return dx, dw


_bdot.defvjp(_bdot_fwd, _bdot_bwd)


def _sigmoid(x):
    return 0.5 * (jnp.tanh(0.5 * x) + 1.0)


def _silu(x):
    return x * _sigmoid(x)


def _rms(x, g):
    return x * lax.rsqrt(jnp.mean(x * x, axis=-1, keepdims=True) + EPS) * g


def _causal_conv(x, w, b):
    kw = w.shape[0]
    tap = lax.broadcasted_iota(jnp.int32, w.shape, 0)
    y = b
    for k in range(kw):
        d = kw - 1 - k
        wk = jnp.sum(jnp.where(tap == k, w, 0.0), axis=0, keepdims=True)
        y = y + wk * (x if d == 0 else _delay(x, d))
    return y


def _rotate(x, cos_f, sin_a, sin_b):
    reps = x.shape[1] // LANE
    if reps > 1:
        cos_f, sin_a, sin_b = (jnp.tile(t, (1, reps)) for t in (cos_f, sin_a, sin_b))
    n = x.shape[1]
    half = QK_ROPE // 2
    return x * cos_f + _lane_roll(x, n - half) * sin_a + _lane_roll(x, half) * sin_b


def _softplus_neg(l):
    u = jnp.exp(-jnp.abs(l))
    log1p_u = jnp.where(u < 0.01, u * (1.0 - u * (0.5 - u * (1.0 / 3.0))), jnp.log(1.0 + u))
    return jnp.maximum(-l, 0.0) + log1p_u


def _f_ln(x, g, scale, shift):
    return (_rms(x, g) * (1.0 + scale) + shift,)


def _f_qkv(qkv, cos_f, sin_a, sin_b, qg, kvg):
    nq, nkv = qg.shape[1], kvg.shape[1]
    qn = _rms(qkv[:, :nq], qg)
    kvn = _rms(qkv[:, nq:nq + nkv], kvg)
    kr = _rotate(qkv[:, nq + nkv:], cos_f, sin_a, sin_b)
    return qn, kvn, kr


def _f_qkv_bwd(qkv, cos_f, sin_a, sin_b, qg, kvg, dqn, dkvn, dkr):
    nq, nkv = qg.shape[1], kvg.shape[1]
    _, vjp_q = jax.vjp(_rms, qkv[:, :nq], qg)
    _, vjp_kv = jax.vjp(_rms, qkv[:, nq:nq + nkv], kvg)
    _, vjp_r = jax.vjp(lambda t: _rotate(t, cos_f, sin_a, sin_b), qkv[:, nq + nkv:])
    dq_lat, dqg = vjp_q(dqn)
    dkv_lat, dkvg = vjp_kv(dkvn)
    (dkr_pre,) = vjp_r(dkr)
    return jnp.concatenate([dq_lat, dkv_lat, dkr_pre], axis=1), dqg, dkvg


def _f_rotq(q, cos_f, sin_a, sin_b):
    return (_rotate(q, cos_f, sin_a, sin_b),)


def _merge(g_rnn, g_mla, p_rnn, p_mla):
    return _sigmoid(g_rnn) * p_rnn + _sigmoid(g_mla) * p_mla


def _f_merge(g, p_rnn, p_mla):
    d = p_rnn.shape[1]
    return (_merge(g[:, :d], g[:, d:], p_rnn, p_mla),)


def _f_merge_bwd(g, p_rnn, p_mla, dm):
    d = p_rnn.shape[1]
    _, vjp = jax.vjp(_merge, g[:, :d], g[:, d:], p_rnn, p_mla)
    dg_rnn, dg_mla, dp_rnn, dp_mla = vjp(dm)
    return jnp.concatenate([dg_rnn, dg_mla], axis=1), dp_rnn, dp_mla


def _f_res_ln(x, o, gate, g2, scale, shift):
    x1 = x + gate * o
    return x1, _rms(x1, g2) * (1.0 + scale) + shift


def _f_ffn(u_gate, u_val, cw_gate, cw_val, cb_gate, cb_val):
    return (_silu(_causal_conv(u_gate, cw_gate, cb_gate)) * _causal_conv(u_val, cw_val, cb_val),)


def _f_loss(x1, f, tgt, gate, fg):
    y = _rms(x1 + gate * f, fg)
    err = (y - tgt) * (y - tgt)
    return 0.5 * jnp.sum(jnp.mean(err, axis=-1, keepdims=True), axis=0, keepdims=True)


def _f_loss_and_grads(x1, f, tgt, gate, fg):
    loss, vjp = jax.vjp(lambda a, b, c, d: _f_loss(a, b, tgt, c, d), x1, f, gate, fg)
    dx1, df, dgate, dfg = vjp(jnp.ones((1, 1), F32))
    return dx1, df, jnp.broadcast_to(loss, (1, LANE)), dgate, dfg


def _f_lru_coeffs(xr, cw, cb, wa, ba, wx, bx, lru, reset):
    xc = _causal_conv(xr, cw, cb)
    r = _sigmoid(_bdot(xc, wa) + ba)
    i = _sigmoid(_bdot(xc, wx) + bx)
    log_a = (-LRU_C) * r * _softplus_neg(lru)
    a = jnp.exp(log_a)
    mult = jnp.sqrt(-jnp.tanh(log_a) * (1.0 + a * a))
    is_reset = reset > 0.5
    a = jnp.where(is_reset, 0.0, a)
    mult = jnp.where(is_reset, 1.0, mult)
    return a, mult * (i * xc)


def _scan(a, b, up=False):
    n = a.shape[0]
    k = 1
    while k < n:
        b = b + a * _shift_rows(b, k, 0.0, up)
        if 2 * k < n:
            a = a * _shift_rows(a, k, 1.0, up)
        k *= 2
    return b


def _f_lru_fwd(xr, cw, cb, wa, ba, wx, bx, lru, reset):
    a, b = _f_lru_coeffs(xr, cw, cb, wa, ba, wx, bx, lru, reset)
    h = _scan(a, b)
    return h, h


def _f_lru_bwd(xr, cw, cb, wa, ba, wx, bx, lru, reset, h, dh):
    (a, _), vjp = jax.vjp(lambda *p: _f_lru_coeffs(*p, reset), xr, cw, cb, wa, ba, wx, bx, lru)
    g = _scan(_shift_rows(a, 1, 0.0, up=True), dh, up=True)
    return vjp((g * _shift_rows(h, 1, 0.0), g))


def _attn_tile(s):
    return 1024 if s >= 2048 else s // 2


def _keys(kv, kr):
    lane = lax.broadcasted_iota(jnp.int32, kv.shape, 1)
    return jnp.where(lane < QK_NOPE, kv, kr)


ATTN_HEADS_PER_STEP = 2


def _scores(q, kc, scale, diagonal):
    s = lax.dot_general(q, kc, NT, preferred_element_type=F32) * scale
    if not diagonal:
        return s
    rows = lax.broadcasted_iota(jnp.int32, s.shape, 0)
    cols = lax.broadcasted_iota(jnp.int32, s.shape, 1)
    return jnp.where(cols <= rows, s, -jnp.inf)


def _causal_pairs(nb, k_major):
    if k_major:
        pairs = [(qb, kb) for kb in range(nb) for qb in range(kb, nb)]
    else:
        pairs = [(qb, kb) for qb in range(nb) for kb in range(qb + 1)]
    return jnp.array([p[0] for p in pairs], jnp.int32), jnp.array([p[1] for p in pairs], jnp.int32)


def _attn_fwd(q, kv, kr):
    s_len = q.shape[0]
    t = _attn_tile(s_len)
    nb = s_len // t
    hp = ATTN_HEADS_PER_STEP
    wide = hp * HEAD_PAD
    scale = 1.0 / math.sqrt(QK_NOPE + QK_ROPE)
    q_tab, k_tab = _causal_pairs(nb, k_major=False)

    def body(qt, kt, q_ref, kv_ref, kr_ref, o_ref, lse_ref, m_s, l_s, acc_s):
        pair = pl.program_id(1)
        qi, ki = qt[pair], kt[pair]

        @pl.when(ki == 0)
        def _():
            m_s[...] = jnp.full(m_s.shape, -jnp.inf, F32)
            l_s[...] = jnp.zeros(l_s.shape, F32)
            acc_s[...] = jnp.zeros(acc_s.shape, F32)

        def step(diagonal):
            krv = kr_ref[...]
            for h in range(hp):
                lanes = slice(h * HEAD_PAD, (h + 1) * HEAD_PAD)
                kvv = kv_ref[:, lanes]
                s = _scores(q_ref[:, lanes], _keys(kvv, krv), scale, diagonal)
                m_old = m_s[h]
                m_new = jnp.maximum(m_old, jnp.max(s, axis=-1, keepdims=True))
                alpha = jnp.exp(m_old - m_new)
                p = jnp.exp(s - m_new)
                l_s[h] = alpha * l_s[h] + jnp.sum(p, axis=-1, keepdims=True)
                acc_s[:, lanes] = alpha * acc_s[:, lanes] + lax.dot_general(
                    p.astype(BF16), kvv, NN, preferred_element_type=F32)
                m_s[h] = m_new

        @pl.when(ki < qi)
        def _():
            step(False)

        @pl.when(ki == qi)
        def _():
            step(True)
            lane = lax.broadcasted_iota(jnp.int32, (t, HEAD_PAD), 1)
            for h in range(hp):
                lanes = slice(h * HEAD_PAD, (h + 1) * HEAD_PAD)
                o_ref[:, lanes] = jnp.where(lane >= QK_NOPE, acc_s[:, lanes] / l_s[h], 0.0).astype(o_ref.dtype)
                lse_ref[h] = m_s[h] + jnp.log(l_s[h])

    grid_spec = pltpu.PrefetchScalarGridSpec(
        num_scalar_prefetch=2, grid=(N_HEADS // hp, q_tab.shape[0]),
        in_specs=[pl.BlockSpec((t, wide), lambda h, p, qt, kt: (qt[p], h)),
                  pl.BlockSpec((t, wide), lambda h, p, qt, kt: (kt[p], h)),
                  pl.BlockSpec((t, HEAD_PAD), lambda h, p, qt, kt: (kt[p], 0))],
        out_specs=[pl.BlockSpec((t, wide), lambda h, p, qt, kt: (qt[p], h)),
                   pl.BlockSpec((hp, t, 1), lambda h, p, qt, kt: (h, qt[p], 0))],
        scratch_shapes=[pltpu.VMEM((hp, t, 1), F32), pltpu.VMEM((hp, t, 1), F32), pltpu.VMEM((t, wide), F32)])
    return pl.pallas_call(
        body, name="attn_fwd", grid_spec=grid_spec,
        out_shape=[jax.ShapeDtypeStruct((s_len, N_HEADS * HEAD_PAD), BF16),
                   jax.ShapeDtypeStruct((N_HEADS, s_len, 1), F32)],
        compiler_params=_cparams(("arbitrary", "arbitrary")),
    )(q_tab, k_tab, q, kv, kr)


def _attn_bwd(q, kv, kr, o, lse, do):
    s_len = q.shape[0]
    t = _attn_tile(s_len)
    nb = s_len // t
    hp = ATTN_HEADS_PER_STEP
    wide = hp * HEAD_PAD
    scale = 1.0 / math.sqrt(QK_NOPE + QK_ROPE)
    q_tab, k_tab = _causal_pairs(nb, k_major=True)

    def body(qt, kt, q_ref, kv_ref, kr_ref, o_ref, lse_ref, do_ref, dq_ref, dkv_ref, dkr_ref, dk_s, dv_s):
        g, pair = pl.program_id(0), pl.program_id(1)
        qb, kb = qt[pair], kt[pair]

        @pl.when(jnp.logical_and(g == 0, pair == 0))
        def _():
            dkr_ref[...] = jnp.zeros(dkr_ref.shape, F32)

        @pl.when(pair == 0)
        def _():
            dq_ref[...] = jnp.zeros(dq_ref.shape, F32)

        @pl.when(qb == kb)
        def _():
            dk_s[...] = jnp.zeros(dk_s.shape, F32)
            dv_s[...] = jnp.zeros(dv_s.shape, F32)

        def step(diagonal):
            krv = kr_ref[...]
            rows = pl.ds(pl.multiple_of(qb * t, t), t)
            for h in range(hp):
                lanes = slice(h * HEAD_PAD, (h + 1) * HEAD_PAD)
                qv, kvv, dov = q_ref[:, lanes], kv_ref[:, lanes], do_ref[:, lanes]
                kc = _keys(kvv, krv)
                p = jnp.exp(_scores(qv, kc, scale, diagonal) - lse_ref[h])
                delta = jnp.sum(dov.astype(F32) * o_ref[:, lanes].astype(F32), axis=-1, keepdims=True)
                dp = lax.dot_general(dov, kvv, NT, preferred_element_type=F32)
                ds = p * (dp - delta) * scale
                dv_s[:, lanes] += lax.dot_general(p.T.astype(BF16), dov, NN, preferred_element_type=F32)
                dk_s[:, lanes] += lax.dot_general(ds.T.astype(BF16), qv, NN, preferred_element_type=F32)
                dq_ref[rows, lanes] += lax.dot_general(ds.astype(BF16), kc, NN, preferred_element_type=F32)

        @pl.when(qb > kb)
        def _():
            step(False)

        @pl.when(qb == kb)
        def _():
            step(True)

        @pl.when(qb == nb - 1)
        def _():
            lane = lax.broadcasted_iota(jnp.int32, (t, HEAD_PAD), 1)
            rows = pl.ds(pl.multiple_of(kb * t, t), t)
            for h in range(hp):
                lanes = slice(h * HEAD_PAD, (h + 1) * HEAD_PAD)
                dkv_ref[:, lanes] = jnp.where(lane < QK_NOPE, dk_s[:, lanes], dv_s[:, lanes])
                dkr_ref[rows, :] += jnp.where(lane >= QK_NOPE, dk_s[:, lanes], 0.0)

    all_lanes = N_HEADS * HEAD_PAD
    qmap = lambda h, p, qt, kt: (qt[p], h)
    kmap = lambda h, p, qt, kt: (kt[p], h)
    grid_spec = pltpu.PrefetchScalarGridSpec(
        num_scalar_prefetch=2, grid=(N_HEADS // hp, q_tab.shape[0]),
        in_specs=[pl.BlockSpec((t, wide), qmap),
                  pl.BlockSpec((t, wide), kmap),
                  pl.BlockSpec((t, HEAD_PAD), lambda h, p, qt, kt: (kt[p], 0)),
                  pl.BlockSpec((t, wide), qmap),
                  pl.BlockSpec((hp, t, 1), lambda h, p, qt, kt: (h, qt[p], 0)),
                  pl.BlockSpec((t, wide), qmap)],
        out_specs=[pl.BlockSpec((s_len, wide), lambda h, p, qt, kt: (0, h)),
                   pl.BlockSpec((t, wide), kmap),
                   pl.BlockSpec((s_len, HEAD_PAD), lambda h, p, qt, kt: (0, 0))],
        scratch_shapes=[pltpu.VMEM((t, wide), F32), pltpu.VMEM((t, wide), F32)])
    return pl.pallas_call(
        body, name="attn_bwd", grid_spec=grid_spec,
        out_shape=[jax.ShapeDtypeStruct((s_len, all_lanes), F32),
                   jax.ShapeDtypeStruct((s_len, all_lanes), F32),
                   jax.ShapeDtypeStruct((s_len, HEAD_PAD), F32)],
        compiler_params=_cparams(("arbitrary", "arbitrary")),
    )(q_tab, k_tab, q, kv, kr, o, lse, do)


def _adamw(name, w, g, m, v):
    rows, cols = w.shape
    tr = _div_tile(rows, max(8, (2 * 1024 * 1024) // (4 * cols)), 8)

    def body(w_ref, g_ref, m_ref, v_ref, d_ref, nm_ref, nv_ref):
        gv = g_ref[...]
        nm = ADAM_B1 * m_ref[...] + (1.0 - ADAM_B1) * gv
        nv = ADAM_B2 * v_ref[...] + (1.0 - ADAM_B2) * jnp.square(gv)
        m_hat = nm / (1.0 - ADAM_B1 ** ADAM_STEP)
        v_hat = nv / (1.0 - ADAM_B2 ** ADAM_STEP)
        d_ref[...] = -ADAM_LR * (m_hat / (jnp.sqrt(v_hat) + ADAM_EPS) + ADAM_WD * w_ref[...])
        nm_ref[...] = nm
        nv_ref[...] = nv

    spec = pl.BlockSpec((tr, cols), lambda i: (i, 0))
    return pl.pallas_call(
        body, name=name, grid=(rows // tr,), in_specs=[spec] * 4, out_specs=[spec] * 3,
        out_shape=[jax.ShapeDtypeStruct((rows, cols), F32)] * 3,
        compiler_params=_cparams(("parallel",)),
    )(w, g, m, v)


ALL7 = (1, 2, 3, 4, 5, 6, 7)
CHIPS = (2, 4, 6)


def _all_gather(name, src, masks):
    bits = 0
    for m in masks:
        bits |= m
    nslots = {7: 8, 6: 4}[bits]
    nm = len(masks)

    def slot_of(x, y, c):
        return {7: 4 * x + 2 * y + c, 6: 2 * x + y}[bits]

    def body(src_ref, out_ref, send_sems, recv_sems, local_sem):
        x, y, c = lax.axis_index("x"), lax.axis_index("y"), lax.axis_index("c")
        mine = slot_of(x, y, c)
        own = pltpu.make_async_copy(src_ref, out_ref.at[mine], local_sem)
        own.start()
        copies = []
        for i, m in enumerate(masks):
            peer = _peer(x, y, c, m)
            copies.append((
                pltpu.make_async_remote_copy(
                    src_ref=src_ref, dst_ref=out_ref.at[mine], send_sem=send_sems.at[i], recv_sem=recv_sems.at[i],
                    device_id=peer, device_id_type=MESH),
                pltpu.make_async_remote_copy(
                    src_ref=src_ref, dst_ref=out_ref.at[slot_of(*peer)], send_sem=send_sems.at[i],
                    recv_sem=recv_sems.at[i], device_id=peer, device_id_type=MESH)))
        for send, _ in copies:
            send.start()
        for _, arrival in copies:
            arrival.wait_recv()
        for send, _ in copies:
            send.wait_send()
        own.wait()

    return pl.pallas_call(
        body, name=name,
        in_specs=[pl.BlockSpec(memory_space=pl.ANY)], out_specs=pl.BlockSpec(memory_space=pl.ANY),
        out_shape=jax.ShapeDtypeStruct((nslots,) + tuple(src.shape), src.dtype),
        scratch_shapes=[pltpu.SemaphoreType.DMA((nm,)), pltpu.SemaphoreType.DMA((nm,)), pltpu.SemaphoreType.DMA],
    )(src)


def _peer(x, y, c, m):
    return (1 - x if m & 4 else x, 1 - y if m & 2 else y, 1 - c if m & 1 else c)


def _comm_call(name, emit, srcs, out_shapes, n_sems, in_place=False):
    n = len(srcs)

    def body(*refs):
        src_refs, out_refs = refs[:n], refs[n:n + len(out_shapes)]
        send_sems, recv_sems = refs[-2], refs[-1]

        def copy(src, dst, i, peer):
            return pltpu.make_async_remote_copy(src_ref=src, dst_ref=dst, send_sem=send_sems.at[i],
                                                recv_sem=recv_sems.at[i], device_id=peer, device_id_type=MESH)

        emit(lax.axis_index("x"), lax.axis_index("y"), lax.axis_index("c"), src_refs, out_refs, copy)

    hbm = pl.BlockSpec(memory_space=pl.ANY)
    return pl.pallas_call(
        body, name=name, in_specs=[hbm] * n, out_specs=[hbm] * len(out_shapes), out_shape=out_shapes,
        scratch_shapes=[pltpu.SemaphoreType.DMA((n_sems,)), pltpu.SemaphoreType.DMA((n_sems,))],
        input_output_aliases={i: i for i in range(n)} if in_place else {},
    )(*srcs)


HBM_SPEC = pl.BlockSpec(memory_space=pltpu.HBM)
SEM_SPEC = pl.BlockSpec(memory_space=pltpu.SEMAPHORE)
DATAFLOW = pltpu.SideEffectType.DATAFLOW_SIDE_EFFECTING


def _chip_copies(srcs, lands, send_sems, recv_sems, mode):
    x, y, c = lax.axis_index("x"), lax.axis_index("y"), lax.axis_index("c")
    chip = 2 * x + y
    sends, arrivals = [], []
    for j, m in enumerate(CHIPS):
        px, py, _ = _peer(x, y, c, m)
        theirs = 2 * px + py
        for k, (s, l) in enumerate(zip(srcs, lands)):
            if mode == "gather":
                src, dst, got = s.at[c], l.at[chip, c], l.at[theirs, c]
            else:
                src, dst, got = s.at[theirs], l.at[chip], l.at[theirs]
            for to, group in ((dst, sends), (got, arrivals)):
                group.append(pltpu.make_async_remote_copy(
                    src_ref=src, dst_ref=to, send_sem=send_sems.at[3 * k + j], recv_sem=recv_sems.at[3 * k + j],
                    device_id=(px, py, c), device_id_type=MESH))
    return sends, arrivals


def _split_start(name, srcs, land_shapes, mode, after):
    n = len(srcs)

    def body(*refs):
        sends, _ = _chip_copies(refs[:n], refs[n:2 * n], refs[2 * n + 1], refs[2 * n + 2], mode)
        for cp in sends:
            cp.start()
        token = refs[-1]
        token[...] = jnp.zeros(token.shape, token.dtype)

    hbm = lambda a: pltpu.with_memory_space_constraint(a, pltpu.HBM)
    lands = [hbm(lax.empty(s.shape, s.dtype)) for s in land_shapes]
    bufs = [pltpu.HBM(a.shape, a.dtype) for a in list(srcs) + lands]
    res = pl.pallas_call(
        body, name=name,
        out_shape=(pltpu.SemaphoreType.DMA((3 * n,)), pltpu.SemaphoreType.DMA((3 * n,)), *bufs,
                   jax.ShapeDtypeStruct((SUBLANES, LANE), F32)),
        in_specs=[HBM_SPEC] * (2 * n) + [pl.BlockSpec(memory_space=pl.ANY)],
        out_specs=[SEM_SPEC, SEM_SPEC] + [HBM_SPEC] * (2 * n) + [pl.BlockSpec(memory_space=pltpu.VMEM)],
        input_output_aliases={i: 2 + i for i in range(2 * n)},
        compiler_params=pltpu.CompilerParams(has_side_effects=DATAFLOW),
    )(*[hbm(s) for s in srcs], *lands, after)
    return res[0], res[1], res[2:2 + n], res[2 + n:2 + 2 * n], res[-1]


def _split_wait(name, send_sems, recv_sems, srcs, lands, mode, after):
    n = len(srcs)

    def body(*refs):
        sends, arrivals = _chip_copies(refs[:n], refs[n:2 * n], refs[2 * n], refs[2 * n + 1], mode)
        for cp in sends:
            cp.wait_send()
        for cp in arrivals:
            cp.wait_recv()

    res = pl.pallas_call(
        body, name=name,
        out_shape=tuple(pltpu.HBM(a.shape, a.dtype) for a in list(srcs) + list(lands)),
        in_specs=[HBM_SPEC] * (2 * n) + [SEM_SPEC, SEM_SPEC, pl.BlockSpec(memory_space=pl.ANY)],
        out_specs=[HBM_SPEC] * (2 * n),
        input_output_aliases={i: i for i in range(2 * n)},
        compiler_params=pltpu.CompilerParams(has_side_effects=DATAFLOW),
    )(*srcs, *lands, send_sems, recv_sems, after)
    return res[n:]


def _relay_sibling(lands):
    def emit(x, y, c, srcs, outs, copy):
        sib = (x, y, 1 - c)
        sends, arrivals = [], []
        for j, m in enumerate(CHIPS):
            px, py, _ = _peer(x, y, c, m)
            theirs = 2 * px + py
            for k, (s, o) in enumerate(zip(srcs, outs)):
                sends.append(copy(s.at[theirs, c], o.at[theirs, c], 3 * k + j, sib))
                arrivals.append(copy(s.at[theirs, c], o.at[theirs, 1 - c], 3 * k + j, sib))
        for cp in sends:
            cp.start()
        for cp in arrivals:
            cp.wait_recv()
        for cp in sends:
            cp.wait_send()

    shapes = [jax.ShapeDtypeStruct(l.shape, l.dtype) for l in lands]
    return _comm_call("relay_weights", emit, lands, shapes, 3 * len(lands), in_place=True)


def _gather_weights(halves):
    n = len(halves)

    def emit(x, y, c, srcs, outs, copy):
        chip = 2 * x + y
        sib = (x, y, 1 - c)
        first, relay, landed, relayed = [], [], [], []
        for j, m in enumerate(CHIPS):
            px, py, _ = _peer(x, y, c, m)
            theirs = 2 * px + py
            for k in range(n):
                i = 6 * k + j
                first.append(copy(srcs[k].at[c], outs[k].at[chip, c], i, (px, py, c)))
                landed.append(copy(srcs[k].at[c], outs[k].at[theirs, c], i, (px, py, c)))
                relay.append(copy(outs[k].at[theirs, c], outs[k].at[theirs, c], i + 3, sib))
                relayed.append(copy(outs[k].at[theirs, 1 - c], outs[k].at[theirs, 1 - c], i + 3, sib))
        for cp in first:
            cp.start()
        for arrival, onward in zip(landed, relay):
            arrival.wait_recv()
            onward.start()
        for arrival in relayed:
            arrival.wait_recv()
        for cp in first + relay:
            cp.wait_send()

    shapes = [jax.ShapeDtypeStruct((4,) + h.shape, h.dtype) for h in halves]
    return _comm_call("gather_weights", emit, halves, shapes, 6 * n)


def _pair_exchange(name, chunks):
    def emit(x, y, c, srcs, outs, copy):
        sib = (x, y, 1 - c)
        sends = [copy(s.at[:, 1 - c], o, k, sib) for k, (s, o) in enumerate(zip(srcs, outs))]
        for cp in sends:
            cp.start()
        for cp in sends:
            cp.wait_recv()
        for cp in sends:
            cp.wait_send()

    shapes = [jax.ShapeDtypeStruct((4,) + g.shape[2:], g.dtype) for g in chunks]
    return _comm_call(name, emit, chunks, shapes, len(chunks))


def _chips_alltoall(parts):
    def emit(x, y, c, srcs, outs, copy):
        chip = 2 * x + y
        sends, arrivals = [], []
        for j, m in enumerate(CHIPS):
            px, py, _ = _peer(x, y, c, m)
            theirs = 2 * px + py
            for k, (s, o) in enumerate(zip(srcs, outs)):
                sends.append(copy(s.at[theirs], o.at[chip], 3 * k + j, (px, py, c)))
                arrivals.append(copy(s.at[theirs], o.at[theirs], 3 * k + j, (px, py, c)))
        for cp in sends:
            cp.start()
        for cp in arrivals:
            cp.wait_recv()
        for cp in sends:
            cp.wait_send()

    shapes = [jax.ShapeDtypeStruct(p.shape, p.dtype) for p in parts]
    return _comm_call("reduce_chips_exchange", emit, parts, shapes, 3 * len(parts))


def _share_sibling(parts):
    def emit(x, y, c, srcs, outs, copy):
        sib = (x, y, 1 - c)
        sends = [copy(s, o.at[c], k, sib) for k, (s, o) in enumerate(zip(srcs, outs))]
        arrivals = [copy(s, o.at[1 - c], k, sib) for k, (s, o) in enumerate(zip(srcs, outs))]
        for cp in sends:
            cp.start()
        for cp in arrivals:
            cp.wait_recv()
        for cp in sends:
            cp.wait_send()

    shapes = [jax.ShapeDtypeStruct((2,) + p.shape, p.dtype) for p in parts]
    return _comm_call("share_sibling", emit, parts, shapes, len(parts))


def _reduce_pair(name, a, b):
    rows = a.shape[0] * a.shape[1]
    cols = a.shape[2]
    rt = _div_tile(rows, max(16, (1 << 20) // (4 * cols)), 16)
    spec = (rt, cols)
    (out,) = _tiled(name, lambda u, v: (u + v,), rows // rt,
                    [(a.reshape(rows, cols), spec, "row"), (b.reshape(rows, cols), spec, "row")],
                    [((rows, cols), BF16, spec, "row")])
    return out.reshape(a.shape)


def _reduce_quad(name, q):
    _, h, cols = q.shape
    rt = _div_tile(h, max(16, (1 << 20) // (4 * cols)), 16)

    def body(q_ref, o_ref):
        v = q_ref[...].astype(F32)
        o_ref[...] = ((v[0] + v[1]) + v[2]) + v[3]

    return pl.pallas_call(
        body, name=name, grid=(h // rt,),
        in_specs=[pl.BlockSpec((4, rt, cols), lambda i: (0, i, 0))],
        out_specs=pl.BlockSpec((rt, cols), lambda i: (i, 0)),
        out_shape=jax.ShapeDtypeStruct((h, cols), F32),
        compiler_params=_cparams(("parallel",)),
    )(q)


def _unshard(seg, kind):
    n, r, c = seg.shape
    if kind == "col":
        return seg.transpose(1, 0, 2).reshape(r, n * c)
    return seg.reshape(n * r, c)


def _pad_rows(flat, rows):
    n, ln = flat.shape
    return jnp.pad(flat, ((0, 0), (0, rows * PACK_COLS - ln))).reshape(n, rows, PACK_COLS)


def _block_diag_pairs(w):
    n2, bs, _ = w.shape
    eye = jnp.eye(2, dtype=w.dtype)
    z = w.reshape(n2 // 2, 2, bs, 1, bs) * eye[None, :, None, :, None]
    return z.reshape(n2 // 2, 2 * bs, 2 * bs).transpose(1, 0, 2).reshape(2 * bs, n2 * bs)


def _block_diag_pairs_t(d, bs=64):
    n = d.shape[1] // (2 * bs)
    z = d.reshape(2 * bs, n, 2 * bs).transpose(1, 0, 2).reshape(n, 2, bs, 2, bs)
    return jnp.stack([z[:, 0, :, 0, :], z[:, 1, :, 1, :]], axis=1).reshape(2 * n, bs, bs)


BIG = (("w_in", "col"), ("w_uq", "col"), ("w_ukv", "col"), ("w_proj_rnn", "row"), ("w_proj_mla", "row"),
       ("w_out", "row"), ("w_up", "col"), ("w_down", "row"))
FIRST_USED = ("w_in", "w_uq", "w_ukv")
CONVS = (("conv_w", "col"), ("ffn_conv_w", "col"))
SMALL = ("b_ada", "norm1_g", "conv_b", "w_gate_a", "b_gate_a", "w_gate_x", "b_gate_x", "lru_param",
         "q_norm_g", "kv_norm_g", "norm2_g", "ffn_conv_b", "final_g")
WEIGHTS = ("w_ada", "b_ada", "norm1_g", "w_in", "conv_w", "conv_b", "w_gate_a", "b_gate_a", "w_gate_x",
           "b_gate_x", "lru_param", "q_norm_g", "w_uq", "kv_norm_g", "w_ukv", "w_proj_rnn", "w_proj_mla",
           "w_out", "norm2_g", "w_up", "ffn_conv_w", "ffn_conv_b", "w_down", "final_g")


def _step(x, c, positions, w, m_in, v_in, loss_target):
    s_len, d = x.shape[1], x.shape[2]
    x2d = x[0]
    tgt = loss_target[0]
    xi, yi, ci = lax.axis_index("x"), lax.axis_index("y"), lax.axis_index("c")
    chip = 2 * xi + yi
    me = 2 * chip + ci
    tile = min(256, s_len)
    nt = s_len // tile

    local2d = {k: w[k][0] for k, _ in BIG + CONVS}
    kinds = dict(BIG)
    halves_bf = {k: local2d[k].astype(BF16).reshape(2, local2d[k].shape[0] // 2, local2d[k].shape[1]) for k, _ in BIG}
    first_names = [k for k, _ in BIG if k in FIRST_USED]
    later_names = [k for k, _ in BIG if k not in FIRST_USED]
    full = {}

    def assemble(k, g):
        g = lax.dynamic_update_index_in_dim(g, halves_bf[k][None], chip, 0).reshape((4,) + local2d[k].shape)
        if k == "w_up":
            full["w_up_gate"], full["w_up_val"] = _unshard(g[:2], kinds[k]), _unshard(g[2:], kinds[k])
        else:
            full[k] = _unshard(g, kinds[k])

    first_got = _gather_weights([halves_bf[k] for k in first_names])
    for k, g in zip(first_names, first_got):
        assemble(k, g)
    conv_flat = jnp.concatenate([local2d[k].reshape(-1) for k, _ in CONVS])
    conv_rows = -(-conv_flat.shape[0] // PACK_COLS)
    conv_all = _all_gather("gather_conv_w", _pad_rows(conv_flat[None], conv_rows)[0], CHIPS)
    conv_all = conv_all.reshape(4, -1)
    off = 0
    for k, kind in CONVS:
        r, cc = local2d[k].shape
        full[k] = _unshard(conv_all[:, off:off + r * cc].reshape(4, r, cc), kind)
        off += r * cc

    d_rnn = w["conv_b"].shape[1]
    n_q, n_kv = w["q_norm_g"].shape[1], w["kv_norm_g"].shape[1]
    w_in = full["w_in"]
    o1, o2, o3 = d_rnn + n_q, d_rnn + n_q + n_kv, d_rnn + n_q + n_kv + QK_ROPE
    w_rnn = w_in[:, :d_rnn]
    zpad = lambda n: jnp.zeros((d, n), BF16)
    w_qkv = jnp.concatenate([w_in[:, d_rnn:o2], zpad(QK_NOPE), w_in[:, o2:o3], zpad(LANE - QK_NOPE - QK_ROPE)], axis=1)
    w_g = w_in[:, o3:]
    hd = QK_NOPE + QK_ROPE
    w_uq = jnp.pad(full["w_uq"].reshape(n_q, N_HEADS, hd), ((0, 0), (0, 0), (0, HEAD_PAD - hd))).reshape(n_q, -1)
    w_ukv = full["w_ukv"]
    v_head = w_ukv.shape[1] // N_HEADS - QK_NOPE
    d_ff = w["ffn_conv_b"].shape[1] // 2
    ffn_cw_gate, ffn_cw_val = full["ffn_conv_w"][:, :d_ff], full["ffn_conv_w"][:, d_ff:]
    ffn_cb_gate, ffn_cb_val = w["ffn_conv_b"][:, :d_ff], w["ffn_conv_b"][:, d_ff:]
    conv_w, conv_b = full["conv_w"], w["conv_b"]
    wa_bd = _block_diag_pairs(w["w_gate_a"][0])
    wx_bd = _block_diag_pairs(w["w_gate_x"][0])

    c_all = _all_gather("gather_c", c, ALL7).reshape(8, d)
    c_rows = 128
    (c_act,) = _tiled("silu_c", lambda v: (_silu(v),), 1, [(jnp.pad(c_all, ((0, c_rows - 8), (0, 0))), (c_rows, d), "full")],
                      [((c_rows, d), F32, (c_rows, d), "full")])
    w_ada = w["w_ada"][0]
    n_mod = w_ada.shape[1]
    b_loc = lax.dynamic_slice_in_dim(w["b_ada"], chip * n_mod, n_mod, axis=1)
    mod_loc = _mm("ada_fwd", c_act, w_ada, add=jnp.broadcast_to(b_loc, (c_rows, n_mod)))
    mod_all = _all_gather("gather_mod", mod_loc[:8], CHIPS)
    mod = lax.dynamic_index_in_dim(mod_all, me, 1, keepdims=False).reshape(1, -1)
    shift1, scale1, gate1, shift2, scale2, gate2 = [mod[:, i * d:(i + 1) * d] for i in range(6)]

    small_done = (mod[:, :1] + conv_all[:1, :1] + first_got[0][0, 0, :1, :1].astype(F32))
    later_flight = _split_start(
        "gather_later_start", [halves_bf[k] for k in later_names],
        [jax.ShapeDtypeStruct((4,) + halves_bf[k].shape, BF16) for k in later_names], "gather", after=small_done)

    half = QK_ROPE // 2
    inv_freq = ROPE_THETA ** (-jnp.arange(half, dtype=F32) / half)
    ang = positions[0].astype(F32)[:, None] * inv_freq
    cos, sin = jnp.cos(ang), jnp.sin(ang)
    one, zero = jnp.ones((s_len, QK_NOPE), F32), jnp.zeros((s_len, half), F32)
    tail = jnp.zeros((s_len, LANE - QK_NOPE - QK_ROPE), F32)
    cos_f = jnp.concatenate([one, cos, cos, tail + 1.0], axis=1)
    sin_a = jnp.concatenate([one * 0.0, -sin, zero, tail], axis=1)
    sin_b = jnp.concatenate([one * 0.0, zero, sin, tail], axis=1)
    reset = (positions[0] == 0).astype(F32)[:, None]
    tabs = [(cos_f, (tile, LANE), "row"), (sin_a, (tile, LANE), "row"), (sin_b, (tile, LANE), "row")]

    def rowspec(a):
        return (a, (tile, a.shape[1]), "row")

    def full2(a):
        return (a, a.shape, "full")

    def rowout(cols, dt):
        return ((s_len, cols), dt, (tile, cols), "row")

    def accout(a):
        return (a.shape, F32, a.shape, "acc")

    norm1_g = w["norm1_g"] + later_flight[4][:1, :1]
    norm2_g, final_g = w["norm2_g"], w["final_g"].reshape(1, d)
    ln1_in = [rowspec(x2d), full2(norm1_g), full2(scale1), full2(shift1)]
    (h1,) = _tiled("ln1", _f_ln, nt, ln1_in, [rowout(d, BF16)])
    x_rnn = _mm("in_rnn", h1, w_rnn)
    qkv = _mm("in_qkv", h1, w_qkv)
    gates = _mm("in_gates", h1, w_g)

    ct = LANE
    n_ct = d_rnn // ct
    colspec = lambda a, width=ct: (a, (a.shape[0], width), "col")
    lru_in = [colspec(x_rnn), colspec(conv_w), colspec(conv_b), colspec(wa_bd), colspec(w["b_gate_a"]),
              colspec(wx_bd), colspec(w["b_gate_x"]), colspec(w["lru_param"]), full2(reset)]
    y_rnn, h_rnn = _tiled("lru_fwd", _f_lru_fwd, n_ct, lru_in,
                          [((s_len, d_rnn), BF16, (s_len, ct), "col"), ((s_len, d_rnn), F32, (s_len, ct), "col")])

    qkv_in = [rowspec(qkv)] + tabs + [full2(w["q_norm_g"]), full2(w["kv_norm_g"])]
    qn, kvn, kr = _tiled("qkv_norm", _f_qkv, nt, qkv_in, [rowout(n_q, BF16), rowout(n_kv, BF16), rowout(LANE, BF16)])
    q_pre = _mm("up_q", qn, w_uq)
    kv = _mm("up_kv", kvn, w_ukv, out_dtype=BF16)
    (q_cat,) = _tiled("rot_q", _f_rotq, nt, [rowspec(q_pre)] + tabs, [rowout(q_pre.shape[1], BF16)])
    o_mla, lse = _attn_fwd(q_cat, kv, kr)

    send_sems, recv_sems, flown, landed, _ = later_flight
    landed = _split_wait("gather_later_wait", send_sems, recv_sems, flown, landed, "gather", after=o_mla)
    for k, g in zip(later_names, _relay_sibling(landed)):
        assemble(k, g)
    w_pr = full["w_proj_rnn"]
    w_pm = jnp.pad(full["w_proj_mla"].reshape(N_HEADS, v_head, d), ((0, 0), (HEAD_PAD - v_head, 0), (0, 0))).reshape(-1, d)
    w_out = full["w_out"]
    w_up_gate, w_up_val = full["w_up_gate"], full["w_up_val"]
    w_down = full["w_down"]

    p_rnn = _mm("proj_rnn", y_rnn, w_pr)
    p_mla = _mm("proj_mla", o_mla, w_pm)
    merge_in = [rowspec(gates), rowspec(p_rnn), rowspec(p_mla)]
    (merged,) = _tiled("merge", _f_merge, nt, merge_in, [rowout(d, BF16)])
    o_tok = _mm("out_proj", merged, w_out)
    res_in = [rowspec(x2d), rowspec(o_tok), full2(gate1), full2(norm2_g), full2(scale2), full2(shift2)]
    x1, h2 = _tiled("res_ln2", _f_res_ln, nt, res_in, [rowout(d, F32), rowout(d, BF16)])
    u_gate = _mm("ffn_up_gate", h2, w_up_gate)
    u_val = _mm("ffn_up_val", h2, w_up_val)
    n_ft = d_ff // LANE
    ffn_in = [colspec(a) for a in (u_gate, u_val, ffn_cw_gate, ffn_cw_val, ffn_cb_gate, ffn_cb_val)]
    (act,) = _tiled("ffn_conv", _f_ffn, n_ft, ffn_in, [((s_len, d_ff), BF16, (s_len, LANE), "col")])
    f_tok = _mm("ffn_down", act, w_down)

    loss_in = [rowspec(x1), rowspec(f_tok), rowspec(tgt), full2(gate2), full2(final_g)]
    dx1, df, loss_row, d_gate2, d_final_g = _tiled(
        "loss", _f_loss_and_grads, nt, loss_in,
        [rowout(d, F32), rowout(d, BF16), ((1, LANE), F32, (1, LANE), "acc"), accout(gate2), accout(final_g)])
    loss = lax.psum(loss_row[0, 0], ("x", "y", "c"))

    d_act = _mm("ffn_down_dx", df, w_down, tb=True)
    g_w_down = _mm("ffn_down_dw", act, df, ta=True)
    taps = ffn_cw_gate.shape[0]
    du_gate, du_val, g_cw_gate, g_cw_val, g_cb_gate, g_cb_val = _tiled(
        "ffn_conv_bwd", _vjp_of(_f_ffn, 6, (0, 1, 2, 3, 4, 5)), n_ft, ffn_in + [colspec(d_act)],
        [((s_len, d_ff), BF16, (s_len, LANE), "col")] * 2 + [((taps, d_ff), F32, (taps, LANE), "col")] * 2
        + [((1, d_ff), F32, (1, LANE), "col")] * 2)
    dh2 = _mm("ffn_up_gate_dx", du_gate, w_up_gate, tb=True)
    dh2 = _mm("ffn_up_val_dx", du_val, w_up_val, tb=True, add=dh2)
    g_w_up = jnp.concatenate([_mm("ffn_up_gate_dw", h2, du_gate, ta=True), _mm("ffn_up_val_dw", h2, du_val, ta=True)], axis=1)
    g_ffn_cw = jnp.concatenate([g_cw_gate, g_cw_val], axis=1)
    g_ffn_cb = jnp.concatenate([g_cb_gate, g_cb_val], axis=1)

    res_bwd = _vjp_of(_f_res_ln, 6, (0, 1, 2, 3, 4, 5))
    dx_res, do_tok, d_gate1, g_norm2, d_scale2, d_shift2 = _tiled(
        "res_ln2_bwd", res_bwd, nt, res_in + [rowspec(dx1), rowspec(dh2)],
        [rowout(d, F32), rowout(d, BF16), accout(gate1), accout(norm2_g), accout(scale2), accout(shift2)])
    d_merged = _mm("out_proj_dx", do_tok, w_out, tb=True)
    g_w_out = _mm("out_proj_dw", merged, do_tok, ta=True)
    d_gates, dp_rnn, dp_mla = _tiled(
        "merge_bwd", _f_merge_bwd, nt, merge_in + [rowspec(d_merged)],
        [rowout(gates.shape[1], BF16), rowout(d, BF16), rowout(d, BF16)])
    dy_rnn = _mm("proj_rnn_dx", dp_rnn, w_pr, tb=True)
    g_w_pr = _mm("proj_rnn_dw", y_rnn, dp_rnn, ta=True)
    do_mla = _mm("proj_mla_dx", dp_mla, w_pm, tb=True, out_dtype=BF16)
    g_w_pm = _mm("proj_mla_dw", o_mla, dp_mla, ta=True)

    def chunked(k, gk):
        r, cc = local2d[k].shape
        if kinds[k] == "col":
            gk = gk.reshape(r, 4, cc).transpose(1, 0, 2)
        return gk.reshape(4, 2, r // 2, cc)

    def pair_sums(tag, names, chunks):
        out = []
        for k, ck, from_sib in zip(names, chunks, _pair_exchange("reduce_pair_exchange_" + tag, chunks)):
            ours = lax.dynamic_index_in_dim(ck, ci, 1, keepdims=False)
            out.append(_reduce_pair("reduce_pair_" + k, ours, from_sib))
        return out

    g_later = {
        "w_proj_rnn": g_w_pr,
        "w_proj_mla": g_w_pm.reshape(N_HEADS, HEAD_PAD, d)[:, HEAD_PAD - v_head:, :].reshape(-1, d),
        "w_out": g_w_out, "w_up": g_w_up, "w_down": g_w_down,
    }
    sums_ready = pair_sums("ready", later_names, [chunked(k, g_later[k]) for k in later_names])
    ready_flight = _split_start(
        "reduce_ready_start", sums_ready, [jax.ShapeDtypeStruct(s.shape, s.dtype) for s in sums_ready], "alltoall",
        after=sums_ready[0])
    kr_held = kr + ready_flight[4][:1, :].astype(BF16)

    dq_cat, dkv, dkr = _attn_bwd(q_cat, kv, kr_held, o_mla, lse, do_mla)
    rot_bwd = _vjp_of(_f_rotq, 4, (0,))
    (dq_pre,) = _tiled("rot_q_bwd", rot_bwd, nt, [rowspec(q_pre)] + tabs + [rowspec(dq_cat)],
                       [rowout(q_pre.shape[1], BF16)])
    dqn = _mm("up_q_dx", dq_pre, w_uq, tb=True)
    g_w_uq = _mm("up_q_dw", qn, dq_pre, ta=True)
    dkv_b = dkv.astype(BF16)
    dkvn = _mm("up_kv_dx", dkv_b, w_ukv, tb=True)
    g_w_ukv = _mm("up_kv_dw", kvn, dkv_b, ta=True)
    dqkv, g_q_norm, g_kv_norm = _tiled(
        "qkv_norm_bwd", _f_qkv_bwd, nt, qkv_in + [rowspec(dqn), rowspec(dkvn), rowspec(dkr)],
        [rowout(qkv.shape[1], BF16), accout(w["q_norm_g"]), accout(w["kv_norm_g"])])

    lru_out = [((s_len, d_rnn), BF16, (s_len, ct), "col")]
    for a in (conv_w, conv_b, wa_bd, w["b_gate_a"], wx_bd, w["b_gate_x"], w["lru_param"]):
        lru_out.append((a.shape, F32, (a.shape[0], ct), "col"))
    dx_rnn, g_conv_w, g_conv_b, g_wa_bd, g_b_a, g_wx_bd, g_b_x, g_lru = _tiled(
        "lru_bwd", _f_lru_bwd, n_ct, lru_in + [colspec(h_rnn), colspec(dy_rnn)], lru_out)

    dh1 = _mm("in_gates_dx", d_gates, w_g, tb=True)
    dh1 = _mm("in_qkv_dx", dqkv, w_qkv, tb=True, add=dh1)
    dh1 = _mm("in_rnn_dx", dx_rnn, w_rnn, tb=True, add=dh1)
    g_w_rnn = _mm("in_rnn_dw", h1, dx_rnn, ta=True)
    g_w_qkv = _mm("in_qkv_dw", h1, dqkv, ta=True)
    g_w_g = _mm("in_gates_dw", h1, d_gates, ta=True)

    ln_bwd = _vjp_of(_f_ln, 4, (0, 1, 2, 3))

    def ln1_bwd(xv, gv, sc, sh, dxr, dh):
        dx, dg, dsc, dsh = ln_bwd(xv, gv, sc, sh, dh)
        return dx + dxr, dg, dsc, dsh

    grad_x, g_norm1, d_scale1, d_shift1 = _tiled(
        "ln1_bwd", ln1_bwd, nt, ln1_in + [rowspec(dx_res), rowspec(dh1)],
        [rowout(d, F32), accout(norm1_g), accout(scale1), accout(shift1)])

    dmod = jnp.concatenate([d_shift1, d_scale1, d_gate1, d_shift2, d_scale2, d_gate2], axis=1)
    dmod_all = _all_gather("gather_dmod", dmod, ALL7).reshape(8, -1)
    dmod_loc = lax.dynamic_slice_in_dim(dmod_all, chip * n_mod, n_mod, axis=1)
    g_w_ada = _mm("ada_dw", c_act, jnp.pad(dmod_loc, ((0, c_rows - 8), (0, 0))), ta=True)

    g_full = {
        "w_in": jnp.concatenate([g_w_rnn, g_w_qkv[:, :n_q + n_kv],
                                 g_w_qkv[:, n_q + n_kv + QK_NOPE:n_q + n_kv + QK_NOPE + QK_ROPE], g_w_g], axis=1),
        "w_uq": g_w_uq.reshape(n_q, N_HEADS, HEAD_PAD)[:, :, :hd].reshape(n_q, -1),
        "w_ukv": g_w_ukv,
        "conv_w": g_conv_w,
        "ffn_conv_w": g_ffn_cw,
    }
    g_small = {
        "b_ada": dmod, "norm1_g": g_norm1, "conv_b": g_conv_b,
        "w_gate_a": _block_diag_pairs_t(g_wa_bd)[None], "b_gate_a": g_b_a,
        "w_gate_x": _block_diag_pairs_t(g_wx_bd)[None], "b_gate_x": g_b_x, "lru_param": g_lru,
        "q_norm_g": g_q_norm, "kv_norm_g": g_kv_norm, "norm2_g": g_norm2,
        "ffn_conv_b": g_ffn_cb, "final_g": d_final_g.reshape(w["final_g"].shape),
    }

    small_flat = jnp.concatenate([g_small[k].reshape(-1) for k in SMALL] + [g_full[k].reshape(-1) for k, _ in CONVS])
    small_rows = -(-small_flat.shape[0] // (8 * PACK_COLS * PACK_ROW_UNIT)) * PACK_ROW_UNIT
    last_names = first_names + ["small"]
    last_chunks = [chunked(k, g_full[k]) for k in first_names]
    last_chunks.append(_pad_rows(small_flat[None], 8 * small_rows).reshape(4, 2, small_rows, PACK_COLS))
    sums_last = pair_sums("last", last_names, last_chunks)
    send_sems, recv_sems, flown, landed, _ = ready_flight
    quads_ready = _split_wait("reduce_ready_wait", send_sems, recv_sems, flown, landed, "alltoall", after=grad_x)
    reduced = {}
    for k, quad, ps in zip(later_names + last_names, list(quads_ready) + list(_chips_alltoall(sums_last)),
                           sums_ready + sums_last):
        quad = lax.dynamic_update_index_in_dim(quad, lax.dynamic_index_in_dim(ps, chip, 0, keepdims=True), chip, 0)
        reduced[k] = _reduce_quad("reduce_quad_" + k, quad)
    grads = {}
    for (k, _), both in zip(BIG, _share_sibling([reduced[k] for k, _ in BIG])):
        grads[k] = lax.dynamic_update_index_in_dim(both, reduced[k][None], ci, 0).reshape(w[k].shape)
    small_grad = _all_gather("share_small", reduced["small"], ALL7).reshape(-1)
    off = 0
    for k in SMALL:
        grads[k] = small_grad[off:off + w[k].size].reshape(w[k].shape)
        off += w[k].size
    for k, _ in CONVS:
        r, cc = local2d[k].shape
        whole = small_grad[off:off + 4 * r * cc].reshape(r, 4 * cc)
        grads[k] = lax.dynamic_slice_in_dim(whole, chip * cc, cc, axis=1)[None]
        off += 4 * r * cc
    grads["w_ada"] = g_w_ada[None]

    delta, new_m, new_v = {}, {}, {}
    for k in WEIGHTS:
        shp = w[k].shape
        two_d = (-1, shp[-1]) if len(shp) > 1 else (1, -1)
        dk, mk, vk = _adamw("adamw_" + k, w[k].reshape(two_d), grads[k].reshape(two_d),
                            m_in[k].reshape(two_d), v_in[k].reshape(two_d))
        delta[k], new_m[k], new_v[k] = dk.reshape(shp), mk.reshape(shp), vk.reshape(shp)

    return (loss, grad_x[None], *[grads[k] for k in WEIGHTS], *[delta[k] for k in WEIGHTS],
            *[new_m[k] for k in WEIGHTS], *[new_v[k] for k in WEIGHTS])


def kernel(x, c, positions, w_ada, b_ada, norm1_g, w_in, conv_w, conv_b, w_gate_a, b_gate_a, w_gate_x, b_gate_x, lru_param, q_norm_g, w_uq, kv_norm_g, w_ukv, w_proj_rnn, w_proj_mla, w_out, norm2_g, w_up, ffn_conv_w, ffn_conv_b, w_down, final_g, loss_target, m_w_ada, m_b_ada, m_norm1_g, m_w_in, m_conv_w, m_conv_b, m_w_gate_a, m_b_gate_a, m_w_gate_x, m_b_gate_x, m_lru_param, m_q_norm_g, m_w_uq, m_kv_norm_g, m_w_ukv, m_w_proj_rnn, m_w_proj_mla, m_w_out, m_norm2_g, m_w_up, m_ffn_conv_w, m_ffn_conv_b, m_w_down, m_final_g, v_w_ada, v_b_ada, v_norm1_g, v_w_in, v_conv_w, v_conv_b, v_w_gate_a, v_b_gate_a, v_w_gate_x, v_b_gate_x, v_lru_param, v_q_norm_g, v_w_uq, v_kv_norm_g, v_w_ukv, v_w_proj_rnn, v_w_proj_mla, v_w_out, v_norm2_g, v_w_up, v_ffn_conv_w, v_ffn_conv_b, v_w_down, v_final_g):
    given = dict(locals())
    w = {k: given[k] for k in WEIGHTS}
    m_in = {k: given["m_" + k] for k in WEIGHTS}
    v_in = {k: given["v_" + k] for k in WEIGHTS}
    return _step(x, c, positions, w, m_in, v_in, loss_target)
```

```python
import functools
import math

import jax
import jax.numpy as jnp
from jax import lax
from jax.experimental import pallas as pl
from jax.experimental.pallas import tpu as pltpu

F32 = jnp.float32
BF16 = jnp.bfloat16

EPS = 1e-6
LRU_C = 8.0
N_HEADS = 16
QK_NOPE = 64
QK_ROPE = 32
HEAD_PAD = 128
ROPE_THETA = 10000.0
ADAM_LR = 0.001
ADAM_B1 = 0.9
ADAM_B2 = 0.999
ADAM_EPS = 1e-08
ADAM_WD = 0.01
ADAM_STEP = 10

LANE = 128
SUBLANES = 8
VMEM_LIMIT = 48 * 1024 * 1024
MM_TILE_M = MM_TILE_N = MM_TILE_K = 1408
PACK_COLS = 1024
PACK_ROW_UNIT = 32
MESH = pl.DeviceIdType.MESH

NN = (((1,), (0,)), ((), ()))
NT = (((1,), (1,)), ((), ()))


def _cparams(sem):
    return pltpu.CompilerParams(dimension_semantics=sem, vmem_limit_bytes=VMEM_LIMIT)


def _div_tile(n, cap, unit):
    best = None
    d = unit
    while d <= min(n, cap):
        if n % d == 0:
            best = d
        d += unit
    return n if best is None else best


def _mm(name, a, b, *, ta=False, tb=False, add=None, out_dtype=F32):
    if ta:
        kdim, m = a.shape
    else:
        m, kdim = a.shape
    if tb:
        n, kb = b.shape
    else:
        kb, n = b.shape
    assert kdim == kb, (name, a.shape, b.shape)
    tm = _div_tile(m, MM_TILE_M, 8 if not ta else LANE)
    tn = _div_tile(n, MM_TILE_N, LANE)
    tk = _div_tile(kdim, MM_TILE_K, LANE)
    nk = kdim // tk
    a_spec = pl.BlockSpec((tk, tm), lambda i, j, k: (k, i)) if ta else pl.BlockSpec((tm, tk), lambda i, j, k: (i, k))
    b_spec = pl.BlockSpec((tn, tk), lambda i, j, k: (j, k)) if tb else pl.BlockSpec((tk, tn), lambda i, j, k: (k, j))
    o_spec = pl.BlockSpec((tm, tn), lambda i, j, k: (i, j))
    has_add = add is not None
    dims = ((((0,) if ta else (1,)), ((1,) if tb else (0,))), ((), ()))

    def body(*refs):
        a_ref, b_ref = refs[0], refs[1]
        c_ref = refs[2] if has_add else None
        o_ref = refs[3] if has_add else refs[2]
        prod = lax.dot_general(a_ref[...].astype(BF16), b_ref[...].astype(BF16), dims, preferred_element_type=F32)
        if nk == 1:
            o_ref[...] = (prod + c_ref[...].astype(F32) if has_add else prod).astype(o_ref.dtype)
            return
        acc = refs[-1]
        k = pl.program_id(2)

        @pl.when(k == 0)
        def _():
            acc[...] = prod + c_ref[...].astype(F32) if has_add else prod

        @pl.when(jnp.logical_and(k > 0, k < nk - 1))
        def _():
            acc[...] += prod

        @pl.when(k == nk - 1)
        def _():
            o_ref[...] = (acc[...] + prod).astype(o_ref.dtype)

    ins = [a, b] + ([add] if has_add else [])
    specs = [a_spec, b_spec] + ([o_spec] if has_add else [])
    return pl.pallas_call(
        body, name=name, grid=(m // tm, n // tn, nk), in_specs=specs, out_specs=o_spec,
        out_shape=jax.ShapeDtypeStruct((m, n), out_dtype),
        scratch_shapes=[pltpu.VMEM((tm, tn), F32)] if nk > 1 else [],
        compiler_params=_cparams(("parallel", "parallel", "arbitrary")),
    )(*ins)


_IMAPS = {
    "row": lambda i: (i, 0),
    "col": lambda i: (0, i),
    "full": lambda i: (0, 0),
    "acc": lambda i: (0, 0),
}


def _tiled(name, fn, n, ins, outs):
    ni = len(ins)
    is_acc = [k == "acc" for *_, k in outs]

    def body(*refs):
        vals = fn(*[r[...] for r in refs[:ni]])
        orefs = refs[ni:]
        if any(is_acc):
            @pl.when(pl.program_id(0) == 0)
            def _():
                for r, a in zip(orefs, is_acc):
                    if a:
                        r[...] = jnp.zeros(r.shape, r.dtype)
        for r, v, a in zip(orefs, vals, is_acc):
            if a:
                r[...] += v.astype(r.dtype)
            else:
                r[...] = v.astype(r.dtype)

    res = pl.pallas_call(
        body, name=name, grid=(n,),
        in_specs=[pl.BlockSpec(bs, _IMAPS[k]) for _, bs, k in ins],
        out_specs=[pl.BlockSpec(bs, _IMAPS[k]) for _, _, bs, k in outs],
        out_shape=[jax.ShapeDtypeStruct(s, d) for s, d, _, _ in outs],
        compiler_params=_cparams(("arbitrary",)),
    )(*[a for a, _, _ in ins])
    return tuple(res)


def _vjp_of(fn, nin, diff):
    def g(*args):
        ins, cots = args[:nin], args[nin:]

        def f(*d):
            full = list(ins)
            for i, v in zip(diff, d):
                full[i] = v
            return fn(*full)

        outs, vjp = jax.vjp(f, *[ins[i] for i in diff])
        return vjp(tuple(c.astype(o.dtype) for c, o in zip(cots, outs)))
    return g


def _shift_rows(x, k, fill, up=False):
    n = x.shape[0]
    if k % SUBLANES == 0:
        pad = jnp.full((k,) + x.shape[1:], fill, x.dtype)
        return jnp.concatenate([x[k:], pad], axis=0) if up else jnp.concatenate([pad, x[:n - k]], axis=0)
    rows = lax.broadcasted_iota(jnp.int32, x.shape, 0)
    if up:
        return jnp.where(rows < n - k, pltpu.roll(x, n - k, 0), fill)
    return jnp.where(rows >= k, pltpu.roll(x, k, 0), fill)


@functools.partial(jax.custom_vjp, nondiff_argnums=(1,))
def _delay(x, k):
    return _shift_rows(x, k, 0.0)


def _delay_fwd(x, k):
    return _shift_rows(x, k, 0.0), None


def _delay_bwd(k, _, g):
    return (_shift_rows(g, k, 0.0, up=True),)


_delay.defvjp(_delay_fwd, _delay_bwd)


@functools.partial(jax.custom_vjp, nondiff_argnums=(1,))
def _lane_roll(x, s):
    return pltpu.roll(x, s, 1)


def _lane_roll_fwd(x, s):
    return pltpu.roll(x, s, 1), None


def _lane_roll_bwd(s, _, g):
    return (pltpu.roll(g, g.shape[1] - s, 1),)


_lane_roll.defvjp(_lane_roll_fwd, _lane_roll_bwd)


@jax.custom_vjp
def _bdot(x, w):
    return lax.dot_general(x.astype(BF16), w.astype(BF16), NN, preferred_element_type=F32)


def _bdot_fwd(x, w):
    return _bdot(x, w), (x, w)


def _bdot_bwd(res, g):
    x, w = res
    gb = g.astype(BF16)
    dx = lax.dot_general(gb, w.astype(BF16), NT, preferred_element_type=F32)
    dw = lax.dot_general(x.T.astype(BF16), gb, NN, preferred_element_type=F32)
    return dx, dw


_bdot.defvjp(_bdot_fwd, _bdot_bwd)


def _sigmoid(x):
    return 0.5 * (jnp.tanh(0.5 * x) + 1.0)


def _silu(x):
    return x * _sigmoid(x)


def _rms(x, g):
    return x * lax.rsqrt(jnp.mean(x * x, axis=-1, keepdims=True) + EPS) * g


def _causal_conv(x, w, b):
    kw = w.shape[0]
    tap = lax.broadcasted_iota(jnp.int32, w.shape, 0)
    y = b
    for k in range(kw):
        d = kw - 1 - k
        wk = jnp.sum(jnp.where(tap == k, w, 0.0), axis=0, keepdims=True)
        y = y + wk * (x if d == 0 else _delay(x, d))
    return y


def _rotate(x, cos_f, sin_a, sin_b):
    reps = x.shape[1] // LANE
    if reps > 1:
        cos_f, sin_a, sin_b = (jnp.tile(t, (1, reps)) for t in (cos_f, sin_a, sin_b))
    n = x.shape[1]
    half = QK_ROPE // 2
    return x * cos_f + _lane_roll(x, n - half) * sin_a + _lane_roll(x, half) * sin_b


def _softplus_neg(l):
    u = jnp.exp(-jnp.abs(l))
    log1p_u = jnp.where(u < 0.01, u * (1.0 - u * (0.5 - u * (1.0 / 3.0))), jnp.log(1.0 + u))
    return jnp.maximum(-l, 0.0) + log1p_u


def _f_ln(x, g, scale, shift):
    return (_rms(x, g) * (1.0 + scale) + shift,)


def _f_qkv(qkv, cos_f, sin_a, sin_b, qg, kvg):
    nq, nkv = qg.shape[1], kvg.shape[1]
    qn = _rms(qkv[:, :nq], qg)
    kvn = _rms(qkv[:, nq:nq + nkv], kvg)
    kr = _rotate(qkv[:, nq + nkv:], cos_f, sin_a, sin_b)
    return qn, kvn, kr


def _f_qkv_bwd(qkv, cos_f, sin_a, sin_b, qg, kvg, dqn, dkvn, dkr):
    nq, nkv = qg.shape[1], kvg.shape[1]
    _, vjp_q = jax.vjp(_rms, qkv[:, :nq], qg)
    _, vjp_kv = jax.vjp(_rms, qkv[:, nq:nq + nkv], kvg)
    _, vjp_r = jax.vjp(lambda t: _rotate(t, cos_f, sin_a, sin_b), qkv[:, nq + nkv:])
    dq_lat, dqg = vjp_q(dqn)
    dkv_lat, dkvg = vjp_kv(dkvn)
    (dkr_pre,) = vjp_r(dkr)
    return jnp.concatenate([dq_lat, dkv_lat, dkr_pre], axis=1), dqg, dkvg


def _f_rotq(q, cos_f, sin_a, sin_b):
    return (_rotate(q, cos_f, sin_a, sin_b),)


def _merge(g_rnn, g_mla, p_rnn, p_mla):
    return _sigmoid(g_rnn) * p_rnn + _sigmoid(g_mla) * p_mla


def _f_merge(g, p_rnn, p_mla):
    d = p_rnn.shape[1]
    return (_merge(g[:, :d], g[:, d:], p_rnn, p_mla),)


def _f_merge_bwd(g, p_rnn, p_mla, dm):
    d = p_rnn.shape[1]
    _, vjp = jax.vjp(_merge, g[:, :d], g[:, d:], p_rnn, p_mla)
    dg_rnn, dg_mla, dp_rnn, dp_mla = vjp(dm)
    return jnp.concatenate([dg_rnn, dg_mla], axis=1), dp_rnn, dp_mla


def _f_res_ln(x, o, gate, g2, scale, shift):
    x1 = x + gate * o
    return x1, _rms(x1, g2) * (1.0 + scale) + shift


def _f_ffn(u_gate, u_val, cw_gate, cw_val, cb_gate, cb_val):
    return (_silu(_causal_conv(u_gate, cw_gate, cb_gate)) * _causal_conv(u_val, cw_val, cb_val),)


def _f_loss(x1, f, tgt, gate, fg):
    y = _rms(x1 + gate * f, fg)
    err = (y - tgt) * (y - tgt)
    return 0.5 * jnp.sum(jnp.mean(err, axis=-1, keepdims=True), axis=0, keepdims=True)


def _f_loss_and_grads(x1, f, tgt, gate, fg):
    loss, vjp = jax.vjp(lambda a, b, c, d: _f_loss(a, b, tgt, c, d), x1, f, gate, fg)
    dx1, df, dgate, dfg = vjp(jnp.ones((1, 1), F32))
    return dx1, df, jnp.broadcast_to(loss, (1, LANE)), dgate, dfg


def _f_lru_coeffs(xr, cw, cb, wa, ba, wx, bx, lru, reset):
    xc = _causal_conv(xr, cw, cb)
    r = _sigmoid(_bdot(xc, wa) + ba)
    i = _sigmoid(_bdot(xc, wx) + bx)
    log_a = (-LRU_C) * r * _softplus_neg(lru)
    a = jnp.exp(log_a)
    mult = jnp.sqrt(-jnp.tanh(log_a) * (1.0 + a * a))
    is_reset = reset > 0.5
    a = jnp.where(is_reset, 0.0, a)
    mult = jnp.where(is_reset, 1.0, mult)
    return a, mult * (i * xc)


SCAN_BLOCK = 64


def _scan(a, b, up=False):
    n = a.shape[0]
    blk = min(SCAN_BLOCK, n)
    pos = lax.broadcasted_iota(jnp.int32, a.shape, 0) % blk
    k = 1
    while k < blk:
        inside = (pos < blk - k) if up else (pos >= k)
        shift = n - k if up else k
        b = b + a * jnp.where(inside, pltpu.roll(b, shift, 0), 0.0)
        a = a * jnp.where(inside, pltpu.roll(a, shift, 0), 1.0)
        k *= 2
    blocks = range(n // blk)
    carry = jnp.zeros((1,) + a.shape[1:], a.dtype)
    out = [None] * len(blocks)
    for i in (reversed(blocks) if up else blocks):
        rows = slice(i * blk, (i + 1) * blk)
        out[i] = b[rows] + a[rows] * carry
        carry = out[i][:1] if up else out[i][blk - 1:]
    return jnp.concatenate(out, axis=0)


def _f_lru_fwd(xr, cw, cb, wa, ba, wx, bx, lru, reset):
    a, b = _f_lru_coeffs(xr, cw, cb, wa, ba, wx, bx, lru, reset)
    h = _scan(a, b)
    return h, h


def _f_lru_bwd(xr, cw, cb, wa, ba, wx, bx, lru, reset, h, dh):
    (a, _), vjp = jax.vjp(lambda *p: _f_lru_coeffs(*p, reset), xr, cw, cb, wa, ba, wx, bx, lru)
    g = _scan(_shift_rows(a, 1, 0.0, up=True), dh, up=True)
    return vjp((g * _shift_rows(h, 1, 0.0), g))


def _attn_tile(s):
    return 1024 if s >= 2048 else s // 2


def _keys(kv, kr):
    lane = lax.broadcasted_iota(jnp.int32, kv.shape, 1)
    return jnp.where(lane < QK_NOPE, kv, kr)


ATTN_HEADS_PER_STEP = 2


def _scores(q, kc, scale, diagonal):
    s = lax.dot_general(q, kc, NT, preferred_element_type=F32) * scale
    if not diagonal:
        return s
    rows = lax.broadcasted_iota(jnp.int32, s.shape, 0)
    cols = lax.broadcasted_iota(jnp.int32, s.shape, 1)
    return jnp.where(cols - (s.shape[1] - s.shape[0]) <= rows, s, -jnp.inf)


def _sub_blocks(t, diagonal):
    return ((0, t // 2, t // 2), (t // 2, t // 2, t)) if diagonal else ((0, t, t),)


def _causal_pairs(nb, k_major):
    if k_major:
        pairs = [(qb, kb) for kb in range(nb) for qb in range(kb, nb)]
    else:
        pairs = [(qb, kb) for qb in range(nb) for kb in range(qb + 1)]
    return jnp.array([p[0] for p in pairs], jnp.int32), jnp.array([p[1] for p in pairs], jnp.int32)


def _attn_fwd(q, kv, kr):
    s_len = q.shape[0]
    t = _attn_tile(s_len)
    nb = s_len // t
    hp = ATTN_HEADS_PER_STEP
    wide = hp * HEAD_PAD
    scale = 1.0 / math.sqrt(QK_NOPE + QK_ROPE)
    q_tab, k_tab = _causal_pairs(nb, k_major=False)

    def body(qt, kt, q_ref, kv_ref, kr_ref, o_ref, lse_ref, m_s, l_s, acc_s):
        pair = pl.program_id(1)
        qi, ki = qt[pair], kt[pair]

        @pl.when(ki == 0)
        def _():
            m_s[...] = jnp.full(m_s.shape, -jnp.inf, F32)
            l_s[...] = jnp.zeros(l_s.shape, F32)
            acc_s[...] = jnp.zeros(acc_s.shape, F32)

        def step(diagonal):
            for h in range(hp):
                lanes = slice(h * HEAD_PAD, (h + 1) * HEAD_PAD)
                for r0, nr, nk in _sub_blocks(t, diagonal):
                    rows = slice(r0, r0 + nr)
                    kvv = kv_ref[:nk, lanes]
                    s = _scores(q_ref[rows, lanes], _keys(kvv, kr_ref[:nk, :]), scale, diagonal)
                    m_old = m_s[h, rows]
                    m_new = jnp.maximum(m_old, jnp.max(s, axis=-1, keepdims=True))
                    alpha = jnp.exp(m_old - m_new)
                    p = jnp.exp(s - m_new)
                    l_s[h, rows] = alpha * l_s[h, rows] + jnp.sum(p, axis=-1, keepdims=True)
                    acc_s[rows, lanes] = alpha * acc_s[rows, lanes] + lax.dot_general(
                        p.astype(BF16), kvv, NN, preferred_element_type=F32)
                    m_s[h, rows] = m_new

        @pl.when(ki < qi)
        def _():
            step(False)

        @pl.when(ki == qi)
        def _():
            step(True)
            lane = lax.broadcasted_iota(jnp.int32, (t, HEAD_PAD), 1)
            for h in range(hp):
                lanes = slice(h * HEAD_PAD, (h + 1) * HEAD_PAD)
                o_ref[:, lanes] = jnp.where(lane >= QK_NOPE, acc_s[:, lanes] / l_s[h], 0.0).astype(o_ref.dtype)
                lse_ref[h] = m_s[h] + jnp.log(l_s[h])

    grid_spec = pltpu.PrefetchScalarGridSpec(
        num_scalar_prefetch=2, grid=(N_HEADS // hp, q_tab.shape[0]),
        in_specs=[pl.BlockSpec((t, wide), lambda h, p, qt, kt: (qt[p], h)),
                  pl.BlockSpec((t, wide), lambda h, p, qt, kt: (kt[p], h)),
                  pl.BlockSpec((t, HEAD_PAD), lambda h, p, qt, kt: (kt[p], 0))],
        out_specs=[pl.BlockSpec((t, wide), lambda h, p, qt, kt: (qt[p], h)),
                   pl.BlockSpec((hp, t, 1), lambda h, p, qt, kt: (h, qt[p], 0))],
        scratch_shapes=[pltpu.VMEM((hp, t, 1), F32), pltpu.VMEM((hp, t, 1), F32), pltpu.VMEM((t, wide), F32)])
    return pl.pallas_call(
        body, name="attn_fwd", grid_spec=grid_spec,
        out_shape=[jax.ShapeDtypeStruct((s_len, N_HEADS * HEAD_PAD), BF16),
                   jax.ShapeDtypeStruct((N_HEADS, s_len, 1), F32)],
        compiler_params=_cparams(("arbitrary", "arbitrary")),
    )(q_tab, k_tab, q, kv, kr)


def _attn_bwd(q, kv, kr, o, lse, do):
    s_len = q.shape[0]
    t = _attn_tile(s_len)
    nb = s_len // t
    hp = ATTN_HEADS_PER_STEP
    wide = hp * HEAD_PAD
    scale = 1.0 / math.sqrt(QK_NOPE + QK_ROPE)
    q_tab, k_tab = _causal_pairs(nb, k_major=True)

    def body(qt, kt, q_ref, kv_ref, kr_ref, o_ref, lse_ref, do_ref, dq_ref, dkv_ref, dkr_ref, dk_s, dv_s):
        g, pair = pl.program_id(0), pl.program_id(1)
        qb, kb = qt[pair], kt[pair]

        @pl.when(jnp.logical_and(g == 0, pair == 0))
        def _():
            dkr_ref[...] = jnp.zeros(dkr_ref.shape, F32)

        @pl.when(pair == 0)
        def _():
            dq_ref[...] = jnp.zeros(dq_ref.shape, F32)

        @pl.when(qb == kb)
        def _():
            dk_s[...] = jnp.zeros(dk_s.shape, F32)
            dv_s[...] = jnp.zeros(dv_s.shape, F32)

        def step(diagonal):
            for h in range(hp):
                lanes = slice(h * HEAD_PAD, (h + 1) * HEAD_PAD)
                for r0, nr, nk in _sub_blocks(t, diagonal):
                    rows, keys = slice(r0, r0 + nr), slice(0, nk)
                    qv, kvv, dov = q_ref[rows, lanes], kv_ref[keys, lanes], do_ref[rows, lanes]
                    kc = _keys(kvv, kr_ref[keys, :])
                    p = jnp.exp(_scores(qv, kc, scale, diagonal) - lse_ref[h, rows])
                    delta = jnp.sum(dov.astype(F32) * o_ref[rows, lanes].astype(F32), axis=-1, keepdims=True)
                    dp = lax.dot_general(dov, kvv, NT, preferred_element_type=F32)
                    ds = p * (dp - delta) * scale
                    dv_s[keys, lanes] += lax.dot_general(p.T.astype(BF16), dov, NN, preferred_element_type=F32)
                    dk_s[keys, lanes] += lax.dot_general(ds.T.astype(BF16), qv, NN, preferred_element_type=F32)
                    q_rows = pl.ds(pl.multiple_of(qb * t + r0, nr), nr)
                    dq_ref[q_rows, lanes] += lax.dot_general(ds.astype(BF16), kc, NN, preferred_element_type=F32)

        @pl.when(qb > kb)
        def _():
            step(False)

        @pl.when(qb == kb)
        def _():
            step(True)

        @pl.when(qb == nb - 1)
        def _():
            lane = lax.broadcasted_iota(jnp.int32, (t, HEAD_PAD), 1)
            rows = pl.ds(pl.multiple_of(kb * t, t), t)
            for h in range(hp):
                lanes = slice(h * HEAD_PAD, (h + 1) * HEAD_PAD)
                dkv_ref[:, lanes] = jnp.where(lane < QK_NOPE, dk_s[:, lanes], dv_s[:, lanes])
                dkr_ref[rows, :] += jnp.where(lane >= QK_NOPE, dk_s[:, lanes], 0.0)

    all_lanes = N_HEADS * HEAD_PAD
    qmap = lambda h, p, qt, kt: (qt[p], h)
    kmap = lambda h, p, qt, kt: (kt[p], h)
    grid_spec = pltpu.PrefetchScalarGridSpec(
        num_scalar_prefetch=2, grid=(N_HEADS // hp, q_tab.shape[0]),
        in_specs=[pl.BlockSpec((t, wide), qmap),
                  pl.BlockSpec((t, wide), kmap),
                  pl.BlockSpec((t, HEAD_PAD), lambda h, p, qt, kt: (kt[p], 0)),
                  pl.BlockSpec((t, wide), qmap),
                  pl.BlockSpec((hp, t, 1), lambda h, p, qt, kt: (h, qt[p], 0)),
                  pl.BlockSpec((t, wide), qmap)],
        out_specs=[pl.BlockSpec((s_len, wide), lambda h, p, qt, kt: (0, h)),
                   pl.BlockSpec((t, wide), kmap),
                   pl.BlockSpec((s_len, HEAD_PAD), lambda h, p, qt, kt: (0, 0))],
        scratch_shapes=[pltpu.VMEM((t, wide), F32), pltpu.VMEM((t, wide), F32)])
    return pl.pallas_call(
        body, name="attn_bwd", grid_spec=grid_spec,
        out_shape=[jax.ShapeDtypeStruct((s_len, all_lanes), F32),
                   jax.ShapeDtypeStruct((s_len, all_lanes), F32),
                   jax.ShapeDtypeStruct((s_len, HEAD_PAD), F32)],
        compiler_params=_cparams(("arbitrary", "arbitrary")),
    )(q_tab, k_tab, q, kv, kr, o, lse, do)


def _adamw(name, w, g, m, v):
    rows, cols = w.shape
    tr = _div_tile(rows, max(8, (2 * 1024 * 1024) // (4 * cols)), 8)

    def body(w_ref, g_ref, m_ref, v_ref, d_ref, nm_ref, nv_ref):
        gv = g_ref[...]
        nm = ADAM_B1 * m_ref[...] + (1.0 - ADAM_B1) * gv
        nv = ADAM_B2 * v_ref[...] + (1.0 - ADAM_B2) * jnp.square(gv)
        m_hat = nm / (1.0 - ADAM_B1 ** ADAM_STEP)
        v_hat = nv / (1.0 - ADAM_B2 ** ADAM_STEP)
        d_ref[...] = -ADAM_LR * (m_hat / (jnp.sqrt(v_hat) + ADAM_EPS) + ADAM_WD * w_ref[...])
        nm_ref[...] = nm
        nv_ref[...] = nv

    spec = pl.BlockSpec((tr, cols), lambda i: (i, 0))
    return pl.pallas_call(
        body, name=name, grid=(rows // tr,), in_specs=[spec] * 4, out_specs=[spec] * 3,
        out_shape=[jax.ShapeDtypeStruct((rows, cols), F32)] * 3,
        compiler_params=_cparams(("parallel",)),
    )(w, g, m, v)


ALL7 = (1, 2, 3, 4, 5, 6, 7)
CHIPS = (2, 4, 6)


def _all_gather(name, src, masks):
    bits = 0
    for m in masks:
        bits |= m
    nslots = {7: 8, 6: 4}[bits]
    nm = len(masks)

    def slot_of(x, y, c):
        return {7: 4 * x + 2 * y + c, 6: 2 * x + y}[bits]

    def body(src_ref, out_ref, send_sems, recv_sems, local_sem):
        x, y, c = lax.axis_index("x"), lax.axis_index("y"), lax.axis_index("c")
        mine = slot_of(x, y, c)
        own = pltpu.make_async_copy(src_ref, out_ref.at[mine], local_sem)
        own.start()
        copies = []
        for i, m in enumerate(masks):
            peer = _peer(x, y, c, m)
            copies.append((
                pltpu.make_async_remote_copy(
                    src_ref=src_ref, dst_ref=out_ref.at[mine], send_sem=send_sems.at[i], recv_sem=recv_sems.at[i],
                    device_id=peer, device_id_type=MESH),
                pltpu.make_async_remote_copy(
                    src_ref=src_ref, dst_ref=out_ref.at[slot_of(*peer)], send_sem=send_sems.at[i],
                    recv_sem=recv_sems.at[i], device_id=peer, device_id_type=MESH)))
        for send, _ in copies:
            send.start()
        for _, arrival in copies:
            arrival.wait_recv()
        for send, _ in copies:
            send.wait_send()
        own.wait()

    return pl.pallas_call(
        body, name=name,
        in_specs=[pl.BlockSpec(memory_space=pl.ANY)], out_specs=pl.BlockSpec(memory_space=pl.ANY),
        out_shape=jax.ShapeDtypeStruct((nslots,) + tuple(src.shape), src.dtype),
        scratch_shapes=[pltpu.SemaphoreType.DMA((nm,)), pltpu.SemaphoreType.DMA((nm,)), pltpu.SemaphoreType.DMA],
    )(src)


def _peer(x, y, c, m):
    return (1 - x if m & 4 else x, 1 - y if m & 2 else y, 1 - c if m & 1 else c)


def _comm_call(name, emit, srcs, out_shapes, n_sems, in_place=False):
    n = len(srcs)

    def body(*refs):
        src_refs, out_refs = refs[:n], refs[n:n + len(out_shapes)]
        send_sems, recv_sems = refs[-2], refs[-1]

        def copy(src, dst, i, peer):
            return pltpu.make_async_remote_copy(src_ref=src, dst_ref=dst, send_sem=send_sems.at[i],
                                                recv_sem=recv_sems.at[i], device_id=peer, device_id_type=MESH)

        emit(lax.axis_index("x"), lax.axis_index("y"), lax.axis_index("c"), src_refs, out_refs, copy)

    hbm = pl.BlockSpec(memory_space=pl.ANY)
    return pl.pallas_call(
        body, name=name, in_specs=[hbm] * n, out_specs=[hbm] * len(out_shapes), out_shape=out_shapes,
        scratch_shapes=[pltpu.SemaphoreType.DMA((n_sems,)), pltpu.SemaphoreType.DMA((n_sems,))],
        input_output_aliases={i: i for i in range(n)} if in_place else {},
    )(*srcs)


HBM_SPEC = pl.BlockSpec(memory_space=pltpu.HBM)
SEM_SPEC = pl.BlockSpec(memory_space=pltpu.SEMAPHORE)
DATAFLOW = pltpu.SideEffectType.DATAFLOW_SIDE_EFFECTING


def _chip_copies(srcs, lands, send_sems, recv_sems, mode):
    x, y, c = lax.axis_index("x"), lax.axis_index("y"), lax.axis_index("c")
    chip = 2 * x + y
    sends, arrivals = [], []
    for j, m in enumerate(CHIPS):
        px, py, _ = _peer(x, y, c, m)
        theirs = 2 * px + py
        for k, (s, l) in enumerate(zip(srcs, lands)):
            if mode == "gather":
                src, dst, got = s.at[c], l.at[chip, c], l.at[theirs, c]
            else:
                src, dst, got = s.at[theirs], l.at[chip], l.at[theirs]
            for to, group in ((dst, sends), (got, arrivals)):
                group.append(pltpu.make_async_remote_copy(
                    src_ref=src, dst_ref=to, send_sem=send_sems.at[3 * k + j], recv_sem=recv_sems.at[3 * k + j],
                    device_id=(px, py, c), device_id_type=MESH))
    return sends, arrivals


def _split_start(name, srcs, land_shapes, mode, after):
    n = len(srcs)

    def body(*refs):
        sends, _ = _chip_copies(refs[:n], refs[n:2 * n], refs[2 * n + 1], refs[2 * n + 2], mode)
        for cp in sends:
            cp.start()
        token = refs[-1]
        token[...] = jnp.zeros(token.shape, token.dtype)

    hbm = lambda a: pltpu.with_memory_space_constraint(a, pltpu.HBM)
    lands = [hbm(lax.empty(s.shape, s.dtype)) for s in land_shapes]
    bufs = [pltpu.HBM(a.shape, a.dtype) for a in list(srcs) + lands]
    res = pl.pallas_call(
        body, name=name,
        out_shape=(pltpu.SemaphoreType.DMA((3 * n,)), pltpu.SemaphoreType.DMA((3 * n,)), *bufs,
                   jax.ShapeDtypeStruct((SUBLANES, LANE), F32)),
        in_specs=[HBM_SPEC] * (2 * n) + [pl.BlockSpec(memory_space=pl.ANY)],
        out_specs=[SEM_SPEC, SEM_SPEC] + [HBM_SPEC] * (2 * n) + [pl.BlockSpec(memory_space=pltpu.VMEM)],
        input_output_aliases={i: 2 + i for i in range(2 * n)},
        compiler_params=pltpu.CompilerParams(has_side_effects=DATAFLOW),
    )(*[hbm(s) for s in srcs], *lands, after)
    return res[0], res[1], res[2:2 + n], res[2 + n:2 + 2 * n], res[-1]


def _split_wait(name, send_sems, recv_sems, srcs, lands, mode, after):
    n = len(srcs)

    def body(*refs):
        sends, arrivals = _chip_copies(refs[:n], refs[n:2 * n], refs[2 * n], refs[2 * n + 1], mode)
        for cp in sends:
            cp.wait_send()
        for cp in arrivals:
            cp.wait_recv()

    res = pl.pallas_call(
        body, name=name,
        out_shape=tuple(pltpu.HBM(a.shape, a.dtype) for a in list(srcs) + list(lands)),
        in_specs=[HBM_SPEC] * (2 * n) + [SEM_SPEC, SEM_SPEC, pl.BlockSpec(memory_space=pl.ANY)],
        out_specs=[HBM_SPEC] * (2 * n),
        input_output_aliases={i: i for i in range(2 * n)},
        compiler_params=pltpu.CompilerParams(has_side_effects=DATAFLOW),
    )(*srcs, *lands, send_sems, recv_sems, after)
    return res[n:]


def _relay_sibling(lands):
    def emit(x, y, c, srcs, outs, copy):
        sib = (x, y, 1 - c)
        sends, arrivals = [], []
        for j, m in enumerate(CHIPS):
            px, py, _ = _peer(x, y, c, m)
            theirs = 2 * px + py
            for k, (s, o) in enumerate(zip(srcs, outs)):
                sends.append(copy(s.at[theirs, c], o.at[theirs, c], 3 * k + j, sib))
                arrivals.append(copy(s.at[theirs, c], o.at[theirs, 1 - c], 3 * k + j, sib))
        for cp in sends:
            cp.start()
        for cp in arrivals:
            cp.wait_recv()
        for cp in sends:
            cp.wait_send()

    shapes = [jax.ShapeDtypeStruct(l.shape, l.dtype) for l in lands]
    return _comm_call("relay_weights", emit, lands, shapes, 3 * len(lands), in_place=True)


def _gather_weights(halves):
    n = len(halves)

    def emit(x, y, c, srcs, outs, copy):
        chip = 2 * x + y
        sib = (x, y, 1 - c)
        first, relay, landed, relayed = [], [], [], []
        for j, m in enumerate(CHIPS):
            px, py, _ = _peer(x, y, c, m)
            theirs = 2 * px + py
            for k in range(n):
                i = 6 * k + j
                first.append(copy(srcs[k].at[c], outs[k].at[chip, c], i, (px, py, c)))
                landed.append(copy(srcs[k].at[c], outs[k].at[theirs, c], i, (px, py, c)))
                relay.append(copy(outs[k].at[theirs, c], outs[k].at[theirs, c], i + 3, sib))
                relayed.append(copy(outs[k].at[theirs, 1 - c], outs[k].at[theirs, 1 - c], i + 3, sib))
        for cp in first:
            cp.start()
        for arrival, onward in zip(landed, relay):
            arrival.wait_recv()
            onward.start()
        for arrival in relayed:
            arrival.wait_recv()
        for cp in first + relay:
            cp.wait_send()

    shapes = [jax.ShapeDtypeStruct((4,) + h.shape, h.dtype) for h in halves]
    return _comm_call("gather_weights", emit, halves, shapes, 6 * n)


def _pair_exchange(name, chunks):
    def emit(x, y, c, srcs, outs, copy):
        sib = (x, y, 1 - c)
        sends = [copy(s.at[:, 1 - c], o, k, sib) for k, (s, o) in enumerate(zip(srcs, outs))]
        for cp in sends:
            cp.start()
        for cp in sends:
            cp.wait_recv()
        for cp in sends:
            cp.wait_send()

    shapes = [jax.ShapeDtypeStruct((4,) + g.shape[2:], g.dtype) for g in chunks]
    return _comm_call(name, emit, chunks, shapes, len(chunks))


def _chips_alltoall(parts):
    def emit(x, y, c, srcs, outs, copy):
        chip = 2 * x + y
        sends, arrivals = [], []
        for j, m in enumerate(CHIPS):
            px, py, _ = _peer(x, y, c, m)
            theirs = 2 * px + py
            for k, (s, o) in enumerate(zip(srcs, outs)):
                sends.append(copy(s.at[theirs], o.at[chip], 3 * k + j, (px, py, c)))
                arrivals.append(copy(s.at[theirs], o.at[theirs], 3 * k + j, (px, py, c)))
        for cp in sends:
            cp.start()
        for cp in arrivals:
            cp.wait_recv()
        for cp in sends:
            cp.wait_send()

    shapes = [jax.ShapeDtypeStruct(p.shape, p.dtype) for p in parts]
    return _comm_call("reduce_chips_exchange", emit, parts, shapes, 3 * len(parts))


def _share_sibling(parts):
    def emit(x, y, c, srcs, outs, copy):
        sib = (x, y, 1 - c)
        sends = [copy(s, o.at[c], k, sib) for k, (s, o) in enumerate(zip(srcs, outs))]
        arrivals = [copy(s, o.at[1 - c], k, sib) for k, (s, o) in enumerate(zip(srcs, outs))]
        for cp in sends:
            cp.start()
        for cp in arrivals:
            cp.wait_recv()
        for cp in sends:
            cp.wait_send()

    shapes = [jax.ShapeDtypeStruct((2,) + p.shape, p.dtype) for p in parts]
    return _comm_call("share_sibling", emit, parts, shapes, len(parts))


def _reduce_pair(name, chunk, from_sibling, core):
    n, _, h, cols = chunk.shape
    rt = _div_tile(h, max(16, (1 << 20) // (4 * cols)), 16)

    def body(core_ref, a_ref, b_ref, o_ref):
        o_ref[...] = (a_ref[...] + b_ref[...]).astype(o_ref.dtype)

    grid_spec = pltpu.PrefetchScalarGridSpec(
        num_scalar_prefetch=1, grid=(n, h // rt),
        in_specs=[pl.BlockSpec((None, None, rt, cols), lambda s, i, core_ref: (s, core_ref[0], i, 0)),
                  pl.BlockSpec((None, rt, cols), lambda s, i, core_ref: (s, i, 0))],
        out_specs=pl.BlockSpec((None, rt, cols), lambda s, i, core_ref: (s, i, 0)))
    return pl.pallas_call(
        body, name=name, grid_spec=grid_spec, out_shape=jax.ShapeDtypeStruct((n, h, cols), BF16),
        compiler_params=_cparams(("parallel", "parallel")),
    )(core, chunk, from_sibling)


def _reduce_quad(name, q):
    _, h, cols = q.shape
    rt = _div_tile(h, max(16, (1 << 20) // (4 * cols)), 16)

    def body(q_ref, o_ref):
        v = q_ref[...].astype(F32)
        o_ref[...] = ((v[0] + v[1]) + v[2]) + v[3]

    return pl.pallas_call(
        body, name=name, grid=(h // rt,),
        in_specs=[pl.BlockSpec((4, rt, cols), lambda i: (0, i, 0))],
        out_specs=pl.BlockSpec((rt, cols), lambda i: (i, 0)),
        out_shape=jax.ShapeDtypeStruct((h, cols), F32),
        compiler_params=_cparams(("parallel",)),
    )(q)


def _unshard(seg, kind):
    n, r, c = seg.shape
    if kind == "col":
        return seg.transpose(1, 0, 2).reshape(r, n * c)
    return seg.reshape(n * r, c)


def _pad_rows(flat, rows):
    n, ln = flat.shape
    return jnp.pad(flat, ((0, 0), (0, rows * PACK_COLS - ln))).reshape(n, rows, PACK_COLS)


def _block_diag_pairs(w):
    n2, bs, _ = w.shape
    eye = jnp.eye(2, dtype=w.dtype)
    z = w.reshape(n2 // 2, 2, bs, 1, bs) * eye[None, :, None, :, None]
    return z.reshape(n2 // 2, 2 * bs, 2 * bs).transpose(1, 0, 2).reshape(2 * bs, n2 * bs)


def _block_diag_pairs_t(d, bs=64):
    n = d.shape[1] // (2 * bs)
    z = d.reshape(2 * bs, n, 2 * bs).transpose(1, 0, 2).reshape(n, 2, bs, 2, bs)
    return jnp.stack([z[:, 0, :, 0, :], z[:, 1, :, 1, :]], axis=1).reshape(2 * n, bs, bs)


BIG = (("w_in", "col"), ("w_uq", "col"), ("w_ukv", "col"), ("w_proj_rnn", "row"), ("w_proj_mla", "row"),
       ("w_out", "row"), ("w_up", "col"), ("w_down", "row"))
FIRST_USED = ("w_in", "w_uq", "w_ukv")
CONVS = (("conv_w", "col"), ("ffn_conv_w", "col"))
SMALL = ("b_ada", "norm1_g", "conv_b", "w_gate_a", "b_gate_a", "w_gate_x", "b_gate_x", "lru_param",
         "q_norm_g", "kv_norm_g", "norm2_g", "ffn_conv_b", "final_g")
WEIGHTS = ("w_ada", "b_ada", "norm1_g", "w_in", "conv_w", "conv_b", "w_gate_a", "b_gate_a", "w_gate_x",
           "b_gate_x", "lru_param", "q_norm_g", "w_uq", "kv_norm_g", "w_ukv", "w_proj_rnn", "w_proj_mla",
           "w_out", "norm2_g", "w_up", "ffn_conv_w", "ffn_conv_b", "w_down", "final_g")


def _step(x, c, positions, w, m_in, v_in, loss_target):
    s_len, d = x.shape[1], x.shape[2]
    x2d = x[0]
    tgt = loss_target[0]
    xi, yi, ci = lax.axis_index("x"), lax.axis_index("y"), lax.axis_index("c")
    chip = 2 * xi + yi
    me = 2 * chip + ci
    tile = min(256, s_len)
    nt = s_len // tile

    local2d = {k: w[k][0] for k, _ in BIG + CONVS}
    kinds = dict(BIG)
    halves_bf = {k: local2d[k].astype(BF16).reshape(2, local2d[k].shape[0] // 2, local2d[k].shape[1]) for k, _ in BIG}
    first_names = [k for k, _ in BIG if k in FIRST_USED]
    later_names = [k for k, _ in BIG if k not in FIRST_USED]
    full = {}

    def assemble(k, g):
        g = lax.dynamic_update_index_in_dim(g, halves_bf[k][None], chip, 0).reshape((4,) + local2d[k].shape)
        if k == "w_up":
            full["w_up_gate"], full["w_up_val"] = _unshard(g[:2], kinds[k]), _unshard(g[2:], kinds[k])
        else:
            full[k] = _unshard(g, kinds[k])

    first_got = _gather_weights([halves_bf[k] for k in first_names])
    for k, g in zip(first_names, first_got):
        assemble(k, g)
    conv_flat = jnp.concatenate([local2d[k].reshape(-1) for k, _ in CONVS])
    conv_rows = -(-conv_flat.shape[0] // PACK_COLS)
    conv_all = _all_gather("gather_conv_w", _pad_rows(conv_flat[None], conv_rows)[0], CHIPS)
    conv_all = conv_all.reshape(4, -1)
    off = 0
    for k, kind in CONVS:
        r, cc = local2d[k].shape
        full[k] = _unshard(conv_all[:, off:off + r * cc].reshape(4, r, cc), kind)
        off += r * cc

    d_rnn = w["conv_b"].shape[1]
    n_q, n_kv = w["q_norm_g"].shape[1], w["kv_norm_g"].shape[1]
    w_in = full["w_in"]
    o1, o2, o3 = d_rnn + n_q, d_rnn + n_q + n_kv, d_rnn + n_q + n_kv + QK_ROPE
    w_rnn = w_in[:, :d_rnn]
    zpad = lambda n: jnp.zeros((d, n), BF16)
    w_qkv = jnp.concatenate([w_in[:, d_rnn:o2], zpad(QK_NOPE), w_in[:, o2:o3], zpad(LANE - QK_NOPE - QK_ROPE)], axis=1)
    w_g = w_in[:, o3:]
    hd = QK_NOPE + QK_ROPE
    w_uq = jnp.pad(full["w_uq"].reshape(n_q, N_HEADS, hd), ((0, 0), (0, 0), (0, HEAD_PAD - hd))).reshape(n_q, -1)
    w_ukv = full["w_ukv"]
    v_head = w_ukv.shape[1] // N_HEADS - QK_NOPE
    d_ff = w["ffn_conv_b"].shape[1] // 2
    ffn_cw_gate, ffn_cw_val = full["ffn_conv_w"][:, :d_ff], full["ffn_conv_w"][:, d_ff:]
    ffn_cb_gate, ffn_cb_val = w["ffn_conv_b"][:, :d_ff], w["ffn_conv_b"][:, d_ff:]
    conv_w, conv_b = full["conv_w"], w["conv_b"]
    wa_bd = _block_diag_pairs(w["w_gate_a"][0])
    wx_bd = _block_diag_pairs(w["w_gate_x"][0])

    c_all = _all_gather("gather_c", c, ALL7).reshape(8, d)
    c_rows = 128
    (c_act,) = _tiled("silu_c", lambda v: (_silu(v),), 1, [(jnp.pad(c_all, ((0, c_rows - 8), (0, 0))), (c_rows, d), "full")],
                      [((c_rows, d), F32, (c_rows, d), "full")])
    w_ada = w["w_ada"][0]
    n_mod = w_ada.shape[1]
    b_loc = lax.dynamic_slice_in_dim(w["b_ada"], chip * n_mod, n_mod, axis=1)
    mod_loc = _mm("ada_fwd", c_act, w_ada, add=jnp.broadcast_to(b_loc, (c_rows, n_mod)))
    mod_all = _all_gather("gather_mod", mod_loc[:8], CHIPS)
    mod = lax.dynamic_index_in_dim(mod_all, me, 1, keepdims=False).reshape(1, -1)
    shift1, scale1, gate1, shift2, scale2, gate2 = [mod[:, i * d:(i + 1) * d] for i in range(6)]

    small_done = (mod[:, :1] + conv_all[:1, :1] + first_got[0][0, 0, :1, :1].astype(F32))
    later_flight = _split_start(
        "gather_later_start", [halves_bf[k] for k in later_names],
        [jax.ShapeDtypeStruct((4,) + halves_bf[k].shape, BF16) for k in later_names], "gather", after=small_done)

    half = QK_ROPE // 2
    inv_freq = ROPE_THETA ** (-jnp.arange(half, dtype=F32) / half)
    ang = positions[0].astype(F32)[:, None] * inv_freq
    cos, sin = jnp.cos(ang), jnp.sin(ang)
    one, zero = jnp.ones((s_len, QK_NOPE), F32), jnp.zeros((s_len, half), F32)
    tail = jnp.zeros((s_len, LANE - QK_NOPE - QK_ROPE), F32)
    cos_f = jnp.concatenate([one, cos, cos, tail + 1.0], axis=1)
    sin_a = jnp.concatenate([one * 0.0, -sin, zero, tail], axis=1)
    sin_b = jnp.concatenate([one * 0.0, zero, sin, tail], axis=1)
    reset = (positions[0] == 0).astype(F32)[:, None]
    tabs = [(cos_f, (tile, LANE), "row"), (sin_a, (tile, LANE), "row"), (sin_b, (tile, LANE), "row")]

    def rowspec(a):
        return (a, (tile, a.shape[1]), "row")

    def full2(a):
        return (a, a.shape, "full")

    def rowout(cols, dt):
        return ((s_len, cols), dt, (tile, cols), "row")

    def accout(a):
        return (a.shape, F32, a.shape, "acc")

    norm1_g = w["norm1_g"] + later_flight[4][:1, :1]
    norm2_g, final_g = w["norm2_g"], w["final_g"].reshape(1, d)
    ln1_in = [rowspec(x2d), full2(norm1_g), full2(scale1), full2(shift1)]
    (h1,) = _tiled("ln1", _f_ln, nt, ln1_in, [rowout(d, BF16)])
    x_rnn = _mm("in_rnn", h1, w_rnn)
    qkv = _mm("in_qkv", h1, w_qkv)
    gates = _mm("in_gates", h1, w_g)

    ct = LANE
    n_ct = d_rnn // ct
    colspec = lambda a, width=ct: (a, (a.shape[0], width), "col")
    lru_in = [colspec(x_rnn), colspec(conv_w), colspec(conv_b), colspec(wa_bd), colspec(w["b_gate_a"]),
              colspec(wx_bd), colspec(w["b_gate_x"]), colspec(w["lru_param"]), full2(reset)]
    y_rnn, h_rnn = _tiled("lru_fwd", _f_lru_fwd, n_ct, lru_in,
                          [((s_len, d_rnn), BF16, (s_len, ct), "col"), ((s_len, d_rnn), F32, (s_len, ct), "col")])

    qkv_in = [rowspec(qkv)] + tabs + [full2(w["q_norm_g"]), full2(w["kv_norm_g"])]
    qn, kvn, kr = _tiled("qkv_norm", _f_qkv, nt, qkv_in, [rowout(n_q, BF16), rowout(n_kv, BF16), rowout(LANE, BF16)])
    q_pre = _mm("up_q", qn, w_uq)
    kv = _mm("up_kv", kvn, w_ukv, out_dtype=BF16)
    (q_cat,) = _tiled("rot_q", _f_rotq, nt, [rowspec(q_pre)] + tabs, [rowout(q_pre.shape[1], BF16)])
    o_mla, lse = _attn_fwd(q_cat, kv, kr)

    send_sems, recv_sems, flown, landed, _ = later_flight
    landed = _split_wait("gather_later_wait", send_sems, recv_sems, flown, landed, "gather", after=o_mla)
    for k, g in zip(later_names, _relay_sibling(landed)):
        assemble(k, g)
    w_pr = full["w_proj_rnn"]
    w_pm = jnp.pad(full["w_proj_mla"].reshape(N_HEADS, v_head, d), ((0, 0), (HEAD_PAD - v_head, 0), (0, 0))).reshape(-1, d)
    w_out = full["w_out"]
    w_up_gate, w_up_val = full["w_up_gate"], full["w_up_val"]
    w_down = full["w_down"]

    p_rnn = _mm("proj_rnn", y_rnn, w_pr)
    p_mla = _mm("proj_mla", o_mla, w_pm)
    merge_in = [rowspec(gates), rowspec(p_rnn), rowspec(p_mla)]
    (merged,) = _tiled("merge", _f_merge, nt, merge_in, [rowout(d, BF16)])
    o_tok = _mm("out_proj", merged, w_out)
    res_in = [rowspec(x2d), rowspec(o_tok), full2(gate1), full2(norm2_g), full2(scale2), full2(shift2)]
    x1, h2 = _tiled("res_ln2", _f_res_ln, nt, res_in, [rowout(d, F32), rowout(d, BF16)])
    u_gate = _mm("ffn_up_gate", h2, w_up_gate)
    u_val = _mm("ffn_up_val", h2, w_up_val)
    n_ft = d_ff // LANE
    ffn_in = [colspec(a) for a in (u_gate, u_val, ffn_cw_gate, ffn_cw_val, ffn_cb_gate, ffn_cb_val)]
    (act,) = _tiled("ffn_conv", _f_ffn, n_ft, ffn_in, [((s_len, d_ff), BF16, (s_len, LANE), "col")])
    f_tok = _mm("ffn_down", act, w_down)

    loss_in = [rowspec(x1), rowspec(f_tok), rowspec(tgt), full2(gate2), full2(final_g)]
    dx1, df, loss_row, d_gate2, d_final_g = _tiled(
        "loss", _f_loss_and_grads, nt, loss_in,
        [rowout(d, F32), rowout(d, BF16), ((1, LANE), F32, (1, LANE), "acc"), accout(gate2), accout(final_g)])
    loss = lax.psum(loss_row[0, 0], ("x", "y", "c"))

    d_act = _mm("ffn_down_dx", df, w_down, tb=True)
    g_w_down = _mm("ffn_down_dw", act, df, ta=True)
    taps = ffn_cw_gate.shape[0]
    du_gate, du_val, g_cw_gate, g_cw_val, g_cb_gate, g_cb_val = _tiled(
        "ffn_conv_bwd", _vjp_of(_f_ffn, 6, (0, 1, 2, 3, 4, 5)), n_ft, ffn_in + [colspec(d_act)],
        [((s_len, d_ff), BF16, (s_len, LANE), "col")] * 2 + [((taps, d_ff), F32, (taps, LANE), "col")] * 2
        + [((1, d_ff), F32, (1, LANE), "col")] * 2)
    dh2 = _mm("ffn_up_gate_dx", du_gate, w_up_gate, tb=True)
    dh2 = _mm("ffn_up_val_dx", du_val, w_up_val, tb=True, add=dh2)
    g_w_up = jnp.concatenate([_mm("ffn_up_gate_dw", h2, du_gate, ta=True), _mm("ffn_up_val_dw", h2, du_val, ta=True)], axis=1)
    g_ffn_cw = jnp.concatenate([g_cw_gate, g_cw_val], axis=1)
    g_ffn_cb = jnp.concatenate([g_cb_gate, g_cb_val], axis=1)

    res_bwd = _vjp_of(_f_res_ln, 6, (0, 1, 2, 3, 4, 5))
    dx_res, do_tok, d_gate1, g_norm2, d_scale2, d_shift2 = _tiled(
        "res_ln2_bwd", res_bwd, nt, res_in + [rowspec(dx1), rowspec(dh2)],
        [rowout(d, F32), rowout(d, BF16), accout(gate1), accout(norm2_g), accout(scale2), accout(shift2)])
    d_merged = _mm("out_proj_dx", do_tok, w_out, tb=True)
    g_w_out = _mm("out_proj_dw", merged, do_tok, ta=True)
    d_gates, dp_rnn, dp_mla = _tiled(
        "merge_bwd", _f_merge_bwd, nt, merge_in + [rowspec(d_merged)],
        [rowout(gates.shape[1], BF16), rowout(d, BF16), rowout(d, BF16)])
    dy_rnn = _mm("proj_rnn_dx", dp_rnn, w_pr, tb=True)
    g_w_pr = _mm("proj_rnn_dw", y_rnn, dp_rnn, ta=True)
    do_mla = _mm("proj_mla_dx", dp_mla, w_pm, tb=True, out_dtype=BF16)
    g_w_pm = _mm("proj_mla_dw", o_mla, dp_mla, ta=True)

    def chunked(k, gk):
        r, cc = local2d[k].shape
        if kinds[k] == "col":
            gk = gk.reshape(r, 4, cc).transpose(1, 0, 2)
        return gk.reshape(4, 2, r // 2, cc)

    def pair_sums(tag, names, chunks):
        out = []
        core = ci.astype(jnp.int32).reshape(1)
        for k, ck, from_sib in zip(names, chunks, _pair_exchange("reduce_pair_exchange_" + tag, chunks)):
            out.append(_reduce_pair("reduce_pair_" + k, ck, from_sib, core))
        return out

    g_later = {
        "w_proj_rnn": g_w_pr,
        "w_proj_mla": g_w_pm.reshape(N_HEADS, HEAD_PAD, d)[:, HEAD_PAD - v_head:, :].reshape(-1, d),
        "w_out": g_w_out, "w_up": g_w_up, "w_down": g_w_down,
    }
    sums_ready = pair_sums("ready", later_names, [chunked(k, g_later[k]) for k in later_names])
    ready_flight = _split_start(
        "reduce_ready_start", sums_ready, [jax.ShapeDtypeStruct(s.shape, s.dtype) for s in sums_ready], "alltoall",
        after=sums_ready[0])
    kr_held = kr + ready_flight[4][:1, :].astype(BF16)

    dq_cat, dkv, dkr = _attn_bwd(q_cat, kv, kr_held, o_mla, lse, do_mla)
    rot_bwd = _vjp_of(_f_rotq, 4, (0,))
    (dq_pre,) = _tiled("rot_q_bwd", rot_bwd, nt, [rowspec(q_pre)] + tabs + [rowspec(dq_cat)],
                       [rowout(q_pre.shape[1], BF16)])
    dqn = _mm("up_q_dx", dq_pre, w_uq, tb=True)
    g_w_uq = _mm("up_q_dw", qn, dq_pre, ta=True)
    dkv_b = dkv.astype(BF16)
    dkvn = _mm("up_kv_dx", dkv_b, w_ukv, tb=True)
    g_w_ukv = _mm("up_kv_dw", kvn, dkv_b, ta=True)
    dqkv, g_q_norm, g_kv_norm = _tiled(
        "qkv_norm_bwd", _f_qkv_bwd, nt, qkv_in + [rowspec(dqn), rowspec(dkvn), rowspec(dkr)],
        [rowout(qkv.shape[1], BF16), accout(w["q_norm_g"]), accout(w["kv_norm_g"])])

    lru_out = [((s_len, d_rnn), BF16, (s_len, ct), "col")]
    for a in (conv_w, conv_b, wa_bd, w["b_gate_a"], wx_bd, w["b_gate_x"], w["lru_param"]):
        lru_out.append((a.shape, F32, (a.shape[0], ct), "col"))
    dx_rnn, g_conv_w, g_conv_b, g_wa_bd, g_b_a, g_wx_bd, g_b_x, g_lru = _tiled(
        "lru_bwd", _f_lru_bwd, n_ct, lru_in + [colspec(h_rnn), colspec(dy_rnn)], lru_out)

    dh1 = _mm("in_gates_dx", d_gates, w_g, tb=True)
    dh1 = _mm("in_qkv_dx", dqkv, w_qkv, tb=True, add=dh1)
    dh1 = _mm("in_rnn_dx", dx_rnn, w_rnn, tb=True, add=dh1)
    g_w_rnn = _mm("in_rnn_dw", h1, dx_rnn, ta=True)
    g_w_qkv = _mm("in_qkv_dw", h1, dqkv, ta=True)
    g_w_g = _mm("in_gates_dw", h1, d_gates, ta=True)

    ln_bwd = _vjp_of(_f_ln, 4, (0, 1, 2, 3))

    def ln1_bwd(xv, gv, sc, sh, dxr, dh):
        dx, dg, dsc, dsh = ln_bwd(xv, gv, sc, sh, dh)
        return dx + dxr, dg, dsc, dsh

    grad_x, g_norm1, d_scale1, d_shift1 = _tiled(
        "ln1_bwd", ln1_bwd, nt, ln1_in + [rowspec(dx_res), rowspec(dh1)],
        [rowout(d, F32), accout(norm1_g), accout(scale1), accout(shift1)])

    dmod = jnp.concatenate([d_shift1, d_scale1, d_gate1, d_shift2, d_scale2, d_gate2], axis=1)
    dmod_all = _all_gather("gather_dmod", dmod, ALL7).reshape(8, -1)
    dmod_loc = lax.dynamic_slice_in_dim(dmod_all, chip * n_mod, n_mod, axis=1)
    g_w_ada = _mm("ada_dw", c_act, jnp.pad(dmod_loc, ((0, c_rows - 8), (0, 0))), ta=True)

    g_full = {
        "w_in": jnp.concatenate([g_w_rnn, g_w_qkv[:, :n_q + n_kv],
                                 g_w_qkv[:, n_q + n_kv + QK_NOPE:n_q + n_kv + QK_NOPE + QK_ROPE], g_w_g], axis=1),
        "w_uq": g_w_uq.reshape(n_q, N_HEADS, HEAD_PAD)[:, :, :hd].reshape(n_q, -1),
        "w_ukv": g_w_ukv,
        "conv_w": g_conv_w,
        "ffn_conv_w": g_ffn_cw,
    }
    g_small = {
        "b_ada": dmod, "norm1_g": g_norm1, "conv_b": g_conv_b,
        "w_gate_a": _block_diag_pairs_t(g_wa_bd)[None], "b_gate_a": g_b_a,
        "w_gate_x": _block_diag_pairs_t(g_wx_bd)[None], "b_gate_x": g_b_x, "lru_param": g_lru,
        "q_norm_g": g_q_norm, "kv_norm_g": g_kv_norm, "norm2_g": g_norm2,
        "ffn_conv_b": g_ffn_cb, "final_g": d_final_g.reshape(w["final_g"].shape),
    }

    small_flat = jnp.concatenate([g_small[k].reshape(-1) for k in SMALL] + [g_full[k].reshape(-1) for k, _ in CONVS])
    small_rows = -(-small_flat.shape[0] // (8 * PACK_COLS * PACK_ROW_UNIT)) * PACK_ROW_UNIT
    last_names = first_names + ["small"]
    last_chunks = [chunked(k, g_full[k]) for k in first_names]
    last_chunks.append(_pad_rows(small_flat[None], 8 * small_rows).reshape(4, 2, small_rows, PACK_COLS))
    sums_last = pair_sums("last", last_names, last_chunks)
    send_sems, recv_sems, flown, landed, _ = ready_flight
    quads_ready = _split_wait("reduce_ready_wait", send_sems, recv_sems, flown, landed, "alltoall", after=grad_x)
    reduced = {}
    for k, quad, ps in zip(later_names + last_names, list(quads_ready) + list(_chips_alltoall(sums_last)),
                           sums_ready + sums_last):
        quad = lax.dynamic_update_index_in_dim(quad, lax.dynamic_index_in_dim(ps, chip, 0, keepdims=True), chip, 0)
        reduced[k] = _reduce_quad("reduce_quad_" + k, quad)
    grads = {}
    for (k, _), both in zip(BIG, _share_sibling([reduced[k] for k, _ in BIG])):
        grads[k] = lax.dynamic_update_index_in_dim(both, reduced[k][None], ci, 0).reshape(w[k].shape)
    small_grad = _all_gather("share_small", reduced["small"], ALL7).reshape(-1)
    off = 0
    for k in SMALL:
        grads[k] = small_grad[off:off + w[k].size].reshape(w[k].shape)
        off += w[k].size
    for k, _ in CONVS:
        r, cc = local2d[k].shape
        whole = small_grad[off:off + 4 * r * cc].reshape(r, 4 * cc)
        grads[k] = lax.dynamic_slice_in_dim(whole, chip * cc, cc, axis=1)[None]
        off += 4 * r * cc
    grads["w_ada"] = g_w_ada[None]

    delta, new_m, new_v = {}, {}, {}
    for k in WEIGHTS:
        shp = w[k].shape
        two_d = (-1, shp[-1]) if len(shp) > 1 else (1, -1)
        dk, mk, vk = _adamw("adamw_" + k, w[k].reshape(two_d), grads[k].reshape(two_d),
                            m_in[k].reshape(two_d), v_in[k].reshape(two_d))
        delta[k], new_m[k], new_v[k] = dk.reshape(shp), mk.reshape(shp), vk.reshape(shp)

    return (loss, grad_x[None], *[grads[k] for k in WEIGHTS], *[delta[k] for k in WEIGHTS],
            *[new_m[k] for k in WEIGHTS], *[new_v[k] for k in WEIGHTS])


def kernel(x, c, positions, w_ada, b_ada, norm1_g, w_in, conv_w, conv_b, w_gate_a, b_gate_a, w_gate_x, b_gate_x, lru_param, q_norm_g, w_uq, kv_norm_g, w_ukv, w_proj_rnn, w_proj_mla, w_out, norm2_g, w_up, ffn_conv_w, ffn_conv_b, w_down, final_g, loss_target, m_w_ada, m_b_ada, m_norm1_g, m_w_in, m_conv_w, m_conv_b, m_w_gate_a, m_b_gate_a, m_w_gate_x, m_b_gate_x, m_lru_param, m_q_norm_g, m_w_uq, m_kv_norm_g, m_w_ukv, m_w_proj_rnn, m_w_proj_mla, m_w_out, m_norm2_g, m_w_up, m_ffn_conv_w, m_ffn_conv_b, m_w_down, m_final_g, v_w_ada, v_b_ada, v_norm1_g, v_w_in, v_conv_w, v_conv_b, v_w_gate_a, v_b_gate_a, v_w_gate_x, v_b_gate_x, v_lru_param, v_q_norm_g, v_w_uq, v_kv_norm_g, v_w_ukv, v_w_proj_rnn, v_w_proj_mla, v_w_out, v_norm2_g, v_w_up, v_ffn_conv_w, v_ffn_conv_b, v_w_down, v_final_g):
    given = dict(locals())
    w = {k: given[k] for k in WEIGHTS}
    m_in = {k: given["m_" + k] for k in WEIGHTS}
    v_in = {k: given["v_" + k] for k in WEIGHTS}
    return _step(x, c, positions, w, m_in, v_in, loss_target)
```

```python
import functools
import math

import jax
import jax.numpy as jnp
from jax import lax
from jax.experimental import pallas as pl
from jax.experimental.pallas import tpu as pltpu

F32 = jnp.float32
BF16 = jnp.bfloat16

EPS = 1e-6
LRU_C = 8.0
N_HEADS = 16
QK_NOPE = 64
QK_ROPE = 32
HEAD_PAD = 128
ROPE_THETA = 10000.0
ADAM_LR = 0.001
ADAM_B1 = 0.9
ADAM_B2 = 0.999
ADAM_EPS = 1e-08
ADAM_WD = 0.01
ADAM_STEP = 10

LANE = 128
SUBLANES = 8
VMEM_LIMIT = 48 * 1024 * 1024
MM_TILE_M = MM_TILE_N = MM_TILE_K = 1408
PACK_COLS = 1024
PACK_ROW_UNIT = 32
MESH = pl.DeviceIdType.MESH

NN = (((1,), (0,)), ((), ()))
NT = (((1,), (1,)), ((), ()))


def _cparams(sem):
    return pltpu.CompilerParams(dimension_semantics=sem, vmem_limit_bytes=VMEM_LIMIT)


def _div_tile(n, cap, unit):
    best = None
    d = unit
    while d <= min(n, cap):
        if n % d == 0:
            best = d
        d += unit
    return n if best is None else best


def _mm(name, a, b, *, ta=False, tb=False, add=None, out_dtype=F32):
    if ta:
        kdim, m = a.shape
    else:
        m, kdim = a.shape
    if tb:
        n, kb = b.shape
    else:
        kb, n = b.shape
    assert kdim == kb, (name, a.shape, b.shape)
    tm = _div_tile(m, MM_TILE_M, 8 if not ta else LANE)
    tn = _div_tile(n, MM_TILE_N, LANE)
    tk = _div_tile(kdim, MM_TILE_K, LANE)
    nk = kdim // tk
    a_spec = pl.BlockSpec((tk, tm), lambda i, j, k: (k, i)) if ta else pl.BlockSpec((tm, tk), lambda i, j, k: (i, k))
    b_spec = pl.BlockSpec((tn, tk), lambda i, j, k: (j, k)) if tb else pl.BlockSpec((tk, tn), lambda i, j, k: (k, j))
    o_spec = pl.BlockSpec((tm, tn), lambda i, j, k: (i, j))
    has_add = add is not None
    dims = ((((0,) if ta else (1,)), ((1,) if tb else (0,))), ((), ()))

    def body(*refs):
        a_ref, b_ref = refs[0], refs[1]
        c_ref = refs[2] if has_add else None
        o_ref = refs[3] if has_add else refs[2]
        prod = lax.dot_general(a_ref[...].astype(BF16), b_ref[...].astype(BF16), dims, preferred_element_type=F32)
        if nk == 1:
            o_ref[...] = (prod + c_ref[...].astype(F32) if has_add else prod).astype(o_ref.dtype)
            return
        acc = refs[-1]
        k = pl.program_id(2)

        @pl.when(k == 0)
        def _():
            acc[...] = prod + c_ref[...].astype(F32) if has_add else prod

        @pl.when(jnp.logical_and(k > 0, k < nk - 1))
        def _():
            acc[...] += prod

        @pl.when(k == nk - 1)
        def _():
            o_ref[...] = (acc[...] + prod).astype(o_ref.dtype)

    ins = [a, b] + ([add] if has_add else [])
    specs = [a_spec, b_spec] + ([o_spec] if has_add else [])
    return pl.pallas_call(
        body, name=name, grid=(m // tm, n // tn, nk), in_specs=specs, out_specs=o_spec,
        out_shape=jax.ShapeDtypeStruct((m, n), out_dtype),
        scratch_shapes=[pltpu.VMEM((tm, tn), F32)] if nk > 1 else [],
        compiler_params=_cparams(("parallel", "parallel", "arbitrary")),
    )(*ins)


_IMAPS = {
    "row": lambda i: (i, 0),
    "col": lambda i: (0, i),
    "full": lambda i: (0, 0),
    "acc": lambda i: (0, 0),
}


def _tiled(name, fn, n, ins, outs):
    ni = len(ins)
    is_acc = [k == "acc" for *_, k in outs]

    def body(*refs):
        vals = fn(*[r[...] for r in refs[:ni]])
        orefs = refs[ni:]
        if any(is_acc):
            @pl.when(pl.program_id(0) == 0)
            def _():
                for r, a in zip(orefs, is_acc):
                    if a:
                        r[...] = jnp.zeros(r.shape, r.dtype)
        for r, v, a in zip(orefs, vals, is_acc):
            if a:
                r[...] += v.astype(r.dtype)
            else:
                r[...] = v.astype(r.dtype)

    res = pl.pallas_call(
        body, name=name, grid=(n,),
        in_specs=[pl.BlockSpec(bs, _IMAPS[k]) for _, bs, k in ins],
        out_specs=[pl.BlockSpec(bs, _IMAPS[k]) for _, _, bs, k in outs],
        out_shape=[jax.ShapeDtypeStruct(s, d) for s, d, _, _ in outs],
        compiler_params=_cparams(("arbitrary",)),
    )(*[a for a, _, _ in ins])
    return tuple(res)


def _vjp_of(fn, nin, diff):
    def g(*args):
        ins, cots = args[:nin], args[nin:]

        def f(*d):
            full = list(ins)
            for i, v in zip(diff, d):
                full[i] = v
            return fn(*full)

        outs, vjp = jax.vjp(f, *[ins[i] for i in diff])
        return vjp(tuple(c.astype(o.dtype) for c, o in zip(cots, outs)))
    return g


def _shift_rows(x, k, fill, up=False):
    n = x.shape[0]
    if k % SUBLANES == 0:
        pad = jnp.full((k,) + x.shape[1:], fill, x.dtype)
        return jnp.concatenate([x[k:], pad], axis=0) if up else jnp.concatenate([pad, x[:n - k]], axis=0)
    rows = lax.broadcasted_iota(jnp.int32, x.shape, 0)
    if up:
        return jnp.where(rows < n - k, pltpu.roll(x, n - k, 0), fill)
    return jnp.where(rows >= k, pltpu.roll(x, k, 0), fill)


@functools.partial(jax.custom_vjp, nondiff_argnums=(1,))
def _delay(x, k):
    return _shift_rows(x, k, 0.0)


def _delay_fwd(x, k):
    return _shift_rows(x, k, 0.0), None


def _delay_bwd(k, _, g):
    return (_shift_rows(g, k, 0.0, up=True),)


_delay.defvjp(_delay_fwd, _delay_bwd)


@functools.partial(jax.custom_vjp, nondiff_argnums=(1,))
def _lane_roll(x, s):
    return pltpu.roll(x, s, 1)


def _lane_roll_fwd(x, s):
    return pltpu.roll(x, s, 1), None


def _lane_roll_bwd(s, _, g):
    return (pltpu.roll(g, g.shape[1] - s, 1),)


_lane_roll.defvjp(_lane_roll_fwd, _lane_roll_bwd)


@jax.custom_vjp
def _bdot(x, w):
    return lax.dot_general(x.astype(BF16), w.astype(BF16), NN, preferred_element_type=F32)


def _bdot_fwd(x, w):
    return _bdot(x, w), (x, w)


def _bdot_bwd(res, g):
    x, w = res
    gb = g.astype(BF16)
    dx = lax.dot_general(gb, w.astype(BF16), NT, preferred_element_type=F32)
    dw = lax.dot_general(x.T.astype(BF16), gb, NN, preferred_element_type=F32)
    return dx, dw


_bdot.defvjp(_bdot_fwd, _bdot_bwd)


def _sigmoid(x):
    return 0.5 * (jnp.tanh(0.5 * x) + 1.0)


def _silu(x):
    return x * _sigmoid(x)


def _rms(x, g):
    return x * lax.rsqrt(jnp.mean(x * x, axis=-1, keepdims=True) + EPS) * g


def _causal_conv(x, w, b):
    kw = w.shape[0]
    tap = lax.broadcasted_iota(jnp.int32, w.shape, 0)
    y = b
    for k in range(kw):
        d = kw - 1 - k
        wk = jnp.sum(jnp.where(tap == k, w, 0.0), axis=0, keepdims=True)
        y = y + wk * (x if d == 0 else _delay(x, d))
    return y


def _rotate(x, cos_f, sin_a, sin_b):
    reps = x.shape[1] // LANE
    if reps > 1:
        cos_f, sin_a, sin_b = (jnp.tile(t, (1, reps)) for t in (cos_f, sin_a, sin_b))
    n = x.shape[1]
    half = QK_ROPE // 2
    return x * cos_f + _lane_roll(x, n - half) * sin_a + _lane_roll(x, half) * sin_b


def _softplus_neg(l):
    u = jnp.exp(-jnp.abs(l))
    log1p_u = jnp.where(u < 0.01, u * (1.0 - u * (0.5 - u * (1.0 / 3.0))), jnp.log(1.0 + u))
    return jnp.maximum(-l, 0.0) + log1p_u


def _f_ln(x, g, scale, shift):
    return (_rms(x, g) * (1.0 + scale) + shift,)


def _f_qkv(qkv, cos_f, sin_a, sin_b, qg, kvg):
    nq, nkv = qg.shape[1], kvg.shape[1]
    qn = _rms(qkv[:, :nq], qg)
    kvn = _rms(qkv[:, nq:nq + nkv], kvg)
    kr = _rotate(qkv[:, nq + nkv:], cos_f, sin_a, sin_b)
    return qn, kvn, kr


def _f_qkv_bwd(qkv, cos_f, sin_a, sin_b, qg, kvg, dqn, dkvn, dkr):
    nq, nkv = qg.shape[1], kvg.shape[1]
    _, vjp_q = jax.vjp(_rms, qkv[:, :nq], qg)
    _, vjp_kv = jax.vjp(_rms, qkv[:, nq:nq + nkv], kvg)
    _, vjp_r = jax.vjp(lambda t: _rotate(t, cos_f, sin_a, sin_b), qkv[:, nq + nkv:])
    dq_lat, dqg = vjp_q(dqn)
    dkv_lat, dkvg = vjp_kv(dkvn)
    (dkr_pre,) = vjp_r(dkr)
    return jnp.concatenate([dq_lat, dkv_lat, dkr_pre], axis=1), dqg, dkvg


QK_SCALE = 1.0 / math.sqrt(QK_NOPE + QK_ROPE)
LOG2_E = 1.4426950408889634
LN_2 = 0.6931471805599453


def _f_rotq(q, cos_f, sin_a, sin_b):
    return (_rotate(q, cos_f, sin_a, sin_b) * (QK_SCALE * LOG2_E),)


def _f_rotq_bwd(q, cos_f, sin_a, sin_b, dq):
    _, vjp = jax.vjp(lambda t: _rotate(t, cos_f, sin_a, sin_b) * QK_SCALE, q)
    return vjp(dq)


def _merge(g_rnn, g_mla, p_rnn, p_mla):
    return _sigmoid(g_rnn) * p_rnn + _sigmoid(g_mla) * p_mla


def _f_merge(g, p_rnn, p_mla):
    d = p_rnn.shape[1]
    return (_merge(g[:, :d], g[:, d:], p_rnn, p_mla),)


def _f_merge_bwd(g, p_rnn, p_mla, dm):
    d = p_rnn.shape[1]
    _, vjp = jax.vjp(_merge, g[:, :d], g[:, d:], p_rnn, p_mla)
    dg_rnn, dg_mla, dp_rnn, dp_mla = vjp(dm)
    return jnp.concatenate([dg_rnn, dg_mla], axis=1), dp_rnn, dp_mla


def _f_res_ln(x, o, gate, g2, scale, shift):
    x1 = x + gate * o
    return x1, _rms(x1, g2) * (1.0 + scale) + shift


def _f_ffn(u_gate, u_val, cw_gate, cw_val, cb_gate, cb_val):
    return (_silu(_causal_conv(u_gate, cw_gate, cb_gate)) * _causal_conv(u_val, cw_val, cb_val),)


def _f_loss(x1, f, tgt, gate, fg):
    y = _rms(x1 + gate * f, fg)
    err = (y - tgt) * (y - tgt)
    return 0.5 * jnp.sum(jnp.mean(err, axis=-1, keepdims=True), axis=0, keepdims=True)


def _f_loss_and_grads(x1, f, tgt, gate, fg):
    loss, vjp = jax.vjp(lambda a, b, c, d: _f_loss(a, b, tgt, c, d), x1, f, gate, fg)
    dx1, df, dgate, dfg = vjp(jnp.ones((1, 1), F32))
    return dx1, df, jnp.broadcast_to(loss, (1, LANE)), dgate, dfg


def _f_lru_coeffs(xr, cw, cb, wa, ba, wx, bx, lru, reset):
    xc = _causal_conv(xr, cw, cb)
    r = _sigmoid(_bdot(xc, wa) + ba)
    i = _sigmoid(_bdot(xc, wx) + bx)
    log_a = (-LRU_C) * r * _softplus_neg(lru)
    a = jnp.exp(log_a)
    mult = jnp.sqrt(-jnp.tanh(log_a) * (1.0 + a * a))
    is_reset = reset > 0.5
    a = jnp.where(is_reset, 0.0, a)
    mult = jnp.where(is_reset, 1.0, mult)
    return a, mult * (i * xc)


SCAN_BLOCK = 64


def _scan(a, b, up=False):
    n = a.shape[0]
    blk = min(SCAN_BLOCK, n)
    pos = lax.broadcasted_iota(jnp.int32, a.shape, 0) % blk
    k = 1
    while k < blk:
        inside = (pos < blk - k) if up else (pos >= k)
        shift = n - k if up else k
        b = b + a * jnp.where(inside, pltpu.roll(b, shift, 0), 0.0)
        a = a * jnp.where(inside, pltpu.roll(a, shift, 0), 1.0)
        k *= 2
    blocks = range(n // blk)
    carry = jnp.zeros((1,) + a.shape[1:], a.dtype)
    out = [None] * len(blocks)
    for i in (reversed(blocks) if up else blocks):
        rows = slice(i * blk, (i + 1) * blk)
        out[i] = b[rows] + a[rows] * carry
        carry = out[i][:1] if up else out[i][blk - 1:]
    return jnp.concatenate(out, axis=0)


def _f_lru_fwd(xr, cw, cb, wa, ba, wx, bx, lru, reset):
    a, b = _f_lru_coeffs(xr, cw, cb, wa, ba, wx, bx, lru, reset)
    h = _scan(a, b)
    return h, h


def _f_lru_bwd(xr, cw, cb, wa, ba, wx, bx, lru, reset, h, dh):
    (a, _), vjp = jax.vjp(lambda *p: _f_lru_coeffs(*p, reset), xr, cw, cb, wa, ba, wx, bx, lru)
    g = _scan(_shift_rows(a, 1, 0.0, up=True), dh, up=True)
    return vjp((g * _shift_rows(h, 1, 0.0), g))


def _attn_tile(s):
    return 1024 if s >= 2048 else s // 2


def _keys(kv, kr):
    lane = lax.broadcasted_iota(jnp.int32, kv.shape, 1)
    return jnp.where(lane < QK_NOPE, kv, kr)


ATTN_HEADS_PER_STEP = 2


def _scores(q, kc, diagonal):
    s = lax.dot_general(q, kc, NT, preferred_element_type=F32)
    if not diagonal:
        return s
    rows = lax.broadcasted_iota(jnp.int32, s.shape, 0)
    cols = lax.broadcasted_iota(jnp.int32, s.shape, 1)
    return jnp.where(cols - (s.shape[1] - s.shape[0]) <= rows, s, -jnp.inf)


def _sub_blocks(t, diagonal):
    return ((0, t // 2, t // 2), (t // 2, t // 2, t)) if diagonal else ((0, t, t),)


def _causal_pairs(nb, k_major):
    if k_major:
        pairs = [(qb, kb) for kb in range(nb) for qb in range(kb, nb)]
    else:
        pairs = [(qb, kb) for qb in range(nb) for kb in range(qb + 1)]
    return jnp.array([p[0] for p in pairs], jnp.int32), jnp.array([p[1] for p in pairs], jnp.int32)


def _attn_fwd(q, kv, kr):
    s_len = q.shape[0]
    t = _attn_tile(s_len)
    nb = s_len // t
    hp = ATTN_HEADS_PER_STEP
    wide = hp * HEAD_PAD
    q_tab, k_tab = _causal_pairs(nb, k_major=False)

    def body(qt, kt, q_ref, kv_ref, kr_ref, o_ref, lse_ref, m_s, l_s, acc_s):
        pair = pl.program_id(1)
        qi, ki = qt[pair], kt[pair]

        @pl.when(ki == 0)
        def _():
            m_s[...] = jnp.full(m_s.shape, -jnp.inf, F32)
            l_s[...] = jnp.zeros(l_s.shape, F32)
            acc_s[...] = jnp.zeros(acc_s.shape, F32)

        def step(diagonal):
            for h in range(hp):
                lanes = slice(h * HEAD_PAD, (h + 1) * HEAD_PAD)
                for r0, nr, nk in _sub_blocks(t, diagonal):
                    rows = slice(r0, r0 + nr)
                    kvv = kv_ref[:nk, lanes]
                    s = _scores(q_ref[rows, lanes], _keys(kvv, kr_ref[:nk, :]), diagonal)
                    m_old = m_s[h, rows]
                    m_new = jnp.maximum(m_old, jnp.max(s, axis=-1, keepdims=True))
                    alpha = jnp.exp2(m_old - m_new)
                    p = jnp.exp2(s - m_new)
                    l_s[h, rows] = alpha * l_s[h, rows] + jnp.sum(p, axis=-1, keepdims=True)
                    acc_s[rows, lanes] = alpha * acc_s[rows, lanes] + lax.dot_general(
                        p.astype(BF16), kvv, NN, preferred_element_type=F32)
                    m_s[h, rows] = m_new

        @pl.when(ki < qi)
        def _():
            step(False)

        @pl.when(ki == qi)
        def _():
            step(True)
            lane = lax.broadcasted_iota(jnp.int32, (t, HEAD_PAD), 1)
            for h in range(hp):
                lanes = slice(h * HEAD_PAD, (h + 1) * HEAD_PAD)
                o_ref[:, lanes] = jnp.where(lane >= QK_NOPE, acc_s[:, lanes] / l_s[h], 0.0).astype(o_ref.dtype)
                lse_ref[h] = m_s[h] + jnp.log(l_s[h]) * LOG2_E

    grid_spec = pltpu.PrefetchScalarGridSpec(
        num_scalar_prefetch=2, grid=(N_HEADS // hp, q_tab.shape[0]),
        in_specs=[pl.BlockSpec((t, wide), lambda h, p, qt, kt: (qt[p], h)),
                  pl.BlockSpec((t, wide), lambda h, p, qt, kt: (kt[p], h)),
                  pl.BlockSpec((t, HEAD_PAD), lambda h, p, qt, kt: (kt[p], 0))],
        out_specs=[pl.BlockSpec((t, wide), lambda h, p, qt, kt: (qt[p], h)),
                   pl.BlockSpec((hp, t, 1), lambda h, p, qt, kt: (h, qt[p], 0))],
        scratch_shapes=[pltpu.VMEM((hp, t, 1), F32), pltpu.VMEM((hp, t, 1), F32), pltpu.VMEM((t, wide), F32)])
    return pl.pallas_call(
        body, name="attn_fwd", grid_spec=grid_spec,
        out_shape=[jax.ShapeDtypeStruct((s_len, N_HEADS * HEAD_PAD), BF16),
                   jax.ShapeDtypeStruct((N_HEADS, s_len, 1), F32)],
        compiler_params=_cparams(("arbitrary", "arbitrary")),
    )(q_tab, k_tab, q, kv, kr)


def _attn_bwd(q, kv, kr, o, lse, do):
    s_len = q.shape[0]
    t = _attn_tile(s_len)
    nb = s_len // t
    hp = ATTN_HEADS_PER_STEP
    wide = hp * HEAD_PAD
    q_tab, k_tab = _causal_pairs(nb, k_major=True)

    def body(qt, kt, q_ref, kv_ref, kr_ref, o_ref, lse_ref, do_ref, dq_ref, dkv_ref, dkr_ref, dk_s, dv_s):
        g, pair = pl.program_id(0), pl.program_id(1)
        qb, kb = qt[pair], kt[pair]

        @pl.when(jnp.logical_and(g == 0, pair == 0))
        def _():
            dkr_ref[...] = jnp.zeros(dkr_ref.shape, F32)

        @pl.when(pair == 0)
        def _():
            dq_ref[...] = jnp.zeros(dq_ref.shape, F32)

        @pl.when(qb == kb)
        def _():
            dk_s[...] = jnp.zeros(dk_s.shape, F32)
            dv_s[...] = jnp.zeros(dv_s.shape, F32)

        def step(diagonal):
            for h in range(hp):
                lanes = slice(h * HEAD_PAD, (h + 1) * HEAD_PAD)
                for r0, nr, nk in _sub_blocks(t, diagonal):
                    rows, keys = slice(r0, r0 + nr), slice(0, nk)
                    qv, kvv, dov = q_ref[rows, lanes], kv_ref[keys, lanes], do_ref[rows, lanes]
                    kc = _keys(kvv, kr_ref[keys, :])
                    p = jnp.exp2(_scores(qv, kc, diagonal) - lse_ref[h, rows])
                    delta = jnp.sum(dov.astype(F32) * o_ref[rows, lanes].astype(F32), axis=-1, keepdims=True)
                    dp = lax.dot_general(dov, kvv, NT, preferred_element_type=F32)
                    ds = p * (dp - delta)
                    dv_s[keys, lanes] += lax.dot_general(p.T.astype(BF16), dov, NN, preferred_element_type=F32)
                    dk_s[keys, lanes] += lax.dot_general(ds.T.astype(BF16), qv, NN, preferred_element_type=F32)
                    q_rows = pl.ds(pl.multiple_of(qb * t + r0, nr), nr)
                    dq_ref[q_rows, lanes] += lax.dot_general(ds.astype(BF16), kc, NN, preferred_element_type=F32)

        @pl.when(qb > kb)
        def _():
            step(False)

        @pl.when(qb == kb)
        def _():
            step(True)

        @pl.when(qb == nb - 1)
        def _():
            lane = lax.broadcasted_iota(jnp.int32, (t, HEAD_PAD), 1)
            rows = pl.ds(pl.multiple_of(kb * t, t), t)
            for h in range(hp):
                lanes = slice(h * HEAD_PAD, (h + 1) * HEAD_PAD)
                dk = dk_s[:, lanes] * LN_2
                dkv_ref[:, lanes] = jnp.where(lane < QK_NOPE, dk, dv_s[:, lanes])
                dkr_ref[rows, :] += jnp.where(lane >= QK_NOPE, dk, 0.0)

    all_lanes = N_HEADS * HEAD_PAD
    qmap = lambda h, p, qt, kt: (qt[p], h)
    kmap = lambda h, p, qt, kt: (kt[p], h)
    grid_spec = pltpu.PrefetchScalarGridSpec(
        num_scalar_prefetch=2, grid=(N_HEADS // hp, q_tab.shape[0]),
        in_specs=[pl.BlockSpec((t, wide), qmap),
                  pl.BlockSpec((t, wide), kmap),
                  pl.BlockSpec((t, HEAD_PAD), lambda h, p, qt, kt: (kt[p], 0)),
                  pl.BlockSpec((t, wide), qmap),
                  pl.BlockSpec((hp, t, 1), lambda h, p, qt, kt: (h, qt[p], 0)),
                  pl.BlockSpec((t, wide), qmap)],
        out_specs=[pl.BlockSpec((s_len, wide), lambda h, p, qt, kt: (0, h)),
                   pl.BlockSpec((t, wide), kmap),
                   pl.BlockSpec((s_len, HEAD_PAD), lambda h, p, qt, kt: (0, 0))],
        scratch_shapes=[pltpu.VMEM((t, wide), F32), pltpu.VMEM((t, wide), F32)])
    return pl.pallas_call(
        body, name="attn_bwd", grid_spec=grid_spec,
        out_shape=[jax.ShapeDtypeStruct((s_len, all_lanes), F32),
                   jax.ShapeDtypeStruct((s_len, all_lanes), F32),
                   jax.ShapeDtypeStruct((s_len, HEAD_PAD), F32)],
        compiler_params=_cparams(("arbitrary", "arbitrary")),
    )(q_tab, k_tab, q, kv, kr, o, lse, do)


def _adamw(name, w, g, m, v):
    rows, cols = w.shape
    tr = _div_tile(rows, max(8, (2 * 1024 * 1024) // (4 * cols)), 8)

    def body(w_ref, g_ref, m_ref, v_ref, d_ref, nm_ref, nv_ref):
        gv = g_ref[...]
        nm = ADAM_B1 * m_ref[...] + (1.0 - ADAM_B1) * gv
        nv = ADAM_B2 * v_ref[...] + (1.0 - ADAM_B2) * jnp.square(gv)
        m_hat = nm / (1.0 - ADAM_B1 ** ADAM_STEP)
        v_hat = nv / (1.0 - ADAM_B2 ** ADAM_STEP)
        d_ref[...] = -ADAM_LR * (m_hat / (jnp.sqrt(v_hat) + ADAM_EPS) + ADAM_WD * w_ref[...])
        nm_ref[...] = nm
        nv_ref[...] = nv

    spec = pl.BlockSpec((tr, cols), lambda i: (i, 0))
    return pl.pallas_call(
        body, name=name, grid=(rows // tr,), in_specs=[spec] * 4, out_specs=[spec] * 3,
        out_shape=[jax.ShapeDtypeStruct((rows, cols), F32)] * 3,
        compiler_params=_cparams(("parallel",)),
    )(w, g, m, v)


ALL7 = (1, 2, 3, 4, 5, 6, 7)
CHIPS = (2, 4, 6)


def _all_gather(name, src, masks):
    bits = 0
    for m in masks:
        bits |= m
    nslots = {7: 8, 6: 4}[bits]
    nm = len(masks)

    def slot_of(x, y, c):
        return {7: 4 * x + 2 * y + c, 6: 2 * x + y}[bits]

    def body(src_ref, out_ref, send_sems, recv_sems, local_sem):
        x, y, c = lax.axis_index("x"), lax.axis_index("y"), lax.axis_index("c")
        mine = slot_of(x, y, c)
        own = pltpu.make_async_copy(src_ref, out_ref.at[mine], local_sem)
        own.start()
        copies = []
        for i, m in enumerate(masks):
            peer = _peer(x, y, c, m)
            copies.append((
                pltpu.make_async_remote_copy(
                    src_ref=src_ref, dst_ref=out_ref.at[mine], send_sem=send_sems.at[i], recv_sem=recv_sems.at[i],
                    device_id=peer, device_id_type=MESH),
                pltpu.make_async_remote_copy(
                    src_ref=src_ref, dst_ref=out_ref.at[slot_of(*peer)], send_sem=send_sems.at[i],
                    recv_sem=recv_sems.at[i], device_id=peer, device_id_type=MESH)))
        for send, _ in copies:
            send.start()
        for _, arrival in copies:
            arrival.wait_recv()
        for send, _ in copies:
            send.wait_send()
        own.wait()

    return pl.pallas_call(
        body, name=name,
        in_specs=[pl.BlockSpec(memory_space=pl.ANY)], out_specs=pl.BlockSpec(memory_space=pl.ANY),
        out_shape=jax.ShapeDtypeStruct((nslots,) + tuple(src.shape), src.dtype),
        scratch_shapes=[pltpu.SemaphoreType.DMA((nm,)), pltpu.SemaphoreType.DMA((nm,)), pltpu.SemaphoreType.DMA],
    )(src)


def _peer(x, y, c, m):
    return (1 - x if m & 4 else x, 1 - y if m & 2 else y, 1 - c if m & 1 else c)


def _comm_call(name, emit, srcs, out_shapes, n_sems, in_place=False):
    n = len(srcs)

    def body(*refs):
        src_refs, out_refs = refs[:n], refs[n:n + len(out_shapes)]
        send_sems, recv_sems = refs[-2], refs[-1]

        def copy(src, dst, i, peer):
            return pltpu.make_async_remote_copy(src_ref=src, dst_ref=dst, send_sem=send_sems.at[i],
                                                recv_sem=recv_sems.at[i], device_id=peer, device_id_type=MESH)

        emit(lax.axis_index("x"), lax.axis_index("y"), lax.axis_index("c"), src_refs, out_refs, copy)

    hbm = pl.BlockSpec(memory_space=pl.ANY)
    return pl.pallas_call(
        body, name=name, in_specs=[hbm] * n, out_specs=[hbm] * len(out_shapes), out_shape=out_shapes,
        scratch_shapes=[pltpu.SemaphoreType.DMA((n_sems,)), pltpu.SemaphoreType.DMA((n_sems,))],
        input_output_aliases={i: i for i in range(n)} if in_place else {},
    )(*srcs)


HBM_SPEC = pl.BlockSpec(memory_space=pltpu.HBM)
SEM_SPEC = pl.BlockSpec(memory_space=pltpu.SEMAPHORE)
DATAFLOW = pltpu.SideEffectType.DATAFLOW_SIDE_EFFECTING


def _chip_copies(srcs, lands, send_sems, recv_sems, mode):
    x, y, c = lax.axis_index("x"), lax.axis_index("y"), lax.axis_index("c")
    chip = 2 * x + y
    sends, arrivals = [], []
    for j, m in enumerate(CHIPS):
        px, py, _ = _peer(x, y, c, m)
        theirs = 2 * px + py
        for k, (s, l) in enumerate(zip(srcs, lands)):
            if mode == "gather":
                src, dst, got = s.at[c], l.at[chip, c], l.at[theirs, c]
            else:
                src, dst, got = s.at[theirs], l.at[chip], l.at[theirs]
            for to, group in ((dst, sends), (got, arrivals)):
                group.append(pltpu.make_async_remote_copy(
                    src_ref=src, dst_ref=to, send_sem=send_sems.at[3 * k + j], recv_sem=recv_sems.at[3 * k + j],
                    device_id=(px, py, c), device_id_type=MESH))
    return sends, arrivals


def _split_start(name, srcs, land_shapes, mode, after):
    n = len(srcs)

    def body(*refs):
        sends, _ = _chip_copies(refs[:n], refs[n:2 * n], refs[2 * n + 1], refs[2 * n + 2], mode)
        for cp in sends:
            cp.start()
        token = refs[-1]
        token[...] = jnp.zeros(token.shape, token.dtype)

    hbm = lambda a: pltpu.with_memory_space_constraint(a, pltpu.HBM)
    lands = [hbm(lax.empty(s.shape, s.dtype)) for s in land_shapes]
    bufs = [pltpu.HBM(a.shape, a.dtype) for a in list(srcs) + lands]
    res = pl.pallas_call(
        body, name=name,
        out_shape=(pltpu.SemaphoreType.DMA((3 * n,)), pltpu.SemaphoreType.DMA((3 * n,)), *bufs,
                   jax.ShapeDtypeStruct((SUBLANES, LANE), F32)),
        in_specs=[HBM_SPEC] * (2 * n) + [pl.BlockSpec(memory_space=pl.ANY)],
        out_specs=[SEM_SPEC, SEM_SPEC] + [HBM_SPEC] * (2 * n) + [pl.BlockSpec(memory_space=pltpu.VMEM)],
        input_output_aliases={i: 2 + i for i in range(2 * n)},
        compiler_params=pltpu.CompilerParams(has_side_effects=DATAFLOW),
    )(*[hbm(s) for s in srcs], *lands, after)
    return res[0], res[1], res[2:2 + n], res[2 + n:2 + 2 * n], res[-1]


def _split_wait(name, send_sems, recv_sems, srcs, lands, mode, after):
    n = len(srcs)

    def body(*refs):
        sends, arrivals = _chip_copies(refs[:n], refs[n:2 * n], refs[2 * n], refs[2 * n + 1], mode)
        for cp in sends:
            cp.wait_send()
        for cp in arrivals:
            cp.wait_recv()

    res = pl.pallas_call(
        body, name=name,
        out_shape=tuple(pltpu.HBM(a.shape, a.dtype) for a in list(srcs) + list(lands)),
        in_specs=[HBM_SPEC] * (2 * n) + [SEM_SPEC, SEM_SPEC, pl.BlockSpec(memory_space=pl.ANY)],
        out_specs=[HBM_SPEC] * (2 * n),
        input_output_aliases={i: i for i in range(2 * n)},
        compiler_params=pltpu.CompilerParams(has_side_effects=DATAFLOW),
    )(*srcs, *lands, send_sems, recv_sems, after)
    return res[n:]


def _relay_sibling(lands):
    def emit(x, y, c, srcs, outs, copy):
        sib = (x, y, 1 - c)
        sends, arrivals = [], []
        for j, m in enumerate(CHIPS):
            px, py, _ = _peer(x, y, c, m)
            theirs = 2 * px + py
            for k, (s, o) in enumerate(zip(srcs, outs)):
                sends.append(copy(s.at[theirs, c], o.at[theirs, c], 3 * k + j, sib))
                arrivals.append(copy(s.at[theirs, c], o.at[theirs, 1 - c], 3 * k + j, sib))
        for cp in sends:
            cp.start()
        for cp in arrivals:
            cp.wait_recv()
        for cp in sends:
            cp.wait_send()

    shapes = [jax.ShapeDtypeStruct(l.shape, l.dtype) for l in lands]
    return _comm_call("relay_weights", emit, lands, shapes, 3 * len(lands), in_place=True)


def _gather_weights(halves):
    n = len(halves)

    def emit(x, y, c, srcs, outs, copy):
        chip = 2 * x + y
        sib = (x, y, 1 - c)
        first, relay, landed, relayed = [], [], [], []
        for j, m in enumerate(CHIPS):
            px, py, _ = _peer(x, y, c, m)
            theirs = 2 * px + py
            for k in range(n):
                i = 6 * k + j
                first.append(copy(srcs[k].at[c], outs[k].at[chip, c], i, (px, py, c)))
                landed.append(copy(srcs[k].at[c], outs[k].at[theirs, c], i, (px, py, c)))
                relay.append(copy(outs[k].at[theirs, c], outs[k].at[theirs, c], i + 3, sib))
                relayed.append(copy(outs[k].at[theirs, 1 - c], outs[k].at[theirs, 1 - c], i + 3, sib))
        for cp in first:
            cp.start()
        for arrival, onward in zip(landed, relay):
            arrival.wait_recv()
            onward.start()
        for arrival in relayed:
            arrival.wait_recv()
        for cp in first + relay:
            cp.wait_send()

    shapes = [jax.ShapeDtypeStruct((4,) + h.shape, h.dtype) for h in halves]
    return _comm_call("gather_weights", emit, halves, shapes, 6 * n)


def _pair_exchange(name, chunks):
    def emit(x, y, c, srcs, outs, copy):
        sib = (x, y, 1 - c)
        sends = [copy(s.at[:, 1 - c], o, k, sib) for k, (s, o) in enumerate(zip(srcs, outs))]
        for cp in sends:
            cp.start()
        for cp in sends:
            cp.wait_recv()
        for cp in sends:
            cp.wait_send()

    shapes = [jax.ShapeDtypeStruct((4,) + g.shape[2:], g.dtype) for g in chunks]
    return _comm_call(name, emit, chunks, shapes, len(chunks))


def _share_sibling(name, parts):
    def emit(x, y, c, srcs, outs, copy):
        sib = (x, y, 1 - c)
        sends = [copy(s, o.at[c], k, sib) for k, (s, o) in enumerate(zip(srcs, outs))]
        arrivals = [copy(s, o.at[1 - c], k, sib) for k, (s, o) in enumerate(zip(srcs, outs))]
        for cp in sends:
            cp.start()
        for cp in arrivals:
            cp.wait_recv()
        for cp in sends:
            cp.wait_send()

    shapes = [jax.ShapeDtypeStruct((2,) + p.shape, p.dtype) for p in parts]
    return _comm_call(name, emit, parts, shapes, len(parts))


def _reduce_pair(name, chunk, from_sibling, core):
    n, _, h, cols = chunk.shape
    rt = _div_tile(h, max(16, (1 << 20) // (4 * cols)), 16)

    def body(core_ref, a_ref, b_ref, o_ref):
        o_ref[...] = (a_ref[...] + b_ref[...]).astype(o_ref.dtype)

    grid_spec = pltpu.PrefetchScalarGridSpec(
        num_scalar_prefetch=1, grid=(n, h // rt),
        in_specs=[pl.BlockSpec((None, None, rt, cols), lambda s, i, core_ref: (s, core_ref[0], i, 0)),
                  pl.BlockSpec((None, rt, cols), lambda s, i, core_ref: (s, i, 0))],
        out_specs=pl.BlockSpec((None, rt, cols), lambda s, i, core_ref: (s, i, 0)))
    return pl.pallas_call(
        body, name=name, grid_spec=grid_spec, out_shape=jax.ShapeDtypeStruct((n, h, cols), BF16),
        compiler_params=_cparams(("parallel", "parallel")),
    )(core, chunk, from_sibling)


def _reduce_quad(name, q, after=None):
    _, h, cols = q.shape
    rt = _div_tile(h, max(16, (1 << 20) // (4 * cols)), 16)

    def body(q_ref, *rest):
        v = q_ref[...].astype(F32)
        rest[-1][...] = ((v[0] + v[1]) + v[2]) + v[3]

    held = [] if after is None else [after]
    return pl.pallas_call(
        body, name=name, grid=(h // rt,),
        in_specs=[pl.BlockSpec((4, rt, cols), lambda i: (0, i, 0))] + [pl.BlockSpec(memory_space=pl.ANY)] * len(held),
        out_specs=pl.BlockSpec((rt, cols), lambda i: (i, 0)),
        out_shape=jax.ShapeDtypeStruct((h, cols), F32),
        compiler_params=_cparams(("parallel",)),
    )(q, *held)


def _unshard(seg, kind):
    n, r, c = seg.shape
    if kind == "col":
        return seg.transpose(1, 0, 2).reshape(r, n * c)
    return seg.reshape(n * r, c)


def _pad_rows(flat, rows):
    n, ln = flat.shape
    return jnp.pad(flat, ((0, 0), (0, rows * PACK_COLS - ln))).reshape(n, rows, PACK_COLS)


def _block_diag_pairs(w):
    n2, bs, _ = w.shape
    eye = jnp.eye(2, dtype=w.dtype)
    z = w.reshape(n2 // 2, 2, bs, 1, bs) * eye[None, :, None, :, None]
    return z.reshape(n2 // 2, 2 * bs, 2 * bs).transpose(1, 0, 2).reshape(2 * bs, n2 * bs)


def _block_diag_pairs_t(d, bs=64):
    n = d.shape[1] // (2 * bs)
    z = d.reshape(2 * bs, n, 2 * bs).transpose(1, 0, 2).reshape(n, 2, bs, 2, bs)
    return jnp.stack([z[:, 0, :, 0, :], z[:, 1, :, 1, :]], axis=1).reshape(2 * n, bs, bs)


BIG = (("w_in", "col"), ("w_uq", "col"), ("w_ukv", "col"), ("w_proj_rnn", "row"), ("w_proj_mla", "row"),
       ("w_out", "row"), ("w_up", "col"), ("w_down", "row"))
FIRST_USED = ("w_in", "w_uq", "w_ukv")
CONVS = (("conv_w", "col"), ("ffn_conv_w", "col"))
SMALL = ("b_ada", "norm1_g", "conv_b", "w_gate_a", "b_gate_a", "w_gate_x", "b_gate_x", "lru_param",
         "q_norm_g", "kv_norm_g", "norm2_g", "ffn_conv_b", "final_g")
WEIGHTS = ("w_ada", "b_ada", "norm1_g", "w_in", "conv_w", "conv_b", "w_gate_a", "b_gate_a", "w_gate_x",
           "b_gate_x", "lru_param", "q_norm_g", "w_uq", "kv_norm_g", "w_ukv", "w_proj_rnn", "w_proj_mla",
           "w_out", "norm2_g", "w_up", "ffn_conv_w", "ffn_conv_b", "w_down", "final_g")


def _step(x, c, positions, w, m_in, v_in, loss_target):
    s_len, d = x.shape[1], x.shape[2]
    x2d = x[0]
    tgt = loss_target[0]
    xi, yi, ci = lax.axis_index("x"), lax.axis_index("y"), lax.axis_index("c")
    chip = 2 * xi + yi
    me = 2 * chip + ci
    tile = min(256, s_len)
    nt = s_len // tile

    local2d = {k: w[k][0] for k, _ in BIG + CONVS}
    kinds = dict(BIG)
    halves_bf = {k: local2d[k].astype(BF16).reshape(2, local2d[k].shape[0] // 2, local2d[k].shape[1]) for k, _ in BIG}
    first_names = [k for k, _ in BIG if k in FIRST_USED]
    later_names = [k for k, _ in BIG if k not in FIRST_USED]
    full = {}

    def assemble(k, g):
        g = lax.dynamic_update_index_in_dim(g, halves_bf[k][None], chip, 0).reshape((4,) + local2d[k].shape)
        if k == "w_up":
            full["w_up_gate"], full["w_up_val"] = _unshard(g[:2], kinds[k]), _unshard(g[2:], kinds[k])
        else:
            full[k] = _unshard(g, kinds[k])

    first_got = _gather_weights([halves_bf[k] for k in first_names])
    for k, g in zip(first_names, first_got):
        assemble(k, g)
    conv_flat = jnp.concatenate([local2d[k].reshape(-1) for k, _ in CONVS])
    conv_rows = -(-conv_flat.shape[0] // PACK_COLS)
    conv_all = _all_gather("gather_conv_w", _pad_rows(conv_flat[None], conv_rows)[0], CHIPS)
    conv_all = conv_all.reshape(4, -1)
    off = 0
    for k, kind in CONVS:
        r, cc = local2d[k].shape
        full[k] = _unshard(conv_all[:, off:off + r * cc].reshape(4, r, cc), kind)
        off += r * cc

    d_rnn = w["conv_b"].shape[1]
    n_q, n_kv = w["q_norm_g"].shape[1], w["kv_norm_g"].shape[1]
    w_in = full["w_in"]
    o1, o2, o3 = d_rnn + n_q, d_rnn + n_q + n_kv, d_rnn + n_q + n_kv + QK_ROPE
    w_rnn = w_in[:, :d_rnn]
    zpad = lambda n: jnp.zeros((d, n), BF16)
    w_qkv = jnp.concatenate([w_in[:, d_rnn:o2], zpad(QK_NOPE), w_in[:, o2:o3], zpad(LANE - QK_NOPE - QK_ROPE)], axis=1)
    w_g = w_in[:, o3:]
    hd = QK_NOPE + QK_ROPE
    w_uq = jnp.pad(full["w_uq"].reshape(n_q, N_HEADS, hd), ((0, 0), (0, 0), (0, HEAD_PAD - hd))).reshape(n_q, -1)
    w_ukv = full["w_ukv"]
    v_head = w_ukv.shape[1] // N_HEADS - QK_NOPE
    d_ff = w["ffn_conv_b"].shape[1] // 2
    ffn_cw_gate, ffn_cw_val = full["ffn_conv_w"][:, :d_ff], full["ffn_conv_w"][:, d_ff:]
    ffn_cb_gate, ffn_cb_val = w["ffn_conv_b"][:, :d_ff], w["ffn_conv_b"][:, d_ff:]
    conv_w, conv_b = full["conv_w"], w["conv_b"]
    wa_bd = _block_diag_pairs(w["w_gate_a"][0])
    wx_bd = _block_diag_pairs(w["w_gate_x"][0])

    c_all = _all_gather("gather_c", c, ALL7).reshape(8, d)
    c_rows = 128
    (c_act,) = _tiled("silu_c", lambda v: (_silu(v),), 1, [(jnp.pad(c_all, ((0, c_rows - 8), (0, 0))), (c_rows, d), "full")],
                      [((c_rows, d), F32, (c_rows, d), "full")])
    w_ada = w["w_ada"][0]
    n_mod = w_ada.shape[1]
    b_loc = lax.dynamic_slice_in_dim(w["b_ada"], chip * n_mod, n_mod, axis=1)
    mod_loc = _mm("ada_fwd", c_act, w_ada, add=jnp.broadcast_to(b_loc, (c_rows, n_mod)))
    mod_all = _all_gather("gather_mod", mod_loc[:8], CHIPS)
    mod = lax.dynamic_index_in_dim(mod_all, me, 1, keepdims=False).reshape(1, -1)
    shift1, scale1, gate1, shift2, scale2, gate2 = [mod[:, i * d:(i + 1) * d] for i in range(6)]

    small_done = (mod[:, :1] + conv_all[:1, :1] + first_got[0][0, 0, :1, :1].astype(F32))
    later_flight = _split_start(
        "gather_later_start", [halves_bf[k] for k in later_names],
        [jax.ShapeDtypeStruct((4,) + halves_bf[k].shape, BF16) for k in later_names], "gather", after=small_done)

    half = QK_ROPE // 2
    inv_freq = ROPE_THETA ** (-jnp.arange(half, dtype=F32) / half)
    ang = positions[0].astype(F32)[:, None] * inv_freq
    cos, sin = jnp.cos(ang), jnp.sin(ang)
    one, zero = jnp.ones((s_len, QK_NOPE), F32), jnp.zeros((s_len, half), F32)
    tail = jnp.zeros((s_len, LANE - QK_NOPE - QK_ROPE), F32)
    cos_f = jnp.concatenate([one, cos, cos, tail + 1.0], axis=1)
    sin_a = jnp.concatenate([one * 0.0, -sin, zero, tail], axis=1)
    sin_b = jnp.concatenate([one * 0.0, zero, sin, tail], axis=1)
    reset = (positions[0] == 0).astype(F32)[:, None]
    tabs = [(cos_f, (tile, LANE), "row"), (sin_a, (tile, LANE), "row"), (sin_b, (tile, LANE), "row")]

    def rowspec(a):
        return (a, (tile, a.shape[1]), "row")

    def full2(a):
        return (a, a.shape, "full")

    def rowout(cols, dt):
        return ((s_len, cols), dt, (tile, cols), "row")

    def accout(a):
        return (a.shape, F32, a.shape, "acc")

    norm1_g = w["norm1_g"] + later_flight[4][:1, :1]
    norm2_g, final_g = w["norm2_g"], w["final_g"].reshape(1, d)
    ln1_in = [rowspec(x2d), full2(norm1_g), full2(scale1), full2(shift1)]
    (h1,) = _tiled("ln1", _f_ln, nt, ln1_in, [rowout(d, BF16)])
    x_rnn = _mm("in_rnn", h1, w_rnn)
    qkv = _mm("in_qkv", h1, w_qkv)
    gates = _mm("in_gates", h1, w_g)

    ct = LANE
    n_ct = d_rnn // ct
    colspec = lambda a, width=ct: (a, (a.shape[0], width), "col")
    lru_in = [colspec(x_rnn), colspec(conv_w), colspec(conv_b), colspec(wa_bd), colspec(w["b_gate_a"]),
              colspec(wx_bd), colspec(w["b_gate_x"]), colspec(w["lru_param"]), full2(reset)]
    y_rnn, h_rnn = _tiled("lru_fwd", _f_lru_fwd, n_ct, lru_in,
                          [((s_len, d_rnn), BF16, (s_len, ct), "col"), ((s_len, d_rnn), F32, (s_len, ct), "col")])

    qkv_in = [rowspec(qkv)] + tabs + [full2(w["q_norm_g"]), full2(w["kv_norm_g"])]
    qn, kvn, kr = _tiled("qkv_norm", _f_qkv, nt, qkv_in, [rowout(n_q, BF16), rowout(n_kv, BF16), rowout(LANE, BF16)])
    q_pre = _mm("up_q", qn, w_uq)
    kv = _mm("up_kv", kvn, w_ukv, out_dtype=BF16)
    (q_cat,) = _tiled("rot_q", _f_rotq, nt, [rowspec(q_pre)] + tabs, [rowout(q_pre.shape[1], BF16)])
    o_mla, lse = _attn_fwd(q_cat, kv, kr)

    send_sems, recv_sems, flown, landed, _ = later_flight
    landed = _split_wait("gather_later_wait", send_sems, recv_sems, flown, landed, "gather", after=o_mla)
    for k, g in zip(later_names, _relay_sibling(landed)):
        assemble(k, g)
    w_pr = full["w_proj_rnn"]
    w_pm = jnp.pad(full["w_proj_mla"].reshape(N_HEADS, v_head, d), ((0, 0), (HEAD_PAD - v_head, 0), (0, 0))).reshape(-1, d)
    w_out = full["w_out"]
    w_up_gate, w_up_val = full["w_up_gate"], full["w_up_val"]
    w_down = full["w_down"]

    p_rnn = _mm("proj_rnn", y_rnn, w_pr)
    p_mla = _mm("proj_mla", o_mla, w_pm)
    merge_in = [rowspec(gates), rowspec(p_rnn), rowspec(p_mla)]
    (merged,) = _tiled("merge", _f_merge, nt, merge_in, [rowout(d, BF16)])
    o_tok = _mm("out_proj", merged, w_out)
    res_in = [rowspec(x2d), rowspec(o_tok), full2(gate1), full2(norm2_g), full2(scale2), full2(shift2)]
    x1, h2 = _tiled("res_ln2", _f_res_ln, nt, res_in, [rowout(d, F32), rowout(d, BF16)])
    u_gate = _mm("ffn_up_gate", h2, w_up_gate)
    u_val = _mm("ffn_up_val", h2, w_up_val)
    n_ft = d_ff // LANE
    ffn_in = [colspec(a) for a in (u_gate, u_val, ffn_cw_gate, ffn_cw_val, ffn_cb_gate, ffn_cb_val)]
    (act,) = _tiled("ffn_conv", _f_ffn, n_ft, ffn_in, [((s_len, d_ff), BF16, (s_len, LANE), "col")])
    f_tok = _mm("ffn_down", act, w_down)

    loss_in = [rowspec(x1), rowspec(f_tok), rowspec(tgt), full2(gate2), full2(final_g)]
    dx1, df, loss_row, d_gate2, d_final_g = _tiled(
        "loss", _f_loss_and_grads, nt, loss_in,
        [rowout(d, F32), rowout(d, BF16), ((1, LANE), F32, (1, LANE), "acc"), accout(gate2), accout(final_g)])
    loss = lax.psum(loss_row[0, 0], ("x", "y", "c"))

    d_act = _mm("ffn_down_dx", df, w_down, tb=True)
    g_w_down = _mm("ffn_down_dw", act, df, ta=True)
    taps = ffn_cw_gate.shape[0]
    du_gate, du_val, g_cw_gate, g_cw_val, g_cb_gate, g_cb_val = _tiled(
        "ffn_conv_bwd", _vjp_of(_f_ffn, 6, (0, 1, 2, 3, 4, 5)), n_ft, ffn_in + [colspec(d_act)],
        [((s_len, d_ff), BF16, (s_len, LANE), "col")] * 2 + [((taps, d_ff), F32, (taps, LANE), "col")] * 2
        + [((1, d_ff), F32, (1, LANE), "col")] * 2)
    dh2 = _mm("ffn_up_gate_dx", du_gate, w_up_gate, tb=True)
    dh2 = _mm("ffn_up_val_dx", du_val, w_up_val, tb=True, add=dh2)
    g_w_up = jnp.concatenate([_mm("ffn_up_gate_dw", h2, du_gate, ta=True), _mm("ffn_up_val_dw", h2, du_val, ta=True)], axis=1)
    g_ffn_cw = jnp.concatenate([g_cw_gate, g_cw_val], axis=1)
    g_ffn_cb = jnp.concatenate([g_cb_gate, g_cb_val], axis=1)

    res_bwd = _vjp_of(_f_res_ln, 6, (0, 1, 2, 3, 4, 5))
    dx_res, do_tok, d_gate1, g_norm2, d_scale2, d_shift2 = _tiled(
        "res_ln2_bwd", res_bwd, nt, res_in + [rowspec(dx1), rowspec(dh2)],
        [rowout(d, F32), rowout(d, BF16), accout(gate1), accout(norm2_g), accout(scale2), accout(shift2)])
    d_merged = _mm("out_proj_dx", do_tok, w_out, tb=True)
    g_w_out = _mm("out_proj_dw", merged, do_tok, ta=True)
    d_gates, dp_rnn, dp_mla = _tiled(
        "merge_bwd", _f_merge_bwd, nt, merge_in + [rowspec(d_merged)],
        [rowout(gates.shape[1], BF16), rowout(d, BF16), rowout(d, BF16)])
    dy_rnn = _mm("proj_rnn_dx", dp_rnn, w_pr, tb=True)
    g_w_pr = _mm("proj_rnn_dw", y_rnn, dp_rnn, ta=True)
    do_mla = _mm("proj_mla_dx", dp_mla, w_pm, tb=True, out_dtype=BF16)
    g_w_pm = _mm("proj_mla_dw", o_mla, dp_mla, ta=True)

    def chunked(k, gk):
        r, cc = local2d[k].shape
        if kinds[k] == "col":
            gk = gk.reshape(r, 4, cc).transpose(1, 0, 2)
        return gk.reshape(4, 2, r // 2, cc)

    def pair_sums(tag, names, chunks):
        out = []
        core = ci.astype(jnp.int32).reshape(1)
        for k, ck, from_sib in zip(names, chunks, _pair_exchange("reduce_pair_exchange_" + tag, chunks)):
            out.append(_reduce_pair("reduce_pair_" + k, ck, from_sib, core))
        return out

    g_later = {
        "w_proj_rnn": g_w_pr,
        "w_proj_mla": g_w_pm.reshape(N_HEADS, HEAD_PAD, d)[:, HEAD_PAD - v_head:, :].reshape(-1, d),
        "w_out": g_w_out, "w_up": g_w_up, "w_down": g_w_down,
    }
    sums_ready = pair_sums("ready", later_names, [chunked(k, g_later[k]) for k in later_names])
    ready_flight = _split_start(
        "reduce_ready_start", sums_ready, [jax.ShapeDtypeStruct(s.shape, s.dtype) for s in sums_ready], "alltoall",
        after=sums_ready[0])
    kr_held = kr + ready_flight[4][:1, :].astype(BF16)

    dq_cat, dkv, dkr = _attn_bwd(q_cat, kv, kr_held, o_mla, lse, do_mla)
    (dq_pre,) = _tiled("rot_q_bwd", _f_rotq_bwd, nt, [rowspec(q_pre)] + tabs + [rowspec(dq_cat)],
                       [rowout(q_pre.shape[1], BF16)])
    dqn = _mm("up_q_dx", dq_pre, w_uq, tb=True)
    g_w_uq = _mm("up_q_dw", qn, dq_pre, ta=True)
    dkv_b = dkv.astype(BF16)
    dkvn = _mm("up_kv_dx", dkv_b, w_ukv, tb=True)
    g_w_ukv = _mm("up_kv_dw", kvn, dkv_b, ta=True)
    dqkv, g_q_norm, g_kv_norm = _tiled(
        "qkv_norm_bwd", _f_qkv_bwd, nt, qkv_in + [rowspec(dqn), rowspec(dkvn), rowspec(dkr)],
        [rowout(qkv.shape[1], BF16), accout(w["q_norm_g"]), accout(w["kv_norm_g"])])

    lru_out = [((s_len, d_rnn), BF16, (s_len, ct), "col")]
    for a in (conv_w, conv_b, wa_bd, w["b_gate_a"], wx_bd, w["b_gate_x"], w["lru_param"]):
        lru_out.append((a.shape, F32, (a.shape[0], ct), "col"))
    dx_rnn, g_conv_w, g_conv_b, g_wa_bd, g_b_a, g_wx_bd, g_b_x, g_lru = _tiled(
        "lru_bwd", _f_lru_bwd, n_ct, lru_in + [colspec(h_rnn), colspec(dy_rnn)], lru_out)

    dh1 = _mm("in_gates_dx", d_gates, w_g, tb=True)
    dh1 = _mm("in_qkv_dx", dqkv, w_qkv, tb=True, add=dh1)
    dh1 = _mm("in_rnn_dx", dx_rnn, w_rnn, tb=True, add=dh1)
    g_w_rnn = _mm("in_rnn_dw", h1, dx_rnn, ta=True)
    g_w_qkv = _mm("in_qkv_dw", h1, dqkv, ta=True)
    g_w_g = _mm("in_gates_dw", h1, d_gates, ta=True)

    ln_bwd = _vjp_of(_f_ln, 4, (0, 1, 2, 3))

    def ln1_bwd(xv, gv, sc, sh, dxr, dh):
        dx, dg, dsc, dsh = ln_bwd(xv, gv, sc, sh, dh)
        return dx + dxr, dg, dsc, dsh

    grad_x, g_norm1, d_scale1, d_shift1 = _tiled(
        "ln1_bwd", ln1_bwd, nt, ln1_in + [rowspec(dx_res), rowspec(dh1)],
        [rowout(d, F32), accout(norm1_g), accout(scale1), accout(shift1)])

    dmod = jnp.concatenate([d_shift1, d_scale1, d_gate1, d_shift2, d_scale2, d_gate2], axis=1)
    dmod_all = _all_gather("gather_dmod", dmod, ALL7).reshape(8, -1)
    dmod_loc = lax.dynamic_slice_in_dim(dmod_all, chip * n_mod, n_mod, axis=1)
    g_w_ada = _mm("ada_dw", c_act, jnp.pad(dmod_loc, ((0, c_rows - 8), (0, 0))), ta=True)

    g_full = {
        "w_in": jnp.concatenate([g_w_rnn, g_w_qkv[:, :n_q + n_kv],
                                 g_w_qkv[:, n_q + n_kv + QK_NOPE:n_q + n_kv + QK_NOPE + QK_ROPE], g_w_g], axis=1),
        "w_uq": g_w_uq.reshape(n_q, N_HEADS, HEAD_PAD)[:, :, :hd].reshape(n_q, -1),
        "w_ukv": g_w_ukv,
        "conv_w": g_conv_w,
        "ffn_conv_w": g_ffn_cw,
    }
    g_small = {
        "b_ada": dmod, "norm1_g": g_norm1, "conv_b": g_conv_b,
        "w_gate_a": _block_diag_pairs_t(g_wa_bd)[None], "b_gate_a": g_b_a,
        "w_gate_x": _block_diag_pairs_t(g_wx_bd)[None], "b_gate_x": g_b_x, "lru_param": g_lru,
        "q_norm_g": g_q_norm, "kv_norm_g": g_kv_norm, "norm2_g": g_norm2,
        "ffn_conv_b": g_ffn_cb, "final_g": d_final_g.reshape(w["final_g"].shape),
    }

    small_flat = jnp.concatenate([g_small[k].reshape(-1) for k in SMALL] + [g_full[k].reshape(-1) for k, _ in CONVS])
    small_rows = -(-small_flat.shape[0] // (8 * PACK_COLS * PACK_ROW_UNIT)) * PACK_ROW_UNIT
    last_names = first_names + ["small"]
    last_chunks = [chunked(k, g_full[k]) for k in first_names]
    last_chunks.append(_pad_rows(small_flat[None], 8 * small_rows).reshape(4, 2, small_rows, PACK_COLS))
    sums_last = pair_sums("last", last_names, last_chunks)
    send_sems, recv_sems, flown, landed, _ = ready_flight
    quads_ready = _split_wait("reduce_ready_wait", send_sems, recv_sems, flown, landed, "alltoall", after=grad_x)
    last_flight = _split_start(
        "reduce_last_start", sums_last, [jax.ShapeDtypeStruct(s.shape, s.dtype) for s in sums_last], "alltoall",
        after=quads_ready[0])
    grads = {"w_ada": g_w_ada[None]}
    delta, new_m, new_v = {}, {}, {}

    def adamw(k):
        shp = w[k].shape
        flip = len(shp) == 3 and shp[-1] % LANE != 0 and shp[-2] % LANE == 0
        view = (lambda a: jnp.swapaxes(a, 1, 2)) if flip else (lambda a: a)
        two_d = (-1, view(w[k]).shape[-1]) if len(shp) > 1 else (1, -1)
        dk, mk, vk = _adamw("adamw_" + k, *[view(a).reshape(two_d) for a in (w[k], grads[k], m_in[k], v_in[k])])
        back = lambda a: view(a.reshape(view(w[k]).shape))
        delta[k], new_m[k], new_v[k] = back(dk), back(mk), back(vk)

    def finish(tag, names, quads, sums, after):
        reduced = {}
        for k, quad, ps in zip(names, quads, sums):
            quad = lax.dynamic_update_index_in_dim(quad, lax.dynamic_index_in_dim(ps, chip, 0, keepdims=True), chip, 0)
            reduced[k] = _reduce_quad("reduce_quad_" + k, quad, after)
        big = [k for k in names if k != "small"]
        for k, both in zip(big, _share_sibling("share_sibling_" + tag, [reduced[k] for k in big])):
            grads[k] = lax.dynamic_update_index_in_dim(both, reduced[k][None], ci, 0).reshape(w[k].shape)
        return reduced

    finish("ready", later_names, quads_ready, sums_ready, after=last_flight[4])
    for k in later_names + ["w_ada"]:
        adamw(k)
    send_sems, recv_sems, flown, landed, _ = last_flight
    quads_last = _split_wait("reduce_last_wait", send_sems, recv_sems, flown, landed, "alltoall",
                             after=delta[later_names[-1]])
    reduced = finish("last", last_names, quads_last, sums_last, after=None)
    small_grad = _all_gather("share_small", reduced["small"], ALL7).reshape(-1)
    off = 0
    for k in SMALL:
        grads[k] = small_grad[off:off + w[k].size].reshape(w[k].shape)
        off += w[k].size
    for k, _ in CONVS:
        r, cc = local2d[k].shape
        whole = small_grad[off:off + 4 * r * cc].reshape(r, 4 * cc)
        grads[k] = lax.dynamic_slice_in_dim(whole, chip * cc, cc, axis=1)[None]
        off += 4 * r * cc
    for k in WEIGHTS:
        if k not in delta:
            adamw(k)

    return (loss, grad_x[None], *[grads[k] for k in WEIGHTS], *[delta[k] for k in WEIGHTS],
            *[new_m[k] for k in WEIGHTS], *[new_v[k] for k in WEIGHTS])


def kernel(x, c, positions, w_ada, b_ada, norm1_g, w_in, conv_w, conv_b, w_gate_a, b_gate_a, w_gate_x, b_gate_x, lru_param, q_norm_g, w_uq, kv_norm_g, w_ukv, w_proj_rnn, w_proj_mla, w_out, norm2_g, w_up, ffn_conv_w, ffn_conv_b, w_down, final_g, loss_target, m_w_ada, m_b_ada, m_norm1_g, m_w_in, m_conv_w, m_conv_b, m_w_gate_a, m_b_gate_a, m_w_gate_x, m_b_gate_x, m_lru_param, m_q_norm_g, m_w_uq, m_kv_norm_g, m_w_ukv, m_w_proj_rnn, m_w_proj_mla, m_w_out, m_norm2_g, m_w_up, m_ffn_conv_w, m_ffn_conv_b, m_w_down, m_final_g, v_w_ada, v_b_ada, v_norm1_g, v_w_in, v_conv_w, v_conv_b, v_w_gate_a, v_b_gate_a, v_w_gate_x, v_b_gate_x, v_lru_param, v_q_norm_g, v_w_uq, v_kv_norm_g, v_w_ukv, v_w_proj_rnn, v_w_proj_mla, v_w_out, v_norm2_g, v_w_up, v_ffn_conv_w, v_ffn_conv_b, v_w_down, v_final_g):
    given = dict(locals())
    w = {k: given[k] for k in WEIGHTS}
    m_in = {k: given["m_" + k] for k in WEIGHTS}
    v_in = {k: given["v_" + k] for k in WEIGHTS}
    return _step(x, c, positions, w, m_in, v_in, loss_target)
```

```python
import functools
import math

import jax
import jax.numpy as jnp
from jax import lax
from jax.experimental import pallas as pl
from jax.experimental.pallas import tpu as pltpu

F32 = jnp.float32
BF16 = jnp.bfloat16

EPS = 1e-6
LRU_C = 8.0
N_HEADS = 16
QK_NOPE = 64
QK_ROPE = 32
HEAD_PAD = 128
ROPE_THETA = 10000.0
ADAM_LR = 0.001
ADAM_B1 = 0.9
ADAM_B2 = 0.999
ADAM_EPS = 1e-08
ADAM_WD = 0.01
ADAM_STEP = 10

LANE = 128
SUBLANES = 8
VMEM_LIMIT = 48 * 1024 * 1024
MM_TILE_M = MM_TILE_N = MM_TILE_K = 1408
PACK_COLS = 1024
PACK_ROW_UNIT = 32
MESH = pl.DeviceIdType.MESH

NN = (((1,), (0,)), ((), ()))
NT = (((1,), (1,)), ((), ()))
TN = (((0,), (0,)), ((), ()))


def _cparams(sem):
    return pltpu.CompilerParams(dimension_semantics=sem, vmem_limit_bytes=VMEM_LIMIT)


def _div_tile(n, cap, unit):
    best = None
    d = unit
    while d <= min(n, cap):
        if n % d == 0:
            best = d
        d += unit
    return n if best is None else best


def _mm(name, a, b, *, ta=False, tb=False, add=None, out_dtype=F32):
    if ta:
        kdim, m = a.shape
    else:
        m, kdim = a.shape
    if tb:
        n, kb = b.shape
    else:
        kb, n = b.shape
    assert kdim == kb, (name, a.shape, b.shape)
    tm = _div_tile(m, MM_TILE_M, 8 if not ta else LANE)
    tn = _div_tile(n, MM_TILE_N, LANE)
    tk = _div_tile(kdim, MM_TILE_K, LANE)
    nk = kdim // tk
    a_spec = pl.BlockSpec((tk, tm), lambda i, j, k: (k, i)) if ta else pl.BlockSpec((tm, tk), lambda i, j, k: (i, k))
    b_spec = pl.BlockSpec((tn, tk), lambda i, j, k: (j, k)) if tb else pl.BlockSpec((tk, tn), lambda i, j, k: (k, j))
    o_spec = pl.BlockSpec((tm, tn), lambda i, j, k: (i, j))
    has_add = add is not None
    dims = ((((0,) if ta else (1,)), ((1,) if tb else (0,))), ((), ()))

    def body(*refs):
        a_ref, b_ref = refs[0], refs[1]
        c_ref = refs[2] if has_add else None
        o_ref = refs[3] if has_add else refs[2]
        prod = lax.dot_general(a_ref[...].astype(BF16), b_ref[...].astype(BF16), dims, preferred_element_type=F32)
        if nk == 1:
            o_ref[...] = (prod + c_ref[...].astype(F32) if has_add else prod).astype(o_ref.dtype)
            return
        acc = refs[-1]
        k = pl.program_id(2)

        @pl.when(k == 0)
        def _():
            acc[...] = prod + c_ref[...].astype(F32) if has_add else prod

        @pl.when(jnp.logical_and(k > 0, k < nk - 1))
        def _():
            acc[...] += prod

        @pl.when(k == nk - 1)
        def _():
            o_ref[...] = (acc[...] + prod).astype(o_ref.dtype)

    ins = [a, b] + ([add] if has_add else [])
    specs = [a_spec, b_spec] + ([o_spec] if has_add else [])
    return pl.pallas_call(
        body, name=name, grid=(m // tm, n // tn, nk), in_specs=specs, out_specs=o_spec,
        out_shape=jax.ShapeDtypeStruct((m, n), out_dtype),
        scratch_shapes=[pltpu.VMEM((tm, tn), F32)] if nk > 1 else [],
        compiler_params=_cparams(("parallel", "parallel", "arbitrary")),
    )(*ins)


_IMAPS = {
    "row": lambda i: (i, 0),
    "col": lambda i: (0, i),
    "full": lambda i: (0, 0),
    "acc": lambda i: (0, 0),
}


def _tiled(name, fn, n, ins, outs):
    ni = len(ins)
    is_acc = [k == "acc" for *_, k in outs]

    def body(*refs):
        vals = fn(*[r[...] for r in refs[:ni]])
        orefs = refs[ni:]
        if any(is_acc):
            @pl.when(pl.program_id(0) == 0)
            def _():
                for r, a in zip(orefs, is_acc):
                    if a:
                        r[...] = jnp.zeros(r.shape, r.dtype)
        for r, v, a in zip(orefs, vals, is_acc):
            if a:
                r[...] += v.astype(r.dtype)
            else:
                r[...] = v.astype(r.dtype)

    res = pl.pallas_call(
        body, name=name, grid=(n,),
        in_specs=[pl.BlockSpec(bs, _IMAPS[k]) for _, bs, k in ins],
        out_specs=[pl.BlockSpec(bs, _IMAPS[k]) for _, _, bs, k in outs],
        out_shape=[jax.ShapeDtypeStruct(s, d) for s, d, _, _ in outs],
        compiler_params=_cparams(("arbitrary",)),
    )(*[a for a, _, _ in ins])
    return tuple(res)


def _vjp_of(fn, nin, diff):
    def g(*args):
        ins, cots = args[:nin], args[nin:]

        def f(*d):
            full = list(ins)
            for i, v in zip(diff, d):
                full[i] = v
            return fn(*full)

        outs, vjp = jax.vjp(f, *[ins[i] for i in diff])
        return vjp(tuple(c.astype(o.dtype) for c, o in zip(cots, outs)))
    return g


def _shift_rows(x, k, fill, up=False):
    n = x.shape[0]
    if k % SUBLANES == 0:
        pad = jnp.full((k,) + x.shape[1:], fill, x.dtype)
        return jnp.concatenate([x[k:], pad], axis=0) if up else jnp.concatenate([pad, x[:n - k]], axis=0)
    rows = lax.broadcasted_iota(jnp.int32, x.shape, 0)
    if up:
        return jnp.where(rows < n - k, pltpu.roll(x, n - k, 0), fill)
    return jnp.where(rows >= k, pltpu.roll(x, k, 0), fill)


@functools.partial(jax.custom_vjp, nondiff_argnums=(1,))
def _delay(x, k):
    return _shift_rows(x, k, 0.0)


def _delay_fwd(x, k):
    return _shift_rows(x, k, 0.0), None


def _delay_bwd(k, _, g):
    return (_shift_rows(g, k, 0.0, up=True),)


_delay.defvjp(_delay_fwd, _delay_bwd)


@functools.partial(jax.custom_vjp, nondiff_argnums=(1,))
def _lane_roll(x, s):
    return pltpu.roll(x, s, 1)


def _lane_roll_fwd(x, s):
    return pltpu.roll(x, s, 1), None


def _lane_roll_bwd(s, _, g):
    return (pltpu.roll(g, g.shape[1] - s, 1),)


_lane_roll.defvjp(_lane_roll_fwd, _lane_roll_bwd)


@jax.custom_vjp
def _bdot(x, w):
    return lax.dot_general(x.astype(BF16), w.astype(BF16), NN, preferred_element_type=F32)


def _bdot_fwd(x, w):
    return _bdot(x, w), (x, w)


def _bdot_bwd(res, g):
    x, w = res
    gb = g.astype(BF16)
    dx = lax.dot_general(gb, w.astype(BF16), NT, preferred_element_type=F32)
    dw = lax.dot_general(x.T.astype(BF16), gb, NN, preferred_element_type=F32)
    return dx, dw


_bdot.defvjp(_bdot_fwd, _bdot_bwd)


def _sigmoid(x):
    return 0.5 * (jnp.tanh(0.5 * x) + 1.0)


def _silu(x):
    return x * _sigmoid(x)


def _rms(x, g):
    return x * lax.rsqrt(jnp.mean(x * x, axis=-1, keepdims=True) + EPS) * g


def _causal_conv(x, w, b):
    kw = w.shape[0]
    tap = lax.broadcasted_iota(jnp.int32, w.shape, 0)
    y = b
    for k in range(kw):
        d = kw - 1 - k
        wk = jnp.sum(jnp.where(tap == k, w, 0.0), axis=0, keepdims=True)
        y = y + wk * (x if d == 0 else _delay(x, d))
    return y


def _rotate(x, cos_f, sin_a, sin_b):
    reps = x.shape[1] // LANE
    if reps > 1:
        cos_f, sin_a, sin_b = (jnp.tile(t, (1, reps)) for t in (cos_f, sin_a, sin_b))
    n = x.shape[1]
    half = QK_ROPE // 2
    return x * cos_f + _lane_roll(x, n - half) * sin_a + _lane_roll(x, half) * sin_b


def _softplus_neg(l):
    u = jnp.exp(-jnp.abs(l))
    log1p_u = jnp.where(u < 0.01, u * (1.0 - u * (0.5 - u * (1.0 / 3.0))), jnp.log(1.0 + u))
    return jnp.maximum(-l, 0.0) + log1p_u


def _f_ln(x, g, scale, shift):
    return (_rms(x, g) * (1.0 + scale) + shift,)


def _f_qkv(qkv, cos_f, sin_a, sin_b, qg, kvg):
    nq, nkv = qg.shape[1], kvg.shape[1]
    qn = _rms(qkv[:, :nq], qg)
    kvn = _rms(qkv[:, nq:nq + nkv], kvg)
    kr = _rotate(qkv[:, nq + nkv:], cos_f, sin_a, sin_b)
    return qn, kvn, kr


def _f_qkv_bwd(qkv, cos_f, sin_a, sin_b, qg, kvg, dqn, dkvn, dkr):
    nq, nkv = qg.shape[1], kvg.shape[1]
    _, vjp_q = jax.vjp(_rms, qkv[:, :nq], qg)
    _, vjp_kv = jax.vjp(_rms, qkv[:, nq:nq + nkv], kvg)
    _, vjp_r = jax.vjp(lambda t: _rotate(t, cos_f, sin_a, sin_b), qkv[:, nq + nkv:])
    dq_lat, dqg = vjp_q(dqn)
    dkv_lat, dkvg = vjp_kv(dkvn)
    (dkr_pre,) = vjp_r(dkr)
    return jnp.concatenate([dq_lat, dkv_lat, dkr_pre], axis=1), dqg, dkvg


QK_SCALE = 1.0 / math.sqrt(QK_NOPE + QK_ROPE)
LOG2_E = 1.4426950408889634
LN_2 = 0.6931471805599453


def _f_rotq(q, cos_f, sin_a, sin_b):
    return (_rotate(q, cos_f, sin_a, sin_b) * (QK_SCALE * LOG2_E),)


def _f_rotq_bwd(q, cos_f, sin_a, sin_b, dq):
    _, vjp = jax.vjp(lambda t: _rotate(t, cos_f, sin_a, sin_b) * QK_SCALE, q)
    return vjp(dq)


def _merge(g_rnn, g_mla, p_rnn, p_mla):
    return _sigmoid(g_rnn) * p_rnn + _sigmoid(g_mla) * p_mla


def _f_merge(g, p_rnn, p_mla):
    d = p_rnn.shape[1]
    return (_merge(g[:, :d], g[:, d:], p_rnn, p_mla),)


def _f_merge_bwd(g, p_rnn, p_mla, dm):
    d = p_rnn.shape[1]
    _, vjp = jax.vjp(_merge, g[:, :d], g[:, d:], p_rnn, p_mla)
    dg_rnn, dg_mla, dp_rnn, dp_mla = vjp(dm)
    return jnp.concatenate([dg_rnn, dg_mla], axis=1), dp_rnn, dp_mla


def _f_res_ln(x, o, gate, g2, scale, shift):
    x1 = x + gate * o
    return x1, _rms(x1, g2) * (1.0 + scale) + shift


def _f_ffn(u_gate, u_val, cw_gate, cw_val, cb_gate, cb_val):
    return (_silu(_causal_conv(u_gate, cw_gate, cb_gate)) * _causal_conv(u_val, cw_val, cb_val),)


def _f_loss(x1, f, tgt, gate, fg):
    y = _rms(x1 + gate * f, fg)
    err = (y - tgt) * (y - tgt)
    return 0.5 * jnp.sum(jnp.mean(err, axis=-1, keepdims=True), axis=0, keepdims=True)


def _f_loss_and_grads(x1, f, tgt, gate, fg):
    loss, vjp = jax.vjp(lambda a, b, c, d: _f_loss(a, b, tgt, c, d), x1, f, gate, fg)
    dx1, df, dgate, dfg = vjp(jnp.ones((1, 1), F32))
    return dx1, df, jnp.broadcast_to(loss, (1, LANE)), dgate, dfg


@jax.custom_vjp
def _decay_and_gain(log_a):
    a = jnp.exp(log_a)
    return a, jnp.sqrt(-jnp.tanh(log_a) * (1.0 + a * a))


def _decay_and_gain_fwd(log_a):
    a, gain = _decay_and_gain(log_a)
    return (a, gain), (a, gain)


def _decay_and_gain_bwd(res, g):
    a, gain = res
    return (g[0] * a - g[1] * (a * a) / gain,)


_decay_and_gain.defvjp(_decay_and_gain_fwd, _decay_and_gain_bwd)


def _f_lru_coeffs(xr, cw, cb, wa, ba, wx, bx, lru, reset):
    xc = _causal_conv(xr, cw, cb)
    r = _sigmoid(_bdot(xc, wa) + ba)
    i = _sigmoid(_bdot(xc, wx) + bx)
    log_a = (-LRU_C) * r * _softplus_neg(lru)
    a, mult = _decay_and_gain(log_a)
    is_reset = reset > 0.5
    a = jnp.where(is_reset, 0.0, a)
    mult = jnp.where(is_reset, 1.0, mult)
    return a, mult * (i * xc)


SCAN_BLOCK = 64


def _scan(a, b, up=False):
    n = a.shape[0]
    blk = min(SCAN_BLOCK, n)
    pos = lax.broadcasted_iota(jnp.int32, a.shape, 0) % blk
    k = 1
    while k < blk:
        inside = (pos < blk - k) if up else (pos >= k)
        shift = n - k if up else k
        b = b + a * jnp.where(inside, pltpu.roll(b, shift, 0), 0.0)
        a = a * jnp.where(inside, pltpu.roll(a, shift, 0), 1.0)
        k *= 2
    blocks = range(n // blk)
    carry = jnp.zeros((1,) + a.shape[1:], a.dtype)
    out = [None] * len(blocks)
    for i in (reversed(blocks) if up else blocks):
        rows = slice(i * blk, (i + 1) * blk)
        out[i] = b[rows] + a[rows] * carry
        carry = out[i][:1] if up else out[i][blk - 1:]
    return jnp.concatenate(out, axis=0)


def _f_lru_fwd(xr, cw, cb, wa, ba, wx, bx, lru, reset):
    a, b = _f_lru_coeffs(xr, cw, cb, wa, ba, wx, bx, lru, reset)
    h = _scan(a, b)
    return h, h


def _f_lru_bwd(xr, cw, cb, wa, ba, wx, bx, lru, reset, h, dh):
    (a, _), vjp = jax.vjp(lambda *p: _f_lru_coeffs(*p, reset), xr, cw, cb, wa, ba, wx, bx, lru)
    g = _scan(_shift_rows(a, 1, 0.0, up=True), dh, up=True)
    return vjp((g * _shift_rows(h, 1, 0.0), g))


def _attn_tile(s):
    return 1024 if s >= 2048 else s // 2


def _keys(kv, kr):
    lane = lax.broadcasted_iota(jnp.int32, kv.shape, 1)
    return jnp.where(lane < QK_NOPE, kv, kr)


ATTN_HEADS_PER_STEP = 2


def _scores(q, kc, diagonal):
    s = lax.dot_general(q, kc, NT, preferred_element_type=F32)
    if not diagonal:
        return s
    rows = lax.broadcasted_iota(jnp.int32, s.shape, 0)
    cols = lax.broadcasted_iota(jnp.int32, s.shape, 1)
    return jnp.where(cols - (s.shape[1] - s.shape[0]) <= rows, s, -jnp.inf)


def _sub_blocks(t, diagonal):
    return ((0, t // 2, t // 2), (t // 2, t // 2, t)) if diagonal else ((0, t, t),)


def _causal_pairs(nb, k_major):
    if k_major:
        pairs = [(qb, kb) for kb in range(nb) for qb in range(kb, nb)]
    else:
        pairs = [(qb, kb) for qb in range(nb) for kb in range(qb + 1)]
    return jnp.array([p[0] for p in pairs], jnp.int32), jnp.array([p[1] for p in pairs], jnp.int32)


def _attn_fwd(q, kv, kr):
    s_len = q.shape[0]
    t = _attn_tile(s_len)
    nb = s_len // t
    hp = ATTN_HEADS_PER_STEP
    wide = hp * HEAD_PAD
    q_tab, k_tab = _causal_pairs(nb, k_major=False)

    def body(qt, kt, q_ref, kv_ref, kr_ref, o_ref, lse_ref, m_s, acc_s):
        pair = pl.program_id(1)
        qi, ki = qt[pair], kt[pair]

        @pl.when(ki == 0)
        def _():
            m_s[...] = jnp.full(m_s.shape, -jnp.inf, F32)
            acc_s[...] = jnp.zeros(acc_s.shape, F32)

        def step(diagonal):
            for h in range(hp):
                lanes = slice(h * HEAD_PAD, (h + 1) * HEAD_PAD)
                for r0, nr, nk in _sub_blocks(t, diagonal):
                    rows = slice(r0, r0 + nr)
                    kvv = kv_ref[:nk, lanes]
                    s = _scores(q_ref[rows, lanes], _keys(kvv, kr_ref[:nk, :]), diagonal)
                    m_old = m_s[h, rows]
                    m_new = jnp.maximum(m_old, jnp.max(s, axis=-1, keepdims=True))
                    alpha = jnp.exp2(m_old - m_new)
                    p = jnp.exp2(s - m_new)
                    lane = lax.broadcasted_iota(jnp.int32, kvv.shape, 1)
                    ones_and_values = jnp.where(lane < QK_NOPE, jnp.ones_like(kvv), kvv)
                    acc_s[rows, lanes] = alpha * acc_s[rows, lanes] + lax.dot_general(
                        p.astype(BF16), ones_and_values, NN, preferred_element_type=F32)
                    m_s[h, rows] = m_new

        @pl.when(ki < qi)
        def _():
            step(False)

        @pl.when(ki == qi)
        def _():
            step(True)
            lane = lax.broadcasted_iota(jnp.int32, (t, HEAD_PAD), 1)
            for h in range(hp):
                lanes = slice(h * HEAD_PAD, (h + 1) * HEAD_PAD)
                acc = acc_s[:, lanes]
                total = acc[:, :1]
                o_ref[:, lanes] = jnp.where(lane >= QK_NOPE, acc / total, 0.0).astype(o_ref.dtype)
                lse_ref[h] = m_s[h] + jnp.log(total) * LOG2_E

    grid_spec = pltpu.PrefetchScalarGridSpec(
        num_scalar_prefetch=2, grid=(N_HEADS // hp, q_tab.shape[0]),
        in_specs=[pl.BlockSpec((t, wide), lambda h, p, qt, kt: (qt[p], h)),
                  pl.BlockSpec((t, wide), lambda h, p, qt, kt: (kt[p], h)),
                  pl.BlockSpec((t, HEAD_PAD), lambda h, p, qt, kt: (kt[p], 0))],
        out_specs=[pl.BlockSpec((t, wide), lambda h, p, qt, kt: (qt[p], h)),
                   pl.BlockSpec((hp, t, 1), lambda h, p, qt, kt: (h, qt[p], 0))],
        scratch_shapes=[pltpu.VMEM((hp, t, 1), F32), pltpu.VMEM((t, wide), F32)])
    return pl.pallas_call(
        body, name="attn_fwd", grid_spec=grid_spec,
        out_shape=[jax.ShapeDtypeStruct((s_len, N_HEADS * HEAD_PAD), BF16),
                   jax.ShapeDtypeStruct((N_HEADS, s_len, 1), F32)],
        compiler_params=_cparams(("arbitrary", "arbitrary")),
    )(q_tab, k_tab, q, kv, kr)


def _attn_bwd(q, kv, kr, o, lse, do):
    s_len = q.shape[0]
    t = _attn_tile(s_len)
    nb = s_len // t
    hp = ATTN_HEADS_PER_STEP
    wide = hp * HEAD_PAD
    q_tab, k_tab = _causal_pairs(nb, k_major=True)

    def body(qt, kt, q_ref, kv_ref, kr_ref, o_ref, lse_ref, do_ref, dq_ref, dkv_ref, dkr_ref, dk_s, dv_s):
        g, pair = pl.program_id(0), pl.program_id(1)
        qb, kb = qt[pair], kt[pair]

        @pl.when(jnp.logical_and(g == 0, pair == 0))
        def _():
            dkr_ref[...] = jnp.zeros(dkr_ref.shape, F32)

        @pl.when(pair == 0)
        def _():
            dq_ref[...] = jnp.zeros(dq_ref.shape, F32)

        @pl.when(qb == kb)
        def _():
            dk_s[...] = jnp.zeros(dk_s.shape, F32)
            dv_s[...] = jnp.zeros(dv_s.shape, F32)

        def step(diagonal):
            for h in range(hp):
                lanes = slice(h * HEAD_PAD, (h + 1) * HEAD_PAD)
                for r0, nr, nk in _sub_blocks(t, diagonal):
                    rows, keys = slice(r0, r0 + nr), slice(0, nk)
                    qv, kvv, dov = q_ref[rows, lanes], kv_ref[keys, lanes], do_ref[rows, lanes]
                    kc = _keys(kvv, kr_ref[keys, :])
                    p = jnp.exp2(_scores(qv, kc, diagonal) - lse_ref[h, rows])
                    delta = jnp.sum(dov.astype(F32) * o_ref[rows, lanes].astype(F32), axis=-1, keepdims=True)
                    dp = lax.dot_general(dov, kvv, NT, preferred_element_type=F32)
                    ds = p * (dp - delta)
                    dv_s[keys, lanes] += lax.dot_general(p.astype(BF16), dov, TN, preferred_element_type=F32)
                    dk_s[keys, lanes] += lax.dot_general(ds.astype(BF16), qv, TN, preferred_element_type=F32)
                    q_rows = pl.ds(pl.multiple_of(qb * t + r0, nr), nr)
                    dq_ref[q_rows, lanes] += lax.dot_general(ds.astype(BF16), kc, NN, preferred_element_type=F32)

        @pl.when(qb > kb)
        def _():
            step(False)

        @pl.when(qb == kb)
        def _():
            step(True)

        @pl.when(qb == nb - 1)
        def _():
            lane = lax.broadcasted_iota(jnp.int32, (t, HEAD_PAD), 1)
            rows = pl.ds(pl.multiple_of(kb * t, t), t)
            for h in range(hp):
                lanes = slice(h * HEAD_PAD, (h + 1) * HEAD_PAD)
                dk = dk_s[:, lanes] * LN_2
                dkv_ref[:, lanes] = jnp.where(lane < QK_NOPE, dk, dv_s[:, lanes])
                dkr_ref[rows, :] += jnp.where(lane >= QK_NOPE, dk, 0.0)

    all_lanes = N_HEADS * HEAD_PAD
    qmap = lambda h, p, qt, kt: (qt[p], h)
    kmap = lambda h, p, qt, kt: (kt[p], h)
    grid_spec = pltpu.PrefetchScalarGridSpec(
        num_scalar_prefetch=2, grid=(N_HEADS // hp, q_tab.shape[0]),
        in_specs=[pl.BlockSpec((t, wide), qmap),
                  pl.BlockSpec((t, wide), kmap),
                  pl.BlockSpec((t, HEAD_PAD), lambda h, p, qt, kt: (kt[p], 0)),
                  pl.BlockSpec((t, wide), qmap),
                  pl.BlockSpec((hp, t, 1), lambda h, p, qt, kt: (h, qt[p], 0)),
                  pl.BlockSpec((t, wide), qmap)],
        out_specs=[pl.BlockSpec((s_len, wide), lambda h, p, qt, kt: (0, h)),
                   pl.BlockSpec((t, wide), kmap),
                   pl.BlockSpec((s_len, HEAD_PAD), lambda h, p, qt, kt: (0, 0))],
        scratch_shapes=[pltpu.VMEM((t, wide), F32), pltpu.VMEM((t, wide), F32)])
    return pl.pallas_call(
        body, name="attn_bwd", grid_spec=grid_spec,
        out_shape=[jax.ShapeDtypeStruct((s_len, all_lanes), F32),
                   jax.ShapeDtypeStruct((s_len, all_lanes), F32),
                   jax.ShapeDtypeStruct((s_len, HEAD_PAD), F32)],
        compiler_params=_cparams(("arbitrary", "arbitrary")),
    )(q_tab, k_tab, q, kv, kr, o, lse, do)


def _adamw(name, w, g, m, v):
    rows, cols = w.shape
    tr = _div_tile(rows, max(8, (2 * 1024 * 1024) // (4 * cols)), 8)

    def body(w_ref, g_ref, m_ref, v_ref, d_ref, nm_ref, nv_ref):
        gv = g_ref[...]
        nm = ADAM_B1 * m_ref[...] + (1.0 - ADAM_B1) * gv
        nv = ADAM_B2 * v_ref[...] + (1.0 - ADAM_B2) * jnp.square(gv)
        m_hat = nm / (1.0 - ADAM_B1 ** ADAM_STEP)
        v_hat = nv / (1.0 - ADAM_B2 ** ADAM_STEP)
        d_ref[...] = -ADAM_LR * (m_hat / (jnp.sqrt(v_hat) + ADAM_EPS) + ADAM_WD * w_ref[...])
        nm_ref[...] = nm
        nv_ref[...] = nv

    spec = pl.BlockSpec((tr, cols), lambda i: (i, 0))
    return pl.pallas_call(
        body, name=name, grid=(rows // tr,), in_specs=[spec] * 4, out_specs=[spec] * 3,
        out_shape=[jax.ShapeDtypeStruct((rows, cols), F32)] * 3,
        compiler_params=_cparams(("parallel",)),
    )(w, g, m, v)


ALL7 = (1, 2, 3, 4, 5, 6, 7)
CHIPS = (2, 4, 6)


def _all_gather(name, src, masks):
    bits = 0
    for m in masks:
        bits |= m
    nslots = {7: 8, 6: 4}[bits]
    nm = len(masks)

    def slot_of(x, y, c):
        return {7: 4 * x + 2 * y + c, 6: 2 * x + y}[bits]

    def body(src_ref, out_ref, send_sems, recv_sems, local_sem):
        x, y, c = lax.axis_index("x"), lax.axis_index("y"), lax.axis_index("c")
        mine = slot_of(x, y, c)
        own = pltpu.make_async_copy(src_ref, out_ref.at[mine], local_sem)
        own.start()
        copies = []
        for i, m in enumerate(masks):
            peer = _peer(x, y, c, m)
            copies.append((
                pltpu.make_async_remote_copy(
                    src_ref=src_ref, dst_ref=out_ref.at[mine], send_sem=send_sems.at[i], recv_sem=recv_sems.at[i],
                    device_id=peer, device_id_type=MESH),
                pltpu.make_async_remote_copy(
                    src_ref=src_ref, dst_ref=out_ref.at[slot_of(*peer)], send_sem=send_sems.at[i],
                    recv_sem=recv_sems.at[i], device_id=peer, device_id_type=MESH)))
        for send, _ in copies:
            send.start()
        for _, arrival in copies:
            arrival.wait_recv()
        for send, _ in copies:
            send.wait_send()
        own.wait()

    return pl.pallas_call(
        body, name=name,
        in_specs=[pl.BlockSpec(memory_space=pl.ANY)], out_specs=pl.BlockSpec(memory_space=pl.ANY),
        out_shape=jax.ShapeDtypeStruct((nslots,) + tuple(src.shape), src.dtype),
        scratch_shapes=[pltpu.SemaphoreType.DMA((nm,)), pltpu.SemaphoreType.DMA((nm,)), pltpu.SemaphoreType.DMA],
    )(src)


def _peer(x, y, c, m):
    return (1 - x if m & 4 else x, 1 - y if m & 2 else y, 1 - c if m & 1 else c)


def _comm_call(name, emit, srcs, out_shapes, n_sems, in_place=False):
    n = len(srcs)

    def body(*refs):
        src_refs, out_refs = refs[:n], refs[n:n + len(out_shapes)]
        send_sems, recv_sems = refs[-2], refs[-1]

        def copy(src, dst, i, peer):
            return pltpu.make_async_remote_copy(src_ref=src, dst_ref=dst, send_sem=send_sems.at[i],
                                                recv_sem=recv_sems.at[i], device_id=peer, device_id_type=MESH)

        emit(lax.axis_index("x"), lax.axis_index("y"), lax.axis_index("c"), src_refs, out_refs, copy)

    hbm = pl.BlockSpec(memory_space=pl.ANY)
    return pl.pallas_call(
        body, name=name, in_specs=[hbm] * n, out_specs=[hbm] * len(out_shapes), out_shape=out_shapes,
        scratch_shapes=[pltpu.SemaphoreType.DMA((n_sems,)), pltpu.SemaphoreType.DMA((n_sems,))],
        input_output_aliases={i: i for i in range(n)} if in_place else {},
    )(*srcs)


HBM_SPEC = pl.BlockSpec(memory_space=pltpu.HBM)
SEM_SPEC = pl.BlockSpec(memory_space=pltpu.SEMAPHORE)
DATAFLOW = pltpu.SideEffectType.DATAFLOW_SIDE_EFFECTING


def _chip_copies(srcs, lands, send_sems, recv_sems, mode):
    x, y, c = lax.axis_index("x"), lax.axis_index("y"), lax.axis_index("c")
    chip = 2 * x + y
    sends, arrivals = [], []
    for j, m in enumerate(CHIPS):
        px, py, _ = _peer(x, y, c, m)
        theirs = 2 * px + py
        for k, (s, l) in enumerate(zip(srcs, lands)):
            if mode == "gather":
                src, dst, got = s.at[c], l.at[chip, c], l.at[theirs, c]
            else:
                src, dst, got = s.at[theirs], l.at[chip], l.at[theirs]
            for to, group in ((dst, sends), (got, arrivals)):
                group.append(pltpu.make_async_remote_copy(
                    src_ref=src, dst_ref=to, send_sem=send_sems.at[3 * k + j], recv_sem=recv_sems.at[3 * k + j],
                    device_id=(px, py, c), device_id_type=MESH))
    return sends, arrivals


def _split_start(name, srcs, land_shapes, mode, after):
    n = len(srcs)

    def body(*refs):
        sends, _ = _chip_copies(refs[:n], refs[n:2 * n], refs[2 * n + 1], refs[2 * n + 2], mode)
        for cp in sends:
            cp.start()
        token = refs[-1]
        token[...] = jnp.zeros(token.shape, token.dtype)

    hbm = lambda a: pltpu.with_memory_space_constraint(a, pltpu.HBM)
    lands = [hbm(lax.empty(s.shape, s.dtype)) for s in land_shapes]
    bufs = [pltpu.HBM(a.shape, a.dtype) for a in list(srcs) + lands]
    res = pl.pallas_call(
        body, name=name,
        out_shape=(pltpu.SemaphoreType.DMA((3 * n,)), pltpu.SemaphoreType.DMA((3 * n,)), *bufs,
                   jax.ShapeDtypeStruct((SUBLANES, LANE), F32)),
        in_specs=[HBM_SPEC] * (2 * n) + [pl.BlockSpec(memory_space=pl.ANY)],
        out_specs=[SEM_SPEC, SEM_SPEC] + [HBM_SPEC] * (2 * n) + [pl.BlockSpec(memory_space=pltpu.VMEM)],
        input_output_aliases={i: 2 + i for i in range(2 * n)},
        compiler_params=pltpu.CompilerParams(has_side_effects=DATAFLOW),
    )(*[hbm(s) for s in srcs], *lands, after)
    return res[0], res[1], res[2:2 + n], res[2 + n:2 + 2 * n], res[-1]


def _split_wait(name, send_sems, recv_sems, srcs, lands, mode, after):
    n = len(srcs)

    def body(*refs):
        sends, arrivals = _chip_copies(refs[:n], refs[n:2 * n], refs[2 * n], refs[2 * n + 1], mode)
        for cp in sends:
            cp.wait_send()
        for cp in arrivals:
            cp.wait_recv()

    res = pl.pallas_call(
        body, name=name,
        out_shape=tuple(pltpu.HBM(a.shape, a.dtype) for a in list(srcs) + list(lands)),
        in_specs=[HBM_SPEC] * (2 * n) + [SEM_SPEC, SEM_SPEC, pl.BlockSpec(memory_space=pl.ANY)],
        out_specs=[HBM_SPEC] * (2 * n),
        input_output_aliases={i: i for i in range(2 * n)},
        compiler_params=pltpu.CompilerParams(has_side_effects=DATAFLOW),
    )(*srcs, *lands, send_sems, recv_sems, after)
    return res[n:]


def _relay_sibling(lands):
    def emit(x, y, c, srcs, outs, copy):
        sib = (x, y, 1 - c)
        sends, arrivals = [], []
        for j, m in enumerate(CHIPS):
            px, py, _ = _peer(x, y, c, m)
            theirs = 2 * px + py
            for k, (s, o) in enumerate(zip(srcs, outs)):
                sends.append(copy(s.at[theirs, c], o.at[theirs, c], 3 * k + j, sib))
                arrivals.append(copy(s.at[theirs, c], o.at[theirs, 1 - c], 3 * k + j, sib))
        for cp in sends:
            cp.start()
        for cp in arrivals:
            cp.wait_recv()
        for cp in sends:
            cp.wait_send()

    shapes = [jax.ShapeDtypeStruct(l.shape, l.dtype) for l in lands]
    return _comm_call("relay_weights", emit, lands, shapes, 3 * len(lands), in_place=True)


def _gather_weights(halves):
    n = len(halves)

    def emit(x, y, c, srcs, outs, copy):
        chip = 2 * x + y
        sib = (x, y, 1 - c)
        first, relay, landed, relayed = [], [], [], []
        for j, m in enumerate(CHIPS):
            px, py, _ = _peer(x, y, c, m)
            theirs = 2 * px + py
            for k in range(n):
                i = 6 * k + j
                first.append(copy(srcs[k].at[c], outs[k].at[chip, c], i, (px, py, c)))
                landed.append(copy(srcs[k].at[c], outs[k].at[theirs, c], i, (px, py, c)))
                relay.append(copy(outs[k].at[theirs, c], outs[k].at[theirs, c], i + 3, sib))
                relayed.append(copy(outs[k].at[theirs, 1 - c], outs[k].at[theirs, 1 - c], i + 3, sib))
        for cp in first:
            cp.start()
        for arrival, onward in zip(landed, relay):
            arrival.wait_recv()
            onward.start()
        for arrival in relayed:
            arrival.wait_recv()
        for cp in first + relay:
            cp.wait_send()

    shapes = [jax.ShapeDtypeStruct((4,) + h.shape, h.dtype) for h in halves]
    return _comm_call("gather_weights", emit, halves, shapes, 6 * n)


def _pair_exchange(name, chunks):
    def emit(x, y, c, srcs, outs, copy):
        sib = (x, y, 1 - c)
        sends = [copy(s.at[:, 1 - c], o, k, sib) for k, (s, o) in enumerate(zip(srcs, outs))]
        for cp in sends:
            cp.start()
        for cp in sends:
            cp.wait_recv()
        for cp in sends:
            cp.wait_send()

    shapes = [jax.ShapeDtypeStruct((4,) + g.shape[2:], g.dtype) for g in chunks]
    return _comm_call(name, emit, chunks, shapes, len(chunks))


def _share_sibling(name, parts):
    def emit(x, y, c, srcs, outs, copy):
        sib = (x, y, 1 - c)
        sends = [copy(s, o.at[c], k, sib) for k, (s, o) in enumerate(zip(srcs, outs))]
        arrivals = [copy(s, o.at[1 - c], k, sib) for k, (s, o) in enumerate(zip(srcs, outs))]
        for cp in sends:
            cp.start()
        for cp in arrivals:
            cp.wait_recv()
        for cp in sends:
            cp.wait_send()

    shapes = [jax.ShapeDtypeStruct((2,) + p.shape, p.dtype) for p in parts]
    return _comm_call(name, emit, parts, shapes, len(parts))


def _reduce_pair(name, chunk, from_sibling, core):
    n, _, h, cols = chunk.shape
    rt = _div_tile(h, max(16, (1 << 20) // (4 * cols)), 16)

    def body(core_ref, a_ref, b_ref, o_ref):
        o_ref[...] = (a_ref[...] + b_ref[...]).astype(o_ref.dtype)

    grid_spec = pltpu.PrefetchScalarGridSpec(
        num_scalar_prefetch=1, grid=(n, h // rt),
        in_specs=[pl.BlockSpec((None, None, rt, cols), lambda s, i, core_ref: (s, core_ref[0], i, 0)),
                  pl.BlockSpec((None, rt, cols), lambda s, i, core_ref: (s, i, 0))],
        out_specs=pl.BlockSpec((None, rt, cols), lambda s, i, core_ref: (s, i, 0)))
    return pl.pallas_call(
        body, name=name, grid_spec=grid_spec, out_shape=jax.ShapeDtypeStruct((n, h, cols), BF16),
        compiler_params=_cparams(("parallel", "parallel")),
    )(core, chunk, from_sibling)


def _reduce_quad(name, q, after=None):
    _, h, cols = q.shape
    rt = _div_tile(h, max(16, (1 << 20) // (4 * cols)), 16)

    def body(q_ref, *rest):
        v = q_ref[...].astype(F32)
        rest[-1][...] = ((v[0] + v[1]) + v[2]) + v[3]

    held = [] if after is None else [after]
    return pl.pallas_call(
        body, name=name, grid=(h // rt,),
        in_specs=[pl.BlockSpec((4, rt, cols), lambda i: (0, i, 0))] + [pl.BlockSpec(memory_space=pl.ANY)] * len(held),
        out_specs=pl.BlockSpec((rt, cols), lambda i: (i, 0)),
        out_shape=jax.ShapeDtypeStruct((h, cols), F32),
        compiler_params=_cparams(("parallel",)),
    )(q, *held)


def _unshard(seg, kind):
    n, r, c = seg.shape
    if kind == "col":
        return seg.transpose(1, 0, 2).reshape(r, n * c)
    return seg.reshape(n * r, c)


def _pad_rows(flat, rows):
    n, ln = flat.shape
    return jnp.pad(flat, ((0, 0), (0, rows * PACK_COLS - ln))).reshape(n, rows, PACK_COLS)


def _block_diag_pairs(w):
    n2, bs, _ = w.shape
    eye = jnp.eye(2, dtype=w.dtype)
    z = w.reshape(n2 // 2, 2, bs, 1, bs) * eye[None, :, None, :, None]
    return z.reshape(n2 // 2, 2 * bs, 2 * bs).transpose(1, 0, 2).reshape(2 * bs, n2 * bs)


def _block_diag_pairs_t(d, bs=64):
    n = d.shape[1] // (2 * bs)
    z = d.reshape(2 * bs, n, 2 * bs).transpose(1, 0, 2).reshape(n, 2, bs, 2, bs)
    return jnp.stack([z[:, 0, :, 0, :], z[:, 1, :, 1, :]], axis=1).reshape(2 * n, bs, bs)


BIG = (("w_in", "col"), ("w_uq", "col"), ("w_ukv", "col"), ("w_proj_rnn", "row"), ("w_proj_mla", "row"),
       ("w_out", "row"), ("w_up", "col"), ("w_down", "row"))
FIRST_USED = ("w_in", "w_uq", "w_ukv")
CONVS = (("conv_w", "col"), ("ffn_conv_w", "col"))
SMALL = ("b_ada", "norm1_g", "conv_b", "w_gate_a", "b_gate_a", "w_gate_x", "b_gate_x", "lru_param",
         "q_norm_g", "kv_norm_g", "norm2_g", "ffn_conv_b", "final_g")
WEIGHTS = ("w_ada", "b_ada", "norm1_g", "w_in", "conv_w", "conv_b", "w_gate_a", "b_gate_a", "w_gate_x",
           "b_gate_x", "lru_param", "q_norm_g", "w_uq", "kv_norm_g", "w_ukv", "w_proj_rnn", "w_proj_mla",
           "w_out", "norm2_g", "w_up", "ffn_conv_w", "ffn_conv_b", "w_down", "final_g")


def _step(x, c, positions, w, m_in, v_in, loss_target):
    s_len, d = x.shape[1], x.shape[2]
    x2d = x[0]
    tgt = loss_target[0]
    xi, yi, ci = lax.axis_index("x"), lax.axis_index("y"), lax.axis_index("c")
    chip = 2 * xi + yi
    me = 2 * chip + ci
    tile = min(256, s_len)
    nt = s_len // tile

    local2d = {k: w[k][0] for k, _ in BIG + CONVS}
    kinds = dict(BIG)
    halves_bf = {k: local2d[k].astype(BF16).reshape(2, local2d[k].shape[0] // 2, local2d[k].shape[1]) for k, _ in BIG}
    first_names = [k for k, _ in BIG if k in FIRST_USED]
    later_names = [k for k, _ in BIG if k not in FIRST_USED]
    full = {}

    def assemble(k, g):
        g = lax.dynamic_update_index_in_dim(g, halves_bf[k][None], chip, 0).reshape((4,) + local2d[k].shape)
        if k == "w_up":
            full["w_up_gate"], full["w_up_val"] = _unshard(g[:2], kinds[k]), _unshard(g[2:], kinds[k])
        else:
            full[k] = _unshard(g, kinds[k])

    first_got = _gather_weights([halves_bf[k] for k in first_names])
    for k, g in zip(first_names, first_got):
        assemble(k, g)
    conv_flat = jnp.concatenate([local2d[k].reshape(-1) for k, _ in CONVS])
    conv_rows = -(-conv_flat.shape[0] // PACK_COLS)
    conv_all = _all_gather("gather_conv_w", _pad_rows(conv_flat[None], conv_rows)[0], CHIPS)
    conv_all = conv_all.reshape(4, -1)
    off = 0
    for k, kind in CONVS:
        r, cc = local2d[k].shape
        full[k] = _unshard(conv_all[:, off:off + r * cc].reshape(4, r, cc), kind)
        off += r * cc

    d_rnn = w["conv_b"].shape[1]
    n_q, n_kv = w["q_norm_g"].shape[1], w["kv_norm_g"].shape[1]
    w_in = full["w_in"]
    o1, o2, o3 = d_rnn + n_q, d_rnn + n_q + n_kv, d_rnn + n_q + n_kv + QK_ROPE
    w_rnn = w_in[:, :d_rnn]
    zpad = lambda n: jnp.zeros((d, n), BF16)
    w_qkv = jnp.concatenate([w_in[:, d_rnn:o2], zpad(QK_NOPE), w_in[:, o2:o3], zpad(LANE - QK_NOPE - QK_ROPE)], axis=1)
    w_g = w_in[:, o3:]
    hd = QK_NOPE + QK_ROPE
    w_uq = jnp.pad(full["w_uq"].reshape(n_q, N_HEADS, hd), ((0, 0), (0, 0), (0, HEAD_PAD - hd))).reshape(n_q, -1)
    w_ukv = full["w_ukv"]
    v_head = w_ukv.shape[1] // N_HEADS - QK_NOPE
    d_ff = w["ffn_conv_b"].shape[1] // 2
    ffn_cw_gate, ffn_cw_val = full["ffn_conv_w"][:, :d_ff], full["ffn_conv_w"][:, d_ff:]
    ffn_cb_gate, ffn_cb_val = w["ffn_conv_b"][:, :d_ff], w["ffn_conv_b"][:, d_ff:]
    conv_w, conv_b = full["conv_w"], w["conv_b"]
    wa_bd = _block_diag_pairs(w["w_gate_a"][0])
    wx_bd = _block_diag_pairs(w["w_gate_x"][0])

    c_all = _all_gather("gather_c", c, ALL7).reshape(8, d)
    c_rows = 128
    (c_act,) = _tiled("silu_c", lambda v: (_silu(v),), 1, [(jnp.pad(c_all, ((0, c_rows - 8), (0, 0))), (c_rows, d), "full")],
                      [((c_rows, d), F32, (c_rows, d), "full")])
    w_ada = w["w_ada"][0]
    n_mod = w_ada.shape[1]
    b_loc = lax.dynamic_slice_in_dim(w["b_ada"], chip * n_mod, n_mod, axis=1)
    mod_loc = _mm("ada_fwd", c_act, w_ada, add=jnp.broadcast_to(b_loc, (c_rows, n_mod)))
    mod_all = _all_gather("gather_mod", mod_loc[:8], CHIPS)
    mod = lax.dynamic_index_in_dim(mod_all, me, 1, keepdims=False).reshape(1, -1)
    shift1, scale1, gate1, shift2, scale2, gate2 = [mod[:, i * d:(i + 1) * d] for i in range(6)]

    small_done = (mod[:, :1] + conv_all[:1, :1] + first_got[0][0, 0, :1, :1].astype(F32))
    later_flight = _split_start(
        "gather_later_start", [halves_bf[k] for k in later_names],
        [jax.ShapeDtypeStruct((4,) + halves_bf[k].shape, BF16) for k in later_names], "gather", after=small_done)

    half = QK_ROPE // 2
    inv_freq = ROPE_THETA ** (-jnp.arange(half, dtype=F32) / half)
    ang = positions[0].astype(F32)[:, None] * inv_freq
    cos, sin = jnp.cos(ang), jnp.sin(ang)
    one, zero = jnp.ones((s_len, QK_NOPE), F32), jnp.zeros((s_len, half), F32)
    tail = jnp.zeros((s_len, LANE - QK_NOPE - QK_ROPE), F32)
    cos_f = jnp.concatenate([one, cos, cos, tail + 1.0], axis=1)
    sin_a = jnp.concatenate([one * 0.0, -sin, zero, tail], axis=1)
    sin_b = jnp.concatenate([one * 0.0, zero, sin, tail], axis=1)
    reset = (positions[0] == 0).astype(F32)[:, None]
    tabs = [(cos_f, (tile, LANE), "row"), (sin_a, (tile, LANE), "row"), (sin_b, (tile, LANE), "row")]

    def rowspec(a):
        return (a, (tile, a.shape[1]), "row")

    def full2(a):
        return (a, a.shape, "full")

    def rowout(cols, dt):
        return ((s_len, cols), dt, (tile, cols), "row")

    def accout(a):
        return (a.shape, F32, a.shape, "acc")

    norm1_g = w["norm1_g"] + later_flight[4][:1, :1]
    norm2_g, final_g = w["norm2_g"], w["final_g"].reshape(1, d)
    ln1_in = [rowspec(x2d), full2(norm1_g), full2(scale1), full2(shift1)]
    (h1,) = _tiled("ln1", _f_ln, nt, ln1_in, [rowout(d, BF16)])
    x_rnn = _mm("in_rnn", h1, w_rnn)
    qkv = _mm("in_qkv", h1, w_qkv)
    gates = _mm("in_gates", h1, w_g)

    ct = LANE
    n_ct = d_rnn // ct
    colspec = lambda a, width=ct: (a, (a.shape[0], width), "col")
    lru_in = [colspec(x_rnn), colspec(conv_w), colspec(conv_b), colspec(wa_bd), colspec(w["b_gate_a"]),
              colspec(wx_bd), colspec(w["b_gate_x"]), colspec(w["lru_param"]), full2(reset)]
    y_rnn, h_rnn = _tiled("lru_fwd", _f_lru_fwd, n_ct, lru_in,
                          [((s_len, d_rnn), BF16, (s_len, ct), "col"), ((s_len, d_rnn), F32, (s_len, ct), "col")])

    qkv_in = [rowspec(qkv)] + tabs + [full2(w["q_norm_g"]), full2(w["kv_norm_g"])]
    qn, kvn, kr = _tiled("qkv_norm", _f_qkv, nt, qkv_in, [rowout(n_q, BF16), rowout(n_kv, BF16), rowout(LANE, BF16)])
    q_pre = _mm("up_q", qn, w_uq)
    kv = _mm("up_kv", kvn, w_ukv, out_dtype=BF16)
    (q_cat,) = _tiled("rot_q", _f_rotq, nt, [rowspec(q_pre)] + tabs, [rowout(q_pre.shape[1], BF16)])
    o_mla, lse = _attn_fwd(q_cat, kv, kr)

    send_sems, recv_sems, flown, landed, _ = later_flight
    landed = _split_wait("gather_later_wait", send_sems, recv_sems, flown, landed, "gather", after=o_mla)
    for k, g in zip(later_names, _relay_sibling(landed)):
        assemble(k, g)
    w_pr = full["w_proj_rnn"]
    w_pm = jnp.pad(full["w_proj_mla"].reshape(N_HEADS, v_head, d), ((0, 0), (HEAD_PAD - v_head, 0), (0, 0))).reshape(-1, d)
    w_out = full["w_out"]
    w_up_gate, w_up_val = full["w_up_gate"], full["w_up_val"]
    w_down = full["w_down"]

    p_rnn = _mm("proj_rnn", y_rnn, w_pr)
    p_mla = _mm("proj_mla", o_mla, w_pm)
    merge_in = [rowspec(gates), rowspec(p_rnn), rowspec(p_mla)]
    (merged,) = _tiled("merge", _f_merge, nt, merge_in, [rowout(d, BF16)])
    o_tok = _mm("out_proj", merged, w_out)
    res_in = [rowspec(x2d), rowspec(o_tok), full2(gate1), full2(norm2_g), full2(scale2), full2(shift2)]
    x1, h2 = _tiled("res_ln2", _f_res_ln, nt, res_in, [rowout(d, F32), rowout(d, BF16)])
    u_gate = _mm("ffn_up_gate", h2, w_up_gate)
    u_val = _mm("ffn_up_val", h2, w_up_val)
    n_ft = d_ff // LANE
    ffn_in = [colspec(a) for a in (u_gate, u_val, ffn_cw_gate, ffn_cw_val, ffn_cb_gate, ffn_cb_val)]
    (act,) = _tiled("ffn_conv", _f_ffn, n_ft, ffn_in, [((s_len, d_ff), BF16, (s_len, LANE), "col")])
    f_tok = _mm("ffn_down", act, w_down)

    loss_in = [rowspec(x1), rowspec(f_tok), rowspec(tgt), full2(gate2), full2(final_g)]
    dx1, df, loss_row, d_gate2, d_final_g = _tiled(
        "loss", _f_loss_and_grads, nt, loss_in,
        [rowout(d, F32), rowout(d, BF16), ((1, LANE), F32, (1, LANE), "acc"), accout(gate2), accout(final_g)])
    loss = lax.psum(loss_row[0, 0], ("x", "y", "c"))

    d_act = _mm("ffn_down_dx", df, w_down, tb=True)
    g_w_down = _mm("ffn_down_dw", act, df, ta=True)
    taps = ffn_cw_gate.shape[0]
    du_gate, du_val, g_cw_gate, g_cw_val, g_cb_gate, g_cb_val = _tiled(
        "ffn_conv_bwd", _vjp_of(_f_ffn, 6, (0, 1, 2, 3, 4, 5)), n_ft, ffn_in + [colspec(d_act)],
        [((s_len, d_ff), BF16, (s_len, LANE), "col")] * 2 + [((taps, d_ff), F32, (taps, LANE), "col")] * 2
        + [((1, d_ff), F32, (1, LANE), "col")] * 2)
    dh2 = _mm("ffn_up_gate_dx", du_gate, w_up_gate, tb=True)
    dh2 = _mm("ffn_up_val_dx", du_val, w_up_val, tb=True, add=dh2)
    g_w_up_halves = [_mm("ffn_up_gate_dw", h2, du_gate, ta=True), _mm("ffn_up_val_dw", h2, du_val, ta=True)]
    g_ffn_cw = jnp.concatenate([g_cw_gate, g_cw_val], axis=1)
    g_ffn_cb = jnp.concatenate([g_cb_gate, g_cb_val], axis=1)

    res_bwd = _vjp_of(_f_res_ln, 6, (0, 1, 2, 3, 4, 5))
    dx_res, do_tok, d_gate1, g_norm2, d_scale2, d_shift2 = _tiled(
        "res_ln2_bwd", res_bwd, nt, res_in + [rowspec(dx1), rowspec(dh2)],
        [rowout(d, F32), rowout(d, BF16), accout(gate1), accout(norm2_g), accout(scale2), accout(shift2)])
    d_merged = _mm("out_proj_dx", do_tok, w_out, tb=True)
    g_w_out = _mm("out_proj_dw", merged, do_tok, ta=True)
    d_gates, dp_rnn, dp_mla = _tiled(
        "merge_bwd", _f_merge_bwd, nt, merge_in + [rowspec(d_merged)],
        [rowout(gates.shape[1], BF16), rowout(d, BF16), rowout(d, BF16)])
    dy_rnn = _mm("proj_rnn_dx", dp_rnn, w_pr, tb=True)
    g_w_pr = _mm("proj_rnn_dw", y_rnn, dp_rnn, ta=True)
    do_mla = _mm("proj_mla_dx", dp_mla, w_pm, tb=True, out_dtype=BF16)
    g_w_pm = _mm("proj_mla_dw", o_mla, dp_mla, ta=True)

    def chunked(k, gk):
        r, cc = local2d[k].shape
        if kinds[k] == "col":
            gk = gk.reshape(r, 4, cc).transpose(1, 0, 2)
        return gk.reshape(4, 2, r // 2, cc)

    def pair_sums(tag, names, chunks):
        out = []
        core = ci.astype(jnp.int32).reshape(1)
        for k, ck, from_sib in zip(names, chunks, _pair_exchange("reduce_pair_exchange_" + tag, chunks)):
            out.append(_reduce_pair("reduce_pair_" + k, ck, from_sib, core))
        return out

    g_later = {
        "w_proj_rnn": g_w_pr,
        "w_proj_mla": g_w_pm.reshape(N_HEADS, HEAD_PAD, d)[:, HEAD_PAD - v_head:, :].reshape(-1, d),
        "w_out": g_w_out, "w_down": g_w_down,
    }
    r_up, c_up = local2d["w_up"].shape
    up_chunks = jnp.concatenate([g.reshape(r_up, 2, c_up).transpose(1, 0, 2) for g in g_w_up_halves], axis=0)
    chunks_ready = [up_chunks.reshape(4, 2, r_up // 2, c_up) if k == "w_up" else chunked(k, g_later[k])
                    for k in later_names]
    sums_ready = pair_sums("ready", later_names, chunks_ready)
    ready_flight = _split_start(
        "reduce_ready_start", sums_ready, [jax.ShapeDtypeStruct(s.shape, s.dtype) for s in sums_ready], "alltoall",
        after=sums_ready[0])
    kr_held = kr + ready_flight[4][:1, :].astype(BF16)

    dq_cat, dkv, dkr = _attn_bwd(q_cat, kv, kr_held, o_mla, lse, do_mla)
    (dq_pre,) = _tiled("rot_q_bwd", _f_rotq_bwd, nt, [rowspec(q_pre)] + tabs + [rowspec(dq_cat)],
                       [rowout(q_pre.shape[1], BF16)])
    dqn = _mm("up_q_dx", dq_pre, w_uq, tb=True)
    g_w_uq = _mm("up_q_dw", qn, dq_pre, ta=True)
    dkv_b = dkv.astype(BF16)
    dkvn = _mm("up_kv_dx", dkv_b, w_ukv, tb=True)
    g_w_ukv = _mm("up_kv_dw", kvn, dkv_b, ta=True)
    dqkv, g_q_norm, g_kv_norm = _tiled(
        "qkv_norm_bwd", _f_qkv_bwd, nt, qkv_in + [rowspec(dqn), rowspec(dkvn), rowspec(dkr)],
        [rowout(qkv.shape[1], BF16), accout(w["q_norm_g"]), accout(w["kv_norm_g"])])

    lru_out = [((s_len, d_rnn), BF16, (s_len, ct), "col")]
    for a in (conv_w, conv_b, wa_bd, w["b_gate_a"], wx_bd, w["b_gate_x"], w["lru_param"]):
        lru_out.append((a.shape, F32, (a.shape[0], ct), "col"))
    dx_rnn, g_conv_w, g_conv_b, g_wa_bd, g_b_a, g_wx_bd, g_b_x, g_lru = _tiled(
        "lru_bwd", _f_lru_bwd, n_ct, lru_in + [colspec(h_rnn), colspec(dy_rnn)], lru_out)

    dh1 = _mm("in_gates_dx", d_gates, w_g, tb=True)
    dh1 = _mm("in_qkv_dx", dqkv, w_qkv, tb=True, add=dh1)
    dh1 = _mm("in_rnn_dx", dx_rnn, w_rnn, tb=True, add=dh1)
    g_w_rnn = _mm("in_rnn_dw", h1, dx_rnn, ta=True)
    g_w_qkv = _mm("in_qkv_dw", h1, dqkv, ta=True)
    g_w_g = _mm("in_gates_dw", h1, d_gates, ta=True)

    ln_bwd = _vjp_of(_f_ln, 4, (0, 1, 2, 3))

    def ln1_bwd(xv, gv, sc, sh, dxr, dh):
        dx, dg, dsc, dsh = ln_bwd(xv, gv, sc, sh, dh)
        return dx + dxr, dg, dsc, dsh

    grad_x, g_norm1, d_scale1, d_shift1 = _tiled(
        "ln1_bwd", ln1_bwd, nt, ln1_in + [rowspec(dx_res), rowspec(dh1)],
        [rowout(d, F32), accout(norm1_g), accout(scale1), accout(shift1)])

    dmod = jnp.concatenate([d_shift1, d_scale1, d_gate1, d_shift2, d_scale2, d_gate2], axis=1)
    dmod_all = _all_gather("gather_dmod", dmod, ALL7).reshape(8, -1)
    dmod_loc = lax.dynamic_slice_in_dim(dmod_all, chip * n_mod, n_mod, axis=1)
    g_w_ada = _mm("ada_dw", c_act, jnp.pad(dmod_loc, ((0, c_rows - 8), (0, 0))), ta=True)

    g_full = {
        "w_in": jnp.concatenate([g_w_rnn, g_w_qkv[:, :n_q + n_kv],
                                 g_w_qkv[:, n_q + n_kv + QK_NOPE:n_q + n_kv + QK_NOPE + QK_ROPE], g_w_g], axis=1),
        "w_uq": g_w_uq.reshape(n_q, N_HEADS, HEAD_PAD)[:, :, :hd].reshape(n_q, -1),
        "w_ukv": g_w_ukv,
        "conv_w": g_conv_w,
        "ffn_conv_w": g_ffn_cw,
    }
    g_small = {
        "b_ada": dmod, "norm1_g": g_norm1, "conv_b": g_conv_b,
        "w_gate_a": _block_diag_pairs_t(g_wa_bd)[None], "b_gate_a": g_b_a,
        "w_gate_x": _block_diag_pairs_t(g_wx_bd)[None], "b_gate_x": g_b_x, "lru_param": g_lru,
        "q_norm_g": g_q_norm, "kv_norm_g": g_kv_norm, "norm2_g": g_norm2,
        "ffn_conv_b": g_ffn_cb, "final_g": d_final_g.reshape(w["final_g"].shape),
    }

    small_flat = jnp.concatenate([g_small[k].reshape(-1) for k in SMALL] + [g_full[k].reshape(-1) for k, _ in CONVS])
    small_rows = -(-small_flat.shape[0] // (8 * PACK_COLS * PACK_ROW_UNIT)) * PACK_ROW_UNIT
    last_names = first_names + ["small"]
    last_chunks = [chunked(k, g_full[k]) for k in first_names]
    last_chunks.append(_pad_rows(small_flat[None], 8 * small_rows).reshape(4, 2, small_rows, PACK_COLS))
    sums_last = pair_sums("last", last_names, last_chunks)
    send_sems, recv_sems, flown, landed, _ = ready_flight
    quads_ready = _split_wait("reduce_ready_wait", send_sems, recv_sems, flown, landed, "alltoall", after=grad_x)
    last_flight = _split_start(
        "reduce_last_start", sums_last, [jax.ShapeDtypeStruct(s.shape, s.dtype) for s in sums_last], "alltoall",
        after=quads_ready[0])
    grads = {"w_ada": g_w_ada[None]}
    delta, new_m, new_v = {}, {}, {}

    def adamw(k):
        shp = w[k].shape
        flip = len(shp) == 3 and shp[-1] % LANE != 0 and shp[-2] % LANE == 0
        view = (lambda a: jnp.swapaxes(a, 1, 2)) if flip else (lambda a: a)
        two_d = (-1, view(w[k]).shape[-1]) if len(shp) > 1 else (1, -1)
        dk, mk, vk = _adamw("adamw_" + k, *[view(a).reshape(two_d) for a in (w[k], grads[k], m_in[k], v_in[k])])
        back = lambda a: view(a.reshape(view(w[k]).shape))
        delta[k], new_m[k], new_v[k] = back(dk), back(mk), back(vk)

    def finish(tag, names, quads, sums, after):
        reduced = {}
        for k, quad, ps in zip(names, quads, sums):
            quad = lax.dynamic_update_index_in_dim(quad, lax.dynamic_index_in_dim(ps, chip, 0, keepdims=True), chip, 0)
            reduced[k] = _reduce_quad("reduce_quad_" + k, quad, after)
        big = [k for k in names if k != "small"]
        for k, both in zip(big, _share_sibling("share_sibling_" + tag, [reduced[k] for k in big])):
            grads[k] = lax.dynamic_update_index_in_dim(both, reduced[k][None], ci, 0).reshape(w[k].shape)
        return reduced

    finish("ready", later_names, quads_ready, sums_ready, after=last_flight[4])
    for k in later_names + ["w_ada"]:
        adamw(k)
    send_sems, recv_sems, flown, landed, _ = last_flight
    quads_last = _split_wait("reduce_last_wait", send_sems, recv_sems, flown, landed, "alltoall",
                             after=delta[later_names[-1]])
    reduced = finish("last", last_names, quads_last, sums_last, after=None)
    small_grad = _all_gather("share_small", reduced["small"], ALL7).reshape(-1)
    off = 0
    for k in SMALL:
        grads[k] = small_grad[off:off + w[k].size].reshape(w[k].shape)
        off += w[k].size
    for k, _ in CONVS:
        r, cc = local2d[k].shape
        whole = small_grad[off:off + 4 * r * cc].reshape(r, 4 * cc)
        grads[k] = lax.dynamic_slice_in_dim(whole, chip * cc, cc, axis=1)[None]
        off += 4 * r * cc
    for k in WEIGHTS:
        if k not in delta:
            adamw(k)

    return (loss, grad_x[None], *[grads[k] for k in WEIGHTS], *[delta[k] for k in WEIGHTS],
            *[new_m[k] for k in WEIGHTS], *[new_v[k] for k in WEIGHTS])


def kernel(x, c, positions, w_ada, b_ada, norm1_g, w_in, conv_w, conv_b, w_gate_a, b_gate_a, w_gate_x, b_gate_x, lru_param, q_norm_g, w_uq, kv_norm_g, w_ukv, w_proj_rnn, w_proj_mla, w_out, norm2_g, w_up, ffn_conv_w, ffn_conv_b, w_down, final_g, loss_target, m_w_ada, m_b_ada, m_norm1_g, m_w_in, m_conv_w, m_conv_b, m_w_gate_a, m_b_gate_a, m_w_gate_x, m_b_gate_x, m_lru_param, m_q_norm_g, m_w_uq, m_kv_norm_g, m_w_ukv, m_w_proj_rnn, m_w_proj_mla, m_w_out, m_norm2_g, m_w_up, m_ffn_conv_w, m_ffn_conv_b, m_w_down, m_final_g, v_w_ada, v_b_ada, v_norm1_g, v_w_in, v_conv_w, v_conv_b, v_w_gate_a, v_b_gate_a, v_w_gate_x, v_b_gate_x, v_lru_param, v_q_norm_g, v_w_uq, v_kv_norm_g, v_w_ukv, v_w_proj_rnn, v_w_proj_mla, v_w_out, v_norm2_g, v_w_up, v_ffn_conv_w, v_ffn_conv_b, v_w_down, v_final_g):
    given = dict(locals())
    w = {k: given[k] for k in WEIGHTS}
    m_in = {k: given["m_" + k] for k in WEIGHTS}
    v_in = {k: given["v_" + k] for k in WEIGHTS}
    return _step(x, c, positions, w, m_in, v_in, loss_target)
```

```python
import functools
import math

import jax
import jax.numpy as jnp
from jax import lax
from jax.experimental import pallas as pl
from jax.experimental.pallas import tpu as pltpu

F32 = jnp.float32
BF16 = jnp.bfloat16

EPS = 1e-6
LRU_C = 8.0
N_HEADS = 16
QK_NOPE = 64
QK_ROPE = 32
HEAD_PAD = 128
ROPE_THETA = 10000.0
ADAM_LR = 0.001
ADAM_B1 = 0.9
ADAM_B2 = 0.999
ADAM_EPS = 1e-08
ADAM_WD = 0.01
ADAM_STEP = 10

LANE = 128
SUBLANES = 8
VMEM_LIMIT = 48 * 1024 * 1024
MM_TILE_M = MM_TILE_N = MM_TILE_K = 1408
PACK_COLS = 1024
PACK_ROW_UNIT = 32
MESH = pl.DeviceIdType.MESH

NN = (((1,), (0,)), ((), ()))
NT = (((1,), (1,)), ((), ()))
TN = (((0,), (0,)), ((), ()))


def _cparams(sem):
    return pltpu.CompilerParams(dimension_semantics=sem, vmem_limit_bytes=VMEM_LIMIT)


def _div_tile(n, cap, unit):
    best = None
    d = unit
    while d <= min(n, cap):
        if n % d == 0:
            best = d
        d += unit
    return n if best is None else best


def _mm(name, a, b, *, ta=False, tb=False, add=None, out_dtype=F32):
    if ta:
        kdim, m = a.shape
    else:
        m, kdim = a.shape
    if tb:
        n, kb = b.shape
    else:
        kb, n = b.shape
    assert kdim == kb, (name, a.shape, b.shape)
    tm = _div_tile(m, MM_TILE_M, 8 if not ta else LANE)
    tn = _div_tile(n, MM_TILE_N, LANE)
    tk = _div_tile(kdim, MM_TILE_K, LANE)
    nk = kdim // tk
    a_spec = pl.BlockSpec((tk, tm), lambda i, j, k: (k, i)) if ta else pl.BlockSpec((tm, tk), lambda i, j, k: (i, k))
    b_spec = pl.BlockSpec((tn, tk), lambda i, j, k: (j, k)) if tb else pl.BlockSpec((tk, tn), lambda i, j, k: (k, j))
    o_spec = pl.BlockSpec((tm, tn), lambda i, j, k: (i, j))
    has_add = add is not None
    dims = ((((0,) if ta else (1,)), ((1,) if tb else (0,))), ((), ()))

    def body(*refs):
        a_ref, b_ref = refs[0], refs[1]
        c_ref = refs[2] if has_add else None
        o_ref = refs[3] if has_add else refs[2]
        prod = lax.dot_general(a_ref[...].astype(BF16), b_ref[...].astype(BF16), dims, preferred_element_type=F32)
        if nk == 1:
            o_ref[...] = (prod + c_ref[...].astype(F32) if has_add else prod).astype(o_ref.dtype)
            return
        acc = refs[-1]
        k = pl.program_id(2)

        @pl.when(k == 0)
        def _():
            acc[...] = prod + c_ref[...].astype(F32) if has_add else prod

        @pl.when(jnp.logical_and(k > 0, k < nk - 1))
        def _():
            acc[...] += prod

        @pl.when(k == nk - 1)
        def _():
            o_ref[...] = (acc[...] + prod).astype(o_ref.dtype)

    ins = [a, b] + ([add] if has_add else [])
    specs = [a_spec, b_spec] + ([o_spec] if has_add else [])
    return pl.pallas_call(
        body, name=name, grid=(m // tm, n // tn, nk), in_specs=specs, out_specs=o_spec,
        out_shape=jax.ShapeDtypeStruct((m, n), out_dtype),
        scratch_shapes=[pltpu.VMEM((tm, tn), F32)] if nk > 1 else [],
        compiler_params=_cparams(("parallel", "parallel", "arbitrary")),
    )(*ins)


_IMAPS = {
    "row": lambda i: (i, 0),
    "col": lambda i: (0, i),
    "full": lambda i: (0, 0),
    "acc": lambda i: (0, 0),
}


def _tiled(name, fn, n, ins, outs):
    ni = len(ins)
    is_acc = [k == "acc" for *_, k in outs]

    def body(*refs):
        vals = fn(*[r[...].astype(F32) if r.dtype == BF16 else r[...] for r in refs[:ni]])
        orefs = refs[ni:]
        if any(is_acc):
            @pl.when(pl.program_id(0) == 0)
            def _():
                for r, a in zip(orefs, is_acc):
                    if a:
                        r[...] = jnp.zeros(r.shape, r.dtype)
        for r, v, a in zip(orefs, vals, is_acc):
            if a:
                r[...] += v.astype(r.dtype)
            else:
                r[...] = v.astype(r.dtype)

    res = pl.pallas_call(
        body, name=name, grid=(n,),
        in_specs=[pl.BlockSpec(bs, _IMAPS[k]) for _, bs, k in ins],
        out_specs=[pl.BlockSpec(bs, _IMAPS[k]) for _, _, bs, k in outs],
        out_shape=[jax.ShapeDtypeStruct(s, d) for s, d, _, _ in outs],
        compiler_params=_cparams(("arbitrary",)),
    )(*[a for a, _, _ in ins])
    return tuple(res)


def _vjp_of(fn, nin, diff):
    def g(*args):
        ins, cots = args[:nin], args[nin:]

        def f(*d):
            full = list(ins)
            for i, v in zip(diff, d):
                full[i] = v
            return fn(*full)

        outs, vjp = jax.vjp(f, *[ins[i] for i in diff])
        return vjp(tuple(c.astype(o.dtype) for c, o in zip(cots, outs)))
    return g


def _shift_rows(x, k, fill, up=False):
    n = x.shape[0]
    if k % SUBLANES == 0:
        pad = jnp.full((k,) + x.shape[1:], fill, x.dtype)
        return jnp.concatenate([x[k:], pad], axis=0) if up else jnp.concatenate([pad, x[:n - k]], axis=0)
    rows = lax.broadcasted_iota(jnp.int32, x.shape, 0)
    if up:
        return jnp.where(rows < n - k, pltpu.roll(x, n - k, 0), fill)
    return jnp.where(rows >= k, pltpu.roll(x, k, 0), fill)


@functools.partial(jax.custom_vjp, nondiff_argnums=(1,))
def _delay(x, k):
    return _shift_rows(x, k, 0.0)


def _delay_fwd(x, k):
    return _shift_rows(x, k, 0.0), None


def _delay_bwd(k, _, g):
    return (_shift_rows(g, k, 0.0, up=True),)


_delay.defvjp(_delay_fwd, _delay_bwd)


@functools.partial(jax.custom_vjp, nondiff_argnums=(1,))
def _lane_roll(x, s):
    return pltpu.roll(x, s, 1)


def _lane_roll_fwd(x, s):
    return pltpu.roll(x, s, 1), None


def _lane_roll_bwd(s, _, g):
    return (pltpu.roll(g, g.shape[1] - s, 1),)


_lane_roll.defvjp(_lane_roll_fwd, _lane_roll_bwd)


@jax.custom_vjp
def _bdot(x, w):
    return lax.dot_general(x.astype(BF16), w.astype(BF16), NN, preferred_element_type=F32)


def _bdot_fwd(x, w):
    return _bdot(x, w), (x, w)


def _bdot_bwd(res, g):
    x, w = res
    gb = g.astype(BF16)
    dx = lax.dot_general(gb, w.astype(BF16), NT, preferred_element_type=F32)
    dw = lax.dot_general(x.T.astype(BF16), gb, NN, preferred_element_type=F32)
    return dx, dw


_bdot.defvjp(_bdot_fwd, _bdot_bwd)


def _sigmoid(x):
    return 0.5 * (jnp.tanh(0.5 * x) + 1.0)


def _silu(x):
    return x * _sigmoid(x)


def _rms(x, g):
    return x * lax.rsqrt(jnp.mean(x * x, axis=-1, keepdims=True) + EPS) * g


def _causal_conv(x, w, b):
    kw = w.shape[0]
    tap = lax.broadcasted_iota(jnp.int32, w.shape, 0)
    y = b
    for k in range(kw):
        d = kw - 1 - k
        wk = jnp.sum(jnp.where(tap == k, w, 0.0), axis=0, keepdims=True)
        y = y + wk * (x if d == 0 else _delay(x, d))
    return y


def _rotate(x, cos_f, sin_a, sin_b):
    reps = x.shape[1] // LANE
    if reps > 1:
        cos_f, sin_a, sin_b = (jnp.tile(t, (1, reps)) for t in (cos_f, sin_a, sin_b))
    n = x.shape[1]
    half = QK_ROPE // 2
    return x * cos_f + _lane_roll(x, n - half) * sin_a + _lane_roll(x, half) * sin_b


def _softplus_neg(l):
    u = jnp.exp(-jnp.abs(l))
    log1p_u = jnp.where(u < 0.01, u * (1.0 - u * (0.5 - u * (1.0 / 3.0))), jnp.log(1.0 + u))
    return jnp.maximum(-l, 0.0) + log1p_u


def _f_ln(x, g, scale, shift):
    return (_rms(x, g) * (1.0 + scale) + shift,)


def _f_qkv(qkv, cos_f, sin_a, sin_b, qg, kvg):
    nq, nkv = qg.shape[1], kvg.shape[1]
    qn = _rms(qkv[:, :nq], qg)
    kvn = _rms(qkv[:, nq:nq + nkv], kvg)
    kr = _rotate(qkv[:, nq + nkv:], cos_f, sin_a, sin_b)
    return qn, kvn, kr


def _f_qkv_bwd(qkv, cos_f, sin_a, sin_b, qg, kvg, dqn, dkvn, dkr):
    nq, nkv = qg.shape[1], kvg.shape[1]
    _, vjp_q = jax.vjp(_rms, qkv[:, :nq], qg)
    _, vjp_kv = jax.vjp(_rms, qkv[:, nq:nq + nkv], kvg)
    _, vjp_r = jax.vjp(lambda t: _rotate(t, cos_f, sin_a, sin_b), qkv[:, nq + nkv:])
    dq_lat, dqg = vjp_q(dqn)
    dkv_lat, dkvg = vjp_kv(dkvn)
    (dkr_pre,) = vjp_r(dkr)
    return jnp.concatenate([dq_lat, dkv_lat, dkr_pre], axis=1), dqg, dkvg


QK_SCALE = 1.0 / math.sqrt(QK_NOPE + QK_ROPE)
LOG2_E = 1.4426950408889634
LN_2 = 0.6931471805599453


def _f_rotq(q, cos_f, sin_a, sin_b):
    return (_rotate(q, cos_f, sin_a, sin_b) * (QK_SCALE * LOG2_E),)


def _f_rotq_bwd(q, cos_f, sin_a, sin_b, dq):
    _, vjp = jax.vjp(lambda t: _rotate(t, cos_f, sin_a, sin_b) * QK_SCALE, q)
    return vjp(dq)


def _merge(g_rnn, g_mla, p_rnn, p_mla):
    return _sigmoid(g_rnn) * p_rnn + _sigmoid(g_mla) * p_mla


def _f_merge(g, p_rnn, p_mla):
    d = p_rnn.shape[1]
    return (_merge(g[:, :d], g[:, d:], p_rnn, p_mla),)


def _f_merge_bwd(g, p_rnn, p_mla, dm):
    d = p_rnn.shape[1]
    _, vjp = jax.vjp(_merge, g[:, :d], g[:, d:], p_rnn, p_mla)
    dg_rnn, dg_mla, dp_rnn, dp_mla = vjp(dm)
    return jnp.concatenate([dg_rnn, dg_mla], axis=1), dp_rnn, dp_mla


def _f_res_ln(x, o, gate, g2, scale, shift):
    x1 = x + gate * o
    return x1, _rms(x1, g2) * (1.0 + scale) + shift


def _f_ffn(u_gate, u_val, cw_gate, cw_val, cb_gate, cb_val):
    return (_silu(_causal_conv(u_gate, cw_gate, cb_gate)) * _causal_conv(u_val, cw_val, cb_val),)


def _f_loss(x1, f, tgt, gate, fg):
    y = _rms(x1 + gate * f, fg)
    err = (y - tgt) * (y - tgt)
    return 0.5 * jnp.sum(jnp.mean(err, axis=-1, keepdims=True), axis=0, keepdims=True)


def _f_loss_and_grads(x1, f, tgt, gate, fg):
    loss, vjp = jax.vjp(lambda a, b, c, d: _f_loss(a, b, tgt, c, d), x1, f, gate, fg)
    dx1, df, dgate, dfg = vjp(jnp.ones((1, 1), F32))
    return dx1, df, jnp.broadcast_to(loss, (1, LANE)), dgate, dfg


@jax.custom_vjp
def _decay_and_gain(log_a):
    a = jnp.exp(log_a)
    return a, jnp.sqrt(-jnp.tanh(log_a) * (1.0 + a * a))


def _decay_and_gain_fwd(log_a):
    a, gain = _decay_and_gain(log_a)
    return (a, gain), (a, gain)


def _decay_and_gain_bwd(res, g):
    a, gain = res
    return (g[0] * a - g[1] * (a * a) / gain,)


_decay_and_gain.defvjp(_decay_and_gain_fwd, _decay_and_gain_bwd)


def _f_lru_coeffs(xr, cw, cb, wa, ba, wx, bx, lru, reset):
    xc = _causal_conv(xr, cw, cb)
    r = _sigmoid(_bdot(xc, wa) + ba)
    i = _sigmoid(_bdot(xc, wx) + bx)
    log_a = (-LRU_C) * r * _softplus_neg(lru)
    a, mult = _decay_and_gain(log_a)
    is_reset = reset > 0.5
    a = jnp.where(is_reset, 0.0, a)
    mult = jnp.where(is_reset, 1.0, mult)
    return a, mult * (i * xc)


SCAN_BLOCK = 64


def _scan(a, b, up=False):
    n = a.shape[0]
    blk = min(SCAN_BLOCK, n)
    pos = lax.broadcasted_iota(jnp.int32, a.shape, 0) % blk
    k = 1
    while k < blk:
        inside = (pos < blk - k) if up else (pos >= k)
        shift = n - k if up else k
        b = b + a * jnp.where(inside, pltpu.roll(b, shift, 0), 0.0)
        a = a * jnp.where(inside, pltpu.roll(a, shift, 0), 1.0)
        k *= 2
    blocks = range(n // blk)
    carry = jnp.zeros((1,) + a.shape[1:], a.dtype)
    out = [None] * len(blocks)
    for i in (reversed(blocks) if up else blocks):
        rows = slice(i * blk, (i + 1) * blk)
        out[i] = b[rows] + a[rows] * carry
        carry = out[i][:1] if up else out[i][blk - 1:]
    return jnp.concatenate(out, axis=0)


def _f_lru_fwd(xr, cw, cb, wa, ba, wx, bx, lru, reset):
    a, b = _f_lru_coeffs(xr, cw, cb, wa, ba, wx, bx, lru, reset)
    h = _scan(a, b)
    return h, h


def _f_lru_bwd(xr, cw, cb, wa, ba, wx, bx, lru, reset, h, dh):
    (a, _), vjp = jax.vjp(lambda *p: _f_lru_coeffs(*p, reset), xr, cw, cb, wa, ba, wx, bx, lru)
    g = _scan(_shift_rows(a, 1, 0.0, up=True), dh, up=True)
    return vjp((g * _shift_rows(h, 1, 0.0), g))


def _attn_tile(s):
    return 1024 if s >= 2048 else s // 2


def _keys(kv, kr):
    lane = lax.broadcasted_iota(jnp.int32, kv.shape, 1)
    return jnp.where(lane < QK_NOPE, kv, kr)


ATTN_HEADS_PER_STEP = 2


def _scores(q, kc, diagonal):
    s = lax.dot_general(q, kc, NT, preferred_element_type=F32)
    if not diagonal:
        return s
    rows = lax.broadcasted_iota(jnp.int32, s.shape, 0)
    cols = lax.broadcasted_iota(jnp.int32, s.shape, 1)
    return jnp.where(cols - (s.shape[1] - s.shape[0]) <= rows, s, -jnp.inf)


def _sub_blocks(t, diagonal):
    return ((0, t // 2, t // 2), (t // 2, t // 2, t)) if diagonal else ((0, t, t),)


def _causal_pairs(nb, k_major):
    if k_major:
        pairs = [(qb, kb) for kb in range(nb) for qb in range(kb, nb)]
    else:
        pairs = [(qb, kb) for qb in range(nb) for kb in range(qb + 1)]
    return jnp.array([p[0] for p in pairs], jnp.int32), jnp.array([p[1] for p in pairs], jnp.int32)


def _attn_fwd(q, kv, kr):
    s_len = q.shape[0]
    t = _attn_tile(s_len)
    nb = s_len // t
    hp = ATTN_HEADS_PER_STEP
    wide = hp * HEAD_PAD
    q_tab, k_tab = _causal_pairs(nb, k_major=False)

    def body(qt, kt, q_ref, kv_ref, kr_ref, o_ref, lse_ref, m_s, acc_s):
        pair = pl.program_id(1)
        qi, ki = qt[pair], kt[pair]

        @pl.when(ki == 0)
        def _():
            m_s[...] = jnp.full(m_s.shape, -jnp.inf, F32)
            acc_s[...] = jnp.zeros(acc_s.shape, F32)

        def step(diagonal):
            for h in range(hp):
                lanes = slice(h * HEAD_PAD, (h + 1) * HEAD_PAD)
                for r0, nr, nk in _sub_blocks(t, diagonal):
                    rows = slice(r0, r0 + nr)
                    kvv = kv_ref[:nk, lanes]
                    s = _scores(q_ref[rows, lanes], _keys(kvv, kr_ref[:nk, :]), diagonal)
                    m_old = m_s[h, rows]
                    m_new = jnp.maximum(m_old, jnp.max(s, axis=-1, keepdims=True))
                    alpha = jnp.exp2(m_old - m_new)
                    p = jnp.exp2(s - m_new)
                    lane = lax.broadcasted_iota(jnp.int32, kvv.shape, 1)
                    ones_and_values = jnp.where(lane < QK_NOPE, jnp.ones_like(kvv), kvv)
                    acc_s[rows, lanes] = alpha * acc_s[rows, lanes] + lax.dot_general(
                        p.astype(BF16), ones_and_values, NN, preferred_element_type=F32)
                    m_s[h, rows] = m_new

        @pl.when(ki < qi)
        def _():
            step(False)

        @pl.when(ki == qi)
        def _():
            step(True)
            lane = lax.broadcasted_iota(jnp.int32, (t, HEAD_PAD), 1)
            for h in range(hp):
                lanes = slice(h * HEAD_PAD, (h + 1) * HEAD_PAD)
                acc = acc_s[:, lanes]
                total = acc[:, :1]
                o_ref[:, lanes] = jnp.where(lane >= QK_NOPE, acc / total, 0.0).astype(o_ref.dtype)
                lse_ref[h] = m_s[h] + jnp.log(total) * LOG2_E

    grid_spec = pltpu.PrefetchScalarGridSpec(
        num_scalar_prefetch=2, grid=(N_HEADS // hp, q_tab.shape[0]),
        in_specs=[pl.BlockSpec((t, wide), lambda h, p, qt, kt: (qt[p], h)),
                  pl.BlockSpec((t, wide), lambda h, p, qt, kt: (kt[p], h)),
                  pl.BlockSpec((t, HEAD_PAD), lambda h, p, qt, kt: (kt[p], 0))],
        out_specs=[pl.BlockSpec((t, wide), lambda h, p, qt, kt: (qt[p], h)),
                   pl.BlockSpec((hp, t, 1), lambda h, p, qt, kt: (h, qt[p], 0))],
        scratch_shapes=[pltpu.VMEM((hp, t, 1), F32), pltpu.VMEM((t, wide), F32)])
    return pl.pallas_call(
        body, name="attn_fwd", grid_spec=grid_spec,
        out_shape=[jax.ShapeDtypeStruct((s_len, N_HEADS * HEAD_PAD), BF16),
                   jax.ShapeDtypeStruct((N_HEADS, s_len, 1), F32)],
        compiler_params=_cparams(("arbitrary", "arbitrary")),
    )(q_tab, k_tab, q, kv, kr)


def _attn_bwd(q, kv, kr, o, lse, do):
    s_len = q.shape[0]
    t = _attn_tile(s_len)
    nb = s_len // t
    hp = ATTN_HEADS_PER_STEP
    wide = hp * HEAD_PAD
    q_tab, k_tab = _causal_pairs(nb, k_major=True)

    def body(qt, kt, q_ref, kv_ref, kr_ref, o_ref, lse_ref, do_ref, dq_ref, dkv_ref, dkr_ref, dk_s, dv_s):
        g, pair = pl.program_id(0), pl.program_id(1)
        qb, kb = qt[pair], kt[pair]

        @pl.when(jnp.logical_and(g == 0, pair == 0))
        def _():
            dkr_ref[...] = jnp.zeros(dkr_ref.shape, F32)

        @pl.when(pair == 0)
        def _():
            dq_ref[...] = jnp.zeros(dq_ref.shape, F32)

        @pl.when(qb == kb)
        def _():
            dk_s[...] = jnp.zeros(dk_s.shape, F32)
            dv_s[...] = jnp.zeros(dv_s.shape, F32)

        def step(diagonal):
            for h in range(hp):
                lanes = slice(h * HEAD_PAD, (h + 1) * HEAD_PAD)
                for r0, nr, nk in _sub_blocks(t, diagonal):
                    rows, keys = slice(r0, r0 + nr), slice(0, nk)
                    qv, kvv, dov = q_ref[rows, lanes], kv_ref[keys, lanes], do_ref[rows, lanes]
                    kc = _keys(kvv, kr_ref[keys, :])
                    p = jnp.exp2(_scores(qv, kc, diagonal) - lse_ref[h, rows])
                    delta = jnp.sum(dov.astype(F32) * o_ref[rows, lanes].astype(F32), axis=-1, keepdims=True)
                    dp = lax.dot_general(dov, kvv, NT, preferred_element_type=F32)
                    ds = p * (dp - delta)
                    dv_s[keys, lanes] += lax.dot_general(p.astype(BF16), dov, TN, preferred_element_type=F32)
                    dk_s[keys, lanes] += lax.dot_general(ds.astype(BF16), qv, TN, preferred_element_type=F32)
                    q_rows = pl.ds(pl.multiple_of(qb * t + r0, nr), nr)
                    dq_ref[q_rows, lanes] += lax.dot_general(ds.astype(BF16), kc, NN, preferred_element_type=F32)

        @pl.when(qb > kb)
        def _():
            step(False)

        @pl.when(qb == kb)
        def _():
            step(True)

        @pl.when(qb == nb - 1)
        def _():
            lane = lax.broadcasted_iota(jnp.int32, (t, HEAD_PAD), 1)
            rows = pl.ds(pl.multiple_of(kb * t, t), t)
            for h in range(hp):
                lanes = slice(h * HEAD_PAD, (h + 1) * HEAD_PAD)
                dk = dk_s[:, lanes] * LN_2
                dkv_ref[:, lanes] = jnp.where(lane < QK_NOPE, dk, dv_s[:, lanes]).astype(dkv_ref.dtype)
                dkr_ref[rows, :] += jnp.where(lane >= QK_NOPE, dk, 0.0)

    all_lanes = N_HEADS * HEAD_PAD
    qmap = lambda h, p, qt, kt: (qt[p], h)
    kmap = lambda h, p, qt, kt: (kt[p], h)
    grid_spec = pltpu.PrefetchScalarGridSpec(
        num_scalar_prefetch=2, grid=(N_HEADS // hp, q_tab.shape[0]),
        in_specs=[pl.BlockSpec((t, wide), qmap),
                  pl.BlockSpec((t, wide), kmap),
                  pl.BlockSpec((t, HEAD_PAD), lambda h, p, qt, kt: (kt[p], 0)),
                  pl.BlockSpec((t, wide), qmap),
                  pl.BlockSpec((hp, t, 1), lambda h, p, qt, kt: (h, qt[p], 0)),
                  pl.BlockSpec((t, wide), qmap)],
        out_specs=[pl.BlockSpec((s_len, wide), lambda h, p, qt, kt: (0, h)),
                   pl.BlockSpec((t, wide), kmap),
                   pl.BlockSpec((s_len, HEAD_PAD), lambda h, p, qt, kt: (0, 0))],
        scratch_shapes=[pltpu.VMEM((t, wide), F32), pltpu.VMEM((t, wide), F32)])
    return pl.pallas_call(
        body, name="attn_bwd", grid_spec=grid_spec,
        out_shape=[jax.ShapeDtypeStruct((s_len, all_lanes), F32),
                   jax.ShapeDtypeStruct((s_len, all_lanes), BF16),
                   jax.ShapeDtypeStruct((s_len, HEAD_PAD), F32)],
        compiler_params=_cparams(("arbitrary", "arbitrary")),
    )(q_tab, k_tab, q, kv, kr, o, lse, do)


def _adamw(name, w, g, m, v):
    rows, cols = w.shape
    tr = _div_tile(rows, max(8, (2 * 1024 * 1024) // (4 * cols)), 8)

    def body(w_ref, g_ref, m_ref, v_ref, d_ref, nm_ref, nv_ref):
        gv = g_ref[...]
        nm = ADAM_B1 * m_ref[...] + (1.0 - ADAM_B1) * gv
        nv = ADAM_B2 * v_ref[...] + (1.0 - ADAM_B2) * jnp.square(gv)
        m_hat = nm / (1.0 - ADAM_B1 ** ADAM_STEP)
        v_hat = nv / (1.0 - ADAM_B2 ** ADAM_STEP)
        d_ref[...] = -ADAM_LR * (m_hat / (jnp.sqrt(v_hat) + ADAM_EPS) + ADAM_WD * w_ref[...])
        nm_ref[...] = nm
        nv_ref[...] = nv

    spec = pl.BlockSpec((tr, cols), lambda i: (i, 0))
    return pl.pallas_call(
        body, name=name, grid=(rows // tr,), in_specs=[spec] * 4, out_specs=[spec] * 3,
        out_shape=[jax.ShapeDtypeStruct((rows, cols), F32)] * 3,
        compiler_params=_cparams(("parallel",)),
    )(w, g, m, v)


ALL7 = (1, 2, 3, 4, 5, 6, 7)
CHIPS = (2, 4, 6)


def _all_gather(name, src, masks):
    bits = 0
    for m in masks:
        bits |= m
    nslots = {7: 8, 6: 4}[bits]
    nm = len(masks)

    def slot_of(x, y, c):
        return {7: 4 * x + 2 * y + c, 6: 2 * x + y}[bits]

    def body(src_ref, out_ref, send_sems, recv_sems, local_sem):
        x, y, c = lax.axis_index("x"), lax.axis_index("y"), lax.axis_index("c")
        mine = slot_of(x, y, c)
        own = pltpu.make_async_copy(src_ref, out_ref.at[mine], local_sem)
        own.start()
        copies = []
        for i, m in enumerate(masks):
            peer = _peer(x, y, c, m)
            copies.append((
                pltpu.make_async_remote_copy(
                    src_ref=src_ref, dst_ref=out_ref.at[mine], send_sem=send_sems.at[i], recv_sem=recv_sems.at[i],
                    device_id=peer, device_id_type=MESH),
                pltpu.make_async_remote_copy(
                    src_ref=src_ref, dst_ref=out_ref.at[slot_of(*peer)], send_sem=send_sems.at[i],
                    recv_sem=recv_sems.at[i], device_id=peer, device_id_type=MESH)))
        for send, _ in copies:
            send.start()
        for _, arrival in copies:
            arrival.wait_recv()
        for send, _ in copies:
            send.wait_send()
        own.wait()

    return pl.pallas_call(
        body, name=name,
        in_specs=[pl.BlockSpec(memory_space=pl.ANY)], out_specs=pl.BlockSpec(memory_space=pl.ANY),
        out_shape=jax.ShapeDtypeStruct((nslots,) + tuple(src.shape), src.dtype),
        scratch_shapes=[pltpu.SemaphoreType.DMA((nm,)), pltpu.SemaphoreType.DMA((nm,)), pltpu.SemaphoreType.DMA],
    )(src)


def _peer(x, y, c, m):
    return (1 - x if m & 4 else x, 1 - y if m & 2 else y, 1 - c if m & 1 else c)


def _comm_call(name, emit, srcs, out_shapes, n_sems, in_place=False):
    n = len(srcs)

    def body(*refs):
        src_refs, out_refs = refs[:n], refs[n:n + len(out_shapes)]
        send_sems, recv_sems = refs[-2], refs[-1]

        def copy(src, dst, i, peer):
            return pltpu.make_async_remote_copy(src_ref=src, dst_ref=dst, send_sem=send_sems.at[i],
                                                recv_sem=recv_sems.at[i], device_id=peer, device_id_type=MESH)

        emit(lax.axis_index("x"), lax.axis_index("y"), lax.axis_index("c"), src_refs, out_refs, copy)

    hbm = pl.BlockSpec(memory_space=pl.ANY)
    return pl.pallas_call(
        body, name=name, in_specs=[hbm] * n, out_specs=[hbm] * len(out_shapes), out_shape=out_shapes,
        scratch_shapes=[pltpu.SemaphoreType.DMA((n_sems,)), pltpu.SemaphoreType.DMA((n_sems,))],
        input_output_aliases={i: i for i in range(n)} if in_place else {},
    )(*srcs)


HBM_SPEC = pl.BlockSpec(memory_space=pltpu.HBM)
SEM_SPEC = pl.BlockSpec(memory_space=pltpu.SEMAPHORE)
DATAFLOW = pltpu.SideEffectType.DATAFLOW_SIDE_EFFECTING


def _chip_copies(srcs, lands, send_sems, recv_sems, mode):
    x, y, c = lax.axis_index("x"), lax.axis_index("y"), lax.axis_index("c")
    chip = 2 * x + y
    sends, arrivals = [], []
    for j, m in enumerate(CHIPS):
        px, py, _ = _peer(x, y, c, m)
        theirs = 2 * px + py
        for k, (s, l) in enumerate(zip(srcs, lands)):
            if mode == "gather":
                src, dst, got = s.at[c], l.at[chip, c], l.at[theirs, c]
            else:
                src, dst, got = s.at[theirs], l.at[chip], l.at[theirs]
            for to, group in ((dst, sends), (got, arrivals)):
                group.append(pltpu.make_async_remote_copy(
                    src_ref=src, dst_ref=to, send_sem=send_sems.at[3 * k + j], recv_sem=recv_sems.at[3 * k + j],
                    device_id=(px, py, c), device_id_type=MESH))
    return sends, arrivals


def _split_start(name, srcs, land_shapes, mode, after):
    n = len(srcs)

    def body(*refs):
        sends, _ = _chip_copies(refs[:n], refs[n:2 * n], refs[2 * n + 1], refs[2 * n + 2], mode)
        for cp in sends:
            cp.start()
        token = refs[-1]
        token[...] = jnp.zeros(token.shape, token.dtype)

    hbm = lambda a: pltpu.with_memory_space_constraint(a, pltpu.HBM)
    lands = [hbm(lax.empty(s.shape, s.dtype)) for s in land_shapes]
    bufs = [pltpu.HBM(a.shape, a.dtype) for a in list(srcs) + lands]
    res = pl.pallas_call(
        body, name=name,
        out_shape=(pltpu.SemaphoreType.DMA((3 * n,)), pltpu.SemaphoreType.DMA((3 * n,)), *bufs,
                   jax.ShapeDtypeStruct((SUBLANES, LANE), F32)),
        in_specs=[HBM_SPEC] * (2 * n) + [pl.BlockSpec(memory_space=pl.ANY)],
        out_specs=[SEM_SPEC, SEM_SPEC] + [HBM_SPEC] * (2 * n) + [pl.BlockSpec(memory_space=pltpu.VMEM)],
        input_output_aliases={i: 2 + i for i in range(2 * n)},
        compiler_params=pltpu.CompilerParams(has_side_effects=DATAFLOW),
    )(*[hbm(s) for s in srcs], *lands, after)
    return res[0], res[1], res[2:2 + n], res[2 + n:2 + 2 * n], res[-1]


def _split_wait(name, send_sems, recv_sems, srcs, lands, mode, after):
    n = len(srcs)

    def body(*refs):
        sends, arrivals = _chip_copies(refs[:n], refs[n:2 * n], refs[2 * n], refs[2 * n + 1], mode)
        for cp in sends:
            cp.wait_send()
        for cp in arrivals:
            cp.wait_recv()

    res = pl.pallas_call(
        body, name=name,
        out_shape=tuple(pltpu.HBM(a.shape, a.dtype) for a in list(srcs) + list(lands)),
        in_specs=[HBM_SPEC] * (2 * n) + [SEM_SPEC, SEM_SPEC, pl.BlockSpec(memory_space=pl.ANY)],
        out_specs=[HBM_SPEC] * (2 * n),
        input_output_aliases={i: i for i in range(2 * n)},
        compiler_params=pltpu.CompilerParams(has_side_effects=DATAFLOW),
    )(*srcs, *lands, send_sems, recv_sems, after)
    return res[n:]


def _relay_sibling(lands):
    def emit(x, y, c, srcs, outs, copy):
        sib = (x, y, 1 - c)
        sends, arrivals = [], []
        for j, m in enumerate(CHIPS):
            px, py, _ = _peer(x, y, c, m)
            theirs = 2 * px + py
            for k, (s, o) in enumerate(zip(srcs, outs)):
                sends.append(copy(s.at[theirs, c], o.at[theirs, c], 3 * k + j, sib))
                arrivals.append(copy(s.at[theirs, c], o.at[theirs, 1 - c], 3 * k + j, sib))
        for cp in sends:
            cp.start()
        for cp in arrivals:
            cp.wait_recv()
        for cp in sends:
            cp.wait_send()

    shapes = [jax.ShapeDtypeStruct(l.shape, l.dtype) for l in lands]
    return _comm_call("relay_weights", emit, lands, shapes, 3 * len(lands), in_place=True)


def _gather_weights(halves):
    n = len(halves)

    def emit(x, y, c, srcs, outs, copy):
        chip = 2 * x + y
        sib = (x, y, 1 - c)
        first, relay, landed, relayed = [], [], [], []
        for j, m in enumerate(CHIPS):
            px, py, _ = _peer(x, y, c, m)
            theirs = 2 * px + py
            for k in range(n):
                i = 6 * k + j
                first.append(copy(srcs[k].at[c], outs[k].at[chip, c], i, (px, py, c)))
                landed.append(copy(srcs[k].at[c], outs[k].at[theirs, c], i, (px, py, c)))
                relay.append(copy(outs[k].at[theirs, c], outs[k].at[theirs, c], i + 3, sib))
                relayed.append(copy(outs[k].at[theirs, 1 - c], outs[k].at[theirs, 1 - c], i + 3, sib))
        for cp in first:
            cp.start()
        for arrival, onward in zip(landed, relay):
            arrival.wait_recv()
            onward.start()
        for arrival in relayed:
            arrival.wait_recv()
        for cp in first + relay:
            cp.wait_send()

    shapes = [jax.ShapeDtypeStruct((4,) + h.shape, h.dtype) for h in halves]
    return _comm_call("gather_weights", emit, halves, shapes, 6 * n)


def _pair_exchange(name, chunks):
    def emit(x, y, c, srcs, outs, copy):
        sib = (x, y, 1 - c)
        sends = [copy(s.at[:, 1 - c], o, k, sib) for k, (s, o) in enumerate(zip(srcs, outs))]
        for cp in sends:
            cp.start()
        for cp in sends:
            cp.wait_recv()
        for cp in sends:
            cp.wait_send()

    shapes = [jax.ShapeDtypeStruct((4,) + g.shape[2:], g.dtype) for g in chunks]
    return _comm_call(name, emit, chunks, shapes, len(chunks))


def _share_sibling(name, parts):
    def emit(x, y, c, srcs, outs, copy):
        sib = (x, y, 1 - c)
        sends = [copy(s, o.at[c], k, sib) for k, (s, o) in enumerate(zip(srcs, outs))]
        arrivals = [copy(s, o.at[1 - c], k, sib) for k, (s, o) in enumerate(zip(srcs, outs))]
        for cp in sends:
            cp.start()
        for cp in arrivals:
            cp.wait_recv()
        for cp in sends:
            cp.wait_send()

    shapes = [jax.ShapeDtypeStruct((2,) + p.shape, p.dtype) for p in parts]
    return _comm_call(name, emit, parts, shapes, len(parts))


def _reduce_pair(name, chunk, from_sibling, core):
    n, _, h, cols = chunk.shape
    rt = _div_tile(h, max(16, (1 << 20) // (4 * cols)), 16)

    def body(core_ref, a_ref, b_ref, o_ref):
        o_ref[...] = (a_ref[...] + b_ref[...]).astype(o_ref.dtype)

    grid_spec = pltpu.PrefetchScalarGridSpec(
        num_scalar_prefetch=1, grid=(n, h // rt),
        in_specs=[pl.BlockSpec((None, None, rt, cols), lambda s, i, core_ref: (s, core_ref[0], i, 0)),
                  pl.BlockSpec((None, rt, cols), lambda s, i, core_ref: (s, i, 0))],
        out_specs=pl.BlockSpec((None, rt, cols), lambda s, i, core_ref: (s, i, 0)))
    return pl.pallas_call(
        body, name=name, grid_spec=grid_spec, out_shape=jax.ShapeDtypeStruct((n, h, cols), BF16),
        compiler_params=_cparams(("parallel", "parallel")),
    )(core, chunk, from_sibling)


def _reduce_quad(name, q, after=None):
    _, h, cols = q.shape
    rt = _div_tile(h, max(16, (1 << 20) // (4 * cols)), 16)

    def body(q_ref, *rest):
        v = q_ref[...].astype(F32)
        rest[-1][...] = ((v[0] + v[1]) + v[2]) + v[3]

    held = [] if after is None else [after]
    return pl.pallas_call(
        body, name=name, grid=(h // rt,),
        in_specs=[pl.BlockSpec((4, rt, cols), lambda i: (0, i, 0))] + [pl.BlockSpec(memory_space=pl.ANY)] * len(held),
        out_specs=pl.BlockSpec((rt, cols), lambda i: (i, 0)),
        out_shape=jax.ShapeDtypeStruct((h, cols), F32),
        compiler_params=_cparams(("parallel",)),
    )(q, *held)


def _unshard(seg, kind):
    n, r, c = seg.shape
    if kind == "col":
        return seg.transpose(1, 0, 2).reshape(r, n * c)
    return seg.reshape(n * r, c)


def _pad_rows(flat, rows):
    n, ln = flat.shape
    return jnp.pad(flat, ((0, 0), (0, rows * PACK_COLS - ln))).reshape(n, rows, PACK_COLS)


def _block_diag_pairs(w):
    n2, bs, _ = w.shape
    eye = jnp.eye(2, dtype=w.dtype)
    z = w.reshape(n2 // 2, 2, bs, 1, bs) * eye[None, :, None, :, None]
    return z.reshape(n2 // 2, 2 * bs, 2 * bs).transpose(1, 0, 2).reshape(2 * bs, n2 * bs)


def _block_diag_pairs_t(d, bs=64):
    n = d.shape[1] // (2 * bs)
    z = d.reshape(2 * bs, n, 2 * bs).transpose(1, 0, 2).reshape(n, 2, bs, 2, bs)
    return jnp.stack([z[:, 0, :, 0, :], z[:, 1, :, 1, :]], axis=1).reshape(2 * n, bs, bs)


BIG = (("w_in", "col"), ("w_uq", "col"), ("w_ukv", "col"), ("w_proj_rnn", "row"), ("w_proj_mla", "row"),
       ("w_out", "row"), ("w_up", "col"), ("w_down", "row"))
FIRST_USED = ("w_in", "w_uq", "w_ukv")
CONVS = (("conv_w", "col"), ("ffn_conv_w", "col"))
SMALL = ("b_ada", "norm1_g", "conv_b", "w_gate_a", "b_gate_a", "w_gate_x", "b_gate_x", "lru_param",
         "q_norm_g", "kv_norm_g", "norm2_g", "ffn_conv_b", "final_g")
WEIGHTS = ("w_ada", "b_ada", "norm1_g", "w_in", "conv_w", "conv_b", "w_gate_a", "b_gate_a", "w_gate_x",
           "b_gate_x", "lru_param", "q_norm_g", "w_uq", "kv_norm_g", "w_ukv", "w_proj_rnn", "w_proj_mla",
           "w_out", "norm2_g", "w_up", "ffn_conv_w", "ffn_conv_b", "w_down", "final_g")


def _step(x, c, positions, w, m_in, v_in, loss_target):
    s_len, d = x.shape[1], x.shape[2]
    x2d = x[0]
    tgt = loss_target[0]
    xi, yi, ci = lax.axis_index("x"), lax.axis_index("y"), lax.axis_index("c")
    chip = 2 * xi + yi
    me = 2 * chip + ci
    tile = min(256, s_len)
    nt = s_len // tile

    local2d = {k: w[k][0] for k, _ in BIG + CONVS}
    kinds = dict(BIG)
    halves_bf = {k: local2d[k].astype(BF16).reshape(2, local2d[k].shape[0] // 2, local2d[k].shape[1]) for k, _ in BIG}
    first_names = [k for k, _ in BIG if k in FIRST_USED]
    later_names = [k for k, _ in BIG if k not in FIRST_USED]
    full = {}

    def assemble(k, g):
        g = lax.dynamic_update_index_in_dim(g, halves_bf[k][None], chip, 0).reshape((4,) + local2d[k].shape)
        if k == "w_up":
            full["w_up_gate"], full["w_up_val"] = _unshard(g[:2], kinds[k]), _unshard(g[2:], kinds[k])
        else:
            full[k] = _unshard(g, kinds[k])

    first_got = _gather_weights([halves_bf[k] for k in first_names])
    for k, g in zip(first_names, first_got):
        assemble(k, g)
    conv_flat = jnp.concatenate([local2d[k].reshape(-1) for k, _ in CONVS])
    conv_rows = -(-conv_flat.shape[0] // PACK_COLS)
    conv_all = _all_gather("gather_conv_w", _pad_rows(conv_flat[None], conv_rows)[0], CHIPS)
    conv_all = conv_all.reshape(4, -1)
    off = 0
    for k, kind in CONVS:
        r, cc = local2d[k].shape
        full[k] = _unshard(conv_all[:, off:off + r * cc].reshape(4, r, cc), kind)
        off += r * cc

    d_rnn = w["conv_b"].shape[1]
    n_q, n_kv = w["q_norm_g"].shape[1], w["kv_norm_g"].shape[1]
    w_in = full["w_in"]
    o1, o2, o3 = d_rnn + n_q, d_rnn + n_q + n_kv, d_rnn + n_q + n_kv + QK_ROPE
    w_rnn = w_in[:, :d_rnn]
    zpad = lambda n: jnp.zeros((d, n), BF16)
    w_qkv = jnp.concatenate([w_in[:, d_rnn:o2], zpad(QK_NOPE), w_in[:, o2:o3], zpad(LANE - QK_NOPE - QK_ROPE)], axis=1)
    w_g = w_in[:, o3:]
    hd = QK_NOPE + QK_ROPE
    w_uq = jnp.pad(full["w_uq"].reshape(n_q, N_HEADS, hd), ((0, 0), (0, 0), (0, HEAD_PAD - hd))).reshape(n_q, -1)
    w_ukv = full["w_ukv"]
    v_head = w_ukv.shape[1] // N_HEADS - QK_NOPE
    d_ff = w["ffn_conv_b"].shape[1] // 2
    ffn_cw_gate, ffn_cw_val = full["ffn_conv_w"][:, :d_ff], full["ffn_conv_w"][:, d_ff:]
    ffn_cb_gate, ffn_cb_val = w["ffn_conv_b"][:, :d_ff], w["ffn_conv_b"][:, d_ff:]
    conv_w, conv_b = full["conv_w"], w["conv_b"]
    wa_bd = _block_diag_pairs(w["w_gate_a"][0])
    wx_bd = _block_diag_pairs(w["w_gate_x"][0])

    c_all = _all_gather("gather_c", c, ALL7).reshape(8, d)
    c_rows = 128
    (c_act,) = _tiled("silu_c", lambda v: (_silu(v),), 1, [(jnp.pad(c_all, ((0, c_rows - 8), (0, 0))), (c_rows, d), "full")],
                      [((c_rows, d), F32, (c_rows, d), "full")])
    w_ada = w["w_ada"][0]
    n_mod = w_ada.shape[1]
    b_loc = lax.dynamic_slice_in_dim(w["b_ada"], chip * n_mod, n_mod, axis=1)
    mod_loc = _mm("ada_fwd", c_act, w_ada, add=jnp.broadcast_to(b_loc, (c_rows, n_mod)))
    mod_all = _all_gather("gather_mod", mod_loc[:8], CHIPS)
    mod = lax.dynamic_index_in_dim(mod_all, me, 1, keepdims=False).reshape(1, -1)
    shift1, scale1, gate1, shift2, scale2, gate2 = [mod[:, i * d:(i + 1) * d] for i in range(6)]

    small_done = (mod[:, :1] + conv_all[:1, :1] + first_got[0][0, 0, :1, :1].astype(F32))
    later_flight = _split_start(
        "gather_later_start", [halves_bf[k] for k in later_names],
        [jax.ShapeDtypeStruct((4,) + halves_bf[k].shape, BF16) for k in later_names], "gather", after=small_done)

    half = QK_ROPE // 2
    inv_freq = ROPE_THETA ** (-jnp.arange(half, dtype=F32) / half)
    ang = positions[0].astype(F32)[:, None] * inv_freq
    cos, sin = jnp.cos(ang), jnp.sin(ang)
    one, zero = jnp.ones((s_len, QK_NOPE), F32), jnp.zeros((s_len, half), F32)
    tail = jnp.zeros((s_len, LANE - QK_NOPE - QK_ROPE), F32)
    cos_f = jnp.concatenate([one, cos, cos, tail + 1.0], axis=1)
    sin_a = jnp.concatenate([one * 0.0, -sin, zero, tail], axis=1)
    sin_b = jnp.concatenate([one * 0.0, zero, sin, tail], axis=1)
    reset = (positions[0] == 0).astype(F32)[:, None]
    tabs = [(cos_f, (tile, LANE), "row"), (sin_a, (tile, LANE), "row"), (sin_b, (tile, LANE), "row")]

    def rowspec(a):
        return (a, (tile, a.shape[1]), "row")

    def full2(a):
        return (a, a.shape, "full")

    def rowout(cols, dt):
        return ((s_len, cols), dt, (tile, cols), "row")

    def accout(a):
        return (a.shape, F32, a.shape, "acc")

    norm1_g = w["norm1_g"] + later_flight[4][:1, :1]
    norm2_g, final_g = w["norm2_g"], w["final_g"].reshape(1, d)
    ln1_in = [rowspec(x2d), full2(norm1_g), full2(scale1), full2(shift1)]
    (h1,) = _tiled("ln1", _f_ln, nt, ln1_in, [rowout(d, BF16)])
    x_rnn = _mm("in_rnn", h1, w_rnn, out_dtype=BF16)
    qkv = _mm("in_qkv", h1, w_qkv)
    gates = _mm("in_gates", h1, w_g, out_dtype=BF16)

    ct = LANE
    n_ct = d_rnn // ct
    colspec = lambda a, width=ct: (a, (a.shape[0], width), "col")
    lru_in = [colspec(x_rnn), colspec(conv_w), colspec(conv_b), colspec(wa_bd), colspec(w["b_gate_a"]),
              colspec(wx_bd), colspec(w["b_gate_x"]), colspec(w["lru_param"]), full2(reset)]
    y_rnn, h_rnn = _tiled("lru_fwd", _f_lru_fwd, n_ct, lru_in,
                          [((s_len, d_rnn), BF16, (s_len, ct), "col"), ((s_len, d_rnn), F32, (s_len, ct), "col")])

    qkv_in = [rowspec(qkv)] + tabs + [full2(w["q_norm_g"]), full2(w["kv_norm_g"])]
    qn, kvn, kr = _tiled("qkv_norm", _f_qkv, nt, qkv_in, [rowout(n_q, BF16), rowout(n_kv, BF16), rowout(LANE, BF16)])
    q_pre = _mm("up_q", qn, w_uq, out_dtype=BF16)
    kv = _mm("up_kv", kvn, w_ukv, out_dtype=BF16)
    (q_cat,) = _tiled("rot_q", _f_rotq, nt, [rowspec(q_pre)] + tabs, [rowout(q_pre.shape[1], BF16)])
    o_mla, lse = _attn_fwd(q_cat, kv, kr)

    send_sems, recv_sems, flown, landed, _ = later_flight
    landed = _split_wait("gather_later_wait", send_sems, recv_sems, flown, landed, "gather", after=o_mla)
    for k, g in zip(later_names, _relay_sibling(landed)):
        assemble(k, g)
    w_pr = full["w_proj_rnn"]
    w_pm = jnp.pad(full["w_proj_mla"].reshape(N_HEADS, v_head, d), ((0, 0), (HEAD_PAD - v_head, 0), (0, 0))).reshape(-1, d)
    w_out = full["w_out"]
    w_up_gate, w_up_val = full["w_up_gate"], full["w_up_val"]
    w_down = full["w_down"]

    p_rnn = _mm("proj_rnn", y_rnn, w_pr, out_dtype=BF16)
    p_mla = _mm("proj_mla", o_mla, w_pm, out_dtype=BF16)
    merge_in = [rowspec(gates), rowspec(p_rnn), rowspec(p_mla)]
    (merged,) = _tiled("merge", _f_merge, nt, merge_in, [rowout(d, BF16)])
    o_tok = _mm("out_proj", merged, w_out)
    res_in = [rowspec(x2d), rowspec(o_tok), full2(gate1), full2(norm2_g), full2(scale2), full2(shift2)]
    x1, h2 = _tiled("res_ln2", _f_res_ln, nt, res_in, [rowout(d, F32), rowout(d, BF16)])
    u_gate = _mm("ffn_up_gate", h2, w_up_gate, out_dtype=BF16)
    u_val = _mm("ffn_up_val", h2, w_up_val, out_dtype=BF16)
    n_ft = d_ff // LANE
    ffn_in = [colspec(a) for a in (u_gate, u_val, ffn_cw_gate, ffn_cw_val, ffn_cb_gate, ffn_cb_val)]
    (act,) = _tiled("ffn_conv", _f_ffn, n_ft, ffn_in, [((s_len, d_ff), BF16, (s_len, LANE), "col")])
    f_tok = _mm("ffn_down", act, w_down)

    loss_in = [rowspec(x1), rowspec(f_tok), rowspec(tgt), full2(gate2), full2(final_g)]
    dx1, df, loss_row, d_gate2, d_final_g = _tiled(
        "loss", _f_loss_and_grads, nt, loss_in,
        [rowout(d, F32), rowout(d, BF16), ((1, LANE), F32, (1, LANE), "acc"), accout(gate2), accout(final_g)])
    loss = lax.psum(loss_row[0, 0], ("x", "y", "c"))

    d_act = _mm("ffn_down_dx", df, w_down, tb=True, out_dtype=BF16)
    g_w_down = _mm("ffn_down_dw", act, df, ta=True)
    taps = ffn_cw_gate.shape[0]
    du_gate, du_val, g_cw_gate, g_cw_val, g_cb_gate, g_cb_val = _tiled(
        "ffn_conv_bwd", _vjp_of(_f_ffn, 6, (0, 1, 2, 3, 4, 5)), n_ft, ffn_in + [colspec(d_act)],
        [((s_len, d_ff), BF16, (s_len, LANE), "col")] * 2 + [((taps, d_ff), F32, (taps, LANE), "col")] * 2
        + [((1, d_ff), F32, (1, LANE), "col")] * 2)
    dh2 = _mm("ffn_up_gate_dx", du_gate, w_up_gate, tb=True)
    dh2 = _mm("ffn_up_val_dx", du_val, w_up_val, tb=True, add=dh2, out_dtype=BF16)
    g_w_up_halves = [_mm("ffn_up_gate_dw", h2, du_gate, ta=True), _mm("ffn_up_val_dw", h2, du_val, ta=True)]
    g_ffn_cw = jnp.concatenate([g_cw_gate, g_cw_val], axis=1)
    g_ffn_cb = jnp.concatenate([g_cb_gate, g_cb_val], axis=1)

    res_bwd = _vjp_of(_f_res_ln, 6, (0, 1, 2, 3, 4, 5))
    dx_res, do_tok, d_gate1, g_norm2, d_scale2, d_shift2 = _tiled(
        "res_ln2_bwd", res_bwd, nt, res_in + [rowspec(dx1), rowspec(dh2)],
        [rowout(d, F32), rowout(d, BF16), accout(gate1), accout(norm2_g), accout(scale2), accout(shift2)])
    d_merged = _mm("out_proj_dx", do_tok, w_out, tb=True, out_dtype=BF16)
    g_w_out = _mm("out_proj_dw", merged, do_tok, ta=True)
    d_gates, dp_rnn, dp_mla = _tiled(
        "merge_bwd", _f_merge_bwd, nt, merge_in + [rowspec(d_merged)],
        [rowout(gates.shape[1], BF16), rowout(d, BF16), rowout(d, BF16)])
    dy_rnn = _mm("proj_rnn_dx", dp_rnn, w_pr, tb=True, out_dtype=BF16)
    g_w_pr = _mm("proj_rnn_dw", y_rnn, dp_rnn, ta=True)
    do_mla = _mm("proj_mla_dx", dp_mla, w_pm, tb=True, out_dtype=BF16)
    g_w_pm = _mm("proj_mla_dw", o_mla, dp_mla, ta=True)

    def chunked(k, gk):
        r, cc = local2d[k].shape
        if kinds[k] == "col":
            gk = gk.reshape(r, 4, cc).transpose(1, 0, 2)
        return gk.reshape(4, 2, r // 2, cc)

    def pair_sums(tag, names, chunks):
        out = []
        core = ci.astype(jnp.int32).reshape(1)
        for k, ck, from_sib in zip(names, chunks, _pair_exchange("reduce_pair_exchange_" + tag, chunks)):
            out.append(_reduce_pair("reduce_pair_" + k, ck, from_sib, core))
        return out

    g_later = {
        "w_proj_rnn": g_w_pr,
        "w_proj_mla": g_w_pm.reshape(N_HEADS, HEAD_PAD, d)[:, HEAD_PAD - v_head:, :].reshape(-1, d),
        "w_out": g_w_out, "w_down": g_w_down,
    }
    r_up, c_up = local2d["w_up"].shape
    up_chunks = jnp.concatenate([g.reshape(r_up, 2, c_up).transpose(1, 0, 2) for g in g_w_up_halves], axis=0)
    chunks_ready = [up_chunks.reshape(4, 2, r_up // 2, c_up) if k == "w_up" else chunked(k, g_later[k])
                    for k in later_names]
    sums_ready = pair_sums("ready", later_names, chunks_ready)
    ready_flight = _split_start(
        "reduce_ready_start", sums_ready, [jax.ShapeDtypeStruct(s.shape, s.dtype) for s in sums_ready], "alltoall",
        after=sums_ready[0])
    kr_held = kr + ready_flight[4][:1, :].astype(BF16)

    dq_cat, dkv, dkr = _attn_bwd(q_cat, kv, kr_held, o_mla, lse, do_mla)
    (dq_pre,) = _tiled("rot_q_bwd", _f_rotq_bwd, nt, [rowspec(q_pre)] + tabs + [rowspec(dq_cat)],
                       [rowout(q_pre.shape[1], BF16)])
    dqn = _mm("up_q_dx", dq_pre, w_uq, tb=True, out_dtype=BF16)
    g_w_uq = _mm("up_q_dw", qn, dq_pre, ta=True)
    dkvn = _mm("up_kv_dx", dkv, w_ukv, tb=True, out_dtype=BF16)
    g_w_ukv = _mm("up_kv_dw", kvn, dkv, ta=True)
    dqkv, g_q_norm, g_kv_norm = _tiled(
        "qkv_norm_bwd", _f_qkv_bwd, nt, qkv_in + [rowspec(dqn), rowspec(dkvn), rowspec(dkr)],
        [rowout(qkv.shape[1], BF16), accout(w["q_norm_g"]), accout(w["kv_norm_g"])])

    lru_out = [((s_len, d_rnn), BF16, (s_len, ct), "col")]
    for a in (conv_w, conv_b, wa_bd, w["b_gate_a"], wx_bd, w["b_gate_x"], w["lru_param"]):
        lru_out.append((a.shape, F32, (a.shape[0], ct), "col"))
    dx_rnn, g_conv_w, g_conv_b, g_wa_bd, g_b_a, g_wx_bd, g_b_x, g_lru = _tiled(
        "lru_bwd", _f_lru_bwd, n_ct, lru_in + [colspec(h_rnn), colspec(dy_rnn)], lru_out)

    dh1 = _mm("in_gates_dx", d_gates, w_g, tb=True)
    dh1 = _mm("in_qkv_dx", dqkv, w_qkv, tb=True, add=dh1)
    dh1 = _mm("in_rnn_dx", dx_rnn, w_rnn, tb=True, add=dh1)
    g_w_rnn = _mm("in_rnn_dw", h1, dx_rnn, ta=True)
    g_w_qkv = _mm("in_qkv_dw", h1, dqkv, ta=True)
    g_w_g = _mm("in_gates_dw", h1, d_gates, ta=True)

    ln_bwd = _vjp_of(_f_ln, 4, (0, 1, 2, 3))

    def ln1_bwd(xv, gv, sc, sh, dxr, dh):
        dx, dg, dsc, dsh = ln_bwd(xv, gv, sc, sh, dh)
        return dx + dxr, dg, dsc, dsh

    grad_x, g_norm1, d_scale1, d_shift1 = _tiled(
        "ln1_bwd", ln1_bwd, nt, ln1_in + [rowspec(dx_res), rowspec(dh1)],
        [rowout(d, F32), accout(norm1_g), accout(scale1), accout(shift1)])

    dmod = jnp.concatenate([d_shift1, d_scale1, d_gate1, d_shift2, d_scale2, d_gate2], axis=1)
    dmod_all = _all_gather("gather_dmod", dmod, ALL7).reshape(8, -1)
    dmod_loc = lax.dynamic_slice_in_dim(dmod_all, chip * n_mod, n_mod, axis=1)
    g_w_ada = _mm("ada_dw", c_act, jnp.pad(dmod_loc, ((0, c_rows - 8), (0, 0))), ta=True)

    g_full = {
        "w_in": jnp.concatenate([g_w_rnn, g_w_qkv[:, :n_q + n_kv],
                                 g_w_qkv[:, n_q + n_kv + QK_NOPE:n_q + n_kv + QK_NOPE + QK_ROPE], g_w_g], axis=1),
        "w_uq": g_w_uq.reshape(n_q, N_HEADS, HEAD_PAD)[:, :, :hd].reshape(n_q, -1),
        "w_ukv": g_w_ukv,
        "conv_w": g_conv_w,
        "ffn_conv_w": g_ffn_cw,
    }
    g_small = {
        "b_ada": dmod, "norm1_g": g_norm1, "conv_b": g_conv_b,
        "w_gate_a": _block_diag_pairs_t(g_wa_bd)[None], "b_gate_a": g_b_a,
        "w_gate_x": _block_diag_pairs_t(g_wx_bd)[None], "b_gate_x": g_b_x, "lru_param": g_lru,
        "q_norm_g": g_q_norm, "kv_norm_g": g_kv_norm, "norm2_g": g_norm2,
        "ffn_conv_b": g_ffn_cb, "final_g": d_final_g.reshape(w["final_g"].shape),
    }

    small_flat = jnp.concatenate([g_small[k].reshape(-1) for k in SMALL] + [g_full[k].reshape(-1) for k, _ in CONVS])
    small_rows = -(-small_flat.shape[0] // (8 * PACK_COLS * PACK_ROW_UNIT)) * PACK_ROW_UNIT
    last_names = first_names + ["small"]
    last_chunks = [chunked(k, g_full[k]) for k in first_names]
    last_chunks.append(_pad_rows(small_flat[None], 8 * small_rows).reshape(4, 2, small_rows, PACK_COLS))
    sums_last = pair_sums("last", last_names, last_chunks)
    send_sems, recv_sems, flown, landed, _ = ready_flight
    quads_ready = _split_wait("reduce_ready_wait", send_sems, recv_sems, flown, landed, "alltoall", after=grad_x)
    last_flight = _split_start(
        "reduce_last_start", sums_last, [jax.ShapeDtypeStruct(s.shape, s.dtype) for s in sums_last], "alltoall",
        after=quads_ready[0])
    grads = {"w_ada": g_w_ada[None]}
    delta, new_m, new_v = {}, {}, {}

    def adamw(k):
        shp = w[k].shape
        flip = len(shp) == 3 and shp[-1] % LANE != 0 and shp[-2] % LANE == 0
        view = (lambda a: jnp.swapaxes(a, 1, 2)) if flip else (lambda a: a)
        two_d = (-1, view(w[k]).shape[-1]) if len(shp) > 1 else (1, -1)
        dk, mk, vk = _adamw("adamw_" + k, *[view(a).reshape(two_d) for a in (w[k], grads[k], m_in[k], v_in[k])])
        back = lambda a: view(a.reshape(view(w[k]).shape))
        delta[k], new_m[k], new_v[k] = back(dk), back(mk), back(vk)

    def finish(tag, names, quads, sums, after):
        reduced = {}
        for k, quad, ps in zip(names, quads, sums):
            quad = lax.dynamic_update_index_in_dim(quad, lax.dynamic_index_in_dim(ps, chip, 0, keepdims=True), chip, 0)
            reduced[k] = _reduce_quad("reduce_quad_" + k, quad, after)
        big = [k for k in names if k != "small"]
        for k, both in zip(big, _share_sibling("share_sibling_" + tag, [reduced[k] for k in big])):
            grads[k] = lax.dynamic_update_index_in_dim(both, reduced[k][None], ci, 0).reshape(w[k].shape)
        return reduced

    finish("ready", later_names, quads_ready, sums_ready, after=last_flight[4])
    for k in later_names + ["w_ada"]:
        adamw(k)
    send_sems, recv_sems, flown, landed, _ = last_flight
    quads_last = _split_wait("reduce_last_wait", send_sems, recv_sems, flown, landed, "alltoall",
                             after=delta[later_names[-1]])
    reduced = finish("last", last_names, quads_last, sums_last, after=None)
    small_grad = _all_gather("share_small", reduced["small"], ALL7).reshape(-1)
    off = 0
    for k in SMALL:
        grads[k] = small_grad[off:off + w[k].size].reshape(w[k].shape)
        off += w[k].size
    for k, _ in CONVS:
        r, cc = local2d[k].shape
        whole = small_grad[off:off + 4 * r * cc].reshape(r, 4 * cc)
        grads[k] = lax.dynamic_slice_in_dim(whole, chip * cc, cc, axis=1)[None]
        off += 4 * r * cc
    for k in WEIGHTS:
        if k not in delta:
            adamw(k)

    return (loss, grad_x[None], *[grads[k] for k in WEIGHTS], *[delta[k] for k in WEIGHTS],
            *[new_m[k] for k in WEIGHTS], *[new_v[k] for k in WEIGHTS])


def kernel(x, c, positions, w_ada, b_ada, norm1_g, w_in, conv_w, conv_b, w_gate_a, b_gate_a, w_gate_x, b_gate_x, lru_param, q_norm_g, w_uq, kv_norm_g, w_ukv, w_proj_rnn, w_proj_mla, w_out, norm2_g, w_up, ffn_conv_w, ffn_conv_b, w_down, final_g, loss_target, m_w_ada, m_b_ada, m_norm1_g, m_w_in, m_conv_w, m_conv_b, m_w_gate_a, m_b_gate_a, m_w_gate_x, m_b_gate_x, m_lru_param, m_q_norm_g, m_w_uq, m_kv_norm_g, m_w_ukv, m_w_proj_rnn, m_w_proj_mla, m_w_out, m_norm2_g, m_w_up, m_ffn_conv_w, m_ffn_conv_b, m_w_down, m_final_g, v_w_ada, v_b_ada, v_norm1_g, v_w_in, v_conv_w, v_conv_b, v_w_gate_a, v_b_gate_a, v_w_gate_x, v_b_gate_x, v_lru_param, v_q_norm_g, v_w_uq, v_kv_norm_g, v_w_ukv, v_w_proj_rnn, v_w_proj_mla, v_w_out, v_norm2_g, v_w_up, v_ffn_conv_w, v_ffn_conv_b, v_w_down, v_final_g):
    given = dict(locals())
    w = {k: given[k] for k in WEIGHTS}
    m_in = {k: given["m_" + k] for k in WEIGHTS}
    v_in = {k: given["v_" + k] for k in WEIGHTS}
    return _step(x, c, positions, w, m_in, v_in, loss_target)
```

```python
import functools
import math

import jax
import jax.numpy as jnp
from jax import lax
from jax.experimental import pallas as pl
from jax.experimental.pallas import tpu as pltpu

F32 = jnp.float32
BF16 = jnp.bfloat16

EPS = 1e-6
LRU_C = 8.0
N_HEADS = 16
QK_NOPE = 64
QK_ROPE = 32
HEAD_PAD = 128
ROPE_THETA = 10000.0
ADAM_LR = 0.001
ADAM_B1 = 0.9
ADAM_B2 = 0.999
ADAM_EPS = 1e-08
ADAM_WD = 0.01
ADAM_STEP = 10

LANE = 128
SUBLANES = 8
VMEM_LIMIT = 48 * 1024 * 1024
MM_TILE_M = MM_TILE_N = MM_TILE_K = 1408
PACK_COLS = 1024
PACK_ROW_UNIT = 32
MESH = pl.DeviceIdType.MESH

NN = (((1,), (0,)), ((), ()))
NT = (((1,), (1,)), ((), ()))
TN = (((0,), (0,)), ((), ()))


def _cparams(sem):
    return pltpu.CompilerParams(dimension_semantics=sem, vmem_limit_bytes=VMEM_LIMIT)


def _div_tile(n, cap, unit):
    best = None
    d = unit
    while d <= min(n, cap):
        if n % d == 0:
            best = d
        d += unit
    return n if best is None else best


def _mm(name, a, b, *, ta=False, tb=False, add=None, out_dtype=F32):
    if ta:
        kdim, m = a.shape
    else:
        m, kdim = a.shape
    if tb:
        n, kb = b.shape
    else:
        kb, n = b.shape
    assert kdim == kb, (name, a.shape, b.shape)
    tm = _div_tile(m, MM_TILE_M, 8 if not ta else LANE)
    tn = _div_tile(n, MM_TILE_N, LANE)
    tk = _div_tile(kdim, MM_TILE_K, LANE)
    nk = kdim // tk
    a_spec = pl.BlockSpec((tk, tm), lambda i, j, k: (k, i)) if ta else pl.BlockSpec((tm, tk), lambda i, j, k: (i, k))
    b_spec = pl.BlockSpec((tn, tk), lambda i, j, k: (j, k)) if tb else pl.BlockSpec((tk, tn), lambda i, j, k: (k, j))
    o_spec = pl.BlockSpec((tm, tn), lambda i, j, k: (i, j))
    has_add = add is not None
    dims = ((((0,) if ta else (1,)), ((1,) if tb else (0,))), ((), ()))

    def body(*refs):
        a_ref, b_ref = refs[0], refs[1]
        c_ref = refs[2] if has_add else None
        o_ref = refs[3] if has_add else refs[2]
        prod = lax.dot_general(a_ref[...].astype(BF16), b_ref[...].astype(BF16), dims, preferred_element_type=F32)
        if nk == 1:
            o_ref[...] = (prod + c_ref[...].astype(F32) if has_add else prod).astype(o_ref.dtype)
            return
        acc = refs[-1]
        k = pl.program_id(2)

        @pl.when(k == 0)
        def _():
            acc[...] = prod + c_ref[...].astype(F32) if has_add else prod

        @pl.when(jnp.logical_and(k > 0, k < nk - 1))
        def _():
            acc[...] += prod

        @pl.when(k == nk - 1)
        def _():
            o_ref[...] = (acc[...] + prod).astype(o_ref.dtype)

    ins = [a, b] + ([add] if has_add else [])
    specs = [a_spec, b_spec] + ([o_spec] if has_add else [])
    return pl.pallas_call(
        body, name=name, grid=(m // tm, n // tn, nk), in_specs=specs, out_specs=o_spec,
        out_shape=jax.ShapeDtypeStruct((m, n), out_dtype),
        scratch_shapes=[pltpu.VMEM((tm, tn), F32)] if nk > 1 else [],
        compiler_params=_cparams(("parallel", "parallel", "arbitrary")),
    )(*ins)


_IMAPS = {
    "row": lambda i: (i, 0),
    "col": lambda i: (0, i),
    "full": lambda i: (0, 0),
    "acc": lambda i: (0, 0),
}


def _tiled(name, fn, n, ins, outs, row_tile=None):
    if row_tile is not None:
        rows = next(a.shape[0] for a, _, k in ins if k == "row")
        n = rows // row_tile
        ins = [(a, (row_tile, bs[1]) if k == "row" else bs, k) for a, bs, k in ins]
        outs = [(s, dt, (row_tile, bs[1]) if k == "row" else bs, k) for s, dt, bs, k in outs]
    ni = len(ins)
    is_acc = [k == "acc" for *_, k in outs]

    def body(*refs):
        vals = fn(*[r[...].astype(F32) if r.dtype == BF16 else r[...] for r in refs[:ni]])
        orefs = refs[ni:]
        if any(is_acc):
            @pl.when(pl.program_id(0) == 0)
            def _():
                for r, a in zip(orefs, is_acc):
                    if a:
                        r[...] = jnp.zeros(r.shape, r.dtype)
        for r, v, a in zip(orefs, vals, is_acc):
            if a:
                r[...] += v.astype(r.dtype)
            else:
                r[...] = v.astype(r.dtype)

    res = pl.pallas_call(
        body, name=name, grid=(n,),
        in_specs=[pl.BlockSpec(bs, _IMAPS[k]) for _, bs, k in ins],
        out_specs=[pl.BlockSpec(bs, _IMAPS[k]) for _, _, bs, k in outs],
        out_shape=[jax.ShapeDtypeStruct(s, d) for s, d, _, _ in outs],
        compiler_params=_cparams(("arbitrary",)),
    )(*[a for a, _, _ in ins])
    return tuple(res)


def _vjp_of(fn, nin, diff):
    def g(*args):
        ins, cots = args[:nin], args[nin:]

        def f(*d):
            full = list(ins)
            for i, v in zip(diff, d):
                full[i] = v
            return fn(*full)

        outs, vjp = jax.vjp(f, *[ins[i] for i in diff])
        return vjp(tuple(c.astype(o.dtype) for c, o in zip(cots, outs)))
    return g


def _shift_rows(x, k, fill, up=False):
    n = x.shape[0]
    if k % SUBLANES == 0:
        pad = jnp.full((k,) + x.shape[1:], fill, x.dtype)
        return jnp.concatenate([x[k:], pad], axis=0) if up else jnp.concatenate([pad, x[:n - k]], axis=0)
    rows = lax.broadcasted_iota(jnp.int32, x.shape, 0)
    if up:
        return jnp.where(rows < n - k, pltpu.roll(x, n - k, 0), fill)
    return jnp.where(rows >= k, pltpu.roll(x, k, 0), fill)


@functools.partial(jax.custom_vjp, nondiff_argnums=(1,))
def _delay(x, k):
    return _shift_rows(x, k, 0.0)


def _delay_fwd(x, k):
    return _shift_rows(x, k, 0.0), None


def _delay_bwd(k, _, g):
    return (_shift_rows(g, k, 0.0, up=True),)


_delay.defvjp(_delay_fwd, _delay_bwd)


@functools.partial(jax.custom_vjp, nondiff_argnums=(1,))
def _lane_roll(x, s):
    return pltpu.roll(x, s, 1)


def _lane_roll_fwd(x, s):
    return pltpu.roll(x, s, 1), None


def _lane_roll_bwd(s, _, g):
    return (pltpu.roll(g, g.shape[1] - s, 1),)


_lane_roll.defvjp(_lane_roll_fwd, _lane_roll_bwd)


@jax.custom_vjp
def _bdot(x, w):
    return lax.dot_general(x.astype(BF16), w.astype(BF16), NN, preferred_element_type=F32)


def _bdot_fwd(x, w):
    return _bdot(x, w), (x, w)


def _bdot_bwd(res, g):
    x, w = res
    gb = g.astype(BF16)
    dx = lax.dot_general(gb, w.astype(BF16), NT, preferred_element_type=F32)
    dw = lax.dot_general(x.T.astype(BF16), gb, NN, preferred_element_type=F32)
    return dx, dw


_bdot.defvjp(_bdot_fwd, _bdot_bwd)


def _sigmoid(x):
    return 0.5 * (jnp.tanh(0.5 * x) + 1.0)


def _silu(x):
    return x * _sigmoid(x)


def _rms(x, g):
    return x * lax.rsqrt(jnp.mean(x * x, axis=-1, keepdims=True) + EPS) * g


def _causal_conv(x, w, b):
    kw = w.shape[0]
    tap = lax.broadcasted_iota(jnp.int32, w.shape, 0)
    y = b
    for k in range(kw):
        d = kw - 1 - k
        wk = jnp.sum(jnp.where(tap == k, w, 0.0), axis=0, keepdims=True)
        y = y + wk * (x if d == 0 else _delay(x, d))
    return y


def _rotate(x, cos_f, sin_a, sin_b):
    reps = x.shape[1] // LANE
    if reps > 1:
        cos_f, sin_a, sin_b = (jnp.tile(t, (1, reps)) for t in (cos_f, sin_a, sin_b))
    n = x.shape[1]
    half = QK_ROPE // 2
    return x * cos_f + _lane_roll(x, n - half) * sin_a + _lane_roll(x, half) * sin_b


def _softplus_neg(l):
    u = jnp.exp(-jnp.abs(l))
    log1p_u = jnp.where(u < 0.01, u * (1.0 - u * (0.5 - u * (1.0 / 3.0))), jnp.log(1.0 + u))
    return jnp.maximum(-l, 0.0) + log1p_u


def _f_ln(x, g, scale, shift):
    return (_rms(x, g) * (1.0 + scale) + shift,)


def _f_qkv(qkv, cos_f, sin_a, sin_b, qg, kvg):
    nq, nkv = qg.shape[1], kvg.shape[1]
    qn = _rms(qkv[:, :nq], qg)
    kvn = _rms(qkv[:, nq:nq + nkv], kvg)
    kr = _rotate(qkv[:, nq + nkv:], cos_f, sin_a, sin_b)
    return qn, kvn, kr


def _f_qkv_bwd(qkv, cos_f, sin_a, sin_b, qg, kvg, dqn, dkvn, dkr):
    nq, nkv = qg.shape[1], kvg.shape[1]
    _, vjp_q = jax.vjp(_rms, qkv[:, :nq], qg)
    _, vjp_kv = jax.vjp(_rms, qkv[:, nq:nq + nkv], kvg)
    _, vjp_r = jax.vjp(lambda t: _rotate(t, cos_f, sin_a, sin_b), qkv[:, nq + nkv:])
    dq_lat, dqg = vjp_q(dqn)
    dkv_lat, dkvg = vjp_kv(dkvn)
    (dkr_pre,) = vjp_r(dkr)
    return jnp.concatenate([dq_lat, dkv_lat, dkr_pre], axis=1), dqg, dkvg


QK_SCALE = 1.0 / math.sqrt(QK_NOPE + QK_ROPE)
LOG2_E = 1.4426950408889634
LN_2 = 0.6931471805599453


def _f_rotq(q, cos_f, sin_a, sin_b):
    return (_rotate(q, cos_f, sin_a, sin_b) * (QK_SCALE * LOG2_E),)


def _f_rotq_bwd(q, cos_f, sin_a, sin_b, dq):
    _, vjp = jax.vjp(lambda t: _rotate(t, cos_f, sin_a, sin_b) * QK_SCALE, q)
    return vjp(dq)


def _merge(g_rnn, g_mla, p_rnn, p_mla):
    return _sigmoid(g_rnn) * p_rnn + _sigmoid(g_mla) * p_mla


def _f_merge(g, p_rnn, p_mla):
    d = p_rnn.shape[1]
    return (_merge(g[:, :d], g[:, d:], p_rnn, p_mla),)


def _f_merge_bwd(g, p_rnn, p_mla, dm):
    d = p_rnn.shape[1]
    _, vjp = jax.vjp(_merge, g[:, :d], g[:, d:], p_rnn, p_mla)
    dg_rnn, dg_mla, dp_rnn, dp_mla = vjp(dm)
    return jnp.concatenate([dg_rnn, dg_mla], axis=1), dp_rnn, dp_mla


def _f_res_ln(x, o, gate, g2, scale, shift):
    x1 = x + gate * o
    return x1, _rms(x1, g2) * (1.0 + scale) + shift


def _f_ffn(u_gate, u_val, cw_gate, cw_val, cb_gate, cb_val):
    return (_silu(_causal_conv(u_gate, cw_gate, cb_gate)) * _causal_conv(u_val, cw_val, cb_val),)


def _f_loss(x1, f, tgt, gate, fg):
    y = _rms(x1 + gate * f, fg)
    err = (y - tgt) * (y - tgt)
    return 0.5 * jnp.sum(jnp.mean(err, axis=-1, keepdims=True), axis=0, keepdims=True)


def _f_loss_and_grads(x1, f, tgt, gate, fg):
    loss, vjp = jax.vjp(lambda a, b, c, d: _f_loss(a, b, tgt, c, d), x1, f, gate, fg)
    dx1, df, dgate, dfg = vjp(jnp.ones((1, 1), F32))
    return dx1, df, jnp.broadcast_to(loss, (1, LANE)), dgate, dfg


@jax.custom_vjp
def _decay_and_gain(log_a):
    a = jnp.exp(log_a)
    return a, jnp.sqrt(-jnp.tanh(log_a) * (1.0 + a * a))


def _decay_and_gain_fwd(log_a):
    a, gain = _decay_and_gain(log_a)
    return (a, gain), (a, gain)


def _decay_and_gain_bwd(res, g):
    a, gain = res
    return (g[0] * a - g[1] * (a * a) / gain,)


_decay_and_gain.defvjp(_decay_and_gain_fwd, _decay_and_gain_bwd)


def _f_lru_coeffs(xr, cw, cb, wa, ba, wx, bx, lru, reset):
    xc = _causal_conv(xr, cw, cb)
    r = _sigmoid(_bdot(xc, wa) + ba)
    i = _sigmoid(_bdot(xc, wx) + bx)
    log_a = (-LRU_C) * r * _softplus_neg(lru)
    a, mult = _decay_and_gain(log_a)
    is_reset = reset > 0.5
    a = jnp.where(is_reset, 0.0, a)
    mult = jnp.where(is_reset, 1.0, mult)
    return a, mult * (i * xc)


SCAN_BLOCK = 64


def _scan(a, b, up=False):
    n = a.shape[0]
    blk = min(SCAN_BLOCK, n)
    pos = lax.broadcasted_iota(jnp.int32, a.shape, 0) % blk
    k = 1
    while k < blk:
        inside = (pos < blk - k) if up else (pos >= k)
        shift = n - k if up else k
        b = b + a * jnp.where(inside, pltpu.roll(b, shift, 0), 0.0)
        a = a * jnp.where(inside, pltpu.roll(a, shift, 0), 1.0)
        k *= 2
    blocks = range(n // blk)
    carry = jnp.zeros((1,) + a.shape[1:], a.dtype)
    out = [None] * len(blocks)
    for i in (reversed(blocks) if up else blocks):
        rows = slice(i * blk, (i + 1) * blk)
        out[i] = b[rows] + a[rows] * carry
        carry = out[i][:1] if up else out[i][blk - 1:]
    return jnp.concatenate(out, axis=0)


def _f_lru_fwd(xr, cw, cb, wa, ba, wx, bx, lru, reset):
    a, b = _f_lru_coeffs(xr, cw, cb, wa, ba, wx, bx, lru, reset)
    h = _scan(a, b)
    return h, h


def _f_lru_bwd(xr, cw, cb, wa, ba, wx, bx, lru, reset, h, dh):
    (a, _), vjp = jax.vjp(lambda *p: _f_lru_coeffs(*p, reset), xr, cw, cb, wa, ba, wx, bx, lru)
    g = _scan(_shift_rows(a, 1, 0.0, up=True), dh, up=True)
    return vjp((g * _shift_rows(h, 1, 0.0), g))


def _attn_tile(s):
    return 1024 if s >= 2048 else s // 2


def _keys(kv, kr):
    lane = lax.broadcasted_iota(jnp.int32, kv.shape, 1)
    return jnp.where(lane < QK_NOPE, kv, kr)


ATTN_HEADS_PER_STEP = 2


def _scores(q, kc, diagonal):
    s = lax.dot_general(q, kc, NT, preferred_element_type=F32)
    if not diagonal:
        return s
    rows = lax.broadcasted_iota(jnp.int32, s.shape, 0)
    cols = lax.broadcasted_iota(jnp.int32, s.shape, 1)
    return jnp.where(cols - (s.shape[1] - s.shape[0]) <= rows, s, -jnp.inf)


def _sub_blocks(t, diagonal):
    return ((0, t // 2, t // 2), (t // 2, t // 2, t)) if diagonal else ((0, t, t),)


def _causal_pairs(nb, k_major):
    if k_major:
        pairs = [(qb, kb) for kb in range(nb) for qb in range(kb, nb)]
    else:
        pairs = [(qb, kb) for qb in range(nb) for kb in range(qb + 1)]
    return jnp.array([p[0] for p in pairs], jnp.int32), jnp.array([p[1] for p in pairs], jnp.int32)


def _attn_fwd(q, kv, kr):
    s_len = q.shape[0]
    t = _attn_tile(s_len)
    nb = s_len // t
    hp = ATTN_HEADS_PER_STEP
    wide = hp * HEAD_PAD
    q_tab, k_tab = _causal_pairs(nb, k_major=False)

    def body(qt, kt, q_ref, kv_ref, kr_ref, o_ref, lse_ref, m_s, acc_s):
        pair = pl.program_id(1)
        qi, ki = qt[pair], kt[pair]

        @pl.when(ki == 0)
        def _():
            m_s[...] = jnp.full(m_s.shape, -jnp.inf, F32)
            acc_s[...] = jnp.zeros(acc_s.shape, F32)

        def step(diagonal):
            for h in range(hp):
                lanes = slice(h * HEAD_PAD, (h + 1) * HEAD_PAD)
                for r0, nr, nk in _sub_blocks(t, diagonal):
                    rows = slice(r0, r0 + nr)
                    kvv = kv_ref[:nk, lanes]
                    s = _scores(q_ref[rows, lanes], _keys(kvv, kr_ref[:nk, :]), diagonal)
                    m_old = m_s[h, rows]
                    m_new = jnp.maximum(m_old, jnp.max(s, axis=-1, keepdims=True))
                    alpha = jnp.exp2(m_old - m_new)
                    p = jnp.exp2(s - m_new)
                    lane = lax.broadcasted_iota(jnp.int32, kvv.shape, 1)
                    ones_and_values = jnp.where(lane < QK_NOPE, jnp.ones_like(kvv), kvv)
                    acc_s[rows, lanes] = alpha * acc_s[rows, lanes] + lax.dot_general(
                        p.astype(BF16), ones_and_values, NN, preferred_element_type=F32)
                    m_s[h, rows] = m_new

        @pl.when(ki < qi)
        def _():
            step(False)

        @pl.when(ki == qi)
        def _():
            step(True)
            lane = lax.broadcasted_iota(jnp.int32, (t, HEAD_PAD), 1)
            outs = []
            for h in range(hp):
                acc = acc_s[:, h * HEAD_PAD:(h + 1) * HEAD_PAD]
                total = acc[:, :1]
                outs.append(acc / total)
                lse_ref[h] = m_s[h] + jnp.log(total) * LOG2_E
            o_ref[...] = jnp.where(lane >= QK_NOPE, outs[0], pltpu.roll(outs[1], QK_NOPE, 1)).astype(o_ref.dtype)

    grid_spec = pltpu.PrefetchScalarGridSpec(
        num_scalar_prefetch=2, grid=(N_HEADS // hp, q_tab.shape[0]),
        in_specs=[pl.BlockSpec((t, wide), lambda h, p, qt, kt: (qt[p], h)),
                  pl.BlockSpec((t, wide), lambda h, p, qt, kt: (kt[p], h)),
                  pl.BlockSpec((t, HEAD_PAD), lambda h, p, qt, kt: (kt[p], 0))],
        out_specs=[pl.BlockSpec((t, HEAD_PAD), lambda h, p, qt, kt: (qt[p], h)),
                   pl.BlockSpec((hp, t, 1), lambda h, p, qt, kt: (h, qt[p], 0))],
        scratch_shapes=[pltpu.VMEM((hp, t, 1), F32), pltpu.VMEM((t, wide), F32)])
    return pl.pallas_call(
        body, name="attn_fwd", grid_spec=grid_spec,
        out_shape=[jax.ShapeDtypeStruct((s_len, N_HEADS // hp * HEAD_PAD), BF16),
                   jax.ShapeDtypeStruct((N_HEADS, s_len, 1), F32)],
        compiler_params=_cparams(("arbitrary", "arbitrary")),
    )(q_tab, k_tab, q, kv, kr)


def _attn_bwd(q, kv, kr, o, lse, do):
    s_len = q.shape[0]
    t = _attn_tile(s_len)
    nb = s_len // t
    hp = ATTN_HEADS_PER_STEP
    wide = hp * HEAD_PAD
    q_tab, k_tab = _causal_pairs(nb, k_major=True)

    def body(qt, kt, q_ref, kv_ref, kr_ref, o_ref, lse_ref, do_ref, dq_ref, dkv_ref, dkr_ref, dk_s, dv_s):
        g, pair = pl.program_id(0), pl.program_id(1)
        qb, kb = qt[pair], kt[pair]

        @pl.when(jnp.logical_and(g == 0, pair == 0))
        def _():
            dkr_ref[...] = jnp.zeros(dkr_ref.shape, F32)

        @pl.when(pair == 0)
        def _():
            dq_ref[...] = jnp.zeros(dq_ref.shape, F32)

        @pl.when(qb == kb)
        def _():
            dk_s[...] = jnp.zeros(dk_s.shape, F32)
            dv_s[...] = jnp.zeros(dv_s.shape, F32)

        def step(diagonal):
            for h in range(hp):
                lanes = slice(h * HEAD_PAD, (h + 1) * HEAD_PAD)
                for r0, nr, nk in _sub_blocks(t, diagonal):
                    rows, keys = slice(r0, r0 + nr), slice(0, nk)
                    qv, kvv = q_ref[rows, lanes], kv_ref[keys, lanes]
                    pair_do = do_ref[rows, :].astype(F32)
                    lane = lax.broadcasted_iota(jnp.int32, pair_do.shape, 1)
                    mine = (lane >= QK_NOPE) if h == 0 else (lane < QK_NOPE)
                    placed = pair_do if h == 0 else pltpu.roll(pair_do, QK_NOPE, 1)
                    dov = jnp.where(lane >= QK_NOPE, placed, 0.0).astype(BF16)
                    delta = jnp.sum(jnp.where(mine, pair_do * o_ref[rows, :].astype(F32), 0.0), axis=-1, keepdims=True)
                    kc = _keys(kvv, kr_ref[keys, :])
                    p = jnp.exp2(_scores(qv, kc, diagonal) - lse_ref[h, rows])
                    dp = lax.dot_general(dov, kvv, NT, preferred_element_type=F32)
                    ds = p * (dp - delta)
                    dv_s[keys, lanes] += lax.dot_general(p.astype(BF16), dov, TN, preferred_element_type=F32)
                    dk_s[keys, lanes] += lax.dot_general(ds.astype(BF16), qv, TN, preferred_element_type=F32)
                    q_rows = pl.ds(pl.multiple_of(qb * t + r0, nr), nr)
                    dq_ref[q_rows, lanes] += lax.dot_general(ds.astype(BF16), kc, NN, preferred_element_type=F32)

        @pl.when(qb > kb)
        def _():
            step(False)

        @pl.when(qb == kb)
        def _():
            step(True)

        @pl.when(qb == nb - 1)
        def _():
            lane = lax.broadcasted_iota(jnp.int32, (t, HEAD_PAD), 1)
            rows = pl.ds(pl.multiple_of(kb * t, t), t)
            for h in range(hp):
                lanes = slice(h * HEAD_PAD, (h + 1) * HEAD_PAD)
                dk = dk_s[:, lanes] * LN_2
                dkv_ref[:, lanes] = jnp.where(lane < QK_NOPE, dk, dv_s[:, lanes]).astype(dkv_ref.dtype)
                dkr_ref[rows, :] += jnp.where(lane >= QK_NOPE, dk, 0.0)

    all_lanes = N_HEADS * HEAD_PAD
    qmap = lambda h, p, qt, kt: (qt[p], h)
    kmap = lambda h, p, qt, kt: (kt[p], h)
    grid_spec = pltpu.PrefetchScalarGridSpec(
        num_scalar_prefetch=2, grid=(N_HEADS // hp, q_tab.shape[0]),
        in_specs=[pl.BlockSpec((t, wide), qmap),
                  pl.BlockSpec((t, wide), kmap),
                  pl.BlockSpec((t, HEAD_PAD), lambda h, p, qt, kt: (kt[p], 0)),
                  pl.BlockSpec((t, HEAD_PAD), qmap),
                  pl.BlockSpec((hp, t, 1), lambda h, p, qt, kt: (h, qt[p], 0)),
                  pl.BlockSpec((t, HEAD_PAD), qmap)],
        out_specs=[pl.BlockSpec((s_len, wide), lambda h, p, qt, kt: (0, h)),
                   pl.BlockSpec((t, wide), kmap),
                   pl.BlockSpec((s_len, HEAD_PAD), lambda h, p, qt, kt: (0, 0))],
        scratch_shapes=[pltpu.VMEM((t, wide), F32), pltpu.VMEM((t, wide), F32)])
    return pl.pallas_call(
        body, name="attn_bwd", grid_spec=grid_spec,
        out_shape=[jax.ShapeDtypeStruct((s_len, all_lanes), F32),
                   jax.ShapeDtypeStruct((s_len, all_lanes), BF16),
                   jax.ShapeDtypeStruct((s_len, HEAD_PAD), F32)],
        compiler_params=_cparams(("arbitrary", "arbitrary")),
    )(q_tab, k_tab, q, kv, kr, o, lse, do)


def _adamw(name, w, g, m, v):
    rows, cols = w.shape
    tr = _div_tile(rows, max(8, (2 * 1024 * 1024) // (4 * cols)), 8)

    def body(w_ref, g_ref, m_ref, v_ref, d_ref, nm_ref, nv_ref):
        gv = g_ref[...]
        nm = ADAM_B1 * m_ref[...] + (1.0 - ADAM_B1) * gv
        nv = ADAM_B2 * v_ref[...] + (1.0 - ADAM_B2) * jnp.square(gv)
        m_hat = nm / (1.0 - ADAM_B1 ** ADAM_STEP)
        v_hat = nv / (1.0 - ADAM_B2 ** ADAM_STEP)
        d_ref[...] = -ADAM_LR * (m_hat / (jnp.sqrt(v_hat) + ADAM_EPS) + ADAM_WD * w_ref[...])
        nm_ref[...] = nm
        nv_ref[...] = nv

    spec = pl.BlockSpec((tr, cols), lambda i: (i, 0))
    return pl.pallas_call(
        body, name=name, grid=(rows // tr,), in_specs=[spec] * 4, out_specs=[spec] * 3,
        out_shape=[jax.ShapeDtypeStruct((rows, cols), F32)] * 3,
        compiler_params=_cparams(("parallel",)),
    )(w, g, m, v)


ALL7 = (1, 2, 3, 4, 5, 6, 7)
CHIPS = (2, 4, 6)


def _all_gather(name, src, masks):
    bits = 0
    for m in masks:
        bits |= m
    nslots = {7: 8, 6: 4}[bits]
    nm = len(masks)

    def slot_of(x, y, c):
        return {7: 4 * x + 2 * y + c, 6: 2 * x + y}[bits]

    def body(src_ref, out_ref, send_sems, recv_sems, local_sem):
        x, y, c = lax.axis_index("x"), lax.axis_index("y"), lax.axis_index("c")
        mine = slot_of(x, y, c)
        own = pltpu.make_async_copy(src_ref, out_ref.at[mine], local_sem)
        own.start()
        copies = []
        for i, m in enumerate(masks):
            peer = _peer(x, y, c, m)
            copies.append((
                pltpu.make_async_remote_copy(
                    src_ref=src_ref, dst_ref=out_ref.at[mine], send_sem=send_sems.at[i], recv_sem=recv_sems.at[i],
                    device_id=peer, device_id_type=MESH),
                pltpu.make_async_remote_copy(
                    src_ref=src_ref, dst_ref=out_ref.at[slot_of(*peer)], send_sem=send_sems.at[i],
                    recv_sem=recv_sems.at[i], device_id=peer, device_id_type=MESH)))
        for send, _ in copies:
            send.start()
        for _, arrival in copies:
            arrival.wait_recv()
        for send, _ in copies:
            send.wait_send()
        own.wait()

    return pl.pallas_call(
        body, name=name,
        in_specs=[pl.BlockSpec(memory_space=pl.ANY)], out_specs=pl.BlockSpec(memory_space=pl.ANY),
        out_shape=jax.ShapeDtypeStruct((nslots,) + tuple(src.shape), src.dtype),
        scratch_shapes=[pltpu.SemaphoreType.DMA((nm,)), pltpu.SemaphoreType.DMA((nm,)), pltpu.SemaphoreType.DMA],
    )(src)


def _peer(x, y, c, m):
    return (1 - x if m & 4 else x, 1 - y if m & 2 else y, 1 - c if m & 1 else c)


def _comm_call(name, emit, srcs, out_shapes, n_sems, in_place=False):
    n = len(srcs)

    def body(*refs):
        src_refs, out_refs = refs[:n], refs[n:n + len(out_shapes)]
        send_sems, recv_sems = refs[-2], refs[-1]

        def copy(src, dst, i, peer):
            return pltpu.make_async_remote_copy(src_ref=src, dst_ref=dst, send_sem=send_sems.at[i],
                                                recv_sem=recv_sems.at[i], device_id=peer, device_id_type=MESH)

        emit(lax.axis_index("x"), lax.axis_index("y"), lax.axis_index("c"), src_refs, out_refs, copy)

    hbm = pl.BlockSpec(memory_space=pl.ANY)
    return pl.pallas_call(
        body, name=name, in_specs=[hbm] * n, out_specs=[hbm] * len(out_shapes), out_shape=out_shapes,
        scratch_shapes=[pltpu.SemaphoreType.DMA((n_sems,)), pltpu.SemaphoreType.DMA((n_sems,))],
        input_output_aliases={i: i for i in range(n)} if in_place else {},
    )(*srcs)


HBM_SPEC = pl.BlockSpec(memory_space=pltpu.HBM)
SEM_SPEC = pl.BlockSpec(memory_space=pltpu.SEMAPHORE)
DATAFLOW = pltpu.SideEffectType.DATAFLOW_SIDE_EFFECTING


def _chip_copies(srcs, lands, send_sems, recv_sems, mode):
    x, y, c = lax.axis_index("x"), lax.axis_index("y"), lax.axis_index("c")
    chip = 2 * x + y
    sends, arrivals = [], []
    for j, m in enumerate(CHIPS):
        px, py, _ = _peer(x, y, c, m)
        theirs = 2 * px + py
        for k, (s, l) in enumerate(zip(srcs, lands)):
            if mode == "gather":
                src, dst, got = s.at[c], l.at[chip, c], l.at[theirs, c]
            else:
                src, dst, got = s.at[theirs], l.at[chip], l.at[theirs]
            for to, group in ((dst, sends), (got, arrivals)):
                group.append(pltpu.make_async_remote_copy(
                    src_ref=src, dst_ref=to, send_sem=send_sems.at[3 * k + j], recv_sem=recv_sems.at[3 * k + j],
                    device_id=(px, py, c), device_id_type=MESH))
    return sends, arrivals


def _split_start(name, srcs, land_shapes, mode, after):
    n = len(srcs)

    def body(*refs):
        sends, _ = _chip_copies(refs[:n], refs[n:2 * n], refs[2 * n + 1], refs[2 * n + 2], mode)
        for cp in sends:
            cp.start()
        token = refs[-1]
        token[...] = jnp.zeros(token.shape, token.dtype)

    hbm = lambda a: pltpu.with_memory_space_constraint(a, pltpu.HBM)
    lands = [hbm(lax.empty(s.shape, s.dtype)) for s in land_shapes]
    bufs = [pltpu.HBM(a.shape, a.dtype) for a in list(srcs) + lands]
    res = pl.pallas_call(
        body, name=name,
        out_shape=(pltpu.SemaphoreType.DMA((3 * n,)), pltpu.SemaphoreType.DMA((3 * n,)), *bufs,
                   jax.ShapeDtypeStruct((SUBLANES, LANE), F32)),
        in_specs=[HBM_SPEC] * (2 * n) + [pl.BlockSpec(memory_space=pl.ANY)],
        out_specs=[SEM_SPEC, SEM_SPEC] + [HBM_SPEC] * (2 * n) + [pl.BlockSpec(memory_space=pltpu.VMEM)],
        input_output_aliases={i: 2 + i for i in range(2 * n)},
        compiler_params=pltpu.CompilerParams(has_side_effects=DATAFLOW),
    )(*[hbm(s) for s in srcs], *lands, after)
    return res[0], res[1], res[2:2 + n], res[2 + n:2 + 2 * n], res[-1]


def _split_wait(name, send_sems, recv_sems, srcs, lands, mode, after):
    n = len(srcs)

    def body(*refs):
        sends, arrivals = _chip_copies(refs[:n], refs[n:2 * n], refs[2 * n], refs[2 * n + 1], mode)
        for cp in sends:
            cp.wait_send()
        for cp in arrivals:
            cp.wait_recv()

    res = pl.pallas_call(
        body, name=name,
        out_shape=tuple(pltpu.HBM(a.shape, a.dtype) for a in list(srcs) + list(lands)),
        in_specs=[HBM_SPEC] * (2 * n) + [SEM_SPEC, SEM_SPEC, pl.BlockSpec(memory_space=pl.ANY)],
        out_specs=[HBM_SPEC] * (2 * n),
        input_output_aliases={i: i for i in range(2 * n)},
        compiler_params=pltpu.CompilerParams(has_side_effects=DATAFLOW),
    )(*srcs, *lands, send_sems, recv_sems, after)
    return res[n:]


def _relay_sibling(lands):
    def emit(x, y, c, srcs, outs, copy):
        sib = (x, y, 1 - c)
        sends, arrivals = [], []
        for j, m in enumerate(CHIPS):
            px, py, _ = _peer(x, y, c, m)
            theirs = 2 * px + py
            for k, (s, o) in enumerate(zip(srcs, outs)):
                sends.append(copy(s.at[theirs, c], o.at[theirs, c], 3 * k + j, sib))
                arrivals.append(copy(s.at[theirs, c], o.at[theirs, 1 - c], 3 * k + j, sib))
        for cp in sends:
            cp.start()
        for cp in arrivals:
            cp.wait_recv()
        for cp in sends:
            cp.wait_send()

    shapes = [jax.ShapeDtypeStruct(l.shape, l.dtype) for l in lands]
    return _comm_call("relay_weights", emit, lands, shapes, 3 * len(lands), in_place=True)


def _gather_weights(halves):
    n = len(halves)

    def emit(x, y, c, srcs, outs, copy):
        chip = 2 * x + y
        sib = (x, y, 1 - c)
        first, relay, landed, relayed = [], [], [], []
        for j, m in enumerate(CHIPS):
            px, py, _ = _peer(x, y, c, m)
            theirs = 2 * px + py
            for k in range(n):
                i = 6 * k + j
                first.append(copy(srcs[k].at[c], outs[k].at[chip, c], i, (px, py, c)))
                landed.append(copy(srcs[k].at[c], outs[k].at[theirs, c], i, (px, py, c)))
                relay.append(copy(outs[k].at[theirs, c], outs[k].at[theirs, c], i + 3, sib))
                relayed.append(copy(outs[k].at[theirs, 1 - c], outs[k].at[theirs, 1 - c], i + 3, sib))
        for cp in first:
            cp.start()
        for arrival, onward in zip(landed, relay):
            arrival.wait_recv()
            onward.start()
        for arrival in relayed:
            arrival.wait_recv()
        for cp in first + relay:
            cp.wait_send()

    shapes = [jax.ShapeDtypeStruct((4,) + h.shape, h.dtype) for h in halves]
    return _comm_call("gather_weights", emit, halves, shapes, 6 * n)


def _pair_exchange(name, chunks):
    def emit(x, y, c, srcs, outs, copy):
        sib = (x, y, 1 - c)
        sends = [copy(s.at[:, 1 - c], o, k, sib) for k, (s, o) in enumerate(zip(srcs, outs))]
        for cp in sends:
            cp.start()
        for cp in sends:
            cp.wait_recv()
        for cp in sends:
            cp.wait_send()

    shapes = [jax.ShapeDtypeStruct((4,) + g.shape[2:], g.dtype) for g in chunks]
    return _comm_call(name, emit, chunks, shapes, len(chunks))


def _share_sibling(name, parts):
    def emit(x, y, c, srcs, outs, copy):
        sib = (x, y, 1 - c)
        sends = [copy(s, o.at[c], k, sib) for k, (s, o) in enumerate(zip(srcs, outs))]
        arrivals = [copy(s, o.at[1 - c], k, sib) for k, (s, o) in enumerate(zip(srcs, outs))]
        for cp in sends:
            cp.start()
        for cp in arrivals:
            cp.wait_recv()
        for cp in sends:
            cp.wait_send()

    shapes = [jax.ShapeDtypeStruct((2,) + p.shape, p.dtype) for p in parts]
    return _comm_call(name, emit, parts, shapes, len(parts))


def _reduce_pair(name, chunk, from_sibling, core):
    n, _, h, cols = chunk.shape
    rt = _div_tile(h, max(16, (1 << 20) // (4 * cols)), 16)

    def body(core_ref, a_ref, b_ref, o_ref):
        o_ref[...] = (a_ref[...] + b_ref[...]).astype(o_ref.dtype)

    grid_spec = pltpu.PrefetchScalarGridSpec(
        num_scalar_prefetch=1, grid=(n, h // rt),
        in_specs=[pl.BlockSpec((None, None, rt, cols), lambda s, i, core_ref: (s, core_ref[0], i, 0)),
                  pl.BlockSpec((None, rt, cols), lambda s, i, core_ref: (s, i, 0))],
        out_specs=pl.BlockSpec((None, rt, cols), lambda s, i, core_ref: (s, i, 0)))
    return pl.pallas_call(
        body, name=name, grid_spec=grid_spec, out_shape=jax.ShapeDtypeStruct((n, h, cols), BF16),
        compiler_params=_cparams(("parallel", "parallel")),
    )(core, chunk, from_sibling)


def _reduce_quad(name, q, after=None):
    _, h, cols = q.shape
    rt = _div_tile(h, max(16, (1 << 20) // (4 * cols)), 16)

    def body(q_ref, *rest):
        v = q_ref[...].astype(F32)
        rest[-1][...] = ((v[0] + v[1]) + v[2]) + v[3]

    held = [] if after is None else [after]
    return pl.pallas_call(
        body, name=name, grid=(h // rt,),
        in_specs=[pl.BlockSpec((4, rt, cols), lambda i: (0, i, 0))] + [pl.BlockSpec(memory_space=pl.ANY)] * len(held),
        out_specs=pl.BlockSpec((rt, cols), lambda i: (i, 0)),
        out_shape=jax.ShapeDtypeStruct((h, cols), F32),
        compiler_params=_cparams(("parallel",)),
    )(q, *held)


def _unshard(seg, kind):
    n, r, c = seg.shape
    if kind == "col":
        return seg.transpose(1, 0, 2).reshape(r, n * c)
    return seg.reshape(n * r, c)


def _pad_rows(flat, rows):
    n, ln = flat.shape
    return jnp.pad(flat, ((0, 0), (0, rows * PACK_COLS - ln))).reshape(n, rows, PACK_COLS)


def _block_diag_pairs(w):
    n2, bs, _ = w.shape
    eye = jnp.eye(2, dtype=w.dtype)
    z = w.reshape(n2 // 2, 2, bs, 1, bs) * eye[None, :, None, :, None]
    return z.reshape(n2 // 2, 2 * bs, 2 * bs).transpose(1, 0, 2).reshape(2 * bs, n2 * bs)


def _block_diag_pairs_t(d, bs=64):
    n = d.shape[1] // (2 * bs)
    z = d.reshape(2 * bs, n, 2 * bs).transpose(1, 0, 2).reshape(n, 2, bs, 2, bs)
    return jnp.stack([z[:, 0, :, 0, :], z[:, 1, :, 1, :]], axis=1).reshape(2 * n, bs, bs)


BIG = (("w_in", "col"), ("w_uq", "col"), ("w_ukv", "col"), ("w_proj_rnn", "row"), ("w_proj_mla", "row"),
       ("w_out", "row"), ("w_up", "col"), ("w_down", "row"))
FIRST_USED = ("w_in", "w_uq", "w_ukv")
CONVS = (("conv_w", "col"), ("ffn_conv_w", "col"))
SMALL = ("b_ada", "norm1_g", "conv_b", "w_gate_a", "b_gate_a", "w_gate_x", "b_gate_x", "lru_param",
         "q_norm_g", "kv_norm_g", "norm2_g", "ffn_conv_b", "final_g")
WEIGHTS = ("w_ada", "b_ada", "norm1_g", "w_in", "conv_w", "conv_b", "w_gate_a", "b_gate_a", "w_gate_x",
           "b_gate_x", "lru_param", "q_norm_g", "w_uq", "kv_norm_g", "w_ukv", "w_proj_rnn", "w_proj_mla",
           "w_out", "norm2_g", "w_up", "ffn_conv_w", "ffn_conv_b", "w_down", "final_g")


def _step(x, c, positions, w, m_in, v_in, loss_target):
    s_len, d = x.shape[1], x.shape[2]
    x2d = x[0]
    tgt = loss_target[0]
    xi, yi, ci = lax.axis_index("x"), lax.axis_index("y"), lax.axis_index("c")
    chip = 2 * xi + yi
    me = 2 * chip + ci
    tile = min(256, s_len)
    nt = s_len // tile

    local2d = {k: w[k][0] for k, _ in BIG + CONVS}
    kinds = dict(BIG)
    halves_bf = {k: local2d[k].astype(BF16).reshape(2, local2d[k].shape[0] // 2, local2d[k].shape[1]) for k, _ in BIG}
    first_names = [k for k, _ in BIG if k in FIRST_USED]
    later_names = [k for k, _ in BIG if k not in FIRST_USED]
    full = {}

    def assemble(k, g):
        g = lax.dynamic_update_index_in_dim(g, halves_bf[k][None], chip, 0).reshape((4,) + local2d[k].shape)
        if k == "w_up":
            full["w_up_gate"], full["w_up_val"] = _unshard(g[:2], kinds[k]), _unshard(g[2:], kinds[k])
        else:
            full[k] = _unshard(g, kinds[k])

    first_got = _gather_weights([halves_bf[k] for k in first_names])
    for k, g in zip(first_names, first_got):
        assemble(k, g)
    conv_flat = jnp.concatenate([local2d[k].reshape(-1) for k, _ in CONVS])
    conv_rows = -(-conv_flat.shape[0] // PACK_COLS)
    conv_all = _all_gather("gather_conv_w", _pad_rows(conv_flat[None], conv_rows)[0], CHIPS)
    conv_all = conv_all.reshape(4, -1)
    off = 0
    for k, kind in CONVS:
        r, cc = local2d[k].shape
        full[k] = _unshard(conv_all[:, off:off + r * cc].reshape(4, r, cc), kind)
        off += r * cc

    d_rnn = w["conv_b"].shape[1]
    n_q, n_kv = w["q_norm_g"].shape[1], w["kv_norm_g"].shape[1]
    w_in = full["w_in"]
    o1, o2, o3 = d_rnn + n_q, d_rnn + n_q + n_kv, d_rnn + n_q + n_kv + QK_ROPE
    w_rnn = w_in[:, :d_rnn]
    zpad = lambda n: jnp.zeros((d, n), BF16)
    w_qkv = jnp.concatenate([w_in[:, d_rnn:o2], zpad(QK_NOPE), w_in[:, o2:o3], zpad(LANE - QK_NOPE - QK_ROPE)], axis=1)
    w_g = w_in[:, o3:]
    hd = QK_NOPE + QK_ROPE
    w_uq = jnp.pad(full["w_uq"].reshape(n_q, N_HEADS, hd), ((0, 0), (0, 0), (0, HEAD_PAD - hd))).reshape(n_q, -1)
    w_ukv = full["w_ukv"]
    v_head = w_ukv.shape[1] // N_HEADS - QK_NOPE
    d_ff = w["ffn_conv_b"].shape[1] // 2
    ffn_cw_gate, ffn_cw_val = full["ffn_conv_w"][:, :d_ff], full["ffn_conv_w"][:, d_ff:]
    ffn_cb_gate, ffn_cb_val = w["ffn_conv_b"][:, :d_ff], w["ffn_conv_b"][:, d_ff:]
    conv_w, conv_b = full["conv_w"], w["conv_b"]
    wa_bd = _block_diag_pairs(w["w_gate_a"][0])
    wx_bd = _block_diag_pairs(w["w_gate_x"][0])

    c_all = _all_gather("gather_c", c, ALL7).reshape(8, d)
    c_rows = 128
    (c_act,) = _tiled("silu_c", lambda v: (_silu(v),), 1, [(jnp.pad(c_all, ((0, c_rows - 8), (0, 0))), (c_rows, d), "full")],
                      [((c_rows, d), F32, (c_rows, d), "full")])
    w_ada = w["w_ada"][0]
    n_mod = w_ada.shape[1]
    b_loc = lax.dynamic_slice_in_dim(w["b_ada"], chip * n_mod, n_mod, axis=1)
    mod_loc = _mm("ada_fwd", c_act, w_ada, add=jnp.broadcast_to(b_loc, (c_rows, n_mod)))
    mod_all = _all_gather("gather_mod", mod_loc[:8], CHIPS)
    mod = lax.dynamic_index_in_dim(mod_all, me, 1, keepdims=False).reshape(1, -1)
    shift1, scale1, gate1, shift2, scale2, gate2 = [mod[:, i * d:(i + 1) * d] for i in range(6)]

    small_done = (mod[:, :1] + conv_all[:1, :1] + first_got[0][0, 0, :1, :1].astype(F32))
    later_flight = _split_start(
        "gather_later_start", [halves_bf[k] for k in later_names],
        [jax.ShapeDtypeStruct((4,) + halves_bf[k].shape, BF16) for k in later_names], "gather", after=small_done)

    half = QK_ROPE // 2
    inv_freq = ROPE_THETA ** (-jnp.arange(half, dtype=F32) / half)
    ang = positions[0].astype(F32)[:, None] * inv_freq
    cos, sin = jnp.cos(ang), jnp.sin(ang)
    one, zero = jnp.ones((s_len, QK_NOPE), F32), jnp.zeros((s_len, half), F32)
    tail = jnp.zeros((s_len, LANE - QK_NOPE - QK_ROPE), F32)
    cos_f = jnp.concatenate([one, cos, cos, tail + 1.0], axis=1)
    sin_a = jnp.concatenate([one * 0.0, -sin, zero, tail], axis=1)
    sin_b = jnp.concatenate([one * 0.0, zero, sin, tail], axis=1)
    reset = (positions[0] == 0).astype(F32)[:, None]
    tabs = [(cos_f, (tile, LANE), "row"), (sin_a, (tile, LANE), "row"), (sin_b, (tile, LANE), "row")]

    def rowspec(a):
        return (a, (tile, a.shape[1]), "row")

    def full2(a):
        return (a, a.shape, "full")

    def rowout(cols, dt):
        return ((s_len, cols), dt, (tile, cols), "row")

    def accout(a):
        return (a.shape, F32, a.shape, "acc")

    norm1_g = w["norm1_g"] + later_flight[4][:1, :1]
    norm2_g, final_g = w["norm2_g"], w["final_g"].reshape(1, d)
    ln1_in = [rowspec(x2d), full2(norm1_g), full2(scale1), full2(shift1)]
    big_tile = min(512, s_len)
    (h1,) = _tiled("ln1", _f_ln, nt, ln1_in, [rowout(d, BF16)], row_tile=big_tile)
    x_rnn = _mm("in_rnn", h1, w_rnn, out_dtype=BF16)
    qkv = _mm("in_qkv", h1, w_qkv)
    gates = _mm("in_gates", h1, w_g, out_dtype=BF16)

    ct = LANE
    n_ct = d_rnn // ct
    colspec = lambda a, width=ct: (a, (a.shape[0], width), "col")
    lru_in = [colspec(x_rnn), colspec(conv_w), colspec(conv_b), colspec(wa_bd), colspec(w["b_gate_a"]),
              colspec(wx_bd), colspec(w["b_gate_x"]), colspec(w["lru_param"]), full2(reset)]
    y_rnn, h_rnn = _tiled("lru_fwd", _f_lru_fwd, n_ct, lru_in,
                          [((s_len, d_rnn), BF16, (s_len, ct), "col"), ((s_len, d_rnn), F32, (s_len, ct), "col")])

    qkv_in = [rowspec(qkv)] + tabs + [full2(w["q_norm_g"]), full2(w["kv_norm_g"])]
    qn, kvn, kr = _tiled("qkv_norm", _f_qkv, nt, qkv_in, [rowout(n_q, BF16), rowout(n_kv, BF16), rowout(LANE, BF16)],
                         row_tile=big_tile)
    q_pre = _mm("up_q", qn, w_uq, out_dtype=BF16)
    kv = _mm("up_kv", kvn, w_ukv, out_dtype=BF16)
    (q_cat,) = _tiled("rot_q", _f_rotq, nt, [rowspec(q_pre)] + tabs, [rowout(q_pre.shape[1], BF16)])
    o_mla, lse = _attn_fwd(q_cat, kv, kr)

    send_sems, recv_sems, flown, landed, _ = later_flight
    landed = _split_wait("gather_later_wait", send_sems, recv_sems, flown, landed, "gather", after=o_mla)
    for k, g in zip(later_names, _relay_sibling(landed)):
        assemble(k, g)
    w_pr = full["w_proj_rnn"]
    assert ATTN_HEADS_PER_STEP == 2 and 2 * v_head == HEAD_PAD
    swap_pairs = lambda a: a.reshape(N_HEADS // 2, 2, v_head, d)[:, ::-1].reshape(-1, d)
    w_pm = swap_pairs(full["w_proj_mla"])
    w_out = full["w_out"]
    w_up_gate, w_up_val = full["w_up_gate"], full["w_up_val"]
    w_down = full["w_down"]

    p_rnn = _mm("proj_rnn", y_rnn, w_pr, out_dtype=BF16)
    p_mla = _mm("proj_mla", o_mla, w_pm, out_dtype=BF16)
    merge_in = [rowspec(gates), rowspec(p_rnn), rowspec(p_mla)]
    (merged,) = _tiled("merge", _f_merge, nt, merge_in, [rowout(d, BF16)])
    o_tok = _mm("out_proj", merged, w_out)
    res_in = [rowspec(x2d), rowspec(o_tok), full2(gate1), full2(norm2_g), full2(scale2), full2(shift2)]
    x1, h2 = _tiled("res_ln2", _f_res_ln, nt, res_in, [rowout(d, F32), rowout(d, BF16)], row_tile=big_tile)
    u_gate = _mm("ffn_up_gate", h2, w_up_gate, out_dtype=BF16)
    u_val = _mm("ffn_up_val", h2, w_up_val, out_dtype=BF16)
    n_ft = d_ff // LANE
    ffn_in = [colspec(a) for a in (u_gate, u_val, ffn_cw_gate, ffn_cw_val, ffn_cb_gate, ffn_cb_val)]
    (act,) = _tiled("ffn_conv", _f_ffn, n_ft, ffn_in, [((s_len, d_ff), BF16, (s_len, LANE), "col")])
    f_tok = _mm("ffn_down", act, w_down)

    loss_in = [rowspec(x1), rowspec(f_tok), rowspec(tgt), full2(gate2), full2(final_g)]
    dx1, df, loss_row, d_gate2, d_final_g = _tiled(
        "loss", _f_loss_and_grads, nt, loss_in,
        [rowout(d, F32), rowout(d, BF16), ((1, LANE), F32, (1, LANE), "acc"), accout(gate2), accout(final_g)],
        row_tile=big_tile)
    loss = lax.psum(loss_row[0, 0], ("x", "y", "c"))

    d_act = _mm("ffn_down_dx", df, w_down, tb=True, out_dtype=BF16)
    g_w_down = _mm("ffn_down_dw", act, df, ta=True)
    taps = ffn_cw_gate.shape[0]
    du_gate, du_val, g_cw_gate, g_cw_val, g_cb_gate, g_cb_val = _tiled(
        "ffn_conv_bwd", _vjp_of(_f_ffn, 6, (0, 1, 2, 3, 4, 5)), n_ft, ffn_in + [colspec(d_act)],
        [((s_len, d_ff), BF16, (s_len, LANE), "col")] * 2 + [((taps, d_ff), F32, (taps, LANE), "col")] * 2
        + [((1, d_ff), F32, (1, LANE), "col")] * 2)
    dh2 = _mm("ffn_up_gate_dx", du_gate, w_up_gate, tb=True)
    dh2 = _mm("ffn_up_val_dx", du_val, w_up_val, tb=True, add=dh2, out_dtype=BF16)
    g_w_up_halves = [_mm("ffn_up_gate_dw", h2, du_gate, ta=True), _mm("ffn_up_val_dw", h2, du_val, ta=True)]
    g_ffn_cw = jnp.concatenate([g_cw_gate, g_cw_val], axis=1)
    g_ffn_cb = jnp.concatenate([g_cb_gate, g_cb_val], axis=1)

    res_bwd = _vjp_of(_f_res_ln, 6, (0, 1, 2, 3, 4, 5))
    dx_res, do_tok, d_gate1, g_norm2, d_scale2, d_shift2 = _tiled(
        "res_ln2_bwd", res_bwd, nt, res_in + [rowspec(dx1), rowspec(dh2)],
        [rowout(d, F32), rowout(d, BF16), accout(gate1), accout(norm2_g), accout(scale2), accout(shift2)],
        row_tile=big_tile)
    d_merged = _mm("out_proj_dx", do_tok, w_out, tb=True, out_dtype=BF16)
    g_w_out = _mm("out_proj_dw", merged, do_tok, ta=True)
    d_gates, dp_rnn, dp_mla = _tiled(
        "merge_bwd", _f_merge_bwd, nt, merge_in + [rowspec(d_merged)],
        [rowout(gates.shape[1], BF16), rowout(d, BF16), rowout(d, BF16)])
    dy_rnn = _mm("proj_rnn_dx", dp_rnn, w_pr, tb=True, out_dtype=BF16)
    g_w_pr = _mm("proj_rnn_dw", y_rnn, dp_rnn, ta=True)
    do_mla = _mm("proj_mla_dx", dp_mla, w_pm, tb=True, out_dtype=BF16)
    g_w_pm = _mm("proj_mla_dw", o_mla, dp_mla, ta=True)

    def chunked(k, gk):
        r, cc = local2d[k].shape
        if kinds[k] == "col":
            gk = gk.reshape(r, 4, cc).transpose(1, 0, 2)
        return gk.reshape(4, 2, r // 2, cc)

    def pair_sums(tag, names, chunks):
        out = []
        core = ci.astype(jnp.int32).reshape(1)
        for k, ck, from_sib in zip(names, chunks, _pair_exchange("reduce_pair_exchange_" + tag, chunks)):
            out.append(_reduce_pair("reduce_pair_" + k, ck, from_sib, core))
        return out

    g_later = {
        "w_proj_rnn": g_w_pr,
        "w_proj_mla": swap_pairs(g_w_pm),
        "w_out": g_w_out, "w_down": g_w_down,
    }
    r_up, c_up = local2d["w_up"].shape
    up_chunks = jnp.concatenate([g.reshape(r_up, 2, c_up).transpose(1, 0, 2) for g in g_w_up_halves], axis=0)
    chunks_ready = [up_chunks.reshape(4, 2, r_up // 2, c_up) if k == "w_up" else chunked(k, g_later[k])
                    for k in later_names]
    sums_ready = pair_sums("ready", later_names, chunks_ready)
    ready_flight = _split_start(
        "reduce_ready_start", sums_ready, [jax.ShapeDtypeStruct(s.shape, s.dtype) for s in sums_ready], "alltoall",
        after=sums_ready[0])
    kr_held = kr + ready_flight[4][:1, :].astype(BF16)

    dq_cat, dkv, dkr = _attn_bwd(q_cat, kv, kr_held, o_mla, lse, do_mla)
    (dq_pre,) = _tiled("rot_q_bwd", _f_rotq_bwd, nt, [rowspec(q_pre)] + tabs + [rowspec(dq_cat)],
                       [rowout(q_pre.shape[1], BF16)])
    dqn = _mm("up_q_dx", dq_pre, w_uq, tb=True, out_dtype=BF16)
    g_w_uq = _mm("up_q_dw", qn, dq_pre, ta=True)
    dkvn = _mm("up_kv_dx", dkv, w_ukv, tb=True, out_dtype=BF16)
    g_w_ukv = _mm("up_kv_dw", kvn, dkv, ta=True)
    dqkv, g_q_norm, g_kv_norm = _tiled(
        "qkv_norm_bwd", _f_qkv_bwd, nt, qkv_in + [rowspec(dqn), rowspec(dkvn), rowspec(dkr)],
        [rowout(qkv.shape[1], BF16), accout(w["q_norm_g"]), accout(w["kv_norm_g"])], row_tile=big_tile)

    lru_out = [((s_len, d_rnn), BF16, (s_len, ct), "col")]
    for a in (conv_w, conv_b, wa_bd, w["b_gate_a"], wx_bd, w["b_gate_x"], w["lru_param"]):
        lru_out.append((a.shape, F32, (a.shape[0], ct), "col"))
    dx_rnn, g_conv_w, g_conv_b, g_wa_bd, g_b_a, g_wx_bd, g_b_x, g_lru = _tiled(
        "lru_bwd", _f_lru_bwd, n_ct, lru_in + [colspec(h_rnn), colspec(dy_rnn)], lru_out)

    dh1 = _mm("in_gates_dx", d_gates, w_g, tb=True)
    dh1 = _mm("in_qkv_dx", dqkv, w_qkv, tb=True, add=dh1)
    dh1 = _mm("in_rnn_dx", dx_rnn, w_rnn, tb=True, add=dh1)
    g_w_rnn = _mm("in_rnn_dw", h1, dx_rnn, ta=True)
    g_w_qkv = _mm("in_qkv_dw", h1, dqkv, ta=True)
    g_w_g = _mm("in_gates_dw", h1, d_gates, ta=True)

    ln_bwd = _vjp_of(_f_ln, 4, (0, 1, 2, 3))

    def ln1_bwd(xv, gv, sc, sh, dxr, dh):
        dx, dg, dsc, dsh = ln_bwd(xv, gv, sc, sh, dh)
        return dx + dxr, dg, dsc, dsh

    grad_x, g_norm1, d_scale1, d_shift1 = _tiled(
        "ln1_bwd", ln1_bwd, nt, ln1_in + [rowspec(dx_res), rowspec(dh1)],
        [rowout(d, F32), accout(norm1_g), accout(scale1), accout(shift1)], row_tile=big_tile)

    dmod = jnp.concatenate([d_shift1, d_scale1, d_gate1, d_shift2, d_scale2, d_gate2], axis=1)
    dmod_all = _all_gather("gather_dmod", dmod, ALL7).reshape(8, -1)
    dmod_loc = lax.dynamic_slice_in_dim(dmod_all, chip * n_mod, n_mod, axis=1)
    g_w_ada = _mm("ada_dw", c_act, jnp.pad(dmod_loc, ((0, c_rows - 8), (0, 0))), ta=True)

    g_full = {
        "w_in": jnp.concatenate([g_w_rnn, g_w_qkv[:, :n_q + n_kv],
                                 g_w_qkv[:, n_q + n_kv + QK_NOPE:n_q + n_kv + QK_NOPE + QK_ROPE], g_w_g], axis=1),
        "w_uq": g_w_uq.reshape(n_q, N_HEADS, HEAD_PAD)[:, :, :hd].reshape(n_q, -1),
        "w_ukv": g_w_ukv,
        "conv_w": g_conv_w,
        "ffn_conv_w": g_ffn_cw,
    }
    g_small = {
        "b_ada": dmod, "norm1_g": g_norm1, "conv_b": g_conv_b,
        "w_gate_a": _block_diag_pairs_t(g_wa_bd)[None], "b_gate_a": g_b_a,
        "w_gate_x": _block_diag_pairs_t(g_wx_bd)[None], "b_gate_x": g_b_x, "lru_param": g_lru,
        "q_norm_g": g_q_norm, "kv_norm_g": g_kv_norm, "norm2_g": g_norm2,
        "ffn_conv_b": g_ffn_cb, "final_g": d_final_g.reshape(w["final_g"].shape),
    }

    small_flat = jnp.concatenate([g_small[k].reshape(-1) for k in SMALL] + [g_full[k].reshape(-1) for k, _ in CONVS])
    small_rows = -(-small_flat.shape[0] // (8 * PACK_COLS * PACK_ROW_UNIT)) * PACK_ROW_UNIT
    last_names = first_names + ["small"]
    last_chunks = [chunked(k, g_full[k]) for k in first_names]
    last_chunks.append(_pad_rows(small_flat[None], 8 * small_rows).reshape(4, 2, small_rows, PACK_COLS))
    sums_last = pair_sums("last", last_names, last_chunks)
    send_sems, recv_sems, flown, landed, _ = ready_flight
    quads_ready = _split_wait("reduce_ready_wait", send_sems, recv_sems, flown, landed, "alltoall", after=grad_x)
    last_flight = _split_start(
        "reduce_last_start", sums_last, [jax.ShapeDtypeStruct(s.shape, s.dtype) for s in sums_last], "alltoall",
        after=quads_ready[0])
    grads = {"w_ada": g_w_ada[None]}
    delta, new_m, new_v = {}, {}, {}

    def adamw(k):
        shp = w[k].shape
        flip = len(shp) == 3 and shp[-1] % LANE != 0 and shp[-2] % LANE == 0
        view = (lambda a: jnp.swapaxes(a, 1, 2)) if flip else (lambda a: a)
        two_d = (-1, view(w[k]).shape[-1]) if len(shp) > 1 else (1, -1)
        dk, mk, vk = _adamw("adamw_" + k, *[view(a).reshape(two_d) for a in (w[k], grads[k], m_in[k], v_in[k])])
        back = lambda a: view(a.reshape(view(w[k]).shape))
        delta[k], new_m[k], new_v[k] = back(dk), back(mk), back(vk)

    def finish(tag, names, quads, sums, after):
        reduced = {}
        for k, quad, ps in zip(names, quads, sums):
            quad = lax.dynamic_update_index_in_dim(quad, lax.dynamic_index_in_dim(ps, chip, 0, keepdims=True), chip, 0)
            reduced[k] = _reduce_quad("reduce_quad_" + k, quad, after)
        big = [k for k in names if k != "small"]
        for k, both in zip(big, _share_sibling("share_sibling_" + tag, [reduced[k] for k in big])):
            grads[k] = lax.dynamic_update_index_in_dim(both, reduced[k][None], ci, 0).reshape(w[k].shape)
        return reduced

    finish("ready", later_names, quads_ready, sums_ready, after=last_flight[4])
    for k in later_names + ["w_ada"]:
        adamw(k)
    send_sems, recv_sems, flown, landed, _ = last_flight
    quads_last = _split_wait("reduce_last_wait", send_sems, recv_sems, flown, landed, "alltoall",
                             after=delta[later_names[-1]])
    reduced = finish("last", last_names, quads_last, sums_last, after=None)
    small_grad = _all_gather("share_small", reduced["small"], ALL7).reshape(-1)
    off = 0
    for k in SMALL:
        grads[k] = small_grad[off:off + w[k].size].reshape(w[k].shape)
        off += w[k].size
    for k, _ in CONVS:
        r, cc = local2d[k].shape
        whole = small_grad[off:off + 4 * r * cc].reshape(r, 4 * cc)
        grads[k] = lax.dynamic_slice_in_dim(whole, chip * cc, cc, axis=1)[None]
        off += 4 * r * cc
    for k in WEIGHTS:
        if k not in delta:
            adamw(k)

    return (loss, grad_x[None], *[grads[k] for k in WEIGHTS], *[delta[k] for k in WEIGHTS],
            *[new_m[k] for k in WEIGHTS], *[new_v[k] for k in WEIGHTS])


def kernel(x, c, positions, w_ada, b_ada, norm1_g, w_in, conv_w, conv_b, w_gate_a, b_gate_a, w_gate_x, b_gate_x, lru_param, q_norm_g, w_uq, kv_norm_g, w_ukv, w_proj_rnn, w_proj_mla, w_out, norm2_g, w_up, ffn_conv_w, ffn_conv_b, w_down, final_g, loss_target, m_w_ada, m_b_ada, m_norm1_g, m_w_in, m_conv_w, m_conv_b, m_w_gate_a, m_b_gate_a, m_w_gate_x, m_b_gate_x, m_lru_param, m_q_norm_g, m_w_uq, m_kv_norm_g, m_w_ukv, m_w_proj_rnn, m_w_proj_mla, m_w_out, m_norm2_g, m_w_up, m_ffn_conv_w, m_ffn_conv_b, m_w_down, m_final_g, v_w_ada, v_b_ada, v_norm1_g, v_w_in, v_conv_w, v_conv_b, v_w_gate_a, v_b_gate_a, v_w_gate_x, v_b_gate_x, v_lru_param, v_q_norm_g, v_w_uq, v_kv_norm_g, v_w_ukv, v_w_proj_rnn, v_w_proj_mla, v_w_out, v_norm2_g, v_w_up, v_ffn_conv_w, v_ffn_conv_b, v_w_down, v_final_g):
    given = dict(locals())
    w = {k: given[k] for k in WEIGHTS}
    m_in = {k: given["m_" + k] for k in WEIGHTS}
    v_in = {k: given["v_" + k] for k in WEIGHTS}
    return _step(x, c, positions, w, m_in, v_in, loss_target)
```

```python
import functools
import math

import jax
import jax.numpy as jnp
from jax import lax
from jax.experimental import pallas as pl
from jax.experimental.pallas import tpu as pltpu

F32 = jnp.float32
BF16 = jnp.bfloat16

EPS = 1e-6
LRU_C = 8.0
N_HEADS = 16
QK_NOPE = 64
QK_ROPE = 32
HEAD_PAD = 128
ROPE_THETA = 10000.0
ADAM_LR = 0.001
ADAM_B1 = 0.9
ADAM_B2 = 0.999
ADAM_EPS = 1e-08
ADAM_WD = 0.01
ADAM_STEP = 10

LANE = 128
SUBLANES = 8
VMEM_LIMIT = 48 * 1024 * 1024
MM_TILE_M = MM_TILE_N = MM_TILE_K = 1408
MM_TILE_M_BF16 = 2048
PACK_COLS = 1024
PACK_ROW_UNIT = 32
MESH = pl.DeviceIdType.MESH

NN = (((1,), (0,)), ((), ()))
NT = (((1,), (1,)), ((), ()))
TN = (((0,), (0,)), ((), ()))


def _cparams(sem):
    return pltpu.CompilerParams(dimension_semantics=sem, vmem_limit_bytes=VMEM_LIMIT)


def _div_tile(n, cap, unit):
    best = None
    d = unit
    while d <= min(n, cap):
        if n % d == 0:
            best = d
        d += unit
    return n if best is None else best


def _mm(name, a, b, *, ta=False, tb=False, add=None, out_dtype=F32):
    if ta:
        kdim, m = a.shape
    else:
        m, kdim = a.shape
    if tb:
        n, kb = b.shape
    else:
        kb, n = b.shape
    assert kdim == kb, (name, a.shape, b.shape)
    tall = not ta and add is None and jnp.dtype(out_dtype).itemsize == 2 and kdim <= MM_TILE_K
    tm = _div_tile(m, MM_TILE_M_BF16 if tall else MM_TILE_M, 8 if not ta else LANE)
    tn = _div_tile(n, MM_TILE_N, LANE)
    tk = _div_tile(kdim, MM_TILE_K, LANE)
    nk = kdim // tk
    a_spec = pl.BlockSpec((tk, tm), lambda i, j, k: (k, i)) if ta else pl.BlockSpec((tm, tk), lambda i, j, k: (i, k))
    b_spec = pl.BlockSpec((tn, tk), lambda i, j, k: (j, k)) if tb else pl.BlockSpec((tk, tn), lambda i, j, k: (k, j))
    o_spec = pl.BlockSpec((tm, tn), lambda i, j, k: (i, j))
    has_add = add is not None
    dims = ((((0,) if ta else (1,)), ((1,) if tb else (0,))), ((), ()))

    def body(*refs):
        a_ref, b_ref = refs[0], refs[1]
        c_ref = refs[2] if has_add else None
        o_ref = refs[3] if has_add else refs[2]
        prod = lax.dot_general(a_ref[...].astype(BF16), b_ref[...].astype(BF16), dims, preferred_element_type=F32)
        if nk == 1:
            o_ref[...] = (prod + c_ref[...].astype(F32) if has_add else prod).astype(o_ref.dtype)
            return
        acc = refs[-1]
        k = pl.program_id(2)

        @pl.when(k == 0)
        def _():
            acc[...] = prod + c_ref[...].astype(F32) if has_add else prod

        @pl.when(jnp.logical_and(k > 0, k < nk - 1))
        def _():
            acc[...] += prod

        @pl.when(k == nk - 1)
        def _():
            o_ref[...] = (acc[...] + prod).astype(o_ref.dtype)

    ins = [a, b] + ([add] if has_add else [])
    specs = [a_spec, b_spec] + ([o_spec] if has_add else [])
    return pl.pallas_call(
        body, name=name, grid=(m // tm, n // tn, nk), in_specs=specs, out_specs=o_spec,
        out_shape=jax.ShapeDtypeStruct((m, n), out_dtype),
        scratch_shapes=[pltpu.VMEM((tm, tn), F32)] if nk > 1 else [],
        compiler_params=_cparams(("parallel", "parallel", "arbitrary")),
    )(*ins)


_IMAPS = {
    "row": lambda i: (i, 0),
    "col": lambda i: (0, i),
    "full": lambda i: (0, 0),
    "acc": lambda i: (0, 0),
}


def _tiled(name, fn, n, ins, outs, row_tile=None):
    if row_tile is not None:
        rows = next(a.shape[0] for a, _, k in ins if k == "row")
        n = rows // row_tile
        ins = [(a, (row_tile, bs[1]) if k == "row" else bs, k) for a, bs, k in ins]
        outs = [(s, dt, (row_tile, bs[1]) if k == "row" else bs, k) for s, dt, bs, k in outs]
    ni = len(ins)
    is_acc = [k == "acc" for *_, k in outs]

    def body(*refs):
        vals = fn(*[r[...].astype(F32) if r.dtype == BF16 else r[...] for r in refs[:ni]])
        orefs = refs[ni:]
        if any(is_acc):
            @pl.when(pl.program_id(0) == 0)
            def _():
                for r, a in zip(orefs, is_acc):
                    if a:
                        r[...] = jnp.zeros(r.shape, r.dtype)
        for r, v, a in zip(orefs, vals, is_acc):
            if a:
                r[...] += v.astype(r.dtype)
            else:
                r[...] = v.astype(r.dtype)

    res = pl.pallas_call(
        body, name=name, grid=(n,),
        in_specs=[pl.BlockSpec(bs, _IMAPS[k]) for _, bs, k in ins],
        out_specs=[pl.BlockSpec(bs, _IMAPS[k]) for _, _, bs, k in outs],
        out_shape=[jax.ShapeDtypeStruct(s, d) for s, d, _, _ in outs],
        compiler_params=_cparams(("arbitrary",)),
    )(*[a for a, _, _ in ins])
    return tuple(res)


def _vjp_of(fn, nin, diff):
    def g(*args):
        ins, cots = args[:nin], args[nin:]

        def f(*d):
            full = list(ins)
            for i, v in zip(diff, d):
                full[i] = v
            return fn(*full)

        outs, vjp = jax.vjp(f, *[ins[i] for i in diff])
        return vjp(tuple(c.astype(o.dtype) for c, o in zip(cots, outs)))
    return g


def _shift_rows(x, k, fill, up=False):
    n = x.shape[0]
    if k % SUBLANES == 0:
        pad = jnp.full((k,) + x.shape[1:], fill, x.dtype)
        return jnp.concatenate([x[k:], pad], axis=0) if up else jnp.concatenate([pad, x[:n - k]], axis=0)
    rows = lax.broadcasted_iota(jnp.int32, x.shape, 0)
    if up:
        return jnp.where(rows < n - k, pltpu.roll(x, n - k, 0), fill)
    return jnp.where(rows >= k, pltpu.roll(x, k, 0), fill)


@functools.partial(jax.custom_vjp, nondiff_argnums=(1,))
def _delay(x, k):
    return _shift_rows(x, k, 0.0)


def _delay_fwd(x, k):
    return _shift_rows(x, k, 0.0), None


def _delay_bwd(k, _, g):
    return (_shift_rows(g, k, 0.0, up=True),)


_delay.defvjp(_delay_fwd, _delay_bwd)


@functools.partial(jax.custom_vjp, nondiff_argnums=(1,))
def _lane_roll(x, s):
    return pltpu.roll(x, s, 1)


def _lane_roll_fwd(x, s):
    return pltpu.roll(x, s, 1), None


def _lane_roll_bwd(s, _, g):
    return (pltpu.roll(g, g.shape[1] - s, 1),)


_lane_roll.defvjp(_lane_roll_fwd, _lane_roll_bwd)


@jax.custom_vjp
def _bdot(x, w):
    return lax.dot_general(x.astype(BF16), w.astype(BF16), NN, preferred_element_type=F32)


def _bdot_fwd(x, w):
    return _bdot(x, w), (x, w)


def _bdot_bwd(res, g):
    x, w = res
    gb = g.astype(BF16)
    dx = lax.dot_general(gb, w.astype(BF16), NT, preferred_element_type=F32)
    dw = lax.dot_general(x.T.astype(BF16), gb, NN, preferred_element_type=F32)
    return dx, dw


_bdot.defvjp(_bdot_fwd, _bdot_bwd)


def _sigmoid(x):
    return 0.5 * (jnp.tanh(0.5 * x) + 1.0)


def _silu(x):
    return x * _sigmoid(x)


def _rms(x, g):
    return x * lax.rsqrt(jnp.mean(x * x, axis=-1, keepdims=True) + EPS) * g


def _causal_conv(x, w, b):
    kw = w.shape[0]
    tap = lax.broadcasted_iota(jnp.int32, w.shape, 0)
    y = b
    for k in range(kw):
        d = kw - 1 - k
        wk = jnp.sum(jnp.where(tap == k, w, 0.0), axis=0, keepdims=True)
        y = y + wk * (x if d == 0 else _delay(x, d))
    return y


def _rotate(x, cos_f, sin_a, sin_b):
    reps = x.shape[1] // LANE
    if reps > 1:
        cos_f, sin_a, sin_b = (jnp.tile(t, (1, reps)) for t in (cos_f, sin_a, sin_b))
    n = x.shape[1]
    half = QK_ROPE // 2
    return x * cos_f + _lane_roll(x, n - half) * sin_a + _lane_roll(x, half) * sin_b


def _softplus_neg(l):
    u = jnp.exp(-jnp.abs(l))
    log1p_u = jnp.where(u < 0.01, u * (1.0 - u * (0.5 - u * (1.0 / 3.0))), jnp.log(1.0 + u))
    return jnp.maximum(-l, 0.0) + log1p_u


def _f_ln(x, g, scale, shift):
    return (_rms(x, g) * (1.0 + scale) + shift,)


def _f_qkv(qkv, cos_f, sin_a, sin_b, qg, kvg):
    nq, nkv = qg.shape[1], kvg.shape[1]
    qn = _rms(qkv[:, :nq], qg)
    kvn = _rms(qkv[:, nq:nq + nkv], kvg)
    kr = _rotate(qkv[:, nq + nkv:], cos_f, sin_a, sin_b)
    return qn, kvn, kr


def _f_qkv_bwd(qkv, cos_f, sin_a, sin_b, qg, kvg, dqn, dkvn, dkr):
    nq, nkv = qg.shape[1], kvg.shape[1]
    _, vjp_q = jax.vjp(_rms, qkv[:, :nq], qg)
    _, vjp_kv = jax.vjp(_rms, qkv[:, nq:nq + nkv], kvg)
    _, vjp_r = jax.vjp(lambda t: _rotate(t, cos_f, sin_a, sin_b), qkv[:, nq + nkv:])
    dq_lat, dqg = vjp_q(dqn)
    dkv_lat, dkvg = vjp_kv(dkvn)
    (dkr_pre,) = vjp_r(dkr)
    return jnp.concatenate([dq_lat, dkv_lat, dkr_pre], axis=1), dqg, dkvg


QK_SCALE = 1.0 / math.sqrt(QK_NOPE + QK_ROPE)
LOG2_E = 1.4426950408889634
LN_2 = 0.6931471805599453


def _f_rotq(q, cos_f, sin_a, sin_b):
    return (_rotate(q, cos_f, sin_a, sin_b) * (QK_SCALE * LOG2_E),)


def _f_rotq_bwd(cos_f, sin_a, sin_b, dq):
    _, vjp = jax.vjp(lambda t: _rotate(t, cos_f, sin_a, sin_b) * QK_SCALE, jnp.zeros_like(dq))
    return vjp(dq)


def _merge(g_rnn, g_mla, p_rnn, p_mla):
    return _sigmoid(g_rnn) * p_rnn + _sigmoid(g_mla) * p_mla


def _f_merge(g, p_rnn, p_mla):
    d = p_rnn.shape[1]
    return (_merge(g[:, :d], g[:, d:], p_rnn, p_mla),)


def _f_merge_bwd(g, p_rnn, p_mla, dm):
    d = p_rnn.shape[1]
    _, vjp = jax.vjp(_merge, g[:, :d], g[:, d:], p_rnn, p_mla)
    dg_rnn, dg_mla, dp_rnn, dp_mla = vjp(dm)
    return jnp.concatenate([dg_rnn, dg_mla], axis=1), dp_rnn, dp_mla


def _f_res_ln(x, o, gate, g2, scale, shift):
    x1 = x + gate * o
    return x1, _rms(x1, g2) * (1.0 + scale) + shift


def _f_ffn(u_gate, u_val, cw_gate, cw_val, cb_gate, cb_val):
    return (_silu(_causal_conv(u_gate, cw_gate, cb_gate)) * _causal_conv(u_val, cw_val, cb_val),)


def _f_loss(x1, f, tgt, gate, fg):
    y = _rms(x1 + gate * f, fg)
    err = (y - tgt) * (y - tgt)
    return 0.5 * jnp.sum(jnp.mean(err, axis=-1, keepdims=True), axis=0, keepdims=True)


def _f_loss_and_grads(x1, f, tgt, gate, fg):
    loss, vjp = jax.vjp(lambda a, b, c, d: _f_loss(a, b, tgt, c, d), x1, f, gate, fg)
    dx1, df, dgate, dfg = vjp(jnp.ones((1, 1), F32))
    return dx1, df, jnp.broadcast_to(loss, (1, LANE)), dgate, dfg


@jax.custom_vjp
def _decay_and_gain(log_a):
    a = jnp.exp(log_a)
    return a, jnp.sqrt(-jnp.tanh(log_a) * (1.0 + a * a))


def _decay_and_gain_fwd(log_a):
    a, gain = _decay_and_gain(log_a)
    return (a, gain), (a, gain)


def _decay_and_gain_bwd(res, g):
    a, gain = res
    return (g[0] * a - g[1] * (a * a) / gain,)


_decay_and_gain.defvjp(_decay_and_gain_fwd, _decay_and_gain_bwd)


def _f_lru_coeffs(xr, cw, cb, wa, ba, wx, bx, lru, reset):
    xc = _causal_conv(xr, cw, cb)
    r = _sigmoid(_bdot(xc, wa) + ba)
    i = _sigmoid(_bdot(xc, wx) + bx)
    log_a = (-LRU_C) * r * _softplus_neg(lru)
    a, mult = _decay_and_gain(log_a)
    is_reset = reset > 0.5
    a = jnp.where(is_reset, 0.0, a)
    mult = jnp.where(is_reset, 1.0, mult)
    return a, mult * (i * xc)


SCAN_BLOCK = 64


def _scan(a, b, up=False):
    n = a.shape[0]
    blk = min(SCAN_BLOCK, n)
    pos = lax.broadcasted_iota(jnp.int32, a.shape, 0) % blk
    k = 1
    while k < blk:
        inside = (pos < blk - k) if up else (pos >= k)
        shift = n - k if up else k
        b = b + a * jnp.where(inside, pltpu.roll(b, shift, 0), 0.0)
        a = a * jnp.where(inside, pltpu.roll(a, shift, 0), 1.0)
        k *= 2
    blocks = range(n // blk)
    carry = jnp.zeros((1,) + a.shape[1:], a.dtype)
    out = [None] * len(blocks)
    for i in (reversed(blocks) if up else blocks):
        rows = slice(i * blk, (i + 1) * blk)
        out[i] = b[rows] + a[rows] * carry
        carry = out[i][:1] if up else out[i][blk - 1:]
    return jnp.concatenate(out, axis=0)


def _f_lru_fwd(xr, cw, cb, wa, ba, wx, bx, lru, reset):
    a, b = _f_lru_coeffs(xr, cw, cb, wa, ba, wx, bx, lru, reset)
    h = _scan(a, b)
    return h, h


def _f_lru_bwd(xr, cw, cb, wa, ba, wx, bx, lru, reset, h, dh):
    (a, _), vjp = jax.vjp(lambda *p: _f_lru_coeffs(*p, reset), xr, cw, cb, wa, ba, wx, bx, lru)
    g = _scan(_shift_rows(a, 1, 0.0, up=True), dh, up=True)
    return vjp((g * _shift_rows(h, 1, 0.0), g))


def _attn_tile(s):
    return 1024 if s >= 2048 else s // 2


def _keys(kv, kr):
    lane = lax.broadcasted_iota(jnp.int32, kv.shape, 1)
    return jnp.where(lane < QK_NOPE, kv, kr)


ATTN_HEADS_PER_STEP = 2


def _scores(q, kc, diagonal):
    s = lax.dot_general(q, kc, NT, preferred_element_type=F32)
    if not diagonal:
        return s
    rows = lax.broadcasted_iota(jnp.int32, s.shape, 0)
    cols = lax.broadcasted_iota(jnp.int32, s.shape, 1)
    return jnp.where(cols - (s.shape[1] - s.shape[0]) <= rows, s, -jnp.inf)


def _sub_blocks(t, diagonal):
    return ((0, t // 2, t // 2), (t // 2, t // 2, t)) if diagonal else ((0, t, t),)


def _causal_pairs(nb, k_major):
    if k_major:
        pairs = [(qb, kb) for kb in range(nb) for qb in range(kb, nb)]
    else:
        pairs = [(qb, kb) for qb in range(nb) for kb in range(qb + 1)]
    return jnp.array([p[0] for p in pairs], jnp.int32), jnp.array([p[1] for p in pairs], jnp.int32)


def _attn_fwd(q_pre, tables, kv, kr):
    s_len = q_pre.shape[0]
    t = _attn_tile(s_len)
    nb = s_len // t
    hp = ATTN_HEADS_PER_STEP
    wide = hp * HEAD_PAD
    q_tab, k_tab = _causal_pairs(nb, k_major=False)

    def body(qt, kt, qp_ref, cos_ref, sina_ref, sinb_ref, kv_ref, kr_ref, o_ref, lse_ref, q_ref, m_s, acc_s):
        pair = pl.program_id(1)
        qi, ki = qt[pair], kt[pair]

        @pl.when(ki == 0)
        def _():
            m_s[...] = jnp.full(m_s.shape, -jnp.inf, F32)
            acc_s[...] = jnp.zeros(acc_s.shape, F32)
            (rotated,) = _f_rotq(qp_ref[...].astype(F32), cos_ref[...], sina_ref[...], sinb_ref[...])
            q_ref[...] = rotated.astype(q_ref.dtype)

        def step(diagonal):
            for h in range(hp):
                lanes = slice(h * HEAD_PAD, (h + 1) * HEAD_PAD)
                for r0, nr, nk in _sub_blocks(t, diagonal):
                    rows = slice(r0, r0 + nr)
                    kvv = kv_ref[:nk, lanes]
                    s = _scores(q_ref[rows, lanes], _keys(kvv, kr_ref[:nk, :]), diagonal)
                    m_old = m_s[h, rows]
                    m_new = jnp.maximum(m_old, jnp.max(s, axis=-1, keepdims=True))
                    alpha = jnp.exp2(m_old - m_new)
                    p = jnp.exp2(s - m_new)
                    lane = lax.broadcasted_iota(jnp.int32, kvv.shape, 1)
                    ones_and_values = jnp.where(lane < QK_NOPE, jnp.ones_like(kvv), kvv)
                    acc_s[rows, lanes] = alpha * acc_s[rows, lanes] + lax.dot_general(
                        p.astype(BF16), ones_and_values, NN, preferred_element_type=F32)
                    m_s[h, rows] = m_new

        @pl.when(ki < qi)
        def _():
            step(False)

        @pl.when(ki == qi)
        def _():
            step(True)
            lane = lax.broadcasted_iota(jnp.int32, (t, HEAD_PAD), 1)
            outs = []
            for h in range(hp):
                acc = acc_s[:, h * HEAD_PAD:(h + 1) * HEAD_PAD]
                total = acc[:, :1]
                outs.append(acc / total)
                lse_ref[h] = m_s[h] + jnp.log(total) * LOG2_E
            o_ref[...] = jnp.where(lane >= QK_NOPE, outs[0], pltpu.roll(outs[1], QK_NOPE, 1)).astype(o_ref.dtype)

    q_rows = lambda h, p, qt, kt: (qt[p], 0)
    grid_spec = pltpu.PrefetchScalarGridSpec(
        num_scalar_prefetch=2, grid=(N_HEADS // hp, q_tab.shape[0]),
        in_specs=[pl.BlockSpec((t, wide), lambda h, p, qt, kt: (qt[p], h)),
                  pl.BlockSpec((t, HEAD_PAD), q_rows), pl.BlockSpec((t, HEAD_PAD), q_rows),
                  pl.BlockSpec((t, HEAD_PAD), q_rows),
                  pl.BlockSpec((t, wide), lambda h, p, qt, kt: (kt[p], h)),
                  pl.BlockSpec((t, HEAD_PAD), lambda h, p, qt, kt: (kt[p], 0))],
        out_specs=[pl.BlockSpec((t, HEAD_PAD), lambda h, p, qt, kt: (qt[p], h)),
                   pl.BlockSpec((hp, t, 1), lambda h, p, qt, kt: (h, qt[p], 0)),
                   pl.BlockSpec((t, wide), lambda h, p, qt, kt: (qt[p], h))],
        scratch_shapes=[pltpu.VMEM((hp, t, 1), F32), pltpu.VMEM((t, wide), F32)])
    return pl.pallas_call(
        body, name="attn_fwd", grid_spec=grid_spec,
        out_shape=[jax.ShapeDtypeStruct((s_len, N_HEADS // hp * HEAD_PAD), BF16),
                   jax.ShapeDtypeStruct((N_HEADS, s_len, 1), F32),
                   jax.ShapeDtypeStruct((s_len, N_HEADS * HEAD_PAD), BF16)],
        compiler_params=_cparams(("arbitrary", "arbitrary")),
    )(q_tab, k_tab, q_pre, *tables, kv, kr)


def _attn_bwd(q, kv, kr, o, lse, do):
    s_len = q.shape[0]
    t = _attn_tile(s_len)
    nb = s_len // t
    hp = ATTN_HEADS_PER_STEP
    wide = hp * HEAD_PAD
    q_tab, k_tab = _causal_pairs(nb, k_major=True)

    def body(qt, kt, q_ref, kv_ref, kr_ref, o_ref, lse_ref, do_ref, dq_ref, dkv_ref, dkr_ref, dk_s, dv_s, dq_s):
        g, pair = pl.program_id(0), pl.program_id(1)
        qb, kb = qt[pair], kt[pair]

        @pl.when(jnp.logical_and(g == 0, pair == 0))
        def _():
            dkr_ref[...] = jnp.zeros(dkr_ref.shape, F32)

        @pl.when(pair == 0)
        def _():
            dq_s[...] = jnp.zeros(dq_s.shape, F32)

        @pl.when(qb == kb)
        def _():
            dk_s[...] = jnp.zeros(dk_s.shape, F32)
            dv_s[...] = jnp.zeros(dv_s.shape, F32)

        def step(diagonal):
            for h in range(hp):
                lanes = slice(h * HEAD_PAD, (h + 1) * HEAD_PAD)
                for r0, nr, nk in _sub_blocks(t, diagonal):
                    rows, keys = slice(r0, r0 + nr), slice(0, nk)
                    qv, kvv = q_ref[rows, lanes], kv_ref[keys, lanes]
                    pair_do = do_ref[rows, :].astype(F32)
                    lane = lax.broadcasted_iota(jnp.int32, pair_do.shape, 1)
                    mine = (lane >= QK_NOPE) if h == 0 else (lane < QK_NOPE)
                    placed = pair_do if h == 0 else pltpu.roll(pair_do, QK_NOPE, 1)
                    dov = jnp.where(lane >= QK_NOPE, placed, 0.0).astype(BF16)
                    delta = jnp.sum(jnp.where(mine, pair_do * o_ref[rows, :].astype(F32), 0.0), axis=-1, keepdims=True)
                    kc = _keys(kvv, kr_ref[keys, :])
                    p = jnp.exp2(_scores(qv, kc, diagonal) - lse_ref[h, rows])
                    dp = lax.dot_general(dov, kvv, NT, preferred_element_type=F32)
                    ds = p * (dp - delta)
                    dv_s[keys, lanes] += lax.dot_general(p.astype(BF16), dov, TN, preferred_element_type=F32)
                    dk_s[keys, lanes] += lax.dot_general(ds.astype(BF16), qv, TN, preferred_element_type=F32)
                    q_rows = pl.ds(pl.multiple_of(qb * t + r0, nr), nr)
                    dq_s[q_rows, lanes] += lax.dot_general(ds.astype(BF16), kc, NN, preferred_element_type=F32)

        @pl.when(qb > kb)
        def _():
            step(False)

        @pl.when(qb == kb)
        def _():
            step(True)

        @pl.when(qb == nb - 1)
        def _():
            lane = lax.broadcasted_iota(jnp.int32, (t, HEAD_PAD), 1)
            rows = pl.ds(pl.multiple_of(kb * t, t), t)
            for h in range(hp):
                lanes = slice(h * HEAD_PAD, (h + 1) * HEAD_PAD)
                dk = dk_s[:, lanes] * LN_2
                dkv_ref[:, lanes] = jnp.where(lane < QK_NOPE, dk, dv_s[:, lanes]).astype(dkv_ref.dtype)
                dkr_ref[rows, :] += jnp.where(lane >= QK_NOPE, dk, 0.0)

        @pl.when(pair == q_tab.shape[0] - 1)
        def _():
            dq_ref[...] = dq_s[...].astype(dq_ref.dtype)

    all_lanes = N_HEADS * HEAD_PAD
    qmap = lambda h, p, qt, kt: (qt[p], h)
    kmap = lambda h, p, qt, kt: (kt[p], h)
    grid_spec = pltpu.PrefetchScalarGridSpec(
        num_scalar_prefetch=2, grid=(N_HEADS // hp, q_tab.shape[0]),
        in_specs=[pl.BlockSpec((t, wide), qmap),
                  pl.BlockSpec((t, wide), kmap),
                  pl.BlockSpec((t, HEAD_PAD), lambda h, p, qt, kt: (kt[p], 0)),
                  pl.BlockSpec((t, HEAD_PAD), qmap),
                  pl.BlockSpec((hp, t, 1), lambda h, p, qt, kt: (h, qt[p], 0)),
                  pl.BlockSpec((t, HEAD_PAD), qmap)],
        out_specs=[pl.BlockSpec((s_len, wide), lambda h, p, qt, kt: (0, h)),
                   pl.BlockSpec((t, wide), kmap),
                   pl.BlockSpec((s_len, HEAD_PAD), lambda h, p, qt, kt: (0, 0))],
        scratch_shapes=[pltpu.VMEM((t, wide), F32), pltpu.VMEM((t, wide), F32), pltpu.VMEM((s_len, wide), F32)])
    return pl.pallas_call(
        body, name="attn_bwd", grid_spec=grid_spec,
        out_shape=[jax.ShapeDtypeStruct((s_len, all_lanes), BF16),
                   jax.ShapeDtypeStruct((s_len, all_lanes), BF16),
                   jax.ShapeDtypeStruct((s_len, HEAD_PAD), F32)],
        compiler_params=_cparams(("arbitrary", "arbitrary")),
    )(q_tab, k_tab, q, kv, kr, o, lse, do)


def _adamw(name, w, g, m, v):
    rows, cols = w.shape
    tr = _div_tile(rows, max(8, (2 * 1024 * 1024) // (4 * cols)), 8)

    def body(w_ref, g_ref, m_ref, v_ref, d_ref, nm_ref, nv_ref):
        gv = g_ref[...]
        nm = ADAM_B1 * m_ref[...] + (1.0 - ADAM_B1) * gv
        nv = ADAM_B2 * v_ref[...] + (1.0 - ADAM_B2) * jnp.square(gv)
        m_hat = nm / (1.0 - ADAM_B1 ** ADAM_STEP)
        v_hat = nv / (1.0 - ADAM_B2 ** ADAM_STEP)
        d_ref[...] = -ADAM_LR * (m_hat / (jnp.sqrt(v_hat) + ADAM_EPS) + ADAM_WD * w_ref[...])
        nm_ref[...] = nm
        nv_ref[...] = nv

    spec = pl.BlockSpec((tr, cols), lambda i: (i, 0))
    return pl.pallas_call(
        body, name=name, grid=(rows // tr,), in_specs=[spec] * 4, out_specs=[spec] * 3,
        out_shape=[jax.ShapeDtypeStruct((rows, cols), F32)] * 3,
        compiler_params=_cparams(("parallel",)),
    )(w, g, m, v)


ALL7 = (1, 2, 3, 4, 5, 6, 7)
CHIPS = (2, 4, 6)


def _all_gather(name, src, masks):
    bits = 0
    for m in masks:
        bits |= m
    nslots = {7: 8, 6: 4}[bits]
    nm = len(masks)

    def slot_of(x, y, c):
        return {7: 4 * x + 2 * y + c, 6: 2 * x + y}[bits]

    def body(src_ref, out_ref, send_sems, recv_sems, local_sem):
        x, y, c = lax.axis_index("x"), lax.axis_index("y"), lax.axis_index("c")
        mine = slot_of(x, y, c)
        own = pltpu.make_async_copy(src_ref, out_ref.at[mine], local_sem)
        own.start()
        copies = []
        for i, m in enumerate(masks):
            peer = _peer(x, y, c, m)
            copies.append((
                pltpu.make_async_remote_copy(
                    src_ref=src_ref, dst_ref=out_ref.at[mine], send_sem=send_sems.at[i], recv_sem=recv_sems.at[i],
                    device_id=peer, device_id_type=MESH),
                pltpu.make_async_remote_copy(
                    src_ref=src_ref, dst_ref=out_ref.at[slot_of(*peer)], send_sem=send_sems.at[i],
                    recv_sem=recv_sems.at[i], device_id=peer, device_id_type=MESH)))
        for send, _ in copies:
            send.start()
        for _, arrival in copies:
            arrival.wait_recv()
        for send, _ in copies:
            send.wait_send()
        own.wait()

    return pl.pallas_call(
        body, name=name,
        in_specs=[pl.BlockSpec(memory_space=pl.ANY)], out_specs=pl.BlockSpec(memory_space=pl.ANY),
        out_shape=jax.ShapeDtypeStruct((nslots,) + tuple(src.shape), src.dtype),
        scratch_shapes=[pltpu.SemaphoreType.DMA((nm,)), pltpu.SemaphoreType.DMA((nm,)), pltpu.SemaphoreType.DMA],
    )(src)


def _peer(x, y, c, m):
    return (1 - x if m & 4 else x, 1 - y if m & 2 else y, 1 - c if m & 1 else c)


def _comm_call(name, emit, srcs, out_shapes, n_sems, in_place=False):
    n = len(srcs)

    def body(*refs):
        src_refs, out_refs = refs[:n], refs[n:n + len(out_shapes)]
        send_sems, recv_sems = refs[-2], refs[-1]

        def copy(src, dst, i, peer):
            return pltpu.make_async_remote_copy(src_ref=src, dst_ref=dst, send_sem=send_sems.at[i],
                                                recv_sem=recv_sems.at[i], device_id=peer, device_id_type=MESH)

        emit(lax.axis_index("x"), lax.axis_index("y"), lax.axis_index("c"), src_refs, out_refs, copy)

    hbm = pl.BlockSpec(memory_space=pl.ANY)
    return pl.pallas_call(
        body, name=name, in_specs=[hbm] * n, out_specs=[hbm] * len(out_shapes), out_shape=out_shapes,
        scratch_shapes=[pltpu.SemaphoreType.DMA((n_sems,)), pltpu.SemaphoreType.DMA((n_sems,))],
        input_output_aliases={i: i for i in range(n)} if in_place else {},
    )(*srcs)


HBM_SPEC = pl.BlockSpec(memory_space=pltpu.HBM)
SEM_SPEC = pl.BlockSpec(memory_space=pltpu.SEMAPHORE)
DATAFLOW = pltpu.SideEffectType.DATAFLOW_SIDE_EFFECTING


def _chip_copies(srcs, lands, send_sems, recv_sems, mode):
    x, y, c = lax.axis_index("x"), lax.axis_index("y"), lax.axis_index("c")
    chip = 2 * x + y
    sends, arrivals = [], []
    for j, m in enumerate(CHIPS):
        px, py, _ = _peer(x, y, c, m)
        theirs = 2 * px + py
        for k, (s, l) in enumerate(zip(srcs, lands)):
            if mode == "gather":
                src, dst, got = s.at[c], l.at[chip, c], l.at[theirs, c]
            else:
                src, dst, got = s.at[theirs], l.at[chip], l.at[theirs]
            for to, group in ((dst, sends), (got, arrivals)):
                group.append(pltpu.make_async_remote_copy(
                    src_ref=src, dst_ref=to, send_sem=send_sems.at[3 * k + j], recv_sem=recv_sems.at[3 * k + j],
                    device_id=(px, py, c), device_id_type=MESH))
    return sends, arrivals


def _split_start(name, srcs, land_shapes, mode, after):
    n = len(srcs)

    def body(*refs):
        sends, _ = _chip_copies(refs[:n], refs[n:2 * n], refs[2 * n + 1], refs[2 * n + 2], mode)
        for cp in sends:
            cp.start()
        token = refs[-1]
        token[...] = jnp.zeros(token.shape, token.dtype)

    hbm = lambda a: pltpu.with_memory_space_constraint(a, pltpu.HBM)
    lands = [hbm(lax.empty(s.shape, s.dtype)) for s in land_shapes]
    bufs = [pltpu.HBM(a.shape, a.dtype) for a in list(srcs) + lands]
    res = pl.pallas_call(
        body, name=name,
        out_shape=(pltpu.SemaphoreType.DMA((3 * n,)), pltpu.SemaphoreType.DMA((3 * n,)), *bufs,
                   jax.ShapeDtypeStruct((SUBLANES, LANE), F32)),
        in_specs=[HBM_SPEC] * (2 * n) + [pl.BlockSpec(memory_space=pl.ANY)],
        out_specs=[SEM_SPEC, SEM_SPEC] + [HBM_SPEC] * (2 * n) + [pl.BlockSpec(memory_space=pltpu.VMEM)],
        input_output_aliases={i: 2 + i for i in range(2 * n)},
        compiler_params=pltpu.CompilerParams(has_side_effects=DATAFLOW),
    )(*[hbm(s) for s in srcs], *lands, after)
    return res[0], res[1], res[2:2 + n], res[2 + n:2 + 2 * n], res[-1]


def _split_wait(name, send_sems, recv_sems, srcs, lands, mode, after):
    n = len(srcs)

    def body(*refs):
        sends, arrivals = _chip_copies(refs[:n], refs[n:2 * n], refs[2 * n], refs[2 * n + 1], mode)
        for cp in sends:
            cp.wait_send()
        for cp in arrivals:
            cp.wait_recv()

    res = pl.pallas_call(
        body, name=name,
        out_shape=tuple(pltpu.HBM(a.shape, a.dtype) for a in list(srcs) + list(lands)),
        in_specs=[HBM_SPEC] * (2 * n) + [SEM_SPEC, SEM_SPEC, pl.BlockSpec(memory_space=pl.ANY)],
        out_specs=[HBM_SPEC] * (2 * n),
        input_output_aliases={i: i for i in range(2 * n)},
        compiler_params=pltpu.CompilerParams(has_side_effects=DATAFLOW),
    )(*srcs, *lands, send_sems, recv_sems, after)
    return res[n:]


def _relay_sibling(lands):
    def emit(x, y, c, srcs, outs, copy):
        sib = (x, y, 1 - c)
        sends, arrivals = [], []
        for j, m in enumerate(CHIPS):
            px, py, _ = _peer(x, y, c, m)
            theirs = 2 * px + py
            for k, (s, o) in enumerate(zip(srcs, outs)):
                sends.append(copy(s.at[theirs, c], o.at[theirs, c], 3 * k + j, sib))
                arrivals.append(copy(s.at[theirs, c], o.at[theirs, 1 - c], 3 * k + j, sib))
        for cp in sends:
            cp.start()
        for cp in arrivals:
            cp.wait_recv()
        for cp in sends:
            cp.wait_send()

    shapes = [jax.ShapeDtypeStruct(l.shape, l.dtype) for l in lands]
    return _comm_call("relay_weights", emit, lands, shapes, 3 * len(lands), in_place=True)


def _gather_weights(halves):
    n = len(halves)

    def emit(x, y, c, srcs, outs, copy):
        chip = 2 * x + y
        sib = (x, y, 1 - c)
        first, relay, landed, relayed = [], [], [], []
        for j, m in enumerate(CHIPS):
            px, py, _ = _peer(x, y, c, m)
            theirs = 2 * px + py
            for k in range(n):
                i = 6 * k + j
                first.append(copy(srcs[k].at[c], outs[k].at[chip, c], i, (px, py, c)))
                landed.append(copy(srcs[k].at[c], outs[k].at[theirs, c], i, (px, py, c)))
                relay.append(copy(outs[k].at[theirs, c], outs[k].at[theirs, c], i + 3, sib))
                relayed.append(copy(outs[k].at[theirs, 1 - c], outs[k].at[theirs, 1 - c], i + 3, sib))
        for cp in first:
            cp.start()
        for arrival, onward in zip(landed, relay):
            arrival.wait_recv()
            onward.start()
        for arrival in relayed:
            arrival.wait_recv()
        for cp in first + relay:
            cp.wait_send()

    shapes = [jax.ShapeDtypeStruct((4,) + h.shape, h.dtype) for h in halves]
    return _comm_call("gather_weights", emit, halves, shapes, 6 * n)


def _pair_exchange(name, chunks):
    def emit(x, y, c, srcs, outs, copy):
        sib = (x, y, 1 - c)
        sends = [copy(s.at[:, 1 - c], o, k, sib) for k, (s, o) in enumerate(zip(srcs, outs))]
        for cp in sends:
            cp.start()
        for cp in sends:
            cp.wait_recv()
        for cp in sends:
            cp.wait_send()

    shapes = [jax.ShapeDtypeStruct((4,) + g.shape[2:], g.dtype) for g in chunks]
    return _comm_call(name, emit, chunks, shapes, len(chunks))


def _share_sibling(name, parts):
    def emit(x, y, c, srcs, outs, copy):
        sib = (x, y, 1 - c)
        sends = [copy(s, o.at[c], k, sib) for k, (s, o) in enumerate(zip(srcs, outs))]
        arrivals = [copy(s, o.at[1 - c], k, sib) for k, (s, o) in enumerate(zip(srcs, outs))]
        for cp in sends:
            cp.start()
        for cp in arrivals:
            cp.wait_recv()
        for cp in sends:
            cp.wait_send()

    shapes = [jax.ShapeDtypeStruct((2,) + p.shape, p.dtype) for p in parts]
    return _comm_call(name, emit, parts, shapes, len(parts))


def _reduce_pair(name, chunk, from_sibling, core):
    n, _, h, cols = chunk.shape
    rt = _div_tile(h, max(16, (1 << 20) // (4 * cols)), 16)

    def body(core_ref, a_ref, b_ref, o_ref):
        o_ref[...] = (a_ref[...] + b_ref[...]).astype(o_ref.dtype)

    grid_spec = pltpu.PrefetchScalarGridSpec(
        num_scalar_prefetch=1, grid=(n, h // rt),
        in_specs=[pl.BlockSpec((None, None, rt, cols), lambda s, i, core_ref: (s, core_ref[0], i, 0)),
                  pl.BlockSpec((None, rt, cols), lambda s, i, core_ref: (s, i, 0))],
        out_specs=pl.BlockSpec((None, rt, cols), lambda s, i, core_ref: (s, i, 0)))
    return pl.pallas_call(
        body, name=name, grid_spec=grid_spec, out_shape=jax.ShapeDtypeStruct((n, h, cols), BF16),
        compiler_params=_cparams(("parallel", "parallel")),
    )(core, chunk, from_sibling)


def _reduce_quad(name, q, after=None):
    _, h, cols = q.shape
    rt = _div_tile(h, max(16, (1 << 20) // (4 * cols)), 16)

    def body(q_ref, *rest):
        v = q_ref[...].astype(F32)
        rest[-1][...] = ((v[0] + v[1]) + v[2]) + v[3]

    held = [] if after is None else [after]
    return pl.pallas_call(
        body, name=name, grid=(h // rt,),
        in_specs=[pl.BlockSpec((4, rt, cols), lambda i: (0, i, 0))] + [pl.BlockSpec(memory_space=pl.ANY)] * len(held),
        out_specs=pl.BlockSpec((rt, cols), lambda i: (i, 0)),
        out_shape=jax.ShapeDtypeStruct((h, cols), F32),
        compiler_params=_cparams(("parallel",)),
    )(q, *held)


def _unshard(seg, kind):
    n, r, c = seg.shape
    if kind == "col":
        return seg.transpose(1, 0, 2).reshape(r, n * c)
    return seg.reshape(n * r, c)


def _pad_rows(flat, rows):
    n, ln = flat.shape
    return jnp.pad(flat, ((0, 0), (0, rows * PACK_COLS - ln))).reshape(n, rows, PACK_COLS)


def _block_diag_pairs(w):
    n2, bs, _ = w.shape
    eye = jnp.eye(2, dtype=w.dtype)
    z = w.reshape(n2 // 2, 2, bs, 1, bs) * eye[None, :, None, :, None]
    return z.reshape(n2 // 2, 2 * bs, 2 * bs).transpose(1, 0, 2).reshape(2 * bs, n2 * bs)


def _block_diag_pairs_t(d, bs=64):
    n = d.shape[1] // (2 * bs)
    z = d.reshape(2 * bs, n, 2 * bs).transpose(1, 0, 2).reshape(n, 2, bs, 2, bs)
    return jnp.stack([z[:, 0, :, 0, :], z[:, 1, :, 1, :]], axis=1).reshape(2 * n, bs, bs)


BIG = (("w_in", "col"), ("w_uq", "col"), ("w_ukv", "col"), ("w_proj_rnn", "row"), ("w_proj_mla", "row"),
       ("w_out", "row"), ("w_up", "col"), ("w_down", "row"))
FIRST_USED = ("w_in", "w_uq", "w_ukv")
CONVS = (("conv_w", "col"), ("ffn_conv_w", "col"))
SMALL = ("b_ada", "norm1_g", "conv_b", "w_gate_a", "b_gate_a", "w_gate_x", "b_gate_x", "lru_param",
         "q_norm_g", "kv_norm_g", "norm2_g", "ffn_conv_b", "final_g")
WEIGHTS = ("w_ada", "b_ada", "norm1_g", "w_in", "conv_w", "conv_b", "w_gate_a", "b_gate_a", "w_gate_x",
           "b_gate_x", "lru_param", "q_norm_g", "w_uq", "kv_norm_g", "w_ukv", "w_proj_rnn", "w_proj_mla",
           "w_out", "norm2_g", "w_up", "ffn_conv_w", "ffn_conv_b", "w_down", "final_g")


def _step(x, c, positions, w, m_in, v_in, loss_target):
    s_len, d = x.shape[1], x.shape[2]
    x2d = x[0]
    tgt = loss_target[0]
    xi, yi, ci = lax.axis_index("x"), lax.axis_index("y"), lax.axis_index("c")
    chip = 2 * xi + yi
    me = 2 * chip + ci
    tile = min(256, s_len)
    nt = s_len // tile

    local2d = {k: w[k][0] for k, _ in BIG + CONVS}
    kinds = dict(BIG)
    halves_bf = {k: local2d[k].astype(BF16).reshape(2, local2d[k].shape[0] // 2, local2d[k].shape[1]) for k, _ in BIG}
    first_names = [k for k, _ in BIG if k in FIRST_USED]
    later_names = [k for k, _ in BIG if k not in FIRST_USED]
    full = {}

    def assemble(k, g):
        g = lax.dynamic_update_index_in_dim(g, halves_bf[k][None], chip, 0).reshape((4,) + local2d[k].shape)
        if k == "w_up":
            full["w_up_gate"], full["w_up_val"] = _unshard(g[:2], kinds[k]), _unshard(g[2:], kinds[k])
        else:
            full[k] = _unshard(g, kinds[k])

    first_got = _gather_weights([halves_bf[k] for k in first_names])
    for k, g in zip(first_names, first_got):
        assemble(k, g)
    conv_flat = jnp.concatenate([local2d[k].reshape(-1) for k, _ in CONVS])
    conv_rows = -(-conv_flat.shape[0] // PACK_COLS)
    conv_all = _all_gather("gather_conv_w", _pad_rows(conv_flat[None], conv_rows)[0], CHIPS)
    conv_all = conv_all.reshape(4, -1)
    off = 0
    for k, kind in CONVS:
        r, cc = local2d[k].shape
        full[k] = _unshard(conv_all[:, off:off + r * cc].reshape(4, r, cc), kind)
        off += r * cc

    d_rnn = w["conv_b"].shape[1]
    n_q, n_kv = w["q_norm_g"].shape[1], w["kv_norm_g"].shape[1]
    w_in = full["w_in"]
    o1, o2, o3 = d_rnn + n_q, d_rnn + n_q + n_kv, d_rnn + n_q + n_kv + QK_ROPE
    w_rnn = w_in[:, :d_rnn]
    zpad = lambda n: jnp.zeros((d, n), BF16)
    w_qkv = jnp.concatenate([w_in[:, d_rnn:o2], zpad(QK_NOPE), w_in[:, o2:o3], zpad(LANE - QK_NOPE - QK_ROPE)], axis=1)
    w_g = w_in[:, o3:]
    hd = QK_NOPE + QK_ROPE
    w_uq = jnp.pad(full["w_uq"].reshape(n_q, N_HEADS, hd), ((0, 0), (0, 0), (0, HEAD_PAD - hd))).reshape(n_q, -1)
    w_ukv = full["w_ukv"]
    v_head = w_ukv.shape[1] // N_HEADS - QK_NOPE
    d_ff = w["ffn_conv_b"].shape[1] // 2
    ffn_cw_gate, ffn_cw_val = full["ffn_conv_w"][:, :d_ff], full["ffn_conv_w"][:, d_ff:]
    ffn_cb_gate, ffn_cb_val = w["ffn_conv_b"][:, :d_ff], w["ffn_conv_b"][:, d_ff:]
    conv_w, conv_b = full["conv_w"], w["conv_b"]
    wa_bd = _block_diag_pairs(w["w_gate_a"][0])
    wx_bd = _block_diag_pairs(w["w_gate_x"][0])

    c_all = _all_gather("gather_c", c, ALL7).reshape(8, d)
    c_rows = 128
    (c_act,) = _tiled("silu_c", lambda v: (_silu(v),), 1, [(jnp.pad(c_all, ((0, c_rows - 8), (0, 0))), (c_rows, d), "full")],
                      [((c_rows, d), F32, (c_rows, d), "full")])
    w_ada = w["w_ada"][0]
    n_mod = w_ada.shape[1]
    b_loc = lax.dynamic_slice_in_dim(w["b_ada"], chip * n_mod, n_mod, axis=1)
    mod_loc = _mm("ada_fwd", c_act, w_ada, add=jnp.broadcast_to(b_loc, (c_rows, n_mod)))
    mod_all = _all_gather("gather_mod", mod_loc[:8], CHIPS)
    mod = lax.dynamic_index_in_dim(mod_all, me, 1, keepdims=False).reshape(1, -1)
    shift1, scale1, gate1, shift2, scale2, gate2 = [mod[:, i * d:(i + 1) * d] for i in range(6)]

    small_done = (mod[:, :1] + conv_all[:1, :1] + first_got[0][0, 0, :1, :1].astype(F32))
    later_flight = _split_start(
        "gather_later_start", [halves_bf[k] for k in later_names],
        [jax.ShapeDtypeStruct((4,) + halves_bf[k].shape, BF16) for k in later_names], "gather", after=small_done)

    half = QK_ROPE // 2
    inv_freq = ROPE_THETA ** (-jnp.arange(half, dtype=F32) / half)
    ang = positions[0].astype(F32)[:, None] * inv_freq
    cos, sin = jnp.cos(ang), jnp.sin(ang)
    one, zero = jnp.ones((s_len, QK_NOPE), F32), jnp.zeros((s_len, half), F32)
    tail = jnp.zeros((s_len, LANE - QK_NOPE - QK_ROPE), F32)
    cos_f = jnp.concatenate([one, cos, cos, tail + 1.0], axis=1)
    sin_a = jnp.concatenate([one * 0.0, -sin, zero, tail], axis=1)
    sin_b = jnp.concatenate([one * 0.0, zero, sin, tail], axis=1)
    reset = (positions[0] == 0).astype(F32)[:, None]
    tabs = [(cos_f, (tile, LANE), "row"), (sin_a, (tile, LANE), "row"), (sin_b, (tile, LANE), "row")]

    def rowspec(a):
        return (a, (tile, a.shape[1]), "row")

    def full2(a):
        return (a, a.shape, "full")

    def rowout(cols, dt):
        return ((s_len, cols), dt, (tile, cols), "row")

    def accout(a):
        return (a.shape, F32, a.shape, "acc")

    norm1_g = w["norm1_g"] + later_flight[4][:1, :1]
    norm2_g, final_g = w["norm2_g"], w["final_g"].reshape(1, d)
    ln1_in = [rowspec(x2d), full2(norm1_g), full2(scale1), full2(shift1)]
    big_tile = min(512, s_len)
    (h1,) = _tiled("ln1", _f_ln, nt, ln1_in, [rowout(d, BF16)], row_tile=big_tile)
    x_rnn = _mm("in_rnn", h1, w_rnn, out_dtype=BF16)
    qkv = _mm("in_qkv", h1, w_qkv)
    gates = _mm("in_gates", h1, w_g, out_dtype=BF16)

    ct = LANE
    n_ct = d_rnn // ct
    colspec = lambda a, width=ct: (a, (a.shape[0], width), "col")
    lru_in = [colspec(x_rnn), colspec(conv_w), colspec(conv_b), colspec(wa_bd), colspec(w["b_gate_a"]),
              colspec(wx_bd), colspec(w["b_gate_x"]), colspec(w["lru_param"]), full2(reset)]
    y_rnn, h_rnn = _tiled("lru_fwd", _f_lru_fwd, n_ct, lru_in,
                          [((s_len, d_rnn), BF16, (s_len, ct), "col"), ((s_len, d_rnn), F32, (s_len, ct), "col")])

    qkv_in = [rowspec(qkv)] + tabs + [full2(w["q_norm_g"]), full2(w["kv_norm_g"])]
    qn, kvn, kr = _tiled("qkv_norm", _f_qkv, nt, qkv_in, [rowout(n_q, BF16), rowout(n_kv, BF16), rowout(LANE, BF16)],
                         row_tile=big_tile)
    q_pre = _mm("up_q", qn, w_uq, out_dtype=BF16)
    kv = _mm("up_kv", kvn, w_ukv, out_dtype=BF16)
    o_mla, lse, q_cat = _attn_fwd(q_pre, (cos_f, sin_a, sin_b), kv, kr)

    send_sems, recv_sems, flown, landed, _ = later_flight
    landed = _split_wait("gather_later_wait", send_sems, recv_sems, flown, landed, "gather", after=o_mla)
    for k, g in zip(later_names, _relay_sibling(landed)):
        assemble(k, g)
    w_pr = full["w_proj_rnn"]
    assert ATTN_HEADS_PER_STEP == 2 and 2 * v_head == HEAD_PAD
    swap_pairs = lambda a: a.reshape(N_HEADS // 2, 2, v_head, d)[:, ::-1].reshape(-1, d)
    w_pm = swap_pairs(full["w_proj_mla"])
    w_out = full["w_out"]
    w_up_gate, w_up_val = full["w_up_gate"], full["w_up_val"]
    w_down = full["w_down"]

    p_rnn = _mm("proj_rnn", y_rnn, w_pr, out_dtype=BF16)
    p_mla = _mm("proj_mla", o_mla, w_pm, out_dtype=BF16)
    merge_in = [rowspec(gates), rowspec(p_rnn), rowspec(p_mla)]
    (merged,) = _tiled("merge", _f_merge, nt, merge_in, [rowout(d, BF16)])
    o_tok = _mm("out_proj", merged, w_out)
    res_in = [rowspec(x2d), rowspec(o_tok), full2(gate1), full2(norm2_g), full2(scale2), full2(shift2)]
    x1, h2 = _tiled("res_ln2", _f_res_ln, nt, res_in, [rowout(d, F32), rowout(d, BF16)], row_tile=big_tile)
    u_gate = _mm("ffn_up_gate", h2, w_up_gate, out_dtype=BF16)
    u_val = _mm("ffn_up_val", h2, w_up_val, out_dtype=BF16)
    n_ft = d_ff // LANE
    ffn_in = [colspec(a) for a in (u_gate, u_val, ffn_cw_gate, ffn_cw_val, ffn_cb_gate, ffn_cb_val)]
    (act,) = _tiled("ffn_conv", _f_ffn, n_ft, ffn_in, [((s_len, d_ff), BF16, (s_len, LANE), "col")])
    f_tok = _mm("ffn_down", act, w_down)

    loss_in = [rowspec(x1), rowspec(f_tok), rowspec(tgt), full2(gate2), full2(final_g)]
    dx1, df, loss_row, d_gate2, d_final_g = _tiled(
        "loss", _f_loss_and_grads, nt, loss_in,
        [rowout(d, F32), rowout(d, BF16), ((1, LANE), F32, (1, LANE), "acc"), accout(gate2), accout(final_g)],
        row_tile=big_tile)
    loss = lax.psum(loss_row[0, 0], ("x", "y", "c"))

    d_act = _mm("ffn_down_dx", df, w_down, tb=True, out_dtype=BF16)
    g_w_down = _mm("ffn_down_dw", act, df, ta=True)
    taps = ffn_cw_gate.shape[0]
    du_gate, du_val, g_cw_gate, g_cw_val, g_cb_gate, g_cb_val = _tiled(
        "ffn_conv_bwd", _vjp_of(_f_ffn, 6, (0, 1, 2, 3, 4, 5)), n_ft, ffn_in + [colspec(d_act)],
        [((s_len, d_ff), BF16, (s_len, LANE), "col")] * 2 + [((taps, d_ff), F32, (taps, LANE), "col")] * 2
        + [((1, d_ff), F32, (1, LANE), "col")] * 2)
    dh2 = _mm("ffn_up_gate_dx", du_gate, w_up_gate, tb=True)
    dh2 = _mm("ffn_up_val_dx", du_val, w_up_val, tb=True, add=dh2, out_dtype=BF16)
    g_w_up_halves = [_mm("ffn_up_gate_dw", h2, du_gate, ta=True), _mm("ffn_up_val_dw", h2, du_val, ta=True)]
    g_ffn_cw = jnp.concatenate([g_cw_gate, g_cw_val], axis=1)
    g_ffn_cb = jnp.concatenate([g_cb_gate, g_cb_val], axis=1)

    res_bwd = _vjp_of(_f_res_ln, 6, (0, 1, 2, 3, 4, 5))
    dx_res, do_tok, d_gate1, g_norm2, d_scale2, d_shift2 = _tiled(
        "res_ln2_bwd", res_bwd, nt, res_in + [rowspec(dx1), rowspec(dh2)],
        [rowout(d, F32), rowout(d, BF16), accout(gate1), accout(norm2_g), accout(scale2), accout(shift2)],
        row_tile=big_tile)
    d_merged = _mm("out_proj_dx", do_tok, w_out, tb=True, out_dtype=BF16)
    g_w_out = _mm("out_proj_dw", merged, do_tok, ta=True)
    d_gates, dp_rnn, dp_mla = _tiled(
        "merge_bwd", _f_merge_bwd, nt, merge_in + [rowspec(d_merged)],
        [rowout(gates.shape[1], BF16), rowout(d, BF16), rowout(d, BF16)])
    dy_rnn = _mm("proj_rnn_dx", dp_rnn, w_pr, tb=True, out_dtype=BF16)
    g_w_pr = _mm("proj_rnn_dw", y_rnn, dp_rnn, ta=True)
    do_mla = _mm("proj_mla_dx", dp_mla, w_pm, tb=True, out_dtype=BF16)
    g_w_pm = _mm("proj_mla_dw", o_mla, dp_mla, ta=True)

    def chunked(k, gk):
        r, cc = local2d[k].shape
        if kinds[k] == "col":
            gk = gk.reshape(r, 4, cc).transpose(1, 0, 2)
        return gk.reshape(4, 2, r // 2, cc)

    def pair_sums(tag, names, chunks):
        out = []
        core = ci.astype(jnp.int32).reshape(1)
        for k, ck, from_sib in zip(names, chunks, _pair_exchange("reduce_pair_exchange_" + tag, chunks)):
            out.append(_reduce_pair("reduce_pair_" + k, ck, from_sib, core))
        return out

    g_later = {
        "w_proj_rnn": g_w_pr,
        "w_proj_mla": swap_pairs(g_w_pm),
        "w_out": g_w_out, "w_down": g_w_down,
    }
    r_up, c_up = local2d["w_up"].shape
    up_chunks = jnp.concatenate([g.reshape(r_up, 2, c_up).transpose(1, 0, 2) for g in g_w_up_halves], axis=0)
    chunks_ready = [up_chunks.reshape(4, 2, r_up // 2, c_up) if k == "w_up" else chunked(k, g_later[k])
                    for k in later_names]
    sums_ready = pair_sums("ready", later_names, chunks_ready)
    ready_flight = _split_start(
        "reduce_ready_start", sums_ready, [jax.ShapeDtypeStruct(s.shape, s.dtype) for s in sums_ready], "alltoall",
        after=sums_ready[0])
    kr_held = kr + ready_flight[4][:1, :].astype(BF16)

    dq_cat, dkv, dkr = _attn_bwd(q_cat, kv, kr_held, o_mla, lse, do_mla)
    (dq_pre,) = _tiled("rot_q_bwd", _f_rotq_bwd, nt, tabs + [rowspec(dq_cat)],
                       [rowout(q_pre.shape[1], BF16)])
    dqn = _mm("up_q_dx", dq_pre, w_uq, tb=True, out_dtype=BF16)
    g_w_uq = _mm("up_q_dw", qn, dq_pre, ta=True)
    dkvn = _mm("up_kv_dx", dkv, w_ukv, tb=True, out_dtype=BF16)
    g_w_ukv = _mm("up_kv_dw", kvn, dkv, ta=True)
    dqkv, g_q_norm, g_kv_norm = _tiled(
        "qkv_norm_bwd", _f_qkv_bwd, nt, qkv_in + [rowspec(dqn), rowspec(dkvn), rowspec(dkr)],
        [rowout(qkv.shape[1], BF16), accout(w["q_norm_g"]), accout(w["kv_norm_g"])], row_tile=big_tile)

    lru_out = [((s_len, d_rnn), BF16, (s_len, ct), "col")]
    for a in (conv_w, conv_b, wa_bd, w["b_gate_a"], wx_bd, w["b_gate_x"], w["lru_param"]):
        lru_out.append((a.shape, F32, (a.shape[0], ct), "col"))
    dx_rnn, g_conv_w, g_conv_b, g_wa_bd, g_b_a, g_wx_bd, g_b_x, g_lru = _tiled(
        "lru_bwd", _f_lru_bwd, n_ct, lru_in + [colspec(h_rnn), colspec(dy_rnn)], lru_out)

    dh1 = _mm("in_gates_dx", d_gates, w_g, tb=True)
    dh1 = _mm("in_qkv_dx", dqkv, w_qkv, tb=True, add=dh1)
    dh1 = _mm("in_rnn_dx", dx_rnn, w_rnn, tb=True, add=dh1)
    g_w_rnn = _mm("in_rnn_dw", h1, dx_rnn, ta=True)
    g_w_qkv = _mm("in_qkv_dw", h1, dqkv, ta=True)
    g_w_g = _mm("in_gates_dw", h1, d_gates, ta=True)

    ln_bwd = _vjp_of(_f_ln, 4, (0, 1, 2, 3))

    def ln1_bwd(xv, gv, sc, sh, dxr, dh):
        dx, dg, dsc, dsh = ln_bwd(xv, gv, sc, sh, dh)
        return dx + dxr, dg, dsc, dsh

    grad_x, g_norm1, d_scale1, d_shift1 = _tiled(
        "ln1_bwd", ln1_bwd, nt, ln1_in + [rowspec(dx_res), rowspec(dh1)],
        [rowout(d, F32), accout(norm1_g), accout(scale1), accout(shift1)], row_tile=big_tile)

    dmod = jnp.concatenate([d_shift1, d_scale1, d_gate1, d_shift2, d_scale2, d_gate2], axis=1)
    dmod_all = _all_gather("gather_dmod", dmod, ALL7).reshape(8, -1)
    dmod_loc = lax.dynamic_slice_in_dim(dmod_all, chip * n_mod, n_mod, axis=1)
    g_w_ada = _mm("ada_dw", c_act, jnp.pad(dmod_loc, ((0, c_rows - 8), (0, 0))), ta=True)

    g_full = {
        "w_in": jnp.concatenate([g_w_rnn, g_w_qkv[:, :n_q + n_kv],
                                 g_w_qkv[:, n_q + n_kv + QK_NOPE:n_q + n_kv + QK_NOPE + QK_ROPE], g_w_g], axis=1),
        "w_uq": g_w_uq.reshape(n_q, N_HEADS, HEAD_PAD)[:, :, :hd].reshape(n_q, -1),
        "w_ukv": g_w_ukv,
        "conv_w": g_conv_w,
        "ffn_conv_w": g_ffn_cw,
    }
    g_small = {
        "b_ada": dmod, "norm1_g": g_norm1, "conv_b": g_conv_b,
        "w_gate_a": _block_diag_pairs_t(g_wa_bd)[None], "b_gate_a": g_b_a,
        "w_gate_x": _block_diag_pairs_t(g_wx_bd)[None], "b_gate_x": g_b_x, "lru_param": g_lru,
        "q_norm_g": g_q_norm, "kv_norm_g": g_kv_norm, "norm2_g": g_norm2,
        "ffn_conv_b": g_ffn_cb, "final_g": d_final_g.reshape(w["final_g"].shape),
    }

    small_flat = jnp.concatenate([g_small[k].reshape(-1) for k in SMALL] + [g_full[k].reshape(-1) for k, _ in CONVS])
    small_rows = -(-small_flat.shape[0] // (8 * PACK_COLS * PACK_ROW_UNIT)) * PACK_ROW_UNIT
    last_names = first_names + ["small"]
    last_chunks = [chunked(k, g_full[k]) for k in first_names]
    last_chunks.append(_pad_rows(small_flat[None], 8 * small_rows).reshape(4, 2, small_rows, PACK_COLS))
    sums_last = pair_sums("last", last_names, last_chunks)
    send_sems, recv_sems, flown, landed, _ = ready_flight
    quads_ready = _split_wait("reduce_ready_wait", send_sems, recv_sems, flown, landed, "alltoall", after=grad_x)
    last_flight = _split_start(
        "reduce_last_start", sums_last, [jax.ShapeDtypeStruct(s.shape, s.dtype) for s in sums_last], "alltoall",
        after=quads_ready[0])
    grads = {"w_ada": g_w_ada[None]}
    delta, new_m, new_v = {}, {}, {}

    def adamw(k):
        shp = w[k].shape
        flip = len(shp) == 3 and shp[-1] % LANE != 0 and shp[-2] % LANE == 0
        view = (lambda a: jnp.swapaxes(a, 1, 2)) if flip else (lambda a: a)
        two_d = (-1, view(w[k]).shape[-1]) if len(shp) > 1 else (1, -1)
        dk, mk, vk = _adamw("adamw_" + k, *[view(a).reshape(two_d) for a in (w[k], grads[k], m_in[k], v_in[k])])
        back = lambda a: view(a.reshape(view(w[k]).shape))
        delta[k], new_m[k], new_v[k] = back(dk), back(mk), back(vk)

    def finish(tag, names, quads, sums, after):
        reduced = {}
        for k, quad, ps in zip(names, quads, sums):
            quad = lax.dynamic_update_index_in_dim(quad, lax.dynamic_index_in_dim(ps, chip, 0, keepdims=True), chip, 0)
            reduced[k] = _reduce_quad("reduce_quad_" + k, quad, after)
        big = [k for k in names if k != "small"]
        for k, both in zip(big, _share_sibling("share_sibling_" + tag, [reduced[k] for k in big])):
            grads[k] = lax.dynamic_update_index_in_dim(both, reduced[k][None], ci, 0).reshape(w[k].shape)
        return reduced

    finish("ready", later_names, quads_ready, sums_ready, after=last_flight[4])
    for k in later_names + ["w_ada"]:
        adamw(k)
    send_sems, recv_sems, flown, landed, _ = last_flight
    quads_last = _split_wait("reduce_last_wait", send_sems, recv_sems, flown, landed, "alltoall",
                             after=delta[later_names[-1]])
    reduced = finish("last", last_names, quads_last, sums_last, after=None)
    small_grad = _all_gather("share_small", reduced["small"], ALL7).reshape(-1)
    off = 0
    for k in SMALL:
        grads[k] = small_grad[off:off + w[k].size].reshape(w[k].shape)
        off += w[k].size
    for k, _ in CONVS:
        r, cc = local2d[k].shape
        whole = small_grad[off:off + 4 * r * cc].reshape(r, 4 * cc)
        grads[k] = lax.dynamic_slice_in_dim(whole, chip * cc, cc, axis=1)[None]
        off += 4 * r * cc
    for k in WEIGHTS:
        if k not in delta:
            adamw(k)

    return (loss, grad_x[None], *[grads[k] for k in WEIGHTS], *[delta[k] for k in WEIGHTS],
            *[new_m[k] for k in WEIGHTS], *[new_v[k] for k in WEIGHTS])


def kernel(x, c, positions, w_ada, b_ada, norm1_g, w_in, conv_w, conv_b, w_gate_a, b_gate_a, w_gate_x, b_gate_x, lru_param, q_norm_g, w_uq, kv_norm_g, w_ukv, w_proj_rnn, w_proj_mla, w_out, norm2_g, w_up, ffn_conv_w, ffn_conv_b, w_down, final_g, loss_target, m_w_ada, m_b_ada, m_norm1_g, m_w_in, m_conv_w, m_conv_b, m_w_gate_a, m_b_gate_a, m_w_gate_x, m_b_gate_x, m_lru_param, m_q_norm_g, m_w_uq, m_kv_norm_g, m_w_ukv, m_w_proj_rnn, m_w_proj_mla, m_w_out, m_norm2_g, m_w_up, m_ffn_conv_w, m_ffn_conv_b, m_w_down, m_final_g, v_w_ada, v_b_ada, v_norm1_g, v_w_in, v_conv_w, v_conv_b, v_w_gate_a, v_b_gate_a, v_w_gate_x, v_b_gate_x, v_lru_param, v_q_norm_g, v_w_uq, v_kv_norm_g, v_w_ukv, v_w_proj_rnn, v_w_proj_mla, v_w_out, v_norm2_g, v_w_up, v_ffn_conv_w, v_ffn_conv_b, v_w_down, v_final_g):
    given = dict(locals())
    w = {k: given[k] for k in WEIGHTS}
    m_in = {k: given["m_" + k] for k in WEIGHTS}
    v_in = {k: given["v_" + k] for k in WEIGHTS}
    return _step(x, c, positions, w, m_in, v_in, loss_target)
```

```python
import functools
import math

import jax
import jax.numpy as jnp
from jax import lax
from jax.experimental import pallas as pl
from jax.experimental.pallas import tpu as pltpu

F32 = jnp.float32
BF16 = jnp.bfloat16

EPS = 1e-6
LRU_C = 8.0
N_HEADS = 16
QK_NOPE = 64
QK_ROPE = 32
HEAD_PAD = 128
ROPE_THETA = 10000.0
ADAM_LR = 0.001
ADAM_B1 = 0.9
ADAM_B2 = 0.999
ADAM_EPS = 1e-08
ADAM_WD = 0.01
ADAM_STEP = 10

LANE = 128
SUBLANES = 8
VMEM_LIMIT = 48 * 1024 * 1024
MM_TILE_M = MM_TILE_N = MM_TILE_K = 1408
PACK_COLS = 1024
PACK_ROW_UNIT = 32
MESH = pl.DeviceIdType.MESH

NN = (((1,), (0,)), ((), ()))
NT = (((1,), (1,)), ((), ()))
TN = (((0,), (0,)), ((), ()))


def _cparams(sem):
    return pltpu.CompilerParams(dimension_semantics=sem, vmem_limit_bytes=VMEM_LIMIT)


def _div_tile(n, cap, unit):
    best = None
    d = unit
    while d <= min(n, cap):
        if n % d == 0:
            best = d
        d += unit
    return n if best is None else best


def _mm(name, a, b, *, ta=False, tb=False, add=None, out_dtype=F32):
    if ta:
        kdim, m = a.shape
    else:
        m, kdim = a.shape
    if tb:
        n, kb = b.shape
    else:
        kb, n = b.shape
    assert kdim == kb, (name, a.shape, b.shape)
    tm = _div_tile(m, MM_TILE_M, 8 if not ta else LANE)
    tn = _div_tile(n, MM_TILE_N, LANE)
    tk = _div_tile(kdim, MM_TILE_K, LANE)
    nk = kdim // tk
    a_spec = pl.BlockSpec((tk, tm), lambda i, j, k: (k, i)) if ta else pl.BlockSpec((tm, tk), lambda i, j, k: (i, k))
    b_spec = pl.BlockSpec((tn, tk), lambda i, j, k: (j, k)) if tb else pl.BlockSpec((tk, tn), lambda i, j, k: (k, j))
    o_spec = pl.BlockSpec((tm, tn), lambda i, j, k: (i, j))
    has_add = add is not None
    dims = ((((0,) if ta else (1,)), ((1,) if tb else (0,))), ((), ()))

    def body(*refs):
        a_ref, b_ref = refs[0], refs[1]
        c_ref = refs[2] if has_add else None
        o_ref = refs[3] if has_add else refs[2]
        prod = lax.dot_general(a_ref[...].astype(BF16), b_ref[...].astype(BF16), dims, preferred_element_type=F32)
        if nk == 1:
            o_ref[...] = (prod + c_ref[...].astype(F32) if has_add else prod).astype(o_ref.dtype)
            return
        acc = refs[-1]
        k = pl.program_id(2)

        @pl.when(k == 0)
        def _():
            acc[...] = prod + c_ref[...].astype(F32) if has_add else prod

        @pl.when(jnp.logical_and(k > 0, k < nk - 1))
        def _():
            acc[...] += prod

        @pl.when(k == nk - 1)
        def _():
            o_ref[...] = (acc[...] + prod).astype(o_ref.dtype)

    ins = [a, b] + ([add] if has_add else [])
    specs = [a_spec, b_spec] + ([o_spec] if has_add else [])
    return pl.pallas_call(
        body, name=name, grid=(m // tm, n // tn, nk), in_specs=specs, out_specs=o_spec,
        out_shape=jax.ShapeDtypeStruct((m, n), out_dtype),
        scratch_shapes=[pltpu.VMEM((tm, tn), F32)] if nk > 1 else [],
        compiler_params=_cparams(("parallel", "parallel", "arbitrary")),
    )(*ins)


_IMAPS = {
    "row": lambda i: (i, 0),
    "col": lambda i: (0, i),
    "full": lambda i: (0, 0),
    "acc": lambda i: (0, 0),
}


def _tiled(name, fn, n, ins, outs, row_tile=None):
    if row_tile is not None:
        rows = next(a.shape[0] for a, _, k in ins if k == "row")
        n = rows // row_tile
        ins = [(a, (row_tile, bs[1]) if k == "row" else bs, k) for a, bs, k in ins]
        outs = [(s, dt, (row_tile, bs[1]) if k == "row" else bs, k) for s, dt, bs, k in outs]
    ni = len(ins)
    is_acc = [k == "acc" for *_, k in outs]

    def body(*refs):
        vals = fn(*[r[...].astype(F32) if r.dtype == BF16 else r[...] for r in refs[:ni]])
        orefs = refs[ni:]
        if any(is_acc):
            @pl.when(pl.program_id(0) == 0)
            def _():
                for r, a in zip(orefs, is_acc):
                    if a:
                        r[...] = jnp.zeros(r.shape, r.dtype)
        for r, v, a in zip(orefs, vals, is_acc):
            if a:
                r[...] += v.astype(r.dtype)
            else:
                r[...] = v.astype(r.dtype)

    res = pl.pallas_call(
        body, name=name, grid=(n,),
        in_specs=[pl.BlockSpec(bs, _IMAPS[k]) for _, bs, k in ins],
        out_specs=[pl.BlockSpec(bs, _IMAPS[k]) for _, _, bs, k in outs],
        out_shape=[jax.ShapeDtypeStruct(s, d) for s, d, _, _ in outs],
        compiler_params=_cparams(("arbitrary",)),
    )(*[a for a, _, _ in ins])
    return tuple(res)


def _vjp_of(fn, nin, diff):
    def g(*args):
        ins, cots = args[:nin], args[nin:]

        def f(*d):
            full = list(ins)
            for i, v in zip(diff, d):
                full[i] = v
            return fn(*full)

        outs, vjp = jax.vjp(f, *[ins[i] for i in diff])
        return vjp(tuple(c.astype(o.dtype) for c, o in zip(cots, outs)))
    return g


def _shift_rows(x, k, fill, up=False):
    n = x.shape[0]
    if k % SUBLANES == 0:
        pad = jnp.full((k,) + x.shape[1:], fill, x.dtype)
        return jnp.concatenate([x[k:], pad], axis=0) if up else jnp.concatenate([pad, x[:n - k]], axis=0)
    rows = lax.broadcasted_iota(jnp.int32, x.shape, 0)
    if up:
        return jnp.where(rows < n - k, pltpu.roll(x, n - k, 0), fill)
    return jnp.where(rows >= k, pltpu.roll(x, k, 0), fill)


@functools.partial(jax.custom_vjp, nondiff_argnums=(1,))
def _delay(x, k):
    return _shift_rows(x, k, 0.0)


def _delay_fwd(x, k):
    return _shift_rows(x, k, 0.0), None


def _delay_bwd(k, _, g):
    return (_shift_rows(g, k, 0.0, up=True),)


_delay.defvjp(_delay_fwd, _delay_bwd)


@functools.partial(jax.custom_vjp, nondiff_argnums=(1,))
def _lane_roll(x, s):
    return pltpu.roll(x, s, 1)


def _lane_roll_fwd(x, s):
    return pltpu.roll(x, s, 1), None


def _lane_roll_bwd(s, _, g):
    return (pltpu.roll(g, g.shape[1] - s, 1),)


_lane_roll.defvjp(_lane_roll_fwd, _lane_roll_bwd)


@jax.custom_vjp
def _bdot(x, w):
    return lax.dot_general(x.astype(BF16), w.astype(BF16), NN, preferred_element_type=F32)


def _bdot_fwd(x, w):
    return _bdot(x, w), (x, w)


def _bdot_bwd(res, g):
    x, w = res
    gb = g.astype(BF16)
    dx = lax.dot_general(gb, w.astype(BF16), NT, preferred_element_type=F32)
    dw = lax.dot_general(x.T.astype(BF16), gb, NN, preferred_element_type=F32)
    return dx, dw


_bdot.defvjp(_bdot_fwd, _bdot_bwd)


def _sigmoid(x):
    return 0.5 * (jnp.tanh(0.5 * x) + 1.0)


def _silu(x):
    return x * _sigmoid(x)


def _rms(x, g):
    return x * lax.rsqrt(jnp.mean(x * x, axis=-1, keepdims=True) + EPS) * g


def _causal_conv(x, w, b):
    kw = w.shape[0]
    tap = lax.broadcasted_iota(jnp.int32, w.shape, 0)
    y = b
    for k in range(kw):
        d = kw - 1 - k
        wk = jnp.sum(jnp.where(tap == k, w, 0.0), axis=0, keepdims=True)
        y = y + wk * (x if d == 0 else _delay(x, d))
    return y


def _rotate(x, cos_f, sin_a, sin_b):
    reps = x.shape[1] // LANE
    if reps > 1:
        cos_f, sin_a, sin_b = (jnp.tile(t, (1, reps)) for t in (cos_f, sin_a, sin_b))
    n = x.shape[1]
    half = QK_ROPE // 2
    return x * cos_f + _lane_roll(x, n - half) * sin_a + _lane_roll(x, half) * sin_b


def _softplus_neg(l):
    u = jnp.exp(-jnp.abs(l))
    log1p_u = jnp.where(u < 0.01, u * (1.0 - u * (0.5 - u * (1.0 / 3.0))), jnp.log(1.0 + u))
    return jnp.maximum(-l, 0.0) + log1p_u


def _f_ln(x, g, scale, shift):
    return (_rms(x, g) * (1.0 + scale) + shift,)


def _f_qkv(qkv, cos_f, sin_a, sin_b, qg, kvg):
    nq, nkv = qg.shape[1], kvg.shape[1]
    qn = _rms(qkv[:, :nq], qg)
    kvn = _rms(qkv[:, nq:nq + nkv], kvg)
    kr = _rotate(qkv[:, nq + nkv:], cos_f, sin_a, sin_b)
    return qn, kvn, kr


def _f_qkv_bwd(qkv, cos_f, sin_a, sin_b, qg, kvg, dqn, dkvn, dkr):
    nq, nkv = qg.shape[1], kvg.shape[1]
    _, vjp_q = jax.vjp(_rms, qkv[:, :nq], qg)
    _, vjp_kv = jax.vjp(_rms, qkv[:, nq:nq + nkv], kvg)
    _, vjp_r = jax.vjp(lambda t: _rotate(t, cos_f, sin_a, sin_b), qkv[:, nq + nkv:])
    dq_lat, dqg = vjp_q(dqn)
    dkv_lat, dkvg = vjp_kv(dkvn)
    (dkr_pre,) = vjp_r(dkr)
    return jnp.concatenate([dq_lat, dkv_lat, dkr_pre], axis=1), dqg, dkvg


QK_SCALE = 1.0 / math.sqrt(QK_NOPE + QK_ROPE)
LOG2_E = 1.4426950408889634
LN_2 = 0.6931471805599453


def _f_rotq(q, cos_f, sin_a, sin_b):
    return (_rotate(q, cos_f, sin_a, sin_b) * (QK_SCALE * LOG2_E),)


def _f_rotq_bwd(cos_f, sin_a, sin_b, dq):
    _, vjp = jax.vjp(lambda t: _rotate(t, cos_f, sin_a, sin_b) * QK_SCALE, jnp.zeros_like(dq))
    return vjp(dq)


def _merge(g_rnn, g_mla, p_rnn, p_mla):
    return _sigmoid(g_rnn) * p_rnn + _sigmoid(g_mla) * p_mla


def _f_merge(g, p_rnn, p_mla):
    d = p_rnn.shape[1]
    return (_merge(g[:, :d], g[:, d:], p_rnn, p_mla),)


def _f_merge_bwd(g, p_rnn, p_mla, dm):
    d = p_rnn.shape[1]
    _, vjp = jax.vjp(_merge, g[:, :d], g[:, d:], p_rnn, p_mla)
    dg_rnn, dg_mla, dp_rnn, dp_mla = vjp(dm)
    return jnp.concatenate([dg_rnn, dg_mla], axis=1), dp_rnn, dp_mla


def _f_res_ln(x, o, gate, g2, scale, shift):
    x1 = x + gate * o
    return x1, _rms(x1, g2) * (1.0 + scale) + shift


def _f_ffn(u_gate, u_val, cw_gate, cw_val, cb_gate, cb_val):
    return (_silu(_causal_conv(u_gate, cw_gate, cb_gate)) * _causal_conv(u_val, cw_val, cb_val),)


def _f_loss(x1, f, tgt, gate, fg):
    y = _rms(x1 + gate * f, fg)
    err = (y - tgt) * (y - tgt)
    return 0.5 * jnp.sum(jnp.mean(err, axis=-1, keepdims=True), axis=0, keepdims=True)


def _f_loss_and_grads(x1, f, tgt, gate, fg):
    loss, vjp = jax.vjp(lambda a, b, c, d: _f_loss(a, b, tgt, c, d), x1, f, gate, fg)
    dx1, df, dgate, dfg = vjp(jnp.ones((1, 1), F32))
    return dx1, df, jnp.broadcast_to(loss, (1, LANE)), dgate, dfg


@jax.custom_vjp
def _decay_and_gain(log_a):
    a = jnp.exp(log_a)
    return a, jnp.sqrt(-jnp.tanh(log_a) * (1.0 + a * a))


def _decay_and_gain_fwd(log_a):
    a, gain = _decay_and_gain(log_a)
    return (a, gain), (a, gain)


def _decay_and_gain_bwd(res, g):
    a, gain = res
    return (g[0] * a - g[1] * (a * a) / gain,)


_decay_and_gain.defvjp(_decay_and_gain_fwd, _decay_and_gain_bwd)


def _f_lru_coeffs(xr, cw, cb, wa, ba, wx, bx, lru, reset):
    xc = _causal_conv(xr, cw, cb)
    r = _sigmoid(_bdot(xc, wa) + ba)
    i = _sigmoid(_bdot(xc, wx) + bx)
    log_a = (-LRU_C) * r * _softplus_neg(lru)
    a, mult = _decay_and_gain(log_a)
    is_reset = reset > 0.5
    a = jnp.where(is_reset, 0.0, a)
    mult = jnp.where(is_reset, 1.0, mult)
    return a, mult * (i * xc)


SCAN_BLOCK = 64


def _scan(a, b, up=False):
    n = a.shape[0]
    blk = min(SCAN_BLOCK, n)
    pos = lax.broadcasted_iota(jnp.int32, a.shape, 0) % blk
    k = 1
    while k < blk:
        inside = (pos < blk - k) if up else (pos >= k)
        shift = n - k if up else k
        b = b + a * jnp.where(inside, pltpu.roll(b, shift, 0), 0.0)
        a = a * jnp.where(inside, pltpu.roll(a, shift, 0), 1.0)
        k *= 2
    blocks = range(n // blk)
    carry = jnp.zeros((1,) + a.shape[1:], a.dtype)
    out = [None] * len(blocks)
    for i in (reversed(blocks) if up else blocks):
        rows = slice(i * blk, (i + 1) * blk)
        out[i] = b[rows] + a[rows] * carry
        carry = out[i][:1] if up else out[i][blk - 1:]
    return jnp.concatenate(out, axis=0)


def _f_lru_fwd(xr, cw, cb, wa, ba, wx, bx, lru, reset):
    a, b = _f_lru_coeffs(xr, cw, cb, wa, ba, wx, bx, lru, reset)
    h = _scan(a, b)
    return h, h


def _f_lru_bwd(xr, cw, cb, wa, ba, wx, bx, lru, reset, h, dh):
    (a, _), vjp = jax.vjp(lambda *p: _f_lru_coeffs(*p, reset), xr, cw, cb, wa, ba, wx, bx, lru)
    g = _scan(_shift_rows(a, 1, 0.0, up=True), dh, up=True)
    return vjp((g * _shift_rows(h, 1, 0.0), g))


def _attn_tile(s):
    return 1024 if s >= 2048 else s // 2


def _keys(kv, kr):
    lane = lax.broadcasted_iota(jnp.int32, kv.shape, 1)
    return jnp.where(lane < QK_NOPE, kv, kr)


ATTN_HEADS_PER_STEP = 2


def _scores(q, kc, diagonal):
    s = lax.dot_general(q, kc, NT, preferred_element_type=F32)
    if not diagonal:
        return s
    rows = lax.broadcasted_iota(jnp.int32, s.shape, 0)
    cols = lax.broadcasted_iota(jnp.int32, s.shape, 1)
    return jnp.where(cols - (s.shape[1] - s.shape[0]) <= rows, s, -jnp.inf)


def _sub_blocks(t, diagonal):
    return ((0, t // 2, t // 2), (t // 2, t // 2, t)) if diagonal else ((0, t, t),)


def _causal_pairs(nb, k_major):
    if k_major:
        pairs = [(qb, kb) for kb in range(nb) for qb in range(kb, nb)]
    else:
        pairs = [(qb, kb) for qb in range(nb) for kb in range(qb + 1)]
    return jnp.array([p[0] for p in pairs], jnp.int32), jnp.array([p[1] for p in pairs], jnp.int32)


def _attn_fwd(q_pre, tables, kv, kr):
    s_len = q_pre.shape[0]
    t = _attn_tile(s_len)
    nb = s_len // t
    hp = ATTN_HEADS_PER_STEP
    wide = hp * HEAD_PAD
    q_tab, k_tab = _causal_pairs(nb, k_major=False)

    def body(qt, kt, qp_ref, cos_ref, sina_ref, sinb_ref, kv_ref, kr_ref, o_ref, lse_ref, q_ref, m_s, acc_s):
        pair = pl.program_id(1)
        qi, ki = qt[pair], kt[pair]

        @pl.when(ki == 0)
        def _():
            m_s[...] = jnp.full(m_s.shape, -jnp.inf, F32)
            acc_s[...] = jnp.zeros(acc_s.shape, F32)
            (rotated,) = _f_rotq(qp_ref[...].astype(F32), cos_ref[...], sina_ref[...], sinb_ref[...])
            q_ref[...] = rotated.astype(q_ref.dtype)

        def step(diagonal):
            for h in range(hp):
                lanes = slice(h * HEAD_PAD, (h + 1) * HEAD_PAD)
                for r0, nr, nk in _sub_blocks(t, diagonal):
                    rows = slice(r0, r0 + nr)
                    kvv = kv_ref[:nk, lanes]
                    s = _scores(q_ref[rows, lanes], _keys(kvv, kr_ref[:nk, :]), diagonal)
                    m_old = m_s[h, rows]
                    m_new = jnp.maximum(m_old, jnp.max(s, axis=-1, keepdims=True))
                    alpha = jnp.exp2(m_old - m_new)
                    p = jnp.exp2(s - m_new)
                    lane = lax.broadcasted_iota(jnp.int32, kvv.shape, 1)
                    ones_and_values = jnp.where(lane < QK_NOPE, jnp.ones_like(kvv), kvv)
                    acc_s[rows, lanes] = alpha * acc_s[rows, lanes] + lax.dot_general(
                        p.astype(BF16), ones_and_values, NN, preferred_element_type=F32)
                    m_s[h, rows] = m_new

        @pl.when(ki < qi)
        def _():
            step(False)

        @pl.when(ki == qi)
        def _():
            step(True)
            lane = lax.broadcasted_iota(jnp.int32, (t, HEAD_PAD), 1)
            outs = []
            for h in range(hp):
                acc = acc_s[:, h * HEAD_PAD:(h + 1) * HEAD_PAD]
                total = acc[:, :1]
                outs.append(acc / total)
                lse_ref[h] = m_s[h] + jnp.log(total) * LOG2_E
            o_ref[...] = jnp.where(lane >= QK_NOPE, outs[0], pltpu.roll(outs[1], QK_NOPE, 1)).astype(o_ref.dtype)

    q_rows = lambda h, p, qt, kt: (qt[p], 0)
    grid_spec = pltpu.PrefetchScalarGridSpec(
        num_scalar_prefetch=2, grid=(N_HEADS // hp, q_tab.shape[0]),
        in_specs=[pl.BlockSpec((t, wide), lambda h, p, qt, kt: (qt[p], h)),
                  pl.BlockSpec((t, HEAD_PAD), q_rows), pl.BlockSpec((t, HEAD_PAD), q_rows),
                  pl.BlockSpec((t, HEAD_PAD), q_rows),
                  pl.BlockSpec((t, wide), lambda h, p, qt, kt: (kt[p], h)),
                  pl.BlockSpec((t, HEAD_PAD), lambda h, p, qt, kt: (kt[p], 0))],
        out_specs=[pl.BlockSpec((t, HEAD_PAD), lambda h, p, qt, kt: (qt[p], h)),
                   pl.BlockSpec((hp, t, 1), lambda h, p, qt, kt: (h, qt[p], 0)),
                   pl.BlockSpec((t, wide), lambda h, p, qt, kt: (qt[p], h))],
        scratch_shapes=[pltpu.VMEM((hp, t, 1), F32), pltpu.VMEM((t, wide), F32)])
    return pl.pallas_call(
        body, name="attn_fwd", grid_spec=grid_spec,
        out_shape=[jax.ShapeDtypeStruct((s_len, N_HEADS // hp * HEAD_PAD), BF16),
                   jax.ShapeDtypeStruct((N_HEADS, s_len, 1), F32),
                   jax.ShapeDtypeStruct((s_len, N_HEADS * HEAD_PAD), BF16)],
        compiler_params=_cparams(("arbitrary", "arbitrary")),
    )(q_tab, k_tab, q_pre, *tables, kv, kr)


def _attn_bwd(q, kv, kr, o, lse, do):
    s_len = q.shape[0]
    t = _attn_tile(s_len)
    nb = s_len // t
    hp = ATTN_HEADS_PER_STEP
    wide = hp * HEAD_PAD
    q_tab, k_tab = _causal_pairs(nb, k_major=True)

    def body(qt, kt, q_ref, kv_ref, kr_ref, o_ref, lse_ref, do_ref, dq_ref, dkv_ref, dkr_ref, dk_s, dv_s, dq_s):
        g, pair = pl.program_id(0), pl.program_id(1)
        qb, kb = qt[pair], kt[pair]

        @pl.when(jnp.logical_and(g == 0, pair == 0))
        def _():
            dkr_ref[...] = jnp.zeros(dkr_ref.shape, F32)

        @pl.when(pair == 0)
        def _():
            dq_s[...] = jnp.zeros(dq_s.shape, F32)

        @pl.when(qb == kb)
        def _():
            dk_s[...] = jnp.zeros(dk_s.shape, F32)
            dv_s[...] = jnp.zeros(dv_s.shape, F32)

        def step(diagonal):
            for h in range(hp):
                lanes = slice(h * HEAD_PAD, (h + 1) * HEAD_PAD)
                for r0, nr, nk in _sub_blocks(t, diagonal):
                    rows, keys = slice(r0, r0 + nr), slice(0, nk)
                    qv, kvv = q_ref[rows, lanes], kv_ref[keys, lanes]
                    pair_do = do_ref[rows, :].astype(F32)
                    lane = lax.broadcasted_iota(jnp.int32, pair_do.shape, 1)
                    mine = (lane >= QK_NOPE) if h == 0 else (lane < QK_NOPE)
                    placed = pair_do if h == 0 else pltpu.roll(pair_do, QK_NOPE, 1)
                    dov = jnp.where(lane >= QK_NOPE, placed, 0.0).astype(BF16)
                    delta = jnp.sum(jnp.where(mine, pair_do * o_ref[rows, :].astype(F32), 0.0), axis=-1, keepdims=True)
                    kc = _keys(kvv, kr_ref[keys, :])
                    p = jnp.exp2(_scores(qv, kc, diagonal) - lse_ref[h, rows])
                    dp = lax.dot_general(dov, kvv, NT, preferred_element_type=F32)
                    ds = p * (dp - delta)
                    dv_s[keys, lanes] += lax.dot_general(p.astype(BF16), dov, TN, preferred_element_type=F32)
                    dk_s[keys, lanes] += lax.dot_general(ds.astype(BF16), qv, TN, preferred_element_type=F32)
                    q_rows = pl.ds(pl.multiple_of(qb * t + r0, nr), nr)
                    dq_s[q_rows, lanes] += lax.dot_general(ds.astype(BF16), kc, NN, preferred_element_type=F32)

        @pl.when(qb > kb)
        def _():
            step(False)

        @pl.when(qb == kb)
        def _():
            step(True)

        @pl.when(qb == nb - 1)
        def _():
            lane = lax.broadcasted_iota(jnp.int32, (t, HEAD_PAD), 1)
            rows = pl.ds(pl.multiple_of(kb * t, t), t)
            for h in range(hp):
                lanes = slice(h * HEAD_PAD, (h + 1) * HEAD_PAD)
                dk = dk_s[:, lanes] * LN_2
                dkv_ref[:, lanes] = jnp.where(lane < QK_NOPE, dk, dv_s[:, lanes]).astype(dkv_ref.dtype)
                dkr_ref[rows, :] += jnp.where(lane >= QK_NOPE, dk, 0.0)

        @pl.when(pair == q_tab.shape[0] - 1)
        def _():
            dq_ref[...] = dq_s[...].astype(dq_ref.dtype)

    all_lanes = N_HEADS * HEAD_PAD
    qmap = lambda h, p, qt, kt: (qt[p], h)
    kmap = lambda h, p, qt, kt: (kt[p], h)
    grid_spec = pltpu.PrefetchScalarGridSpec(
        num_scalar_prefetch=2, grid=(N_HEADS // hp, q_tab.shape[0]),
        in_specs=[pl.BlockSpec((t, wide), qmap),
                  pl.BlockSpec((t, wide), kmap),
                  pl.BlockSpec((t, HEAD_PAD), lambda h, p, qt, kt: (kt[p], 0)),
                  pl.BlockSpec((t, HEAD_PAD), qmap),
                  pl.BlockSpec((hp, t, 1), lambda h, p, qt, kt: (h, qt[p], 0)),
                  pl.BlockSpec((t, HEAD_PAD), qmap)],
        out_specs=[pl.BlockSpec((s_len, wide), lambda h, p, qt, kt: (0, h)),
                   pl.BlockSpec((t, wide), kmap),
                   pl.BlockSpec((s_len, HEAD_PAD), lambda h, p, qt, kt: (0, 0))],
        scratch_shapes=[pltpu.VMEM((t, wide), F32), pltpu.VMEM((t, wide), F32), pltpu.VMEM((s_len, wide), F32)])
    return pl.pallas_call(
        body, name="attn_bwd", grid_spec=grid_spec,
        out_shape=[jax.ShapeDtypeStruct((s_len, all_lanes), BF16),
                   jax.ShapeDtypeStruct((s_len, all_lanes), BF16),
                   jax.ShapeDtypeStruct((s_len, HEAD_PAD), F32)],
        compiler_params=_cparams(("arbitrary", "arbitrary")),
    )(q_tab, k_tab, q, kv, kr, o, lse, do)


def _adamw(name, w, g, m, v):
    rows, cols = w.shape
    tr = _div_tile(rows, max(8, (2 * 1024 * 1024) // (4 * cols)), 8)

    def body(w_ref, g_ref, m_ref, v_ref, d_ref, nm_ref, nv_ref):
        gv = g_ref[...]
        nm = ADAM_B1 * m_ref[...] + (1.0 - ADAM_B1) * gv
        nv = ADAM_B2 * v_ref[...] + (1.0 - ADAM_B2) * jnp.square(gv)
        m_hat = nm / (1.0 - ADAM_B1 ** ADAM_STEP)
        v_hat = nv / (1.0 - ADAM_B2 ** ADAM_STEP)
        d_ref[...] = -ADAM_LR * (m_hat / (jnp.sqrt(v_hat) + ADAM_EPS) + ADAM_WD * w_ref[...])
        nm_ref[...] = nm
        nv_ref[...] = nv

    spec = pl.BlockSpec((tr, cols), lambda i: (i, 0))
    return pl.pallas_call(
        body, name=name, grid=(rows // tr,), in_specs=[spec] * 4, out_specs=[spec] * 3,
        out_shape=[jax.ShapeDtypeStruct((rows, cols), F32)] * 3,
        compiler_params=_cparams(("parallel",)),
    )(w, g, m, v)


ALL7 = (1, 2, 3, 4, 5, 6, 7)
CHIPS = (2, 4, 6)


def _all_gather(name, src, masks):
    bits = 0
    for m in masks:
        bits |= m
    nslots = {7: 8, 6: 4}[bits]
    nm = len(masks)

    def slot_of(x, y, c):
        return {7: 4 * x + 2 * y + c, 6: 2 * x + y}[bits]

    def body(src_ref, out_ref, send_sems, recv_sems, local_sem):
        x, y, c = lax.axis_index("x"), lax.axis_index("y"), lax.axis_index("c")
        mine = slot_of(x, y, c)
        own = pltpu.make_async_copy(src_ref, out_ref.at[mine], local_sem)
        own.start()
        copies = []
        for i, m in enumerate(masks):
            peer = _peer(x, y, c, m)
            copies.append((
                pltpu.make_async_remote_copy(
                    src_ref=src_ref, dst_ref=out_ref.at[mine], send_sem=send_sems.at[i], recv_sem=recv_sems.at[i],
                    device_id=peer, device_id_type=MESH),
                pltpu.make_async_remote_copy(
                    src_ref=src_ref, dst_ref=out_ref.at[slot_of(*peer)], send_sem=send_sems.at[i],
                    recv_sem=recv_sems.at[i], device_id=peer, device_id_type=MESH)))
        for send, _ in copies:
            send.start()
        for _, arrival in copies:
            arrival.wait_recv()
        for send, _ in copies:
            send.wait_send()
        own.wait()

    return pl.pallas_call(
        body, name=name,
        in_specs=[pl.BlockSpec(memory_space=pl.ANY)], out_specs=pl.BlockSpec(memory_space=pl.ANY),
        out_shape=jax.ShapeDtypeStruct((nslots,) + tuple(src.shape), src.dtype),
        scratch_shapes=[pltpu.SemaphoreType.DMA((nm,)), pltpu.SemaphoreType.DMA((nm,)), pltpu.SemaphoreType.DMA],
    )(src)


def _peer(x, y, c, m):
    return (1 - x if m & 4 else x, 1 - y if m & 2 else y, 1 - c if m & 1 else c)


def _comm_call(name, emit, srcs, out_shapes, n_sems, in_place=False):
    n = len(srcs)

    def body(*refs):
        src_refs, out_refs = refs[:n], refs[n:n + len(out_shapes)]
        send_sems, recv_sems = refs[-2], refs[-1]

        def copy(src, dst, i, peer):
            return pltpu.make_async_remote_copy(src_ref=src, dst_ref=dst, send_sem=send_sems.at[i],
                                                recv_sem=recv_sems.at[i], device_id=peer, device_id_type=MESH)

        emit(lax.axis_index("x"), lax.axis_index("y"), lax.axis_index("c"), src_refs, out_refs, copy)

    hbm = pl.BlockSpec(memory_space=pl.ANY)
    return pl.pallas_call(
        body, name=name, in_specs=[hbm] * n, out_specs=[hbm] * len(out_shapes), out_shape=out_shapes,
        scratch_shapes=[pltpu.SemaphoreType.DMA((n_sems,)), pltpu.SemaphoreType.DMA((n_sems,))],
        input_output_aliases={i: i for i in range(n)} if in_place else {},
    )(*srcs)


HBM_SPEC = pl.BlockSpec(memory_space=pltpu.HBM)
SEM_SPEC = pl.BlockSpec(memory_space=pltpu.SEMAPHORE)
DATAFLOW = pltpu.SideEffectType.DATAFLOW_SIDE_EFFECTING


def _chip_copies(srcs, lands, send_sems, recv_sems, mode):
    x, y, c = lax.axis_index("x"), lax.axis_index("y"), lax.axis_index("c")
    chip = 2 * x + y
    sends, arrivals = [], []
    if mode == "pair":
        for k, (s, l) in enumerate(zip(srcs, lands)):
            for group in (sends, arrivals):
                group.append(pltpu.make_async_remote_copy(
                    src_ref=s.at[:, 1 - c], dst_ref=l, send_sem=send_sems.at[3 * k], recv_sem=recv_sems.at[3 * k],
                    device_id=(x, y, 1 - c), device_id_type=MESH))
        return sends, arrivals
    for j, m in enumerate(CHIPS):
        px, py, _ = _peer(x, y, c, m)
        theirs = 2 * px + py
        for k, (s, l) in enumerate(zip(srcs, lands)):
            if mode == "gather":
                src, dst, got = s.at[c], l.at[chip, c], l.at[theirs, c]
            else:
                src, dst, got = s.at[theirs], l.at[chip], l.at[theirs]
            for to, group in ((dst, sends), (got, arrivals)):
                group.append(pltpu.make_async_remote_copy(
                    src_ref=src, dst_ref=to, send_sem=send_sems.at[3 * k + j], recv_sem=recv_sems.at[3 * k + j],
                    device_id=(px, py, c), device_id_type=MESH))
    return sends, arrivals


def _split_start(name, srcs, land_shapes, mode, after):
    n = len(srcs)

    def body(*refs):
        sends, _ = _chip_copies(refs[:n], refs[n:2 * n], refs[2 * n + 1], refs[2 * n + 2], mode)
        for cp in sends:
            cp.start()
        token = refs[-1]
        token[...] = jnp.zeros(token.shape, token.dtype)

    hbm = lambda a: pltpu.with_memory_space_constraint(a, pltpu.HBM)
    lands = [hbm(lax.empty(s.shape, s.dtype)) for s in land_shapes]
    bufs = [pltpu.HBM(a.shape, a.dtype) for a in list(srcs) + lands]
    res = pl.pallas_call(
        body, name=name,
        out_shape=(pltpu.SemaphoreType.DMA((3 * n,)), pltpu.SemaphoreType.DMA((3 * n,)), *bufs,
                   jax.ShapeDtypeStruct((SUBLANES, LANE), F32)),
        in_specs=[HBM_SPEC] * (2 * n) + [pl.BlockSpec(memory_space=pl.ANY)],
        out_specs=[SEM_SPEC, SEM_SPEC] + [HBM_SPEC] * (2 * n) + [pl.BlockSpec(memory_space=pltpu.VMEM)],
        input_output_aliases={i: 2 + i for i in range(2 * n)},
        compiler_params=pltpu.CompilerParams(has_side_effects=DATAFLOW),
    )(*[hbm(s) for s in srcs], *lands, after)
    return res[0], res[1], res[2:2 + n], res[2 + n:2 + 2 * n], res[-1]


def _split_wait(name, send_sems, recv_sems, srcs, lands, mode, after):
    n = len(srcs)

    def body(*refs):
        sends, arrivals = _chip_copies(refs[:n], refs[n:2 * n], refs[2 * n], refs[2 * n + 1], mode)
        for cp in sends:
            cp.wait_send()
        for cp in arrivals:
            cp.wait_recv()

    res = pl.pallas_call(
        body, name=name,
        out_shape=tuple(pltpu.HBM(a.shape, a.dtype) for a in list(srcs) + list(lands)),
        in_specs=[HBM_SPEC] * (2 * n) + [SEM_SPEC, SEM_SPEC, pl.BlockSpec(memory_space=pl.ANY)],
        out_specs=[HBM_SPEC] * (2 * n),
        input_output_aliases={i: i for i in range(2 * n)},
        compiler_params=pltpu.CompilerParams(has_side_effects=DATAFLOW),
    )(*srcs, *lands, send_sems, recv_sems, after)
    return res[n:]


def _relay_sibling(lands):
    def emit(x, y, c, srcs, outs, copy):
        sib = (x, y, 1 - c)
        sends, arrivals = [], []
        for j, m in enumerate(CHIPS):
            px, py, _ = _peer(x, y, c, m)
            theirs = 2 * px + py
            for k, (s, o) in enumerate(zip(srcs, outs)):
                sends.append(copy(s.at[theirs, c], o.at[theirs, c], 3 * k + j, sib))
                arrivals.append(copy(s.at[theirs, c], o.at[theirs, 1 - c], 3 * k + j, sib))
        for cp in sends:
            cp.start()
        for cp in arrivals:
            cp.wait_recv()
        for cp in sends:
            cp.wait_send()

    shapes = [jax.ShapeDtypeStruct(l.shape, l.dtype) for l in lands]
    return _comm_call("relay_weights", emit, lands, shapes, 3 * len(lands), in_place=True)


def _gather_weights(halves):
    n = len(halves)

    def emit(x, y, c, srcs, outs, copy):
        chip = 2 * x + y
        sib = (x, y, 1 - c)
        first, relay, landed, relayed = [], [], [], []
        for j, m in enumerate(CHIPS):
            px, py, _ = _peer(x, y, c, m)
            theirs = 2 * px + py
            for k in range(n):
                i = 6 * k + j
                first.append(copy(srcs[k].at[c], outs[k].at[chip, c], i, (px, py, c)))
                landed.append(copy(srcs[k].at[c], outs[k].at[theirs, c], i, (px, py, c)))
                relay.append(copy(outs[k].at[theirs, c], outs[k].at[theirs, c], i + 3, sib))
                relayed.append(copy(outs[k].at[theirs, 1 - c], outs[k].at[theirs, 1 - c], i + 3, sib))
        for cp in first:
            cp.start()
        for arrival, onward in zip(landed, relay):
            arrival.wait_recv()
            onward.start()
        for arrival in relayed:
            arrival.wait_recv()
        for cp in first + relay:
            cp.wait_send()

    shapes = [jax.ShapeDtypeStruct((4,) + h.shape, h.dtype) for h in halves]
    return _comm_call("gather_weights", emit, halves, shapes, 6 * n)


def _pair_exchange(name, chunks):
    def emit(x, y, c, srcs, outs, copy):
        sib = (x, y, 1 - c)
        sends = [copy(s.at[:, 1 - c], o, k, sib) for k, (s, o) in enumerate(zip(srcs, outs))]
        for cp in sends:
            cp.start()
        for cp in sends:
            cp.wait_recv()
        for cp in sends:
            cp.wait_send()

    shapes = [jax.ShapeDtypeStruct((4,) + g.shape[2:], g.dtype) for g in chunks]
    return _comm_call(name, emit, chunks, shapes, len(chunks))


def _share_sibling(name, parts):
    def emit(x, y, c, srcs, outs, copy):
        sib = (x, y, 1 - c)
        sends = [copy(s, o.at[c], k, sib) for k, (s, o) in enumerate(zip(srcs, outs))]
        arrivals = [copy(s, o.at[1 - c], k, sib) for k, (s, o) in enumerate(zip(srcs, outs))]
        for cp in sends:
            cp.start()
        for cp in arrivals:
            cp.wait_recv()
        for cp in sends:
            cp.wait_send()

    shapes = [jax.ShapeDtypeStruct((2,) + p.shape, p.dtype) for p in parts]
    return _comm_call(name, emit, parts, shapes, len(parts))


def _reduce_pair(name, chunk, from_sibling, core):
    n, _, h, cols = chunk.shape
    rt = _div_tile(h, max(16, (1 << 20) // (4 * cols)), 16)

    def body(core_ref, a_ref, b_ref, o_ref):
        o_ref[...] = (a_ref[...] + b_ref[...]).astype(o_ref.dtype)

    grid_spec = pltpu.PrefetchScalarGridSpec(
        num_scalar_prefetch=1, grid=(n, h // rt),
        in_specs=[pl.BlockSpec((None, None, rt, cols), lambda s, i, core_ref: (s, core_ref[0], i, 0)),
                  pl.BlockSpec((None, rt, cols), lambda s, i, core_ref: (s, i, 0))],
        out_specs=pl.BlockSpec((None, rt, cols), lambda s, i, core_ref: (s, i, 0)))
    return pl.pallas_call(
        body, name=name, grid_spec=grid_spec, out_shape=jax.ShapeDtypeStruct((n, h, cols), BF16),
        compiler_params=_cparams(("parallel", "parallel")),
    )(core, chunk, from_sibling)


def _reduce_quad(name, q, after=None):
    _, h, cols = q.shape
    rt = _div_tile(h, max(16, (1 << 20) // (4 * cols)), 16)

    def body(q_ref, *rest):
        v = q_ref[...].astype(F32)
        rest[-1][...] = ((v[0] + v[1]) + v[2]) + v[3]

    held = [] if after is None else [after]
    return pl.pallas_call(
        body, name=name, grid=(h // rt,),
        in_specs=[pl.BlockSpec((4, rt, cols), lambda i: (0, i, 0))] + [pl.BlockSpec(memory_space=pl.ANY)] * len(held),
        out_specs=pl.BlockSpec((rt, cols), lambda i: (i, 0)),
        out_shape=jax.ShapeDtypeStruct((h, cols), F32),
        compiler_params=_cparams(("parallel",)),
    )(q, *held)


def _unshard(seg, kind):
    n, r, c = seg.shape
    if kind == "col":
        return seg.transpose(1, 0, 2).reshape(r, n * c)
    return seg.reshape(n * r, c)


def _pad_rows(flat, rows):
    n, ln = flat.shape
    return jnp.pad(flat, ((0, 0), (0, rows * PACK_COLS - ln))).reshape(n, rows, PACK_COLS)


def _block_diag_pairs(w):
    n2, bs, _ = w.shape
    eye = jnp.eye(2, dtype=w.dtype)
    z = w.reshape(n2 // 2, 2, bs, 1, bs) * eye[None, :, None, :, None]
    return z.reshape(n2 // 2, 2 * bs, 2 * bs).transpose(1, 0, 2).reshape(2 * bs, n2 * bs)


def _block_diag_pairs_t(d, bs=64):
    n = d.shape[1] // (2 * bs)
    z = d.reshape(2 * bs, n, 2 * bs).transpose(1, 0, 2).reshape(n, 2, bs, 2, bs)
    return jnp.stack([z[:, 0, :, 0, :], z[:, 1, :, 1, :]], axis=1).reshape(2 * n, bs, bs)


BIG = (("w_in", "col"), ("w_uq", "col"), ("w_ukv", "col"), ("w_proj_rnn", "row"), ("w_proj_mla", "row"),
       ("w_out", "row"), ("w_up", "col"), ("w_down", "row"))
FIRST_USED = ("w_in", "w_uq", "w_ukv")
CONVS = (("conv_w", "col"), ("ffn_conv_w", "col"))
SMALL = ("b_ada", "norm1_g", "conv_b", "w_gate_a", "b_gate_a", "w_gate_x", "b_gate_x", "lru_param",
         "q_norm_g", "kv_norm_g", "norm2_g", "ffn_conv_b", "final_g")
WEIGHTS = ("w_ada", "b_ada", "norm1_g", "w_in", "conv_w", "conv_b", "w_gate_a", "b_gate_a", "w_gate_x",
           "b_gate_x", "lru_param", "q_norm_g", "w_uq", "kv_norm_g", "w_ukv", "w_proj_rnn", "w_proj_mla",
           "w_out", "norm2_g", "w_up", "ffn_conv_w", "ffn_conv_b", "w_down", "final_g")


def _step(x, c, positions, w, m_in, v_in, loss_target):
    s_len, d = x.shape[1], x.shape[2]
    x2d = x[0]
    tgt = loss_target[0]
    xi, yi, ci = lax.axis_index("x"), lax.axis_index("y"), lax.axis_index("c")
    chip = 2 * xi + yi
    me = 2 * chip + ci
    tile = min(256, s_len)
    nt = s_len // tile

    local2d = {k: w[k][0] for k, _ in BIG + CONVS}
    kinds = dict(BIG)
    halves_bf = {k: local2d[k].astype(BF16).reshape(2, local2d[k].shape[0] // 2, local2d[k].shape[1]) for k, _ in BIG}
    first_names = [k for k, _ in BIG if k in FIRST_USED]
    later_names = [k for k, _ in BIG if k not in FIRST_USED]
    full = {}

    def assemble(k, g):
        g = lax.dynamic_update_index_in_dim(g, halves_bf[k][None], chip, 0).reshape((4,) + local2d[k].shape)
        if k == "w_up":
            full["w_up_gate"], full["w_up_val"] = _unshard(g[:2], kinds[k]), _unshard(g[2:], kinds[k])
        else:
            full[k] = _unshard(g, kinds[k])

    first_got = _gather_weights([halves_bf[k] for k in first_names])
    for k, g in zip(first_names, first_got):
        assemble(k, g)
    conv_flat = jnp.concatenate([local2d[k].reshape(-1) for k, _ in CONVS])
    conv_rows = -(-conv_flat.shape[0] // PACK_COLS)
    conv_all = _all_gather("gather_conv_w", _pad_rows(conv_flat[None], conv_rows)[0], CHIPS)
    conv_all = conv_all.reshape(4, -1)
    off = 0
    for k, kind in CONVS:
        r, cc = local2d[k].shape
        full[k] = _unshard(conv_all[:, off:off + r * cc].reshape(4, r, cc), kind)
        off += r * cc

    d_rnn = w["conv_b"].shape[1]
    n_q, n_kv = w["q_norm_g"].shape[1], w["kv_norm_g"].shape[1]
    w_in = full["w_in"]
    o1, o2, o3 = d_rnn + n_q, d_rnn + n_q + n_kv, d_rnn + n_q + n_kv + QK_ROPE
    w_rnn = w_in[:, :d_rnn]
    zpad = lambda n: jnp.zeros((d, n), BF16)
    w_qkv = jnp.concatenate([w_in[:, d_rnn:o2], zpad(QK_NOPE), w_in[:, o2:o3], zpad(LANE - QK_NOPE - QK_ROPE)], axis=1)
    w_g = w_in[:, o3:]
    hd = QK_NOPE + QK_ROPE
    w_uq = jnp.pad(full["w_uq"].reshape(n_q, N_HEADS, hd), ((0, 0), (0, 0), (0, HEAD_PAD - hd))).reshape(n_q, -1)
    w_ukv = full["w_ukv"]
    v_head = w_ukv.shape[1] // N_HEADS - QK_NOPE
    d_ff = w["ffn_conv_b"].shape[1] // 2
    ffn_cw_gate, ffn_cw_val = full["ffn_conv_w"][:, :d_ff], full["ffn_conv_w"][:, d_ff:]
    ffn_cb_gate, ffn_cb_val = w["ffn_conv_b"][:, :d_ff], w["ffn_conv_b"][:, d_ff:]
    conv_w, conv_b = full["conv_w"], w["conv_b"]
    wa_bd = _block_diag_pairs(w["w_gate_a"][0])
    wx_bd = _block_diag_pairs(w["w_gate_x"][0])

    c_all = _all_gather("gather_c", c, ALL7).reshape(8, d)
    c_rows = 128
    (c_act,) = _tiled("silu_c", lambda v: (_silu(v),), 1, [(jnp.pad(c_all, ((0, c_rows - 8), (0, 0))), (c_rows, d), "full")],
                      [((c_rows, d), F32, (c_rows, d), "full")])
    w_ada = w["w_ada"][0]
    n_mod = w_ada.shape[1]
    b_loc = lax.dynamic_slice_in_dim(w["b_ada"], chip * n_mod, n_mod, axis=1)
    mod_loc = _mm("ada_fwd", c_act, w_ada, add=jnp.broadcast_to(b_loc, (c_rows, n_mod)))
    mod_all = _all_gather("gather_mod", mod_loc[:8], CHIPS)
    mod = lax.dynamic_index_in_dim(mod_all, me, 1, keepdims=False).reshape(1, -1)
    shift1, scale1, gate1, shift2, scale2, gate2 = [mod[:, i * d:(i + 1) * d] for i in range(6)]

    small_done = (mod[:, :1] + conv_all[:1, :1] + first_got[0][0, 0, :1, :1].astype(F32))
    later_flight = _split_start(
        "gather_later_start", [halves_bf[k] for k in later_names],
        [jax.ShapeDtypeStruct((4,) + halves_bf[k].shape, BF16) for k in later_names], "gather", after=small_done)

    half = QK_ROPE // 2
    inv_freq = ROPE_THETA ** (-jnp.arange(half, dtype=F32) / half)
    ang = positions[0].astype(F32)[:, None] * inv_freq
    cos, sin = jnp.cos(ang), jnp.sin(ang)
    one, zero = jnp.ones((s_len, QK_NOPE), F32), jnp.zeros((s_len, half), F32)
    tail = jnp.zeros((s_len, LANE - QK_NOPE - QK_ROPE), F32)
    cos_f = jnp.concatenate([one, cos, cos, tail + 1.0], axis=1)
    sin_a = jnp.concatenate([one * 0.0, -sin, zero, tail], axis=1)
    sin_b = jnp.concatenate([one * 0.0, zero, sin, tail], axis=1)
    reset = (positions[0] == 0).astype(F32)[:, None]
    tabs = [(cos_f, (tile, LANE), "row"), (sin_a, (tile, LANE), "row"), (sin_b, (tile, LANE), "row")]

    def rowspec(a):
        return (a, (tile, a.shape[1]), "row")

    def full2(a):
        return (a, a.shape, "full")

    def rowout(cols, dt):
        return ((s_len, cols), dt, (tile, cols), "row")

    def accout(a):
        return (a.shape, F32, a.shape, "acc")

    norm1_g = w["norm1_g"] + later_flight[4][:1, :1]
    norm2_g, final_g = w["norm2_g"], w["final_g"].reshape(1, d)
    ln1_in = [rowspec(x2d), full2(norm1_g), full2(scale1), full2(shift1)]
    big_tile = min(512, s_len)
    (h1,) = _tiled("ln1", _f_ln, nt, ln1_in, [rowout(d, BF16)], row_tile=big_tile)
    x_rnn = _mm("in_rnn", h1, w_rnn, out_dtype=BF16)
    qkv = _mm("in_qkv", h1, w_qkv)
    gates = _mm("in_gates", h1, w_g, out_dtype=BF16)

    ct = LANE
    n_ct = d_rnn // ct
    colspec = lambda a, width=ct: (a, (a.shape[0], width), "col")
    lru_in = [colspec(x_rnn), colspec(conv_w), colspec(conv_b), colspec(wa_bd), colspec(w["b_gate_a"]),
              colspec(wx_bd), colspec(w["b_gate_x"]), colspec(w["lru_param"]), full2(reset)]
    y_rnn, h_rnn = _tiled("lru_fwd", _f_lru_fwd, n_ct, lru_in,
                          [((s_len, d_rnn), BF16, (s_len, ct), "col"), ((s_len, d_rnn), F32, (s_len, ct), "col")])

    qkv_in = [rowspec(qkv)] + tabs + [full2(w["q_norm_g"]), full2(w["kv_norm_g"])]
    qn, kvn, kr = _tiled("qkv_norm", _f_qkv, nt, qkv_in, [rowout(n_q, BF16), rowout(n_kv, BF16), rowout(LANE, BF16)],
                         row_tile=big_tile)
    q_pre = _mm("up_q", qn, w_uq, out_dtype=BF16)
    kv = _mm("up_kv", kvn, w_ukv, out_dtype=BF16)
    o_mla, lse, q_cat = _attn_fwd(q_pre, (cos_f, sin_a, sin_b), kv, kr)

    send_sems, recv_sems, flown, landed, _ = later_flight
    landed = _split_wait("gather_later_wait", send_sems, recv_sems, flown, landed, "gather", after=o_mla)
    for k, g in zip(later_names, _relay_sibling(landed)):
        assemble(k, g)
    w_pr = full["w_proj_rnn"]
    assert ATTN_HEADS_PER_STEP == 2 and 2 * v_head == HEAD_PAD
    swap_pairs = lambda a: a.reshape(N_HEADS // 2, 2, v_head, d)[:, ::-1].reshape(-1, d)
    w_pm = swap_pairs(full["w_proj_mla"])
    w_out = full["w_out"]
    w_up_gate, w_up_val = full["w_up_gate"], full["w_up_val"]
    w_down = full["w_down"]

    p_rnn = _mm("proj_rnn", y_rnn, w_pr, out_dtype=BF16)
    p_mla = _mm("proj_mla", o_mla, w_pm, out_dtype=BF16)
    merge_in = [rowspec(gates), rowspec(p_rnn), rowspec(p_mla)]
    (merged,) = _tiled("merge", _f_merge, nt, merge_in, [rowout(d, BF16)])
    o_tok = _mm("out_proj", merged, w_out)
    res_in = [rowspec(x2d), rowspec(o_tok), full2(gate1), full2(norm2_g), full2(scale2), full2(shift2)]
    x1, h2 = _tiled("res_ln2", _f_res_ln, nt, res_in, [rowout(d, F32), rowout(d, BF16)], row_tile=big_tile)
    u_gate = _mm("ffn_up_gate", h2, w_up_gate, out_dtype=BF16)
    u_val = _mm("ffn_up_val", h2, w_up_val, out_dtype=BF16)
    n_ft = d_ff // LANE
    ffn_in = [colspec(a) for a in (u_gate, u_val, ffn_cw_gate, ffn_cw_val, ffn_cb_gate, ffn_cb_val)]
    (act,) = _tiled("ffn_conv", _f_ffn, n_ft, ffn_in, [((s_len, d_ff), BF16, (s_len, LANE), "col")])
    f_tok = _mm("ffn_down", act, w_down)

    loss_in = [rowspec(x1), rowspec(f_tok), rowspec(tgt), full2(gate2), full2(final_g)]
    dx1, df, loss_row, d_gate2, d_final_g = _tiled(
        "loss", _f_loss_and_grads, nt, loss_in,
        [rowout(d, F32), rowout(d, BF16), ((1, LANE), F32, (1, LANE), "acc"), accout(gate2), accout(final_g)],
        row_tile=big_tile)
    loss = lax.psum(loss_row[0, 0], ("x", "y", "c"))

    d_act = _mm("ffn_down_dx", df, w_down, tb=True, out_dtype=BF16)
    g_w_down = _mm("ffn_down_dw", act, df, ta=True)
    taps = ffn_cw_gate.shape[0]
    du_gate, du_val, g_cw_gate, g_cw_val, g_cb_gate, g_cb_val = _tiled(
        "ffn_conv_bwd", _vjp_of(_f_ffn, 6, (0, 1, 2, 3, 4, 5)), n_ft, ffn_in + [colspec(d_act)],
        [((s_len, d_ff), BF16, (s_len, LANE), "col")] * 2 + [((taps, d_ff), F32, (taps, LANE), "col")] * 2
        + [((1, d_ff), F32, (1, LANE), "col")] * 2)
    dh2 = _mm("ffn_up_gate_dx", du_gate, w_up_gate, tb=True)
    dh2 = _mm("ffn_up_val_dx", du_val, w_up_val, tb=True, add=dh2, out_dtype=BF16)
    g_w_up_halves = [_mm("ffn_up_gate_dw", h2, du_gate, ta=True), _mm("ffn_up_val_dw", h2, du_val, ta=True)]
    g_ffn_cw = jnp.concatenate([g_cw_gate, g_cw_val], axis=1)
    g_ffn_cb = jnp.concatenate([g_cb_gate, g_cb_val], axis=1)

    def chunked(k, gk):
        r, cc = local2d[k].shape
        if kinds[k] == "col":
            gk = gk.reshape(r, 4, cc).transpose(1, 0, 2)
        return gk.reshape(4, 2, r // 2, cc)

    r_up, c_up = local2d["w_up"].shape
    up_chunks = jnp.concatenate([g.reshape(r_up, 2, c_up).transpose(1, 0, 2) for g in g_w_up_halves], axis=0)
    ffn_chunks = {"w_up": up_chunks.reshape(4, 2, r_up // 2, c_up), "w_down": chunked("w_down", g_w_down)}
    ffn_names = [k for k in later_names if k in ffn_chunks]
    ffn_pair_flight = _split_start(
        "reduce_pair_ffn_start", [ffn_chunks[k] for k in ffn_names],
        [jax.ShapeDtypeStruct((4,) + ffn_chunks[k].shape[2:], F32) for k in ffn_names], "pair",
        after=ffn_chunks[ffn_names[-1]])
    gate1_held = gate1 + ffn_pair_flight[4][:1, :1]

    res_bwd = _vjp_of(_f_res_ln, 6, (0, 1, 2, 3, 4, 5))
    dx_res, do_tok, d_gate1, g_norm2, d_scale2, d_shift2 = _tiled(
        "res_ln2_bwd", res_bwd, nt, res_in[:2] + [full2(gate1_held)] + res_in[3:] + [rowspec(dx1), rowspec(dh2)],
        [rowout(d, F32), rowout(d, BF16), accout(gate1), accout(norm2_g), accout(scale2), accout(shift2)],
        row_tile=big_tile)
    d_merged = _mm("out_proj_dx", do_tok, w_out, tb=True, out_dtype=BF16)
    g_w_out = _mm("out_proj_dw", merged, do_tok, ta=True)
    d_gates, dp_rnn, dp_mla = _tiled(
        "merge_bwd", _f_merge_bwd, nt, merge_in + [rowspec(d_merged)],
        [rowout(gates.shape[1], BF16), rowout(d, BF16), rowout(d, BF16)])
    dy_rnn = _mm("proj_rnn_dx", dp_rnn, w_pr, tb=True, out_dtype=BF16)
    g_w_pr = _mm("proj_rnn_dw", y_rnn, dp_rnn, ta=True)
    do_mla = _mm("proj_mla_dx", dp_mla, w_pm, tb=True, out_dtype=BF16)
    g_w_pm = _mm("proj_mla_dw", o_mla, dp_mla, ta=True)

    core = ci.astype(jnp.int32).reshape(1)

    def pair_sums(tag, names, chunks):
        received = _pair_exchange("reduce_pair_exchange_" + tag, chunks)
        return [_reduce_pair("reduce_pair_" + k, ck, got, core) for k, ck, got in zip(names, chunks, received)]

    g_later = {"w_proj_rnn": g_w_pr, "w_proj_mla": swap_pairs(g_w_pm), "w_out": g_w_out}
    send_sems, recv_sems, flown, landed, _ = ffn_pair_flight
    ffn_received = _split_wait("reduce_pair_ffn_wait", send_sems, recv_sems, flown, landed, "pair", after=g_w_pm)
    sums = {k: _reduce_pair("reduce_pair_" + k, ffn_chunks[k], got, core) for k, got in zip(ffn_names, ffn_received)}
    other_names = [k for k in later_names if k not in ffn_chunks]
    sums.update(zip(other_names, pair_sums("ready", other_names, [chunked(k, g_later[k]) for k in other_names])))
    sums_ready = [sums[k] for k in later_names]
    ready_flight = _split_start(
        "reduce_ready_start", sums_ready, [jax.ShapeDtypeStruct(s.shape, s.dtype) for s in sums_ready], "alltoall",
        after=sums_ready[0])
    kr_held = kr + ready_flight[4][:1, :].astype(BF16)

    dq_cat, dkv, dkr = _attn_bwd(q_cat, kv, kr_held, o_mla, lse, do_mla)
    (dq_pre,) = _tiled("rot_q_bwd", _f_rotq_bwd, nt, tabs + [rowspec(dq_cat)],
                       [rowout(q_pre.shape[1], BF16)])
    dqn = _mm("up_q_dx", dq_pre, w_uq, tb=True, out_dtype=BF16)
    g_w_uq = _mm("up_q_dw", qn, dq_pre, ta=True)
    dkvn = _mm("up_kv_dx", dkv, w_ukv, tb=True, out_dtype=BF16)
    g_w_ukv = _mm("up_kv_dw", kvn, dkv, ta=True)
    dqkv, g_q_norm, g_kv_norm = _tiled(
        "qkv_norm_bwd", _f_qkv_bwd, nt, qkv_in + [rowspec(dqn), rowspec(dkvn), rowspec(dkr)],
        [rowout(qkv.shape[1], BF16), accout(w["q_norm_g"]), accout(w["kv_norm_g"])], row_tile=big_tile)

    lru_out = [((s_len, d_rnn), BF16, (s_len, ct), "col")]
    for a in (conv_w, conv_b, wa_bd, w["b_gate_a"], wx_bd, w["b_gate_x"], w["lru_param"]):
        lru_out.append((a.shape, F32, (a.shape[0], ct), "col"))
    dx_rnn, g_conv_w, g_conv_b, g_wa_bd, g_b_a, g_wx_bd, g_b_x, g_lru = _tiled(
        "lru_bwd", _f_lru_bwd, n_ct, lru_in + [colspec(h_rnn), colspec(dy_rnn)], lru_out)

    dh1 = _mm("in_gates_dx", d_gates, w_g, tb=True)
    dh1 = _mm("in_qkv_dx", dqkv, w_qkv, tb=True, add=dh1)
    dh1 = _mm("in_rnn_dx", dx_rnn, w_rnn, tb=True, add=dh1)
    g_w_rnn = _mm("in_rnn_dw", h1, dx_rnn, ta=True)
    g_w_qkv = _mm("in_qkv_dw", h1, dqkv, ta=True)
    g_w_g = _mm("in_gates_dw", h1, d_gates, ta=True)

    ln_bwd = _vjp_of(_f_ln, 4, (0, 1, 2, 3))

    def ln1_bwd(xv, gv, sc, sh, dxr, dh):
        dx, dg, dsc, dsh = ln_bwd(xv, gv, sc, sh, dh)
        return dx + dxr, dg, dsc, dsh

    grad_x, g_norm1, d_scale1, d_shift1 = _tiled(
        "ln1_bwd", ln1_bwd, nt, ln1_in + [rowspec(dx_res), rowspec(dh1)],
        [rowout(d, F32), accout(norm1_g), accout(scale1), accout(shift1)], row_tile=big_tile)

    dmod = jnp.concatenate([d_shift1, d_scale1, d_gate1, d_shift2, d_scale2, d_gate2], axis=1)
    dmod_all = _all_gather("gather_dmod", dmod, ALL7).reshape(8, -1)
    dmod_loc = lax.dynamic_slice_in_dim(dmod_all, chip * n_mod, n_mod, axis=1)
    g_w_ada = _mm("ada_dw", c_act, jnp.pad(dmod_loc, ((0, c_rows - 8), (0, 0))), ta=True)

    g_full = {
        "w_in": jnp.concatenate([g_w_rnn, g_w_qkv[:, :n_q + n_kv],
                                 g_w_qkv[:, n_q + n_kv + QK_NOPE:n_q + n_kv + QK_NOPE + QK_ROPE], g_w_g], axis=1),
        "w_uq": g_w_uq.reshape(n_q, N_HEADS, HEAD_PAD)[:, :, :hd].reshape(n_q, -1),
        "w_ukv": g_w_ukv,
        "conv_w": g_conv_w,
        "ffn_conv_w": g_ffn_cw,
    }
    g_small = {
        "b_ada": dmod, "norm1_g": g_norm1, "conv_b": g_conv_b,
        "w_gate_a": _block_diag_pairs_t(g_wa_bd)[None], "b_gate_a": g_b_a,
        "w_gate_x": _block_diag_pairs_t(g_wx_bd)[None], "b_gate_x": g_b_x, "lru_param": g_lru,
        "q_norm_g": g_q_norm, "kv_norm_g": g_kv_norm, "norm2_g": g_norm2,
        "ffn_conv_b": g_ffn_cb, "final_g": d_final_g.reshape(w["final_g"].shape),
    }

    small_flat = jnp.concatenate([g_small[k].reshape(-1) for k in SMALL] + [g_full[k].reshape(-1) for k, _ in CONVS])
    small_rows = -(-small_flat.shape[0] // (8 * PACK_COLS * PACK_ROW_UNIT)) * PACK_ROW_UNIT
    last_names = first_names + ["small"]
    last_chunks = [chunked(k, g_full[k]) for k in first_names]
    last_chunks.append(_pad_rows(small_flat[None], 8 * small_rows).reshape(4, 2, small_rows, PACK_COLS))
    sums_last = pair_sums("last", last_names, last_chunks)
    send_sems, recv_sems, flown, landed, _ = ready_flight
    quads_ready = _split_wait("reduce_ready_wait", send_sems, recv_sems, flown, landed, "alltoall", after=grad_x)
    last_flight = _split_start(
        "reduce_last_start", sums_last, [jax.ShapeDtypeStruct(s.shape, s.dtype) for s in sums_last], "alltoall",
        after=quads_ready[0])
    grads = {"w_ada": g_w_ada[None]}
    delta, new_m, new_v = {}, {}, {}

    def adamw(k):
        shp = w[k].shape
        flip = len(shp) == 3 and shp[-1] % LANE != 0 and shp[-2] % LANE == 0
        view = (lambda a: jnp.swapaxes(a, 1, 2)) if flip else (lambda a: a)
        two_d = (-1, view(w[k]).shape[-1]) if len(shp) > 1 else (1, -1)
        dk, mk, vk = _adamw("adamw_" + k, *[view(a).reshape(two_d) for a in (w[k], grads[k], m_in[k], v_in[k])])
        back = lambda a: view(a.reshape(view(w[k]).shape))
        delta[k], new_m[k], new_v[k] = back(dk), back(mk), back(vk)

    def finish(tag, names, quads, sums, after):
        reduced = {}
        for k, quad, ps in zip(names, quads, sums):
            quad = lax.dynamic_update_index_in_dim(quad, lax.dynamic_index_in_dim(ps, chip, 0, keepdims=True), chip, 0)
            reduced[k] = _reduce_quad("reduce_quad_" + k, quad, after)
        big = [k for k in names if k != "small"]
        for k, both in zip(big, _share_sibling("share_sibling_" + tag, [reduced[k] for k in big])):
            grads[k] = lax.dynamic_update_index_in_dim(both, reduced[k][None], ci, 0).reshape(w[k].shape)
        return reduced

    finish("ready", later_names, quads_ready, sums_ready, after=last_flight[4])
    for k in later_names + ["w_ada"]:
        adamw(k)
    send_sems, recv_sems, flown, landed, _ = last_flight
    quads_last = _split_wait("reduce_last_wait", send_sems, recv_sems, flown, landed, "alltoall",
                             after=delta[later_names[-1]])
    reduced = finish("last", last_names, quads_last, sums_last, after=None)
    small_grad = _all_gather("share_small", reduced["small"], ALL7).reshape(-1)
    off = 0
    for k in SMALL:
        grads[k] = small_grad[off:off + w[k].size].reshape(w[k].shape)
        off += w[k].size
    for k, _ in CONVS:
        r, cc = local2d[k].shape
        whole = small_grad[off:off + 4 * r * cc].reshape(r, 4 * cc)
        grads[k] = lax.dynamic_slice_in_dim(whole, chip * cc, cc, axis=1)[None]
        off += 4 * r * cc
    for k in WEIGHTS:
        if k not in delta:
            adamw(k)

    return (loss, grad_x[None], *[grads[k] for k in WEIGHTS], *[delta[k] for k in WEIGHTS],
            *[new_m[k] for k in WEIGHTS], *[new_v[k] for k in WEIGHTS])


def kernel(x, c, positions, w_ada, b_ada, norm1_g, w_in, conv_w, conv_b, w_gate_a, b_gate_a, w_gate_x, b_gate_x, lru_param, q_norm_g, w_uq, kv_norm_g, w_ukv, w_proj_rnn, w_proj_mla, w_out, norm2_g, w_up, ffn_conv_w, ffn_conv_b, w_down, final_g, loss_target, m_w_ada, m_b_ada, m_norm1_g, m_w_in, m_conv_w, m_conv_b, m_w_gate_a, m_b_gate_a, m_w_gate_x, m_b_gate_x, m_lru_param, m_q_norm_g, m_w_uq, m_kv_norm_g, m_w_ukv, m_w_proj_rnn, m_w_proj_mla, m_w_out, m_norm2_g, m_w_up, m_ffn_conv_w, m_ffn_conv_b, m_w_down, m_final_g, v_w_ada, v_b_ada, v_norm1_g, v_w_in, v_conv_w, v_conv_b, v_w_gate_a, v_b_gate_a, v_w_gate_x, v_b_gate_x, v_lru_param, v_q_norm_g, v_w_uq, v_kv_norm_g, v_w_ukv, v_w_proj_rnn, v_w_proj_mla, v_w_out, v_norm2_g, v_w_up, v_ffn_conv_w, v_ffn_conv_b, v_w_down, v_final_g):
    given = dict(locals())
    w = {k: given[k] for k in WEIGHTS}
    m_in = {k: given["m_" + k] for k in WEIGHTS}
    v_in = {k: given["v_" + k] for k in WEIGHTS}
    return _step(x, c, positions, w, m_in, v_in, loss_target)
```

```python
import functools
import math

import jax
import jax.numpy as jnp
from jax import lax
from jax.experimental import pallas as pl
from jax.experimental.pallas import tpu as pltpu

F32 = jnp.float32
BF16 = jnp.bfloat16

EPS = 1e-6
LRU_C = 8.0
N_HEADS = 16
QK_NOPE = 64
QK_ROPE = 32
HEAD_PAD = 128
ROPE_THETA = 10000.0
ADAM_LR = 0.001
ADAM_B1 = 0.9
ADAM_B2 = 0.999
ADAM_EPS = 1e-08
ADAM_WD = 0.01
ADAM_STEP = 10

LANE = 128
SUBLANES = 8
VMEM_LIMIT = 48 * 1024 * 1024
MM_TILE_M = MM_TILE_N = MM_TILE_K = 1408
PACK_COLS = 1024
PACK_ROW_UNIT = 32
MESH = pl.DeviceIdType.MESH

NN = (((1,), (0,)), ((), ()))
NT = (((1,), (1,)), ((), ()))
TN = (((0,), (0,)), ((), ()))


def _cparams(sem):
    return pltpu.CompilerParams(dimension_semantics=sem, vmem_limit_bytes=VMEM_LIMIT)


def _div_tile(n, cap, unit):
    best = None
    d = unit
    while d <= min(n, cap):
        if n % d == 0:
            best = d
        d += unit
    return n if best is None else best


def _mm(name, a, b, *, ta=False, tb=False, add=None, out_dtype=F32):
    if ta:
        kdim, m = a.shape
    else:
        m, kdim = a.shape
    if tb:
        n, kb = b.shape
    else:
        kb, n = b.shape
    assert kdim == kb, (name, a.shape, b.shape)
    tm = _div_tile(m, MM_TILE_M, 8 if not ta else LANE)
    tn = _div_tile(n, MM_TILE_N, LANE)
    tk = _div_tile(kdim, MM_TILE_K, LANE)
    nk = kdim // tk
    a_spec = pl.BlockSpec((tk, tm), lambda i, j, k: (k, i)) if ta else pl.BlockSpec((tm, tk), lambda i, j, k: (i, k))
    b_spec = pl.BlockSpec((tn, tk), lambda i, j, k: (j, k)) if tb else pl.BlockSpec((tk, tn), lambda i, j, k: (k, j))
    o_spec = pl.BlockSpec((tm, tn), lambda i, j, k: (i, j))
    has_add = add is not None
    dims = ((((0,) if ta else (1,)), ((1,) if tb else (0,))), ((), ()))

    def body(*refs):
        a_ref, b_ref = refs[0], refs[1]
        c_ref = refs[2] if has_add else None
        o_ref = refs[3] if has_add else refs[2]
        prod = lax.dot_general(a_ref[...].astype(BF16), b_ref[...].astype(BF16), dims, preferred_element_type=F32)
        if nk == 1:
            o_ref[...] = (prod + c_ref[...].astype(F32) if has_add else prod).astype(o_ref.dtype)
            return
        acc = refs[-1]
        k = pl.program_id(2)

        @pl.when(k == 0)
        def _():
            acc[...] = prod + c_ref[...].astype(F32) if has_add else prod

        @pl.when(jnp.logical_and(k > 0, k < nk - 1))
        def _():
            acc[...] += prod

        @pl.when(k == nk - 1)
        def _():
            o_ref[...] = (acc[...] + prod).astype(o_ref.dtype)

    ins = [a, b] + ([add] if has_add else [])
    specs = [a_spec, b_spec] + ([o_spec] if has_add else [])
    return pl.pallas_call(
        body, name=name, grid=(m // tm, n // tn, nk), in_specs=specs, out_specs=o_spec,
        out_shape=jax.ShapeDtypeStruct((m, n), out_dtype),
        scratch_shapes=[pltpu.VMEM((tm, tn), F32)] if nk > 1 else [],
        compiler_params=_cparams(("parallel", "parallel", "arbitrary")),
    )(*ins)


_IMAPS = {
    "row": lambda i: (i, 0),
    "col": lambda i: (0, i),
    "full": lambda i: (0, 0),
    "acc": lambda i: (0, 0),
}


def _tiled(name, fn, n, ins, outs, row_tile=None):
    if row_tile is not None:
        rows = next(a.shape[0] for a, _, k in ins if k == "row")
        n = rows // row_tile
        ins = [(a, (row_tile, bs[1]) if k == "row" else bs, k) for a, bs, k in ins]
        outs = [(s, dt, (row_tile, bs[1]) if k == "row" else bs, k) for s, dt, bs, k in outs]
    ni = len(ins)
    is_acc = [k == "acc" for *_, k in outs]

    def body(*refs):
        vals = fn(*[r[...].astype(F32) if r.dtype == BF16 else r[...] for r in refs[:ni]])
        orefs = refs[ni:]
        if any(is_acc):
            @pl.when(pl.program_id(0) == 0)
            def _():
                for r, a in zip(orefs, is_acc):
                    if a:
                        r[...] = jnp.zeros(r.shape, r.dtype)
        for r, v, a in zip(orefs, vals, is_acc):
            if a:
                r[...] += v.astype(r.dtype)
            else:
                r[...] = v.astype(r.dtype)

    res = pl.pallas_call(
        body, name=name, grid=(n,),
        in_specs=[pl.BlockSpec(bs, _IMAPS[k]) for _, bs, k in ins],
        out_specs=[pl.BlockSpec(bs, _IMAPS[k]) for _, _, bs, k in outs],
        out_shape=[jax.ShapeDtypeStruct(s, d) for s, d, _, _ in outs],
        compiler_params=_cparams(("arbitrary",)),
    )(*[a for a, _, _ in ins])
    return tuple(res)


def _vjp_of(fn, nin, diff):
    def g(*args):
        ins, cots = args[:nin], args[nin:]

        def f(*d):
            full = list(ins)
            for i, v in zip(diff, d):
                full[i] = v
            return fn(*full)

        outs, vjp = jax.vjp(f, *[ins[i] for i in diff])
        return vjp(tuple(c.astype(o.dtype) for c, o in zip(cots, outs)))
    return g


def _shift_rows(x, k, fill, up=False):
    n = x.shape[0]
    if k % SUBLANES == 0:
        pad = jnp.full((k,) + x.shape[1:], fill, x.dtype)
        return jnp.concatenate([x[k:], pad], axis=0) if up else jnp.concatenate([pad, x[:n - k]], axis=0)
    rows = lax.broadcasted_iota(jnp.int32, x.shape, 0)
    if up:
        return jnp.where(rows < n - k, pltpu.roll(x, n - k, 0), fill)
    return jnp.where(rows >= k, pltpu.roll(x, k, 0), fill)


@functools.partial(jax.custom_vjp, nondiff_argnums=(1,))
def _delay(x, k):
    return _shift_rows(x, k, 0.0)


def _delay_fwd(x, k):
    return _shift_rows(x, k, 0.0), None


def _delay_bwd(k, _, g):
    return (_shift_rows(g, k, 0.0, up=True),)


_delay.defvjp(_delay_fwd, _delay_bwd)


@functools.partial(jax.custom_vjp, nondiff_argnums=(1,))
def _lane_roll(x, s):
    return pltpu.roll(x, s, 1)


def _lane_roll_fwd(x, s):
    return pltpu.roll(x, s, 1), None


def _lane_roll_bwd(s, _, g):
    return (pltpu.roll(g, g.shape[1] - s, 1),)


_lane_roll.defvjp(_lane_roll_fwd, _lane_roll_bwd)


@jax.custom_vjp
def _bdot(x, w):
    return lax.dot_general(x.astype(BF16), w.astype(BF16), NN, preferred_element_type=F32)


def _bdot_fwd(x, w):
    return _bdot(x, w), (x, w)


def _bdot_bwd(res, g):
    x, w = res
    gb = g.astype(BF16)
    dx = lax.dot_general(gb, w.astype(BF16), NT, preferred_element_type=F32)
    dw = lax.dot_general(x.T.astype(BF16), gb, NN, preferred_element_type=F32)
    return dx, dw


_bdot.defvjp(_bdot_fwd, _bdot_bwd)


def _sigmoid(x):
    return 0.5 * (jnp.tanh(0.5 * x) + 1.0)


def _silu(x):
    return x * _sigmoid(x)


def _rms(x, g):
    return x * lax.rsqrt(jnp.mean(x * x, axis=-1, keepdims=True) + EPS) * g


def _causal_conv(x, w, b):
    kw = w.shape[0]
    tap = lax.broadcasted_iota(jnp.int32, w.shape, 0)
    y = b
    for k in range(kw):
        d = kw - 1 - k
        wk = jnp.sum(jnp.where(tap == k, w, 0.0), axis=0, keepdims=True)
        y = y + wk * (x if d == 0 else _delay(x, d))
    return y


def _rotate(x, cos_f, sin_a, sin_b):
    reps = x.shape[1] // LANE
    if reps > 1:
        cos_f, sin_a, sin_b = (jnp.tile(t, (1, reps)) for t in (cos_f, sin_a, sin_b))
    n = x.shape[1]
    half = QK_ROPE // 2
    return x * cos_f + _lane_roll(x, n - half) * sin_a + _lane_roll(x, half) * sin_b


def _softplus_neg(l):
    u = jnp.exp(-jnp.abs(l))
    log1p_u = jnp.where(u < 0.01, u * (1.0 - u * (0.5 - u * (1.0 / 3.0))), jnp.log(1.0 + u))
    return jnp.maximum(-l, 0.0) + log1p_u


def _f_ln(x, g, scale, shift):
    return (_rms(x, g) * (1.0 + scale) + shift,)


def _f_qkv(qkv, cos_f, sin_a, sin_b, qg, kvg):
    nq, nkv = qg.shape[1], kvg.shape[1]
    qn = _rms(qkv[:, :nq], qg)
    kvn = _rms(qkv[:, nq:nq + nkv], kvg)
    kr = _rotate(qkv[:, nq + nkv:], cos_f, sin_a, sin_b)
    return qn, kvn, kr


def _f_qkv_bwd(qkv, cos_f, sin_a, sin_b, qg, kvg, dqn, dkvn, dkr):
    nq, nkv = qg.shape[1], kvg.shape[1]
    _, vjp_q = jax.vjp(_rms, qkv[:, :nq], qg)
    _, vjp_kv = jax.vjp(_rms, qkv[:, nq:nq + nkv], kvg)
    _, vjp_r = jax.vjp(lambda t: _rotate(t, cos_f, sin_a, sin_b), qkv[:, nq + nkv:])
    dq_lat, dqg = vjp_q(dqn)
    dkv_lat, dkvg = vjp_kv(dkvn)
    (dkr_pre,) = vjp_r(dkr)
    return jnp.concatenate([dq_lat, dkv_lat, dkr_pre], axis=1), dqg, dkvg


QK_SCALE = 1.0 / math.sqrt(QK_NOPE + QK_ROPE)
LOG2_E = 1.4426950408889634
LN_2 = 0.6931471805599453


def _f_rotq(q, cos_f, sin_a, sin_b):
    return (_rotate(q, cos_f, sin_a, sin_b) * (QK_SCALE * LOG2_E),)


def _f_rotq_bwd(cos_f, sin_a, sin_b, dq):
    _, vjp = jax.vjp(lambda t: _rotate(t, cos_f, sin_a, sin_b) * QK_SCALE, jnp.zeros_like(dq))
    return vjp(dq)


def _merge(g_rnn, g_mla, p_rnn, p_mla):
    return _sigmoid(g_rnn) * p_rnn + _sigmoid(g_mla) * p_mla


def _f_merge(g, p_rnn, p_mla):
    d = p_rnn.shape[1]
    return (_merge(g[:, :d], g[:, d:], p_rnn, p_mla),)


def _f_merge_bwd(g, p_rnn, p_mla, dm):
    d = p_rnn.shape[1]
    _, vjp = jax.vjp(_merge, g[:, :d], g[:, d:], p_rnn, p_mla)
    dg_rnn, dg_mla, dp_rnn, dp_mla = vjp(dm)
    return jnp.concatenate([dg_rnn, dg_mla], axis=1), dp_rnn, dp_mla


def _f_res_ln(x, o, gate, g2, scale, shift):
    x1 = x + gate * o
    return x1, _rms(x1, g2) * (1.0 + scale) + shift


def _f_ffn(u_gate, u_val, cw_gate, cw_val, cb_gate, cb_val):
    return (_silu(_causal_conv(u_gate, cw_gate, cb_gate)) * _causal_conv(u_val, cw_val, cb_val),)


def _f_loss(x1, f, tgt, gate, fg):
    y = _rms(x1 + gate * f, fg)
    err = (y - tgt) * (y - tgt)
    return 0.5 * jnp.sum(jnp.mean(err, axis=-1, keepdims=True), axis=0, keepdims=True)


def _f_loss_and_grads(x1, f, tgt, gate, fg):
    loss, vjp = jax.vjp(lambda a, b, c, d: _f_loss(a, b, tgt, c, d), x1, f, gate, fg)
    dx1, df, dgate, dfg = vjp(jnp.ones((1, 1), F32))
    return dx1, df, jnp.broadcast_to(loss, (1, LANE)), dgate, dfg


@jax.custom_vjp
def _decay_and_gain(log_a):
    a = jnp.exp(log_a)
    return a, jnp.sqrt(-jnp.tanh(log_a) * (1.0 + a * a))


def _decay_and_gain_fwd(log_a):
    a, gain = _decay_and_gain(log_a)
    return (a, gain), (a, gain)


def _decay_and_gain_bwd(res, g):
    a, gain = res
    return (g[0] * a - g[1] * (a * a) / gain,)


_decay_and_gain.defvjp(_decay_and_gain_fwd, _decay_and_gain_bwd)


def _f_lru_coeffs(xr, cw, cb, wa, ba, wx, bx, lru, reset):
    xc = _causal_conv(xr, cw, cb)
    r = _sigmoid(_bdot(xc, wa) + ba)
    i = _sigmoid(_bdot(xc, wx) + bx)
    log_a = (-LRU_C) * r * _softplus_neg(lru)
    a, mult = _decay_and_gain(log_a)
    is_reset = reset > 0.5
    a = jnp.where(is_reset, 0.0, a)
    mult = jnp.where(is_reset, 1.0, mult)
    return a, mult * (i * xc)


SCAN_BLOCK = 64


def _scan(a, b, up=False):
    n = a.shape[0]
    blk = min(SCAN_BLOCK, n)
    pos = lax.broadcasted_iota(jnp.int32, a.shape, 0) % blk
    k = 1
    while k < blk:
        inside = (pos < blk - k) if up else (pos >= k)
        shift = n - k if up else k
        b = b + a * jnp.where(inside, pltpu.roll(b, shift, 0), 0.0)
        a = a * jnp.where(inside, pltpu.roll(a, shift, 0), 1.0)
        k *= 2
    blocks = range(n // blk)
    carry = jnp.zeros((1,) + a.shape[1:], a.dtype)
    out = [None] * len(blocks)
    for i in (reversed(blocks) if up else blocks):
        rows = slice(i * blk, (i + 1) * blk)
        out[i] = b[rows] + a[rows] * carry
        carry = out[i][:1] if up else out[i][blk - 1:]
    return jnp.concatenate(out, axis=0)


def _f_lru_fwd(xr, cw, cb, wa, ba, wx, bx, lru, reset):
    a, b = _f_lru_coeffs(xr, cw, cb, wa, ba, wx, bx, lru, reset)
    h = _scan(a, b)
    return h, h


def _f_lru_bwd(xr, cw, cb, wa, ba, wx, bx, lru, reset, h, dh):
    (a, _), vjp = jax.vjp(lambda *p: _f_lru_coeffs(*p, reset), xr, cw, cb, wa, ba, wx, bx, lru)
    g = _scan(_shift_rows(a, 1, 0.0, up=True), dh, up=True)
    return vjp((g * _shift_rows(h, 1, 0.0), g))


def _attn_tile(s):
    return 1024 if s >= 2048 else s // 2


def _keys(kv, kr):
    lane = lax.broadcasted_iota(jnp.int32, kv.shape, 1)
    return jnp.where(lane < QK_NOPE, kv, kr)


ATTN_HEADS_PER_STEP = 2


def _scores(q, kc, diagonal):
    s = lax.dot_general(q, kc, NT, preferred_element_type=F32)
    if not diagonal:
        return s
    rows = lax.broadcasted_iota(jnp.int32, s.shape, 0)
    cols = lax.broadcasted_iota(jnp.int32, s.shape, 1)
    return jnp.where(cols - (s.shape[1] - s.shape[0]) <= rows, s, -jnp.inf)


def _sub_blocks(t, diagonal):
    return ((0, t // 2, t // 2), (t // 2, t // 2, t)) if diagonal else ((0, t, t),)


def _causal_pairs(nb, k_major):
    if k_major:
        pairs = [(qb, kb) for kb in range(nb) for qb in range(kb, nb)]
    else:
        pairs = [(qb, kb) for qb in range(nb) for kb in range(qb + 1)]
    return jnp.array([p[0] for p in pairs], jnp.int32), jnp.array([p[1] for p in pairs], jnp.int32)


def _attn_fwd(q_pre, tables, kv, kr):
    s_len = q_pre.shape[0]
    t = _attn_tile(s_len)
    nb = s_len // t
    hp = ATTN_HEADS_PER_STEP
    wide = hp * HEAD_PAD
    q_tab, k_tab = _causal_pairs(nb, k_major=False)

    def body(qt, kt, qp_ref, cos_ref, sina_ref, sinb_ref, kv_ref, kr_ref, o_ref, lse_ref, q_ref, m_s, acc_s):
        pair = pl.program_id(1)
        qi, ki = qt[pair], kt[pair]

        @pl.when(ki == 0)
        def _():
            m_s[...] = jnp.full(m_s.shape, -jnp.inf, F32)
            acc_s[...] = jnp.zeros(acc_s.shape, F32)
            (rotated,) = _f_rotq(qp_ref[...].astype(F32), cos_ref[...], sina_ref[...], sinb_ref[...])
            q_ref[...] = rotated.astype(q_ref.dtype)

        def step(diagonal):
            for h in range(hp):
                lanes = slice(h * HEAD_PAD, (h + 1) * HEAD_PAD)
                for r0, nr, nk in _sub_blocks(t, diagonal):
                    rows = slice(r0, r0 + nr)
                    kvv = kv_ref[:nk, lanes]
                    s = _scores(q_ref[rows, lanes], _keys(kvv, kr_ref[:nk, :]), diagonal)
                    m_old = m_s[h, rows]
                    m_new = jnp.maximum(m_old, jnp.max(s, axis=-1, keepdims=True))
                    alpha = jnp.exp2(m_old - m_new)
                    p = jnp.exp2(s - m_new)
                    lane = lax.broadcasted_iota(jnp.int32, kvv.shape, 1)
                    ones_and_values = jnp.where(lane < QK_NOPE, jnp.ones_like(kvv), kvv)
                    acc_s[rows, lanes] = alpha * acc_s[rows, lanes] + lax.dot_general(
                        p.astype(BF16), ones_and_values, NN, preferred_element_type=F32)
                    m_s[h, rows] = m_new

        @pl.when(ki < qi)
        def _():
            step(False)

        @pl.when(ki == qi)
        def _():
            step(True)
            lane = lax.broadcasted_iota(jnp.int32, (t, HEAD_PAD), 1)
            outs = []
            for h in range(hp):
                acc = acc_s[:, h * HEAD_PAD:(h + 1) * HEAD_PAD]
                total = acc[:, :1]
                outs.append(acc / total)
                lse_ref[h] = m_s[h] + jnp.log(total) * LOG2_E
            o_ref[...] = jnp.where(lane >= QK_NOPE, outs[0], pltpu.roll(outs[1], QK_NOPE, 1)).astype(o_ref.dtype)

    q_rows = lambda h, p, qt, kt: (qt[p], 0)
    grid_spec = pltpu.PrefetchScalarGridSpec(
        num_scalar_prefetch=2, grid=(N_HEADS // hp, q_tab.shape[0]),
        in_specs=[pl.BlockSpec((t, wide), lambda h, p, qt, kt: (qt[p], h)),
                  pl.BlockSpec((t, HEAD_PAD), q_rows), pl.BlockSpec((t, HEAD_PAD), q_rows),
                  pl.BlockSpec((t, HEAD_PAD), q_rows),
                  pl.BlockSpec((t, wide), lambda h, p, qt, kt: (kt[p], h)),
                  pl.BlockSpec((t, HEAD_PAD), lambda h, p, qt, kt: (kt[p], 0))],
        out_specs=[pl.BlockSpec((t, HEAD_PAD), lambda h, p, qt, kt: (qt[p], h)),
                   pl.BlockSpec((hp, t, 1), lambda h, p, qt, kt: (h, qt[p], 0)),
                   pl.BlockSpec((t, wide), lambda h, p, qt, kt: (qt[p], h))],
        scratch_shapes=[pltpu.VMEM((hp, t, 1), F32), pltpu.VMEM((t, wide), F32)])
    return pl.pallas_call(
        body, name="attn_fwd", grid_spec=grid_spec,
        out_shape=[jax.ShapeDtypeStruct((s_len, N_HEADS // hp * HEAD_PAD), BF16),
                   jax.ShapeDtypeStruct((N_HEADS, s_len, 1), F32),
                   jax.ShapeDtypeStruct((s_len, N_HEADS * HEAD_PAD), BF16)],
        compiler_params=_cparams(("arbitrary", "arbitrary")),
    )(q_tab, k_tab, q_pre, *tables, kv, kr)


def _attn_bwd(q, kv, kr, o, lse, do):
    s_len = q.shape[0]
    t = _attn_tile(s_len)
    nb = s_len // t
    hp = ATTN_HEADS_PER_STEP
    wide = hp * HEAD_PAD
    q_tab, k_tab = _causal_pairs(nb, k_major=True)

    def body(qt, kt, q_ref, kv_ref, kr_ref, o_ref, lse_ref, do_ref, dq_ref, dkv_ref, dkr_ref, dk_s, dv_s, dq_s):
        g, pair = pl.program_id(0), pl.program_id(1)
        qb, kb = qt[pair], kt[pair]

        @pl.when(jnp.logical_and(g == 0, pair == 0))
        def _():
            dkr_ref[...] = jnp.zeros(dkr_ref.shape, F32)

        @pl.when(pair == 0)
        def _():
            dq_s[...] = jnp.zeros(dq_s.shape, F32)

        @pl.when(qb == kb)
        def _():
            dk_s[...] = jnp.zeros(dk_s.shape, F32)
            dv_s[...] = jnp.zeros(dv_s.shape, F32)

        def step(diagonal):
            for h in range(hp):
                lanes = slice(h * HEAD_PAD, (h + 1) * HEAD_PAD)
                for r0, nr, nk in _sub_blocks(t, diagonal):
                    rows, keys = slice(r0, r0 + nr), slice(0, nk)
                    qv, kvv = q_ref[rows, lanes], kv_ref[keys, lanes]
                    pair_do = do_ref[rows, :].astype(F32)
                    lane = lax.broadcasted_iota(jnp.int32, pair_do.shape, 1)
                    mine = (lane >= QK_NOPE) if h == 0 else (lane < QK_NOPE)
                    placed = pair_do if h == 0 else pltpu.roll(pair_do, QK_NOPE, 1)
                    dov = jnp.where(lane >= QK_NOPE, placed, 0.0).astype(BF16)
                    delta = jnp.sum(jnp.where(mine, pair_do * o_ref[rows, :].astype(F32), 0.0), axis=-1, keepdims=True)
                    kc = _keys(kvv, kr_ref[keys, :])
                    p = jnp.exp2(_scores(qv, kc, diagonal) - lse_ref[h, rows])
                    dp = lax.dot_general(dov, kvv, NT, preferred_element_type=F32)
                    ds = p * (dp - delta)
                    dv_s[keys, lanes] += lax.dot_general(p.astype(BF16), dov, TN, preferred_element_type=F32)
                    dk_s[keys, lanes] += lax.dot_general(ds.astype(BF16), qv, TN, preferred_element_type=F32)
                    q_rows = pl.ds(pl.multiple_of(qb * t + r0, nr), nr)
                    dq_s[q_rows, lanes] += lax.dot_general(ds.astype(BF16), kc, NN, preferred_element_type=F32)

        @pl.when(qb > kb)
        def _():
            step(False)

        @pl.when(qb == kb)
        def _():
            step(True)

        @pl.when(qb == nb - 1)
        def _():
            lane = lax.broadcasted_iota(jnp.int32, (t, HEAD_PAD), 1)
            rows = pl.ds(pl.multiple_of(kb * t, t), t)
            for h in range(hp):
                lanes = slice(h * HEAD_PAD, (h + 1) * HEAD_PAD)
                dk = dk_s[:, lanes] * LN_2
                dkv_ref[:, lanes] = jnp.where(lane < QK_NOPE, dk, dv_s[:, lanes]).astype(dkv_ref.dtype)
                dkr_ref[rows, :] += jnp.where(lane >= QK_NOPE, dk, 0.0)

        @pl.when(pair == q_tab.shape[0] - 1)
        def _():
            dq_ref[...] = dq_s[...].astype(dq_ref.dtype)

    all_lanes = N_HEADS * HEAD_PAD
    qmap = lambda h, p, qt, kt: (qt[p], h)
    kmap = lambda h, p, qt, kt: (kt[p], h)
    grid_spec = pltpu.PrefetchScalarGridSpec(
        num_scalar_prefetch=2, grid=(N_HEADS // hp, q_tab.shape[0]),
        in_specs=[pl.BlockSpec((t, wide), qmap),
                  pl.BlockSpec((t, wide), kmap),
                  pl.BlockSpec((t, HEAD_PAD), lambda h, p, qt, kt: (kt[p], 0)),
                  pl.BlockSpec((t, HEAD_PAD), qmap),
                  pl.BlockSpec((hp, t, 1), lambda h, p, qt, kt: (h, qt[p], 0)),
                  pl.BlockSpec((t, HEAD_PAD), qmap)],
        out_specs=[pl.BlockSpec((s_len, wide), lambda h, p, qt, kt: (0, h)),
                   pl.BlockSpec((t, wide), kmap),
                   pl.BlockSpec((s_len, HEAD_PAD), lambda h, p, qt, kt: (0, 0))],
        scratch_shapes=[pltpu.VMEM((t, wide), F32), pltpu.VMEM((t, wide), F32), pltpu.VMEM((s_len, wide), F32)])
    return pl.pallas_call(
        body, name="attn_bwd", grid_spec=grid_spec,
        out_shape=[jax.ShapeDtypeStruct((s_len, all_lanes), BF16),
                   jax.ShapeDtypeStruct((s_len, all_lanes), BF16),
                   jax.ShapeDtypeStruct((s_len, HEAD_PAD), F32)],
        compiler_params=_cparams(("arbitrary", "arbitrary")),
    )(q_tab, k_tab, q, kv, kr, o, lse, do)


def _adamw(name, w, g, m, v):
    rows, cols = w.shape
    tr = _div_tile(rows, max(8, (2 * 1024 * 1024) // (4 * cols)), 8)

    def body(w_ref, g_ref, m_ref, v_ref, d_ref, nm_ref, nv_ref):
        gv = g_ref[...]
        nm = ADAM_B1 * m_ref[...] + (1.0 - ADAM_B1) * gv
        nv = ADAM_B2 * v_ref[...] + (1.0 - ADAM_B2) * jnp.square(gv)
        m_hat = nm / (1.0 - ADAM_B1 ** ADAM_STEP)
        v_hat = nv / (1.0 - ADAM_B2 ** ADAM_STEP)
        d_ref[...] = -ADAM_LR * (m_hat / (jnp.sqrt(v_hat) + ADAM_EPS) + ADAM_WD * w_ref[...])
        nm_ref[...] = nm
        nv_ref[...] = nv

    spec = pl.BlockSpec((tr, cols), lambda i: (i, 0))
    return pl.pallas_call(
        body, name=name, grid=(rows // tr,), in_specs=[spec] * 4, out_specs=[spec] * 3,
        out_shape=[jax.ShapeDtypeStruct((rows, cols), F32)] * 3,
        compiler_params=_cparams(("parallel",)),
    )(w, g, m, v)


ALL7 = (1, 2, 3, 4, 5, 6, 7)
CHIPS = (2, 4, 6)


def _all_gather(name, src, masks):
    bits = 0
    for m in masks:
        bits |= m
    nslots = {7: 8, 6: 4}[bits]
    nm = len(masks)

    def slot_of(x, y, c):
        return {7: 4 * x + 2 * y + c, 6: 2 * x + y}[bits]

    def body(src_ref, out_ref, send_sems, recv_sems, local_sem):
        x, y, c = lax.axis_index("x"), lax.axis_index("y"), lax.axis_index("c")
        mine = slot_of(x, y, c)
        own = pltpu.make_async_copy(src_ref, out_ref.at[mine], local_sem)
        own.start()
        copies = []
        for i, m in enumerate(masks):
            peer = _peer(x, y, c, m)
            copies.append((
                pltpu.make_async_remote_copy(
                    src_ref=src_ref, dst_ref=out_ref.at[mine], send_sem=send_sems.at[i], recv_sem=recv_sems.at[i],
                    device_id=peer, device_id_type=MESH),
                pltpu.make_async_remote_copy(
                    src_ref=src_ref, dst_ref=out_ref.at[slot_of(*peer)], send_sem=send_sems.at[i],
                    recv_sem=recv_sems.at[i], device_id=peer, device_id_type=MESH)))
        for send, _ in copies:
            send.start()
        for _, arrival in copies:
            arrival.wait_recv()
        for send, _ in copies:
            send.wait_send()
        own.wait()

    return pl.pallas_call(
        body, name=name,
        in_specs=[pl.BlockSpec(memory_space=pl.ANY)], out_specs=pl.BlockSpec(memory_space=pl.ANY),
        out_shape=jax.ShapeDtypeStruct((nslots,) + tuple(src.shape), src.dtype),
        scratch_shapes=[pltpu.SemaphoreType.DMA((nm,)), pltpu.SemaphoreType.DMA((nm,)), pltpu.SemaphoreType.DMA],
    )(src)


def _peer(x, y, c, m):
    return (1 - x if m & 4 else x, 1 - y if m & 2 else y, 1 - c if m & 1 else c)


def _comm_call(name, emit, srcs, out_shapes, n_sems, in_place=False):
    n = len(srcs)

    def body(*refs):
        src_refs, out_refs = refs[:n], refs[n:n + len(out_shapes)]
        send_sems, recv_sems = refs[-2], refs[-1]

        def copy(src, dst, i, peer):
            return pltpu.make_async_remote_copy(src_ref=src, dst_ref=dst, send_sem=send_sems.at[i],
                                                recv_sem=recv_sems.at[i], device_id=peer, device_id_type=MESH)

        emit(lax.axis_index("x"), lax.axis_index("y"), lax.axis_index("c"), src_refs, out_refs, copy)

    hbm = pl.BlockSpec(memory_space=pl.ANY)
    return pl.pallas_call(
        body, name=name, in_specs=[hbm] * n, out_specs=[hbm] * len(out_shapes), out_shape=out_shapes,
        scratch_shapes=[pltpu.SemaphoreType.DMA((n_sems,)), pltpu.SemaphoreType.DMA((n_sems,))],
        input_output_aliases={i: i for i in range(n)} if in_place else {},
    )(*srcs)


HBM_SPEC = pl.BlockSpec(memory_space=pltpu.HBM)
SEM_SPEC = pl.BlockSpec(memory_space=pltpu.SEMAPHORE)
DATAFLOW = pltpu.SideEffectType.DATAFLOW_SIDE_EFFECTING


def _chip_copies(srcs, lands, send_sems, recv_sems, mode):
    x, y, c = lax.axis_index("x"), lax.axis_index("y"), lax.axis_index("c")
    chip = 2 * x + y
    sends, arrivals = [], []
    if mode == "pair":
        for k, (s, l) in enumerate(zip(srcs, lands)):
            for group in (sends, arrivals):
                group.append(pltpu.make_async_remote_copy(
                    src_ref=s.at[:, 1 - c], dst_ref=l, send_sem=send_sems.at[3 * k], recv_sem=recv_sems.at[3 * k],
                    device_id=(x, y, 1 - c), device_id_type=MESH))
        return sends, arrivals
    for j, m in enumerate(CHIPS):
        px, py, _ = _peer(x, y, c, m)
        theirs = 2 * px + py
        for k, (s, l) in enumerate(zip(srcs, lands)):
            if mode == "gather":
                src, dst, got = s.at[c], l.at[chip, c], l.at[theirs, c]
            else:
                src, dst, got = s.at[theirs], l.at[chip], l.at[theirs]
            for to, group in ((dst, sends), (got, arrivals)):
                group.append(pltpu.make_async_remote_copy(
                    src_ref=src, dst_ref=to, send_sem=send_sems.at[3 * k + j], recv_sem=recv_sems.at[3 * k + j],
                    device_id=(px, py, c), device_id_type=MESH))
    return sends, arrivals


def _split_start(name, srcs, land_shapes, mode, after):
    n = len(srcs)

    def body(*refs):
        sends, _ = _chip_copies(refs[:n], refs[n:2 * n], refs[2 * n + 1], refs[2 * n + 2], mode)
        for cp in sends:
            cp.start()
        token = refs[-1]
        token[...] = jnp.zeros(token.shape, token.dtype)

    hbm = lambda a: pltpu.with_memory_space_constraint(a, pltpu.HBM)
    lands = [hbm(lax.empty(s.shape, s.dtype)) for s in land_shapes]
    bufs = [pltpu.HBM(a.shape, a.dtype) for a in list(srcs) + lands]
    res = pl.pallas_call(
        body, name=name,
        out_shape=(pltpu.SemaphoreType.DMA((3 * n,)), pltpu.SemaphoreType.DMA((3 * n,)), *bufs,
                   jax.ShapeDtypeStruct((SUBLANES, LANE), F32)),
        in_specs=[HBM_SPEC] * (2 * n) + [pl.BlockSpec(memory_space=pl.ANY)],
        out_specs=[SEM_SPEC, SEM_SPEC] + [HBM_SPEC] * (2 * n) + [pl.BlockSpec(memory_space=pltpu.VMEM)],
        input_output_aliases={i: 2 + i for i in range(2 * n)},
        compiler_params=pltpu.CompilerParams(has_side_effects=DATAFLOW),
    )(*[hbm(s) for s in srcs], *lands, after)
    return res[0], res[1], res[2:2 + n], res[2 + n:2 + 2 * n], res[-1]


def _split_wait(name, send_sems, recv_sems, srcs, lands, mode, after):
    n = len(srcs)

    def body(*refs):
        sends, arrivals = _chip_copies(refs[:n], refs[n:2 * n], refs[2 * n], refs[2 * n + 1], mode)
        for cp in sends:
            cp.wait_send()
        for cp in arrivals:
            cp.wait_recv()

    res = pl.pallas_call(
        body, name=name,
        out_shape=tuple(pltpu.HBM(a.shape, a.dtype) for a in list(srcs) + list(lands)),
        in_specs=[HBM_SPEC] * (2 * n) + [SEM_SPEC, SEM_SPEC, pl.BlockSpec(memory_space=pl.ANY)],
        out_specs=[HBM_SPEC] * (2 * n),
        input_output_aliases={i: i for i in range(2 * n)},
        compiler_params=pltpu.CompilerParams(has_side_effects=DATAFLOW),
    )(*srcs, *lands, send_sems, recv_sems, after)
    return res[n:]


def _relay_sibling(lands):
    def emit(x, y, c, srcs, outs, copy):
        sib = (x, y, 1 - c)
        sends, arrivals = [], []
        for j, m in enumerate(CHIPS):
            px, py, _ = _peer(x, y, c, m)
            theirs = 2 * px + py
            for k, (s, o) in enumerate(zip(srcs, outs)):
                sends.append(copy(s.at[theirs, c], o.at[theirs, c], 3 * k + j, sib))
                arrivals.append(copy(s.at[theirs, c], o.at[theirs, 1 - c], 3 * k + j, sib))
        for cp in sends:
            cp.start()
        for cp in arrivals:
            cp.wait_recv()
        for cp in sends:
            cp.wait_send()

    shapes = [jax.ShapeDtypeStruct(l.shape, l.dtype) for l in lands]
    return _comm_call("relay_weights", emit, lands, shapes, 3 * len(lands), in_place=True)


def _gather_weights(halves):
    n = len(halves)

    def emit(x, y, c, srcs, outs, copy):
        chip = 2 * x + y
        sib = (x, y, 1 - c)
        first, relay, landed, relayed = [], [], [], []
        for j, m in enumerate(CHIPS):
            px, py, _ = _peer(x, y, c, m)
            theirs = 2 * px + py
            for k in range(n):
                i = 6 * k + j
                first.append(copy(srcs[k].at[c], outs[k].at[chip, c], i, (px, py, c)))
                landed.append(copy(srcs[k].at[c], outs[k].at[theirs, c], i, (px, py, c)))
                relay.append(copy(outs[k].at[theirs, c], outs[k].at[theirs, c], i + 3, sib))
                relayed.append(copy(outs[k].at[theirs, 1 - c], outs[k].at[theirs, 1 - c], i + 3, sib))
        for cp in first:
            cp.start()
        for arrival, onward in zip(landed, relay):
            arrival.wait_recv()
            onward.start()
        for arrival in relayed:
            arrival.wait_recv()
        for cp in first + relay:
            cp.wait_send()

    shapes = [jax.ShapeDtypeStruct((4,) + h.shape, h.dtype) for h in halves]
    return _comm_call("gather_weights", emit, halves, shapes, 6 * n)


def _pair_exchange(name, chunks):
    def emit(x, y, c, srcs, outs, copy):
        sib = (x, y, 1 - c)
        sends = [copy(s.at[:, 1 - c], o, k, sib) for k, (s, o) in enumerate(zip(srcs, outs))]
        for cp in sends:
            cp.start()
        for cp in sends:
            cp.wait_recv()
        for cp in sends:
            cp.wait_send()

    shapes = [jax.ShapeDtypeStruct((4,) + g.shape[2:], g.dtype) for g in chunks]
    return _comm_call(name, emit, chunks, shapes, len(chunks))


def _share_sibling(name, parts):
    def emit(x, y, c, srcs, outs, copy):
        sib = (x, y, 1 - c)
        sends = [copy(s, o.at[c], k, sib) for k, (s, o) in enumerate(zip(srcs, outs))]
        arrivals = [copy(s, o.at[1 - c], k, sib) for k, (s, o) in enumerate(zip(srcs, outs))]
        for cp in sends:
            cp.start()
        for cp in arrivals:
            cp.wait_recv()
        for cp in sends:
            cp.wait_send()

    shapes = [jax.ShapeDtypeStruct((2,) + p.shape, p.dtype) for p in parts]
    return _comm_call(name, emit, parts, shapes, len(parts))


def _reduce_pair(name, chunk, from_sibling, core):
    n, _, h, cols = chunk.shape
    rt = _div_tile(h, max(16, (1 << 20) // (4 * cols)), 16)

    def body(core_ref, a_ref, b_ref, o_ref):
        o_ref[...] = (a_ref[...] + b_ref[...]).astype(o_ref.dtype)

    grid_spec = pltpu.PrefetchScalarGridSpec(
        num_scalar_prefetch=1, grid=(n, h // rt),
        in_specs=[pl.BlockSpec((None, None, rt, cols), lambda s, i, core_ref: (s, core_ref[0], i, 0)),
                  pl.BlockSpec((None, rt, cols), lambda s, i, core_ref: (s, i, 0))],
        out_specs=pl.BlockSpec((None, rt, cols), lambda s, i, core_ref: (s, i, 0)))
    return pl.pallas_call(
        body, name=name, grid_spec=grid_spec, out_shape=jax.ShapeDtypeStruct((n, h, cols), BF16),
        compiler_params=_cparams(("parallel", "parallel")),
    )(core, chunk, from_sibling)


def _reduce_quad(name, q, after=None):
    _, h, cols = q.shape
    rt = _div_tile(h, max(16, (1 << 20) // (4 * cols)), 16)

    def body(q_ref, *rest):
        v = q_ref[...].astype(F32)
        rest[-1][...] = ((v[0] + v[1]) + v[2]) + v[3]

    held = [] if after is None else [after]
    return pl.pallas_call(
        body, name=name, grid=(h // rt,),
        in_specs=[pl.BlockSpec((4, rt, cols), lambda i: (0, i, 0))] + [pl.BlockSpec(memory_space=pl.ANY)] * len(held),
        out_specs=pl.BlockSpec((rt, cols), lambda i: (i, 0)),
        out_shape=jax.ShapeDtypeStruct((h, cols), F32),
        compiler_params=_cparams(("parallel",)),
    )(q, *held)


def _unshard(seg, kind):
    n, r, c = seg.shape
    if kind == "col":
        return seg.transpose(1, 0, 2).reshape(r, n * c)
    return seg.reshape(n * r, c)


def _pad_rows(flat, rows):
    n, ln = flat.shape
    return jnp.pad(flat, ((0, 0), (0, rows * PACK_COLS - ln))).reshape(n, rows, PACK_COLS)


def _block_diag_pairs(w):
    n2, bs, _ = w.shape
    eye = jnp.eye(2, dtype=w.dtype)
    z = w.reshape(n2 // 2, 2, bs, 1, bs) * eye[None, :, None, :, None]
    return z.reshape(n2 // 2, 2 * bs, 2 * bs).transpose(1, 0, 2).reshape(2 * bs, n2 * bs)


def _block_diag_pairs_t(d, bs=64):
    n = d.shape[1] // (2 * bs)
    z = d.reshape(2 * bs, n, 2 * bs).transpose(1, 0, 2).reshape(n, 2, bs, 2, bs)
    return jnp.stack([z[:, 0, :, 0, :], z[:, 1, :, 1, :]], axis=1).reshape(2 * n, bs, bs)


BIG = (("w_in", "col"), ("w_uq", "col"), ("w_ukv", "col"), ("w_proj_rnn", "row"), ("w_proj_mla", "row"),
       ("w_out", "row"), ("w_up", "col"), ("w_down", "row"))
FIRST_USED = ("w_in", "w_uq", "w_ukv")
CONVS = (("conv_w", "col"), ("ffn_conv_w", "col"))
SMALL = ("b_ada", "norm1_g", "conv_b", "w_gate_a", "b_gate_a", "w_gate_x", "b_gate_x", "lru_param",
         "q_norm_g", "kv_norm_g", "norm2_g", "ffn_conv_b", "final_g")
WEIGHTS = ("w_ada", "b_ada", "norm1_g", "w_in", "conv_w", "conv_b", "w_gate_a", "b_gate_a", "w_gate_x",
           "b_gate_x", "lru_param", "q_norm_g", "w_uq", "kv_norm_g", "w_ukv", "w_proj_rnn", "w_proj_mla",
           "w_out", "norm2_g", "w_up", "ffn_conv_w", "ffn_conv_b", "w_down", "final_g")


def _step(x, c, positions, w, m_in, v_in, loss_target):
    s_len, d = x.shape[1], x.shape[2]
    x2d = x[0]
    tgt = loss_target[0]
    xi, yi, ci = lax.axis_index("x"), lax.axis_index("y"), lax.axis_index("c")
    chip = 2 * xi + yi
    me = 2 * chip + ci
    tile = min(256, s_len)
    nt = s_len // tile

    local2d = {k: w[k][0] for k, _ in BIG + CONVS}
    kinds = dict(BIG)
    halves_bf = {k: local2d[k].astype(BF16).reshape(2, local2d[k].shape[0] // 2, local2d[k].shape[1]) for k, _ in BIG}
    first_names = [k for k, _ in BIG if k in FIRST_USED]
    later_names = [k for k, _ in BIG if k not in FIRST_USED]
    full = {}

    def assemble(k, g):
        g = lax.dynamic_update_index_in_dim(g, halves_bf[k][None], chip, 0).reshape((4,) + local2d[k].shape)
        if k == "w_up":
            full["w_up_gate"], full["w_up_val"] = _unshard(g[:2], kinds[k]), _unshard(g[2:], kinds[k])
        else:
            full[k] = _unshard(g, kinds[k])

    first_got = _gather_weights([halves_bf[k] for k in first_names])
    for k, g in zip(first_names, first_got):
        assemble(k, g)
    conv_flat = jnp.concatenate([local2d[k].reshape(-1) for k, _ in CONVS])
    conv_rows = -(-conv_flat.shape[0] // PACK_COLS)
    conv_all = _all_gather("gather_conv_w", _pad_rows(conv_flat[None], conv_rows)[0], CHIPS)
    conv_all = conv_all.reshape(4, -1)
    off = 0
    for k, kind in CONVS:
        r, cc = local2d[k].shape
        full[k] = _unshard(conv_all[:, off:off + r * cc].reshape(4, r, cc), kind)
        off += r * cc

    d_rnn = w["conv_b"].shape[1]
    n_q, n_kv = w["q_norm_g"].shape[1], w["kv_norm_g"].shape[1]
    w_in = full["w_in"]
    o1, o2, o3 = d_rnn + n_q, d_rnn + n_q + n_kv, d_rnn + n_q + n_kv + QK_ROPE
    w_rnn = w_in[:, :d_rnn]
    zpad = lambda n: jnp.zeros((d, n), BF16)
    w_qkv = jnp.concatenate([w_in[:, d_rnn:o2], zpad(QK_NOPE), w_in[:, o2:o3], zpad(LANE - QK_NOPE - QK_ROPE)], axis=1)
    w_g = w_in[:, o3:]
    hd = QK_NOPE + QK_ROPE
    w_uq = jnp.pad(full["w_uq"].reshape(n_q, N_HEADS, hd), ((0, 0), (0, 0), (0, HEAD_PAD - hd))).reshape(n_q, -1)
    w_ukv = full["w_ukv"]
    v_head = w_ukv.shape[1] // N_HEADS - QK_NOPE
    d_ff = w["ffn_conv_b"].shape[1] // 2
    ffn_cw_gate, ffn_cw_val = full["ffn_conv_w"][:, :d_ff], full["ffn_conv_w"][:, d_ff:]
    ffn_cb_gate, ffn_cb_val = w["ffn_conv_b"][:, :d_ff], w["ffn_conv_b"][:, d_ff:]
    conv_w, conv_b = full["conv_w"], w["conv_b"]
    wa_bd = _block_diag_pairs(w["w_gate_a"][0])
    wx_bd = _block_diag_pairs(w["w_gate_x"][0])

    c_all = _all_gather("gather_c", c, ALL7).reshape(8, d)
    c_rows = 128
    (c_act,) = _tiled("silu_c", lambda v: (_silu(v),), 1, [(jnp.pad(c_all, ((0, c_rows - 8), (0, 0))), (c_rows, d), "full")],
                      [((c_rows, d), F32, (c_rows, d), "full")])
    w_ada = w["w_ada"][0]
    n_mod = w_ada.shape[1]
    b_loc = lax.dynamic_slice_in_dim(w["b_ada"], chip * n_mod, n_mod, axis=1)
    mod_loc = _mm("ada_fwd", c_act, w_ada, add=jnp.broadcast_to(b_loc, (c_rows, n_mod)))
    mod_all = _all_gather("gather_mod", mod_loc[:8], CHIPS)
    mod = lax.dynamic_index_in_dim(mod_all, me, 1, keepdims=False).reshape(1, -1)
    shift1, scale1, gate1, shift2, scale2, gate2 = [mod[:, i * d:(i + 1) * d] for i in range(6)]

    small_done = (mod[:, :1] + conv_all[:1, :1] + first_got[0][0, 0, :1, :1].astype(F32))
    later_flight = _split_start(
        "gather_later_start", [halves_bf[k] for k in later_names],
        [jax.ShapeDtypeStruct((4,) + halves_bf[k].shape, BF16) for k in later_names], "gather", after=small_done)

    half = QK_ROPE // 2
    inv_freq = ROPE_THETA ** (-jnp.arange(half, dtype=F32) / half)
    ang = positions[0].astype(F32)[:, None] * inv_freq
    cos, sin = jnp.cos(ang), jnp.sin(ang)
    one, zero = jnp.ones((s_len, QK_NOPE), F32), jnp.zeros((s_len, half), F32)
    tail = jnp.zeros((s_len, LANE - QK_NOPE - QK_ROPE), F32)
    cos_f = jnp.concatenate([one, cos, cos, tail + 1.0], axis=1)
    sin_a = jnp.concatenate([one * 0.0, -sin, zero, tail], axis=1)
    sin_b = jnp.concatenate([one * 0.0, zero, sin, tail], axis=1)
    reset = (positions[0] == 0).astype(F32)[:, None]
    tabs = [(cos_f, (tile, LANE), "row"), (sin_a, (tile, LANE), "row"), (sin_b, (tile, LANE), "row")]

    def rowspec(a):
        return (a, (tile, a.shape[1]), "row")

    def full2(a):
        return (a, a.shape, "full")

    def rowout(cols, dt):
        return ((s_len, cols), dt, (tile, cols), "row")

    def accout(a):
        return (a.shape, F32, a.shape, "acc")

    norm1_g = w["norm1_g"] + later_flight[4][:1, :1]
    norm2_g, final_g = w["norm2_g"], w["final_g"].reshape(1, d)
    ln1_in = [rowspec(x2d), full2(norm1_g), full2(scale1), full2(shift1)]
    big_tile = min(512, s_len)
    (h1,) = _tiled("ln1", _f_ln, nt, ln1_in, [rowout(d, BF16)], row_tile=big_tile)
    x_rnn = _mm("in_rnn", h1, w_rnn, out_dtype=BF16)
    qkv = _mm("in_qkv", h1, w_qkv)
    gates = _mm("in_gates", h1, w_g, out_dtype=BF16)

    ct = LANE
    n_ct = d_rnn // ct
    colspec = lambda a, width=ct: (a, (a.shape[0], width), "col")
    lru_in = [colspec(x_rnn), colspec(conv_w), colspec(conv_b), colspec(wa_bd), colspec(w["b_gate_a"]),
              colspec(wx_bd), colspec(w["b_gate_x"]), colspec(w["lru_param"]), full2(reset)]
    y_rnn, h_rnn = _tiled("lru_fwd", _f_lru_fwd, n_ct, lru_in,
                          [((s_len, d_rnn), BF16, (s_len, ct), "col"), ((s_len, d_rnn), F32, (s_len, ct), "col")])

    qkv_in = [rowspec(qkv)] + tabs + [full2(w["q_norm_g"]), full2(w["kv_norm_g"])]
    qn, kvn, kr = _tiled("qkv_norm", _f_qkv, nt, qkv_in, [rowout(n_q, BF16), rowout(n_kv, BF16), rowout(LANE, BF16)],
                         row_tile=big_tile)
    q_pre = _mm("up_q", qn, w_uq, out_dtype=BF16)
    kv = _mm("up_kv", kvn, w_ukv, out_dtype=BF16)
    o_mla, lse, q_cat = _attn_fwd(q_pre, (cos_f, sin_a, sin_b), kv, kr)

    send_sems, recv_sems, flown, landed, _ = later_flight
    landed = _split_wait("gather_later_wait", send_sems, recv_sems, flown, landed, "gather", after=o_mla)
    for k, g in zip(later_names, _relay_sibling(landed)):
        assemble(k, g)
    w_pr = full["w_proj_rnn"]
    assert ATTN_HEADS_PER_STEP == 2 and 2 * v_head == HEAD_PAD
    swap_pairs = lambda a: a.reshape(N_HEADS // 2, 2, v_head, d)[:, ::-1].reshape(-1, d)
    w_pm = swap_pairs(full["w_proj_mla"])
    w_out = full["w_out"]
    w_up_gate, w_up_val = full["w_up_gate"], full["w_up_val"]
    w_down = full["w_down"]

    p_rnn = _mm("proj_rnn", y_rnn, w_pr, out_dtype=BF16)
    p_mla = _mm("proj_mla", o_mla, w_pm, out_dtype=BF16)
    merge_in = [rowspec(gates), rowspec(p_rnn), rowspec(p_mla)]
    (merged,) = _tiled("merge", _f_merge, nt, merge_in, [rowout(d, BF16)])
    o_tok = _mm("out_proj", merged, w_out)
    res_in = [rowspec(x2d), rowspec(o_tok), full2(gate1), full2(norm2_g), full2(scale2), full2(shift2)]
    x1, h2 = _tiled("res_ln2", _f_res_ln, nt, res_in, [rowout(d, F32), rowout(d, BF16)], row_tile=big_tile)
    u_gate = _mm("ffn_up_gate", h2, w_up_gate, out_dtype=BF16)
    u_val = _mm("ffn_up_val", h2, w_up_val, out_dtype=BF16)
    n_ft = d_ff // LANE
    ffn_in = [colspec(a) for a in (u_gate, u_val, ffn_cw_gate, ffn_cw_val, ffn_cb_gate, ffn_cb_val)]
    (act,) = _tiled("ffn_conv", _f_ffn, n_ft, ffn_in, [((s_len, d_ff), BF16, (s_len, LANE), "col")])
    f_tok = _mm("ffn_down", act, w_down)

    loss_in = [rowspec(x1), rowspec(f_tok), rowspec(tgt), full2(gate2), full2(final_g)]
    dx1, df, loss_row, d_gate2, d_final_g = _tiled(
        "loss", _f_loss_and_grads, nt, loss_in,
        [rowout(d, F32), rowout(d, BF16), ((1, LANE), F32, (1, LANE), "acc"), accout(gate2), accout(final_g)],
        row_tile=big_tile)
    loss = lax.psum(loss_row[0, 0], ("x", "y", "c"))

    d_act = _mm("ffn_down_dx", df, w_down, tb=True, out_dtype=BF16)
    g_w_down = _mm("ffn_down_dw", act, df, ta=True)
    taps = ffn_cw_gate.shape[0]
    du_gate, du_val, g_cw_gate, g_cw_val, g_cb_gate, g_cb_val = _tiled(
        "ffn_conv_bwd", _vjp_of(_f_ffn, 6, (0, 1, 2, 3, 4, 5)), n_ft, ffn_in + [colspec(d_act)],
        [((s_len, d_ff), BF16, (s_len, LANE), "col")] * 2 + [((taps, d_ff), F32, (taps, LANE), "col")] * 2
        + [((1, d_ff), F32, (1, LANE), "col")] * 2)
    dh2 = _mm("ffn_up_gate_dx", du_gate, w_up_gate, tb=True)
    dh2 = _mm("ffn_up_val_dx", du_val, w_up_val, tb=True, add=dh2, out_dtype=BF16)
    g_w_up_halves = [_mm("ffn_up_gate_dw", h2, du_gate, ta=True), _mm("ffn_up_val_dw", h2, du_val, ta=True)]
    g_ffn_cw = jnp.concatenate([g_cw_gate, g_cw_val], axis=1)
    g_ffn_cb = jnp.concatenate([g_cb_gate, g_cb_val], axis=1)

    def chunked(k, gk):
        r, cc = local2d[k].shape
        if kinds[k] == "col":
            gk = gk.reshape(r, 4, cc).transpose(1, 0, 2)
        return gk.reshape(4, 2, r // 2, cc)

    r_up, c_up = local2d["w_up"].shape
    up_chunks = jnp.concatenate([g.reshape(r_up, 2, c_up).transpose(1, 0, 2) for g in g_w_up_halves], axis=0)
    ffn_chunks = {"w_up": up_chunks.reshape(4, 2, r_up // 2, c_up), "w_down": chunked("w_down", g_w_down)}
    ffn_names = [k for k in later_names if k in ffn_chunks]
    ffn_pair_flight = _split_start(
        "reduce_pair_ffn_start", [ffn_chunks[k] for k in ffn_names],
        [jax.ShapeDtypeStruct((4,) + ffn_chunks[k].shape[2:], F32) for k in ffn_names], "pair",
        after=ffn_chunks[ffn_names[-1]])
    gate1_held = gate1 + ffn_pair_flight[4][:1, :1]

    res_bwd = _vjp_of(_f_res_ln, 6, (0, 1, 2, 3, 4, 5))
    dx_res, do_tok, d_gate1, g_norm2, d_scale2, d_shift2 = _tiled(
        "res_ln2_bwd", res_bwd, nt, res_in[:2] + [full2(gate1_held)] + res_in[3:] + [rowspec(dx1), rowspec(dh2)],
        [rowout(d, F32), rowout(d, BF16), accout(gate1), accout(norm2_g), accout(scale2), accout(shift2)],
        row_tile=big_tile)
    d_merged = _mm("out_proj_dx", do_tok, w_out, tb=True, out_dtype=BF16)
    g_w_out = _mm("out_proj_dw", merged, do_tok, ta=True)
    d_gates, dp_rnn, dp_mla = _tiled(
        "merge_bwd", _f_merge_bwd, nt, merge_in + [rowspec(d_merged)],
        [rowout(gates.shape[1], BF16), rowout(d, BF16), rowout(d, BF16)])
    dy_rnn = _mm("proj_rnn_dx", dp_rnn, w_pr, tb=True, out_dtype=BF16)
    g_w_pr = _mm("proj_rnn_dw", y_rnn, dp_rnn, ta=True)
    do_mla = _mm("proj_mla_dx", dp_mla, w_pm, tb=True, out_dtype=BF16)
    g_w_pm = _mm("proj_mla_dw", o_mla, dp_mla, ta=True)

    core = ci.astype(jnp.int32).reshape(1)

    def pair_sums(tag, names, chunks):
        received = _pair_exchange("reduce_pair_exchange_" + tag, chunks)
        return [_reduce_pair("reduce_pair_" + k, ck, got, core) for k, ck, got in zip(names, chunks, received)]

    g_later = {"w_proj_rnn": g_w_pr, "w_proj_mla": swap_pairs(g_w_pm), "w_out": g_w_out}
    send_sems, recv_sems, flown, landed, _ = ffn_pair_flight
    ffn_received = _split_wait("reduce_pair_ffn_wait", send_sems, recv_sems, flown, landed, "pair", after=g_w_pm)
    sums = {k: _reduce_pair("reduce_pair_" + k, ffn_chunks[k], got, core) for k, got in zip(ffn_names, ffn_received)}
    other_names = [k for k in later_names if k not in ffn_chunks]
    sums.update(zip(other_names, pair_sums("ready", other_names, [chunked(k, g_later[k]) for k in other_names])))
    sums_ready = [sums[k] for k in later_names]
    ready_flight = _split_start(
        "reduce_ready_start", sums_ready, [jax.ShapeDtypeStruct(s.shape, s.dtype) for s in sums_ready], "alltoall",
        after=sums_ready[0])
    kr_held = kr + ready_flight[4][:1, :].astype(BF16)

    dq_cat, dkv, dkr = _attn_bwd(q_cat, kv, kr_held, o_mla, lse, do_mla)
    (dq_pre,) = _tiled("rot_q_bwd", _f_rotq_bwd, nt, tabs + [rowspec(dq_cat)],
                       [rowout(q_pre.shape[1], BF16)])
    dqn = _mm("up_q_dx", dq_pre, w_uq, tb=True, out_dtype=BF16)
    g_w_uq = _mm("up_q_dw", qn, dq_pre, ta=True)
    dkvn = _mm("up_kv_dx", dkv, w_ukv, tb=True, out_dtype=BF16)
    g_w_ukv = _mm("up_kv_dw", kvn, dkv, ta=True)
    dqkv, g_q_norm, g_kv_norm = _tiled(
        "qkv_norm_bwd", _f_qkv_bwd, nt, qkv_in + [rowspec(dqn), rowspec(dkvn), rowspec(dkr)],
        [rowout(qkv.shape[1], BF16), accout(w["q_norm_g"]), accout(w["kv_norm_g"])], row_tile=big_tile)

    lru_out = [((s_len, d_rnn), BF16, (s_len, ct), "col")]
    for a in (conv_w, conv_b, wa_bd, w["b_gate_a"], wx_bd, w["b_gate_x"], w["lru_param"]):
        lru_out.append((a.shape, F32, (a.shape[0], ct), "col"))
    dx_rnn, g_conv_w, g_conv_b, g_wa_bd, g_b_a, g_wx_bd, g_b_x, g_lru = _tiled(
        "lru_bwd", _f_lru_bwd, n_ct, lru_in + [colspec(h_rnn), colspec(dy_rnn)], lru_out)

    dh1 = _mm("in_gates_dx", d_gates, w_g, tb=True)
    dh1 = _mm("in_qkv_dx", dqkv, w_qkv, tb=True, add=dh1)
    dh1 = _mm("in_rnn_dx", dx_rnn, w_rnn, tb=True, add=dh1)
    g_w_rnn = _mm("in_rnn_dw", h1, dx_rnn, ta=True)
    g_w_qkv = _mm("in_qkv_dw", h1, dqkv, ta=True)
    g_w_g = _mm("in_gates_dw", h1, d_gates, ta=True)

    g_first = {
        "w_in": jnp.concatenate([g_w_rnn, g_w_qkv[:, :n_q + n_kv],
                                 g_w_qkv[:, n_q + n_kv + QK_NOPE:n_q + n_kv + QK_NOPE + QK_ROPE], g_w_g], axis=1),
        "w_uq": g_w_uq.reshape(n_q, N_HEADS, HEAD_PAD)[:, :, :hd].reshape(n_q, -1),
        "w_ukv": g_w_ukv,
    }
    first_chunks = [chunked(k, g_first[k]) for k in first_names]
    first_pair_flight = _split_start(
        "reduce_pair_first_start", first_chunks,
        [jax.ShapeDtypeStruct((4,) + ck.shape[2:], F32) for ck in first_chunks], "pair", after=first_chunks[0])

    ln_bwd = _vjp_of(_f_ln, 4, (0, 1, 2, 3))

    def ln1_bwd(xv, gv, sc, sh, dxr, dh):
        dx, dg, dsc, dsh = ln_bwd(xv, gv, sc, sh, dh)
        return dx + dxr, dg, dsc, dsh

    ln1_held = [ln1_in[0], full2(norm1_g + first_pair_flight[4][:1, :1])] + ln1_in[2:]
    grad_x, g_norm1, d_scale1, d_shift1 = _tiled(
        "ln1_bwd", ln1_bwd, nt, ln1_held + [rowspec(dx_res), rowspec(dh1)],
        [rowout(d, F32), accout(norm1_g), accout(scale1), accout(shift1)], row_tile=big_tile)

    dmod = jnp.concatenate([d_shift1, d_scale1, d_gate1, d_shift2, d_scale2, d_gate2], axis=1)
    dmod_all = _all_gather("gather_dmod", dmod, ALL7).reshape(8, -1)
    dmod_loc = lax.dynamic_slice_in_dim(dmod_all, chip * n_mod, n_mod, axis=1)
    g_w_ada = _mm("ada_dw", c_act, jnp.pad(dmod_loc, ((0, c_rows - 8), (0, 0))), ta=True)

    g_convs = {"conv_w": g_conv_w, "ffn_conv_w": g_ffn_cw}
    g_small = {
        "b_ada": dmod, "norm1_g": g_norm1, "conv_b": g_conv_b,
        "w_gate_a": _block_diag_pairs_t(g_wa_bd)[None], "b_gate_a": g_b_a,
        "w_gate_x": _block_diag_pairs_t(g_wx_bd)[None], "b_gate_x": g_b_x, "lru_param": g_lru,
        "q_norm_g": g_q_norm, "kv_norm_g": g_kv_norm, "norm2_g": g_norm2,
        "ffn_conv_b": g_ffn_cb, "final_g": d_final_g.reshape(w["final_g"].shape),
    }

    small_flat = jnp.concatenate([g_small[k].reshape(-1) for k in SMALL] + [g_convs[k].reshape(-1) for k, _ in CONVS])
    small_rows = -(-small_flat.shape[0] // (8 * PACK_COLS * PACK_ROW_UNIT)) * PACK_ROW_UNIT
    small_chunk = _pad_rows(small_flat[None], 8 * small_rows).reshape(4, 2, small_rows, PACK_COLS)
    last_names = first_names + ["small"]
    send_sems, recv_sems, flown, landed, _ = first_pair_flight
    first_received = _split_wait("reduce_pair_first_wait", send_sems, recv_sems, flown, landed, "pair", after=small_chunk)
    sums_last = [_reduce_pair("reduce_pair_" + k, ck, got, core)
                 for k, ck, got in zip(first_names, first_chunks, first_received)]
    sums_last += pair_sums("small", ["small"], [small_chunk])
    send_sems, recv_sems, flown, landed, _ = ready_flight
    quads_ready = _split_wait("reduce_ready_wait", send_sems, recv_sems, flown, landed, "alltoall", after=grad_x)
    last_flight = _split_start(
        "reduce_last_start", sums_last, [jax.ShapeDtypeStruct(s.shape, s.dtype) for s in sums_last], "alltoall",
        after=quads_ready[0])
    grads = {"w_ada": g_w_ada[None]}
    delta, new_m, new_v = {}, {}, {}

    def adamw(k):
        shp = w[k].shape
        flip = len(shp) == 3 and shp[-1] % LANE != 0 and shp[-2] % LANE == 0
        view = (lambda a: jnp.swapaxes(a, 1, 2)) if flip else (lambda a: a)
        two_d = (-1, view(w[k]).shape[-1]) if len(shp) > 1 else (1, -1)
        dk, mk, vk = _adamw("adamw_" + k, *[view(a).reshape(two_d) for a in (w[k], grads[k], m_in[k], v_in[k])])
        back = lambda a: view(a.reshape(view(w[k]).shape))
        delta[k], new_m[k], new_v[k] = back(dk), back(mk), back(vk)

    def finish(tag, names, quads, sums, after):
        reduced = {}
        for k, quad, ps in zip(names, quads, sums):
            quad = lax.dynamic_update_index_in_dim(quad, lax.dynamic_index_in_dim(ps, chip, 0, keepdims=True), chip, 0)
            reduced[k] = _reduce_quad("reduce_quad_" + k, quad, after)
        big = [k for k in names if k != "small"]
        for k, both in zip(big, _share_sibling("share_sibling_" + tag, [reduced[k] for k in big])):
            grads[k] = lax.dynamic_update_index_in_dim(both, reduced[k][None], ci, 0).reshape(w[k].shape)
        return reduced

    finish("ready", later_names, quads_ready, sums_ready, after=last_flight[4])
    for k in later_names + ["w_ada"]:
        adamw(k)
    send_sems, recv_sems, flown, landed, _ = last_flight
    quads_last = _split_wait("reduce_last_wait", send_sems, recv_sems, flown, landed, "alltoall",
                             after=delta[later_names[-1]])
    reduced = finish("last", last_names, quads_last, sums_last, after=None)
    small_grad = _all_gather("share_small", reduced["small"], ALL7).reshape(-1)
    off = 0
    for k in SMALL:
        grads[k] = small_grad[off:off + w[k].size].reshape(w[k].shape)
        off += w[k].size
    for k, _ in CONVS:
        r, cc = local2d[k].shape
        whole = small_grad[off:off + 4 * r * cc].reshape(r, 4 * cc)
        grads[k] = lax.dynamic_slice_in_dim(whole, chip * cc, cc, axis=1)[None]
        off += 4 * r * cc
    for k in WEIGHTS:
        if k not in delta:
            adamw(k)

    return (loss, grad_x[None], *[grads[k] for k in WEIGHTS], *[delta[k] for k in WEIGHTS],
            *[new_m[k] for k in WEIGHTS], *[new_v[k] for k in WEIGHTS])


def kernel(x, c, positions, w_ada, b_ada, norm1_g, w_in, conv_w, conv_b, w_gate_a, b_gate_a, w_gate_x, b_gate_x, lru_param, q_norm_g, w_uq, kv_norm_g, w_ukv, w_proj_rnn, w_proj_mla, w_out, norm2_g, w_up, ffn_conv_w, ffn_conv_b, w_down, final_g, loss_target, m_w_ada, m_b_ada, m_norm1_g, m_w_in, m_conv_w, m_conv_b, m_w_gate_a, m_b_gate_a, m_w_gate_x, m_b_gate_x, m_lru_param, m_q_norm_g, m_w_uq, m_kv_norm_g, m_w_ukv, m_w_proj_rnn, m_w_proj_mla, m_w_out, m_norm2_g, m_w_up, m_ffn_conv_w, m_ffn_conv_b, m_w_down, m_final_g, v_w_ada, v_b_ada, v_norm1_g, v_w_in, v_conv_w, v_conv_b, v_w_gate_a, v_b_gate_a, v_w_gate_x, v_b_gate_x, v_lru_param, v_q_norm_g, v_w_uq, v_kv_norm_g, v_w_ukv, v_w_proj_rnn, v_w_proj_mla, v_w_out, v_norm2_g, v_w_up, v_ffn_conv_w, v_ffn_conv_b, v_w_down, v_final_g):
    given = dict(locals())
    w = {k: given[k] for k in WEIGHTS}
    m_in = {k: given["m_" + k] for k in WEIGHTS}
    v_in = {k: given["v_" + k] for k in WEIGHTS}
    return _step(x, c, positions, w, m_in, v_in, loss_target)
```

```python
import functools
import math

import jax
import jax.numpy as jnp
from jax import lax
from jax.experimental import pallas as pl
from jax.experimental.pallas import tpu as pltpu

F32 = jnp.float32
BF16 = jnp.bfloat16

EPS = 1e-6
LRU_C = 8.0
N_HEADS = 16
QK_NOPE = 64
QK_ROPE = 32
HEAD_PAD = 128
ROPE_THETA = 10000.0
ADAM_LR = 0.001
ADAM_B1 = 0.9
ADAM_B2 = 0.999
ADAM_EPS = 1e-08
ADAM_WD = 0.01
ADAM_STEP = 10

LANE = 128
SUBLANES = 8
VMEM_LIMIT = 48 * 1024 * 1024
MM_TILE_M = MM_TILE_N = MM_TILE_K = 1408
PACK_COLS = 1024
PACK_ROW_UNIT = 32
MESH = pl.DeviceIdType.MESH

NN = (((1,), (0,)), ((), ()))
NT = (((1,), (1,)), ((), ()))
TN = (((0,), (0,)), ((), ()))


def _cparams(sem):
    return pltpu.CompilerParams(dimension_semantics=sem, vmem_limit_bytes=VMEM_LIMIT)


def _div_tile(n, cap, unit):
    best = None
    d = unit
    while d <= min(n, cap):
        if n % d == 0:
            best = d
        d += unit
    return n if best is None else best


def _mm(name, a, b, *, ta=False, tb=False, add=None, out_dtype=F32):
    if ta:
        kdim, m = a.shape
    else:
        m, kdim = a.shape
    if tb:
        n, kb = b.shape
    else:
        kb, n = b.shape
    assert kdim == kb, (name, a.shape, b.shape)
    tm = _div_tile(m, MM_TILE_M, 8 if not ta else LANE)
    tn = _div_tile(n, MM_TILE_N, LANE)
    tk = _div_tile(kdim, MM_TILE_K, LANE)
    nk = kdim // tk
    a_spec = pl.BlockSpec((tk, tm), lambda i, j, k: (k, i)) if ta else pl.BlockSpec((tm, tk), lambda i, j, k: (i, k))
    b_spec = pl.BlockSpec((tn, tk), lambda i, j, k: (j, k)) if tb else pl.BlockSpec((tk, tn), lambda i, j, k: (k, j))
    o_spec = pl.BlockSpec((tm, tn), lambda i, j, k: (i, j))
    has_add = add is not None
    dims = ((((0,) if ta else (1,)), ((1,) if tb else (0,))), ((), ()))

    def body(*refs):
        a_ref, b_ref = refs[0], refs[1]
        c_ref = refs[2] if has_add else None
        o_ref = refs[3] if has_add else refs[2]
        prod = lax.dot_general(a_ref[...].astype(BF16), b_ref[...].astype(BF16), dims, preferred_element_type=F32)
        if nk == 1:
            o_ref[...] = (prod + c_ref[...].astype(F32) if has_add else prod).astype(o_ref.dtype)
            return
        acc = refs[-1]
        k = pl.program_id(2)

        @pl.when(k == 0)
        def _():
            acc[...] = prod + c_ref[...].astype(F32) if has_add else prod

        @pl.when(jnp.logical_and(k > 0, k < nk - 1))
        def _():
            acc[...] += prod

        @pl.when(k == nk - 1)
        def _():
            o_ref[...] = (acc[...] + prod).astype(o_ref.dtype)

    ins = [a, b] + ([add] if has_add else [])
    specs = [a_spec, b_spec] + ([o_spec] if has_add else [])
    return pl.pallas_call(
        body, name=name, grid=(m // tm, n // tn, nk), in_specs=specs, out_specs=o_spec,
        out_shape=jax.ShapeDtypeStruct((m, n), out_dtype),
        scratch_shapes=[pltpu.VMEM((tm, tn), F32)] if nk > 1 else [],
        compiler_params=_cparams(("parallel", "parallel", "arbitrary")),
    )(*ins)


_IMAPS = {
    "row": lambda i: (i, 0),
    "col": lambda i: (0, i),
    "full": lambda i: (0, 0),
    "acc": lambda i: (0, 0),
}


def _tiled(name, fn, n, ins, outs, row_tile=None):
    if row_tile is not None:
        rows = next(a.shape[0] for a, _, k in ins if k == "row")
        n = rows // row_tile
        ins = [(a, (row_tile, bs[1]) if k == "row" else bs, k) for a, bs, k in ins]
        outs = [(s, dt, (row_tile, bs[1]) if k == "row" else bs, k) for s, dt, bs, k in outs]
    ni = len(ins)
    is_acc = [k == "acc" for *_, k in outs]

    def body(*refs):
        vals = fn(*[r[...].astype(F32) if r.dtype == BF16 else r[...] for r in refs[:ni]])
        orefs = refs[ni:]
        if any(is_acc):
            @pl.when(pl.program_id(0) == 0)
            def _():
                for r, a in zip(orefs, is_acc):
                    if a:
                        r[...] = jnp.zeros(r.shape, r.dtype)
        for r, v, a in zip(orefs, vals, is_acc):
            if a:
                r[...] += v.astype(r.dtype)
            else:
                r[...] = v.astype(r.dtype)

    res = pl.pallas_call(
        body, name=name, grid=(n,),
        in_specs=[pl.BlockSpec(bs, _IMAPS[k]) for _, bs, k in ins],
        out_specs=[pl.BlockSpec(bs, _IMAPS[k]) for _, _, bs, k in outs],
        out_shape=[jax.ShapeDtypeStruct(s, d) for s, d, _, _ in outs],
        compiler_params=_cparams(("arbitrary",)),
    )(*[a for a, _, _ in ins])
    return tuple(res)


def _vjp_of(fn, nin, diff):
    def g(*args):
        ins, cots = args[:nin], args[nin:]

        def f(*d):
            full = list(ins)
            for i, v in zip(diff, d):
                full[i] = v
            return fn(*full)

        outs, vjp = jax.vjp(f, *[ins[i] for i in diff])
        return vjp(tuple(c.astype(o.dtype) for c, o in zip(cots, outs)))
    return g


def _shift_rows(x, k, fill, up=False):
    n = x.shape[0]
    rows = lax.broadcasted_iota(jnp.int32, x.shape, 0)
    if up:
        return jnp.where(rows < n - k, pltpu.roll(x, n - k, 0), fill)
    return jnp.where(rows >= k, pltpu.roll(x, k, 0), fill)


@functools.partial(jax.custom_vjp, nondiff_argnums=(1,))
def _delay(x, k):
    return _shift_rows(x, k, 0.0)


def _delay_fwd(x, k):
    return _shift_rows(x, k, 0.0), None


def _delay_bwd(k, _, g):
    return (_shift_rows(g, k, 0.0, up=True),)


_delay.defvjp(_delay_fwd, _delay_bwd)


@functools.partial(jax.custom_vjp, nondiff_argnums=(1,))
def _lane_roll(x, s):
    return pltpu.roll(x, s, 1)


def _lane_roll_fwd(x, s):
    return pltpu.roll(x, s, 1), None


def _lane_roll_bwd(s, _, g):
    return (pltpu.roll(g, g.shape[1] - s, 1),)


_lane_roll.defvjp(_lane_roll_fwd, _lane_roll_bwd)


@jax.custom_vjp
def _bdot(x, w):
    return lax.dot_general(x.astype(BF16), w.astype(BF16), NN, preferred_element_type=F32)


def _bdot_fwd(x, w):
    return _bdot(x, w), (x, w)


def _bdot_bwd(res, g):
    x, w = res
    gb = g.astype(BF16)
    dx = lax.dot_general(gb, w.astype(BF16), NT, preferred_element_type=F32)
    dw = lax.dot_general(x.T.astype(BF16), gb, NN, preferred_element_type=F32)
    return dx, dw


_bdot.defvjp(_bdot_fwd, _bdot_bwd)


def _sigmoid(x):
    return 0.5 * (jnp.tanh(0.5 * x) + 1.0)


def _silu(x):
    return x * _sigmoid(x)


def _rms(x, g):
    return x * lax.rsqrt(jnp.mean(x * x, axis=-1, keepdims=True) + EPS) * g


def _causal_conv(x, w, b):
    kw = w.shape[0]
    tap = lax.broadcasted_iota(jnp.int32, w.shape, 0)
    y = b
    for k in range(kw):
        d = kw - 1 - k
        wk = jnp.sum(jnp.where(tap == k, w, 0.0), axis=0, keepdims=True)
        y = y + wk * (x if d == 0 else _delay(x, d))
    return y


def _rotate(x, cos_f, sin_a, sin_b):
    reps = x.shape[1] // LANE
    if reps > 1:
        cos_f, sin_a, sin_b = (jnp.tile(t, (1, reps)) for t in (cos_f, sin_a, sin_b))
    n = x.shape[1]
    half = QK_ROPE // 2
    return x * cos_f + _lane_roll(x, n - half) * sin_a + _lane_roll(x, half) * sin_b


def _softplus_neg(l):
    u = jnp.exp(-jnp.abs(l))
    log1p_u = jnp.where(u < 0.01, u * (1.0 - u * (0.5 - u * (1.0 / 3.0))), jnp.log(1.0 + u))
    return jnp.maximum(-l, 0.0) + log1p_u


def _f_ln(x, g, scale, shift):
    return (_rms(x, g) * (1.0 + scale) + shift,)


def _f_qkv(qkv, cos_f, sin_a, sin_b, qg, kvg):
    nq, nkv = qg.shape[1], kvg.shape[1]
    qn = _rms(qkv[:, :nq], qg)
    kvn = _rms(qkv[:, nq:nq + nkv], kvg)
    kr = _rotate(qkv[:, nq + nkv:], cos_f, sin_a, sin_b)
    return qn, kvn, kr


def _f_qkv_bwd(qkv, cos_f, sin_a, sin_b, qg, kvg, dqn, dkvn, dkr):
    nq, nkv = qg.shape[1], kvg.shape[1]
    _, vjp_q = jax.vjp(_rms, qkv[:, :nq], qg)
    _, vjp_kv = jax.vjp(_rms, qkv[:, nq:nq + nkv], kvg)
    _, vjp_r = jax.vjp(lambda t: _rotate(t, cos_f, sin_a, sin_b), qkv[:, nq + nkv:])
    dq_lat, dqg = vjp_q(dqn)
    dkv_lat, dkvg = vjp_kv(dkvn)
    (dkr_pre,) = vjp_r(dkr)
    return jnp.concatenate([dq_lat, dkv_lat, dkr_pre], axis=1), dqg, dkvg


QK_SCALE = 1.0 / math.sqrt(QK_NOPE + QK_ROPE)
LOG2_E = 1.4426950408889634
LN_2 = 0.6931471805599453


def _f_rotq(q, cos_f, sin_a, sin_b):
    return (_rotate(q, cos_f, sin_a, sin_b) * (QK_SCALE * LOG2_E),)


def _f_rotq_bwd(cos_f, sin_a, sin_b, dq):
    _, vjp = jax.vjp(lambda t: _rotate(t, cos_f, sin_a, sin_b) * QK_SCALE, jnp.zeros_like(dq))
    return vjp(dq)


def _merge(g_rnn, g_mla, p_rnn, p_mla):
    return _sigmoid(g_rnn) * p_rnn + _sigmoid(g_mla) * p_mla


def _f_merge(g, p_rnn, p_mla):
    d = p_rnn.shape[1]
    return (_merge(g[:, :d], g[:, d:], p_rnn, p_mla),)


def _f_merge_bwd(g, p_rnn, p_mla, dm):
    d = p_rnn.shape[1]
    _, vjp = jax.vjp(_merge, g[:, :d], g[:, d:], p_rnn, p_mla)
    dg_rnn, dg_mla, dp_rnn, dp_mla = vjp(dm)
    return jnp.concatenate([dg_rnn, dg_mla], axis=1), dp_rnn, dp_mla


def _f_res_ln(x, o, gate, g2, scale, shift):
    x1 = x + gate * o
    return x1, _rms(x1, g2) * (1.0 + scale) + shift


def _f_ffn(u_gate, u_val, cw_gate, cw_val, cb_gate, cb_val):
    return (_silu(_causal_conv(u_gate, cw_gate, cb_gate)) * _causal_conv(u_val, cw_val, cb_val),)


def _f_loss(x1, f, tgt, gate, fg):
    y = _rms(x1 + gate * f, fg)
    err = (y - tgt) * (y - tgt)
    return 0.5 * jnp.sum(jnp.mean(err, axis=-1, keepdims=True), axis=0, keepdims=True)


def _f_loss_and_grads(x1, f, tgt, gate, fg):
    loss, vjp = jax.vjp(lambda a, b, c, d: _f_loss(a, b, tgt, c, d), x1, f, gate, fg)
    dx1, df, dgate, dfg = vjp(jnp.ones((1, 1), F32))
    return dx1, df, jnp.broadcast_to(loss, (1, LANE)), dgate, dfg


@jax.custom_vjp
def _decay_and_gain(log_a):
    a = jnp.exp(log_a)
    return a, jnp.sqrt(-jnp.tanh(log_a) * (1.0 + a * a))


def _decay_and_gain_fwd(log_a):
    a, gain = _decay_and_gain(log_a)
    return (a, gain), (a, gain)


def _decay_and_gain_bwd(res, g):
    a, gain = res
    return (g[0] * a - g[1] * (a * a) / gain,)


_decay_and_gain.defvjp(_decay_and_gain_fwd, _decay_and_gain_bwd)


def _f_lru_coeffs(xr, cw, cb, wa, ba, wx, bx, lru, reset):
    xc = _causal_conv(xr, cw, cb)
    r = _sigmoid(_bdot(xc, wa) + ba)
    i = _sigmoid(_bdot(xc, wx) + bx)
    log_a = (-LRU_C) * r * _softplus_neg(lru)
    a, mult = _decay_and_gain(log_a)
    is_reset = reset > 0.5
    a = jnp.where(is_reset, 0.0, a)
    mult = jnp.where(is_reset, 1.0, mult)
    return a, mult * (i * xc)


SCAN_BLOCK = 64


def _scan(a, b, up=False):
    n = a.shape[0]
    blk = min(SCAN_BLOCK, n)
    pos = lax.broadcasted_iota(jnp.int32, a.shape, 0) % blk
    k = 1
    while k < blk:
        inside = (pos < blk - k) if up else (pos >= k)
        shift = n - k if up else k
        b = b + a * jnp.where(inside, pltpu.roll(b, shift, 0), 0.0)
        a = a * jnp.where(inside, pltpu.roll(a, shift, 0), 1.0)
        k *= 2
    blocks = range(n // blk)
    carry = jnp.zeros((1,) + a.shape[1:], a.dtype)
    out = [None] * len(blocks)
    for i in (reversed(blocks) if up else blocks):
        rows = slice(i * blk, (i + 1) * blk)
        out[i] = b[rows] + a[rows] * carry
        carry = out[i][:1] if up else out[i][blk - 1:]
    return jnp.concatenate(out, axis=0)


def _f_lru_fwd(xr, cw, cb, wa, ba, wx, bx, lru, reset):
    a, b = _f_lru_coeffs(xr, cw, cb, wa, ba, wx, bx, lru, reset)
    h = _scan(a, b)
    return h, h


def _f_lru_bwd(xr, cw, cb, wa, ba, wx, bx, lru, reset, h, dh):
    (a, _), vjp = jax.vjp(lambda *p: _f_lru_coeffs(*p, reset), xr, cw, cb, wa, ba, wx, bx, lru)
    g = _scan(_shift_rows(a, 1, 0.0, up=True), dh, up=True)
    return vjp((g * _shift_rows(h, 1, 0.0), g))


def _attn_tile(s):
    return 1024 if s >= 2048 else s // 2


def _keys(kv, kr):
    lane = lax.broadcasted_iota(jnp.int32, kv.shape, 1)
    return jnp.where(lane < QK_NOPE, kv, kr)


ATTN_HEADS_PER_STEP = 2


def _scores(q, kc, diagonal):
    s = lax.dot_general(q, kc, NT, preferred_element_type=F32)
    if not diagonal:
        return s
    rows = lax.broadcasted_iota(jnp.int32, s.shape, 0)
    cols = lax.broadcasted_iota(jnp.int32, s.shape, 1)
    return jnp.where(cols - (s.shape[1] - s.shape[0]) <= rows, s, -jnp.inf)


def _sub_blocks(t, diagonal):
    return ((0, t // 2, t // 2), (t // 2, t // 2, t)) if diagonal else ((0, t, t),)


def _causal_pairs(nb, k_major):
    if k_major:
        pairs = [(qb, kb) for kb in range(nb) for qb in range(kb, nb)]
    else:
        pairs = [(qb, kb) for qb in range(nb) for kb in range(qb + 1)]
    return jnp.array([p[0] for p in pairs], jnp.int32), jnp.array([p[1] for p in pairs], jnp.int32)


def _attn_fwd(q_pre, tables, kv, kr):
    s_len = q_pre.shape[0]
    t = _attn_tile(s_len)
    nb = s_len // t
    hp = ATTN_HEADS_PER_STEP
    wide = hp * HEAD_PAD
    q_tab, k_tab = _causal_pairs(nb, k_major=False)

    def body(qt, kt, qp_ref, cos_ref, sina_ref, sinb_ref, kv_ref, kr_ref, o_ref, lse_ref, q_ref, m_s, acc_s):
        pair = pl.program_id(1)
        qi, ki = qt[pair], kt[pair]

        @pl.when(ki == 0)
        def _():
            m_s[...] = jnp.full(m_s.shape, -jnp.inf, F32)
            acc_s[...] = jnp.zeros(acc_s.shape, F32)
            (rotated,) = _f_rotq(qp_ref[...].astype(F32), cos_ref[...], sina_ref[...], sinb_ref[...])
            q_ref[...] = rotated.astype(q_ref.dtype)

        def step(diagonal):
            for h in range(hp):
                lanes = slice(h * HEAD_PAD, (h + 1) * HEAD_PAD)
                for r0, nr, nk in _sub_blocks(t, diagonal):
                    rows = slice(r0, r0 + nr)
                    kvv = kv_ref[:nk, lanes]
                    s = _scores(q_ref[rows, lanes], _keys(kvv, kr_ref[:nk, :]), diagonal)
                    m_old = m_s[h, rows]
                    m_new = jnp.maximum(m_old, jnp.max(s, axis=-1, keepdims=True))
                    alpha = jnp.exp2(m_old - m_new)
                    p = jnp.exp2(s - jnp.tile(m_new, (1, s.shape[1] // HEAD_PAD)))
                    lane = lax.broadcasted_iota(jnp.int32, kvv.shape, 1)
                    ones_and_values = jnp.where(lane < QK_NOPE, jnp.ones_like(kvv), kvv)
                    acc_s[rows, lanes] = alpha * acc_s[rows, lanes] + lax.dot_general(
                        p.astype(BF16), ones_and_values, NN, preferred_element_type=F32)
                    m_s[h, rows] = m_new

        @pl.when(ki < qi)
        def _():
            step(False)

        @pl.when(ki == qi)
        def _():
            step(True)
            lane = lax.broadcasted_iota(jnp.int32, (t, HEAD_PAD), 1)
            outs = []
            for h in range(hp):
                acc = acc_s[:, h * HEAD_PAD:(h + 1) * HEAD_PAD]
                total = acc[:, :1]
                outs.append(acc / total)
                lse_ref[h] = m_s[h][:, :1] + jnp.log(total) * LOG2_E
            o_ref[...] = jnp.where(lane >= QK_NOPE, outs[0], pltpu.roll(outs[1], QK_NOPE, 1)).astype(o_ref.dtype)

    q_rows = lambda h, p, qt, kt: (qt[p], 0)
    grid_spec = pltpu.PrefetchScalarGridSpec(
        num_scalar_prefetch=2, grid=(N_HEADS // hp, q_tab.shape[0]),
        in_specs=[pl.BlockSpec((t, wide), lambda h, p, qt, kt: (qt[p], h)),
                  pl.BlockSpec((t, HEAD_PAD), q_rows), pl.BlockSpec((t, HEAD_PAD), q_rows),
                  pl.BlockSpec((t, HEAD_PAD), q_rows),
                  pl.BlockSpec((t, wide), lambda h, p, qt, kt: (kt[p], h)),
                  pl.BlockSpec((t, HEAD_PAD), lambda h, p, qt, kt: (kt[p], 0))],
        out_specs=[pl.BlockSpec((t, HEAD_PAD), lambda h, p, qt, kt: (qt[p], h)),
                   pl.BlockSpec((hp, t, 1), lambda h, p, qt, kt: (h, qt[p], 0)),
                   pl.BlockSpec((t, wide), lambda h, p, qt, kt: (qt[p], h))],
        scratch_shapes=[pltpu.VMEM((hp, t, HEAD_PAD), F32), pltpu.VMEM((t, wide), F32)])
    return pl.pallas_call(
        body, name="attn_fwd", grid_spec=grid_spec,
        out_shape=[jax.ShapeDtypeStruct((s_len, N_HEADS // hp * HEAD_PAD), BF16),
                   jax.ShapeDtypeStruct((N_HEADS, s_len, 1), F32),
                   jax.ShapeDtypeStruct((s_len, N_HEADS * HEAD_PAD), BF16)],
        compiler_params=_cparams(("arbitrary", "arbitrary")),
    )(q_tab, k_tab, q_pre, *tables, kv, kr)


def _attn_bwd(q, kv, kr, o, lse, do):
    s_len = q.shape[0]
    t = _attn_tile(s_len)
    nb = s_len // t
    hp = ATTN_HEADS_PER_STEP
    wide = hp * HEAD_PAD
    q_tab, k_tab = _causal_pairs(nb, k_major=True)

    def body(qt, kt, q_ref, kv_ref, kr_ref, o_ref, lse_ref, do_ref, dq_ref, dkv_ref, dkr_ref, dk_s, dv_s, dq_s):
        g, pair = pl.program_id(0), pl.program_id(1)
        qb, kb = qt[pair], kt[pair]

        @pl.when(jnp.logical_and(g == 0, pair == 0))
        def _():
            dkr_ref[...] = jnp.zeros(dkr_ref.shape, F32)

        @pl.when(pair == 0)
        def _():
            dq_s[...] = jnp.zeros(dq_s.shape, F32)

        @pl.when(qb == kb)
        def _():
            dk_s[...] = jnp.zeros(dk_s.shape, F32)
            dv_s[...] = jnp.zeros(dv_s.shape, F32)

        def step(diagonal):
            for h in range(hp):
                lanes = slice(h * HEAD_PAD, (h + 1) * HEAD_PAD)
                for r0, nr, nk in _sub_blocks(t, diagonal):
                    rows, keys = slice(r0, r0 + nr), slice(0, nk)
                    qv, kvv = q_ref[rows, lanes], kv_ref[keys, lanes]
                    pair_do = do_ref[rows, :].astype(F32)
                    lane = lax.broadcasted_iota(jnp.int32, pair_do.shape, 1)
                    mine = (lane >= QK_NOPE) if h == 0 else (lane < QK_NOPE)
                    placed = pair_do if h == 0 else pltpu.roll(pair_do, QK_NOPE, 1)
                    dov = jnp.where(lane >= QK_NOPE, placed, 0.0).astype(BF16)
                    delta = jnp.sum(jnp.where(mine, pair_do * o_ref[rows, :].astype(F32), 0.0), axis=-1, keepdims=True)
                    kc = _keys(kvv, kr_ref[keys, :])
                    p = jnp.exp2(_scores(qv, kc, diagonal) - lse_ref[h, rows])
                    dp = lax.dot_general(dov, kvv, NT, preferred_element_type=F32)
                    ds = p * (dp - delta)
                    dv_s[keys, lanes] += lax.dot_general(p.astype(BF16), dov, TN, preferred_element_type=F32)
                    dk_s[keys, lanes] += lax.dot_general(ds.astype(BF16), qv, TN, preferred_element_type=F32)
                    q_rows = pl.ds(pl.multiple_of(qb * t + r0, nr), nr)
                    dq_s[q_rows, lanes] += lax.dot_general(ds.astype(BF16), kc, NN, preferred_element_type=F32)

        @pl.when(qb > kb)
        def _():
            step(False)

        @pl.when(qb == kb)
        def _():
            step(True)

        @pl.when(qb == nb - 1)
        def _():
            lane = lax.broadcasted_iota(jnp.int32, (t, HEAD_PAD), 1)
            rows = pl.ds(pl.multiple_of(kb * t, t), t)
            for h in range(hp):
                lanes = slice(h * HEAD_PAD, (h + 1) * HEAD_PAD)
                dk = dk_s[:, lanes] * LN_2
                dkv_ref[:, lanes] = jnp.where(lane < QK_NOPE, dk, dv_s[:, lanes]).astype(dkv_ref.dtype)
                dkr_ref[rows, :] += jnp.where(lane >= QK_NOPE, dk, 0.0)

        @pl.when(pair == q_tab.shape[0] - 1)
        def _():
            dq_ref[...] = dq_s[...].astype(dq_ref.dtype)

    all_lanes = N_HEADS * HEAD_PAD
    qmap = lambda h, p, qt, kt: (qt[p], h)
    kmap = lambda h, p, qt, kt: (kt[p], h)
    grid_spec = pltpu.PrefetchScalarGridSpec(
        num_scalar_prefetch=2, grid=(N_HEADS // hp, q_tab.shape[0]),
        in_specs=[pl.BlockSpec((t, wide), qmap),
                  pl.BlockSpec((t, wide), kmap),
                  pl.BlockSpec((t, HEAD_PAD), lambda h, p, qt, kt: (kt[p], 0)),
                  pl.BlockSpec((t, HEAD_PAD), qmap),
                  pl.BlockSpec((hp, t, 1), lambda h, p, qt, kt: (h, qt[p], 0)),
                  pl.BlockSpec((t, HEAD_PAD), qmap)],
        out_specs=[pl.BlockSpec((s_len, wide), lambda h, p, qt, kt: (0, h)),
                   pl.BlockSpec((t, wide), kmap),
                   pl.BlockSpec((s_len, HEAD_PAD), lambda h, p, qt, kt: (0, 0))],
        scratch_shapes=[pltpu.VMEM((t, wide), F32), pltpu.VMEM((t, wide), F32), pltpu.VMEM((s_len, wide), F32)])
    return pl.pallas_call(
        body, name="attn_bwd", grid_spec=grid_spec,
        out_shape=[jax.ShapeDtypeStruct((s_len, all_lanes), BF16),
                   jax.ShapeDtypeStruct((s_len, all_lanes), BF16),
                   jax.ShapeDtypeStruct((s_len, HEAD_PAD), F32)],
        compiler_params=_cparams(("arbitrary", "arbitrary")),
    )(q_tab, k_tab, q, kv, kr, o, lse, do)


def _adamw(name, w, g, m, v):
    rows, cols = w.shape
    tr = _div_tile(rows, max(8, (2 * 1024 * 1024) // (4 * cols)), 8)

    def body(w_ref, g_ref, m_ref, v_ref, d_ref, nm_ref, nv_ref):
        gv = g_ref[...]
        nm = ADAM_B1 * m_ref[...] + (1.0 - ADAM_B1) * gv
        nv = ADAM_B2 * v_ref[...] + (1.0 - ADAM_B2) * jnp.square(gv)
        m_hat = nm / (1.0 - ADAM_B1 ** ADAM_STEP)
        v_hat = nv / (1.0 - ADAM_B2 ** ADAM_STEP)
        d_ref[...] = -ADAM_LR * (m_hat / (jnp.sqrt(v_hat) + ADAM_EPS) + ADAM_WD * w_ref[...])
        nm_ref[...] = nm
        nv_ref[...] = nv

    spec = pl.BlockSpec((tr, cols), lambda i: (i, 0))
    return pl.pallas_call(
        body, name=name, grid=(rows // tr,), in_specs=[spec] * 4, out_specs=[spec] * 3,
        out_shape=[jax.ShapeDtypeStruct((rows, cols), F32)] * 3,
        compiler_params=_cparams(("parallel",)),
    )(w, g, m, v)


ALL7 = (1, 2, 3, 4, 5, 6, 7)
CHIPS = (2, 4, 6)


def _all_gather(name, src, masks):
    bits = 0
    for m in masks:
        bits |= m
    nslots = {7: 8, 6: 4}[bits]
    nm = len(masks)

    def slot_of(x, y, c):
        return {7: 4 * x + 2 * y + c, 6: 2 * x + y}[bits]

    def body(src_ref, out_ref, send_sems, recv_sems, local_sem):
        x, y, c = lax.axis_index("x"), lax.axis_index("y"), lax.axis_index("c")
        mine = slot_of(x, y, c)
        own = pltpu.make_async_copy(src_ref, out_ref.at[mine], local_sem)
        own.start()
        copies = []
        for i, m in enumerate(masks):
            peer = _peer(x, y, c, m)
            copies.append((
                pltpu.make_async_remote_copy(
                    src_ref=src_ref, dst_ref=out_ref.at[mine], send_sem=send_sems.at[i], recv_sem=recv_sems.at[i],
                    device_id=peer, device_id_type=MESH),
                pltpu.make_async_remote_copy(
                    src_ref=src_ref, dst_ref=out_ref.at[slot_of(*peer)], send_sem=send_sems.at[i],
                    recv_sem=recv_sems.at[i], device_id=peer, device_id_type=MESH)))
        for send, _ in copies:
            send.start()
        for _, arrival in copies:
            arrival.wait_recv()
        for send, _ in copies:
            send.wait_send()
        own.wait()

    return pl.pallas_call(
        body, name=name,
        in_specs=[pl.BlockSpec(memory_space=pl.ANY)], out_specs=pl.BlockSpec(memory_space=pl.ANY),
        out_shape=jax.ShapeDtypeStruct((nslots,) + tuple(src.shape), src.dtype),
        scratch_shapes=[pltpu.SemaphoreType.DMA((nm,)), pltpu.SemaphoreType.DMA((nm,)), pltpu.SemaphoreType.DMA],
    )(src)


def _peer(x, y, c, m):
    return (1 - x if m & 4 else x, 1 - y if m & 2 else y, 1 - c if m & 1 else c)


def _comm_call(name, emit, srcs, out_shapes, n_sems, in_place=False):
    n = len(srcs)

    def body(*refs):
        src_refs, out_refs = refs[:n], refs[n:n + len(out_shapes)]
        send_sems, recv_sems = refs[-2], refs[-1]

        def copy(src, dst, i, peer):
            return pltpu.make_async_remote_copy(src_ref=src, dst_ref=dst, send_sem=send_sems.at[i],
                                                recv_sem=recv_sems.at[i], device_id=peer, device_id_type=MESH)

        emit(lax.axis_index("x"), lax.axis_index("y"), lax.axis_index("c"), src_refs, out_refs, copy)

    hbm = pl.BlockSpec(memory_space=pl.ANY)
    return pl.pallas_call(
        body, name=name, in_specs=[hbm] * n, out_specs=[hbm] * len(out_shapes), out_shape=out_shapes,
        scratch_shapes=[pltpu.SemaphoreType.DMA((n_sems,)), pltpu.SemaphoreType.DMA((n_sems,))],
        input_output_aliases={i: i for i in range(n)} if in_place else {},
    )(*srcs)


HBM_SPEC = pl.BlockSpec(memory_space=pltpu.HBM)
SEM_SPEC = pl.BlockSpec(memory_space=pltpu.SEMAPHORE)
DATAFLOW = pltpu.SideEffectType.DATAFLOW_SIDE_EFFECTING


def _chip_copies(srcs, lands, send_sems, recv_sems, mode):
    x, y, c = lax.axis_index("x"), lax.axis_index("y"), lax.axis_index("c")
    chip = 2 * x + y
    sends, arrivals = [], []
    if mode == "pair":
        for k, (s, l) in enumerate(zip(srcs, lands)):
            for group in (sends, arrivals):
                group.append(pltpu.make_async_remote_copy(
                    src_ref=s.at[:, 1 - c], dst_ref=l, send_sem=send_sems.at[3 * k], recv_sem=recv_sems.at[3 * k],
                    device_id=(x, y, 1 - c), device_id_type=MESH))
        return sends, arrivals
    for j, m in enumerate(CHIPS):
        px, py, _ = _peer(x, y, c, m)
        theirs = 2 * px + py
        for k, (s, l) in enumerate(zip(srcs, lands)):
            if mode == "gather":
                src, dst, got = s.at[c], l.at[chip, c], l.at[theirs, c]
            else:
                src, dst, got = s.at[theirs], l.at[chip], l.at[theirs]
            for to, group in ((dst, sends), (got, arrivals)):
                group.append(pltpu.make_async_remote_copy(
                    src_ref=src, dst_ref=to, send_sem=send_sems.at[3 * k + j], recv_sem=recv_sems.at[3 * k + j],
                    device_id=(px, py, c), device_id_type=MESH))
    return sends, arrivals


def _split_start(name, srcs, land_shapes, mode, after):
    n = len(srcs)

    def body(*refs):
        sends, _ = _chip_copies(refs[:n], refs[n:2 * n], refs[2 * n + 1], refs[2 * n + 2], mode)
        for cp in sends:
            cp.start()
        token = refs[-1]
        token[...] = jnp.zeros(token.shape, token.dtype)

    hbm = lambda a: pltpu.with_memory_space_constraint(a, pltpu.HBM)
    lands = [hbm(lax.empty(s.shape, s.dtype)) for s in land_shapes]
    bufs = [pltpu.HBM(a.shape, a.dtype) for a in list(srcs) + lands]
    res = pl.pallas_call(
        body, name=name,
        out_shape=(pltpu.SemaphoreType.DMA((3 * n,)), pltpu.SemaphoreType.DMA((3 * n,)), *bufs,
                   jax.ShapeDtypeStruct((SUBLANES, LANE), F32)),
        in_specs=[HBM_SPEC] * (2 * n) + [pl.BlockSpec(memory_space=pl.ANY)],
        out_specs=[SEM_SPEC, SEM_SPEC] + [HBM_SPEC] * (2 * n) + [pl.BlockSpec(memory_space=pltpu.VMEM)],
        input_output_aliases={i: 2 + i for i in range(2 * n)},
        compiler_params=pltpu.CompilerParams(has_side_effects=DATAFLOW),
    )(*[hbm(s) for s in srcs], *lands, after)
    return res[0], res[1], res[2:2 + n], res[2 + n:2 + 2 * n], res[-1]


def _split_wait(name, send_sems, recv_sems, srcs, lands, mode, after):
    n = len(srcs)

    def body(*refs):
        sends, arrivals = _chip_copies(refs[:n], refs[n:2 * n], refs[2 * n], refs[2 * n + 1], mode)
        for cp in sends:
            cp.wait_send()
        for cp in arrivals:
            cp.wait_recv()

    res = pl.pallas_call(
        body, name=name,
        out_shape=tuple(pltpu.HBM(a.shape, a.dtype) for a in list(srcs) + list(lands)),
        in_specs=[HBM_SPEC] * (2 * n) + [SEM_SPEC, SEM_SPEC, pl.BlockSpec(memory_space=pl.ANY)],
        out_specs=[HBM_SPEC] * (2 * n),
        input_output_aliases={i: i for i in range(2 * n)},
        compiler_params=pltpu.CompilerParams(has_side_effects=DATAFLOW),
    )(*srcs, *lands, send_sems, recv_sems, after)
    return res[n:]


def _relay_sibling(lands):
    def emit(x, y, c, srcs, outs, copy):
        sib = (x, y, 1 - c)
        sends, arrivals = [], []
        for j, m in enumerate(CHIPS):
            px, py, _ = _peer(x, y, c, m)
            theirs = 2 * px + py
            for k, (s, o) in enumerate(zip(srcs, outs)):
                sends.append(copy(s.at[theirs, c], o.at[theirs, c], 3 * k + j, sib))
                arrivals.append(copy(s.at[theirs, c], o.at[theirs, 1 - c], 3 * k + j, sib))
        for cp in sends:
            cp.start()
        for cp in arrivals:
            cp.wait_recv()
        for cp in sends:
            cp.wait_send()

    shapes = [jax.ShapeDtypeStruct(l.shape, l.dtype) for l in lands]
    return _comm_call("relay_weights", emit, lands, shapes, 3 * len(lands), in_place=True)


def _gather_weights(halves):
    n = len(halves)

    def emit(x, y, c, srcs, outs, copy):
        chip = 2 * x + y
        sib = (x, y, 1 - c)
        first, relay, landed, relayed = [], [], [], []
        for j, m in enumerate(CHIPS):
            px, py, _ = _peer(x, y, c, m)
            theirs = 2 * px + py
            for k in range(n):
                i = 6 * k + j
                first.append(copy(srcs[k].at[c], outs[k].at[chip, c], i, (px, py, c)))
                landed.append(copy(srcs[k].at[c], outs[k].at[theirs, c], i, (px, py, c)))
                relay.append(copy(outs[k].at[theirs, c], outs[k].at[theirs, c], i + 3, sib))
                relayed.append(copy(outs[k].at[theirs, 1 - c], outs[k].at[theirs, 1 - c], i + 3, sib))
        for cp in first:
            cp.start()
        for arrival, onward in zip(landed, relay):
            arrival.wait_recv()
            onward.start()
        for arrival in relayed:
            arrival.wait_recv()
        for cp in first + relay:
            cp.wait_send()

    shapes = [jax.ShapeDtypeStruct((4,) + h.shape, h.dtype) for h in halves]
    return _comm_call("gather_weights", emit, halves, shapes, 6 * n)


def _pair_exchange(name, chunks):
    def emit(x, y, c, srcs, outs, copy):
        sib = (x, y, 1 - c)
        sends = [copy(s.at[:, 1 - c], o, k, sib) for k, (s, o) in enumerate(zip(srcs, outs))]
        for cp in sends:
            cp.start()
        for cp in sends:
            cp.wait_recv()
        for cp in sends:
            cp.wait_send()

    shapes = [jax.ShapeDtypeStruct((4,) + g.shape[2:], g.dtype) for g in chunks]
    return _comm_call(name, emit, chunks, shapes, len(chunks))


def _share_sibling(name, parts):
    def emit(x, y, c, srcs, outs, copy):
        sib = (x, y, 1 - c)
        sends = [copy(s, o.at[c], k, sib) for k, (s, o) in enumerate(zip(srcs, outs))]
        arrivals = [copy(s, o.at[1 - c], k, sib) for k, (s, o) in enumerate(zip(srcs, outs))]
        for cp in sends:
            cp.start()
        for cp in arrivals:
            cp.wait_recv()
        for cp in sends:
            cp.wait_send()

    shapes = [jax.ShapeDtypeStruct((2,) + p.shape, p.dtype) for p in parts]
    return _comm_call(name, emit, parts, shapes, len(parts))


def _reduce_pair(name, chunk, from_sibling, core):
    n, _, h, cols = chunk.shape
    rt = _div_tile(h, max(16, (1 << 20) // (4 * cols)), 16)

    def body(core_ref, a_ref, b_ref, o_ref):
        o_ref[...] = (a_ref[...] + b_ref[...]).astype(o_ref.dtype)

    grid_spec = pltpu.PrefetchScalarGridSpec(
        num_scalar_prefetch=1, grid=(n, h // rt),
        in_specs=[pl.BlockSpec((None, None, rt, cols), lambda s, i, core_ref: (s, core_ref[0], i, 0)),
                  pl.BlockSpec((None, rt, cols), lambda s, i, core_ref: (s, i, 0))],
        out_specs=pl.BlockSpec((None, rt, cols), lambda s, i, core_ref: (s, i, 0)))
    return pl.pallas_call(
        body, name=name, grid_spec=grid_spec, out_shape=jax.ShapeDtypeStruct((n, h, cols), BF16),
        compiler_params=_cparams(("parallel", "parallel")),
    )(core, chunk, from_sibling)


def _reduce_quad(name, q, after=None):
    _, h, cols = q.shape
    rt = _div_tile(h, max(16, (1 << 20) // (4 * cols)), 16)

    def body(q_ref, *rest):
        v = q_ref[...].astype(F32)
        rest[-1][...] = ((v[0] + v[1]) + v[2]) + v[3]

    held = [] if after is None else [after]
    return pl.pallas_call(
        body, name=name, grid=(h // rt,),
        in_specs=[pl.BlockSpec((4, rt, cols), lambda i: (0, i, 0))] + [pl.BlockSpec(memory_space=pl.ANY)] * len(held),
        out_specs=pl.BlockSpec((rt, cols), lambda i: (i, 0)),
        out_shape=jax.ShapeDtypeStruct((h, cols), F32),
        compiler_params=_cparams(("parallel",)),
    )(q, *held)


def _unshard(seg, kind):
    n, r, c = seg.shape
    if kind == "col":
        return seg.transpose(1, 0, 2).reshape(r, n * c)
    return seg.reshape(n * r, c)


def _pad_rows(flat, rows):
    n, ln = flat.shape
    return jnp.pad(flat, ((0, 0), (0, rows * PACK_COLS - ln))).reshape(n, rows, PACK_COLS)


def _block_diag_pairs(w):
    n2, bs, _ = w.shape
    eye = jnp.eye(2, dtype=w.dtype)
    z = w.reshape(n2 // 2, 2, bs, 1, bs) * eye[None, :, None, :, None]
    return z.reshape(n2 // 2, 2 * bs, 2 * bs).transpose(1, 0, 2).reshape(2 * bs, n2 * bs)


def _block_diag_pairs_t(d, bs=64):
    n = d.shape[1] // (2 * bs)
    z = d.reshape(2 * bs, n, 2 * bs).transpose(1, 0, 2).reshape(n, 2, bs, 2, bs)
    return jnp.stack([z[:, 0, :, 0, :], z[:, 1, :, 1, :]], axis=1).reshape(2 * n, bs, bs)


BIG = (("w_in", "col"), ("w_uq", "col"), ("w_ukv", "col"), ("w_proj_rnn", "row"), ("w_proj_mla", "row"),
       ("w_out", "row"), ("w_up", "col"), ("w_down", "row"))
FIRST_USED = ("w_in", "w_uq", "w_ukv")
CONVS = (("conv_w", "col"), ("ffn_conv_w", "col"))
SMALL = ("b_ada", "norm1_g", "conv_b", "w_gate_a", "b_gate_a", "w_gate_x", "b_gate_x", "lru_param",
         "q_norm_g", "kv_norm_g", "norm2_g", "ffn_conv_b", "final_g")
WEIGHTS = ("w_ada", "b_ada", "norm1_g", "w_in", "conv_w", "conv_b", "w_gate_a", "b_gate_a", "w_gate_x",
           "b_gate_x", "lru_param", "q_norm_g", "w_uq", "kv_norm_g", "w_ukv", "w_proj_rnn", "w_proj_mla",
           "w_out", "norm2_g", "w_up", "ffn_conv_w", "ffn_conv_b", "w_down", "final_g")


def _step(x, c, positions, w, m_in, v_in, loss_target):
    s_len, d = x.shape[1], x.shape[2]
    x2d = x[0]
    tgt = loss_target[0]
    xi, yi, ci = lax.axis_index("x"), lax.axis_index("y"), lax.axis_index("c")
    chip = 2 * xi + yi
    me = 2 * chip + ci
    tile = min(256, s_len)
    nt = s_len // tile

    local2d = {k: w[k][0] for k, _ in BIG + CONVS}
    kinds = dict(BIG)
    halves_bf = {k: local2d[k].astype(BF16).reshape(2, local2d[k].shape[0] // 2, local2d[k].shape[1]) for k, _ in BIG}
    first_names = [k for k, _ in BIG if k in FIRST_USED]
    later_names = [k for k, _ in BIG if k not in FIRST_USED]
    full = {}

    def assemble(k, g):
        g = lax.dynamic_update_index_in_dim(g, halves_bf[k][None], chip, 0).reshape((4,) + local2d[k].shape)
        if k == "w_up":
            full["w_up_gate"], full["w_up_val"] = _unshard(g[:2], kinds[k]), _unshard(g[2:], kinds[k])
        else:
            full[k] = _unshard(g, kinds[k])

    first_got = _gather_weights([halves_bf[k] for k in first_names])
    for k, g in zip(first_names, first_got):
        assemble(k, g)
    conv_flat = jnp.concatenate([local2d[k].reshape(-1) for k, _ in CONVS])
    conv_rows = -(-conv_flat.shape[0] // PACK_COLS)
    conv_all = _all_gather("gather_conv_w", _pad_rows(conv_flat[None], conv_rows)[0], CHIPS)
    conv_all = conv_all.reshape(4, -1)
    off = 0
    for k, kind in CONVS:
        r, cc = local2d[k].shape
        full[k] = _unshard(conv_all[:, off:off + r * cc].reshape(4, r, cc), kind)
        off += r * cc

    d_rnn = w["conv_b"].shape[1]
    n_q, n_kv = w["q_norm_g"].shape[1], w["kv_norm_g"].shape[1]
    w_in = full["w_in"]
    o1, o2, o3 = d_rnn + n_q, d_rnn + n_q + n_kv, d_rnn + n_q + n_kv + QK_ROPE
    w_rnn = w_in[:, :d_rnn]
    zpad = lambda n: jnp.zeros((d, n), BF16)
    w_qkv = jnp.concatenate([w_in[:, d_rnn:o2], zpad(QK_NOPE), w_in[:, o2:o3], zpad(LANE - QK_NOPE - QK_ROPE)], axis=1)
    w_g = w_in[:, o3:]
    hd = QK_NOPE + QK_ROPE
    w_uq = jnp.pad(full["w_uq"].reshape(n_q, N_HEADS, hd), ((0, 0), (0, 0), (0, HEAD_PAD - hd))).reshape(n_q, -1)
    w_ukv = full["w_ukv"]
    v_head = w_ukv.shape[1] // N_HEADS - QK_NOPE
    d_ff = w["ffn_conv_b"].shape[1] // 2
    ffn_cw_gate, ffn_cw_val = full["ffn_conv_w"][:, :d_ff], full["ffn_conv_w"][:, d_ff:]
    ffn_cb_gate, ffn_cb_val = w["ffn_conv_b"][:, :d_ff], w["ffn_conv_b"][:, d_ff:]
    conv_w, conv_b = full["conv_w"], w["conv_b"]
    wa_bd = _block_diag_pairs(w["w_gate_a"][0])
    wx_bd = _block_diag_pairs(w["w_gate_x"][0])

    c_all = _all_gather("gather_c", c, ALL7).reshape(8, d)
    c_rows = 128
    (c_act,) = _tiled("silu_c", lambda v: (_silu(v),), 1, [(jnp.pad(c_all, ((0, c_rows - 8), (0, 0))), (c_rows, d), "full")],
                      [((c_rows, d), F32, (c_rows, d), "full")])
    w_ada = w["w_ada"][0]
    n_mod = w_ada.shape[1]
    b_loc = lax.dynamic_slice_in_dim(w["b_ada"], chip * n_mod, n_mod, axis=1)
    mod_loc = _mm("ada_fwd", c_act, w_ada, add=jnp.broadcast_to(b_loc, (c_rows, n_mod)))
    mod_all = _all_gather("gather_mod", mod_loc[:8], CHIPS)
    mod = lax.dynamic_index_in_dim(mod_all, me, 1, keepdims=False).reshape(1, -1)
    shift1, scale1, gate1, shift2, scale2, gate2 = [mod[:, i * d:(i + 1) * d] for i in range(6)]

    small_done = (mod[:, :1] + conv_all[:1, :1] + first_got[0][0, 0, :1, :1].astype(F32))
    later_flight = _split_start(
        "gather_later_start", [halves_bf[k] for k in later_names],
        [jax.ShapeDtypeStruct((4,) + halves_bf[k].shape, BF16) for k in later_names], "gather", after=small_done)

    half = QK_ROPE // 2
    inv_freq = ROPE_THETA ** (-jnp.arange(half, dtype=F32) / half)
    ang = positions[0].astype(F32)[:, None] * inv_freq
    cos, sin = jnp.cos(ang), jnp.sin(ang)
    one, zero = jnp.ones((s_len, QK_NOPE), F32), jnp.zeros((s_len, half), F32)
    tail = jnp.zeros((s_len, LANE - QK_NOPE - QK_ROPE), F32)
    cos_f = jnp.concatenate([one, cos, cos, tail + 1.0], axis=1)
    sin_a = jnp.concatenate([one * 0.0, -sin, zero, tail], axis=1)
    sin_b = jnp.concatenate([one * 0.0, zero, sin, tail], axis=1)
    reset = (positions[0] == 0).astype(F32)[:, None]
    tabs = [(cos_f, (tile, LANE), "row"), (sin_a, (tile, LANE), "row"), (sin_b, (tile, LANE), "row")]

    def rowspec(a):
        return (a, (tile, a.shape[1]), "row")

    def full2(a):
        return (a, a.shape, "full")

    def rowout(cols, dt):
        return ((s_len, cols), dt, (tile, cols), "row")

    def accout(a):
        return (a.shape, F32, a.shape, "acc")

    norm1_g = w["norm1_g"] + later_flight[4][:1, :1]
    norm2_g, final_g = w["norm2_g"], w["final_g"].reshape(1, d)
    ln1_in = [rowspec(x2d), full2(norm1_g), full2(scale1), full2(shift1)]
    big_tile = min(512, s_len)
    (h1,) = _tiled("ln1", _f_ln, nt, ln1_in, [rowout(d, BF16)], row_tile=big_tile)
    x_rnn = _mm("in_rnn", h1, w_rnn, out_dtype=BF16)
    qkv = _mm("in_qkv", h1, w_qkv)
    gates = _mm("in_gates", h1, w_g, out_dtype=BF16)

    ct = LANE
    n_ct = d_rnn // ct
    colspec = lambda a, width=ct: (a, (a.shape[0], width), "col")
    lru_in = [colspec(x_rnn), colspec(conv_w), colspec(conv_b), colspec(wa_bd), colspec(w["b_gate_a"]),
              colspec(wx_bd), colspec(w["b_gate_x"]), colspec(w["lru_param"]), full2(reset)]
    y_rnn, h_rnn = _tiled("lru_fwd", _f_lru_fwd, n_ct, lru_in,
                          [((s_len, d_rnn), BF16, (s_len, ct), "col"), ((s_len, d_rnn), F32, (s_len, ct), "col")])

    qkv_in = [rowspec(qkv)] + tabs + [full2(w["q_norm_g"]), full2(w["kv_norm_g"])]
    qn, kvn, kr = _tiled("qkv_norm", _f_qkv, nt, qkv_in, [rowout(n_q, BF16), rowout(n_kv, BF16), rowout(LANE, BF16)],
                         row_tile=big_tile)
    q_pre = _mm("up_q", qn, w_uq, out_dtype=BF16)
    kv = _mm("up_kv", kvn, w_ukv, out_dtype=BF16)
    o_mla, lse, q_cat = _attn_fwd(q_pre, (cos_f, sin_a, sin_b), kv, kr)

    send_sems, recv_sems, flown, landed, _ = later_flight
    landed = _split_wait("gather_later_wait", send_sems, recv_sems, flown, landed, "gather", after=o_mla)
    for k, g in zip(later_names, _relay_sibling(landed)):
        assemble(k, g)
    w_pr = full["w_proj_rnn"]
    assert ATTN_HEADS_PER_STEP == 2 and 2 * v_head == HEAD_PAD
    swap_pairs = lambda a: a.reshape(N_HEADS // 2, 2, v_head, d)[:, ::-1].reshape(-1, d)
    w_pm = swap_pairs(full["w_proj_mla"])
    w_out = full["w_out"]
    w_up_gate, w_up_val = full["w_up_gate"], full["w_up_val"]
    w_down = full["w_down"]

    p_rnn = _mm("proj_rnn", y_rnn, w_pr, out_dtype=BF16)
    p_mla = _mm("proj_mla", o_mla, w_pm, out_dtype=BF16)
    merge_in = [rowspec(gates), rowspec(p_rnn), rowspec(p_mla)]
    (merged,) = _tiled("merge", _f_merge, nt, merge_in, [rowout(d, BF16)])
    o_tok = _mm("out_proj", merged, w_out)
    res_in = [rowspec(x2d), rowspec(o_tok), full2(gate1), full2(norm2_g), full2(scale2), full2(shift2)]
    x1, h2 = _tiled("res_ln2", _f_res_ln, nt, res_in, [rowout(d, F32), rowout(d, BF16)], row_tile=big_tile)
    u_gate = _mm("ffn_up_gate", h2, w_up_gate, out_dtype=BF16)
    u_val = _mm("ffn_up_val", h2, w_up_val, out_dtype=BF16)
    n_ft = d_ff // LANE
    ffn_in = [colspec(a) for a in (u_gate, u_val, ffn_cw_gate, ffn_cw_val, ffn_cb_gate, ffn_cb_val)]
    (act,) = _tiled("ffn_conv", _f_ffn, n_ft, ffn_in, [((s_len, d_ff), BF16, (s_len, LANE), "col")])
    f_tok = _mm("ffn_down", act, w_down)

    loss_in = [rowspec(x1), rowspec(f_tok), rowspec(tgt), full2(gate2), full2(final_g)]
    dx1, df, loss_row, d_gate2, d_final_g = _tiled(
        "loss", _f_loss_and_grads, nt, loss_in,
        [rowout(d, F32), rowout(d, BF16), ((1, LANE), F32, (1, LANE), "acc"), accout(gate2), accout(final_g)],
        row_tile=big_tile)
    loss = lax.psum(loss_row[0, 0], ("x", "y", "c"))

    d_act = _mm("ffn_down_dx", df, w_down, tb=True, out_dtype=BF16)
    g_w_down = _mm("ffn_down_dw", act, df, ta=True)
    taps = ffn_cw_gate.shape[0]
    du_gate, du_val, g_cw_gate, g_cw_val, g_cb_gate, g_cb_val = _tiled(
        "ffn_conv_bwd", _vjp_of(_f_ffn, 6, (0, 1, 2, 3, 4, 5)), n_ft, ffn_in + [colspec(d_act)],
        [((s_len, d_ff), BF16, (s_len, LANE), "col")] * 2 + [((taps, d_ff), F32, (taps, LANE), "col")] * 2
        + [((1, d_ff), F32, (1, LANE), "col")] * 2)
    dh2 = _mm("ffn_up_gate_dx", du_gate, w_up_gate, tb=True)
    dh2 = _mm("ffn_up_val_dx", du_val, w_up_val, tb=True, add=dh2, out_dtype=BF16)
    g_w_up_halves = [_mm("ffn_up_gate_dw", h2, du_gate, ta=True), _mm("ffn_up_val_dw", h2, du_val, ta=True)]
    g_ffn_cw = jnp.concatenate([g_cw_gate, g_cw_val], axis=1)
    g_ffn_cb = jnp.concatenate([g_cb_gate, g_cb_val], axis=1)

    def chunked(k, gk):
        r, cc = local2d[k].shape
        if kinds[k] == "col":
            gk = gk.reshape(r, 4, cc).transpose(1, 0, 2)
        return gk.reshape(4, 2, r // 2, cc)

    r_up, c_up = local2d["w_up"].shape
    up_chunks = jnp.concatenate([g.reshape(r_up, 2, c_up).transpose(1, 0, 2) for g in g_w_up_halves], axis=0)
    ffn_chunks = {"w_up": up_chunks.reshape(4, 2, r_up // 2, c_up), "w_down": chunked("w_down", g_w_down)}
    ffn_names = [k for k in later_names if k in ffn_chunks]
    ffn_pair_flight = _split_start(
        "reduce_pair_ffn_start", [ffn_chunks[k] for k in ffn_names],
        [jax.ShapeDtypeStruct((4,) + ffn_chunks[k].shape[2:], F32) for k in ffn_names], "pair",
        after=ffn_chunks[ffn_names[-1]])
    gate1_held = gate1 + ffn_pair_flight[4][:1, :1]

    res_bwd = _vjp_of(_f_res_ln, 6, (0, 1, 2, 3, 4, 5))
    dx_res, do_tok, d_gate1, g_norm2, d_scale2, d_shift2 = _tiled(
        "res_ln2_bwd", res_bwd, nt, res_in[:2] + [full2(gate1_held)] + res_in[3:] + [rowspec(dx1), rowspec(dh2)],
        [rowout(d, F32), rowout(d, BF16), accout(gate1), accout(norm2_g), accout(scale2), accout(shift2)],
        row_tile=big_tile)
    d_merged = _mm("out_proj_dx", do_tok, w_out, tb=True, out_dtype=BF16)
    g_w_out = _mm("out_proj_dw", merged, do_tok, ta=True)
    d_gates, dp_rnn, dp_mla = _tiled(
        "merge_bwd", _f_merge_bwd, nt, merge_in + [rowspec(d_merged)],
        [rowout(gates.shape[1], BF16), rowout(d, BF16), rowout(d, BF16)])
    dy_rnn = _mm("proj_rnn_dx", dp_rnn, w_pr, tb=True, out_dtype=BF16)
    g_w_pr = _mm("proj_rnn_dw", y_rnn, dp_rnn, ta=True)
    do_mla = _mm("proj_mla_dx", dp_mla, w_pm, tb=True, out_dtype=BF16)
    g_w_pm = _mm("proj_mla_dw", o_mla, dp_mla, ta=True)

    core = ci.astype(jnp.int32).reshape(1)

    def pair_sums(tag, names, chunks):
        received = _pair_exchange("reduce_pair_exchange_" + tag, chunks)
        return [_reduce_pair("reduce_pair_" + k, ck, got, core) for k, ck, got in zip(names, chunks, received)]

    g_later = {"w_proj_rnn": g_w_pr, "w_proj_mla": swap_pairs(g_w_pm), "w_out": g_w_out}
    send_sems, recv_sems, flown, landed, _ = ffn_pair_flight
    ffn_received = _split_wait("reduce_pair_ffn_wait", send_sems, recv_sems, flown, landed, "pair", after=g_w_pm)
    sums = {k: _reduce_pair("reduce_pair_" + k, ffn_chunks[k], got, core) for k, got in zip(ffn_names, ffn_received)}
    other_names = [k for k in later_names if k not in ffn_chunks]
    sums.update(zip(other_names, pair_sums("ready", other_names, [chunked(k, g_later[k]) for k in other_names])))
    sums_ready = [sums[k] for k in later_names]
    ready_flight = _split_start(
        "reduce_ready_start", sums_ready, [jax.ShapeDtypeStruct(s.shape, s.dtype) for s in sums_ready], "alltoall",
        after=sums_ready[0])
    kr_held = kr + ready_flight[4][:1, :].astype(BF16)

    dq_cat, dkv, dkr = _attn_bwd(q_cat, kv, kr_held, o_mla, lse, do_mla)
    (dq_pre,) = _tiled("rot_q_bwd", _f_rotq_bwd, nt, tabs + [rowspec(dq_cat)],
                       [rowout(q_pre.shape[1], BF16)])
    dqn = _mm("up_q_dx", dq_pre, w_uq, tb=True, out_dtype=BF16)
    g_w_uq = _mm("up_q_dw", qn, dq_pre, ta=True)
    dkvn = _mm("up_kv_dx", dkv, w_ukv, tb=True, out_dtype=BF16)
    g_w_ukv = _mm("up_kv_dw", kvn, dkv, ta=True)
    dqkv, g_q_norm, g_kv_norm = _tiled(
        "qkv_norm_bwd", _f_qkv_bwd, nt, qkv_in + [rowspec(dqn), rowspec(dkvn), rowspec(dkr)],
        [rowout(qkv.shape[1], BF16), accout(w["q_norm_g"]), accout(w["kv_norm_g"])], row_tile=big_tile)

    lru_out = [((s_len, d_rnn), BF16, (s_len, ct), "col")]
    for a in (conv_w, conv_b, wa_bd, w["b_gate_a"], wx_bd, w["b_gate_x"], w["lru_param"]):
        lru_out.append((a.shape, F32, (a.shape[0], ct), "col"))
    dx_rnn, g_conv_w, g_conv_b, g_wa_bd, g_b_a, g_wx_bd, g_b_x, g_lru = _tiled(
        "lru_bwd", _f_lru_bwd, n_ct, lru_in + [colspec(h_rnn), colspec(dy_rnn)], lru_out)

    dh1 = _mm("in_gates_dx", d_gates, w_g, tb=True)
    dh1 = _mm("in_qkv_dx", dqkv, w_qkv, tb=True, add=dh1)
    dh1 = _mm("in_rnn_dx", dx_rnn, w_rnn, tb=True, add=dh1)
    g_w_rnn = _mm("in_rnn_dw", h1, dx_rnn, ta=True)
    g_w_qkv = _mm("in_qkv_dw", h1, dqkv, ta=True)
    g_w_g = _mm("in_gates_dw", h1, d_gates, ta=True)

    g_first = {
        "w_in": jnp.concatenate([g_w_rnn, g_w_qkv[:, :n_q + n_kv],
                                 g_w_qkv[:, n_q + n_kv + QK_NOPE:n_q + n_kv + QK_NOPE + QK_ROPE], g_w_g], axis=1),
        "w_uq": g_w_uq.reshape(n_q, N_HEADS, HEAD_PAD)[:, :, :hd].reshape(n_q, -1),
        "w_ukv": g_w_ukv,
    }
    first_chunks = [chunked(k, g_first[k]) for k in first_names]
    first_pair_flight = _split_start(
        "reduce_pair_first_start", first_chunks,
        [jax.ShapeDtypeStruct((4,) + ck.shape[2:], F32) for ck in first_chunks], "pair", after=first_chunks[0])

    ln_bwd = _vjp_of(_f_ln, 4, (0, 1, 2, 3))

    def ln1_bwd(xv, gv, sc, sh, dxr, dh):
        dx, dg, dsc, dsh = ln_bwd(xv, gv, sc, sh, dh)
        return dx + dxr, dg, dsc, dsh

    ln1_held = [ln1_in[0], full2(norm1_g + first_pair_flight[4][:1, :1])] + ln1_in[2:]
    grad_x, g_norm1, d_scale1, d_shift1 = _tiled(
        "ln1_bwd", ln1_bwd, nt, ln1_held + [rowspec(dx_res), rowspec(dh1)],
        [rowout(d, F32), accout(norm1_g), accout(scale1), accout(shift1)], row_tile=big_tile)

    dmod = jnp.concatenate([d_shift1, d_scale1, d_gate1, d_shift2, d_scale2, d_gate2], axis=1)
    dmod_all = _all_gather("gather_dmod", dmod, ALL7).reshape(8, -1)
    dmod_loc = lax.dynamic_slice_in_dim(dmod_all, chip * n_mod, n_mod, axis=1)
    g_w_ada = _mm("ada_dw", c_act, jnp.pad(dmod_loc, ((0, c_rows - 8), (0, 0))), ta=True)

    g_convs = {"conv_w": g_conv_w, "ffn_conv_w": g_ffn_cw}
    g_small = {
        "b_ada": dmod, "norm1_g": g_norm1, "conv_b": g_conv_b,
        "w_gate_a": _block_diag_pairs_t(g_wa_bd)[None], "b_gate_a": g_b_a,
        "w_gate_x": _block_diag_pairs_t(g_wx_bd)[None], "b_gate_x": g_b_x, "lru_param": g_lru,
        "q_norm_g": g_q_norm, "kv_norm_g": g_kv_norm, "norm2_g": g_norm2,
        "ffn_conv_b": g_ffn_cb, "final_g": d_final_g.reshape(w["final_g"].shape),
    }

    small_flat = jnp.concatenate([g_small[k].reshape(-1) for k in SMALL] + [g_convs[k].reshape(-1) for k, _ in CONVS])
    small_rows = -(-small_flat.shape[0] // (8 * PACK_COLS * PACK_ROW_UNIT)) * PACK_ROW_UNIT
    small_chunk = _pad_rows(small_flat[None], 8 * small_rows).reshape(4, 2, small_rows, PACK_COLS)
    last_names = first_names + ["small"]
    send_sems, recv_sems, flown, landed, _ = first_pair_flight
    first_received = _split_wait("reduce_pair_first_wait", send_sems, recv_sems, flown, landed, "pair", after=small_chunk)
    sums_last = [_reduce_pair("reduce_pair_" + k, ck, got, core)
                 for k, ck, got in zip(first_names, first_chunks, first_received)]
    sums_last += pair_sums("small", ["small"], [small_chunk])
    send_sems, recv_sems, flown, landed, _ = ready_flight
    quads_ready = _split_wait("reduce_ready_wait", send_sems, recv_sems, flown, landed, "alltoall", after=grad_x)
    last_flight = _split_start(
        "reduce_last_start", sums_last, [jax.ShapeDtypeStruct(s.shape, s.dtype) for s in sums_last], "alltoall",
        after=quads_ready[0])
    grads = {"w_ada": g_w_ada[None]}
    delta, new_m, new_v = {}, {}, {}

    def adamw(k):
        shp = w[k].shape
        flip = len(shp) == 3 and shp[-1] % LANE != 0 and shp[-2] % LANE == 0
        view = (lambda a: jnp.swapaxes(a, 1, 2)) if flip else (lambda a: a)
        two_d = (-1, view(w[k]).shape[-1]) if len(shp) > 1 else (1, -1)
        dk, mk, vk = _adamw("adamw_" + k, *[view(a).reshape(two_d) for a in (w[k], grads[k], m_in[k], v_in[k])])
        back = lambda a: view(a.reshape(view(w[k]).shape))
        delta[k], new_m[k], new_v[k] = back(dk), back(mk), back(vk)

    def finish(tag, names, quads, sums, after):
        reduced = {}
        for k, quad, ps in zip(names, quads, sums):
            quad = lax.dynamic_update_index_in_dim(quad, lax.dynamic_index_in_dim(ps, chip, 0, keepdims=True), chip, 0)
            reduced[k] = _reduce_quad("reduce_quad_" + k, quad, after)
        big = [k for k in names if k != "small"]
        for k, both in zip(big, _share_sibling("share_sibling_" + tag, [reduced[k] for k in big])):
            grads[k] = lax.dynamic_update_index_in_dim(both, reduced[k][None], ci, 0).reshape(w[k].shape)
        return reduced

    finish("ready", later_names, quads_ready, sums_ready, after=last_flight[4])
    for k in later_names + ["w_ada"]:
        adamw(k)
    send_sems, recv_sems, flown, landed, _ = last_flight
    quads_last = _split_wait("reduce_last_wait", send_sems, recv_sems, flown, landed, "alltoall",
                             after=delta[later_names[-1]])
    reduced = finish("last", last_names, quads_last, sums_last, after=None)
    small_grad = _all_gather("share_small", reduced["small"], ALL7).reshape(-1)
    off = 0
    for k in SMALL:
        grads[k] = small_grad[off:off + w[k].size].reshape(w[k].shape)
        off += w[k].size
    for k, _ in CONVS:
        r, cc = local2d[k].shape
        whole = small_grad[off:off + 4 * r * cc].reshape(r, 4 * cc)
        grads[k] = lax.dynamic_slice_in_dim(whole, chip * cc, cc, axis=1)[None]
        off += 4 * r * cc
    for k in WEIGHTS:
        if k not in delta:
            adamw(k)

    return (loss, grad_x[None], *[grads[k] for k in WEIGHTS], *[delta[k] for k in WEIGHTS],
            *[new_m[k] for k in WEIGHTS], *[new_v[k] for k in WEIGHTS])


def kernel(x, c, positions, w_ada, b_ada, norm1_g, w_in, conv_w, conv_b, w_gate_a, b_gate_a, w_gate_x, b_gate_x, lru_param, q_norm_g, w_uq, kv_norm_g, w_ukv, w_proj_rnn, w_proj_mla, w_out, norm2_g, w_up, ffn_conv_w, ffn_conv_b, w_down, final_g, loss_target, m_w_ada, m_b_ada, m_norm1_g, m_w_in, m_conv_w, m_conv_b, m_w_gate_a, m_b_gate_a, m_w_gate_x, m_b_gate_x, m_lru_param, m_q_norm_g, m_w_uq, m_kv_norm_g, m_w_ukv, m_w_proj_rnn, m_w_proj_mla, m_w_out, m_norm2_g, m_w_up, m_ffn_conv_w, m_ffn_conv_b, m_w_down, m_final_g, v_w_ada, v_b_ada, v_norm1_g, v_w_in, v_conv_w, v_conv_b, v_w_gate_a, v_b_gate_a, v_w_gate_x, v_b_gate_x, v_lru_param, v_q_norm_g, v_w_uq, v_kv_norm_g, v_w_ukv, v_w_proj_rnn, v_w_proj_mla, v_w_out, v_norm2_g, v_w_up, v_ffn_conv_w, v_ffn_conv_b, v_w_down, v_final_g):
    given = dict(locals())
    w = {k: given[k] for k in WEIGHTS}
    m_in = {k: given["m_" + k] for k in WEIGHTS}
    v_in = {k: given["v_" + k] for k in WEIGHTS}
    return _step(x, c, positions, w, m_in, v_in, loss_target)
```

```python
import functools
import math

import jax
import jax.numpy as jnp
from jax import lax
from jax.experimental import pallas as pl
from jax.experimental.pallas import tpu as pltpu

F32 = jnp.float32
BF16 = jnp.bfloat16

EPS = 1e-6
LRU_C = 8.0
N_HEADS = 16
QK_NOPE = 64
QK_ROPE = 32
HEAD_PAD = 128
ROPE_THETA = 10000.0
ADAM_LR = 0.001
ADAM_B1 = 0.9
ADAM_B2 = 0.999
ADAM_EPS = 1e-08
ADAM_WD = 0.01
ADAM_STEP = 10

LANE = 128
SUBLANES = 8
VMEM_LIMIT = 48 * 1024 * 1024
MM_TILE_M = MM_TILE_N = MM_TILE_K = 1408
PACK_COLS = 1024
PACK_ROW_UNIT = 32
MESH = pl.DeviceIdType.MESH

NN = (((1,), (0,)), ((), ()))
NT = (((1,), (1,)), ((), ()))
TN = (((0,), (0,)), ((), ()))


def _cparams(sem):
    return pltpu.CompilerParams(dimension_semantics=sem, vmem_limit_bytes=VMEM_LIMIT)


def _div_tile(n, cap, unit):
    best = None
    d = unit
    while d <= min(n, cap):
        if n % d == 0:
            best = d
        d += unit
    return n if best is None else best


def _mm(name, a, b, *, ta=False, tb=False, add=None, out_dtype=F32):
    if ta:
        kdim, m = a.shape
    else:
        m, kdim = a.shape
    if tb:
        n, kb = b.shape
    else:
        kb, n = b.shape
    assert kdim == kb, (name, a.shape, b.shape)
    tm = _div_tile(m, MM_TILE_M, 8 if not ta else LANE)
    tn = _div_tile(n, MM_TILE_N, LANE)
    tk = _div_tile(kdim, MM_TILE_K, LANE)
    nk = kdim // tk
    a_spec = pl.BlockSpec((tk, tm), lambda i, j, k: (k, i)) if ta else pl.BlockSpec((tm, tk), lambda i, j, k: (i, k))
    b_spec = pl.BlockSpec((tn, tk), lambda i, j, k: (j, k)) if tb else pl.BlockSpec((tk, tn), lambda i, j, k: (k, j))
    o_spec = pl.BlockSpec((tm, tn), lambda i, j, k: (i, j))
    has_add = add is not None
    dims = ((((0,) if ta else (1,)), ((1,) if tb else (0,))), ((), ()))

    def body(*refs):
        a_ref, b_ref = refs[0], refs[1]
        c_ref = refs[2] if has_add else None
        o_ref = refs[3] if has_add else refs[2]
        prod = lax.dot_general(a_ref[...].astype(BF16), b_ref[...].astype(BF16), dims, preferred_element_type=F32)
        if nk == 1:
            o_ref[...] = (prod + c_ref[...].astype(F32) if has_add else prod).astype(o_ref.dtype)
            return
        acc = refs[-1]
        k = pl.program_id(2)

        @pl.when(k == 0)
        def _():
            acc[...] = prod + c_ref[...].astype(F32) if has_add else prod

        @pl.when(jnp.logical_and(k > 0, k < nk - 1))
        def _():
            acc[...] += prod

        @pl.when(k == nk - 1)
        def _():
            o_ref[...] = (acc[...] + prod).astype(o_ref.dtype)

    ins = [a, b] + ([add] if has_add else [])
    specs = [a_spec, b_spec] + ([o_spec] if has_add else [])
    return pl.pallas_call(
        body, name=name, grid=(m // tm, n // tn, nk), in_specs=specs, out_specs=o_spec,
        out_shape=jax.ShapeDtypeStruct((m, n), out_dtype),
        scratch_shapes=[pltpu.VMEM((tm, tn), F32)] if nk > 1 else [],
        compiler_params=_cparams(("parallel", "parallel", "arbitrary")),
    )(*ins)


_IMAPS = {
    "row": lambda i: (i, 0),
    "col": lambda i: (0, i),
    "full": lambda i: (0, 0),
    "acc": lambda i: (0, 0),
}


def _tiled(name, fn, n, ins, outs, row_tile=None):
    if row_tile is not None:
        rows = next(a.shape[0] for a, _, k in ins if k == "row")
        n = rows // row_tile
        ins = [(a, (row_tile, bs[1]) if k == "row" else bs, k) for a, bs, k in ins]
        outs = [(s, dt, (row_tile, bs[1]) if k == "row" else bs, k) for s, dt, bs, k in outs]
    ni = len(ins)
    is_acc = [k == "acc" for *_, k in outs]

    def body(*refs):
        vals = fn(*[r[...].astype(F32) if r.dtype == BF16 else r[...] for r in refs[:ni]])
        orefs = refs[ni:]
        if any(is_acc):
            @pl.when(pl.program_id(0) == 0)
            def _():
                for r, a in zip(orefs, is_acc):
                    if a:
                        r[...] = jnp.zeros(r.shape, r.dtype)
        for r, v, a in zip(orefs, vals, is_acc):
            if a:
                r[...] += v.astype(r.dtype)
            else:
                r[...] = v.astype(r.dtype)

    res = pl.pallas_call(
        body, name=name, grid=(n,),
        in_specs=[pl.BlockSpec(bs, _IMAPS[k]) for _, bs, k in ins],
        out_specs=[pl.BlockSpec(bs, _IMAPS[k]) for _, _, bs, k in outs],
        out_shape=[jax.ShapeDtypeStruct(s, d) for s, d, _, _ in outs],
        compiler_params=_cparams(("arbitrary",)),
    )(*[a for a, _, _ in ins])
    return tuple(res)


def _vjp_of(fn, nin, diff):
    def g(*args):
        ins, cots = args[:nin], args[nin:]

        def f(*d):
            full = list(ins)
            for i, v in zip(diff, d):
                full[i] = v
            return fn(*full)

        outs, vjp = jax.vjp(f, *[ins[i] for i in diff])
        return vjp(tuple(c.astype(o.dtype) for c, o in zip(cots, outs)))
    return g


def _shift_rows(x, k, fill, up=False):
    n = x.shape[0]
    rows = lax.broadcasted_iota(jnp.int32, x.shape, 0)
    if up:
        return jnp.where(rows < n - k, pltpu.roll(x, n - k, 0), fill)
    return jnp.where(rows >= k, pltpu.roll(x, k, 0), fill)


@functools.partial(jax.custom_vjp, nondiff_argnums=(1,))
def _delay(x, k):
    return _shift_rows(x, k, 0.0)


def _delay_fwd(x, k):
    return _shift_rows(x, k, 0.0), None


def _delay_bwd(k, _, g):
    return (_shift_rows(g, k, 0.0, up=True),)


_delay.defvjp(_delay_fwd, _delay_bwd)


@functools.partial(jax.custom_vjp, nondiff_argnums=(1,))
def _lane_roll(x, s):
    return pltpu.roll(x, s, 1)


def _lane_roll_fwd(x, s):
    return pltpu.roll(x, s, 1), None


def _lane_roll_bwd(s, _, g):
    return (pltpu.roll(g, g.shape[1] - s, 1),)


_lane_roll.defvjp(_lane_roll_fwd, _lane_roll_bwd)


@jax.custom_vjp
def _bdot(x, w):
    return lax.dot_general(x.astype(BF16), w.astype(BF16), NN, preferred_element_type=F32)


def _bdot_fwd(x, w):
    return _bdot(x, w), (x, w)


def _bdot_bwd(res, g):
    x, w = res
    gb = g.astype(BF16)
    dx = lax.dot_general(gb, w.astype(BF16), NT, preferred_element_type=F32)
    dw = lax.dot_general(x.T.astype(BF16), gb, NN, preferred_element_type=F32)
    return dx, dw


_bdot.defvjp(_bdot_fwd, _bdot_bwd)


def _sigmoid(x):
    return 0.5 * (jnp.tanh(0.5 * x) + 1.0)


def _silu(x):
    return x * _sigmoid(x)


def _rms(x, g):
    return x * lax.rsqrt(jnp.mean(x * x, axis=-1, keepdims=True) + EPS) * g


def _causal_conv(x, w, b):
    kw = w.shape[0]
    tap = lax.broadcasted_iota(jnp.int32, w.shape, 0)
    y = b
    for k in range(kw):
        d = kw - 1 - k
        wk = jnp.sum(jnp.where(tap == k, w, 0.0), axis=0, keepdims=True)
        y = y + wk * (x if d == 0 else _delay(x, d))
    return y


def _rotate(x, cos_f, sin_a, sin_b):
    reps = x.shape[1] // LANE
    if reps > 1:
        cos_f, sin_a, sin_b = (jnp.tile(t, (1, reps)) for t in (cos_f, sin_a, sin_b))
    n = x.shape[1]
    half = QK_ROPE // 2
    return x * cos_f + _lane_roll(x, n - half) * sin_a + _lane_roll(x, half) * sin_b


def _softplus_neg(l):
    u = jnp.exp(-jnp.abs(l))
    log1p_u = jnp.where(u < 0.01, u * (1.0 - u * (0.5 - u * (1.0 / 3.0))), jnp.log(1.0 + u))
    return jnp.maximum(-l, 0.0) + log1p_u


def _f_ln(x, g, scale, shift):
    return (_rms(x, g) * (1.0 + scale) + shift,)


def _f_qkv(qkv, cos_f, sin_a, sin_b, qg, kvg):
    nq, nkv = qg.shape[1], kvg.shape[1]
    qn = _rms(qkv[:, :nq], qg)
    kvn = _rms(qkv[:, nq:nq + nkv], kvg)
    kr = _rotate(qkv[:, nq + nkv:], cos_f, sin_a, sin_b)
    return qn, kvn, kr


def _f_qkv_bwd(qkv, cos_f, sin_a, sin_b, qg, kvg, dqn, dkvn, dkr):
    nq, nkv = qg.shape[1], kvg.shape[1]
    _, vjp_q = jax.vjp(_rms, qkv[:, :nq], qg)
    _, vjp_kv = jax.vjp(_rms, qkv[:, nq:nq + nkv], kvg)
    _, vjp_r = jax.vjp(lambda t: _rotate(t, cos_f, sin_a, sin_b), qkv[:, nq + nkv:])
    dq_lat, dqg = vjp_q(dqn)
    dkv_lat, dkvg = vjp_kv(dkvn)
    (dkr_pre,) = vjp_r(dkr)
    return jnp.concatenate([dq_lat, dkv_lat, dkr_pre], axis=1), dqg, dkvg


QK_SCALE = 1.0 / math.sqrt(QK_NOPE + QK_ROPE)
LOG2_E = 1.4426950408889634
LN_2 = 0.6931471805599453


def _f_rotq(q, cos_f, sin_a, sin_b):
    return (_rotate(q, cos_f, sin_a, sin_b) * (QK_SCALE * LOG2_E),)


def _f_rotq_bwd(cos_f, sin_a, sin_b, dq):
    _, vjp = jax.vjp(lambda t: _rotate(t, cos_f, sin_a, sin_b) * QK_SCALE, jnp.zeros_like(dq))
    return vjp(dq)


def _merge(g_rnn, g_mla, p_rnn, p_mla):
    return _sigmoid(g_rnn) * p_rnn + _sigmoid(g_mla) * p_mla


def _f_merge(g, p_rnn, p_mla):
    d = p_rnn.shape[1]
    return (_merge(g[:, :d], g[:, d:], p_rnn, p_mla),)


def _f_merge_bwd(g, p_rnn, p_mla, dm):
    d = p_rnn.shape[1]
    _, vjp = jax.vjp(_merge, g[:, :d], g[:, d:], p_rnn, p_mla)
    dg_rnn, dg_mla, dp_rnn, dp_mla = vjp(dm)
    return jnp.concatenate([dg_rnn, dg_mla], axis=1), dp_rnn, dp_mla


def _f_res_ln(x, o, gate, g2, scale, shift):
    x1 = x + gate * o
    return x1, _rms(x1, g2) * (1.0 + scale) + shift


def _f_ffn(u_gate, u_val, cw_gate, cw_val, cb_gate, cb_val):
    return (_silu(_causal_conv(u_gate, cw_gate, cb_gate)) * _causal_conv(u_val, cw_val, cb_val),)


def _f_loss(x1, f, tgt, gate, fg):
    y = _rms(x1 + gate * f, fg)
    err = (y - tgt) * (y - tgt)
    return 0.5 * jnp.sum(jnp.mean(err, axis=-1, keepdims=True), axis=0, keepdims=True)


def _f_loss_and_grads(x1, f, tgt, gate, fg):
    loss, vjp = jax.vjp(lambda a, b, c, d: _f_loss(a, b, tgt, c, d), x1, f, gate, fg)
    dx1, df, dgate, dfg = vjp(jnp.ones((1, 1), F32))
    return dx1, df, jnp.broadcast_to(loss, (1, LANE)), dgate, dfg


@jax.custom_vjp
def _decay_and_gain(log_a):
    a = jnp.exp(log_a)
    return a, jnp.sqrt(-jnp.tanh(log_a) * (1.0 + a * a))


def _decay_and_gain_fwd(log_a):
    a, gain = _decay_and_gain(log_a)
    return (a, gain), (a, gain)


def _decay_and_gain_bwd(res, g):
    a, gain = res
    return (g[0] * a - g[1] * (a * a) / gain,)


_decay_and_gain.defvjp(_decay_and_gain_fwd, _decay_and_gain_bwd)


def _f_lru_coeffs(xr, cw, cb, wa, ba, wx, bx, lru, reset):
    xc = _causal_conv(xr, cw, cb)
    r = _sigmoid(_bdot(xc, wa) + ba)
    i = _sigmoid(_bdot(xc, wx) + bx)
    log_a = (-LRU_C) * r * _softplus_neg(lru)
    a, mult = _decay_and_gain(log_a)
    is_reset = reset > 0.5
    a = jnp.where(is_reset, 0.0, a)
    mult = jnp.where(is_reset, 1.0, mult)
    return a, mult * (i * xc)


SCAN_BLOCK = 64


def _scan(a, b, up=False):
    n = a.shape[0]
    blk = min(SCAN_BLOCK, n)
    pos = lax.broadcasted_iota(jnp.int32, a.shape, 0) % blk
    k = 1
    while k < blk:
        inside = (pos < blk - k) if up else (pos >= k)
        shift = n - k if up else k
        b = b + a * jnp.where(inside, pltpu.roll(b, shift, 0), 0.0)
        a = a * jnp.where(inside, pltpu.roll(a, shift, 0), 1.0)
        k *= 2
    blocks = range(n // blk)
    carry = jnp.zeros((1,) + a.shape[1:], a.dtype)
    out = [None] * len(blocks)
    for i in (reversed(blocks) if up else blocks):
        rows = slice(i * blk, (i + 1) * blk)
        out[i] = b[rows] + a[rows] * carry
        carry = out[i][:1] if up else out[i][blk - 1:]
    return jnp.concatenate(out, axis=0)


def _f_lru_fwd(xr, cw, cb, wa, ba, wx, bx, lru, reset):
    a, b = _f_lru_coeffs(xr, cw, cb, wa, ba, wx, bx, lru, reset)
    h = _scan(a, b)
    return h, h


def _f_lru_bwd(xr, cw, cb, wa, ba, wx, bx, lru, reset, h, dh):
    (a, _), vjp = jax.vjp(lambda *p: _f_lru_coeffs(*p, reset), xr, cw, cb, wa, ba, wx, bx, lru)
    g = _scan(_shift_rows(a, 1, 0.0, up=True), dh, up=True)
    return vjp((g * _shift_rows(h, 1, 0.0), g))


def _attn_tile(s):
    return 1024 if s >= 2048 else s // 2


def _keys(kv, kr):
    lane = lax.broadcasted_iota(jnp.int32, kv.shape, 1)
    return jnp.where(lane < QK_NOPE, kv, kr)


ATTN_HEADS_PER_STEP = 2


def _scores(q, kc, diagonal):
    s = lax.dot_general(q, kc, NT, preferred_element_type=F32)
    if not diagonal:
        return s
    rows = lax.broadcasted_iota(jnp.int32, s.shape, 0)
    cols = lax.broadcasted_iota(jnp.int32, s.shape, 1)
    return jnp.where(cols - (s.shape[1] - s.shape[0]) <= rows, s, -jnp.inf)


def _sub_blocks(t, diagonal):
    return ((0, t // 2, t // 2), (t // 2, t // 2, t)) if diagonal else ((0, t, t),)


def _causal_pairs(nb, k_major):
    if k_major:
        pairs = [(qb, kb) for kb in range(nb) for qb in range(kb, nb)]
    else:
        pairs = [(qb, kb) for qb in range(nb) for kb in range(qb + 1)]
    return jnp.array([p[0] for p in pairs], jnp.int32), jnp.array([p[1] for p in pairs], jnp.int32)


def _attn_fwd(q_pre, tables, kv, kr):
    s_len = q_pre.shape[0]
    t = _attn_tile(s_len)
    nb = s_len // t
    hp = ATTN_HEADS_PER_STEP
    wide = hp * HEAD_PAD
    q_tab, k_tab = _causal_pairs(nb, k_major=False)

    def body(qt, kt, qp_ref, cos_ref, sina_ref, sinb_ref, kv_ref, kr_ref, o_ref, lse_ref, q_ref, m_s, acc_s):
        pair = pl.program_id(1)
        qi, ki = qt[pair], kt[pair]

        @pl.when(ki == 0)
        def _():
            m_s[...] = jnp.full(m_s.shape, -jnp.inf, F32)
            acc_s[...] = jnp.zeros(acc_s.shape, F32)
            (rotated,) = _f_rotq(qp_ref[...].astype(F32), cos_ref[...], sina_ref[...], sinb_ref[...])
            q_ref[...] = rotated.astype(q_ref.dtype)

        def step(diagonal):
            for h in range(hp):
                lanes = slice(h * HEAD_PAD, (h + 1) * HEAD_PAD)
                for r0, nr, nk in _sub_blocks(t, diagonal):
                    rows = slice(r0, r0 + nr)
                    kvv = kv_ref[:nk, lanes]
                    s = _scores(q_ref[rows, lanes], _keys(kvv, kr_ref[:nk, :]), diagonal)
                    m_old = m_s[h, rows]
                    m_new = jnp.maximum(m_old, jnp.max(s, axis=-1, keepdims=True))
                    alpha = jnp.exp2(m_old - m_new)
                    p = jnp.exp2(s - jnp.tile(m_new, (1, s.shape[1] // HEAD_PAD)))
                    lane = lax.broadcasted_iota(jnp.int32, kvv.shape, 1)
                    ones_and_values = jnp.where(lane < QK_NOPE, jnp.ones_like(kvv), kvv)
                    acc_s[rows, lanes] = alpha * acc_s[rows, lanes] + lax.dot_general(
                        p.astype(BF16), ones_and_values, NN, preferred_element_type=F32)
                    m_s[h, rows] = m_new

        @pl.when(ki < qi)
        def _():
            step(False)

        @pl.when(ki == qi)
        def _():
            step(True)
            lane = lax.broadcasted_iota(jnp.int32, (t, HEAD_PAD), 1)
            outs = []
            for h in range(hp):
                acc = acc_s[:, h * HEAD_PAD:(h + 1) * HEAD_PAD]
                total = acc[:, :1]
                outs.append(acc / total)
                lse_ref[h] = m_s[h][:, :1] + jnp.log(total) * LOG2_E
            o_ref[...] = jnp.where(lane >= QK_NOPE, outs[0], pltpu.roll(outs[1], QK_NOPE, 1)).astype(o_ref.dtype)

    q_rows = lambda h, p, qt, kt: (qt[p], 0)
    grid_spec = pltpu.PrefetchScalarGridSpec(
        num_scalar_prefetch=2, grid=(N_HEADS // hp, q_tab.shape[0]),
        in_specs=[pl.BlockSpec((t, wide), lambda h, p, qt, kt: (qt[p], h)),
                  pl.BlockSpec((t, HEAD_PAD), q_rows), pl.BlockSpec((t, HEAD_PAD), q_rows),
                  pl.BlockSpec((t, HEAD_PAD), q_rows),
                  pl.BlockSpec((t, wide), lambda h, p, qt, kt: (kt[p], h)),
                  pl.BlockSpec((t, HEAD_PAD), lambda h, p, qt, kt: (kt[p], 0))],
        out_specs=[pl.BlockSpec((t, HEAD_PAD), lambda h, p, qt, kt: (qt[p], h)),
                   pl.BlockSpec((hp, t, 1), lambda h, p, qt, kt: (h, qt[p], 0)),
                   pl.BlockSpec((t, wide), lambda h, p, qt, kt: (qt[p], h))],
        scratch_shapes=[pltpu.VMEM((hp, t, HEAD_PAD), F32), pltpu.VMEM((t, wide), F32)])
    return pl.pallas_call(
        body, name="attn_fwd", grid_spec=grid_spec,
        out_shape=[jax.ShapeDtypeStruct((s_len, N_HEADS // hp * HEAD_PAD), BF16),
                   jax.ShapeDtypeStruct((N_HEADS, s_len, 1), F32),
                   jax.ShapeDtypeStruct((s_len, N_HEADS * HEAD_PAD), BF16)],
        compiler_params=_cparams(("arbitrary", "arbitrary")),
    )(q_tab, k_tab, q_pre, *tables, kv, kr)


def _attn_bwd(q, kv, kr, o, lse, do):
    s_len = q.shape[0]
    t = _attn_tile(s_len)
    nb = s_len // t
    hp = ATTN_HEADS_PER_STEP
    wide = hp * HEAD_PAD
    q_tab, k_tab = _causal_pairs(nb, k_major=True)

    def body(qt, kt, q_ref, kv_ref, kr_ref, o_ref, lse_ref, do_ref, dq_ref, dkv_ref, dkr_ref, dk_s, dv_s, dq_s):
        g, pair = pl.program_id(0), pl.program_id(1)
        qb, kb = qt[pair], kt[pair]

        @pl.when(jnp.logical_and(g == 0, pair == 0))
        def _():
            dkr_ref[...] = jnp.zeros(dkr_ref.shape, F32)

        @pl.when(pair == 0)
        def _():
            dq_s[...] = jnp.zeros(dq_s.shape, F32)

        @pl.when(qb == kb)
        def _():
            dk_s[...] = jnp.zeros(dk_s.shape, F32)
            dv_s[...] = jnp.zeros(dv_s.shape, F32)

        def step(diagonal):
            for h in range(hp):
                lanes = slice(h * HEAD_PAD, (h + 1) * HEAD_PAD)
                for r0, nr, nk in _sub_blocks(t, diagonal):
                    rows, keys = slice(r0, r0 + nr), slice(0, nk)
                    qv, kvv = q_ref[rows, lanes], kv_ref[keys, lanes]
                    pair_do = do_ref[rows, :].astype(F32)
                    lane = lax.broadcasted_iota(jnp.int32, pair_do.shape, 1)
                    mine = (lane >= QK_NOPE) if h == 0 else (lane < QK_NOPE)
                    placed = pair_do if h == 0 else pltpu.roll(pair_do, QK_NOPE, 1)
                    dov = jnp.where(lane >= QK_NOPE, placed, 0.0).astype(BF16)
                    delta = jnp.sum(jnp.where(mine, pair_do * o_ref[rows, :].astype(F32), 0.0), axis=-1, keepdims=True)
                    kc = _keys(kvv, kr_ref[keys, :])
                    p = jnp.exp2(_scores(qv, kc, diagonal) - lse_ref[h, rows])
                    dp = lax.dot_general(dov, kvv, NT, preferred_element_type=F32)
                    ds = p * (dp - delta)
                    dv_s[keys, lanes] += lax.dot_general(p.astype(BF16), dov, TN, preferred_element_type=F32)
                    dk_s[keys, lanes] += lax.dot_general(ds.astype(BF16), qv, TN, preferred_element_type=F32)
                    q_rows = pl.ds(pl.multiple_of(qb * t + r0, nr), nr)
                    dq_s[q_rows, lanes] += lax.dot_general(ds.astype(BF16), kc, NN, preferred_element_type=F32)

        @pl.when(qb > kb)
        def _():
            step(False)

        @pl.when(qb == kb)
        def _():
            step(True)

        @pl.when(qb == nb - 1)
        def _():
            lane = lax.broadcasted_iota(jnp.int32, (t, HEAD_PAD), 1)
            rows = pl.ds(pl.multiple_of(kb * t, t), t)
            for h in range(hp):
                lanes = slice(h * HEAD_PAD, (h + 1) * HEAD_PAD)
                dk = dk_s[:, lanes] * LN_2
                dkv_ref[:, lanes] = jnp.where(lane < QK_NOPE, dk, dv_s[:, lanes]).astype(dkv_ref.dtype)
                dkr_ref[rows, :] += jnp.where(lane >= QK_NOPE, dk, 0.0)

        @pl.when(pair == q_tab.shape[0] - 1)
        def _():
            dq_ref[...] = dq_s[...].astype(dq_ref.dtype)

    all_lanes = N_HEADS * HEAD_PAD
    qmap = lambda h, p, qt, kt: (qt[p], h)
    kmap = lambda h, p, qt, kt: (kt[p], h)
    grid_spec = pltpu.PrefetchScalarGridSpec(
        num_scalar_prefetch=2, grid=(N_HEADS // hp, q_tab.shape[0]),
        in_specs=[pl.BlockSpec((t, wide), qmap),
                  pl.BlockSpec((t, wide), kmap),
                  pl.BlockSpec((t, HEAD_PAD), lambda h, p, qt, kt: (kt[p], 0)),
                  pl.BlockSpec((t, HEAD_PAD), qmap),
                  pl.BlockSpec((hp, t, 1), lambda h, p, qt, kt: (h, qt[p], 0)),
                  pl.BlockSpec((t, HEAD_PAD), qmap)],
        out_specs=[pl.BlockSpec((s_len, wide), lambda h, p, qt, kt: (0, h)),
                   pl.BlockSpec((t, wide), kmap),
                   pl.BlockSpec((s_len, HEAD_PAD), lambda h, p, qt, kt: (0, 0))],
        scratch_shapes=[pltpu.VMEM((t, wide), F32), pltpu.VMEM((t, wide), F32), pltpu.VMEM((s_len, wide), F32)])
    return pl.pallas_call(
        body, name="attn_bwd", grid_spec=grid_spec,
        out_shape=[jax.ShapeDtypeStruct((s_len, all_lanes), BF16),
                   jax.ShapeDtypeStruct((s_len, all_lanes), BF16),
                   jax.ShapeDtypeStruct((s_len, HEAD_PAD), F32)],
        compiler_params=_cparams(("arbitrary", "arbitrary")),
    )(q_tab, k_tab, q, kv, kr, o, lse, do)


def _adamw(name, w, g, m, v):
    rows, cols = w.shape
    tr = _div_tile(rows, max(8, (2 * 1024 * 1024) // (4 * cols)), 8)

    def body(w_ref, g_ref, m_ref, v_ref, d_ref, nm_ref, nv_ref):
        gv = g_ref[...]
        nm = ADAM_B1 * m_ref[...] + (1.0 - ADAM_B1) * gv
        nv = ADAM_B2 * v_ref[...] + (1.0 - ADAM_B2) * jnp.square(gv)
        m_hat = nm / (1.0 - ADAM_B1 ** ADAM_STEP)
        v_hat = nv / (1.0 - ADAM_B2 ** ADAM_STEP)
        d_ref[...] = -ADAM_LR * (m_hat / (jnp.sqrt(v_hat) + ADAM_EPS) + ADAM_WD * w_ref[...])
        nm_ref[...] = nm
        nv_ref[...] = nv

    spec = pl.BlockSpec((tr, cols), lambda i: (i, 0))
    return pl.pallas_call(
        body, name=name, grid=(rows // tr,), in_specs=[spec] * 4, out_specs=[spec] * 3,
        out_shape=[jax.ShapeDtypeStruct((rows, cols), F32)] * 3,
        compiler_params=_cparams(("parallel",)),
    )(w, g, m, v)


ALL7 = (1, 2, 3, 4, 5, 6, 7)
CHIPS = (2, 4, 6)


def _all_gather(name, src, masks):
    bits = 0
    for m in masks:
        bits |= m
    nslots = {7: 8, 6: 4}[bits]
    nm = len(masks)

    def slot_of(x, y, c):
        return {7: 4 * x + 2 * y + c, 6: 2 * x + y}[bits]

    def body(src_ref, out_ref, send_sems, recv_sems, local_sem):
        x, y, c = lax.axis_index("x"), lax.axis_index("y"), lax.axis_index("c")
        mine = slot_of(x, y, c)
        own = pltpu.make_async_copy(src_ref, out_ref.at[mine], local_sem)
        own.start()
        copies = []
        for i, m in enumerate(masks):
            peer = _peer(x, y, c, m)
            copies.append((
                pltpu.make_async_remote_copy(
                    src_ref=src_ref, dst_ref=out_ref.at[mine], send_sem=send_sems.at[i], recv_sem=recv_sems.at[i],
                    device_id=peer, device_id_type=MESH),
                pltpu.make_async_remote_copy(
                    src_ref=src_ref, dst_ref=out_ref.at[slot_of(*peer)], send_sem=send_sems.at[i],
                    recv_sem=recv_sems.at[i], device_id=peer, device_id_type=MESH)))
        for send, _ in copies:
            send.start()
        for _, arrival in copies:
            arrival.wait_recv()
        for send, _ in copies:
            send.wait_send()
        own.wait()

    return pl.pallas_call(
        body, name=name,
        in_specs=[pl.BlockSpec(memory_space=pl.ANY)], out_specs=pl.BlockSpec(memory_space=pl.ANY),
        out_shape=jax.ShapeDtypeStruct((nslots,) + tuple(src.shape), src.dtype),
        scratch_shapes=[pltpu.SemaphoreType.DMA((nm,)), pltpu.SemaphoreType.DMA((nm,)), pltpu.SemaphoreType.DMA],
    )(src)


def _peer(x, y, c, m):
    return (1 - x if m & 4 else x, 1 - y if m & 2 else y, 1 - c if m & 1 else c)


def _comm_call(name, emit, srcs, out_shapes, n_sems, in_place=False):
    n = len(srcs)

    def body(*refs):
        src_refs, out_refs = refs[:n], refs[n:n + len(out_shapes)]
        send_sems, recv_sems = refs[-2], refs[-1]

        def copy(src, dst, i, peer):
            return pltpu.make_async_remote_copy(src_ref=src, dst_ref=dst, send_sem=send_sems.at[i],
                                                recv_sem=recv_sems.at[i], device_id=peer, device_id_type=MESH)

        emit(lax.axis_index("x"), lax.axis_index("y"), lax.axis_index("c"), src_refs, out_refs, copy)

    hbm = pl.BlockSpec(memory_space=pl.ANY)
    return pl.pallas_call(
        body, name=name, in_specs=[hbm] * n, out_specs=[hbm] * len(out_shapes), out_shape=out_shapes,
        scratch_shapes=[pltpu.SemaphoreType.DMA((n_sems,)), pltpu.SemaphoreType.DMA((n_sems,))],
        input_output_aliases={i: i for i in range(n)} if in_place else {},
    )(*srcs)


HBM_SPEC = pl.BlockSpec(memory_space=pltpu.HBM)
SEM_SPEC = pl.BlockSpec(memory_space=pltpu.SEMAPHORE)
DATAFLOW = pltpu.SideEffectType.DATAFLOW_SIDE_EFFECTING


def _chip_copies(srcs, lands, send_sems, recv_sems, mode):
    x, y, c = lax.axis_index("x"), lax.axis_index("y"), lax.axis_index("c")
    chip = 2 * x + y
    sends, arrivals = [], []
    if mode == "pair":
        for k, (s, l) in enumerate(zip(srcs, lands)):
            for group in (sends, arrivals):
                group.append(pltpu.make_async_remote_copy(
                    src_ref=s.at[:, 1 - c], dst_ref=l, send_sem=send_sems.at[3 * k], recv_sem=recv_sems.at[3 * k],
                    device_id=(x, y, 1 - c), device_id_type=MESH))
        return sends, arrivals
    for j, m in enumerate(CHIPS):
        px, py, _ = _peer(x, y, c, m)
        theirs = 2 * px + py
        for k, (s, l) in enumerate(zip(srcs, lands)):
            if mode == "gather":
                src, dst, got = s.at[c], l.at[chip, c], l.at[theirs, c]
            else:
                src, dst, got = s.at[theirs], l.at[chip], l.at[theirs]
            for to, group in ((dst, sends), (got, arrivals)):
                group.append(pltpu.make_async_remote_copy(
                    src_ref=src, dst_ref=to, send_sem=send_sems.at[3 * k + j], recv_sem=recv_sems.at[3 * k + j],
                    device_id=(px, py, c), device_id_type=MESH))
    return sends, arrivals


def _split_start(name, srcs, land_shapes, mode, after):
    n = len(srcs)

    def body(*refs):
        sends, _ = _chip_copies(refs[:n], refs[n:2 * n], refs[2 * n + 1], refs[2 * n + 2], mode)
        for cp in sends:
            cp.start()
        token = refs[-1]
        token[...] = jnp.zeros(token.shape, token.dtype)

    hbm = lambda a: pltpu.with_memory_space_constraint(a, pltpu.HBM)
    lands = [hbm(lax.empty(s.shape, s.dtype)) for s in land_shapes]
    bufs = [pltpu.HBM(a.shape, a.dtype) for a in list(srcs) + lands]
    res = pl.pallas_call(
        body, name=name,
        out_shape=(pltpu.SemaphoreType.DMA((3 * n,)), pltpu.SemaphoreType.DMA((3 * n,)), *bufs,
                   jax.ShapeDtypeStruct((SUBLANES, LANE), F32)),
        in_specs=[HBM_SPEC] * (2 * n) + [pl.BlockSpec(memory_space=pl.ANY)],
        out_specs=[SEM_SPEC, SEM_SPEC] + [HBM_SPEC] * (2 * n) + [pl.BlockSpec(memory_space=pltpu.VMEM)],
        input_output_aliases={i: 2 + i for i in range(2 * n)},
        compiler_params=pltpu.CompilerParams(has_side_effects=DATAFLOW),
    )(*[hbm(s) for s in srcs], *lands, after)
    return res[0], res[1], res[2:2 + n], res[2 + n:2 + 2 * n], res[-1]


def _split_wait(name, send_sems, recv_sems, srcs, lands, mode, after):
    n = len(srcs)

    def body(*refs):
        sends, arrivals = _chip_copies(refs[:n], refs[n:2 * n], refs[2 * n], refs[2 * n + 1], mode)
        for cp in sends:
            cp.wait_send()
        for cp in arrivals:
            cp.wait_recv()

    res = pl.pallas_call(
        body, name=name,
        out_shape=tuple(pltpu.HBM(a.shape, a.dtype) for a in list(srcs) + list(lands)),
        in_specs=[HBM_SPEC] * (2 * n) + [SEM_SPEC, SEM_SPEC, pl.BlockSpec(memory_space=pl.ANY)],
        out_specs=[HBM_SPEC] * (2 * n),
        input_output_aliases={i: i for i in range(2 * n)},
        compiler_params=pltpu.CompilerParams(has_side_effects=DATAFLOW),
    )(*srcs, *lands, send_sems, recv_sems, after)
    return res[n:]


def _relay_sibling(lands):
    def emit(x, y, c, srcs, outs, copy):
        sib = (x, y, 1 - c)
        sends, arrivals = [], []
        for j, m in enumerate(CHIPS):
            px, py, _ = _peer(x, y, c, m)
            theirs = 2 * px + py
            for k, (s, o) in enumerate(zip(srcs, outs)):
                sends.append(copy(s.at[theirs, c], o.at[theirs, c], 3 * k + j, sib))
                arrivals.append(copy(s.at[theirs, c], o.at[theirs, 1 - c], 3 * k + j, sib))
        for cp in sends:
            cp.start()
        for cp in arrivals:
            cp.wait_recv()
        for cp in sends:
            cp.wait_send()

    shapes = [jax.ShapeDtypeStruct(l.shape, l.dtype) for l in lands]
    return _comm_call("relay_weights", emit, lands, shapes, 3 * len(lands), in_place=True)


def _gather_weights(halves):
    n = len(halves)

    def emit(x, y, c, srcs, outs, copy):
        chip = 2 * x + y
        sib = (x, y, 1 - c)
        first, relay, landed, relayed = [], [], [], []
        for j, m in enumerate(CHIPS):
            px, py, _ = _peer(x, y, c, m)
            theirs = 2 * px + py
            for k in range(n):
                i = 6 * k + j
                first.append(copy(srcs[k].at[c], outs[k].at[chip, c], i, (px, py, c)))
                landed.append(copy(srcs[k].at[c], outs[k].at[theirs, c], i, (px, py, c)))
                relay.append(copy(outs[k].at[theirs, c], outs[k].at[theirs, c], i + 3, sib))
                relayed.append(copy(outs[k].at[theirs, 1 - c], outs[k].at[theirs, 1 - c], i + 3, sib))
        for cp in first:
            cp.start()
        for arrival, onward in zip(landed, relay):
            arrival.wait_recv()
            onward.start()
        for arrival in relayed:
            arrival.wait_recv()
        for cp in first + relay:
            cp.wait_send()

    shapes = [jax.ShapeDtypeStruct((4,) + h.shape, h.dtype) for h in halves]
    return _comm_call("gather_weights", emit, halves, shapes, 6 * n)


def _pair_exchange(name, chunks):
    def emit(x, y, c, srcs, outs, copy):
        sib = (x, y, 1 - c)
        sends = [copy(s.at[:, 1 - c], o, k, sib) for k, (s, o) in enumerate(zip(srcs, outs))]
        for cp in sends:
            cp.start()
        for cp in sends:
            cp.wait_recv()
        for cp in sends:
            cp.wait_send()

    shapes = [jax.ShapeDtypeStruct((4,) + g.shape[2:], g.dtype) for g in chunks]
    return _comm_call(name, emit, chunks, shapes, len(chunks))


def _share_sibling(name, parts):
    def emit(x, y, c, srcs, outs, copy):
        sib = (x, y, 1 - c)
        sends = [copy(s, o.at[c], k, sib) for k, (s, o) in enumerate(zip(srcs, outs))]
        arrivals = [copy(s, o.at[1 - c], k, sib) for k, (s, o) in enumerate(zip(srcs, outs))]
        for cp in sends:
            cp.start()
        for cp in arrivals:
            cp.wait_recv()
        for cp in sends:
            cp.wait_send()

    shapes = [jax.ShapeDtypeStruct((2,) + p.shape, p.dtype) for p in parts]
    return _comm_call(name, emit, parts, shapes, len(parts))


def _reduce_pair(name, chunk, from_sibling, core):
    n, _, h, cols = chunk.shape
    rt = _div_tile(h, max(16, (1 << 20) // (4 * cols)), 16)

    def body(core_ref, a_ref, b_ref, o_ref):
        o_ref[...] = (a_ref[...] + b_ref[...]).astype(o_ref.dtype)

    grid_spec = pltpu.PrefetchScalarGridSpec(
        num_scalar_prefetch=1, grid=(n, h // rt),
        in_specs=[pl.BlockSpec((None, None, rt, cols), lambda s, i, core_ref: (s, core_ref[0], i, 0)),
                  pl.BlockSpec((None, rt, cols), lambda s, i, core_ref: (s, i, 0))],
        out_specs=pl.BlockSpec((None, rt, cols), lambda s, i, core_ref: (s, i, 0)))
    return pl.pallas_call(
        body, name=name, grid_spec=grid_spec, out_shape=jax.ShapeDtypeStruct((n, h, cols), BF16),
        compiler_params=_cparams(("parallel", "parallel")),
    )(core, chunk, from_sibling)


def _reduce_quad(name, q, after=None):
    _, h, cols = q.shape
    rt = _div_tile(h, max(16, (1 << 20) // (4 * cols)), 16)

    def body(q_ref, *rest):
        v = q_ref[...].astype(F32)
        rest[-1][...] = ((v[0] + v[1]) + v[2]) + v[3]

    held = [] if after is None else [after]
    return pl.pallas_call(
        body, name=name, grid=(h // rt,),
        in_specs=[pl.BlockSpec((4, rt, cols), lambda i: (0, i, 0))] + [pl.BlockSpec(memory_space=pl.ANY)] * len(held),
        out_specs=pl.BlockSpec((rt, cols), lambda i: (i, 0)),
        out_shape=jax.ShapeDtypeStruct((h, cols), F32),
        compiler_params=_cparams(("parallel",)),
    )(q, *held)


def _unshard(seg, kind):
    n, r, c = seg.shape
    if kind == "col":
        return seg.transpose(1, 0, 2).reshape(r, n * c)
    return seg.reshape(n * r, c)


def _pad_rows(flat, rows):
    n, ln = flat.shape
    return jnp.pad(flat, ((0, 0), (0, rows * PACK_COLS - ln))).reshape(n, rows, PACK_COLS)


def _block_diag_pairs(w):
    n2, bs, _ = w.shape
    eye = jnp.eye(2, dtype=w.dtype)
    z = w.reshape(n2 // 2, 2, bs, 1, bs) * eye[None, :, None, :, None]
    return z.reshape(n2 // 2, 2 * bs, 2 * bs).transpose(1, 0, 2).reshape(2 * bs, n2 * bs)


def _block_diag_pairs_t(d, bs=64):
    n = d.shape[1] // (2 * bs)
    z = d.reshape(2 * bs, n, 2 * bs).transpose(1, 0, 2).reshape(n, 2, bs, 2, bs)
    return jnp.stack([z[:, 0, :, 0, :], z[:, 1, :, 1, :]], axis=1).reshape(2 * n, bs, bs)


BIG = (("w_in", "col"), ("w_uq", "col"), ("w_ukv", "col"), ("w_proj_rnn", "row"), ("w_proj_mla", "row"),
       ("w_out", "row"), ("w_up", "col"), ("w_down", "row"))
FIRST_USED = ("w_in", "w_uq", "w_ukv")
CONVS = (("conv_w", "col"), ("ffn_conv_w", "col"))
SMALL = ("b_ada", "norm1_g", "conv_b", "w_gate_a", "b_gate_a", "w_gate_x", "b_gate_x", "lru_param",
         "q_norm_g", "kv_norm_g", "norm2_g", "ffn_conv_b", "final_g")
WEIGHTS = ("w_ada", "b_ada", "norm1_g", "w_in", "conv_w", "conv_b", "w_gate_a", "b_gate_a", "w_gate_x",
           "b_gate_x", "lru_param", "q_norm_g", "w_uq", "kv_norm_g", "w_ukv", "w_proj_rnn", "w_proj_mla",
           "w_out", "norm2_g", "w_up", "ffn_conv_w", "ffn_conv_b", "w_down", "final_g")


def _step(x, c, positions, w, m_in, v_in, loss_target):
    s_len, d = x.shape[1], x.shape[2]
    x2d = x[0]
    tgt = loss_target[0]
    xi, yi, ci = lax.axis_index("x"), lax.axis_index("y"), lax.axis_index("c")
    chip = 2 * xi + yi
    me = 2 * chip + ci
    tile = min(256, s_len)
    nt = s_len // tile

    local2d = {k: w[k][0] for k, _ in BIG + CONVS}
    kinds = dict(BIG)
    halves_bf = {k: local2d[k].astype(BF16).reshape(2, local2d[k].shape[0] // 2, local2d[k].shape[1]) for k, _ in BIG}
    first_names = [k for k, _ in BIG if k in FIRST_USED]
    later_names = [k for k, _ in BIG if k not in FIRST_USED]
    full = {}

    def assemble(k, g):
        g = lax.dynamic_update_index_in_dim(g, halves_bf[k][None], chip, 0).reshape((4,) + local2d[k].shape)
        if k == "w_up":
            full["w_up_gate"], full["w_up_val"] = _unshard(g[:2], kinds[k]), _unshard(g[2:], kinds[k])
        else:
            full[k] = _unshard(g, kinds[k])

    first_got = _gather_weights([halves_bf[k] for k in first_names])
    for k, g in zip(first_names, first_got):
        assemble(k, g)
    conv_flat = jnp.concatenate([local2d[k].reshape(-1) for k, _ in CONVS])
    conv_rows = -(-conv_flat.shape[0] // PACK_COLS)
    conv_all = _all_gather("gather_conv_w", _pad_rows(conv_flat[None], conv_rows)[0], CHIPS)
    conv_all = conv_all.reshape(4, -1)
    off = 0
    for k, kind in CONVS:
        r, cc = local2d[k].shape
        full[k] = _unshard(conv_all[:, off:off + r * cc].reshape(4, r, cc), kind)
        off += r * cc

    d_rnn = w["conv_b"].shape[1]
    n_q, n_kv = w["q_norm_g"].shape[1], w["kv_norm_g"].shape[1]
    w_in = full["w_in"]
    o1, o2, o3 = d_rnn + n_q, d_rnn + n_q + n_kv, d_rnn + n_q + n_kv + QK_ROPE
    w_rnn = w_in[:, :d_rnn]
    zpad = lambda n: jnp.zeros((d, n), BF16)
    w_qkv = jnp.concatenate([w_in[:, d_rnn:o2], zpad(QK_NOPE), w_in[:, o2:o3], zpad(LANE - QK_NOPE - QK_ROPE)], axis=1)
    w_g = w_in[:, o3:]
    hd = QK_NOPE + QK_ROPE
    w_uq = jnp.pad(full["w_uq"].reshape(n_q, N_HEADS, hd), ((0, 0), (0, 0), (0, HEAD_PAD - hd))).reshape(n_q, -1)
    w_ukv = full["w_ukv"]
    v_head = w_ukv.shape[1] // N_HEADS - QK_NOPE
    d_ff = w["ffn_conv_b"].shape[1] // 2
    ffn_cw_gate, ffn_cw_val = full["ffn_conv_w"][:, :d_ff], full["ffn_conv_w"][:, d_ff:]
    ffn_cb_gate, ffn_cb_val = w["ffn_conv_b"][:, :d_ff], w["ffn_conv_b"][:, d_ff:]
    conv_w, conv_b = full["conv_w"], w["conv_b"]
    wa_bd = _block_diag_pairs(w["w_gate_a"][0])
    wx_bd = _block_diag_pairs(w["w_gate_x"][0])

    c_all = _all_gather("gather_c", c, ALL7).reshape(8, d)
    c_rows = 128
    (c_act,) = _tiled("silu_c", lambda v: (_silu(v),), 1, [(jnp.pad(c_all, ((0, c_rows - 8), (0, 0))), (c_rows, d), "full")],
                      [((c_rows, d), F32, (c_rows, d), "full")])
    w_ada = w["w_ada"][0]
    n_mod = w_ada.shape[1]
    b_loc = lax.dynamic_slice_in_dim(w["b_ada"], chip * n_mod, n_mod, axis=1)
    mod_loc = _mm("ada_fwd", c_act, w_ada, add=jnp.broadcast_to(b_loc, (c_rows, n_mod)))
    mod_all = _all_gather("gather_mod", mod_loc[:8], CHIPS)
    mod = lax.dynamic_index_in_dim(mod_all, me, 1, keepdims=False).reshape(1, -1)
    shift1, scale1, gate1, shift2, scale2, gate2 = [mod[:, i * d:(i + 1) * d] for i in range(6)]

    small_done = (mod[:, :1] + conv_all[:1, :1] + first_got[0][0, 0, :1, :1].astype(F32))
    later_flight = _split_start(
        "gather_later_start", [halves_bf[k] for k in later_names],
        [jax.ShapeDtypeStruct((4,) + halves_bf[k].shape, BF16) for k in later_names], "gather", after=small_done)

    half = QK_ROPE // 2
    inv_freq = ROPE_THETA ** (-jnp.arange(half, dtype=F32) / half)
    ang = positions[0].astype(F32)[:, None] * inv_freq
    cos, sin = jnp.cos(ang), jnp.sin(ang)
    one, zero = jnp.ones((s_len, QK_NOPE), F32), jnp.zeros((s_len, half), F32)
    tail = jnp.zeros((s_len, LANE - QK_NOPE - QK_ROPE), F32)
    cos_f = jnp.concatenate([one, cos, cos, tail + 1.0], axis=1)
    sin_a = jnp.concatenate([one * 0.0, -sin, zero, tail], axis=1)
    sin_b = jnp.concatenate([one * 0.0, zero, sin, tail], axis=1)
    reset = (positions[0] == 0).astype(F32)[:, None]
    tabs = [(cos_f, (tile, LANE), "row"), (sin_a, (tile, LANE), "row"), (sin_b, (tile, LANE), "row")]

    def rowspec(a):
        return (a, (tile, a.shape[1]), "row")

    def full2(a):
        return (a, a.shape, "full")

    def rowout(cols, dt):
        return ((s_len, cols), dt, (tile, cols), "row")

    def accout(a):
        return (a.shape, F32, a.shape, "acc")

    norm1_g = w["norm1_g"] + later_flight[4][:1, :1]
    norm2_g, final_g = w["norm2_g"], w["final_g"].reshape(1, d)
    ln1_in = [rowspec(x2d), full2(norm1_g), full2(scale1), full2(shift1)]
    big_tile = min(512, s_len)
    (h1,) = _tiled("ln1", _f_ln, nt, ln1_in, [rowout(d, BF16)], row_tile=big_tile)
    x_rnn = _mm("in_rnn", h1, w_rnn, out_dtype=BF16)
    qkv = _mm("in_qkv", h1, w_qkv)
    gates = _mm("in_gates", h1, w_g, out_dtype=BF16)

    ct = LANE
    n_ct = d_rnn // ct
    colspec = lambda a, width=ct: (a, (a.shape[0], width), "col")
    lru_in = [colspec(x_rnn), colspec(conv_w), colspec(conv_b), colspec(wa_bd), colspec(w["b_gate_a"]),
              colspec(wx_bd), colspec(w["b_gate_x"]), colspec(w["lru_param"]), full2(reset)]
    y_rnn, h_rnn = _tiled("lru_fwd", _f_lru_fwd, n_ct, lru_in,
                          [((s_len, d_rnn), BF16, (s_len, ct), "col"), ((s_len, d_rnn), F32, (s_len, ct), "col")])

    qkv_in = [rowspec(qkv)] + tabs + [full2(w["q_norm_g"]), full2(w["kv_norm_g"])]
    qn, kvn, kr = _tiled("qkv_norm", _f_qkv, nt, qkv_in, [rowout(n_q, BF16), rowout(n_kv, BF16), rowout(LANE, BF16)],
                         row_tile=big_tile)
    q_pre = _mm("up_q", qn, w_uq, out_dtype=BF16)
    kv = _mm("up_kv", kvn, w_ukv, out_dtype=BF16)
    o_mla, lse, q_cat = _attn_fwd(q_pre, (cos_f, sin_a, sin_b), kv, kr)

    send_sems, recv_sems, flown, landed, _ = later_flight
    landed = _split_wait("gather_later_wait", send_sems, recv_sems, flown, landed, "gather", after=o_mla)
    for k, g in zip(later_names, _relay_sibling(landed)):
        assemble(k, g)
    w_pr = full["w_proj_rnn"]
    assert ATTN_HEADS_PER_STEP == 2 and 2 * v_head == HEAD_PAD
    swap_pairs = lambda a: a.reshape(N_HEADS // 2, 2, v_head, d)[:, ::-1].reshape(-1, d)
    w_pm = swap_pairs(full["w_proj_mla"])
    w_out = full["w_out"]
    w_up_gate, w_up_val = full["w_up_gate"], full["w_up_val"]
    w_down = full["w_down"]

    p_rnn = _mm("proj_rnn", y_rnn, w_pr, out_dtype=BF16)
    p_mla = _mm("proj_mla", o_mla, w_pm, out_dtype=BF16)
    merge_in = [rowspec(gates), rowspec(p_rnn), rowspec(p_mla)]
    (merged,) = _tiled("merge", _f_merge, nt, merge_in, [rowout(d, BF16)], row_tile=big_tile)
    o_tok = _mm("out_proj", merged, w_out)
    res_in = [rowspec(x2d), rowspec(o_tok), full2(gate1), full2(norm2_g), full2(scale2), full2(shift2)]
    x1, h2 = _tiled("res_ln2", _f_res_ln, nt, res_in, [rowout(d, F32), rowout(d, BF16)], row_tile=big_tile)
    u_gate = _mm("ffn_up_gate", h2, w_up_gate, out_dtype=BF16)
    u_val = _mm("ffn_up_val", h2, w_up_val, out_dtype=BF16)
    n_ft = d_ff // LANE
    ffn_in = [colspec(a) for a in (u_gate, u_val, ffn_cw_gate, ffn_cw_val, ffn_cb_gate, ffn_cb_val)]
    (act,) = _tiled("ffn_conv", _f_ffn, n_ft, ffn_in, [((s_len, d_ff), BF16, (s_len, LANE), "col")])
    f_tok = _mm("ffn_down", act, w_down)

    loss_in = [rowspec(x1), rowspec(f_tok), rowspec(tgt), full2(gate2), full2(final_g)]
    dx1, df, loss_row, d_gate2, d_final_g = _tiled(
        "loss", _f_loss_and_grads, nt, loss_in,
        [rowout(d, F32), rowout(d, BF16), ((1, LANE), F32, (1, LANE), "acc"), accout(gate2), accout(final_g)],
        row_tile=big_tile)
    loss = lax.psum(loss_row[0, 0], ("x", "y", "c"))

    d_act = _mm("ffn_down_dx", df, w_down, tb=True, out_dtype=BF16)
    g_w_down = _mm("ffn_down_dw", act, df, ta=True)
    taps = ffn_cw_gate.shape[0]
    du_gate, du_val, g_cw_gate, g_cw_val, g_cb_gate, g_cb_val = _tiled(
        "ffn_conv_bwd", _vjp_of(_f_ffn, 6, (0, 1, 2, 3, 4, 5)), n_ft, ffn_in + [colspec(d_act)],
        [((s_len, d_ff), BF16, (s_len, LANE), "col")] * 2 + [((taps, d_ff), F32, (taps, LANE), "col")] * 2
        + [((1, d_ff), F32, (1, LANE), "col")] * 2)
    dh2 = _mm("ffn_up_gate_dx", du_gate, w_up_gate, tb=True)
    dh2 = _mm("ffn_up_val_dx", du_val, w_up_val, tb=True, add=dh2, out_dtype=BF16)
    g_w_up_halves = [_mm("ffn_up_gate_dw", h2, du_gate, ta=True), _mm("ffn_up_val_dw", h2, du_val, ta=True)]
    g_ffn_cw = jnp.concatenate([g_cw_gate, g_cw_val], axis=1)
    g_ffn_cb = jnp.concatenate([g_cb_gate, g_cb_val], axis=1)

    def chunked(k, gk):
        r, cc = local2d[k].shape
        if kinds[k] == "col":
            gk = gk.reshape(r, 4, cc).transpose(1, 0, 2)
        return gk.reshape(4, 2, r // 2, cc)

    r_up, c_up = local2d["w_up"].shape
    up_chunks = jnp.concatenate([g.reshape(r_up, 2, c_up).transpose(1, 0, 2) for g in g_w_up_halves], axis=0)
    ffn_chunks = {"w_up": up_chunks.reshape(4, 2, r_up // 2, c_up), "w_down": chunked("w_down", g_w_down)}
    ffn_names = [k for k in later_names if k in ffn_chunks]
    ffn_pair_flight = _split_start(
        "reduce_pair_ffn_start", [ffn_chunks[k] for k in ffn_names],
        [jax.ShapeDtypeStruct((4,) + ffn_chunks[k].shape[2:], F32) for k in ffn_names], "pair",
        after=ffn_chunks[ffn_names[-1]])
    gate1_held = gate1 + ffn_pair_flight[4][:1, :1]

    res_bwd = _vjp_of(_f_res_ln, 6, (0, 1, 2, 3, 4, 5))
    dx_res, do_tok, d_gate1, g_norm2, d_scale2, d_shift2 = _tiled(
        "res_ln2_bwd", res_bwd, nt, res_in[:2] + [full2(gate1_held)] + res_in[3:] + [rowspec(dx1), rowspec(dh2)],
        [rowout(d, F32), rowout(d, BF16), accout(gate1), accout(norm2_g), accout(scale2), accout(shift2)],
        row_tile=big_tile)
    d_merged = _mm("out_proj_dx", do_tok, w_out, tb=True, out_dtype=BF16)
    g_w_out = _mm("out_proj_dw", merged, do_tok, ta=True)
    d_gates, dp_rnn, dp_mla = _tiled(
        "merge_bwd", _f_merge_bwd, nt, merge_in + [rowspec(d_merged)],
        [rowout(gates.shape[1], BF16), rowout(d, BF16), rowout(d, BF16)])
    dy_rnn = _mm("proj_rnn_dx", dp_rnn, w_pr, tb=True, out_dtype=BF16)
    g_w_pr = _mm("proj_rnn_dw", y_rnn, dp_rnn, ta=True)
    do_mla = _mm("proj_mla_dx", dp_mla, w_pm, tb=True, out_dtype=BF16)
    g_w_pm = _mm("proj_mla_dw", o_mla, dp_mla, ta=True)

    core = ci.astype(jnp.int32).reshape(1)

    def pair_sums(tag, names, chunks):
        received = _pair_exchange("reduce_pair_exchange_" + tag, chunks)
        return [_reduce_pair("reduce_pair_" + k, ck, got, core) for k, ck, got in zip(names, chunks, received)]

    g_later = {"w_proj_rnn": g_w_pr, "w_proj_mla": swap_pairs(g_w_pm), "w_out": g_w_out}
    send_sems, recv_sems, flown, landed, _ = ffn_pair_flight
    ffn_received = _split_wait("reduce_pair_ffn_wait", send_sems, recv_sems, flown, landed, "pair", after=g_w_pm)
    sums = {k: _reduce_pair("reduce_pair_" + k, ffn_chunks[k], got, core) for k, got in zip(ffn_names, ffn_received)}
    other_names = [k for k in later_names if k not in ffn_chunks]
    sums.update(zip(other_names, pair_sums("ready", other_names, [chunked(k, g_later[k]) for k in other_names])))
    sums_ready = [sums[k] for k in later_names]
    ready_flight = _split_start(
        "reduce_ready_start", sums_ready, [jax.ShapeDtypeStruct(s.shape, s.dtype) for s in sums_ready], "alltoall",
        after=sums_ready[0])
    kr_held = kr + ready_flight[4][:1, :].astype(BF16)

    dq_cat, dkv, dkr = _attn_bwd(q_cat, kv, kr_held, o_mla, lse, do_mla)
    (dq_pre,) = _tiled("rot_q_bwd", _f_rotq_bwd, nt, tabs + [rowspec(dq_cat)],
                       [rowout(q_pre.shape[1], BF16)])
    dqn = _mm("up_q_dx", dq_pre, w_uq, tb=True, out_dtype=BF16)
    g_w_uq = _mm("up_q_dw", qn, dq_pre, ta=True)
    dkvn = _mm("up_kv_dx", dkv, w_ukv, tb=True, out_dtype=BF16)
    g_w_ukv = _mm("up_kv_dw", kvn, dkv, ta=True)
    dqkv, g_q_norm, g_kv_norm = _tiled(
        "qkv_norm_bwd", _f_qkv_bwd, nt, qkv_in + [rowspec(dqn), rowspec(dkvn), rowspec(dkr)],
        [rowout(qkv.shape[1], BF16), accout(w["q_norm_g"]), accout(w["kv_norm_g"])], row_tile=big_tile)

    lru_out = [((s_len, d_rnn), BF16, (s_len, ct), "col")]
    for a in (conv_w, conv_b, wa_bd, w["b_gate_a"], wx_bd, w["b_gate_x"], w["lru_param"]):
        lru_out.append((a.shape, F32, (a.shape[0], ct), "col"))
    dx_rnn, g_conv_w, g_conv_b, g_wa_bd, g_b_a, g_wx_bd, g_b_x, g_lru = _tiled(
        "lru_bwd", _f_lru_bwd, n_ct, lru_in + [colspec(h_rnn), colspec(dy_rnn)], lru_out)

    dh1 = _mm("in_gates_dx", d_gates, w_g, tb=True)
    dh1 = _mm("in_qkv_dx", dqkv, w_qkv, tb=True, add=dh1)
    dh1 = _mm("in_rnn_dx", dx_rnn, w_rnn, tb=True, add=dh1)
    g_w_rnn = _mm("in_rnn_dw", h1, dx_rnn, ta=True)
    g_w_qkv = _mm("in_qkv_dw", h1, dqkv, ta=True)
    g_w_g = _mm("in_gates_dw", h1, d_gates, ta=True)

    def column_shards(pieces, n=4):
        total = sum(p.shape[1] for p in pieces)
        width = total // n
        shards = []
        for s in range(n):
            parts, start = [], 0
            for p in pieces:
                lo, hi = max(s * width, start), min((s + 1) * width, start + p.shape[1])
                if lo < hi:
                    parts.append(p[:, lo - start:hi - start])
                start += p.shape[1]
            shards.append(jnp.concatenate(parts, axis=1))
        return jnp.stack(shards)

    r_in, c_in = local2d["w_in"].shape
    w_in_pieces = [g_w_rnn, g_w_qkv[:, :n_q + n_kv],
                   g_w_qkv[:, n_q + n_kv + QK_NOPE:n_q + n_kv + QK_NOPE + QK_ROPE], g_w_g]
    first_chunk = {
        "w_in": column_shards(w_in_pieces).reshape(4, 2, r_in // 2, c_in),
        "w_uq": chunked("w_uq", g_w_uq.reshape(n_q, N_HEADS, HEAD_PAD)[:, :, :hd].reshape(n_q, -1)),
        "w_ukv": chunked("w_ukv", g_w_ukv),
    }
    first_chunks = [first_chunk[k] for k in first_names]
    first_pair_flight = _split_start(
        "reduce_pair_first_start", first_chunks,
        [jax.ShapeDtypeStruct((4,) + ck.shape[2:], F32) for ck in first_chunks], "pair", after=first_chunks[0])

    ln_bwd = _vjp_of(_f_ln, 4, (0, 1, 2, 3))

    def ln1_bwd(xv, gv, sc, sh, dxr, dh):
        dx, dg, dsc, dsh = ln_bwd(xv, gv, sc, sh, dh)
        return dx + dxr, dg, dsc, dsh

    ln1_held = [ln1_in[0], full2(norm1_g + first_pair_flight[4][:1, :1])] + ln1_in[2:]
    grad_x, g_norm1, d_scale1, d_shift1 = _tiled(
        "ln1_bwd", ln1_bwd, nt, ln1_held + [rowspec(dx_res), rowspec(dh1)],
        [rowout(d, F32), accout(norm1_g), accout(scale1), accout(shift1)], row_tile=big_tile)

    dmod = jnp.concatenate([d_shift1, d_scale1, d_gate1, d_shift2, d_scale2, d_gate2], axis=1)
    dmod_all = _all_gather("gather_dmod", dmod, ALL7).reshape(8, -1)
    dmod_loc = lax.dynamic_slice_in_dim(dmod_all, chip * n_mod, n_mod, axis=1)
    g_w_ada = _mm("ada_dw", c_act, jnp.pad(dmod_loc, ((0, c_rows - 8), (0, 0))), ta=True)

    g_convs = {"conv_w": g_conv_w, "ffn_conv_w": g_ffn_cw}
    g_small = {
        "b_ada": dmod, "norm1_g": g_norm1, "conv_b": g_conv_b,
        "w_gate_a": _block_diag_pairs_t(g_wa_bd)[None], "b_gate_a": g_b_a,
        "w_gate_x": _block_diag_pairs_t(g_wx_bd)[None], "b_gate_x": g_b_x, "lru_param": g_lru,
        "q_norm_g": g_q_norm, "kv_norm_g": g_kv_norm, "norm2_g": g_norm2,
        "ffn_conv_b": g_ffn_cb, "final_g": d_final_g.reshape(w["final_g"].shape),
    }

    small_flat = jnp.concatenate([g_small[k].reshape(-1) for k in SMALL] + [g_convs[k].reshape(-1) for k, _ in CONVS])
    small_rows = -(-small_flat.shape[0] // (8 * PACK_COLS * PACK_ROW_UNIT)) * PACK_ROW_UNIT
    small_chunk = _pad_rows(small_flat[None], 8 * small_rows).reshape(4, 2, small_rows, PACK_COLS)
    last_names = first_names + ["small"]
    send_sems, recv_sems, flown, landed, _ = first_pair_flight
    first_received = _split_wait("reduce_pair_first_wait", send_sems, recv_sems, flown, landed, "pair", after=small_chunk)
    sums_last = [_reduce_pair("reduce_pair_" + k, ck, got, core)
                 for k, ck, got in zip(first_names, first_chunks, first_received)]
    sums_last += pair_sums("small", ["small"], [small_chunk])
    send_sems, recv_sems, flown, landed, _ = ready_flight
    quads_ready = _split_wait("reduce_ready_wait", send_sems, recv_sems, flown, landed, "alltoall", after=grad_x)
    last_flight = _split_start(
        "reduce_last_start", sums_last, [jax.ShapeDtypeStruct(s.shape, s.dtype) for s in sums_last], "alltoall",
        after=quads_ready[0])
    grads = {"w_ada": g_w_ada[None]}
    delta, new_m, new_v = {}, {}, {}

    def adamw(k):
        shp = w[k].shape
        flip = len(shp) == 3 and shp[-1] % LANE != 0 and shp[-2] % LANE == 0
        view = (lambda a: jnp.swapaxes(a, 1, 2)) if flip else (lambda a: a)
        two_d = (-1, view(w[k]).shape[-1]) if len(shp) > 1 else (1, -1)
        dk, mk, vk = _adamw("adamw_" + k, *[view(a).reshape(two_d) for a in (w[k], grads[k], m_in[k], v_in[k])])
        back = lambda a: view(a.reshape(view(w[k]).shape))
        delta[k], new_m[k], new_v[k] = back(dk), back(mk), back(vk)

    def finish(tag, names, quads, sums, after):
        reduced = {}
        for k, quad, ps in zip(names, quads, sums):
            quad = lax.dynamic_update_index_in_dim(quad, lax.dynamic_index_in_dim(ps, chip, 0, keepdims=True), chip, 0)
            reduced[k] = _reduce_quad("reduce_quad_" + k, quad, after)
        big = [k for k in names if k != "small"]
        for k, both in zip(big, _share_sibling("share_sibling_" + tag, [reduced[k] for k in big])):
            grads[k] = lax.dynamic_update_index_in_dim(both, reduced[k][None], ci, 0).reshape(w[k].shape)
        return reduced

    finish("ready", later_names, quads_ready, sums_ready, after=last_flight[4])
    for k in later_names + ["w_ada"]:
        adamw(k)
    send_sems, recv_sems, flown, landed, _ = last_flight
    quads_last = _split_wait("reduce_last_wait", send_sems, recv_sems, flown, landed, "alltoall",
                             after=delta[later_names[-1]])
    reduced = finish("last", last_names, quads_last, sums_last, after=None)
    small_grad = _all_gather("share_small", reduced["small"], ALL7).reshape(-1)
    off = 0
    for k in SMALL:
        grads[k] = small_grad[off:off + w[k].size].reshape(w[k].shape)
        off += w[k].size
    for k, _ in CONVS:
        r, cc = local2d[k].shape
        whole = small_grad[off:off + 4 * r * cc].reshape(r, 4 * cc)
        grads[k] = lax.dynamic_slice_in_dim(whole, chip * cc, cc, axis=1)[None]
        off += 4 * r * cc
    for k in WEIGHTS:
        if k not in delta:
            adamw(k)

    return (loss, grad_x[None], *[grads[k] for k in WEIGHTS], *[delta[k] for k in WEIGHTS],
            *[new_m[k] for k in WEIGHTS], *[new_v[k] for k in WEIGHTS])


def kernel(x, c, positions, w_ada, b_ada, norm1_g, w_in, conv_w, conv_b, w_gate_a, b_gate_a, w_gate_x, b_gate_x, lru_param, q_norm_g, w_uq, kv_norm_g, w_ukv, w_proj_rnn, w_proj_mla, w_out, norm2_g, w_up, ffn_conv_w, ffn_conv_b, w_down, final_g, loss_target, m_w_ada, m_b_ada, m_norm1_g, m_w_in, m_conv_w, m_conv_b, m_w_gate_a, m_b_gate_a, m_w_gate_x, m_b_gate_x, m_lru_param, m_q_norm_g, m_w_uq, m_kv_norm_g, m_w_ukv, m_w_proj_rnn, m_w_proj_mla, m_w_out, m_norm2_g, m_w_up, m_ffn_conv_w, m_ffn_conv_b, m_w_down, m_final_g, v_w_ada, v_b_ada, v_norm1_g, v_w_in, v_conv_w, v_conv_b, v_w_gate_a, v_b_gate_a, v_w_gate_x, v_b_gate_x, v_lru_param, v_q_norm_g, v_w_uq, v_kv_norm_g, v_w_ukv, v_w_proj_rnn, v_w_proj_mla, v_w_out, v_norm2_g, v_w_up, v_ffn_conv_w, v_ffn_conv_b, v_w_down, v_final_g):
    given = dict(locals())
    w = {k: given[k] for k in WEIGHTS}
    m_in = {k: given["m_" + k] for k in WEIGHTS}
    v_in = {k: given["v_" + k] for k in WEIGHTS}
    return _step(x, c, positions, w, m_in, v_in, loss_target)
```

```python
import functools
import math

import jax
import jax.numpy as jnp
from jax import lax
from jax.experimental import pallas as pl
from jax.experimental.pallas import tpu as pltpu

F32 = jnp.float32
BF16 = jnp.bfloat16

EPS = 1e-6
LRU_C = 8.0
N_HEADS = 16
QK_NOPE = 64
QK_ROPE = 32
HEAD_PAD = 128
ROPE_THETA = 10000.0
ADAM_LR = 0.001
ADAM_B1 = 0.9
ADAM_B2 = 0.999
ADAM_EPS = 1e-08
ADAM_WD = 0.01
ADAM_STEP = 10

LANE = 128
SUBLANES = 8
VMEM_LIMIT = 48 * 1024 * 1024
MM_TILE_M = MM_TILE_N = MM_TILE_K = 1408
PACK_COLS = 1024
PACK_ROW_UNIT = 32
MESH = pl.DeviceIdType.MESH

NN = (((1,), (0,)), ((), ()))
NT = (((1,), (1,)), ((), ()))
TN = (((0,), (0,)), ((), ()))


def _cparams(sem):
    return pltpu.CompilerParams(dimension_semantics=sem, vmem_limit_bytes=VMEM_LIMIT)


def _div_tile(n, cap, unit):
    best = None
    d = unit
    while d <= min(n, cap):
        if n % d == 0:
            best = d
        d += unit
    return n if best is None else best


def _mm(name, a, b, *, ta=False, tb=False, add=None, out_dtype=F32):
    if ta:
        kdim, m = a.shape
    else:
        m, kdim = a.shape
    if tb:
        n, kb = b.shape
    else:
        kb, n = b.shape
    assert kdim == kb, (name, a.shape, b.shape)
    tm = _div_tile(m, MM_TILE_M, 8 if not ta else LANE)
    tn = _div_tile(n, MM_TILE_N, LANE)
    tk = _div_tile(kdim, MM_TILE_K, LANE)
    nk = kdim // tk
    a_spec = pl.BlockSpec((tk, tm), lambda i, j, k: (k, i)) if ta else pl.BlockSpec((tm, tk), lambda i, j, k: (i, k))
    b_spec = pl.BlockSpec((tn, tk), lambda i, j, k: (j, k)) if tb else pl.BlockSpec((tk, tn), lambda i, j, k: (k, j))
    o_spec = pl.BlockSpec((tm, tn), lambda i, j, k: (i, j))
    has_add = add is not None
    dims = ((((0,) if ta else (1,)), ((1,) if tb else (0,))), ((), ()))

    def body(*refs):
        a_ref, b_ref = refs[0], refs[1]
        c_ref = refs[2] if has_add else None
        o_ref = refs[3] if has_add else refs[2]
        prod = lax.dot_general(a_ref[...].astype(BF16), b_ref[...].astype(BF16), dims, preferred_element_type=F32)
        if nk == 1:
            o_ref[...] = (prod + c_ref[...].astype(F32) if has_add else prod).astype(o_ref.dtype)
            return
        acc = refs[-1]
        k = pl.program_id(2)

        @pl.when(k == 0)
        def _():
            acc[...] = prod + c_ref[...].astype(F32) if has_add else prod

        @pl.when(jnp.logical_and(k > 0, k < nk - 1))
        def _():
            acc[...] += prod

        @pl.when(k == nk - 1)
        def _():
            o_ref[...] = (acc[...] + prod).astype(o_ref.dtype)

    ins = [a, b] + ([add] if has_add else [])
    specs = [a_spec, b_spec] + ([o_spec] if has_add else [])
    return pl.pallas_call(
        body, name=name, grid=(m // tm, n // tn, nk), in_specs=specs, out_specs=o_spec,
        out_shape=jax.ShapeDtypeStruct((m, n), out_dtype),
        scratch_shapes=[pltpu.VMEM((tm, tn), F32)] if nk > 1 else [],
        compiler_params=_cparams(("parallel", "parallel", "arbitrary")),
    )(*ins)


_IMAPS = {
    "row": lambda i: (i, 0),
    "col": lambda i: (0, i),
    "full": lambda i: (0, 0),
    "acc": lambda i: (0, 0),
}


def _tiled(name, fn, n, ins, outs, row_tile=None):
    if row_tile is not None:
        rows = next(a.shape[0] for a, _, k in ins if k == "row")
        n = rows // row_tile
        ins = [(a, (row_tile, bs[1]) if k == "row" else bs, k) for a, bs, k in ins]
        outs = [(s, dt, (row_tile, bs[1]) if k == "row" else bs, k) for s, dt, bs, k in outs]
    ni = len(ins)
    is_acc = [k == "acc" for *_, k in outs]

    def body(*refs):
        vals = fn(*[r[...].astype(F32) if r.dtype == BF16 else r[...] for r in refs[:ni]])
        orefs = refs[ni:]
        if any(is_acc):
            @pl.when(pl.program_id(0) == 0)
            def _():
                for r, a in zip(orefs, is_acc):
                    if a:
                        r[...] = jnp.zeros(r.shape, r.dtype)
        for r, v, a in zip(orefs, vals, is_acc):
            if a:
                r[...] += v.astype(r.dtype)
            else:
                r[...] = v.astype(r.dtype)

    res = pl.pallas_call(
        body, name=name, grid=(n,),
        in_specs=[pl.BlockSpec(bs, _IMAPS[k]) for _, bs, k in ins],
        out_specs=[pl.BlockSpec(bs, _IMAPS[k]) for _, _, bs, k in outs],
        out_shape=[jax.ShapeDtypeStruct(s, d) for s, d, _, _ in outs],
        compiler_params=_cparams(("arbitrary",)),
    )(*[a for a, _, _ in ins])
    return tuple(res)


def _vjp_of(fn, nin, diff):
    def g(*args):
        ins, cots = args[:nin], args[nin:]

        def f(*d):
            full = list(ins)
            for i, v in zip(diff, d):
                full[i] = v
            return fn(*full)

        outs, vjp = jax.vjp(f, *[ins[i] for i in diff])
        return vjp(tuple(c.astype(o.dtype) for c, o in zip(cots, outs)))
    return g


def _shift_rows(x, k, fill, up=False):
    n = x.shape[0]
    rows = lax.broadcasted_iota(jnp.int32, x.shape, 0)
    if up:
        return jnp.where(rows < n - k, pltpu.roll(x, n - k, 0), fill)
    return jnp.where(rows >= k, pltpu.roll(x, k, 0), fill)


@functools.partial(jax.custom_vjp, nondiff_argnums=(1,))
def _delay(x, k):
    return _shift_rows(x, k, 0.0)


def _delay_fwd(x, k):
    return _shift_rows(x, k, 0.0), None


def _delay_bwd(k, _, g):
    return (_shift_rows(g, k, 0.0, up=True),)


_delay.defvjp(_delay_fwd, _delay_bwd)


@functools.partial(jax.custom_vjp, nondiff_argnums=(1,))
def _lane_roll(x, s):
    return pltpu.roll(x, s, 1)


def _lane_roll_fwd(x, s):
    return pltpu.roll(x, s, 1), None


def _lane_roll_bwd(s, _, g):
    return (pltpu.roll(g, g.shape[1] - s, 1),)


_lane_roll.defvjp(_lane_roll_fwd, _lane_roll_bwd)


@jax.custom_vjp
def _bdot(x, w):
    return lax.dot_general(x.astype(BF16), w.astype(BF16), NN, preferred_element_type=F32)


def _bdot_fwd(x, w):
    return _bdot(x, w), (x, w)


def _bdot_bwd(res, g):
    x, w = res
    gb = g.astype(BF16)
    dx = lax.dot_general(gb, w.astype(BF16), NT, preferred_element_type=F32)
    dw = lax.dot_general(x.T.astype(BF16), gb, NN, preferred_element_type=F32)
    return dx, dw


_bdot.defvjp(_bdot_fwd, _bdot_bwd)


def _sigmoid(x):
    return 0.5 * (jnp.tanh(0.5 * x) + 1.0)


def _silu(x):
    return x * _sigmoid(x)


def _rms(x, g):
    return x * lax.rsqrt(jnp.mean(x * x, axis=-1, keepdims=True) + EPS) * g


def _causal_conv(x, w, b):
    kw = w.shape[0]
    tap = lax.broadcasted_iota(jnp.int32, w.shape, 0)
    y = b
    for k in range(kw):
        d = kw - 1 - k
        wk = jnp.sum(jnp.where(tap == k, w, 0.0), axis=0, keepdims=True)
        y = y + wk * (x if d == 0 else _delay(x, d))
    return y


def _rotate(x, cos_f, sin_a, sin_b):
    reps = x.shape[1] // LANE
    if reps > 1:
        cos_f, sin_a, sin_b = (jnp.tile(t, (1, reps)) for t in (cos_f, sin_a, sin_b))
    n = x.shape[1]
    half = QK_ROPE // 2
    return x * cos_f + _lane_roll(x, n - half) * sin_a + _lane_roll(x, half) * sin_b


def _softplus_neg(l):
    u = jnp.exp(-jnp.abs(l))
    log1p_u = jnp.where(u < 0.01, u * (1.0 - u * (0.5 - u * (1.0 / 3.0))), jnp.log(1.0 + u))
    return jnp.maximum(-l, 0.0) + log1p_u


def _f_ln(x, g, scale, shift):
    return (_rms(x, g) * (1.0 + scale) + shift,)


def _f_qkv(qkv, cos_f, sin_a, sin_b, qg, kvg):
    nq, nkv = qg.shape[1], kvg.shape[1]
    qn = _rms(qkv[:, :nq], qg)
    kvn = _rms(qkv[:, nq:nq + nkv], kvg)
    kr = _rotate(qkv[:, nq + nkv:], cos_f, sin_a, sin_b)
    return qn, kvn, kr


def _f_qkv_bwd(qkv, cos_f, sin_a, sin_b, qg, kvg, dqn, dkvn, dkr):
    nq, nkv = qg.shape[1], kvg.shape[1]
    _, vjp_q = jax.vjp(_rms, qkv[:, :nq], qg)
    _, vjp_kv = jax.vjp(_rms, qkv[:, nq:nq + nkv], kvg)
    _, vjp_r = jax.vjp(lambda t: _rotate(t, cos_f, sin_a, sin_b), qkv[:, nq + nkv:])
    dq_lat, dqg = vjp_q(dqn)
    dkv_lat, dkvg = vjp_kv(dkvn)
    (dkr_pre,) = vjp_r(dkr)
    return jnp.concatenate([dq_lat, dkv_lat, dkr_pre], axis=1), dqg, dkvg


QK_SCALE = 1.0 / math.sqrt(QK_NOPE + QK_ROPE)
LOG2_E = 1.4426950408889634
LN_2 = 0.6931471805599453


def _rotate_bf16(x, cos_f, sin_s):
    row = lax.broadcasted_iota(jnp.int32, (LANE, LANE), 0)
    col = lax.broadcasted_iota(jnp.int32, (LANE, LANE), 1)
    half = QK_ROPE // 2
    first, second = QK_NOPE, QK_NOPE + half
    swap = (((row >= first) & (row < second) & (col == row + half))
            | ((row >= second) & (row < second + half) & (col == row - half))).astype(BF16)
    partner = jnp.concatenate(
        [lax.dot_general(x[:, b * LANE:(b + 1) * LANE].astype(BF16), swap, NN, preferred_element_type=F32)
         for b in range(x.shape[1] // LANE)], axis=1)
    reps = x.shape[1] // LANE
    return x * jnp.tile(cos_f, (1, reps)) + partner * jnp.tile(sin_s, (1, reps))


def _f_rotq(q, cos_f, sin_a, sin_b):
    return (_rotate_bf16(q, cos_f, sin_a + sin_b) * (QK_SCALE * LOG2_E),)


def _f_rotq_bwd(cos_f, sin_a, sin_b, dq):
    return (_rotate_bf16(dq, cos_f, -(sin_a + sin_b)) * QK_SCALE,)


def _merge(g_rnn, g_mla, p_rnn, p_mla):
    return _sigmoid(g_rnn) * p_rnn + _sigmoid(g_mla) * p_mla


def _f_merge(g, p_rnn, p_mla):
    d = p_rnn.shape[1]
    return (_merge(g[:, :d], g[:, d:], p_rnn, p_mla),)


def _f_merge_bwd(g, p_rnn, p_mla, dm):
    d = p_rnn.shape[1]
    _, vjp = jax.vjp(_merge, g[:, :d], g[:, d:], p_rnn, p_mla)
    dg_rnn, dg_mla, dp_rnn, dp_mla = vjp(dm)
    return jnp.concatenate([dg_rnn, dg_mla], axis=1), dp_rnn, dp_mla


def _f_res_ln(x, o, gate, g2, scale, shift):
    x1 = x + gate * o
    return x1, _rms(x1, g2) * (1.0 + scale) + shift


def _f_ffn(u_gate, u_val, cw_gate, cw_val, cb_gate, cb_val):
    return (_silu(_causal_conv(u_gate, cw_gate, cb_gate)) * _causal_conv(u_val, cw_val, cb_val),)


def _f_loss(x1, f, tgt, gate, fg):
    y = _rms(x1 + gate * f, fg)
    err = (y - tgt) * (y - tgt)
    return 0.5 * jnp.sum(jnp.mean(err, axis=-1, keepdims=True), axis=0, keepdims=True)


def _f_loss_and_grads(x1, f, tgt, gate, fg):
    loss, vjp = jax.vjp(lambda a, b, c, d: _f_loss(a, b, tgt, c, d), x1, f, gate, fg)
    dx1, df, dgate, dfg = vjp(jnp.ones((1, 1), F32))
    return dx1, df, jnp.broadcast_to(loss, (1, LANE)), dgate, dfg


@jax.custom_vjp
def _decay_and_gain(log_a):
    a = jnp.exp(log_a)
    return a, jnp.sqrt(-jnp.tanh(log_a) * (1.0 + a * a))


def _decay_and_gain_fwd(log_a):
    a, gain = _decay_and_gain(log_a)
    return (a, gain), (a, gain)


def _decay_and_gain_bwd(res, g):
    a, gain = res
    return (g[0] * a - g[1] * (a * a) / gain,)


_decay_and_gain.defvjp(_decay_and_gain_fwd, _decay_and_gain_bwd)


def _f_lru_coeffs(xr, cw, cb, wa, ba, wx, bx, lru, reset):
    xc = _causal_conv(xr, cw, cb)
    r = _sigmoid(_bdot(xc, wa) + ba)
    i = _sigmoid(_bdot(xc, wx) + bx)
    log_a = (-LRU_C) * r * _softplus_neg(lru)
    a, mult = _decay_and_gain(log_a)
    is_reset = reset > 0.5
    a = jnp.where(is_reset, 0.0, a)
    mult = jnp.where(is_reset, 1.0, mult)
    return a, mult * (i * xc)


SCAN_BLOCK = 64


def _scan(a, b, up=False):
    n = a.shape[0]
    blk = min(SCAN_BLOCK, n)
    pos = lax.broadcasted_iota(jnp.int32, a.shape, 0) % blk
    k = 1
    while k < blk:
        inside = (pos < blk - k) if up else (pos >= k)
        shift = n - k if up else k
        b = b + a * jnp.where(inside, pltpu.roll(b, shift, 0), 0.0)
        a = a * jnp.where(inside, pltpu.roll(a, shift, 0), 1.0)
        k *= 2
    blocks = range(n // blk)
    carry = jnp.zeros((1,) + a.shape[1:], a.dtype)
    out = [None] * len(blocks)
    for i in (reversed(blocks) if up else blocks):
        rows = slice(i * blk, (i + 1) * blk)
        out[i] = b[rows] + a[rows] * carry
        carry = out[i][:1] if up else out[i][blk - 1:]
    return jnp.concatenate(out, axis=0)


def _f_lru_fwd(xr, cw, cb, wa, ba, wx, bx, lru, reset):
    a, b = _f_lru_coeffs(xr, cw, cb, wa, ba, wx, bx, lru, reset)
    h = _scan(a, b)
    return h, h


def _f_lru_bwd(xr, cw, cb, wa, ba, wx, bx, lru, reset, h, dh):
    (a, _), vjp = jax.vjp(lambda *p: _f_lru_coeffs(*p, reset), xr, cw, cb, wa, ba, wx, bx, lru)
    g = _scan(_shift_rows(a, 1, 0.0, up=True), dh, up=True)
    return vjp((g * _shift_rows(h, 1, 0.0), g))


def _attn_tile(s):
    return 1024 if s >= 2048 else s // 2


def _keys(kv, kr):
    lane = lax.broadcasted_iota(jnp.int32, kv.shape, 1)
    return jnp.where(lane < QK_NOPE, kv, kr)


ATTN_HEADS_PER_STEP = 2


def _scores(q, kc, diagonal):
    s = lax.dot_general(q, kc, NT, preferred_element_type=F32)
    if not diagonal:
        return s
    rows = lax.broadcasted_iota(jnp.int32, s.shape, 0)
    cols = lax.broadcasted_iota(jnp.int32, s.shape, 1)
    return jnp.where(cols - (s.shape[1] - s.shape[0]) <= rows, s, -jnp.inf)


def _sub_blocks(t, diagonal):
    return ((0, t // 2, t // 2), (t // 2, t // 2, t)) if diagonal else ((0, t, t),)


def _causal_pairs(nb, k_major):
    if k_major:
        pairs = [(qb, kb) for kb in range(nb) for qb in range(kb, nb)]
    else:
        pairs = [(qb, kb) for qb in range(nb) for kb in range(qb + 1)]
    return jnp.array([p[0] for p in pairs], jnp.int32), jnp.array([p[1] for p in pairs], jnp.int32)


def _attn_fwd(q_pre, tables, kv, kr):
    s_len = q_pre.shape[0]
    t = _attn_tile(s_len)
    nb = s_len // t
    hp = ATTN_HEADS_PER_STEP
    wide = hp * HEAD_PAD
    q_tab, k_tab = _causal_pairs(nb, k_major=False)

    def body(qt, kt, qp_ref, cos_ref, sina_ref, sinb_ref, kv_ref, kr_ref, o_ref, lse_ref, q_ref, m_s, acc_s):
        pair = pl.program_id(1)
        qi, ki = qt[pair], kt[pair]

        @pl.when(ki == 0)
        def _():
            m_s[...] = jnp.full(m_s.shape, -jnp.inf, F32)
            acc_s[...] = jnp.zeros(acc_s.shape, F32)
            (rotated,) = _f_rotq(qp_ref[...].astype(F32), cos_ref[...], sina_ref[...], sinb_ref[...])
            q_ref[...] = rotated.astype(q_ref.dtype)

        def step(diagonal):
            for h in range(hp):
                lanes = slice(h * HEAD_PAD, (h + 1) * HEAD_PAD)
                for r0, nr, nk in _sub_blocks(t, diagonal):
                    rows = slice(r0, r0 + nr)
                    kvv = kv_ref[:nk, lanes]
                    s = _scores(q_ref[rows, lanes], _keys(kvv, kr_ref[:nk, :]), diagonal)
                    m_old = m_s[h, rows]
                    m_new = jnp.maximum(m_old, jnp.max(s, axis=-1, keepdims=True))
                    alpha = jnp.exp2(m_old - m_new)
                    p = jnp.exp2(s - jnp.tile(m_new, (1, s.shape[1] // HEAD_PAD)))
                    lane = lax.broadcasted_iota(jnp.int32, kvv.shape, 1)
                    ones_and_values = jnp.where(lane < QK_NOPE, jnp.ones_like(kvv), kvv)
                    acc_s[rows, lanes] = alpha * acc_s[rows, lanes] + lax.dot_general(
                        p.astype(BF16), ones_and_values, NN, preferred_element_type=F32)
                    m_s[h, rows] = m_new

        @pl.when(ki < qi)
        def _():
            step(False)

        @pl.when(ki == qi)
        def _():
            step(True)
            lane = lax.broadcasted_iota(jnp.int32, (t, HEAD_PAD), 1)
            outs = []
            for h in range(hp):
                acc = acc_s[:, h * HEAD_PAD:(h + 1) * HEAD_PAD]
                total = acc[:, :1]
                outs.append(acc / total)
                lse_ref[h] = m_s[h][:, :1] + jnp.log(total) * LOG2_E
            o_ref[...] = jnp.where(lane >= QK_NOPE, outs[0], pltpu.roll(outs[1], QK_NOPE, 1)).astype(o_ref.dtype)

    q_rows = lambda h, p, qt, kt: (qt[p], 0)
    grid_spec = pltpu.PrefetchScalarGridSpec(
        num_scalar_prefetch=2, grid=(N_HEADS // hp, q_tab.shape[0]),
        in_specs=[pl.BlockSpec((t, wide), lambda h, p, qt, kt: (qt[p], h)),
                  pl.BlockSpec((t, HEAD_PAD), q_rows), pl.BlockSpec((t, HEAD_PAD), q_rows),
                  pl.BlockSpec((t, HEAD_PAD), q_rows),
                  pl.BlockSpec((t, wide), lambda h, p, qt, kt: (kt[p], h)),
                  pl.BlockSpec((t, HEAD_PAD), lambda h, p, qt, kt: (kt[p], 0))],
        out_specs=[pl.BlockSpec((t, HEAD_PAD), lambda h, p, qt, kt: (qt[p], h)),
                   pl.BlockSpec((hp, t, 1), lambda h, p, qt, kt: (h, qt[p], 0)),
                   pl.BlockSpec((t, wide), lambda h, p, qt, kt: (qt[p], h))],
        scratch_shapes=[pltpu.VMEM((hp, t, HEAD_PAD), F32), pltpu.VMEM((t, wide), F32)])
    return pl.pallas_call(
        body, name="attn_fwd", grid_spec=grid_spec,
        out_shape=[jax.ShapeDtypeStruct((s_len, N_HEADS // hp * HEAD_PAD), BF16),
                   jax.ShapeDtypeStruct((N_HEADS, s_len, 1), F32),
                   jax.ShapeDtypeStruct((s_len, N_HEADS * HEAD_PAD), BF16)],
        compiler_params=_cparams(("arbitrary", "arbitrary")),
    )(q_tab, k_tab, q_pre, *tables, kv, kr)


def _attn_bwd(q, kv, kr, o, lse, do):
    s_len = q.shape[0]
    t = _attn_tile(s_len)
    nb = s_len // t
    hp = ATTN_HEADS_PER_STEP
    wide = hp * HEAD_PAD
    q_tab, k_tab = _causal_pairs(nb, k_major=True)

    def body(qt, kt, q_ref, kv_ref, kr_ref, o_ref, lse_ref, do_ref, dq_ref, dkv_ref, dkr_ref, dk_s, dv_s, dq_s):
        g, pair = pl.program_id(0), pl.program_id(1)
        qb, kb = qt[pair], kt[pair]

        @pl.when(jnp.logical_and(g == 0, pair == 0))
        def _():
            dkr_ref[...] = jnp.zeros(dkr_ref.shape, F32)

        @pl.when(pair == 0)
        def _():
            dq_s[...] = jnp.zeros(dq_s.shape, F32)

        @pl.when(qb == kb)
        def _():
            dk_s[...] = jnp.zeros(dk_s.shape, F32)
            dv_s[...] = jnp.zeros(dv_s.shape, F32)

        def step(diagonal):
            for h in range(hp):
                lanes = slice(h * HEAD_PAD, (h + 1) * HEAD_PAD)
                for r0, nr, nk in _sub_blocks(t, diagonal):
                    rows, keys = slice(r0, r0 + nr), slice(0, nk)
                    qv, kvv = q_ref[rows, lanes], kv_ref[keys, lanes]
                    pair_do = do_ref[rows, :].astype(F32)
                    lane = lax.broadcasted_iota(jnp.int32, pair_do.shape, 1)
                    mine = (lane >= QK_NOPE) if h == 0 else (lane < QK_NOPE)
                    placed = pair_do if h == 0 else pltpu.roll(pair_do, QK_NOPE, 1)
                    dov = jnp.where(lane >= QK_NOPE, placed, 0.0).astype(BF16)
                    delta = jnp.sum(jnp.where(mine, pair_do * o_ref[rows, :].astype(F32), 0.0), axis=-1, keepdims=True)
                    kc = _keys(kvv, kr_ref[keys, :])
                    p = jnp.exp2(_scores(qv, kc, diagonal) - lse_ref[h, rows])
                    dp = lax.dot_general(dov, kvv, NT, preferred_element_type=F32)
                    ds = p * (dp - delta)
                    dv_s[keys, lanes] += lax.dot_general(p.astype(BF16), dov, TN, preferred_element_type=F32)
                    dk_s[keys, lanes] += lax.dot_general(ds.astype(BF16), qv, TN, preferred_element_type=F32)
                    q_rows = pl.ds(pl.multiple_of(qb * t + r0, nr), nr)
                    dq_s[q_rows, lanes] += lax.dot_general(ds.astype(BF16), kc, NN, preferred_element_type=F32)

        @pl.when(qb > kb)
        def _():
            step(False)

        @pl.when(qb == kb)
        def _():
            step(True)

        @pl.when(qb == nb - 1)
        def _():
            lane = lax.broadcasted_iota(jnp.int32, (t, HEAD_PAD), 1)
            rows = pl.ds(pl.multiple_of(kb * t, t), t)
            for h in range(hp):
                lanes = slice(h * HEAD_PAD, (h + 1) * HEAD_PAD)
                dk = dk_s[:, lanes] * LN_2
                dkv_ref[:, lanes] = jnp.where(lane < QK_NOPE, dk, dv_s[:, lanes]).astype(dkv_ref.dtype)
                dkr_ref[rows, :] += jnp.where(lane >= QK_NOPE, dk, 0.0)

        @pl.when(pair == q_tab.shape[0] - 1)
        def _():
            dq_ref[...] = dq_s[...].astype(dq_ref.dtype)

    all_lanes = N_HEADS * HEAD_PAD
    qmap = lambda h, p, qt, kt: (qt[p], h)
    kmap = lambda h, p, qt, kt: (kt[p], h)
    grid_spec = pltpu.PrefetchScalarGridSpec(
        num_scalar_prefetch=2, grid=(N_HEADS // hp, q_tab.shape[0]),
        in_specs=[pl.BlockSpec((t, wide), qmap),
                  pl.BlockSpec((t, wide), kmap),
                  pl.BlockSpec((t, HEAD_PAD), lambda h, p, qt, kt: (kt[p], 0)),
                  pl.BlockSpec((t, HEAD_PAD), qmap),
                  pl.BlockSpec((hp, t, 1), lambda h, p, qt, kt: (h, qt[p], 0)),
                  pl.BlockSpec((t, HEAD_PAD), qmap)],
        out_specs=[pl.BlockSpec((s_len, wide), lambda h, p, qt, kt: (0, h)),
                   pl.BlockSpec((t, wide), kmap),
                   pl.BlockSpec((s_len, HEAD_PAD), lambda h, p, qt, kt: (0, 0))],
        scratch_shapes=[pltpu.VMEM((t, wide), F32), pltpu.VMEM((t, wide), F32), pltpu.VMEM((s_len, wide), F32)])
    return pl.pallas_call(
        body, name="attn_bwd", grid_spec=grid_spec,
        out_shape=[jax.ShapeDtypeStruct((s_len, all_lanes), BF16),
                   jax.ShapeDtypeStruct((s_len, all_lanes), BF16),
                   jax.ShapeDtypeStruct((s_len, HEAD_PAD), F32)],
        compiler_params=_cparams(("arbitrary", "arbitrary")),
    )(q_tab, k_tab, q, kv, kr, o, lse, do)


def _adamw(name, w, g, m, v):
    rows, cols = w.shape
    tr = _div_tile(rows, max(8, (2 * 1024 * 1024) // (4 * cols)), 8)

    def body(w_ref, g_ref, m_ref, v_ref, d_ref, nm_ref, nv_ref):
        gv = g_ref[...]
        nm = ADAM_B1 * m_ref[...] + (1.0 - ADAM_B1) * gv
        nv = ADAM_B2 * v_ref[...] + (1.0 - ADAM_B2) * jnp.square(gv)
        m_hat = nm / (1.0 - ADAM_B1 ** ADAM_STEP)
        v_hat = nv / (1.0 - ADAM_B2 ** ADAM_STEP)
        d_ref[...] = -ADAM_LR * (m_hat / (jnp.sqrt(v_hat) + ADAM_EPS) + ADAM_WD * w_ref[...])
        nm_ref[...] = nm
        nv_ref[...] = nv

    spec = pl.BlockSpec((tr, cols), lambda i: (i, 0))
    return pl.pallas_call(
        body, name=name, grid=(rows // tr,), in_specs=[spec] * 4, out_specs=[spec] * 3,
        out_shape=[jax.ShapeDtypeStruct((rows, cols), F32)] * 3,
        compiler_params=_cparams(("parallel",)),
    )(w, g, m, v)


ALL7 = (1, 2, 3, 4, 5, 6, 7)
CHIPS = (2, 4, 6)


def _all_gather(name, src, masks):
    bits = 0
    for m in masks:
        bits |= m
    nslots = {7: 8, 6: 4}[bits]
    nm = len(masks)

    def slot_of(x, y, c):
        return {7: 4 * x + 2 * y + c, 6: 2 * x + y}[bits]

    def body(src_ref, out_ref, send_sems, recv_sems, local_sem):
        x, y, c = lax.axis_index("x"), lax.axis_index("y"), lax.axis_index("c")
        mine = slot_of(x, y, c)
        own = pltpu.make_async_copy(src_ref, out_ref.at[mine], local_sem)
        own.start()
        copies = []
        for i, m in enumerate(masks):
            peer = _peer(x, y, c, m)
            copies.append((
                pltpu.make_async_remote_copy(
                    src_ref=src_ref, dst_ref=out_ref.at[mine], send_sem=send_sems.at[i], recv_sem=recv_sems.at[i],
                    device_id=peer, device_id_type=MESH),
                pltpu.make_async_remote_copy(
                    src_ref=src_ref, dst_ref=out_ref.at[slot_of(*peer)], send_sem=send_sems.at[i],
                    recv_sem=recv_sems.at[i], device_id=peer, device_id_type=MESH)))
        for send, _ in copies:
            send.start()
        for _, arrival in copies:
            arrival.wait_recv()
        for send, _ in copies:
            send.wait_send()
        own.wait()

    return pl.pallas_call(
        body, name=name,
        in_specs=[pl.BlockSpec(memory_space=pl.ANY)], out_specs=pl.BlockSpec(memory_space=pl.ANY),
        out_shape=jax.ShapeDtypeStruct((nslots,) + tuple(src.shape), src.dtype),
        scratch_shapes=[pltpu.SemaphoreType.DMA((nm,)), pltpu.SemaphoreType.DMA((nm,)), pltpu.SemaphoreType.DMA],
    )(src)


def _peer(x, y, c, m):
    return (1 - x if m & 4 else x, 1 - y if m & 2 else y, 1 - c if m & 1 else c)


def _comm_call(name, emit, srcs, out_shapes, n_sems, in_place=False):
    n = len(srcs)

    def body(*refs):
        src_refs, out_refs = refs[:n], refs[n:n + len(out_shapes)]
        send_sems, recv_sems = refs[-2], refs[-1]

        def copy(src, dst, i, peer):
            return pltpu.make_async_remote_copy(src_ref=src, dst_ref=dst, send_sem=send_sems.at[i],
                                                recv_sem=recv_sems.at[i], device_id=peer, device_id_type=MESH)

        emit(lax.axis_index("x"), lax.axis_index("y"), lax.axis_index("c"), src_refs, out_refs, copy)

    hbm = pl.BlockSpec(memory_space=pl.ANY)
    return pl.pallas_call(
        body, name=name, in_specs=[hbm] * n, out_specs=[hbm] * len(out_shapes), out_shape=out_shapes,
        scratch_shapes=[pltpu.SemaphoreType.DMA((n_sems,)), pltpu.SemaphoreType.DMA((n_sems,))],
        input_output_aliases={i: i for i in range(n)} if in_place else {},
    )(*srcs)


HBM_SPEC = pl.BlockSpec(memory_space=pltpu.HBM)
SEM_SPEC = pl.BlockSpec(memory_space=pltpu.SEMAPHORE)
DATAFLOW = pltpu.SideEffectType.DATAFLOW_SIDE_EFFECTING


def _chip_copies(srcs, lands, send_sems, recv_sems, mode):
    x, y, c = lax.axis_index("x"), lax.axis_index("y"), lax.axis_index("c")
    chip = 2 * x + y
    sends, arrivals = [], []
    if mode == "pair":
        for k, (s, l) in enumerate(zip(srcs, lands)):
            for group in (sends, arrivals):
                group.append(pltpu.make_async_remote_copy(
                    src_ref=s.at[:, 1 - c], dst_ref=l, send_sem=send_sems.at[3 * k], recv_sem=recv_sems.at[3 * k],
                    device_id=(x, y, 1 - c), device_id_type=MESH))
        return sends, arrivals
    for j, m in enumerate(CHIPS):
        px, py, _ = _peer(x, y, c, m)
        theirs = 2 * px + py
        for k, (s, l) in enumerate(zip(srcs, lands)):
            if mode == "gather":
                src, dst, got = s.at[c], l.at[chip, c], l.at[theirs, c]
            else:
                src, dst, got = s.at[theirs], l.at[chip], l.at[theirs]
            for to, group in ((dst, sends), (got, arrivals)):
                group.append(pltpu.make_async_remote_copy(
                    src_ref=src, dst_ref=to, send_sem=send_sems.at[3 * k + j], recv_sem=recv_sems.at[3 * k + j],
                    device_id=(px, py, c), device_id_type=MESH))
    return sends, arrivals


def _split_start(name, srcs, land_shapes, mode, after):
    n = len(srcs)

    def body(*refs):
        sends, _ = _chip_copies(refs[:n], refs[n:2 * n], refs[2 * n + 1], refs[2 * n + 2], mode)
        for cp in sends:
            cp.start()
        token = refs[-1]
        token[...] = jnp.zeros(token.shape, token.dtype)

    hbm = lambda a: pltpu.with_memory_space_constraint(a, pltpu.HBM)
    lands = [hbm(lax.empty(s.shape, s.dtype)) for s in land_shapes]
    bufs = [pltpu.HBM(a.shape, a.dtype) for a in list(srcs) + lands]
    res = pl.pallas_call(
        body, name=name,
        out_shape=(pltpu.SemaphoreType.DMA((3 * n,)), pltpu.SemaphoreType.DMA((3 * n,)), *bufs,
                   jax.ShapeDtypeStruct((SUBLANES, LANE), F32)),
        in_specs=[HBM_SPEC] * (2 * n) + [pl.BlockSpec(memory_space=pl.ANY)],
        out_specs=[SEM_SPEC, SEM_SPEC] + [HBM_SPEC] * (2 * n) + [pl.BlockSpec(memory_space=pltpu.VMEM)],
        input_output_aliases={i: 2 + i for i in range(2 * n)},
        compiler_params=pltpu.CompilerParams(has_side_effects=DATAFLOW),
    )(*[hbm(s) for s in srcs], *lands, after)
    return res[0], res[1], res[2:2 + n], res[2 + n:2 + 2 * n], res[-1]


def _split_wait(name, send_sems, recv_sems, srcs, lands, mode, after):
    n = len(srcs)

    def body(*refs):
        sends, arrivals = _chip_copies(refs[:n], refs[n:2 * n], refs[2 * n], refs[2 * n + 1], mode)
        for cp in sends:
            cp.wait_send()
        for cp in arrivals:
            cp.wait_recv()

    res = pl.pallas_call(
        body, name=name,
        out_shape=tuple(pltpu.HBM(a.shape, a.dtype) for a in list(srcs) + list(lands)),
        in_specs=[HBM_SPEC] * (2 * n) + [SEM_SPEC, SEM_SPEC, pl.BlockSpec(memory_space=pl.ANY)],
        out_specs=[HBM_SPEC] * (2 * n),
        input_output_aliases={i: i for i in range(2 * n)},
        compiler_params=pltpu.CompilerParams(has_side_effects=DATAFLOW),
    )(*srcs, *lands, send_sems, recv_sems, after)
    return res[n:]


def _relay_sibling(lands):
    def emit(x, y, c, srcs, outs, copy):
        sib = (x, y, 1 - c)
        sends, arrivals = [], []
        for j, m in enumerate(CHIPS):
            px, py, _ = _peer(x, y, c, m)
            theirs = 2 * px + py
            for k, (s, o) in enumerate(zip(srcs, outs)):
                sends.append(copy(s.at[theirs, c], o.at[theirs, c], 3 * k + j, sib))
                arrivals.append(copy(s.at[theirs, c], o.at[theirs, 1 - c], 3 * k + j, sib))
        for cp in sends:
            cp.start()
        for cp in arrivals:
            cp.wait_recv()
        for cp in sends:
            cp.wait_send()

    shapes = [jax.ShapeDtypeStruct(l.shape, l.dtype) for l in lands]
    return _comm_call("relay_weights", emit, lands, shapes, 3 * len(lands), in_place=True)


def _gather_weights(halves):
    n = len(halves)

    def emit(x, y, c, srcs, outs, copy):
        chip = 2 * x + y
        sib = (x, y, 1 - c)
        first, relay, landed, relayed = [], [], [], []
        for j, m in enumerate(CHIPS):
            px, py, _ = _peer(x, y, c, m)
            theirs = 2 * px + py
            for k in range(n):
                i = 6 * k + j
                first.append(copy(srcs[k].at[c], outs[k].at[chip, c], i, (px, py, c)))
                landed.append(copy(srcs[k].at[c], outs[k].at[theirs, c], i, (px, py, c)))
                relay.append(copy(outs[k].at[theirs, c], outs[k].at[theirs, c], i + 3, sib))
                relayed.append(copy(outs[k].at[theirs, 1 - c], outs[k].at[theirs, 1 - c], i + 3, sib))
        for cp in first:
            cp.start()
        for arrival, onward in zip(landed, relay):
            arrival.wait_recv()
            onward.start()
        for arrival in relayed:
            arrival.wait_recv()
        for cp in first + relay:
            cp.wait_send()

    shapes = [jax.ShapeDtypeStruct((4,) + h.shape, h.dtype) for h in halves]
    return _comm_call("gather_weights", emit, halves, shapes, 6 * n)


def _pair_exchange(name, chunks):
    def emit(x, y, c, srcs, outs, copy):
        sib = (x, y, 1 - c)
        sends = [copy(s.at[:, 1 - c], o, k, sib) for k, (s, o) in enumerate(zip(srcs, outs))]
        for cp in sends:
            cp.start()
        for cp in sends:
            cp.wait_recv()
        for cp in sends:
            cp.wait_send()

    shapes = [jax.ShapeDtypeStruct((4,) + g.shape[2:], g.dtype) for g in chunks]
    return _comm_call(name, emit, chunks, shapes, len(chunks))


def _share_sibling(name, parts):
    def emit(x, y, c, srcs, outs, copy):
        sib = (x, y, 1 - c)
        sends = [copy(s, o.at[c], k, sib) for k, (s, o) in enumerate(zip(srcs, outs))]
        arrivals = [copy(s, o.at[1 - c], k, sib) for k, (s, o) in enumerate(zip(srcs, outs))]
        for cp in sends:
            cp.start()
        for cp in arrivals:
            cp.wait_recv()
        for cp in sends:
            cp.wait_send()

    shapes = [jax.ShapeDtypeStruct((2,) + p.shape, p.dtype) for p in parts]
    return _comm_call(name, emit, parts, shapes, len(parts))


def _reduce_pair(name, chunk, from_sibling, core):
    n, _, h, cols = chunk.shape
    rt = _div_tile(h, max(16, (1 << 20) // (4 * cols)), 16)

    def body(core_ref, a_ref, b_ref, o_ref):
        o_ref[...] = (a_ref[...] + b_ref[...]).astype(o_ref.dtype)

    grid_spec = pltpu.PrefetchScalarGridSpec(
        num_scalar_prefetch=1, grid=(n, h // rt),
        in_specs=[pl.BlockSpec((None, None, rt, cols), lambda s, i, core_ref: (s, core_ref[0], i, 0)),
                  pl.BlockSpec((None, rt, cols), lambda s, i, core_ref: (s, i, 0))],
        out_specs=pl.BlockSpec((None, rt, cols), lambda s, i, core_ref: (s, i, 0)))
    return pl.pallas_call(
        body, name=name, grid_spec=grid_spec, out_shape=jax.ShapeDtypeStruct((n, h, cols), BF16),
        compiler_params=_cparams(("parallel", "parallel")),
    )(core, chunk, from_sibling)


def _reduce_quad(name, q, after=None):
    _, h, cols = q.shape
    rt = _div_tile(h, max(16, (1 << 20) // (4 * cols)), 16)

    def body(q_ref, *rest):
        v = q_ref[...].astype(F32)
        rest[-1][...] = ((v[0] + v[1]) + v[2]) + v[3]

    held = [] if after is None else [after]
    return pl.pallas_call(
        body, name=name, grid=(h // rt,),
        in_specs=[pl.BlockSpec((4, rt, cols), lambda i: (0, i, 0))] + [pl.BlockSpec(memory_space=pl.ANY)] * len(held),
        out_specs=pl.BlockSpec((rt, cols), lambda i: (i, 0)),
        out_shape=jax.ShapeDtypeStruct((h, cols), F32),
        compiler_params=_cparams(("parallel",)),
    )(q, *held)


def _unshard(seg, kind):
    n, r, c = seg.shape
    if kind == "col":
        return seg.transpose(1, 0, 2).reshape(r, n * c)
    return seg.reshape(n * r, c)


def _pad_rows(flat, rows):
    n, ln = flat.shape
    return jnp.pad(flat, ((0, 0), (0, rows * PACK_COLS - ln))).reshape(n, rows, PACK_COLS)


def _block_diag_pairs(w):
    n2, bs, _ = w.shape
    eye = jnp.eye(2, dtype=w.dtype)
    z = w.reshape(n2 // 2, 2, bs, 1, bs) * eye[None, :, None, :, None]
    return z.reshape(n2 // 2, 2 * bs, 2 * bs).transpose(1, 0, 2).reshape(2 * bs, n2 * bs)


def _block_diag_pairs_t(d, bs=64):
    n = d.shape[1] // (2 * bs)
    z = d.reshape(2 * bs, n, 2 * bs).transpose(1, 0, 2).reshape(n, 2, bs, 2, bs)
    return jnp.stack([z[:, 0, :, 0, :], z[:, 1, :, 1, :]], axis=1).reshape(2 * n, bs, bs)


BIG = (("w_in", "col"), ("w_uq", "col"), ("w_ukv", "col"), ("w_proj_rnn", "row"), ("w_proj_mla", "row"),
       ("w_out", "row"), ("w_up", "col"), ("w_down", "row"))
FIRST_USED = ("w_in", "w_uq", "w_ukv")
CONVS = (("conv_w", "col"), ("ffn_conv_w", "col"))
SMALL = ("b_ada", "norm1_g", "conv_b", "w_gate_a", "b_gate_a", "w_gate_x", "b_gate_x", "lru_param",
         "q_norm_g", "kv_norm_g", "norm2_g", "ffn_conv_b", "final_g")
WEIGHTS = ("w_ada", "b_ada", "norm1_g", "w_in", "conv_w", "conv_b", "w_gate_a", "b_gate_a", "w_gate_x",
           "b_gate_x", "lru_param", "q_norm_g", "w_uq", "kv_norm_g", "w_ukv", "w_proj_rnn", "w_proj_mla",
           "w_out", "norm2_g", "w_up", "ffn_conv_w", "ffn_conv_b", "w_down", "final_g")


def _step(x, c, positions, w, m_in, v_in, loss_target):
    s_len, d = x.shape[1], x.shape[2]
    x2d = x[0]
    tgt = loss_target[0]
    xi, yi, ci = lax.axis_index("x"), lax.axis_index("y"), lax.axis_index("c")
    chip = 2 * xi + yi
    me = 2 * chip + ci
    tile = min(256, s_len)
    nt = s_len // tile

    local2d = {k: w[k][0] for k, _ in BIG + CONVS}
    kinds = dict(BIG)
    halves_bf = {k: local2d[k].astype(BF16).reshape(2, local2d[k].shape[0] // 2, local2d[k].shape[1]) for k, _ in BIG}
    first_names = [k for k, _ in BIG if k in FIRST_USED]
    later_names = [k for k, _ in BIG if k not in FIRST_USED]
    full = {}

    def assemble(k, g):
        g = lax.dynamic_update_index_in_dim(g, halves_bf[k][None], chip, 0).reshape((4,) + local2d[k].shape)
        if k == "w_up":
            full["w_up_gate"], full["w_up_val"] = _unshard(g[:2], kinds[k]), _unshard(g[2:], kinds[k])
        else:
            full[k] = _unshard(g, kinds[k])

    first_got = _gather_weights([halves_bf[k] for k in first_names])
    for k, g in zip(first_names, first_got):
        assemble(k, g)
    conv_flat = jnp.concatenate([local2d[k].reshape(-1) for k, _ in CONVS])
    conv_rows = -(-conv_flat.shape[0] // PACK_COLS)
    conv_all = _all_gather("gather_conv_w", _pad_rows(conv_flat[None], conv_rows)[0], CHIPS)
    conv_all = conv_all.reshape(4, -1)
    off = 0
    for k, kind in CONVS:
        r, cc = local2d[k].shape
        full[k] = _unshard(conv_all[:, off:off + r * cc].reshape(4, r, cc), kind)
        off += r * cc

    d_rnn = w["conv_b"].shape[1]
    n_q, n_kv = w["q_norm_g"].shape[1], w["kv_norm_g"].shape[1]
    w_in = full["w_in"]
    o1, o2, o3 = d_rnn + n_q, d_rnn + n_q + n_kv, d_rnn + n_q + n_kv + QK_ROPE
    w_rnn = w_in[:, :d_rnn]
    zpad = lambda n: jnp.zeros((d, n), BF16)
    w_qkv = jnp.concatenate([w_in[:, d_rnn:o2], zpad(QK_NOPE), w_in[:, o2:o3], zpad(LANE - QK_NOPE - QK_ROPE)], axis=1)
    w_g = w_in[:, o3:]
    hd = QK_NOPE + QK_ROPE
    w_uq = jnp.pad(full["w_uq"].reshape(n_q, N_HEADS, hd), ((0, 0), (0, 0), (0, HEAD_PAD - hd))).reshape(n_q, -1)
    w_ukv = full["w_ukv"]
    v_head = w_ukv.shape[1] // N_HEADS - QK_NOPE
    d_ff = w["ffn_conv_b"].shape[1] // 2
    ffn_cw_gate, ffn_cw_val = full["ffn_conv_w"][:, :d_ff], full["ffn_conv_w"][:, d_ff:]
    ffn_cb_gate, ffn_cb_val = w["ffn_conv_b"][:, :d_ff], w["ffn_conv_b"][:, d_ff:]
    conv_w, conv_b = full["conv_w"], w["conv_b"]
    wa_bd = _block_diag_pairs(w["w_gate_a"][0])
    wx_bd = _block_diag_pairs(w["w_gate_x"][0])

    c_all = _all_gather("gather_c", c, ALL7).reshape(8, d)
    c_rows = 128
    (c_act,) = _tiled("silu_c", lambda v: (_silu(v),), 1, [(jnp.pad(c_all, ((0, c_rows - 8), (0, 0))), (c_rows, d), "full")],
                      [((c_rows, d), F32, (c_rows, d), "full")])
    w_ada = w["w_ada"][0]
    n_mod = w_ada.shape[1]
    b_loc = lax.dynamic_slice_in_dim(w["b_ada"], chip * n_mod, n_mod, axis=1)
    mod_loc = _mm("ada_fwd", c_act, w_ada, add=jnp.broadcast_to(b_loc, (c_rows, n_mod)))
    mod_all = _all_gather("gather_mod", mod_loc[:8], CHIPS)
    mod = lax.dynamic_index_in_dim(mod_all, me, 1, keepdims=False).reshape(1, -1)
    shift1, scale1, gate1, shift2, scale2, gate2 = [mod[:, i * d:(i + 1) * d] for i in range(6)]

    small_done = (mod[:, :1] + conv_all[:1, :1] + first_got[0][0, 0, :1, :1].astype(F32))
    later_flight = _split_start(
        "gather_later_start", [halves_bf[k] for k in later_names],
        [jax.ShapeDtypeStruct((4,) + halves_bf[k].shape, BF16) for k in later_names], "gather", after=small_done)

    half = QK_ROPE // 2
    inv_freq = ROPE_THETA ** (-jnp.arange(half, dtype=F32) / half)
    ang = positions[0].astype(F32)[:, None] * inv_freq
    cos, sin = jnp.cos(ang), jnp.sin(ang)
    one, zero = jnp.ones((s_len, QK_NOPE), F32), jnp.zeros((s_len, half), F32)
    tail = jnp.zeros((s_len, LANE - QK_NOPE - QK_ROPE), F32)
    cos_f = jnp.concatenate([one, cos, cos, tail + 1.0], axis=1)
    sin_a = jnp.concatenate([one * 0.0, -sin, zero, tail], axis=1)
    sin_b = jnp.concatenate([one * 0.0, zero, sin, tail], axis=1)
    reset = (positions[0] == 0).astype(F32)[:, None]
    tabs = [(cos_f, (tile, LANE), "row"), (sin_a, (tile, LANE), "row"), (sin_b, (tile, LANE), "row")]

    def rowspec(a):
        return (a, (tile, a.shape[1]), "row")

    def full2(a):
        return (a, a.shape, "full")

    def rowout(cols, dt):
        return ((s_len, cols), dt, (tile, cols), "row")

    def accout(a):
        return (a.shape, F32, a.shape, "acc")

    norm1_g = w["norm1_g"] + later_flight[4][:1, :1]
    norm2_g, final_g = w["norm2_g"], w["final_g"].reshape(1, d)
    ln1_in = [rowspec(x2d), full2(norm1_g), full2(scale1), full2(shift1)]
    big_tile = min(512, s_len)
    (h1,) = _tiled("ln1", _f_ln, nt, ln1_in, [rowout(d, BF16)], row_tile=big_tile)
    x_rnn = _mm("in_rnn", h1, w_rnn, out_dtype=BF16)
    qkv = _mm("in_qkv", h1, w_qkv)
    gates = _mm("in_gates", h1, w_g, out_dtype=BF16)

    ct = LANE
    n_ct = d_rnn // ct
    colspec = lambda a, width=ct: (a, (a.shape[0], width), "col")
    lru_in = [colspec(x_rnn), colspec(conv_w), colspec(conv_b), colspec(wa_bd), colspec(w["b_gate_a"]),
              colspec(wx_bd), colspec(w["b_gate_x"]), colspec(w["lru_param"]), full2(reset)]
    y_rnn, h_rnn = _tiled("lru_fwd", _f_lru_fwd, n_ct, lru_in,
                          [((s_len, d_rnn), BF16, (s_len, ct), "col"), ((s_len, d_rnn), F32, (s_len, ct), "col")])

    qkv_in = [rowspec(qkv)] + tabs + [full2(w["q_norm_g"]), full2(w["kv_norm_g"])]
    qn, kvn, kr = _tiled("qkv_norm", _f_qkv, nt, qkv_in, [rowout(n_q, BF16), rowout(n_kv, BF16), rowout(LANE, BF16)],
                         row_tile=big_tile)
    q_pre = _mm("up_q", qn, w_uq, out_dtype=BF16)
    kv = _mm("up_kv", kvn, w_ukv, out_dtype=BF16)
    o_mla, lse, q_cat = _attn_fwd(q_pre, (cos_f, sin_a, sin_b), kv, kr)

    send_sems, recv_sems, flown, landed, _ = later_flight
    landed = _split_wait("gather_later_wait", send_sems, recv_sems, flown, landed, "gather", after=o_mla)
    for k, g in zip(later_names, _relay_sibling(landed)):
        assemble(k, g)
    w_pr = full["w_proj_rnn"]
    assert ATTN_HEADS_PER_STEP == 2 and 2 * v_head == HEAD_PAD
    swap_pairs = lambda a: a.reshape(N_HEADS // 2, 2, v_head, d)[:, ::-1].reshape(-1, d)
    w_pm = swap_pairs(full["w_proj_mla"])
    w_out = full["w_out"]
    w_up_gate, w_up_val = full["w_up_gate"], full["w_up_val"]
    w_down = full["w_down"]

    p_rnn = _mm("proj_rnn", y_rnn, w_pr, out_dtype=BF16)
    p_mla = _mm("proj_mla", o_mla, w_pm, out_dtype=BF16)
    merge_in = [rowspec(gates), rowspec(p_rnn), rowspec(p_mla)]
    (merged,) = _tiled("merge", _f_merge, nt, merge_in, [rowout(d, BF16)])
    o_tok = _mm("out_proj", merged, w_out)
    res_in = [rowspec(x2d), rowspec(o_tok), full2(gate1), full2(norm2_g), full2(scale2), full2(shift2)]
    x1, h2 = _tiled("res_ln2", _f_res_ln, nt, res_in, [rowout(d, F32), rowout(d, BF16)], row_tile=big_tile)
    u_gate = _mm("ffn_up_gate", h2, w_up_gate, out_dtype=BF16)
    u_val = _mm("ffn_up_val", h2, w_up_val, out_dtype=BF16)
    n_ft = d_ff // LANE
    ffn_in = [colspec(a) for a in (u_gate, u_val, ffn_cw_gate, ffn_cw_val, ffn_cb_gate, ffn_cb_val)]
    (act,) = _tiled("ffn_conv", _f_ffn, n_ft, ffn_in, [((s_len, d_ff), BF16, (s_len, LANE), "col")])
    f_tok = _mm("ffn_down", act, w_down)

    loss_in = [rowspec(x1), rowspec(f_tok), rowspec(tgt), full2(gate2), full2(final_g)]
    dx1, df, loss_row, d_gate2, d_final_g = _tiled(
        "loss", _f_loss_and_grads, nt, loss_in,
        [rowout(d, F32), rowout(d, BF16), ((1, LANE), F32, (1, LANE), "acc"), accout(gate2), accout(final_g)],
        row_tile=big_tile)
    loss = lax.psum(loss_row[0, 0], ("x", "y", "c"))

    d_act = _mm("ffn_down_dx", df, w_down, tb=True, out_dtype=BF16)
    g_w_down = _mm("ffn_down_dw", act, df, ta=True)
    taps = ffn_cw_gate.shape[0]
    du_gate, du_val, g_cw_gate, g_cw_val, g_cb_gate, g_cb_val = _tiled(
        "ffn_conv_bwd", _vjp_of(_f_ffn, 6, (0, 1, 2, 3, 4, 5)), n_ft, ffn_in + [colspec(d_act)],
        [((s_len, d_ff), BF16, (s_len, LANE), "col")] * 2 + [((taps, d_ff), F32, (taps, LANE), "col")] * 2
        + [((1, d_ff), F32, (1, LANE), "col")] * 2)
    dh2 = _mm("ffn_up_gate_dx", du_gate, w_up_gate, tb=True)
    dh2 = _mm("ffn_up_val_dx", du_val, w_up_val, tb=True, add=dh2, out_dtype=BF16)
    g_w_up_halves = [_mm("ffn_up_gate_dw", h2, du_gate, ta=True), _mm("ffn_up_val_dw", h2, du_val, ta=True)]
    g_ffn_cw = jnp.concatenate([g_cw_gate, g_cw_val], axis=1)
    g_ffn_cb = jnp.concatenate([g_cb_gate, g_cb_val], axis=1)

    def chunked(k, gk):
        r, cc = local2d[k].shape
        if kinds[k] == "col":
            gk = gk.reshape(r, 4, cc).transpose(1, 0, 2)
        return gk.reshape(4, 2, r // 2, cc)

    r_up, c_up = local2d["w_up"].shape
    up_chunks = jnp.concatenate([g.reshape(r_up, 2, c_up).transpose(1, 0, 2) for g in g_w_up_halves], axis=0)
    ffn_chunks = {"w_up": up_chunks.reshape(4, 2, r_up // 2, c_up), "w_down": chunked("w_down", g_w_down)}
    ffn_names = [k for k in later_names if k in ffn_chunks]
    ffn_pair_flight = _split_start(
        "reduce_pair_ffn_start", [ffn_chunks[k] for k in ffn_names],
        [jax.ShapeDtypeStruct((4,) + ffn_chunks[k].shape[2:], F32) for k in ffn_names], "pair",
        after=ffn_chunks[ffn_names[-1]])
    gate1_held = gate1 + ffn_pair_flight[4][:1, :1]

    res_bwd = _vjp_of(_f_res_ln, 6, (0, 1, 2, 3, 4, 5))
    dx_res, do_tok, d_gate1, g_norm2, d_scale2, d_shift2 = _tiled(
        "res_ln2_bwd", res_bwd, nt, res_in[:2] + [full2(gate1_held)] + res_in[3:] + [rowspec(dx1), rowspec(dh2)],
        [rowout(d, F32), rowout(d, BF16), accout(gate1), accout(norm2_g), accout(scale2), accout(shift2)],
        row_tile=big_tile)
    d_merged = _mm("out_proj_dx", do_tok, w_out, tb=True, out_dtype=BF16)
    g_w_out = _mm("out_proj_dw", merged, do_tok, ta=True)
    d_gates, dp_rnn, dp_mla = _tiled(
        "merge_bwd", _f_merge_bwd, nt, merge_in + [rowspec(d_merged)],
        [rowout(gates.shape[1], BF16), rowout(d, BF16), rowout(d, BF16)])
    dy_rnn = _mm("proj_rnn_dx", dp_rnn, w_pr, tb=True, out_dtype=BF16)
    g_w_pr = _mm("proj_rnn_dw", y_rnn, dp_rnn, ta=True)
    do_mla = _mm("proj_mla_dx", dp_mla, w_pm, tb=True, out_dtype=BF16)
    g_w_pm = _mm("proj_mla_dw", o_mla, dp_mla, ta=True)

    core = ci.astype(jnp.int32).reshape(1)

    def pair_sums(tag, names, chunks):
        received = _pair_exchange("reduce_pair_exchange_" + tag, chunks)
        return [_reduce_pair("reduce_pair_" + k, ck, got, core) for k, ck, got in zip(names, chunks, received)]

    g_later = {"w_proj_rnn": g_w_pr, "w_proj_mla": swap_pairs(g_w_pm), "w_out": g_w_out}
    send_sems, recv_sems, flown, landed, _ = ffn_pair_flight
    ffn_received = _split_wait("reduce_pair_ffn_wait", send_sems, recv_sems, flown, landed, "pair", after=g_w_pm)
    sums = {k: _reduce_pair("reduce_pair_" + k, ffn_chunks[k], got, core) for k, got in zip(ffn_names, ffn_received)}
    other_names = [k for k in later_names if k not in ffn_chunks]
    sums.update(zip(other_names, pair_sums("ready", other_names, [chunked(k, g_later[k]) for k in other_names])))
    sums_ready = [sums[k] for k in later_names]
    ready_flight = _split_start(
        "reduce_ready_start", sums_ready, [jax.ShapeDtypeStruct(s.shape, s.dtype) for s in sums_ready], "alltoall",
        after=sums_ready[0])
    kr_held = kr + ready_flight[4][:1, :].astype(BF16)

    dq_cat, dkv, dkr = _attn_bwd(q_cat, kv, kr_held, o_mla, lse, do_mla)
    (dq_pre,) = _tiled("rot_q_bwd", _f_rotq_bwd, nt, tabs + [rowspec(dq_cat)],
                       [rowout(q_pre.shape[1], BF16)])
    dqn = _mm("up_q_dx", dq_pre, w_uq, tb=True, out_dtype=BF16)
    g_w_uq = _mm("up_q_dw", qn, dq_pre, ta=True)
    dkvn = _mm("up_kv_dx", dkv, w_ukv, tb=True, out_dtype=BF16)
    g_w_ukv = _mm("up_kv_dw", kvn, dkv, ta=True)
    dqkv, g_q_norm, g_kv_norm = _tiled(
        "qkv_norm_bwd", _f_qkv_bwd, nt, qkv_in + [rowspec(dqn), rowspec(dkvn), rowspec(dkr)],
        [rowout(qkv.shape[1], BF16), accout(w["q_norm_g"]), accout(w["kv_norm_g"])], row_tile=big_tile)

    lru_out = [((s_len, d_rnn), BF16, (s_len, ct), "col")]
    for a in (conv_w, conv_b, wa_bd, w["b_gate_a"], wx_bd, w["b_gate_x"], w["lru_param"]):
        lru_out.append((a.shape, F32, (a.shape[0], ct), "col"))
    dx_rnn, g_conv_w, g_conv_b, g_wa_bd, g_b_a, g_wx_bd, g_b_x, g_lru = _tiled(
        "lru_bwd", _f_lru_bwd, n_ct, lru_in + [colspec(h_rnn), colspec(dy_rnn)], lru_out)

    dh1 = _mm("in_gates_dx", d_gates, w_g, tb=True)
    dh1 = _mm("in_qkv_dx", dqkv, w_qkv, tb=True, add=dh1)
    dh1 = _mm("in_rnn_dx", dx_rnn, w_rnn, tb=True, add=dh1)
    g_w_rnn = _mm("in_rnn_dw", h1, dx_rnn, ta=True)
    g_w_qkv = _mm("in_qkv_dw", h1, dqkv, ta=True)
    g_w_g = _mm("in_gates_dw", h1, d_gates, ta=True)

    g_first = {
        "w_in": jnp.concatenate([g_w_rnn, g_w_qkv[:, :n_q + n_kv],
                                 g_w_qkv[:, n_q + n_kv + QK_NOPE:n_q + n_kv + QK_NOPE + QK_ROPE], g_w_g], axis=1),
        "w_uq": g_w_uq.reshape(n_q, N_HEADS, HEAD_PAD)[:, :, :hd].reshape(n_q, -1),
        "w_ukv": g_w_ukv,
    }
    first_chunks = [chunked(k, g_first[k]) for k in first_names]
    first_pair_flight = _split_start(
        "reduce_pair_first_start", first_chunks,
        [jax.ShapeDtypeStruct((4,) + ck.shape[2:], F32) for ck in first_chunks], "pair", after=first_chunks[0])

    ln_bwd = _vjp_of(_f_ln, 4, (0, 1, 2, 3))

    def ln1_bwd(xv, gv, sc, sh, dxr, dh):
        dx, dg, dsc, dsh = ln_bwd(xv, gv, sc, sh, dh)
        return dx + dxr, dg, dsc, dsh

    ln1_held = [ln1_in[0], full2(norm1_g + first_pair_flight[4][:1, :1])] + ln1_in[2:]
    grad_x, g_norm1, d_scale1, d_shift1 = _tiled(
        "ln1_bwd", ln1_bwd, nt, ln1_held + [rowspec(dx_res), rowspec(dh1)],
        [rowout(d, F32), accout(norm1_g), accout(scale1), accout(shift1)], row_tile=big_tile)

    dmod = jnp.concatenate([d_shift1, d_scale1, d_gate1, d_shift2, d_scale2, d_gate2], axis=1)
    dmod_all = _all_gather("gather_dmod", dmod, ALL7).reshape(8, -1)
    dmod_loc = lax.dynamic_slice_in_dim(dmod_all, chip * n_mod, n_mod, axis=1)
    g_w_ada = _mm("ada_dw", c_act, jnp.pad(dmod_loc, ((0, c_rows - 8), (0, 0))), ta=True)

    g_convs = {"conv_w": g_conv_w, "ffn_conv_w": g_ffn_cw}
    g_small = {
        "b_ada": dmod, "norm1_g": g_norm1, "conv_b": g_conv_b,
        "w_gate_a": _block_diag_pairs_t(g_wa_bd)[None], "b_gate_a": g_b_a,
        "w_gate_x": _block_diag_pairs_t(g_wx_bd)[None], "b_gate_x": g_b_x, "lru_param": g_lru,
        "q_norm_g": g_q_norm, "kv_norm_g": g_kv_norm, "norm2_g": g_norm2,
        "ffn_conv_b": g_ffn_cb, "final_g": d_final_g.reshape(w["final_g"].shape),
    }

    small_flat = jnp.concatenate([g_small[k].reshape(-1) for k in SMALL] + [g_convs[k].reshape(-1) for k, _ in CONVS])
    small_rows = -(-small_flat.shape[0] // (8 * PACK_COLS * PACK_ROW_UNIT)) * PACK_ROW_UNIT
    small_chunk = _pad_rows(small_flat[None], 8 * small_rows).reshape(4, 2, small_rows, PACK_COLS)
    last_names = first_names + ["small"]
    send_sems, recv_sems, flown, landed, _ = first_pair_flight
    first_received = _split_wait("reduce_pair_first_wait", send_sems, recv_sems, flown, landed, "pair", after=small_chunk)
    sums_last = [_reduce_pair("reduce_pair_" + k, ck, got, core)
                 for k, ck, got in zip(first_names, first_chunks, first_received)]
    sums_last += pair_sums("small", ["small"], [small_chunk])
    send_sems, recv_sems, flown, landed, _ = ready_flight
    quads_ready = _split_wait("reduce_ready_wait", send_sems, recv_sems, flown, landed, "alltoall", after=grad_x)
    last_flight = _split_start(
        "reduce_last_start", sums_last, [jax.ShapeDtypeStruct(s.shape, s.dtype) for s in sums_last], "alltoall",
        after=quads_ready[0])
    grads = {"w_ada": g_w_ada[None]}
    delta, new_m, new_v = {}, {}, {}

    def adamw(k):
        shp = w[k].shape
        flip = len(shp) == 3 and shp[-1] % LANE != 0 and shp[-2] % LANE == 0
        view = (lambda a: jnp.swapaxes(a, 1, 2)) if flip else (lambda a: a)
        two_d = (-1, view(w[k]).shape[-1]) if len(shp) > 1 else (1, -1)
        dk, mk, vk = _adamw("adamw_" + k, *[view(a).reshape(two_d) for a in (w[k], grads[k], m_in[k], v_in[k])])
        back = lambda a: view(a.reshape(view(w[k]).shape))
        delta[k], new_m[k], new_v[k] = back(dk), back(mk), back(vk)

    def finish(tag, names, quads, sums, after):
        reduced = {}
        for k, quad, ps in zip(names, quads, sums):
            quad = lax.dynamic_update_index_in_dim(quad, lax.dynamic_index_in_dim(ps, chip, 0, keepdims=True), chip, 0)
            reduced[k] = _reduce_quad("reduce_quad_" + k, quad, after)
        big = [k for k in names if k != "small"]
        for k, both in zip(big, _share_sibling("share_sibling_" + tag, [reduced[k] for k in big])):
            grads[k] = lax.dynamic_update_index_in_dim(both, reduced[k][None], ci, 0).reshape(w[k].shape)
        return reduced

    finish("ready", later_names, quads_ready, sums_ready, after=last_flight[4])
    for k in later_names + ["w_ada"]:
        adamw(k)
    send_sems, recv_sems, flown, landed, _ = last_flight
    quads_last = _split_wait("reduce_last_wait", send_sems, recv_sems, flown, landed, "alltoall",
                             after=delta[later_names[-1]])
    reduced = finish("last", last_names, quads_last, sums_last, after=None)
    small_grad = _all_gather("share_small", reduced["small"], ALL7).reshape(-1)
    off = 0
    for k in SMALL:
        grads[k] = small_grad[off:off + w[k].size].reshape(w[k].shape)
        off += w[k].size
    for k, _ in CONVS:
        r, cc = local2d[k].shape
        whole = small_grad[off:off + 4 * r * cc].reshape(r, 4 * cc)
        grads[k] = lax.dynamic_slice_in_dim(whole, chip * cc, cc, axis=1)[None]
        off += 4 * r * cc
    for k in WEIGHTS:
        if k not in delta:
            adamw(k)

    return (loss, grad_x[None], *[grads[k] for k in WEIGHTS], *[delta[k] for k in WEIGHTS],
            *[new_m[k] for k in WEIGHTS], *[new_v[k] for k in WEIGHTS])


def kernel(x, c, positions, w_ada, b_ada, norm1_g, w_in, conv_w, conv_b, w_gate_a, b_gate_a, w_gate_x, b_gate_x, lru_param, q_norm_g, w_uq, kv_norm_g, w_ukv, w_proj_rnn, w_proj_mla, w_out, norm2_g, w_up, ffn_conv_w, ffn_conv_b, w_down, final_g, loss_target, m_w_ada, m_b_ada, m_norm1_g, m_w_in, m_conv_w, m_conv_b, m_w_gate_a, m_b_gate_a, m_w_gate_x, m_b_gate_x, m_lru_param, m_q_norm_g, m_w_uq, m_kv_norm_g, m_w_ukv, m_w_proj_rnn, m_w_proj_mla, m_w_out, m_norm2_g, m_w_up, m_ffn_conv_w, m_ffn_conv_b, m_w_down, m_final_g, v_w_ada, v_b_ada, v_norm1_g, v_w_in, v_conv_w, v_conv_b, v_w_gate_a, v_b_gate_a, v_w_gate_x, v_b_gate_x, v_lru_param, v_q_norm_g, v_w_uq, v_kv_norm_g, v_w_ukv, v_w_proj_rnn, v_w_proj_mla, v_w_out, v_norm2_g, v_w_up, v_ffn_conv_w, v_ffn_conv_b, v_w_down, v_final_g):
    given = dict(locals())
    w = {k: given[k] for k in WEIGHTS}
    m_in = {k: given["m_" + k] for k in WEIGHTS}
    v_in = {k: given["v_" + k] for k in WEIGHTS}
    return _step(x, c, positions, w, m_in, v_in, loss_target)
```

```python
import functools
import math

import jax
import jax.numpy as jnp
from jax import lax
from jax.experimental import pallas as pl
from jax.experimental.pallas import tpu as pltpu

F32 = jnp.float32
BF16 = jnp.bfloat16

EPS = 1e-6
LRU_C = 8.0
N_HEADS = 16
QK_NOPE = 64
QK_ROPE = 32
HEAD_PAD = 128
ROPE_THETA = 10000.0
ADAM_LR = 0.001
ADAM_B1 = 0.9
ADAM_B2 = 0.999
ADAM_EPS = 1e-08
ADAM_WD = 0.01
ADAM_STEP = 10

LANE = 128
SUBLANES = 8
VMEM_LIMIT = 48 * 1024 * 1024
MM_TILE_M = MM_TILE_N = MM_TILE_K = 1408
PACK_COLS = 1024
PACK_ROW_UNIT = 32
MESH = pl.DeviceIdType.MESH

NN = (((1,), (0,)), ((), ()))
NT = (((1,), (1,)), ((), ()))
TN = (((0,), (0,)), ((), ()))


def _cparams(sem):
    return pltpu.CompilerParams(dimension_semantics=sem, vmem_limit_bytes=VMEM_LIMIT)


def _div_tile(n, cap, unit):
    best = None
    d = unit
    while d <= min(n, cap):
        if n % d == 0:
            best = d
        d += unit
    return n if best is None else best


def _mm(name, a, b, *, ta=False, tb=False, add=None, out_dtype=F32):
    if ta:
        kdim, m = a.shape
    else:
        m, kdim = a.shape
    if tb:
        n, kb = b.shape
    else:
        kb, n = b.shape
    assert kdim == kb, (name, a.shape, b.shape)
    tm = _div_tile(m, MM_TILE_M, 8 if not ta else LANE)
    tn = _div_tile(n, MM_TILE_N, LANE)
    tk = _div_tile(kdim, MM_TILE_K, LANE)
    nk = kdim // tk
    a_spec = pl.BlockSpec((tk, tm), lambda i, j, k: (k, i)) if ta else pl.BlockSpec((tm, tk), lambda i, j, k: (i, k))
    b_spec = pl.BlockSpec((tn, tk), lambda i, j, k: (j, k)) if tb else pl.BlockSpec((tk, tn), lambda i, j, k: (k, j))
    o_spec = pl.BlockSpec((tm, tn), lambda i, j, k: (i, j))
    has_add = add is not None
    dims = ((((0,) if ta else (1,)), ((1,) if tb else (0,))), ((), ()))

    def body(*refs):
        a_ref, b_ref = refs[0], refs[1]
        c_ref = refs[2] if has_add else None
        o_ref = refs[3] if has_add else refs[2]
        prod = lax.dot_general(a_ref[...].astype(BF16), b_ref[...].astype(BF16), dims, preferred_element_type=F32)
        if nk == 1:
            o_ref[...] = (prod + c_ref[...].astype(F32) if has_add else prod).astype(o_ref.dtype)
            return
        acc = refs[-1]
        k = pl.program_id(2)

        @pl.when(k == 0)
        def _():
            acc[...] = prod + c_ref[...].astype(F32) if has_add else prod

        @pl.when(jnp.logical_and(k > 0, k < nk - 1))
        def _():
            acc[...] += prod

        @pl.when(k == nk - 1)
        def _():
            o_ref[...] = (acc[...] + prod).astype(o_ref.dtype)

    ins = [a, b] + ([add] if has_add else [])
    specs = [a_spec, b_spec] + ([o_spec] if has_add else [])
    return pl.pallas_call(
        body, name=name, grid=(m // tm, n // tn, nk), in_specs=specs, out_specs=o_spec,
        out_shape=jax.ShapeDtypeStruct((m, n), out_dtype),
        scratch_shapes=[pltpu.VMEM((tm, tn), F32)] if nk > 1 else [],
        compiler_params=_cparams(("parallel", "parallel", "arbitrary")),
    )(*ins)


_IMAPS = {
    "row": lambda i: (i, 0),
    "col": lambda i: (0, i),
    "full": lambda i: (0, 0),
    "acc": lambda i: (0, 0),
}


def _tiled(name, fn, n, ins, outs, row_tile=None):
    if row_tile is not None:
        rows = next(a.shape[0] for a, _, k in ins if k == "row")
        n = rows // row_tile
        ins = [(a, (row_tile, bs[1]) if k == "row" else bs, k) for a, bs, k in ins]
        outs = [(s, dt, (row_tile, bs[1]) if k == "row" else bs, k) for s, dt, bs, k in outs]
    ni = len(ins)
    is_acc = [k == "acc" for *_, k in outs]

    def body(*refs):
        vals = fn(*[r[...].astype(F32) if r.dtype == BF16 else r[...] for r in refs[:ni]])
        orefs = refs[ni:]
        if any(is_acc):
            @pl.when(pl.program_id(0) == 0)
            def _():
                for r, a in zip(orefs, is_acc):
                    if a:
                        r[...] = jnp.zeros(r.shape, r.dtype)
        for r, v, a in zip(orefs, vals, is_acc):
            if a:
                r[...] += v.astype(r.dtype)
            else:
                r[...] = v.astype(r.dtype)

    res = pl.pallas_call(
        body, name=name, grid=(n,),
        in_specs=[pl.BlockSpec(bs, _IMAPS[k]) for _, bs, k in ins],
        out_specs=[pl.BlockSpec(bs, _IMAPS[k]) for _, _, bs, k in outs],
        out_shape=[jax.ShapeDtypeStruct(s, d) for s, d, _, _ in outs],
        compiler_params=_cparams(("arbitrary",)),
    )(*[a for a, _, _ in ins])
    return tuple(res)


def _vjp_of(fn, nin, diff):
    def g(*args):
        ins, cots = args[:nin], args[nin:]

        def f(*d):
            full = list(ins)
            for i, v in zip(diff, d):
                full[i] = v
            return fn(*full)

        outs, vjp = jax.vjp(f, *[ins[i] for i in diff])
        return vjp(tuple(c.astype(o.dtype) for c, o in zip(cots, outs)))
    return g


def _shift_rows(x, k, fill, up=False):
    n = x.shape[0]
    rows = lax.broadcasted_iota(jnp.int32, x.shape, 0)
    if up:
        return jnp.where(rows < n - k, pltpu.roll(x, n - k, 0), fill)
    return jnp.where(rows >= k, pltpu.roll(x, k, 0), fill)


@functools.partial(jax.custom_vjp, nondiff_argnums=(1,))
def _delay(x, k):
    return _shift_rows(x, k, 0.0)


def _delay_fwd(x, k):
    return _shift_rows(x, k, 0.0), None


def _delay_bwd(k, _, g):
    return (_shift_rows(g, k, 0.0, up=True),)


_delay.defvjp(_delay_fwd, _delay_bwd)


@functools.partial(jax.custom_vjp, nondiff_argnums=(1,))
def _lane_roll(x, s):
    return pltpu.roll(x, s, 1)


def _lane_roll_fwd(x, s):
    return pltpu.roll(x, s, 1), None


def _lane_roll_bwd(s, _, g):
    return (pltpu.roll(g, g.shape[1] - s, 1),)


_lane_roll.defvjp(_lane_roll_fwd, _lane_roll_bwd)


@jax.custom_vjp
def _bdot(x, w):
    return lax.dot_general(x.astype(BF16), w.astype(BF16), NN, preferred_element_type=F32)


def _bdot_fwd(x, w):
    return _bdot(x, w), (x, w)


def _bdot_bwd(res, g):
    x, w = res
    gb = g.astype(BF16)
    dx = lax.dot_general(gb, w.astype(BF16), NT, preferred_element_type=F32)
    dw = lax.dot_general(x.T.astype(BF16), gb, NN, preferred_element_type=F32)
    return dx, dw


_bdot.defvjp(_bdot_fwd, _bdot_bwd)


def _sigmoid(x):
    return 0.5 * (jnp.tanh(0.5 * x) + 1.0)


def _silu(x):
    return x * _sigmoid(x)


def _rms(x, g):
    return x * lax.rsqrt(jnp.mean(x * x, axis=-1, keepdims=True) + EPS) * g


def _causal_conv(x, w, b):
    kw = w.shape[0]
    tap = lax.broadcasted_iota(jnp.int32, w.shape, 0)
    y = b
    for k in range(kw):
        d = kw - 1 - k
        wk = jnp.sum(jnp.where(tap == k, w, 0.0), axis=0, keepdims=True)
        y = y + wk * (x if d == 0 else _delay(x, d))
    return y


def _rotate(x, cos_f, sin_a, sin_b):
    reps = x.shape[1] // LANE
    if reps > 1:
        cos_f, sin_a, sin_b = (jnp.tile(t, (1, reps)) for t in (cos_f, sin_a, sin_b))
    n = x.shape[1]
    half = QK_ROPE // 2
    return x * cos_f + _lane_roll(x, n - half) * sin_a + _lane_roll(x, half) * sin_b


def _softplus_neg(l):
    u = jnp.exp(-jnp.abs(l))
    log1p_u = jnp.where(u < 0.01, u * (1.0 - u * (0.5 - u * (1.0 / 3.0))), jnp.log(1.0 + u))
    return jnp.maximum(-l, 0.0) + log1p_u


def _f_ln(x, g, scale, shift):
    return (_rms(x, g) * (1.0 + scale) + shift,)


def _f_qkv(qkv, cos_f, sin_a, sin_b, qg, kvg):
    nq, nkv = qg.shape[1], kvg.shape[1]
    qn = _rms(qkv[:, :nq], qg)
    kvn = _rms(qkv[:, nq:nq + nkv], kvg)
    kr = _rotate(qkv[:, nq + nkv:], cos_f, sin_a, sin_b)
    return qn, kvn, kr


def _f_qkv_bwd(qkv, cos_f, sin_a, sin_b, qg, kvg, dqn, dkvn, dkr):
    nq, nkv = qg.shape[1], kvg.shape[1]
    _, vjp_q = jax.vjp(_rms, qkv[:, :nq], qg)
    _, vjp_kv = jax.vjp(_rms, qkv[:, nq:nq + nkv], kvg)
    _, vjp_r = jax.vjp(lambda t: _rotate(t, cos_f, sin_a, sin_b), qkv[:, nq + nkv:])
    dq_lat, dqg = vjp_q(dqn)
    dkv_lat, dkvg = vjp_kv(dkvn)
    (dkr_pre,) = vjp_r(dkr)
    return jnp.concatenate([dq_lat, dkv_lat, dkr_pre], axis=1), dqg, dkvg


QK_SCALE = 1.0 / math.sqrt(QK_NOPE + QK_ROPE)
LOG2_E = 1.4426950408889634
LN_2 = 0.6931471805599453


def _rotate_bf16(x, cos_f, sin_s):
    row = lax.broadcasted_iota(jnp.int32, (LANE, LANE), 0)
    col = lax.broadcasted_iota(jnp.int32, (LANE, LANE), 1)
    half = QK_ROPE // 2
    first, second = QK_NOPE, QK_NOPE + half
    swap = (((row >= first) & (row < second) & (col == row + half))
            | ((row >= second) & (row < second + half) & (col == row - half))).astype(BF16)
    partner = jnp.concatenate(
        [lax.dot_general(x[:, b * LANE:(b + 1) * LANE].astype(BF16), swap, NN, preferred_element_type=F32)
         for b in range(x.shape[1] // LANE)], axis=1)
    reps = x.shape[1] // LANE
    return x * jnp.tile(cos_f, (1, reps)) + partner * jnp.tile(sin_s, (1, reps))


def _f_rotq(q, cos_f, sin_a, sin_b):
    return (_rotate_bf16(q, cos_f, sin_a + sin_b) * (QK_SCALE * LOG2_E),)


def _f_rotq_bwd(cos_f, sin_a, sin_b, dq):
    return (_rotate_bf16(dq, cos_f, -(sin_a + sin_b)) * QK_SCALE,)


def _merge(g_rnn, g_mla, p_rnn, p_mla):
    return _sigmoid(g_rnn) * p_rnn + _sigmoid(g_mla) * p_mla


def _f_merge(g, p_rnn, p_mla):
    d = p_rnn.shape[1]
    return (_merge(g[:, :d], g[:, d:], p_rnn, p_mla),)


def _f_merge_bwd(g, p_rnn, p_mla, dm):
    d = p_rnn.shape[1]
    _, vjp = jax.vjp(_merge, g[:, :d], g[:, d:], p_rnn, p_mla)
    dg_rnn, dg_mla, dp_rnn, dp_mla = vjp(dm)
    return jnp.concatenate([dg_rnn, dg_mla], axis=1), dp_rnn, dp_mla


def _f_res_ln(x, o, gate, g2, scale, shift):
    x1 = x + gate * o
    return x1, _rms(x1, g2) * (1.0 + scale) + shift


def _f_ffn(u_gate, u_val, cw_gate, cw_val, cb_gate, cb_val):
    h = _causal_conv(u_gate, 0.5 * cw_gate, 0.5 * cb_gate)
    return ((h + h * jnp.tanh(h)) * _causal_conv(u_val, cw_val, cb_val),)


def _f_loss(x1, f, tgt, gate, fg):
    y = _rms(x1 + gate * f, fg)
    err = (y - tgt) * (y - tgt)
    return 0.5 * jnp.sum(jnp.mean(err, axis=-1, keepdims=True), axis=0, keepdims=True)


def _f_loss_and_grads(x1, f, tgt, gate, fg):
    loss, vjp = jax.vjp(lambda a, b, c, d: _f_loss(a, b, tgt, c, d), x1, f, gate, fg)
    dx1, df, dgate, dfg = vjp(jnp.ones((1, 1), F32))
    return dx1, df, jnp.broadcast_to(loss, (1, LANE)), dgate, dfg


@jax.custom_vjp
def _decay_and_gain(log_a):
    a = jnp.exp(log_a)
    return a, jnp.sqrt(-jnp.tanh(log_a) * (1.0 + a * a))


def _decay_and_gain_fwd(log_a):
    a, gain = _decay_and_gain(log_a)
    return (a, gain), (a, gain)


def _decay_and_gain_bwd(res, g):
    a, gain = res
    return (g[0] * a - g[1] * (a * a) / gain,)


_decay_and_gain.defvjp(_decay_and_gain_fwd, _decay_and_gain_bwd)


def _f_lru_coeffs(xr, cw, cb, wa, ba, wx, bx, lru, reset):
    xc = _causal_conv(xr, cw, cb)
    r = _sigmoid(_bdot(xc, wa) + ba)
    i = _sigmoid(_bdot(xc, wx) + bx)
    log_a = (-LRU_C) * r * _softplus_neg(lru)
    a, mult = _decay_and_gain(log_a)
    is_reset = reset > 0.5
    a = jnp.where(is_reset, 0.0, a)
    mult = jnp.where(is_reset, 1.0, mult)
    return a, mult * (i * xc)


SCAN_BLOCK = 64


def _scan(a, b, up=False):
    n = a.shape[0]
    blk = min(SCAN_BLOCK, n)
    pos = lax.broadcasted_iota(jnp.int32, a.shape, 0) % blk
    k = 1
    while k < blk:
        inside = (pos < blk - k) if up else (pos >= k)
        shift = n - k if up else k
        b = b + a * jnp.where(inside, pltpu.roll(b, shift, 0), 0.0)
        a = a * jnp.where(inside, pltpu.roll(a, shift, 0), 1.0)
        k *= 2
    blocks = range(n // blk)
    carry = jnp.zeros((1,) + a.shape[1:], a.dtype)
    out = [None] * len(blocks)
    for i in (reversed(blocks) if up else blocks):
        rows = slice(i * blk, (i + 1) * blk)
        out[i] = b[rows] + a[rows] * carry
        carry = out[i][:1] if up else out[i][blk - 1:]
    return jnp.concatenate(out, axis=0)


def _f_lru_fwd(xr, cw, cb, wa, ba, wx, bx, lru, reset):
    a, b = _f_lru_coeffs(xr, cw, cb, wa, ba, wx, bx, lru, reset)
    h = _scan(a, b)
    return h, h


def _f_lru_bwd(xr, cw, cb, wa, ba, wx, bx, lru, reset, h, dh):
    (a, _), vjp = jax.vjp(lambda *p: _f_lru_coeffs(*p, reset), xr, cw, cb, wa, ba, wx, bx, lru)
    g = _scan(_shift_rows(a, 1, 0.0, up=True), dh, up=True)
    return vjp((g * _shift_rows(h, 1, 0.0), g))


def _attn_tile(s):
    return 1024 if s >= 2048 else s // 2


def _keys(kv, kr):
    lane = lax.broadcasted_iota(jnp.int32, kv.shape, 1)
    return jnp.where(lane < QK_NOPE, kv, kr)


ATTN_HEADS_PER_STEP = 2


def _scores(q, kc, diagonal):
    s = lax.dot_general(q, kc, NT, preferred_element_type=F32)
    if not diagonal:
        return s
    rows = lax.broadcasted_iota(jnp.int32, s.shape, 0)
    cols = lax.broadcasted_iota(jnp.int32, s.shape, 1)
    return jnp.where(cols - (s.shape[1] - s.shape[0]) <= rows, s, -jnp.inf)


def _sub_blocks(t, diagonal):
    return ((0, t // 2, t // 2), (t // 2, t // 2, t)) if diagonal else ((0, t, t),)


def _causal_pairs(nb, k_major):
    if k_major:
        pairs = [(qb, kb) for kb in range(nb) for qb in range(kb, nb)]
    else:
        pairs = [(qb, kb) for qb in range(nb) for kb in range(qb + 1)]
    return jnp.array([p[0] for p in pairs], jnp.int32), jnp.array([p[1] for p in pairs], jnp.int32)


def _attn_fwd(q_pre, tables, kv, kr):
    s_len = q_pre.shape[0]
    t = _attn_tile(s_len)
    nb = s_len // t
    hp = ATTN_HEADS_PER_STEP
    wide = hp * HEAD_PAD
    q_tab, k_tab = _causal_pairs(nb, k_major=False)

    def body(qt, kt, qp_ref, cos_ref, sina_ref, sinb_ref, kv_ref, kr_ref, o_ref, lse_ref, q_ref, m_s, acc_s):
        pair = pl.program_id(1)
        qi, ki = qt[pair], kt[pair]

        @pl.when(ki == 0)
        def _():
            m_s[...] = jnp.full(m_s.shape, -jnp.inf, F32)
            acc_s[...] = jnp.zeros(acc_s.shape, F32)
            (rotated,) = _f_rotq(qp_ref[...].astype(F32), cos_ref[...], sina_ref[...], sinb_ref[...])
            q_ref[...] = rotated.astype(q_ref.dtype)

        def step(diagonal):
            for h in range(hp):
                lanes = slice(h * HEAD_PAD, (h + 1) * HEAD_PAD)
                for r0, nr, nk in _sub_blocks(t, diagonal):
                    rows = slice(r0, r0 + nr)
                    kvv = kv_ref[:nk, lanes]
                    s = _scores(q_ref[rows, lanes], _keys(kvv, kr_ref[:nk, :]), diagonal)
                    m_old = m_s[h, rows]
                    m_new = jnp.maximum(m_old, jnp.max(s, axis=-1, keepdims=True))
                    alpha = jnp.exp2(m_old - m_new)
                    p = jnp.exp2(s - jnp.tile(m_new, (1, s.shape[1] // HEAD_PAD)))
                    lane = lax.broadcasted_iota(jnp.int32, kvv.shape, 1)
                    ones_and_values = jnp.where(lane < QK_NOPE, jnp.ones_like(kvv), kvv)
                    acc_s[rows, lanes] = alpha * acc_s[rows, lanes] + lax.dot_general(
                        p.astype(BF16), ones_and_values, NN, preferred_element_type=F32)
                    m_s[h, rows] = m_new

        @pl.when(ki < qi)
        def _():
            step(False)

        @pl.when(ki == qi)
        def _():
            step(True)
            lane = lax.broadcasted_iota(jnp.int32, (t, HEAD_PAD), 1)
            outs = []
            for h in range(hp):
                acc = acc_s[:, h * HEAD_PAD:(h + 1) * HEAD_PAD]
                total = acc[:, :1]
                outs.append(acc / total)
                lse_ref[h] = m_s[h][:, :1] + jnp.log(total) * LOG2_E
            o_ref[...] = jnp.where(lane >= QK_NOPE, outs[0], pltpu.roll(outs[1], QK_NOPE, 1)).astype(o_ref.dtype)

    q_rows = lambda h, p, qt, kt: (qt[p], 0)
    grid_spec = pltpu.PrefetchScalarGridSpec(
        num_scalar_prefetch=2, grid=(N_HEADS // hp, q_tab.shape[0]),
        in_specs=[pl.BlockSpec((t, wide), lambda h, p, qt, kt: (qt[p], h)),
                  pl.BlockSpec((t, HEAD_PAD), q_rows), pl.BlockSpec((t, HEAD_PAD), q_rows),
                  pl.BlockSpec((t, HEAD_PAD), q_rows),
                  pl.BlockSpec((t, wide), lambda h, p, qt, kt: (kt[p], h)),
                  pl.BlockSpec((t, HEAD_PAD), lambda h, p, qt, kt: (kt[p], 0))],
        out_specs=[pl.BlockSpec((t, HEAD_PAD), lambda h, p, qt, kt: (qt[p], h)),
                   pl.BlockSpec((hp, t, 1), lambda h, p, qt, kt: (h, qt[p], 0)),
                   pl.BlockSpec((t, wide), lambda h, p, qt, kt: (qt[p], h))],
        scratch_shapes=[pltpu.VMEM((hp, t, HEAD_PAD), F32), pltpu.VMEM((t, wide), F32)])
    return pl.pallas_call(
        body, name="attn_fwd", grid_spec=grid_spec,
        out_shape=[jax.ShapeDtypeStruct((s_len, N_HEADS // hp * HEAD_PAD), BF16),
                   jax.ShapeDtypeStruct((N_HEADS, s_len, 1), F32),
                   jax.ShapeDtypeStruct((s_len, N_HEADS * HEAD_PAD), BF16)],
        compiler_params=_cparams(("arbitrary", "arbitrary")),
    )(q_tab, k_tab, q_pre, *tables, kv, kr)


def _attn_bwd(q, kv, kr, o, lse, do):
    s_len = q.shape[0]
    t = _attn_tile(s_len)
    nb = s_len // t
    hp = ATTN_HEADS_PER_STEP
    wide = hp * HEAD_PAD
    q_tab, k_tab = _causal_pairs(nb, k_major=True)

    def body(qt, kt, q_ref, kv_ref, kr_ref, o_ref, lse_ref, do_ref, dq_ref, dkv_ref, dkr_ref, dk_s, dv_s, dq_s):
        g, pair = pl.program_id(0), pl.program_id(1)
        qb, kb = qt[pair], kt[pair]

        @pl.when(jnp.logical_and(g == 0, pair == 0))
        def _():
            dkr_ref[...] = jnp.zeros(dkr_ref.shape, F32)

        @pl.when(pair == 0)
        def _():
            dq_s[...] = jnp.zeros(dq_s.shape, F32)

        @pl.when(qb == kb)
        def _():
            dk_s[...] = jnp.zeros(dk_s.shape, F32)
            dv_s[...] = jnp.zeros(dv_s.shape, F32)

        def step(diagonal):
            for h in range(hp):
                lanes = slice(h * HEAD_PAD, (h + 1) * HEAD_PAD)
                for r0, nr, nk in _sub_blocks(t, diagonal):
                    rows, keys = slice(r0, r0 + nr), slice(0, nk)
                    qv, kvv = q_ref[rows, lanes], kv_ref[keys, lanes]
                    pair_do = do_ref[rows, :].astype(F32)
                    lane = lax.broadcasted_iota(jnp.int32, pair_do.shape, 1)
                    mine = (lane >= QK_NOPE) if h == 0 else (lane < QK_NOPE)
                    placed = pair_do if h == 0 else pltpu.roll(pair_do, QK_NOPE, 1)
                    dov = jnp.where(lane >= QK_NOPE, placed, 0.0).astype(BF16)
                    delta = jnp.sum(jnp.where(mine, pair_do * o_ref[rows, :].astype(F32), 0.0), axis=-1, keepdims=True)
                    kc = _keys(kvv, kr_ref[keys, :])
                    p = jnp.exp2(_scores(qv, kc, diagonal) - lse_ref[h, rows])
                    dp = lax.dot_general(dov, kvv, NT, preferred_element_type=F32)
                    ds = p * (dp - delta)
                    dv_s[keys, lanes] += lax.dot_general(p.astype(BF16), dov, TN, preferred_element_type=F32)
                    dk_s[keys, lanes] += lax.dot_general(ds.astype(BF16), qv, TN, preferred_element_type=F32)
                    q_rows = pl.ds(pl.multiple_of(qb * t + r0, nr), nr)
                    dq_s[q_rows, lanes] += lax.dot_general(ds.astype(BF16), kc, NN, preferred_element_type=F32)

        @pl.when(qb > kb)
        def _():
            step(False)

        @pl.when(qb == kb)
        def _():
            step(True)

        @pl.when(qb == nb - 1)
        def _():
            lane = lax.broadcasted_iota(jnp.int32, (t, HEAD_PAD), 1)
            rows = pl.ds(pl.multiple_of(kb * t, t), t)
            for h in range(hp):
                lanes = slice(h * HEAD_PAD, (h + 1) * HEAD_PAD)
                dk = dk_s[:, lanes] * LN_2
                dkv_ref[:, lanes] = jnp.where(lane < QK_NOPE, dk, dv_s[:, lanes]).astype(dkv_ref.dtype)
                dkr_ref[rows, :] += jnp.where(lane >= QK_NOPE, dk, 0.0)

        @pl.when(pair == q_tab.shape[0] - 1)
        def _():
            dq_ref[...] = dq_s[...].astype(dq_ref.dtype)

    all_lanes = N_HEADS * HEAD_PAD
    qmap = lambda h, p, qt, kt: (qt[p], h)
    kmap = lambda h, p, qt, kt: (kt[p], h)
    grid_spec = pltpu.PrefetchScalarGridSpec(
        num_scalar_prefetch=2, grid=(N_HEADS // hp, q_tab.shape[0]),
        in_specs=[pl.BlockSpec((t, wide), qmap),
                  pl.BlockSpec((t, wide), kmap),
                  pl.BlockSpec((t, HEAD_PAD), lambda h, p, qt, kt: (kt[p], 0)),
                  pl.BlockSpec((t, HEAD_PAD), qmap),
                  pl.BlockSpec((hp, t, 1), lambda h, p, qt, kt: (h, qt[p], 0)),
                  pl.BlockSpec((t, HEAD_PAD), qmap)],
        out_specs=[pl.BlockSpec((s_len, wide), lambda h, p, qt, kt: (0, h)),
                   pl.BlockSpec((t, wide), kmap),
                   pl.BlockSpec((s_len, HEAD_PAD), lambda h, p, qt, kt: (0, 0))],
        scratch_shapes=[pltpu.VMEM((t, wide), F32), pltpu.VMEM((t, wide), F32), pltpu.VMEM((s_len, wide), F32)])
    return pl.pallas_call(
        body, name="attn_bwd", grid_spec=grid_spec,
        out_shape=[jax.ShapeDtypeStruct((s_len, all_lanes), BF16),
                   jax.ShapeDtypeStruct((s_len, all_lanes), BF16),
                   jax.ShapeDtypeStruct((s_len, HEAD_PAD), F32)],
        compiler_params=_cparams(("arbitrary", "arbitrary")),
    )(q_tab, k_tab, q, kv, kr, o, lse, do)


def _adamw(name, w, g, m, v):
    rows, cols = w.shape
    tr = _div_tile(rows, max(8, (2 * 1024 * 1024) // (4 * cols)), 8)

    def body(w_ref, g_ref, m_ref, v_ref, d_ref, nm_ref, nv_ref):
        gv = g_ref[...]
        nm = ADAM_B1 * m_ref[...] + (1.0 - ADAM_B1) * gv
        nv = ADAM_B2 * v_ref[...] + (1.0 - ADAM_B2) * jnp.square(gv)
        m_hat = nm / (1.0 - ADAM_B1 ** ADAM_STEP)
        v_hat = nv / (1.0 - ADAM_B2 ** ADAM_STEP)
        d_ref[...] = -ADAM_LR * (m_hat / (jnp.sqrt(v_hat) + ADAM_EPS) + ADAM_WD * w_ref[...])
        nm_ref[...] = nm
        nv_ref[...] = nv

    spec = pl.BlockSpec((tr, cols), lambda i: (i, 0))
    return pl.pallas_call(
        body, name=name, grid=(rows // tr,), in_specs=[spec] * 4, out_specs=[spec] * 3,
        out_shape=[jax.ShapeDtypeStruct((rows, cols), F32)] * 3,
        compiler_params=_cparams(("parallel",)),
    )(w, g, m, v)


ALL7 = (1, 2, 3, 4, 5, 6, 7)
CHIPS = (2, 4, 6)


def _all_gather(name, src, masks):
    bits = 0
    for m in masks:
        bits |= m
    nslots = {7: 8, 6: 4}[bits]
    nm = len(masks)

    def slot_of(x, y, c):
        return {7: 4 * x + 2 * y + c, 6: 2 * x + y}[bits]

    def body(src_ref, out_ref, send_sems, recv_sems, local_sem):
        x, y, c = lax.axis_index("x"), lax.axis_index("y"), lax.axis_index("c")
        mine = slot_of(x, y, c)
        own = pltpu.make_async_copy(src_ref, out_ref.at[mine], local_sem)
        own.start()
        copies = []
        for i, m in enumerate(masks):
            peer = _peer(x, y, c, m)
            copies.append((
                pltpu.make_async_remote_copy(
                    src_ref=src_ref, dst_ref=out_ref.at[mine], send_sem=send_sems.at[i], recv_sem=recv_sems.at[i],
                    device_id=peer, device_id_type=MESH),
                pltpu.make_async_remote_copy(
                    src_ref=src_ref, dst_ref=out_ref.at[slot_of(*peer)], send_sem=send_sems.at[i],
                    recv_sem=recv_sems.at[i], device_id=peer, device_id_type=MESH)))
        for send, _ in copies:
            send.start()
        for _, arrival in copies:
            arrival.wait_recv()
        for send, _ in copies:
            send.wait_send()
        own.wait()

    return pl.pallas_call(
        body, name=name,
        in_specs=[pl.BlockSpec(memory_space=pl.ANY)], out_specs=pl.BlockSpec(memory_space=pl.ANY),
        out_shape=jax.ShapeDtypeStruct((nslots,) + tuple(src.shape), src.dtype),
        scratch_shapes=[pltpu.SemaphoreType.DMA((nm,)), pltpu.SemaphoreType.DMA((nm,)), pltpu.SemaphoreType.DMA],
    )(src)


def _peer(x, y, c, m):
    return (1 - x if m & 4 else x, 1 - y if m & 2 else y, 1 - c if m & 1 else c)


def _comm_call(name, emit, srcs, out_shapes, n_sems, in_place=False):
    n = len(srcs)

    def body(*refs):
        src_refs, out_refs = refs[:n], refs[n:n + len(out_shapes)]
        send_sems, recv_sems = refs[-2], refs[-1]

        def copy(src, dst, i, peer):
            return pltpu.make_async_remote_copy(src_ref=src, dst_ref=dst, send_sem=send_sems.at[i],
                                                recv_sem=recv_sems.at[i], device_id=peer, device_id_type=MESH)

        emit(lax.axis_index("x"), lax.axis_index("y"), lax.axis_index("c"), src_refs, out_refs, copy)

    hbm = pl.BlockSpec(memory_space=pl.ANY)
    return pl.pallas_call(
        body, name=name, in_specs=[hbm] * n, out_specs=[hbm] * len(out_shapes), out_shape=out_shapes,
        scratch_shapes=[pltpu.SemaphoreType.DMA((n_sems,)), pltpu.SemaphoreType.DMA((n_sems,))],
        input_output_aliases={i: i for i in range(n)} if in_place else {},
    )(*srcs)


HBM_SPEC = pl.BlockSpec(memory_space=pltpu.HBM)
SEM_SPEC = pl.BlockSpec(memory_space=pltpu.SEMAPHORE)
DATAFLOW = pltpu.SideEffectType.DATAFLOW_SIDE_EFFECTING


def _chip_copies(srcs, lands, send_sems, recv_sems, mode):
    x, y, c = lax.axis_index("x"), lax.axis_index("y"), lax.axis_index("c")
    chip = 2 * x + y
    sends, arrivals = [], []
    if mode == "pair":
        for k, (s, l) in enumerate(zip(srcs, lands)):
            for group in (sends, arrivals):
                group.append(pltpu.make_async_remote_copy(
                    src_ref=s.at[:, 1 - c], dst_ref=l, send_sem=send_sems.at[3 * k], recv_sem=recv_sems.at[3 * k],
                    device_id=(x, y, 1 - c), device_id_type=MESH))
        return sends, arrivals
    for j, m in enumerate(CHIPS):
        px, py, _ = _peer(x, y, c, m)
        theirs = 2 * px + py
        for k, (s, l) in enumerate(zip(srcs, lands)):
            if mode == "gather":
                src, dst, got = s.at[c], l.at[chip, c], l.at[theirs, c]
            else:
                src, dst, got = s.at[theirs], l.at[chip], l.at[theirs]
            for to, group in ((dst, sends), (got, arrivals)):
                group.append(pltpu.make_async_remote_copy(
                    src_ref=src, dst_ref=to, send_sem=send_sems.at[3 * k + j], recv_sem=recv_sems.at[3 * k + j],
                    device_id=(px, py, c), device_id_type=MESH))
    return sends, arrivals


def _split_start(name, srcs, land_shapes, mode, after):
    n = len(srcs)

    def body(*refs):
        sends, _ = _chip_copies(refs[:n], refs[n:2 * n], refs[2 * n + 1], refs[2 * n + 2], mode)
        for cp in sends:
            cp.start()
        token = refs[-1]
        token[...] = jnp.zeros(token.shape, token.dtype)

    hbm = lambda a: pltpu.with_memory_space_constraint(a, pltpu.HBM)
    lands = [hbm(lax.empty(s.shape, s.dtype)) for s in land_shapes]
    bufs = [pltpu.HBM(a.shape, a.dtype) for a in list(srcs) + lands]
    res = pl.pallas_call(
        body, name=name,
        out_shape=(pltpu.SemaphoreType.DMA((3 * n,)), pltpu.SemaphoreType.DMA((3 * n,)), *bufs,
                   jax.ShapeDtypeStruct((SUBLANES, LANE), F32)),
        in_specs=[HBM_SPEC] * (2 * n) + [pl.BlockSpec(memory_space=pl.ANY)],
        out_specs=[SEM_SPEC, SEM_SPEC] + [HBM_SPEC] * (2 * n) + [pl.BlockSpec(memory_space=pltpu.VMEM)],
        input_output_aliases={i: 2 + i for i in range(2 * n)},
        compiler_params=pltpu.CompilerParams(has_side_effects=DATAFLOW),
    )(*[hbm(s) for s in srcs], *lands, after)
    return res[0], res[1], res[2:2 + n], res[2 + n:2 + 2 * n], res[-1]


def _split_wait(name, send_sems, recv_sems, srcs, lands, mode, after):
    n = len(srcs)

    def body(*refs):
        sends, arrivals = _chip_copies(refs[:n], refs[n:2 * n], refs[2 * n], refs[2 * n + 1], mode)
        for cp in sends:
            cp.wait_send()
        for cp in arrivals:
            cp.wait_recv()

    res = pl.pallas_call(
        body, name=name,
        out_shape=tuple(pltpu.HBM(a.shape, a.dtype) for a in list(srcs) + list(lands)),
        in_specs=[HBM_SPEC] * (2 * n) + [SEM_SPEC, SEM_SPEC, pl.BlockSpec(memory_space=pl.ANY)],
        out_specs=[HBM_SPEC] * (2 * n),
        input_output_aliases={i: i for i in range(2 * n)},
        compiler_params=pltpu.CompilerParams(has_side_effects=DATAFLOW),
    )(*srcs, *lands, send_sems, recv_sems, after)
    return res[n:]


def _relay_sibling(lands):
    def emit(x, y, c, srcs, outs, copy):
        sib = (x, y, 1 - c)
        sends, arrivals = [], []
        for j, m in enumerate(CHIPS):
            px, py, _ = _peer(x, y, c, m)
            theirs = 2 * px + py
            for k, (s, o) in enumerate(zip(srcs, outs)):
                sends.append(copy(s.at[theirs, c], o.at[theirs, c], 3 * k + j, sib))
                arrivals.append(copy(s.at[theirs, c], o.at[theirs, 1 - c], 3 * k + j, sib))
        for cp in sends:
            cp.start()
        for cp in arrivals:
            cp.wait_recv()
        for cp in sends:
            cp.wait_send()

    shapes = [jax.ShapeDtypeStruct(l.shape, l.dtype) for l in lands]
    return _comm_call("relay_weights", emit, lands, shapes, 3 * len(lands), in_place=True)


def _gather_weights(halves):
    n = len(halves)

    def emit(x, y, c, srcs, outs, copy):
        chip = 2 * x + y
        sib = (x, y, 1 - c)
        first, relay, landed, relayed = [], [], [], []
        for j, m in enumerate(CHIPS):
            px, py, _ = _peer(x, y, c, m)
            theirs = 2 * px + py
            for k in range(n):
                i = 6 * k + j
                first.append(copy(srcs[k].at[c], outs[k].at[chip, c], i, (px, py, c)))
                landed.append(copy(srcs[k].at[c], outs[k].at[theirs, c], i, (px, py, c)))
                relay.append(copy(outs[k].at[theirs, c], outs[k].at[theirs, c], i + 3, sib))
                relayed.append(copy(outs[k].at[theirs, 1 - c], outs[k].at[theirs, 1 - c], i + 3, sib))
        for cp in first:
            cp.start()
        for arrival, onward in zip(landed, relay):
            arrival.wait_recv()
            onward.start()
        for arrival in relayed:
            arrival.wait_recv()
        for cp in first + relay:
            cp.wait_send()

    shapes = [jax.ShapeDtypeStruct((4,) + h.shape, h.dtype) for h in halves]
    return _comm_call("gather_weights", emit, halves, shapes, 6 * n)


def _pair_exchange(name, chunks):
    def emit(x, y, c, srcs, outs, copy):
        sib = (x, y, 1 - c)
        sends = [copy(s.at[:, 1 - c], o, k, sib) for k, (s, o) in enumerate(zip(srcs, outs))]
        for cp in sends:
            cp.start()
        for cp in sends:
            cp.wait_recv()
        for cp in sends:
            cp.wait_send()

    shapes = [jax.ShapeDtypeStruct((4,) + g.shape[2:], g.dtype) for g in chunks]
    return _comm_call(name, emit, chunks, shapes, len(chunks))


def _share_sibling(name, parts):
    def emit(x, y, c, srcs, outs, copy):
        sib = (x, y, 1 - c)
        sends = [copy(s, o.at[c], k, sib) for k, (s, o) in enumerate(zip(srcs, outs))]
        arrivals = [copy(s, o.at[1 - c], k, sib) for k, (s, o) in enumerate(zip(srcs, outs))]
        for cp in sends:
            cp.start()
        for cp in arrivals:
            cp.wait_recv()
        for cp in sends:
            cp.wait_send()

    shapes = [jax.ShapeDtypeStruct((2,) + p.shape, p.dtype) for p in parts]
    return _comm_call(name, emit, parts, shapes, len(parts))


def _reduce_pair(name, chunk, from_sibling, core):
    n, _, h, cols = chunk.shape
    rt = _div_tile(h, max(16, (1 << 20) // (4 * cols)), 16)

    def body(core_ref, a_ref, b_ref, o_ref):
        o_ref[...] = (a_ref[...] + b_ref[...]).astype(o_ref.dtype)

    grid_spec = pltpu.PrefetchScalarGridSpec(
        num_scalar_prefetch=1, grid=(n, h // rt),
        in_specs=[pl.BlockSpec((None, None, rt, cols), lambda s, i, core_ref: (s, core_ref[0], i, 0)),
                  pl.BlockSpec((None, rt, cols), lambda s, i, core_ref: (s, i, 0))],
        out_specs=pl.BlockSpec((None, rt, cols), lambda s, i, core_ref: (s, i, 0)))
    return pl.pallas_call(
        body, name=name, grid_spec=grid_spec, out_shape=jax.ShapeDtypeStruct((n, h, cols), BF16),
        compiler_params=_cparams(("parallel", "parallel")),
    )(core, chunk, from_sibling)


def _reduce_quad(name, q, after=None):
    _, h, cols = q.shape
    rt = _div_tile(h, max(16, (1 << 20) // (4 * cols)), 16)

    def body(q_ref, *rest):
        v = q_ref[...].astype(F32)
        rest[-1][...] = ((v[0] + v[1]) + v[2]) + v[3]

    held = [] if after is None else [after]
    return pl.pallas_call(
        body, name=name, grid=(h // rt,),
        in_specs=[pl.BlockSpec((4, rt, cols), lambda i: (0, i, 0))] + [pl.BlockSpec(memory_space=pl.ANY)] * len(held),
        out_specs=pl.BlockSpec((rt, cols), lambda i: (i, 0)),
        out_shape=jax.ShapeDtypeStruct((h, cols), F32),
        compiler_params=_cparams(("parallel",)),
    )(q, *held)


def _unshard(seg, kind):
    n, r, c = seg.shape
    if kind == "col":
        return seg.transpose(1, 0, 2).reshape(r, n * c)
    return seg.reshape(n * r, c)


def _pad_rows(flat, rows):
    n, ln = flat.shape
    return jnp.pad(flat, ((0, 0), (0, rows * PACK_COLS - ln))).reshape(n, rows, PACK_COLS)


def _block_diag_pairs(w):
    n2, bs, _ = w.shape
    eye = jnp.eye(2, dtype=w.dtype)
    z = w.reshape(n2 // 2, 2, bs, 1, bs) * eye[None, :, None, :, None]
    return z.reshape(n2 // 2, 2 * bs, 2 * bs).transpose(1, 0, 2).reshape(2 * bs, n2 * bs)


def _block_diag_pairs_t(d, bs=64):
    n = d.shape[1] // (2 * bs)
    z = d.reshape(2 * bs, n, 2 * bs).transpose(1, 0, 2).reshape(n, 2, bs, 2, bs)
    return jnp.stack([z[:, 0, :, 0, :], z[:, 1, :, 1, :]], axis=1).reshape(2 * n, bs, bs)


BIG = (("w_in", "col"), ("w_uq", "col"), ("w_ukv", "col"), ("w_proj_rnn", "row"), ("w_proj_mla", "row"),
       ("w_out", "row"), ("w_up", "col"), ("w_down", "row"))
FIRST_USED = ("w_in", "w_uq", "w_ukv")
CONVS = (("conv_w", "col"), ("ffn_conv_w", "col"))
SMALL = ("b_ada", "norm1_g", "conv_b", "w_gate_a", "b_gate_a", "w_gate_x", "b_gate_x", "lru_param",
         "q_norm_g", "kv_norm_g", "norm2_g", "ffn_conv_b", "final_g")
WEIGHTS = ("w_ada", "b_ada", "norm1_g", "w_in", "conv_w", "conv_b", "w_gate_a", "b_gate_a", "w_gate_x",
           "b_gate_x", "lru_param", "q_norm_g", "w_uq", "kv_norm_g", "w_ukv", "w_proj_rnn", "w_proj_mla",
           "w_out", "norm2_g", "w_up", "ffn_conv_w", "ffn_conv_b", "w_down", "final_g")


def _step(x, c, positions, w, m_in, v_in, loss_target):
    s_len, d = x.shape[1], x.shape[2]
    x2d = x[0]
    tgt = loss_target[0]
    xi, yi, ci = lax.axis_index("x"), lax.axis_index("y"), lax.axis_index("c")
    chip = 2 * xi + yi
    me = 2 * chip + ci
    tile = min(256, s_len)
    nt = s_len // tile

    local2d = {k: w[k][0] for k, _ in BIG + CONVS}
    kinds = dict(BIG)
    halves_bf = {k: local2d[k].astype(BF16).reshape(2, local2d[k].shape[0] // 2, local2d[k].shape[1]) for k, _ in BIG}
    first_names = [k for k, _ in BIG if k in FIRST_USED]
    later_names = [k for k, _ in BIG if k not in FIRST_USED]
    full = {}

    def assemble(k, g):
        g = lax.dynamic_update_index_in_dim(g, halves_bf[k][None], chip, 0).reshape((4,) + local2d[k].shape)
        if k == "w_up":
            full["w_up_gate"], full["w_up_val"] = _unshard(g[:2], kinds[k]), _unshard(g[2:], kinds[k])
        else:
            full[k] = _unshard(g, kinds[k])

    first_got = _gather_weights([halves_bf[k] for k in first_names])
    for k, g in zip(first_names, first_got):
        assemble(k, g)
    conv_flat = jnp.concatenate([local2d[k].reshape(-1) for k, _ in CONVS])
    conv_rows = -(-conv_flat.shape[0] // PACK_COLS)
    conv_all = _all_gather("gather_conv_w", _pad_rows(conv_flat[None], conv_rows)[0], CHIPS)
    conv_all = conv_all.reshape(4, -1)
    off = 0
    for k, kind in CONVS:
        r, cc = local2d[k].shape
        full[k] = _unshard(conv_all[:, off:off + r * cc].reshape(4, r, cc), kind)
        off += r * cc

    d_rnn = w["conv_b"].shape[1]
    n_q, n_kv = w["q_norm_g"].shape[1], w["kv_norm_g"].shape[1]
    w_in = full["w_in"]
    o1, o2, o3 = d_rnn + n_q, d_rnn + n_q + n_kv, d_rnn + n_q + n_kv + QK_ROPE
    w_rnn = w_in[:, :d_rnn]
    zpad = lambda n: jnp.zeros((d, n), BF16)
    w_qkv = jnp.concatenate([w_in[:, d_rnn:o2], zpad(QK_NOPE), w_in[:, o2:o3], zpad(LANE - QK_NOPE - QK_ROPE)], axis=1)
    w_g = w_in[:, o3:]
    hd = QK_NOPE + QK_ROPE
    w_uq = jnp.pad(full["w_uq"].reshape(n_q, N_HEADS, hd), ((0, 0), (0, 0), (0, HEAD_PAD - hd))).reshape(n_q, -1)
    w_ukv = full["w_ukv"]
    v_head = w_ukv.shape[1] // N_HEADS - QK_NOPE
    d_ff = w["ffn_conv_b"].shape[1] // 2
    ffn_cw_gate, ffn_cw_val = full["ffn_conv_w"][:, :d_ff], full["ffn_conv_w"][:, d_ff:]
    ffn_cb_gate, ffn_cb_val = w["ffn_conv_b"][:, :d_ff], w["ffn_conv_b"][:, d_ff:]
    conv_w, conv_b = full["conv_w"], w["conv_b"]
    wa_bd = _block_diag_pairs(w["w_gate_a"][0])
    wx_bd = _block_diag_pairs(w["w_gate_x"][0])

    c_all = _all_gather("gather_c", c, ALL7).reshape(8, d)
    c_rows = 128
    (c_act,) = _tiled("silu_c", lambda v: (_silu(v),), 1, [(jnp.pad(c_all, ((0, c_rows - 8), (0, 0))), (c_rows, d), "full")],
                      [((c_rows, d), F32, (c_rows, d), "full")])
    w_ada = w["w_ada"][0]
    n_mod = w_ada.shape[1]
    b_loc = lax.dynamic_slice_in_dim(w["b_ada"], chip * n_mod, n_mod, axis=1)
    mod_loc = _mm("ada_fwd", c_act, w_ada, add=jnp.broadcast_to(b_loc, (c_rows, n_mod)))
    mod_all = _all_gather("gather_mod", mod_loc[:8], CHIPS)
    mod = lax.dynamic_index_in_dim(mod_all, me, 1, keepdims=False).reshape(1, -1)
    shift1, scale1, gate1, shift2, scale2, gate2 = [mod[:, i * d:(i + 1) * d] for i in range(6)]

    small_done = (mod[:, :1] + conv_all[:1, :1] + first_got[0][0, 0, :1, :1].astype(F32))
    later_flight = _split_start(
        "gather_later_start", [halves_bf[k] for k in later_names],
        [jax.ShapeDtypeStruct((4,) + halves_bf[k].shape, BF16) for k in later_names], "gather", after=small_done)

    half = QK_ROPE // 2
    inv_freq = ROPE_THETA ** (-jnp.arange(half, dtype=F32) / half)
    ang = positions[0].astype(F32)[:, None] * inv_freq
    cos, sin = jnp.cos(ang), jnp.sin(ang)
    one, zero = jnp.ones((s_len, QK_NOPE), F32), jnp.zeros((s_len, half), F32)
    tail = jnp.zeros((s_len, LANE - QK_NOPE - QK_ROPE), F32)
    cos_f = jnp.concatenate([one, cos, cos, tail + 1.0], axis=1)
    sin_a = jnp.concatenate([one * 0.0, -sin, zero, tail], axis=1)
    sin_b = jnp.concatenate([one * 0.0, zero, sin, tail], axis=1)
    reset = (positions[0] == 0).astype(F32)[:, None]
    tabs = [(cos_f, (tile, LANE), "row"), (sin_a, (tile, LANE), "row"), (sin_b, (tile, LANE), "row")]

    def rowspec(a):
        return (a, (tile, a.shape[1]), "row")

    def full2(a):
        return (a, a.shape, "full")

    def rowout(cols, dt):
        return ((s_len, cols), dt, (tile, cols), "row")

    def accout(a):
        return (a.shape, F32, a.shape, "acc")

    norm1_g = w["norm1_g"] + later_flight[4][:1, :1]
    norm2_g, final_g = w["norm2_g"], w["final_g"].reshape(1, d)
    ln1_in = [rowspec(x2d), full2(norm1_g), full2(scale1), full2(shift1)]
    big_tile = min(512, s_len)
    (h1,) = _tiled("ln1", _f_ln, nt, ln1_in, [rowout(d, BF16)], row_tile=big_tile)
    x_rnn = _mm("in_rnn", h1, w_rnn, out_dtype=BF16)
    qkv = _mm("in_qkv", h1, w_qkv)
    gates = _mm("in_gates", h1, w_g, out_dtype=BF16)

    ct = LANE
    n_ct = d_rnn // ct
    colspec = lambda a, width=ct: (a, (a.shape[0], width), "col")
    lru_in = [colspec(x_rnn), colspec(conv_w), colspec(conv_b), colspec(wa_bd), colspec(w["b_gate_a"]),
              colspec(wx_bd), colspec(w["b_gate_x"]), colspec(w["lru_param"]), full2(reset)]
    y_rnn, h_rnn = _tiled("lru_fwd", _f_lru_fwd, n_ct, lru_in,
                          [((s_len, d_rnn), BF16, (s_len, ct), "col"), ((s_len, d_rnn), F32, (s_len, ct), "col")])

    qkv_in = [rowspec(qkv)] + tabs + [full2(w["q_norm_g"]), full2(w["kv_norm_g"])]
    qn, kvn, kr = _tiled("qkv_norm", _f_qkv, nt, qkv_in, [rowout(n_q, BF16), rowout(n_kv, BF16), rowout(LANE, BF16)],
                         row_tile=big_tile)
    q_pre = _mm("up_q", qn, w_uq, out_dtype=BF16)
    kv = _mm("up_kv", kvn, w_ukv, out_dtype=BF16)
    o_mla, lse, q_cat = _attn_fwd(q_pre, (cos_f, sin_a, sin_b), kv, kr)

    send_sems, recv_sems, flown, landed, _ = later_flight
    landed = _split_wait("gather_later_wait", send_sems, recv_sems, flown, landed, "gather", after=o_mla)
    for k, g in zip(later_names, _relay_sibling(landed)):
        assemble(k, g)
    w_pr = full["w_proj_rnn"]
    assert ATTN_HEADS_PER_STEP == 2 and 2 * v_head == HEAD_PAD
    swap_pairs = lambda a: a.reshape(N_HEADS // 2, 2, v_head, d)[:, ::-1].reshape(-1, d)
    w_pm = swap_pairs(full["w_proj_mla"])
    w_out = full["w_out"]
    w_up_gate, w_up_val = full["w_up_gate"], full["w_up_val"]
    w_down = full["w_down"]

    p_rnn = _mm("proj_rnn", y_rnn, w_pr, out_dtype=BF16)
    p_mla = _mm("proj_mla", o_mla, w_pm, out_dtype=BF16)
    merge_in = [rowspec(gates), rowspec(p_rnn), rowspec(p_mla)]
    (merged,) = _tiled("merge", _f_merge, nt, merge_in, [rowout(d, BF16)])
    o_tok = _mm("out_proj", merged, w_out)
    res_in = [rowspec(x2d), rowspec(o_tok), full2(gate1), full2(norm2_g), full2(scale2), full2(shift2)]
    x1, h2 = _tiled("res_ln2", _f_res_ln, nt, res_in, [rowout(d, F32), rowout(d, BF16)], row_tile=big_tile)
    u_gate = _mm("ffn_up_gate", h2, w_up_gate, out_dtype=BF16)
    u_val = _mm("ffn_up_val", h2, w_up_val, out_dtype=BF16)
    n_ft = d_ff // LANE
    ffn_in = [colspec(a) for a in (u_gate, u_val, ffn_cw_gate, ffn_cw_val, ffn_cb_gate, ffn_cb_val)]
    (act,) = _tiled("ffn_conv", _f_ffn, n_ft, ffn_in, [((s_len, d_ff), BF16, (s_len, LANE), "col")])
    f_tok = _mm("ffn_down", act, w_down)

    loss_in = [rowspec(x1), rowspec(f_tok), rowspec(tgt), full2(gate2), full2(final_g)]
    dx1, df, loss_row, d_gate2, d_final_g = _tiled(
        "loss", _f_loss_and_grads, nt, loss_in,
        [rowout(d, F32), rowout(d, BF16), ((1, LANE), F32, (1, LANE), "acc"), accout(gate2), accout(final_g)],
        row_tile=big_tile)
    loss = lax.psum(loss_row[0, 0], ("x", "y", "c"))

    d_act = _mm("ffn_down_dx", df, w_down, tb=True, out_dtype=BF16)
    g_w_down = _mm("ffn_down_dw", act, df, ta=True)
    taps = ffn_cw_gate.shape[0]
    du_gate, du_val, g_cw_gate, g_cw_val, g_cb_gate, g_cb_val = _tiled(
        "ffn_conv_bwd", _vjp_of(_f_ffn, 6, (0, 1, 2, 3, 4, 5)), n_ft, ffn_in + [colspec(d_act)],
        [((s_len, d_ff), BF16, (s_len, LANE), "col")] * 2 + [((taps, d_ff), F32, (taps, LANE), "col")] * 2
        + [((1, d_ff), F32, (1, LANE), "col")] * 2)
    dh2 = _mm("ffn_up_gate_dx", du_gate, w_up_gate, tb=True)
    dh2 = _mm("ffn_up_val_dx", du_val, w_up_val, tb=True, add=dh2, out_dtype=BF16)
    g_w_up_halves = [_mm("ffn_up_gate_dw", h2, du_gate, ta=True), _mm("ffn_up_val_dw", h2, du_val, ta=True)]
    g_ffn_cw = jnp.concatenate([g_cw_gate, g_cw_val], axis=1)
    g_ffn_cb = jnp.concatenate([g_cb_gate, g_cb_val], axis=1)

    def chunked(k, gk):
        r, cc = local2d[k].shape
        if kinds[k] == "col":
            gk = gk.reshape(r, 4, cc).transpose(1, 0, 2)
        return gk.reshape(4, 2, r // 2, cc)

    r_up, c_up = local2d["w_up"].shape
    up_chunks = jnp.concatenate([g.reshape(r_up, 2, c_up).transpose(1, 0, 2) for g in g_w_up_halves], axis=0)
    ffn_chunks = {"w_up": up_chunks.reshape(4, 2, r_up // 2, c_up), "w_down": chunked("w_down", g_w_down)}
    ffn_names = [k for k in later_names if k in ffn_chunks]
    ffn_pair_flight = _split_start(
        "reduce_pair_ffn_start", [ffn_chunks[k] for k in ffn_names],
        [jax.ShapeDtypeStruct((4,) + ffn_chunks[k].shape[2:], F32) for k in ffn_names], "pair",
        after=ffn_chunks[ffn_names[-1]])
    gate1_held = gate1 + ffn_pair_flight[4][:1, :1]

    res_bwd = _vjp_of(_f_res_ln, 6, (0, 1, 2, 3, 4, 5))
    dx_res, do_tok, d_gate1, g_norm2, d_scale2, d_shift2 = _tiled(
        "res_ln2_bwd", res_bwd, nt, res_in[:2] + [full2(gate1_held)] + res_in[3:] + [rowspec(dx1), rowspec(dh2)],
        [rowout(d, F32), rowout(d, BF16), accout(gate1), accout(norm2_g), accout(scale2), accout(shift2)],
        row_tile=big_tile)
    d_merged = _mm("out_proj_dx", do_tok, w_out, tb=True, out_dtype=BF16)
    g_w_out = _mm("out_proj_dw", merged, do_tok, ta=True)
    d_gates, dp_rnn, dp_mla = _tiled(
        "merge_bwd", _f_merge_bwd, nt, merge_in + [rowspec(d_merged)],
        [rowout(gates.shape[1], BF16), rowout(d, BF16), rowout(d, BF16)])
    dy_rnn = _mm("proj_rnn_dx", dp_rnn, w_pr, tb=True, out_dtype=BF16)
    g_w_pr = _mm("proj_rnn_dw", y_rnn, dp_rnn, ta=True)
    do_mla = _mm("proj_mla_dx", dp_mla, w_pm, tb=True, out_dtype=BF16)
    g_w_pm = _mm("proj_mla_dw", o_mla, dp_mla, ta=True)

    core = ci.astype(jnp.int32).reshape(1)

    def pair_sums(tag, names, chunks):
        received = _pair_exchange("reduce_pair_exchange_" + tag, chunks)
        return [_reduce_pair("reduce_pair_" + k, ck, got, core) for k, ck, got in zip(names, chunks, received)]

    g_later = {"w_proj_rnn": g_w_pr, "w_proj_mla": swap_pairs(g_w_pm), "w_out": g_w_out}
    send_sems, recv_sems, flown, landed, _ = ffn_pair_flight
    ffn_received = _split_wait("reduce_pair_ffn_wait", send_sems, recv_sems, flown, landed, "pair", after=g_w_pm)
    sums = {k: _reduce_pair("reduce_pair_" + k, ffn_chunks[k], got, core) for k, got in zip(ffn_names, ffn_received)}
    other_names = [k for k in later_names if k not in ffn_chunks]
    sums.update(zip(other_names, pair_sums("ready", other_names, [chunked(k, g_later[k]) for k in other_names])))
    sums_ready = [sums[k] for k in later_names]
    ready_flight = _split_start(
        "reduce_ready_start", sums_ready, [jax.ShapeDtypeStruct(s.shape, s.dtype) for s in sums_ready], "alltoall",
        after=sums_ready[0])
    kr_held = kr + ready_flight[4][:1, :].astype(BF16)

    dq_cat, dkv, dkr = _attn_bwd(q_cat, kv, kr_held, o_mla, lse, do_mla)
    (dq_pre,) = _tiled("rot_q_bwd", _f_rotq_bwd, nt, tabs + [rowspec(dq_cat)],
                       [rowout(q_pre.shape[1], BF16)], row_tile=big_tile)
    dqn = _mm("up_q_dx", dq_pre, w_uq, tb=True, out_dtype=BF16)
    g_w_uq = _mm("up_q_dw", qn, dq_pre, ta=True)
    dkvn = _mm("up_kv_dx", dkv, w_ukv, tb=True, out_dtype=BF16)
    g_w_ukv = _mm("up_kv_dw", kvn, dkv, ta=True)
    dqkv, g_q_norm, g_kv_norm = _tiled(
        "qkv_norm_bwd", _f_qkv_bwd, nt, qkv_in + [rowspec(dqn), rowspec(dkvn), rowspec(dkr)],
        [rowout(qkv.shape[1], BF16), accout(w["q_norm_g"]), accout(w["kv_norm_g"])], row_tile=big_tile)

    lru_out = [((s_len, d_rnn), BF16, (s_len, ct), "col")]
    for a in (conv_w, conv_b, wa_bd, w["b_gate_a"], wx_bd, w["b_gate_x"], w["lru_param"]):
        lru_out.append((a.shape, F32, (a.shape[0], ct), "col"))
    dx_rnn, g_conv_w, g_conv_b, g_wa_bd, g_b_a, g_wx_bd, g_b_x, g_lru = _tiled(
        "lru_bwd", _f_lru_bwd, n_ct, lru_in + [colspec(h_rnn), colspec(dy_rnn)], lru_out)

    dh1 = _mm("in_gates_dx", d_gates, w_g, tb=True)
    dh1 = _mm("in_qkv_dx", dqkv, w_qkv, tb=True, add=dh1)
    dh1 = _mm("in_rnn_dx", dx_rnn, w_rnn, tb=True, add=dh1)
    g_w_rnn = _mm("in_rnn_dw", h1, dx_rnn, ta=True)
    g_w_qkv = _mm("in_qkv_dw", h1, dqkv, ta=True)
    g_w_g = _mm("in_gates_dw", h1, d_gates, ta=True)

    g_first = {
        "w_in": jnp.concatenate([g_w_rnn, g_w_qkv[:, :n_q + n_kv],
                                 g_w_qkv[:, n_q + n_kv + QK_NOPE:n_q + n_kv + QK_NOPE + QK_ROPE], g_w_g], axis=1),
        "w_uq": g_w_uq.reshape(n_q, N_HEADS, HEAD_PAD)[:, :, :hd].reshape(n_q, -1),
        "w_ukv": g_w_ukv,
    }
    first_chunks = [chunked(k, g_first[k]) for k in first_names]
    first_pair_flight = _split_start(
        "reduce_pair_first_start", first_chunks,
        [jax.ShapeDtypeStruct((4,) + ck.shape[2:], F32) for ck in first_chunks], "pair", after=first_chunks[0])

    ln_bwd = _vjp_of(_f_ln, 4, (0, 1, 2, 3))

    def ln1_bwd(xv, gv, sc, sh, dxr, dh):
        dx, dg, dsc, dsh = ln_bwd(xv, gv, sc, sh, dh)
        return dx + dxr, dg, dsc, dsh

    ln1_held = [ln1_in[0], full2(norm1_g + first_pair_flight[4][:1, :1])] + ln1_in[2:]
    grad_x, g_norm1, d_scale1, d_shift1 = _tiled(
        "ln1_bwd", ln1_bwd, nt, ln1_held + [rowspec(dx_res), rowspec(dh1)],
        [rowout(d, F32), accout(norm1_g), accout(scale1), accout(shift1)], row_tile=big_tile)

    dmod = jnp.concatenate([d_shift1, d_scale1, d_gate1, d_shift2, d_scale2, d_gate2], axis=1)
    dmod_all = _all_gather("gather_dmod", dmod, ALL7).reshape(8, -1)
    dmod_loc = lax.dynamic_slice_in_dim(dmod_all, chip * n_mod, n_mod, axis=1)
    g_w_ada = _mm("ada_dw", c_act, jnp.pad(dmod_loc, ((0, c_rows - 8), (0, 0))), ta=True)

    g_convs = {"conv_w": g_conv_w, "ffn_conv_w": g_ffn_cw}
    g_small = {
        "b_ada": dmod, "norm1_g": g_norm1, "conv_b": g_conv_b,
        "w_gate_a": _block_diag_pairs_t(g_wa_bd)[None], "b_gate_a": g_b_a,
        "w_gate_x": _block_diag_pairs_t(g_wx_bd)[None], "b_gate_x": g_b_x, "lru_param": g_lru,
        "q_norm_g": g_q_norm, "kv_norm_g": g_kv_norm, "norm2_g": g_norm2,
        "ffn_conv_b": g_ffn_cb, "final_g": d_final_g.reshape(w["final_g"].shape),
    }

    small_flat = jnp.concatenate([g_small[k].reshape(-1) for k in SMALL] + [g_convs[k].reshape(-1) for k, _ in CONVS])
    small_rows = -(-small_flat.shape[0] // (8 * PACK_COLS * PACK_ROW_UNIT)) * PACK_ROW_UNIT
    small_chunk = _pad_rows(small_flat[None], 8 * small_rows).reshape(4, 2, small_rows, PACK_COLS)
    last_names = first_names + ["small"]
    send_sems, recv_sems, flown, landed, _ = first_pair_flight
    first_received = _split_wait("reduce_pair_first_wait", send_sems, recv_sems, flown, landed, "pair", after=small_chunk)
    sums_last = [_reduce_pair("reduce_pair_" + k, ck, got, core)
                 for k, ck, got in zip(first_names, first_chunks, first_received)]
    sums_last += pair_sums("small", ["small"], [small_chunk])
    send_sems, recv_sems, flown, landed, _ = ready_flight
    quads_ready = _split_wait("reduce_ready_wait", send_sems, recv_sems, flown, landed, "alltoall", after=grad_x)
    last_flight = _split_start(
        "reduce_last_start", sums_last, [jax.ShapeDtypeStruct(s.shape, s.dtype) for s in sums_last], "alltoall",
        after=quads_ready[0])
    grads = {"w_ada": g_w_ada[None]}
    delta, new_m, new_v = {}, {}, {}

    def adamw(k):
        shp = w[k].shape
        flip = len(shp) == 3 and shp[-1] % LANE != 0 and shp[-2] % LANE == 0
        view = (lambda a: jnp.swapaxes(a, 1, 2)) if flip else (lambda a: a)
        two_d = (-1, view(w[k]).shape[-1]) if len(shp) > 1 else (1, -1)
        dk, mk, vk = _adamw("adamw_" + k, *[view(a).reshape(two_d) for a in (w[k], grads[k], m_in[k], v_in[k])])
        back = lambda a: view(a.reshape(view(w[k]).shape))
        delta[k], new_m[k], new_v[k] = back(dk), back(mk), back(vk)

    def finish(tag, names, quads, sums, after):
        reduced = {}
        for k, quad, ps in zip(names, quads, sums):
            quad = lax.dynamic_update_index_in_dim(quad, lax.dynamic_index_in_dim(ps, chip, 0, keepdims=True), chip, 0)
            reduced[k] = _reduce_quad("reduce_quad_" + k, quad, after)
        big = [k for k in names if k != "small"]
        for k, both in zip(big, _share_sibling("share_sibling_" + tag, [reduced[k] for k in big])):
            grads[k] = lax.dynamic_update_index_in_dim(both, reduced[k][None], ci, 0).reshape(w[k].shape)
        return reduced

    finish("ready", later_names, quads_ready, sums_ready, after=last_flight[4])
    for k in later_names + ["w_ada"]:
        adamw(k)
    send_sems, recv_sems, flown, landed, _ = last_flight
    quads_last = _split_wait("reduce_last_wait", send_sems, recv_sems, flown, landed, "alltoall",
                             after=delta[later_names[-1]])
    reduced = finish("last", last_names, quads_last, sums_last, after=None)
    small_grad = _all_gather("share_small", reduced["small"], ALL7).reshape(-1)
    off = 0
    for k in SMALL:
        grads[k] = small_grad[off:off + w[k].size].reshape(w[k].shape)
        off += w[k].size
    for k, _ in CONVS:
        r, cc = local2d[k].shape
        whole = small_grad[off:off + 4 * r * cc].reshape(r, 4 * cc)
        grads[k] = lax.dynamic_slice_in_dim(whole, chip * cc, cc, axis=1)[None]
        off += 4 * r * cc
    for k in WEIGHTS:
        if k not in delta:
            adamw(k)

    return (loss, grad_x[None], *[grads[k] for k in WEIGHTS], *[delta[k] for k in WEIGHTS],
            *[new_m[k] for k in WEIGHTS], *[new_v[k] for k in WEIGHTS])


def kernel(x, c, positions, w_ada, b_ada, norm1_g, w_in, conv_w, conv_b, w_gate_a, b_gate_a, w_gate_x, b_gate_x, lru_param, q_norm_g, w_uq, kv_norm_g, w_ukv, w_proj_rnn, w_proj_mla, w_out, norm2_g, w_up, ffn_conv_w, ffn_conv_b, w_down, final_g, loss_target, m_w_ada, m_b_ada, m_norm1_g, m_w_in, m_conv_w, m_conv_b, m_w_gate_a, m_b_gate_a, m_w_gate_x, m_b_gate_x, m_lru_param, m_q_norm_g, m_w_uq, m_kv_norm_g, m_w_ukv, m_w_proj_rnn, m_w_proj_mla, m_w_out, m_norm2_g, m_w_up, m_ffn_conv_w, m_ffn_conv_b, m_w_down, m_final_g, v_w_ada, v_b_ada, v_norm1_g, v_w_in, v_conv_w, v_conv_b, v_w_gate_a, v_b_gate_a, v_w_gate_x, v_b_gate_x, v_lru_param, v_q_norm_g, v_w_uq, v_kv_norm_g, v_w_ukv, v_w_proj_rnn, v_w_proj_mla, v_w_out, v_norm2_g, v_w_up, v_ffn_conv_w, v_ffn_conv_b, v_w_down, v_final_g):
    given = dict(locals())
    w = {k: given[k] for k in WEIGHTS}
    m_in = {k: given["m_" + k] for k in WEIGHTS}
    v_in = {k: given["v_" + k] for k in WEIGHTS}
    return _step(x, c, positions, w, m_in, v_in, loss_target)
```

```python
import functools
import math

import jax
import jax.numpy as jnp
from jax import lax
from jax.experimental import pallas as pl
from jax.experimental.pallas import tpu as pltpu

F32 = jnp.float32
BF16 = jnp.bfloat16

EPS = 1e-6
LRU_C = 8.0
N_HEADS = 16
QK_NOPE = 64
QK_ROPE = 32
HEAD_PAD = 128
ROPE_THETA = 10000.0
ADAM_LR = 0.001
ADAM_B1 = 0.9
ADAM_B2 = 0.999
ADAM_EPS = 1e-08
ADAM_WD = 0.01
ADAM_STEP = 10

LANE = 128
SUBLANES = 8
VMEM_LIMIT = 48 * 1024 * 1024
MM_TILE_M = MM_TILE_N = MM_TILE_K = 1408
ROW_TILE = 256
WIDE_ROW_TILE = 512
ATTN_TILE = 1024
ADAM_BLOCK_BYTES = 2 * 1024 * 1024
REDUCE_BLOCK_BYTES = 1024 * 1024
PACK_COLS = 1024
PACK_ROW_UNIT = 32
MESH = pl.DeviceIdType.MESH

NN = (((1,), (0,)), ((), ()))
NT = (((1,), (1,)), ((), ()))
TN = (((0,), (0,)), ((), ()))


def _cparams(sem):
    return pltpu.CompilerParams(dimension_semantics=sem, vmem_limit_bytes=VMEM_LIMIT)


def _div_tile(n, cap, unit):
    best = None
    d = unit
    while d <= min(n, cap):
        if n % d == 0:
            best = d
        d += unit
    return n if best is None else best


def _mm(name, a, b, *, ta=False, tb=False, add=None, out_dtype=F32):
    if ta:
        kdim, m = a.shape
    else:
        m, kdim = a.shape
    if tb:
        n, kb = b.shape
    else:
        kb, n = b.shape
    assert kdim == kb, (name, a.shape, b.shape)
    tm = _div_tile(m, MM_TILE_M, 8 if not ta else LANE)
    tn = _div_tile(n, MM_TILE_N, LANE)
    tk = _div_tile(kdim, MM_TILE_K, LANE)
    nk = kdim // tk
    a_spec = pl.BlockSpec((tk, tm), lambda i, j, k: (k, i)) if ta else pl.BlockSpec((tm, tk), lambda i, j, k: (i, k))
    b_spec = pl.BlockSpec((tn, tk), lambda i, j, k: (j, k)) if tb else pl.BlockSpec((tk, tn), lambda i, j, k: (k, j))
    o_spec = pl.BlockSpec((tm, tn), lambda i, j, k: (i, j))
    has_add = add is not None
    dims = ((((0,) if ta else (1,)), ((1,) if tb else (0,))), ((), ()))

    def body(*refs):
        a_ref, b_ref = refs[0], refs[1]
        c_ref = refs[2] if has_add else None
        o_ref = refs[3] if has_add else refs[2]
        prod = lax.dot_general(a_ref[...].astype(BF16), b_ref[...].astype(BF16), dims, preferred_element_type=F32)
        if nk == 1:
            o_ref[...] = (prod + c_ref[...].astype(F32) if has_add else prod).astype(o_ref.dtype)
            return
        acc = refs[-1]
        k = pl.program_id(2)

        @pl.when(k == 0)
        def _():
            acc[...] = prod + c_ref[...].astype(F32) if has_add else prod

        @pl.when(jnp.logical_and(k > 0, k < nk - 1))
        def _():
            acc[...] += prod

        @pl.when(k == nk - 1)
        def _():
            o_ref[...] = (acc[...] + prod).astype(o_ref.dtype)

    ins = [a, b] + ([add] if has_add else [])
    specs = [a_spec, b_spec] + ([o_spec] if has_add else [])
    return pl.pallas_call(
        body, name=name, grid=(m // tm, n // tn, nk), in_specs=specs, out_specs=o_spec,
        out_shape=jax.ShapeDtypeStruct((m, n), out_dtype),
        scratch_shapes=[pltpu.VMEM((tm, tn), F32)] if nk > 1 else [],
        compiler_params=_cparams(("parallel", "parallel", "arbitrary")),
    )(*ins)


_IMAPS = {
    "row": lambda i: (i, 0),
    "col": lambda i: (0, i),
    "full": lambda i: (0, 0),
    "acc": lambda i: (0, 0),
}


def _tiled(name, fn, n, ins, outs, row_tile=None):
    if row_tile is not None:
        rows = next(a.shape[0] for a, _, k in ins if k == "row")
        n = rows // row_tile
        ins = [(a, (row_tile, bs[1]) if k == "row" else bs, k) for a, bs, k in ins]
        outs = [(s, dt, (row_tile, bs[1]) if k == "row" else bs, k) for s, dt, bs, k in outs]
    ni = len(ins)
    is_acc = [k == "acc" for *_, k in outs]

    def body(*refs):
        vals = fn(*[r[...].astype(F32) if r.dtype == BF16 else r[...] for r in refs[:ni]])
        orefs = refs[ni:]
        if any(is_acc):
            @pl.when(pl.program_id(0) == 0)
            def _():
                for r, a in zip(orefs, is_acc):
                    if a:
                        r[...] = jnp.zeros(r.shape, r.dtype)
        for r, v, a in zip(orefs, vals, is_acc):
            if a:
                r[...] += v.astype(r.dtype)
            else:
                r[...] = v.astype(r.dtype)

    res = pl.pallas_call(
        body, name=name, grid=(n,),
        in_specs=[pl.BlockSpec(bs, _IMAPS[k]) for _, bs, k in ins],
        out_specs=[pl.BlockSpec(bs, _IMAPS[k]) for _, _, bs, k in outs],
        out_shape=[jax.ShapeDtypeStruct(s, d) for s, d, _, _ in outs],
        compiler_params=_cparams(("arbitrary",)),
    )(*[a for a, _, _ in ins])
    return tuple(res)


def _vjp_of(fn, nin, diff):
    def g(*args):
        ins, cots = args[:nin], args[nin:]

        def f(*d):
            full = list(ins)
            for i, v in zip(diff, d):
                full[i] = v
            return fn(*full)

        outs, vjp = jax.vjp(f, *[ins[i] for i in diff])
        return vjp(tuple(c.astype(o.dtype) for c, o in zip(cots, outs)))
    return g


def _shift_rows(x, k, fill, up=False):
    n = x.shape[0]
    rows = lax.broadcasted_iota(jnp.int32, x.shape, 0)
    if up:
        return jnp.where(rows < n - k, pltpu.roll(x, n - k, 0), fill)
    return jnp.where(rows >= k, pltpu.roll(x, k, 0), fill)


@functools.partial(jax.custom_vjp, nondiff_argnums=(1,))
def _delay(x, k):
    return _shift_rows(x, k, 0.0)


def _delay_fwd(x, k):
    return _shift_rows(x, k, 0.0), None


def _delay_bwd(k, _, g):
    return (_shift_rows(g, k, 0.0, up=True),)


_delay.defvjp(_delay_fwd, _delay_bwd)


@functools.partial(jax.custom_vjp, nondiff_argnums=(1,))
def _lane_roll(x, s):
    return pltpu.roll(x, s, 1)


def _lane_roll_fwd(x, s):
    return pltpu.roll(x, s, 1), None


def _lane_roll_bwd(s, _, g):
    return (pltpu.roll(g, g.shape[1] - s, 1),)


_lane_roll.defvjp(_lane_roll_fwd, _lane_roll_bwd)


@jax.custom_vjp
def _bdot(x, w):
    return lax.dot_general(x.astype(BF16), w.astype(BF16), NN, preferred_element_type=F32)


def _bdot_fwd(x, w):
    return _bdot(x, w), (x, w)


def _bdot_bwd(res, g):
    x, w = res
    gb = g.astype(BF16)
    dx = lax.dot_general(gb, w.astype(BF16), NT, preferred_element_type=F32)
    dw = lax.dot_general(x.T.astype(BF16), gb, NN, preferred_element_type=F32)
    return dx, dw


_bdot.defvjp(_bdot_fwd, _bdot_bwd)


def _sigmoid(x):
    return 0.5 * (jnp.tanh(0.5 * x) + 1.0)


def _silu(x):
    return x * _sigmoid(x)


def _rms(x, g):
    return x * lax.rsqrt(jnp.mean(x * x, axis=-1, keepdims=True) + EPS) * g


def _causal_conv(x, w, b):
    kw = w.shape[0]
    tap = lax.broadcasted_iota(jnp.int32, w.shape, 0)
    y = b
    for k in range(kw):
        d = kw - 1 - k
        wk = jnp.sum(jnp.where(tap == k, w, 0.0), axis=0, keepdims=True)
        y = y + wk * (x if d == 0 else _delay(x, d))
    return y


def _rotate(x, cos_f, sin_a, sin_b):
    reps = x.shape[1] // LANE
    if reps > 1:
        cos_f, sin_a, sin_b = (jnp.tile(t, (1, reps)) for t in (cos_f, sin_a, sin_b))
    n = x.shape[1]
    half = QK_ROPE // 2
    return x * cos_f + _lane_roll(x, n - half) * sin_a + _lane_roll(x, half) * sin_b


def _softplus_neg(l):
    u = jnp.exp(-jnp.abs(l))
    log1p_u = jnp.where(u < 0.01, u * (1.0 - u * (0.5 - u * (1.0 / 3.0))), jnp.log(1.0 + u))
    return jnp.maximum(-l, 0.0) + log1p_u


def _f_ln(x, g, scale, shift):
    return (_rms(x, g) * (1.0 + scale) + shift,)


def _f_qkv(qkv, cos_f, sin_a, sin_b, qg, kvg):
    nq, nkv = qg.shape[1], kvg.shape[1]
    qn = _rms(qkv[:, :nq], qg)
    kvn = _rms(qkv[:, nq:nq + nkv], kvg)
    kr = _rotate(qkv[:, nq + nkv:], cos_f, sin_a, sin_b)
    return qn, kvn, kr


def _f_qkv_bwd(qkv, cos_f, sin_a, sin_b, qg, kvg, dqn, dkvn, dkr):
    nq, nkv = qg.shape[1], kvg.shape[1]
    _, vjp_q = jax.vjp(_rms, qkv[:, :nq], qg)
    _, vjp_kv = jax.vjp(_rms, qkv[:, nq:nq + nkv], kvg)
    _, vjp_r = jax.vjp(lambda t: _rotate(t, cos_f, sin_a, sin_b), qkv[:, nq + nkv:])
    dq_lat, dqg = vjp_q(dqn)
    dkv_lat, dkvg = vjp_kv(dkvn)
    (dkr_pre,) = vjp_r(dkr)
    return jnp.concatenate([dq_lat, dkv_lat, dkr_pre], axis=1), dqg, dkvg


QK_SCALE = 1.0 / math.sqrt(QK_NOPE + QK_ROPE)
LOG2_E = 1.4426950408889634
LN_2 = 0.6931471805599453


def _rotate_bf16(x, cos_f, sin_s):
    row = lax.broadcasted_iota(jnp.int32, (LANE, LANE), 0)
    col = lax.broadcasted_iota(jnp.int32, (LANE, LANE), 1)
    half = QK_ROPE // 2
    first, second = QK_NOPE, QK_NOPE + half
    swap = (((row >= first) & (row < second) & (col == row + half))
            | ((row >= second) & (row < second + half) & (col == row - half))).astype(BF16)
    partner = jnp.concatenate(
        [lax.dot_general(x[:, b * LANE:(b + 1) * LANE].astype(BF16), swap, NN, preferred_element_type=F32)
         for b in range(x.shape[1] // LANE)], axis=1)
    reps = x.shape[1] // LANE
    return x * jnp.tile(cos_f, (1, reps)) + partner * jnp.tile(sin_s, (1, reps))


def _f_rotq(q, cos_f, sin_a, sin_b):
    return (_rotate_bf16(q, cos_f, sin_a + sin_b) * (QK_SCALE * LOG2_E),)


def _f_rotq_bwd(cos_f, sin_a, sin_b, dq):
    return (_rotate_bf16(dq, cos_f, -(sin_a + sin_b)) * QK_SCALE,)


def _merge(g_rnn, g_mla, p_rnn, p_mla):
    return _sigmoid(g_rnn) * p_rnn + _sigmoid(g_mla) * p_mla


def _f_merge(g, p_rnn, p_mla):
    d = p_rnn.shape[1]
    return (_merge(g[:, :d], g[:, d:], p_rnn, p_mla),)


def _f_merge_bwd(g, p_rnn, p_mla, dm):
    d = p_rnn.shape[1]
    _, vjp = jax.vjp(_merge, g[:, :d], g[:, d:], p_rnn, p_mla)
    dg_rnn, dg_mla, dp_rnn, dp_mla = vjp(dm)
    return jnp.concatenate([dg_rnn, dg_mla], axis=1), dp_rnn, dp_mla


def _f_res_ln(x, o, gate, g2, scale, shift):
    x1 = x + gate * o
    return x1, _rms(x1, g2) * (1.0 + scale) + shift


def _f_ffn(u_gate, u_val, cw_gate, cw_val, cb_gate, cb_val):
    h = _causal_conv(u_gate, 0.5 * cw_gate, 0.5 * cb_gate)
    return ((h + h * jnp.tanh(h)) * _causal_conv(u_val, cw_val, cb_val),)


def _f_loss(x1, f, tgt, gate, fg):
    y = _rms(x1 + gate * f, fg)
    err = (y - tgt) * (y - tgt)
    return 0.5 * jnp.sum(jnp.mean(err, axis=-1, keepdims=True), axis=0, keepdims=True)


def _f_loss_and_grads(x1, f, tgt, gate, fg):
    loss, vjp = jax.vjp(lambda a, b, c, d: _f_loss(a, b, tgt, c, d), x1, f, gate, fg)
    dx1, df, dgate, dfg = vjp(jnp.ones((1, 1), F32))
    return dx1, df, jnp.broadcast_to(loss, (1, LANE)), dgate, dfg


@jax.custom_vjp
def _decay_and_gain(log_a):
    a = jnp.exp(log_a)
    return a, jnp.sqrt(-jnp.tanh(log_a) * (1.0 + a * a))


def _decay_and_gain_fwd(log_a):
    a, gain = _decay_and_gain(log_a)
    return (a, gain), (a, gain)


def _decay_and_gain_bwd(res, g):
    a, gain = res
    return (g[0] * a - g[1] * (a * a) / gain,)


_decay_and_gain.defvjp(_decay_and_gain_fwd, _decay_and_gain_bwd)


def _f_lru_coeffs(xr, cw, cb, wa, ba, wx, bx, lru, reset):
    xc = _causal_conv(xr, cw, cb)
    r = _sigmoid(_bdot(xc, wa) + ba)
    i = _sigmoid(_bdot(xc, wx) + bx)
    log_a = (-LRU_C) * r * _softplus_neg(lru)
    a, mult = _decay_and_gain(log_a)
    is_reset = reset > 0.5
    a = jnp.where(is_reset, 0.0, a)
    mult = jnp.where(is_reset, 1.0, mult)
    return a, mult * (i * xc)


SCAN_BLOCK = 64


def _scan(a, b, up=False):
    n = a.shape[0]
    blk = min(SCAN_BLOCK, n)
    pos = lax.broadcasted_iota(jnp.int32, a.shape, 0) % blk
    k = 1
    while k < blk:
        inside = (pos < blk - k) if up else (pos >= k)
        shift = n - k if up else k
        b = b + a * jnp.where(inside, pltpu.roll(b, shift, 0), 0.0)
        a = a * jnp.where(inside, pltpu.roll(a, shift, 0), 1.0)
        k *= 2
    blocks = range(n // blk)
    carry = jnp.zeros((1,) + a.shape[1:], a.dtype)
    out = [None] * len(blocks)
    for i in (reversed(blocks) if up else blocks):
        rows = slice(i * blk, (i + 1) * blk)
        out[i] = b[rows] + a[rows] * carry
        carry = out[i][:1] if up else out[i][blk - 1:]
    return jnp.concatenate(out, axis=0)


def _f_lru_fwd(xr, cw, cb, wa, ba, wx, bx, lru, reset):
    a, b = _f_lru_coeffs(xr, cw, cb, wa, ba, wx, bx, lru, reset)
    h = _scan(a, b)
    return h, h


def _f_lru_bwd(xr, cw, cb, wa, ba, wx, bx, lru, reset, h, dh):
    (a, _), vjp = jax.vjp(lambda *p: _f_lru_coeffs(*p, reset), xr, cw, cb, wa, ba, wx, bx, lru)
    g = _scan(_shift_rows(a, 1, 0.0, up=True), dh, up=True)
    return vjp((g * _shift_rows(h, 1, 0.0), g))


def _attn_tile(s):
    return ATTN_TILE if s >= 2 * ATTN_TILE else s // 2


def _keys(kv, kr):
    lane = lax.broadcasted_iota(jnp.int32, kv.shape, 1)
    return jnp.where(lane < QK_NOPE, kv, kr)


ATTN_HEADS_PER_STEP = 2


def _scores(q, kc, diagonal):
    s = lax.dot_general(q, kc, NT, preferred_element_type=F32)
    if not diagonal:
        return s
    rows = lax.broadcasted_iota(jnp.int32, s.shape, 0)
    cols = lax.broadcasted_iota(jnp.int32, s.shape, 1)
    return jnp.where(cols - (s.shape[1] - s.shape[0]) <= rows, s, -jnp.inf)


def _sub_blocks(t, diagonal):
    return ((0, t // 2, t // 2), (t // 2, t // 2, t)) if diagonal else ((0, t, t),)


def _causal_pairs(nb, k_major):
    if k_major:
        pairs = [(qb, kb) for kb in range(nb) for qb in range(kb, nb)]
    else:
        pairs = [(qb, kb) for qb in range(nb) for kb in range(qb + 1)]
    return jnp.array([p[0] for p in pairs], jnp.int32), jnp.array([p[1] for p in pairs], jnp.int32)


def _attn_fwd(q_pre, tables, kv, kr):
    s_len = q_pre.shape[0]
    t = _attn_tile(s_len)
    nb = s_len // t
    hp = ATTN_HEADS_PER_STEP
    wide = hp * HEAD_PAD
    q_tab, k_tab = _causal_pairs(nb, k_major=False)

    def body(qt, kt, qp_ref, cos_ref, sina_ref, sinb_ref, kv_ref, kr_ref, o_ref, lse_ref, q_ref, m_s, acc_s):
        pair = pl.program_id(1)
        qi, ki = qt[pair], kt[pair]

        @pl.when(ki == 0)
        def _():
            m_s[...] = jnp.full(m_s.shape, -jnp.inf, F32)
            acc_s[...] = jnp.zeros(acc_s.shape, F32)
            (rotated,) = _f_rotq(qp_ref[...].astype(F32), cos_ref[...], sina_ref[...], sinb_ref[...])
            q_ref[...] = rotated.astype(q_ref.dtype)

        def step(diagonal):
            for h in range(hp):
                lanes = slice(h * HEAD_PAD, (h + 1) * HEAD_PAD)
                for r0, nr, nk in _sub_blocks(t, diagonal):
                    rows = slice(r0, r0 + nr)
                    kvv = kv_ref[:nk, lanes]
                    s = _scores(q_ref[rows, lanes], _keys(kvv, kr_ref[:nk, :]), diagonal)
                    m_old = m_s[h, rows]
                    m_new = jnp.maximum(m_old, jnp.max(s, axis=-1, keepdims=True))
                    alpha = jnp.exp2(m_old - m_new)
                    p = jnp.exp2(s - jnp.tile(m_new, (1, s.shape[1] // HEAD_PAD)))
                    lane = lax.broadcasted_iota(jnp.int32, kvv.shape, 1)
                    ones_and_values = jnp.where(lane < QK_NOPE, jnp.ones_like(kvv), kvv)
                    acc_s[rows, lanes] = alpha * acc_s[rows, lanes] + lax.dot_general(
                        p.astype(BF16), ones_and_values, NN, preferred_element_type=F32)
                    m_s[h, rows] = m_new

        @pl.when(ki < qi)
        def _():
            step(False)

        @pl.when(ki == qi)
        def _():
            step(True)
            lane = lax.broadcasted_iota(jnp.int32, (t, HEAD_PAD), 1)
            outs = []
            for h in range(hp):
                acc = acc_s[:, h * HEAD_PAD:(h + 1) * HEAD_PAD]
                total = acc[:, :1]
                outs.append(acc / total)
                lse_ref[h] = m_s[h][:, :1] + jnp.log(total) * LOG2_E
            o_ref[...] = jnp.where(lane >= QK_NOPE, outs[0], pltpu.roll(outs[1], QK_NOPE, 1)).astype(o_ref.dtype)

    q_rows = lambda h, p, qt, kt: (qt[p], 0)
    grid_spec = pltpu.PrefetchScalarGridSpec(
        num_scalar_prefetch=2, grid=(N_HEADS // hp, q_tab.shape[0]),
        in_specs=[pl.BlockSpec((t, wide), lambda h, p, qt, kt: (qt[p], h)),
                  pl.BlockSpec((t, HEAD_PAD), q_rows), pl.BlockSpec((t, HEAD_PAD), q_rows),
                  pl.BlockSpec((t, HEAD_PAD), q_rows),
                  pl.BlockSpec((t, wide), lambda h, p, qt, kt: (kt[p], h)),
                  pl.BlockSpec((t, HEAD_PAD), lambda h, p, qt, kt: (kt[p], 0))],
        out_specs=[pl.BlockSpec((t, HEAD_PAD), lambda h, p, qt, kt: (qt[p], h)),
                   pl.BlockSpec((hp, t, 1), lambda h, p, qt, kt: (h, qt[p], 0)),
                   pl.BlockSpec((t, wide), lambda h, p, qt, kt: (qt[p], h))],
        scratch_shapes=[pltpu.VMEM((hp, t, HEAD_PAD), F32), pltpu.VMEM((t, wide), F32)])
    return pl.pallas_call(
        body, name="attn_fwd", grid_spec=grid_spec,
        out_shape=[jax.ShapeDtypeStruct((s_len, N_HEADS // hp * HEAD_PAD), BF16),
                   jax.ShapeDtypeStruct((N_HEADS, s_len, 1), F32),
                   jax.ShapeDtypeStruct((s_len, N_HEADS * HEAD_PAD), BF16)],
        compiler_params=_cparams(("arbitrary", "arbitrary")),
    )(q_tab, k_tab, q_pre, *tables, kv, kr)


def _attn_bwd(q, kv, kr, o, lse, do):
    s_len = q.shape[0]
    t = _attn_tile(s_len)
    nb = s_len // t
    hp = ATTN_HEADS_PER_STEP
    wide = hp * HEAD_PAD
    q_tab, k_tab = _causal_pairs(nb, k_major=True)

    def body(qt, kt, q_ref, kv_ref, kr_ref, o_ref, lse_ref, do_ref, dq_ref, dkv_ref, dkr_ref, dk_s, dv_s, dq_s):
        g, pair = pl.program_id(0), pl.program_id(1)
        qb, kb = qt[pair], kt[pair]

        @pl.when(jnp.logical_and(g == 0, pair == 0))
        def _():
            dkr_ref[...] = jnp.zeros(dkr_ref.shape, F32)

        @pl.when(pair == 0)
        def _():
            dq_s[...] = jnp.zeros(dq_s.shape, F32)

        @pl.when(qb == kb)
        def _():
            dk_s[...] = jnp.zeros(dk_s.shape, F32)
            dv_s[...] = jnp.zeros(dv_s.shape, F32)

        def step(diagonal):
            for h in range(hp):
                lanes = slice(h * HEAD_PAD, (h + 1) * HEAD_PAD)
                for r0, nr, nk in _sub_blocks(t, diagonal):
                    rows, keys = slice(r0, r0 + nr), slice(0, nk)
                    qv, kvv = q_ref[rows, lanes], kv_ref[keys, lanes]
                    pair_do = do_ref[rows, :].astype(F32)
                    lane = lax.broadcasted_iota(jnp.int32, pair_do.shape, 1)
                    mine = (lane >= QK_NOPE) if h == 0 else (lane < QK_NOPE)
                    placed = pair_do if h == 0 else pltpu.roll(pair_do, QK_NOPE, 1)
                    dov = jnp.where(lane >= QK_NOPE, placed, 0.0).astype(BF16)
                    delta = jnp.sum(jnp.where(mine, pair_do * o_ref[rows, :].astype(F32), 0.0), axis=-1, keepdims=True)
                    kc = _keys(kvv, kr_ref[keys, :])
                    p = jnp.exp2(_scores(qv, kc, diagonal) - lse_ref[h, rows])
                    dp = lax.dot_general(dov, kvv, NT, preferred_element_type=F32)
                    ds = p * (dp - delta)
                    dv_s[keys, lanes] += lax.dot_general(p.astype(BF16), dov, TN, preferred_element_type=F32)
                    dk_s[keys, lanes] += lax.dot_general(ds.astype(BF16), qv, TN, preferred_element_type=F32)
                    q_rows = pl.ds(pl.multiple_of(qb * t + r0, nr), nr)
                    dq_s[q_rows, lanes] += lax.dot_general(ds.astype(BF16), kc, NN, preferred_element_type=F32)

        @pl.when(qb > kb)
        def _():
            step(False)

        @pl.when(qb == kb)
        def _():
            step(True)

        @pl.when(qb == nb - 1)
        def _():
            lane = lax.broadcasted_iota(jnp.int32, (t, HEAD_PAD), 1)
            rows = pl.ds(pl.multiple_of(kb * t, t), t)
            for h in range(hp):
                lanes = slice(h * HEAD_PAD, (h + 1) * HEAD_PAD)
                dk = dk_s[:, lanes] * LN_2
                dkv_ref[:, lanes] = jnp.where(lane < QK_NOPE, dk, dv_s[:, lanes]).astype(dkv_ref.dtype)
                dkr_ref[rows, :] += jnp.where(lane >= QK_NOPE, dk, 0.0)

        @pl.when(pair == q_tab.shape[0] - 1)
        def _():
            dq_ref[...] = dq_s[...].astype(dq_ref.dtype)

    all_lanes = N_HEADS * HEAD_PAD
    qmap = lambda h, p, qt, kt: (qt[p], h)
    kmap = lambda h, p, qt, kt: (kt[p], h)
    grid_spec = pltpu.PrefetchScalarGridSpec(
        num_scalar_prefetch=2, grid=(N_HEADS // hp, q_tab.shape[0]),
        in_specs=[pl.BlockSpec((t, wide), qmap),
                  pl.BlockSpec((t, wide), kmap),
                  pl.BlockSpec((t, HEAD_PAD), lambda h, p, qt, kt: (kt[p], 0)),
                  pl.BlockSpec((t, HEAD_PAD), qmap),
                  pl.BlockSpec((hp, t, 1), lambda h, p, qt, kt: (h, qt[p], 0)),
                  pl.BlockSpec((t, HEAD_PAD), qmap)],
        out_specs=[pl.BlockSpec((s_len, wide), lambda h, p, qt, kt: (0, h)),
                   pl.BlockSpec((t, wide), kmap),
                   pl.BlockSpec((s_len, HEAD_PAD), lambda h, p, qt, kt: (0, 0))],
        scratch_shapes=[pltpu.VMEM((t, wide), F32), pltpu.VMEM((t, wide), F32), pltpu.VMEM((s_len, wide), F32)])
    return pl.pallas_call(
        body, name="attn_bwd", grid_spec=grid_spec,
        out_shape=[jax.ShapeDtypeStruct((s_len, all_lanes), BF16),
                   jax.ShapeDtypeStruct((s_len, all_lanes), BF16),
                   jax.ShapeDtypeStruct((s_len, HEAD_PAD), F32)],
        compiler_params=_cparams(("arbitrary", "arbitrary")),
    )(q_tab, k_tab, q, kv, kr, o, lse, do)


def _adamw(name, w, g, m, v):
    rows, cols = w.shape
    tr = _div_tile(rows, max(SUBLANES, ADAM_BLOCK_BYTES // (4 * cols)), SUBLANES)

    def body(w_ref, g_ref, m_ref, v_ref, d_ref, nm_ref, nv_ref):
        gv = g_ref[...]
        nm = ADAM_B1 * m_ref[...] + (1.0 - ADAM_B1) * gv
        nv = ADAM_B2 * v_ref[...] + (1.0 - ADAM_B2) * jnp.square(gv)
        m_hat = nm / (1.0 - ADAM_B1 ** ADAM_STEP)
        v_hat = nv / (1.0 - ADAM_B2 ** ADAM_STEP)
        d_ref[...] = -ADAM_LR * (m_hat / (jnp.sqrt(v_hat) + ADAM_EPS) + ADAM_WD * w_ref[...])
        nm_ref[...] = nm
        nv_ref[...] = nv

    spec = pl.BlockSpec((tr, cols), lambda i: (i, 0))
    return pl.pallas_call(
        body, name=name, grid=(rows // tr,), in_specs=[spec] * 4, out_specs=[spec] * 3,
        out_shape=[jax.ShapeDtypeStruct((rows, cols), F32)] * 3,
        compiler_params=_cparams(("parallel",)),
    )(w, g, m, v)


ALL7 = (1, 2, 3, 4, 5, 6, 7)
CHIPS = (2, 4, 6)


def _all_gather(name, src, masks):
    bits = 0
    for m in masks:
        bits |= m
    nslots = {7: 8, 6: 4}[bits]
    nm = len(masks)

    def slot_of(x, y, c):
        return {7: 4 * x + 2 * y + c, 6: 2 * x + y}[bits]

    def body(src_ref, out_ref, send_sems, recv_sems, local_sem):
        x, y, c = lax.axis_index("x"), lax.axis_index("y"), lax.axis_index("c")
        mine = slot_of(x, y, c)
        own = pltpu.make_async_copy(src_ref, out_ref.at[mine], local_sem)
        own.start()
        copies = []
        for i, m in enumerate(masks):
            peer = _peer(x, y, c, m)
            copies.append((
                pltpu.make_async_remote_copy(
                    src_ref=src_ref, dst_ref=out_ref.at[mine], send_sem=send_sems.at[i], recv_sem=recv_sems.at[i],
                    device_id=peer, device_id_type=MESH),
                pltpu.make_async_remote_copy(
                    src_ref=src_ref, dst_ref=out_ref.at[slot_of(*peer)], send_sem=send_sems.at[i],
                    recv_sem=recv_sems.at[i], device_id=peer, device_id_type=MESH)))
        for send, _ in copies:
            send.start()
        for _, arrival in copies:
            arrival.wait_recv()
        for send, _ in copies:
            send.wait_send()
        own.wait()

    return pl.pallas_call(
        body, name=name,
        in_specs=[pl.BlockSpec(memory_space=pl.ANY)], out_specs=pl.BlockSpec(memory_space=pl.ANY),
        out_shape=jax.ShapeDtypeStruct((nslots,) + tuple(src.shape), src.dtype),
        scratch_shapes=[pltpu.SemaphoreType.DMA((nm,)), pltpu.SemaphoreType.DMA((nm,)), pltpu.SemaphoreType.DMA],
    )(src)


def _peer(x, y, c, m):
    return (1 - x if m & 4 else x, 1 - y if m & 2 else y, 1 - c if m & 1 else c)


def _comm_call(name, emit, srcs, out_shapes, n_sems, in_place=False):
    n = len(srcs)

    def body(*refs):
        src_refs, out_refs = refs[:n], refs[n:n + len(out_shapes)]
        send_sems, recv_sems = refs[-2], refs[-1]

        def copy(src, dst, i, peer):
            return pltpu.make_async_remote_copy(src_ref=src, dst_ref=dst, send_sem=send_sems.at[i],
                                                recv_sem=recv_sems.at[i], device_id=peer, device_id_type=MESH)

        emit(lax.axis_index("x"), lax.axis_index("y"), lax.axis_index("c"), src_refs, out_refs, copy)

    hbm = pl.BlockSpec(memory_space=pl.ANY)
    return pl.pallas_call(
        body, name=name, in_specs=[hbm] * n, out_specs=[hbm] * len(out_shapes), out_shape=out_shapes,
        scratch_shapes=[pltpu.SemaphoreType.DMA((n_sems,)), pltpu.SemaphoreType.DMA((n_sems,))],
        input_output_aliases={i: i for i in range(n)} if in_place else {},
    )(*srcs)


HBM_SPEC = pl.BlockSpec(memory_space=pltpu.HBM)
SEM_SPEC = pl.BlockSpec(memory_space=pltpu.SEMAPHORE)
DATAFLOW = pltpu.SideEffectType.DATAFLOW_SIDE_EFFECTING


def _chip_copies(srcs, lands, send_sems, recv_sems, mode):
    x, y, c = lax.axis_index("x"), lax.axis_index("y"), lax.axis_index("c")
    chip = 2 * x + y
    sends, arrivals = [], []
    if mode == "pair":
        for k, (s, l) in enumerate(zip(srcs, lands)):
            for group in (sends, arrivals):
                group.append(pltpu.make_async_remote_copy(
                    src_ref=s.at[:, 1 - c], dst_ref=l, send_sem=send_sems.at[3 * k], recv_sem=recv_sems.at[3 * k],
                    device_id=(x, y, 1 - c), device_id_type=MESH))
        return sends, arrivals
    for j, m in enumerate(CHIPS):
        px, py, _ = _peer(x, y, c, m)
        theirs = 2 * px + py
        for k, (s, l) in enumerate(zip(srcs, lands)):
            if mode == "gather":
                src, dst, got = s.at[c], l.at[chip, c], l.at[theirs, c]
            else:
                src, dst, got = s.at[theirs], l.at[chip], l.at[theirs]
            for to, group in ((dst, sends), (got, arrivals)):
                group.append(pltpu.make_async_remote_copy(
                    src_ref=src, dst_ref=to, send_sem=send_sems.at[3 * k + j], recv_sem=recv_sems.at[3 * k + j],
                    device_id=(px, py, c), device_id_type=MESH))
    return sends, arrivals


def _split_start(name, srcs, land_shapes, mode, after):
    n = len(srcs)

    def body(*refs):
        sends, _ = _chip_copies(refs[:n], refs[n:2 * n], refs[2 * n + 1], refs[2 * n + 2], mode)
        for cp in sends:
            cp.start()
        token = refs[-1]
        token[...] = jnp.zeros(token.shape, token.dtype)

    hbm = lambda a: pltpu.with_memory_space_constraint(a, pltpu.HBM)
    lands = [hbm(lax.empty(s.shape, s.dtype)) for s in land_shapes]
    bufs = [pltpu.HBM(a.shape, a.dtype) for a in list(srcs) + lands]
    res = pl.pallas_call(
        body, name=name,
        out_shape=(pltpu.SemaphoreType.DMA((3 * n,)), pltpu.SemaphoreType.DMA((3 * n,)), *bufs,
                   jax.ShapeDtypeStruct((SUBLANES, LANE), F32)),
        in_specs=[HBM_SPEC] * (2 * n) + [pl.BlockSpec(memory_space=pl.ANY)],
        out_specs=[SEM_SPEC, SEM_SPEC] + [HBM_SPEC] * (2 * n) + [pl.BlockSpec(memory_space=pltpu.VMEM)],
        input_output_aliases={i: 2 + i for i in range(2 * n)},
        compiler_params=pltpu.CompilerParams(has_side_effects=DATAFLOW),
    )(*[hbm(s) for s in srcs], *lands, after)
    return res[0], res[1], res[2:2 + n], res[2 + n:2 + 2 * n], res[-1]


def _split_wait(name, send_sems, recv_sems, srcs, lands, mode, after):
    n = len(srcs)

    def body(*refs):
        sends, arrivals = _chip_copies(refs[:n], refs[n:2 * n], refs[2 * n], refs[2 * n + 1], mode)
        for cp in sends:
            cp.wait_send()
        for cp in arrivals:
            cp.wait_recv()

    res = pl.pallas_call(
        body, name=name,
        out_shape=tuple(pltpu.HBM(a.shape, a.dtype) for a in list(srcs) + list(lands)),
        in_specs=[HBM_SPEC] * (2 * n) + [SEM_SPEC, SEM_SPEC, pl.BlockSpec(memory_space=pl.ANY)],
        out_specs=[HBM_SPEC] * (2 * n),
        input_output_aliases={i: i for i in range(2 * n)},
        compiler_params=pltpu.CompilerParams(has_side_effects=DATAFLOW),
    )(*srcs, *lands, send_sems, recv_sems, after)
    return res[n:]


def _relay_sibling(lands):
    def emit(x, y, c, srcs, outs, copy):
        sib = (x, y, 1 - c)
        sends, arrivals = [], []
        for j, m in enumerate(CHIPS):
            px, py, _ = _peer(x, y, c, m)
            theirs = 2 * px + py
            for k, (s, o) in enumerate(zip(srcs, outs)):
                sends.append(copy(s.at[theirs, c], o.at[theirs, c], 3 * k + j, sib))
                arrivals.append(copy(s.at[theirs, c], o.at[theirs, 1 - c], 3 * k + j, sib))
        for cp in sends:
            cp.start()
        for cp in arrivals:
            cp.wait_recv()
        for cp in sends:
            cp.wait_send()

    shapes = [jax.ShapeDtypeStruct(l.shape, l.dtype) for l in lands]
    return _comm_call("relay_weights", emit, lands, shapes, 3 * len(lands), in_place=True)


def _gather_weights(halves):
    n = len(halves)

    def emit(x, y, c, srcs, outs, copy):
        chip = 2 * x + y
        sib = (x, y, 1 - c)
        first, relay, landed, relayed = [], [], [], []
        for j, m in enumerate(CHIPS):
            px, py, _ = _peer(x, y, c, m)
            theirs = 2 * px + py
            for k in range(n):
                i = 6 * k + j
                first.append(copy(srcs[k].at[c], outs[k].at[chip, c], i, (px, py, c)))
                landed.append(copy(srcs[k].at[c], outs[k].at[theirs, c], i, (px, py, c)))
                relay.append(copy(outs[k].at[theirs, c], outs[k].at[theirs, c], i + 3, sib))
                relayed.append(copy(outs[k].at[theirs, 1 - c], outs[k].at[theirs, 1 - c], i + 3, sib))
        for cp in first:
            cp.start()
        for arrival, onward in zip(landed, relay):
            arrival.wait_recv()
            onward.start()
        for arrival in relayed:
            arrival.wait_recv()
        for cp in first + relay:
            cp.wait_send()

    shapes = [jax.ShapeDtypeStruct((4,) + h.shape, h.dtype) for h in halves]
    return _comm_call("gather_weights", emit, halves, shapes, 6 * n)


def _pair_exchange(name, chunks):
    def emit(x, y, c, srcs, outs, copy):
        sib = (x, y, 1 - c)
        sends = [copy(s.at[:, 1 - c], o, k, sib) for k, (s, o) in enumerate(zip(srcs, outs))]
        for cp in sends:
            cp.start()
        for cp in sends:
            cp.wait_recv()
        for cp in sends:
            cp.wait_send()

    shapes = [jax.ShapeDtypeStruct((4,) + g.shape[2:], g.dtype) for g in chunks]
    return _comm_call(name, emit, chunks, shapes, len(chunks))


def _share_sibling(name, parts):
    def emit(x, y, c, srcs, outs, copy):
        sib = (x, y, 1 - c)
        sends = [copy(s, o.at[c], k, sib) for k, (s, o) in enumerate(zip(srcs, outs))]
        arrivals = [copy(s, o.at[1 - c], k, sib) for k, (s, o) in enumerate(zip(srcs, outs))]
        for cp in sends:
            cp.start()
        for cp in arrivals:
            cp.wait_recv()
        for cp in sends:
            cp.wait_send()

    shapes = [jax.ShapeDtypeStruct((2,) + p.shape, p.dtype) for p in parts]
    return _comm_call(name, emit, parts, shapes, len(parts))


def _reduce_pair(name, chunk, from_sibling, core):
    n, _, h, cols = chunk.shape
    rt = _div_tile(h, max(2 * SUBLANES, REDUCE_BLOCK_BYTES // (4 * cols)), 2 * SUBLANES)

    def body(core_ref, a_ref, b_ref, o_ref):
        o_ref[...] = (a_ref[...] + b_ref[...]).astype(o_ref.dtype)

    grid_spec = pltpu.PrefetchScalarGridSpec(
        num_scalar_prefetch=1, grid=(n, h // rt),
        in_specs=[pl.BlockSpec((None, None, rt, cols), lambda s, i, core_ref: (s, core_ref[0], i, 0)),
                  pl.BlockSpec((None, rt, cols), lambda s, i, core_ref: (s, i, 0))],
        out_specs=pl.BlockSpec((None, rt, cols), lambda s, i, core_ref: (s, i, 0)))
    return pl.pallas_call(
        body, name=name, grid_spec=grid_spec, out_shape=jax.ShapeDtypeStruct((n, h, cols), BF16),
        compiler_params=_cparams(("parallel", "parallel")),
    )(core, chunk, from_sibling)


def _reduce_quad(name, q, after=None):
    _, h, cols = q.shape
    rt = _div_tile(h, max(2 * SUBLANES, REDUCE_BLOCK_BYTES // (4 * cols)), 2 * SUBLANES)

    def body(q_ref, *rest):
        v = q_ref[...].astype(F32)
        rest[-1][...] = ((v[0] + v[1]) + v[2]) + v[3]

    held = [] if after is None else [after]
    return pl.pallas_call(
        body, name=name, grid=(h // rt,),
        in_specs=[pl.BlockSpec((4, rt, cols), lambda i: (0, i, 0))] + [pl.BlockSpec(memory_space=pl.ANY)] * len(held),
        out_specs=pl.BlockSpec((rt, cols), lambda i: (i, 0)),
        out_shape=jax.ShapeDtypeStruct((h, cols), F32),
        compiler_params=_cparams(("parallel",)),
    )(q, *held)


def _unshard(seg, kind):
    n, r, c = seg.shape
    if kind == "col":
        return seg.transpose(1, 0, 2).reshape(r, n * c)
    return seg.reshape(n * r, c)


def _pad_rows(flat, rows):
    n, ln = flat.shape
    return jnp.pad(flat, ((0, 0), (0, rows * PACK_COLS - ln))).reshape(n, rows, PACK_COLS)


def _block_diag_pairs(w):
    n2, bs, _ = w.shape
    eye = jnp.eye(2, dtype=w.dtype)
    z = w.reshape(n2 // 2, 2, bs, 1, bs) * eye[None, :, None, :, None]
    return z.reshape(n2 // 2, 2 * bs, 2 * bs).transpose(1, 0, 2).reshape(2 * bs, n2 * bs)


def _block_diag_pairs_t(d, bs=64):
    n = d.shape[1] // (2 * bs)
    z = d.reshape(2 * bs, n, 2 * bs).transpose(1, 0, 2).reshape(n, 2, bs, 2, bs)
    return jnp.stack([z[:, 0, :, 0, :], z[:, 1, :, 1, :]], axis=1).reshape(2 * n, bs, bs)


BIG = (("w_in", "col"), ("w_uq", "col"), ("w_ukv", "col"), ("w_proj_rnn", "row"), ("w_proj_mla", "row"),
       ("w_out", "row"), ("w_up", "col"), ("w_down", "row"))
FIRST_USED = ("w_in", "w_uq", "w_ukv")
CONVS = (("conv_w", "col"), ("ffn_conv_w", "col"))
SMALL = ("b_ada", "norm1_g", "conv_b", "w_gate_a", "b_gate_a", "w_gate_x", "b_gate_x", "lru_param",
         "q_norm_g", "kv_norm_g", "norm2_g", "ffn_conv_b", "final_g")
WEIGHTS = ("w_ada", "b_ada", "norm1_g", "w_in", "conv_w", "conv_b", "w_gate_a", "b_gate_a", "w_gate_x",
           "b_gate_x", "lru_param", "q_norm_g", "w_uq", "kv_norm_g", "w_ukv", "w_proj_rnn", "w_proj_mla",
           "w_out", "norm2_g", "w_up", "ffn_conv_w", "ffn_conv_b", "w_down", "final_g")


def _step(x, c, positions, w, m_in, v_in, loss_target):
    s_len, d = x.shape[1], x.shape[2]
    x2d = x[0]
    tgt = loss_target[0]
    xi, yi, ci = lax.axis_index("x"), lax.axis_index("y"), lax.axis_index("c")
    chip = 2 * xi + yi
    me = 2 * chip + ci
    tile = min(ROW_TILE, s_len)
    nt = s_len // tile

    local2d = {k: w[k][0] for k, _ in BIG + CONVS}
    kinds = dict(BIG)
    halves_bf = {k: local2d[k].astype(BF16).reshape(2, local2d[k].shape[0] // 2, local2d[k].shape[1]) for k, _ in BIG}
    first_names = [k for k, _ in BIG if k in FIRST_USED]
    later_names = [k for k, _ in BIG if k not in FIRST_USED]
    full = {}

    def assemble(k, g):
        g = lax.dynamic_update_index_in_dim(g, halves_bf[k][None], chip, 0).reshape((4,) + local2d[k].shape)
        if k == "w_up":
            full["w_up_gate"], full["w_up_val"] = _unshard(g[:2], kinds[k]), _unshard(g[2:], kinds[k])
        else:
            full[k] = _unshard(g, kinds[k])

    first_got = _gather_weights([halves_bf[k] for k in first_names])
    for k, g in zip(first_names, first_got):
        assemble(k, g)
    conv_flat = jnp.concatenate([local2d[k].reshape(-1) for k, _ in CONVS])
    conv_rows = -(-conv_flat.shape[0] // PACK_COLS)
    conv_all = _all_gather("gather_conv_w", _pad_rows(conv_flat[None], conv_rows)[0], CHIPS)
    conv_all = conv_all.reshape(4, -1)
    off = 0
    for k, kind in CONVS:
        r, cc = local2d[k].shape
        full[k] = _unshard(conv_all[:, off:off + r * cc].reshape(4, r, cc), kind)
        off += r * cc

    d_rnn = w["conv_b"].shape[1]
    n_q, n_kv = w["q_norm_g"].shape[1], w["kv_norm_g"].shape[1]
    w_in = full["w_in"]
    o1, o2, o3 = d_rnn + n_q, d_rnn + n_q + n_kv, d_rnn + n_q + n_kv + QK_ROPE
    w_rnn = w_in[:, :d_rnn]
    zpad = lambda n: jnp.zeros((d, n), BF16)
    w_qkv = jnp.concatenate([w_in[:, d_rnn:o2], zpad(QK_NOPE), w_in[:, o2:o3], zpad(LANE - QK_NOPE - QK_ROPE)], axis=1)
    w_g = w_in[:, o3:]
    hd = QK_NOPE + QK_ROPE
    w_uq = jnp.pad(full["w_uq"].reshape(n_q, N_HEADS, hd), ((0, 0), (0, 0), (0, HEAD_PAD - hd))).reshape(n_q, -1)
    w_ukv = full["w_ukv"]
    v_head = w_ukv.shape[1] // N_HEADS - QK_NOPE
    d_ff = w["ffn_conv_b"].shape[1] // 2
    ffn_cw_gate, ffn_cw_val = full["ffn_conv_w"][:, :d_ff], full["ffn_conv_w"][:, d_ff:]
    ffn_cb_gate, ffn_cb_val = w["ffn_conv_b"][:, :d_ff], w["ffn_conv_b"][:, d_ff:]
    conv_w, conv_b = full["conv_w"], w["conv_b"]
    wa_bd = _block_diag_pairs(w["w_gate_a"][0])
    wx_bd = _block_diag_pairs(w["w_gate_x"][0])

    c_all = _all_gather("gather_c", c, ALL7).reshape(8, d)
    c_rows = LANE
    (c_act,) = _tiled("silu_c", lambda v: (_silu(v),), 1, [(jnp.pad(c_all, ((0, c_rows - 8), (0, 0))), (c_rows, d), "full")],
                      [((c_rows, d), F32, (c_rows, d), "full")])
    w_ada = w["w_ada"][0]
    n_mod = w_ada.shape[1]
    b_loc = lax.dynamic_slice_in_dim(w["b_ada"], chip * n_mod, n_mod, axis=1)
    mod_loc = _mm("ada_fwd", c_act, w_ada, add=jnp.broadcast_to(b_loc, (c_rows, n_mod)))
    mod_all = _all_gather("gather_mod", mod_loc[:8], CHIPS)
    mod = lax.dynamic_index_in_dim(mod_all, me, 1, keepdims=False).reshape(1, -1)
    shift1, scale1, gate1, shift2, scale2, gate2 = [mod[:, i * d:(i + 1) * d] for i in range(6)]

    small_done = (mod[:, :1] + conv_all[:1, :1] + first_got[0][0, 0, :1, :1].astype(F32))
    later_flight = _split_start(
        "gather_later_start", [halves_bf[k] for k in later_names],
        [jax.ShapeDtypeStruct((4,) + halves_bf[k].shape, BF16) for k in later_names], "gather", after=small_done)

    half = QK_ROPE // 2
    inv_freq = ROPE_THETA ** (-jnp.arange(half, dtype=F32) / half)
    ang = positions[0].astype(F32)[:, None] * inv_freq
    cos, sin = jnp.cos(ang), jnp.sin(ang)
    one, zero = jnp.ones((s_len, QK_NOPE), F32), jnp.zeros((s_len, half), F32)
    tail = jnp.zeros((s_len, LANE - QK_NOPE - QK_ROPE), F32)
    cos_f = jnp.concatenate([one, cos, cos, tail + 1.0], axis=1)
    sin_a = jnp.concatenate([one * 0.0, -sin, zero, tail], axis=1)
    sin_b = jnp.concatenate([one * 0.0, zero, sin, tail], axis=1)
    reset = (positions[0] == 0).astype(F32)[:, None]
    tabs = [(cos_f, (tile, LANE), "row"), (sin_a, (tile, LANE), "row"), (sin_b, (tile, LANE), "row")]

    def rowspec(a):
        return (a, (tile, a.shape[1]), "row")

    def full2(a):
        return (a, a.shape, "full")

    def rowout(cols, dt):
        return ((s_len, cols), dt, (tile, cols), "row")

    def accout(a):
        return (a.shape, F32, a.shape, "acc")

    norm1_g = w["norm1_g"] + later_flight[4][:1, :1]
    norm2_g, final_g = w["norm2_g"], w["final_g"].reshape(1, d)
    ln1_in = [rowspec(x2d), full2(norm1_g), full2(scale1), full2(shift1)]
    big_tile = min(WIDE_ROW_TILE, s_len)
    (h1,) = _tiled("ln1", _f_ln, nt, ln1_in, [rowout(d, BF16)], row_tile=big_tile)
    x_rnn = _mm("in_rnn", h1, w_rnn, out_dtype=BF16)
    qkv = _mm("in_qkv", h1, w_qkv)
    gates = _mm("in_gates", h1, w_g, out_dtype=BF16)

    ct = LANE
    n_ct = d_rnn // ct
    colspec = lambda a, width=ct: (a, (a.shape[0], width), "col")
    lru_in = [colspec(x_rnn), colspec(conv_w), colspec(conv_b), colspec(wa_bd), colspec(w["b_gate_a"]),
              colspec(wx_bd), colspec(w["b_gate_x"]), colspec(w["lru_param"]), full2(reset)]
    y_rnn, h_rnn = _tiled("lru_fwd", _f_lru_fwd, n_ct, lru_in,
                          [((s_len, d_rnn), BF16, (s_len, ct), "col"), ((s_len, d_rnn), F32, (s_len, ct), "col")])

    qkv_in = [rowspec(qkv)] + tabs + [full2(w["q_norm_g"]), full2(w["kv_norm_g"])]
    qn, kvn, kr = _tiled("qkv_norm", _f_qkv, nt, qkv_in, [rowout(n_q, BF16), rowout(n_kv, BF16), rowout(LANE, BF16)],
                         row_tile=big_tile)
    q_pre = _mm("up_q", qn, w_uq, out_dtype=BF16)
    kv = _mm("up_kv", kvn, w_ukv, out_dtype=BF16)
    o_mla, lse, q_cat = _attn_fwd(q_pre, (cos_f, sin_a, sin_b), kv, kr)

    send_sems, recv_sems, flown, landed, _ = later_flight
    landed = _split_wait("gather_later_wait", send_sems, recv_sems, flown, landed, "gather", after=o_mla)
    for k, g in zip(later_names, _relay_sibling(landed)):
        assemble(k, g)
    w_pr = full["w_proj_rnn"]
    assert ATTN_HEADS_PER_STEP == 2 and 2 * v_head == HEAD_PAD
    swap_pairs = lambda a: a.reshape(N_HEADS // 2, 2, v_head, d)[:, ::-1].reshape(-1, d)
    w_pm = swap_pairs(full["w_proj_mla"])
    w_out = full["w_out"]
    w_up_gate, w_up_val = full["w_up_gate"], full["w_up_val"]
    w_down = full["w_down"]

    p_rnn = _mm("proj_rnn", y_rnn, w_pr, out_dtype=BF16)
    p_mla = _mm("proj_mla", o_mla, w_pm, out_dtype=BF16)
    merge_in = [rowspec(gates), rowspec(p_rnn), rowspec(p_mla)]
    (merged,) = _tiled("merge", _f_merge, nt, merge_in, [rowout(d, BF16)], row_tile=big_tile)
    o_tok = _mm("out_proj", merged, w_out)
    res_in = [rowspec(x2d), rowspec(o_tok), full2(gate1), full2(norm2_g), full2(scale2), full2(shift2)]
    x1, h2 = _tiled("res_ln2", _f_res_ln, nt, res_in, [rowout(d, F32), rowout(d, BF16)], row_tile=big_tile)
    u_gate = _mm("ffn_up_gate", h2, w_up_gate, out_dtype=BF16)
    u_val = _mm("ffn_up_val", h2, w_up_val, out_dtype=BF16)
    n_ft = d_ff // LANE
    ffn_in = [colspec(a) for a in (u_gate, u_val, ffn_cw_gate, ffn_cw_val, ffn_cb_gate, ffn_cb_val)]
    (act,) = _tiled("ffn_conv", _f_ffn, n_ft, ffn_in, [((s_len, d_ff), BF16, (s_len, LANE), "col")])
    f_tok = _mm("ffn_down", act, w_down)

    loss_in = [rowspec(x1), rowspec(f_tok), rowspec(tgt), full2(gate2), full2(final_g)]
    dx1, df, loss_row, d_gate2, d_final_g = _tiled(
        "loss", _f_loss_and_grads, nt, loss_in,
        [rowout(d, F32), rowout(d, BF16), ((1, LANE), F32, (1, LANE), "acc"), accout(gate2), accout(final_g)],
        row_tile=big_tile)
    loss = lax.psum(loss_row[0, 0], ("x", "y", "c"))

    d_act = _mm("ffn_down_dx", df, w_down, tb=True, out_dtype=BF16)
    g_w_down = _mm("ffn_down_dw", act, df, ta=True)
    taps = ffn_cw_gate.shape[0]
    du_gate, du_val, g_cw_gate, g_cw_val, g_cb_gate, g_cb_val = _tiled(
        "ffn_conv_bwd", _vjp_of(_f_ffn, 6, (0, 1, 2, 3, 4, 5)), n_ft, ffn_in + [colspec(d_act)],
        [((s_len, d_ff), BF16, (s_len, LANE), "col")] * 2 + [((taps, d_ff), F32, (taps, LANE), "col")] * 2
        + [((1, d_ff), F32, (1, LANE), "col")] * 2)
    dh2 = _mm("ffn_up_gate_dx", du_gate, w_up_gate, tb=True)
    dh2 = _mm("ffn_up_val_dx", du_val, w_up_val, tb=True, add=dh2, out_dtype=BF16)
    g_w_up_halves = [_mm("ffn_up_gate_dw", h2, du_gate, ta=True), _mm("ffn_up_val_dw", h2, du_val, ta=True)]
    g_ffn_cw = jnp.concatenate([g_cw_gate, g_cw_val], axis=1)
    g_ffn_cb = jnp.concatenate([g_cb_gate, g_cb_val], axis=1)

    def chunked(k, gk):
        r, cc = local2d[k].shape
        if kinds[k] == "col":
            gk = gk.reshape(r, 4, cc).transpose(1, 0, 2)
        return gk.reshape(4, 2, r // 2, cc)

    r_up, c_up = local2d["w_up"].shape
    up_chunks = jnp.concatenate([g.reshape(r_up, 2, c_up).transpose(1, 0, 2) for g in g_w_up_halves], axis=0)
    ffn_chunks = {"w_up": up_chunks.reshape(4, 2, r_up // 2, c_up), "w_down": chunked("w_down", g_w_down)}
    ffn_names = [k for k in later_names if k in ffn_chunks]
    ffn_pair_flight = _split_start(
        "reduce_pair_ffn_start", [ffn_chunks[k] for k in ffn_names],
        [jax.ShapeDtypeStruct((4,) + ffn_chunks[k].shape[2:], F32) for k in ffn_names], "pair",
        after=ffn_chunks[ffn_names[-1]])
    gate1_held = gate1 + ffn_pair_flight[4][:1, :1]

    res_bwd = _vjp_of(_f_res_ln, 6, (0, 1, 2, 3, 4, 5))
    dx_res, do_tok, d_gate1, g_norm2, d_scale2, d_shift2 = _tiled(
        "res_ln2_bwd", res_bwd, nt, res_in[:2] + [full2(gate1_held)] + res_in[3:] + [rowspec(dx1), rowspec(dh2)],
        [rowout(d, F32), rowout(d, BF16), accout(gate1), accout(norm2_g), accout(scale2), accout(shift2)],
        row_tile=big_tile)
    d_merged = _mm("out_proj_dx", do_tok, w_out, tb=True, out_dtype=BF16)
    g_w_out = _mm("out_proj_dw", merged, do_tok, ta=True)
    d_gates, dp_rnn, dp_mla = _tiled(
        "merge_bwd", _f_merge_bwd, nt, merge_in + [rowspec(d_merged)],
        [rowout(gates.shape[1], BF16), rowout(d, BF16), rowout(d, BF16)], row_tile=big_tile)
    dy_rnn = _mm("proj_rnn_dx", dp_rnn, w_pr, tb=True, out_dtype=BF16)
    g_w_pr = _mm("proj_rnn_dw", y_rnn, dp_rnn, ta=True)
    do_mla = _mm("proj_mla_dx", dp_mla, w_pm, tb=True, out_dtype=BF16)
    g_w_pm = _mm("proj_mla_dw", o_mla, dp_mla, ta=True)

    core = ci.astype(jnp.int32).reshape(1)

    def pair_sums(tag, names, chunks):
        received = _pair_exchange("reduce_pair_exchange_" + tag, chunks)
        return [_reduce_pair("reduce_pair_" + k, ck, got, core) for k, ck, got in zip(names, chunks, received)]

    g_later = {"w_proj_rnn": g_w_pr, "w_proj_mla": swap_pairs(g_w_pm), "w_out": g_w_out}
    send_sems, recv_sems, flown, landed, _ = ffn_pair_flight
    ffn_received = _split_wait("reduce_pair_ffn_wait", send_sems, recv_sems, flown, landed, "pair", after=g_w_pm)
    sums = {k: _reduce_pair("reduce_pair_" + k, ffn_chunks[k], got, core) for k, got in zip(ffn_names, ffn_received)}
    other_names = [k for k in later_names if k not in ffn_chunks]
    sums.update(zip(other_names, pair_sums("ready", other_names, [chunked(k, g_later[k]) for k in other_names])))
    sums_ready = [sums[k] for k in later_names]
    ready_flight = _split_start(
        "reduce_ready_start", sums_ready, [jax.ShapeDtypeStruct(s.shape, s.dtype) for s in sums_ready], "alltoall",
        after=sums_ready[0])
    kr_held = kr + ready_flight[4][:1, :].astype(BF16)

    dq_cat, dkv, dkr = _attn_bwd(q_cat, kv, kr_held, o_mla, lse, do_mla)
    (dq_pre,) = _tiled("rot_q_bwd", _f_rotq_bwd, nt, tabs + [rowspec(dq_cat)],
                       [rowout(q_pre.shape[1], BF16)], row_tile=big_tile)
    dqn = _mm("up_q_dx", dq_pre, w_uq, tb=True, out_dtype=BF16)
    g_w_uq = _mm("up_q_dw", qn, dq_pre, ta=True)
    dkvn = _mm("up_kv_dx", dkv, w_ukv, tb=True, out_dtype=BF16)
    g_w_ukv = _mm("up_kv_dw", kvn, dkv, ta=True)
    dqkv, g_q_norm, g_kv_norm = _tiled(
        "qkv_norm_bwd", _f_qkv_bwd, nt, qkv_in + [rowspec(dqn), rowspec(dkvn), rowspec(dkr)],
        [rowout(qkv.shape[1], BF16), accout(w["q_norm_g"]), accout(w["kv_norm_g"])], row_tile=big_tile)

    lru_out = [((s_len, d_rnn), BF16, (s_len, ct), "col")]
    for a in (conv_w, conv_b, wa_bd, w["b_gate_a"], wx_bd, w["b_gate_x"], w["lru_param"]):
        lru_out.append((a.shape, F32, (a.shape[0], ct), "col"))
    dx_rnn, g_conv_w, g_conv_b, g_wa_bd, g_b_a, g_wx_bd, g_b_x, g_lru = _tiled(
        "lru_bwd", _f_lru_bwd, n_ct, lru_in + [colspec(h_rnn), colspec(dy_rnn)], lru_out)

    dh1 = _mm("in_gates_dx", d_gates, w_g, tb=True)
    dh1 = _mm("in_qkv_dx", dqkv, w_qkv, tb=True, add=dh1)
    dh1 = _mm("in_rnn_dx", dx_rnn, w_rnn, tb=True, add=dh1)
    g_w_rnn = _mm("in_rnn_dw", h1, dx_rnn, ta=True)
    g_w_qkv = _mm("in_qkv_dw", h1, dqkv, ta=True)
    g_w_g = _mm("in_gates_dw", h1, d_gates, ta=True)

    g_first = {
        "w_in": jnp.concatenate([g_w_rnn, g_w_qkv[:, :n_q + n_kv],
                                 g_w_qkv[:, n_q + n_kv + QK_NOPE:n_q + n_kv + QK_NOPE + QK_ROPE], g_w_g], axis=1),
        "w_uq": g_w_uq.reshape(n_q, N_HEADS, HEAD_PAD)[:, :, :hd].reshape(n_q, -1),
        "w_ukv": g_w_ukv,
    }
    first_chunks = [chunked(k, g_first[k]) for k in first_names]
    first_pair_flight = _split_start(
        "reduce_pair_first_start", first_chunks,
        [jax.ShapeDtypeStruct((4,) + ck.shape[2:], F32) for ck in first_chunks], "pair", after=first_chunks[0])

    ln_bwd = _vjp_of(_f_ln, 4, (0, 1, 2, 3))

    def ln1_bwd(xv, gv, sc, sh, dxr, dh):
        dx, dg, dsc, dsh = ln_bwd(xv, gv, sc, sh, dh)
        return dx + dxr, dg, dsc, dsh

    ln1_held = [ln1_in[0], full2(norm1_g + first_pair_flight[4][:1, :1])] + ln1_in[2:]
    grad_x, g_norm1, d_scale1, d_shift1 = _tiled(
        "ln1_bwd", ln1_bwd, nt, ln1_held + [rowspec(dx_res), rowspec(dh1)],
        [rowout(d, F32), accout(norm1_g), accout(scale1), accout(shift1)], row_tile=big_tile)

    dmod = jnp.concatenate([d_shift1, d_scale1, d_gate1, d_shift2, d_scale2, d_gate2], axis=1)
    dmod_all = _all_gather("gather_dmod", dmod, ALL7).reshape(8, -1)
    dmod_loc = lax.dynamic_slice_in_dim(dmod_all, chip * n_mod, n_mod, axis=1)
    g_w_ada = _mm("ada_dw", c_act, jnp.pad(dmod_loc, ((0, c_rows - 8), (0, 0))), ta=True)

    g_convs = {"conv_w": g_conv_w, "ffn_conv_w": g_ffn_cw}
    g_small = {
        "b_ada": dmod, "norm1_g": g_norm1, "conv_b": g_conv_b,
        "w_gate_a": _block_diag_pairs_t(g_wa_bd)[None], "b_gate_a": g_b_a,
        "w_gate_x": _block_diag_pairs_t(g_wx_bd)[None], "b_gate_x": g_b_x, "lru_param": g_lru,
        "q_norm_g": g_q_norm, "kv_norm_g": g_kv_norm, "norm2_g": g_norm2,
        "ffn_conv_b": g_ffn_cb, "final_g": d_final_g.reshape(w["final_g"].shape),
    }

    small_flat = jnp.concatenate([g_small[k].reshape(-1) for k in SMALL] + [g_convs[k].reshape(-1) for k, _ in CONVS])
    small_rows = -(-small_flat.shape[0] // (8 * PACK_COLS * PACK_ROW_UNIT)) * PACK_ROW_UNIT
    small_chunk = _pad_rows(small_flat[None], 8 * small_rows).reshape(4, 2, small_rows, PACK_COLS)
    last_names = first_names + ["small"]
    send_sems, recv_sems, flown, landed, _ = first_pair_flight
    first_received = _split_wait("reduce_pair_first_wait", send_sems, recv_sems, flown, landed, "pair", after=small_chunk)
    sums_last = [_reduce_pair("reduce_pair_" + k, ck, got, core)
                 for k, ck, got in zip(first_names, first_chunks, first_received)]
    sums_last += pair_sums("small", ["small"], [small_chunk])
    send_sems, recv_sems, flown, landed, _ = ready_flight
    quads_ready = _split_wait("reduce_ready_wait", send_sems, recv_sems, flown, landed, "alltoall", after=grad_x)
    last_flight = _split_start(
        "reduce_last_start", sums_last, [jax.ShapeDtypeStruct(s.shape, s.dtype) for s in sums_last], "alltoall",
        after=quads_ready[0])
    grads = {"w_ada": g_w_ada[None]}
    delta, new_m, new_v = {}, {}, {}

    def adamw(k):
        shp = w[k].shape
        flip = len(shp) == 3 and shp[-1] % LANE != 0 and shp[-2] % LANE == 0
        view = (lambda a: jnp.swapaxes(a, 1, 2)) if flip else (lambda a: a)
        two_d = (-1, view(w[k]).shape[-1]) if len(shp) > 1 else (1, -1)
        dk, mk, vk = _adamw("adamw_" + k, *[view(a).reshape(two_d) for a in (w[k], grads[k], m_in[k], v_in[k])])
        back = lambda a: view(a.reshape(view(w[k]).shape))
        delta[k], new_m[k], new_v[k] = back(dk), back(mk), back(vk)

    def finish(tag, names, quads, sums, after):
        reduced = {}
        for k, quad, ps in zip(names, quads, sums):
            quad = lax.dynamic_update_index_in_dim(quad, lax.dynamic_index_in_dim(ps, chip, 0, keepdims=True), chip, 0)
            reduced[k] = _reduce_quad("reduce_quad_" + k, quad, after)
        big = [k for k in names if k != "small"]
        for k, both in zip(big, _share_sibling("share_sibling_" + tag, [reduced[k] for k in big])):
            grads[k] = lax.dynamic_update_index_in_dim(both, reduced[k][None], ci, 0).reshape(w[k].shape)
        return reduced

    finish("ready", later_names, quads_ready, sums_ready, after=last_flight[4])
    for k in later_names + ["w_ada"]:
        adamw(k)
    send_sems, recv_sems, flown, landed, _ = last_flight
    quads_last = _split_wait("reduce_last_wait", send_sems, recv_sems, flown, landed, "alltoall",
                             after=delta[later_names[-1]])
    reduced = finish("last", last_names, quads_last, sums_last, after=None)
    small_grad = _all_gather("share_small", reduced["small"], ALL7).reshape(-1)
    off = 0
    for k in SMALL:
        grads[k] = small_grad[off:off + w[k].size].reshape(w[k].shape)
        off += w[k].size
    for k, _ in CONVS:
        r, cc = local2d[k].shape
        whole = small_grad[off:off + 4 * r * cc].reshape(r, 4 * cc)
        grads[k] = lax.dynamic_slice_in_dim(whole, chip * cc, cc, axis=1)[None]
        off += 4 * r * cc
    for k in WEIGHTS:
        if k not in delta:
            adamw(k)

    return (loss, grad_x[None], *[grads[k] for k in WEIGHTS], *[delta[k] for k in WEIGHTS],
            *[new_m[k] for k in WEIGHTS], *[new_v[k] for k in WEIGHTS])


def kernel(x, c, positions, w_ada, b_ada, norm1_g, w_in, conv_w, conv_b, w_gate_a, b_gate_a, w_gate_x, b_gate_x, lru_param, q_norm_g, w_uq, kv_norm_g, w_ukv, w_proj_rnn, w_proj_mla, w_out, norm2_g, w_up, ffn_conv_w, ffn_conv_b, w_down, final_g, loss_target, m_w_ada, m_b_ada, m_norm1_g, m_w_in, m_conv_w, m_conv_b, m_w_gate_a, m_b_gate_a, m_w_gate_x, m_b_gate_x, m_lru_param, m_q_norm_g, m_w_uq, m_kv_norm_g, m_w_ukv, m_w_proj_rnn, m_w_proj_mla, m_w_out, m_norm2_g, m_w_up, m_ffn_conv_w, m_ffn_conv_b, m_w_down, m_final_g, v_w_ada, v_b_ada, v_norm1_g, v_w_in, v_conv_w, v_conv_b, v_w_gate_a, v_b_gate_a, v_w_gate_x, v_b_gate_x, v_lru_param, v_q_norm_g, v_w_uq, v_kv_norm_g, v_w_ukv, v_w_proj_rnn, v_w_proj_mla, v_w_out, v_norm2_g, v_w_up, v_ffn_conv_w, v_ffn_conv_b, v_w_down, v_final_g):
    given = dict(locals())
    w = {k: given[k] for k in WEIGHTS}
    m_in = {k: given["m_" + k] for k in WEIGHTS}
    v_in = {k: given["v_" + k] for k in WEIGHTS}
    return _step(x, c, positions, w, m_in, v_in, loss_target)
```

```python
import functools
import math

import jax
import jax.numpy as jnp
from jax import lax
from jax.experimental import pallas as pl
from jax.experimental.pallas import tpu as pltpu

F32 = jnp.float32
BF16 = jnp.bfloat16

EPS = 1e-6
LRU_C = 8.0
N_HEADS = 16
QK_NOPE = 64
QK_ROPE = 32
HEAD_PAD = 128
ROPE_THETA = 10000.0
ADAM_LR = 0.001
ADAM_B1 = 0.9
ADAM_B2 = 0.999
ADAM_EPS = 1e-08
ADAM_WD = 0.01
ADAM_STEP = 10

LANE = 128
SUBLANES = 8
VMEM_LIMIT = 48 * 1024 * 1024
MM_TILE_M = MM_TILE_N = MM_TILE_K = 1408
ROW_TILE = 256
WIDE_ROW_TILE = 512
ATTN_TILE = 1024
ADAM_BLOCK_BYTES = 2 * 1024 * 1024
REDUCE_BLOCK_BYTES = 1024 * 1024
PACK_COLS = 1024
PACK_ROW_UNIT = 32
MESH = pl.DeviceIdType.MESH

NN = (((1,), (0,)), ((), ()))
NT = (((1,), (1,)), ((), ()))
TN = (((0,), (0,)), ((), ()))


def _cparams(sem):
    return pltpu.CompilerParams(dimension_semantics=sem, vmem_limit_bytes=VMEM_LIMIT)


def _div_tile(n, cap, unit):
    best = None
    d = unit
    while d <= min(n, cap):
        if n % d == 0:
            best = d
        d += unit
    return n if best is None else best


def _mm(name, a, b, *, ta=False, tb=False, add=None, out_dtype=F32):
    if ta:
        kdim, m = a.shape
    else:
        m, kdim = a.shape
    if tb:
        n, kb = b.shape
    else:
        kb, n = b.shape
    assert kdim == kb, (name, a.shape, b.shape)
    tm = _div_tile(m, MM_TILE_M, 8 if not ta else LANE)
    tn = _div_tile(n, MM_TILE_N, LANE)
    tk = _div_tile(kdim, MM_TILE_K, LANE)
    nk = kdim // tk
    a_spec = pl.BlockSpec((tk, tm), lambda i, j, k: (k, i)) if ta else pl.BlockSpec((tm, tk), lambda i, j, k: (i, k))
    b_spec = pl.BlockSpec((tn, tk), lambda i, j, k: (j, k)) if tb else pl.BlockSpec((tk, tn), lambda i, j, k: (k, j))
    o_spec = pl.BlockSpec((tm, tn), lambda i, j, k: (i, j))
    has_add = add is not None
    dims = ((((0,) if ta else (1,)), ((1,) if tb else (0,))), ((), ()))

    def body(*refs):
        a_ref, b_ref = refs[0], refs[1]
        c_ref = refs[2] if has_add else None
        o_ref = refs[3] if has_add else refs[2]
        prod = lax.dot_general(a_ref[...].astype(BF16), b_ref[...].astype(BF16), dims, preferred_element_type=F32)
        if nk == 1:
            o_ref[...] = (prod + c_ref[...].astype(F32) if has_add else prod).astype(o_ref.dtype)
            return
        acc = refs[-1]
        k = pl.program_id(2)

        @pl.when(k == 0)
        def _():
            acc[...] = prod + c_ref[...].astype(F32) if has_add else prod

        @pl.when(jnp.logical_and(k > 0, k < nk - 1))
        def _():
            acc[...] += prod

        @pl.when(k == nk - 1)
        def _():
            o_ref[...] = (acc[...] + prod).astype(o_ref.dtype)

    ins = [a, b] + ([add] if has_add else [])
    specs = [a_spec, b_spec] + ([o_spec] if has_add else [])
    return pl.pallas_call(
        body, name=name, grid=(m // tm, n // tn, nk), in_specs=specs, out_specs=o_spec,
        out_shape=jax.ShapeDtypeStruct((m, n), out_dtype),
        scratch_shapes=[pltpu.VMEM((tm, tn), F32)] if nk > 1 else [],
        compiler_params=_cparams(("parallel", "parallel", "arbitrary")),
    )(*ins)


_IMAPS = {
    "row": lambda i: (i, 0),
    "col": lambda i: (0, i),
    "full": lambda i: (0, 0),
    "acc": lambda i: (0, 0),
}


def _tiled(name, fn, n, ins, outs, row_tile=None):
    if row_tile is not None:
        rows = next(a.shape[0] for a, _, k in ins if k == "row")
        n = rows // row_tile
        ins = [(a, (row_tile, bs[1]) if k == "row" else bs, k) for a, bs, k in ins]
        outs = [(s, dt, (row_tile, bs[1]) if k == "row" else bs, k) for s, dt, bs, k in outs]
    ni = len(ins)
    is_acc = [k == "acc" for *_, k in outs]

    def body(*refs):
        vals = fn(*[r[...].astype(F32) if r.dtype == BF16 else r[...] for r in refs[:ni]])
        orefs = refs[ni:]
        if any(is_acc):
            @pl.when(pl.program_id(0) == 0)
            def _():
                for r, a in zip(orefs, is_acc):
                    if a:
                        r[...] = jnp.zeros(r.shape, r.dtype)
        for r, v, a in zip(orefs, vals, is_acc):
            if a:
                r[...] += v.astype(r.dtype)
            else:
                r[...] = v.astype(r.dtype)

    res = pl.pallas_call(
        body, name=name, grid=(n,),
        in_specs=[pl.BlockSpec(bs, _IMAPS[k]) for _, bs, k in ins],
        out_specs=[pl.BlockSpec(bs, _IMAPS[k]) for _, _, bs, k in outs],
        out_shape=[jax.ShapeDtypeStruct(s, d) for s, d, _, _ in outs],
        compiler_params=_cparams(("arbitrary",)),
    )(*[a for a, _, _ in ins])
    return tuple(res)


def _vjp_of(fn, nin, diff):
    def g(*args):
        ins, cots = args[:nin], args[nin:]

        def f(*d):
            full = list(ins)
            for i, v in zip(diff, d):
                full[i] = v
            return fn(*full)

        outs, vjp = jax.vjp(f, *[ins[i] for i in diff])
        return vjp(tuple(c.astype(o.dtype) for c, o in zip(cots, outs)))
    return g


def _shift_rows(x, k, fill, up=False):
    n = x.shape[0]
    rows = lax.broadcasted_iota(jnp.int32, x.shape, 0)
    if up:
        return jnp.where(rows < n - k, pltpu.roll(x, n - k, 0), fill)
    return jnp.where(rows >= k, pltpu.roll(x, k, 0), fill)


@functools.partial(jax.custom_vjp, nondiff_argnums=(1,))
def _delay(x, k):
    return _shift_rows(x, k, 0.0)


def _delay_fwd(x, k):
    return _shift_rows(x, k, 0.0), None


def _delay_bwd(k, _, g):
    return (_shift_rows(g, k, 0.0, up=True),)


_delay.defvjp(_delay_fwd, _delay_bwd)


@functools.partial(jax.custom_vjp, nondiff_argnums=(1,))
def _lane_roll(x, s):
    return pltpu.roll(x, s, 1)


def _lane_roll_fwd(x, s):
    return pltpu.roll(x, s, 1), None


def _lane_roll_bwd(s, _, g):
    return (pltpu.roll(g, g.shape[1] - s, 1),)


_lane_roll.defvjp(_lane_roll_fwd, _lane_roll_bwd)


@jax.custom_vjp
def _bdot(x, w):
    return lax.dot_general(x.astype(BF16), w.astype(BF16), NN, preferred_element_type=F32)


def _bdot_fwd(x, w):
    return _bdot(x, w), (x, w)


def _bdot_bwd(res, g):
    x, w = res
    gb = g.astype(BF16)
    dx = lax.dot_general(gb, w.astype(BF16), NT, preferred_element_type=F32)
    dw = lax.dot_general(x.T.astype(BF16), gb, NN, preferred_element_type=F32)
    return dx, dw


_bdot.defvjp(_bdot_fwd, _bdot_bwd)


def _sigmoid(x):
    return 0.5 * (jnp.tanh(0.5 * x) + 1.0)


def _silu(x):
    return x * _sigmoid(x)


def _rms(x, g):
    return x * lax.rsqrt(jnp.mean(x * x, axis=-1, keepdims=True) + EPS) * g


def _causal_conv(x, w, b):
    kw = w.shape[0]
    tap = lax.broadcasted_iota(jnp.int32, w.shape, 0)
    y = b
    for k in range(kw):
        d = kw - 1 - k
        wk = jnp.sum(jnp.where(tap == k, w, 0.0), axis=0, keepdims=True)
        y = y + wk * (x if d == 0 else _delay(x, d))
    return y


def _rotate(x, cos_f, sin_a, sin_b):
    reps = x.shape[1] // LANE
    if reps > 1:
        cos_f, sin_a, sin_b = (jnp.tile(t, (1, reps)) for t in (cos_f, sin_a, sin_b))
    n = x.shape[1]
    half = QK_ROPE // 2
    return x * cos_f + _lane_roll(x, n - half) * sin_a + _lane_roll(x, half) * sin_b


def _softplus_neg(l):
    u = jnp.exp(-jnp.abs(l))
    log1p_u = jnp.where(u < 0.01, u * (1.0 - u * (0.5 - u * (1.0 / 3.0))), jnp.log(1.0 + u))
    return jnp.maximum(-l, 0.0) + log1p_u


def _f_ln(x, g, scale, shift):
    return (_rms(x, g) * (1.0 + scale) + shift,)


def _f_qkv(qkv, cos_f, sin_a, sin_b, qg, kvg):
    nq, nkv = qg.shape[1], kvg.shape[1]
    qn = _rms(qkv[:, :nq], qg)
    kvn = _rms(qkv[:, nq:nq + nkv], kvg)
    kr = _rotate(qkv[:, nq + nkv:], cos_f, sin_a, sin_b)
    return qn, kvn, kr


def _f_qkv_bwd(qkv, cos_f, sin_a, sin_b, qg, kvg, dqn, dkvn, dkr):
    nq, nkv = qg.shape[1], kvg.shape[1]
    _, vjp_q = jax.vjp(_rms, qkv[:, :nq], qg)
    _, vjp_kv = jax.vjp(_rms, qkv[:, nq:nq + nkv], kvg)
    _, vjp_r = jax.vjp(lambda t: _rotate(t, cos_f, sin_a, sin_b), qkv[:, nq + nkv:])
    dq_lat, dqg = vjp_q(dqn)
    dkv_lat, dkvg = vjp_kv(dkvn)
    (dkr_pre,) = vjp_r(dkr)
    return jnp.concatenate([dq_lat, dkv_lat, dkr_pre], axis=1), dqg, dkvg


QK_SCALE = 1.0 / math.sqrt(QK_NOPE + QK_ROPE)
LOG2_E = 1.4426950408889634
LN_2 = 0.6931471805599453


def _rotate_bf16(x, cos_f, sin_s):
    row = lax.broadcasted_iota(jnp.int32, (LANE, LANE), 0)
    col = lax.broadcasted_iota(jnp.int32, (LANE, LANE), 1)
    half = QK_ROPE // 2
    first, second = QK_NOPE, QK_NOPE + half
    swap = (((row >= first) & (row < second) & (col == row + half))
            | ((row >= second) & (row < second + half) & (col == row - half))).astype(BF16)
    partner = jnp.concatenate(
        [lax.dot_general(x[:, b * LANE:(b + 1) * LANE].astype(BF16), swap, NN, preferred_element_type=F32)
         for b in range(x.shape[1] // LANE)], axis=1)
    reps = x.shape[1] // LANE
    return x * jnp.tile(cos_f, (1, reps)) + partner * jnp.tile(sin_s, (1, reps))


def _f_rotq(q, cos_f, sin_a, sin_b):
    return (_rotate_bf16(q, cos_f, sin_a + sin_b) * (QK_SCALE * LOG2_E),)


def _f_rotq_bwd(cos_f, sin_a, sin_b, dq):
    return (_rotate_bf16(dq, cos_f, -(sin_a + sin_b)) * QK_SCALE,)


def _merge(g_rnn, g_mla, p_rnn, p_mla):
    return _sigmoid(g_rnn) * p_rnn + _sigmoid(g_mla) * p_mla


def _f_merge(g, p_rnn, p_mla):
    d = p_rnn.shape[1]
    return (_merge(g[:, :d], g[:, d:], p_rnn, p_mla),)


def _f_merge_bwd(g, p_rnn, p_mla, dm):
    d = p_rnn.shape[1]
    _, vjp = jax.vjp(_merge, g[:, :d], g[:, d:], p_rnn, p_mla)
    dg_rnn, dg_mla, dp_rnn, dp_mla = vjp(dm)
    return jnp.concatenate([dg_rnn, dg_mla], axis=1), dp_rnn, dp_mla


def _f_res_ln(x, o, gate, g2, scale, shift):
    x1 = x + gate * o
    return x1, _rms(x1, g2) * (1.0 + scale) + shift


def _f_ffn(u_gate, u_val, cw_gate, cw_val, cb_gate, cb_val):
    h = _causal_conv(u_gate, 0.5 * cw_gate, 0.5 * cb_gate)
    return ((h + h * jnp.tanh(h)) * _causal_conv(u_val, cw_val, cb_val),)


def _f_loss(x1, f, tgt, gate, fg):
    y = _rms(x1 + gate * f, fg)
    err = (y - tgt) * (y - tgt)
    return 0.5 * jnp.sum(jnp.mean(err, axis=-1, keepdims=True), axis=0, keepdims=True)


def _f_loss_and_grads(x1, f, tgt, gate, fg):
    loss, vjp = jax.vjp(lambda a, b, c, d: _f_loss(a, b, tgt, c, d), x1, f, gate, fg)
    dx1, df, dgate, dfg = vjp(jnp.ones((1, 1), F32))
    return dx1, df, jnp.broadcast_to(loss, (1, LANE)), dgate, dfg


@jax.custom_vjp
def _decay_and_gain(log_a):
    a = jnp.exp(log_a)
    return a, jnp.sqrt(-jnp.tanh(log_a) * (1.0 + a * a))


def _decay_and_gain_fwd(log_a):
    a, gain = _decay_and_gain(log_a)
    return (a, gain), (a, gain)


def _decay_and_gain_bwd(res, g):
    a, gain = res
    return (g[0] * a - g[1] * (a * a) / gain,)


_decay_and_gain.defvjp(_decay_and_gain_fwd, _decay_and_gain_bwd)


def _f_lru_coeffs(xr, cw, cb, wa, ba, wx, bx, lru, reset):
    xc = _causal_conv(xr, cw, cb)
    r = _sigmoid(_bdot(xc, wa) + ba)
    i = _sigmoid(_bdot(xc, wx) + bx)
    log_a = (-LRU_C) * r * _softplus_neg(lru)
    a, mult = _decay_and_gain(log_a)
    is_reset = reset > 0.5
    a = jnp.where(is_reset, 0.0, a)
    mult = jnp.where(is_reset, 1.0, mult)
    return a, mult * (i * xc)


SCAN_BLOCK = 64


def _scan(a, b, up=False):
    n = a.shape[0]
    blk = min(SCAN_BLOCK, n)
    pos = lax.broadcasted_iota(jnp.int32, a.shape, 0) % blk
    k = 1
    while k < blk:
        inside = (pos < blk - k) if up else (pos >= k)
        shift = n - k if up else k
        b = b + a * jnp.where(inside, pltpu.roll(b, shift, 0), 0.0)
        a = a * jnp.where(inside, pltpu.roll(a, shift, 0), 1.0)
        k *= 2
    blocks = range(n // blk)
    carry = jnp.zeros((1,) + a.shape[1:], a.dtype)
    out = [None] * len(blocks)
    for i in (reversed(blocks) if up else blocks):
        rows = slice(i * blk, (i + 1) * blk)
        out[i] = b[rows] + a[rows] * carry
        carry = out[i][:1] if up else out[i][blk - 1:]
    return jnp.concatenate(out, axis=0)


def _f_lru_fwd(xr, cw, cb, wa, ba, wx, bx, lru, reset):
    a, b = _f_lru_coeffs(xr, cw, cb, wa, ba, wx, bx, lru, reset)
    h = _scan(a, b)
    return h, h


def _f_lru_bwd(xr, cw, cb, wa, ba, wx, bx, lru, reset, h, dh):
    (a, _), vjp = jax.vjp(lambda *p: _f_lru_coeffs(*p, reset), xr, cw, cb, wa, ba, wx, bx, lru)
    g = _scan(_shift_rows(a, 1, 0.0, up=True), dh, up=True)
    return vjp((g * _shift_rows(h, 1, 0.0), g))


def _attn_tile(s):
    return ATTN_TILE if s >= 2 * ATTN_TILE else s // 2


def _keys(kv, kr):
    lane = lax.broadcasted_iota(jnp.int32, kv.shape, 1)
    return jnp.where(lane < QK_NOPE, kv, kr)


ATTN_HEADS_PER_STEP = 2


def _scores(q, kc, diagonal):
    s = lax.dot_general(q, kc, NT, preferred_element_type=F32)
    if not diagonal:
        return s
    rows = lax.broadcasted_iota(jnp.int32, s.shape, 0)
    cols = lax.broadcasted_iota(jnp.int32, s.shape, 1)
    return jnp.where(cols - (s.shape[1] - s.shape[0]) <= rows, s, -jnp.inf)


def _sub_blocks(t, diagonal):
    return ((0, t // 2, t // 2), (t // 2, t // 2, t)) if diagonal else ((0, t, t),)


def _causal_pairs(nb, k_major):
    if k_major:
        pairs = [(qb, kb) for kb in range(nb) for qb in range(kb, nb)]
    else:
        pairs = [(qb, kb) for qb in range(nb) for kb in range(qb + 1)]
    return jnp.array([p[0] for p in pairs], jnp.int32), jnp.array([p[1] for p in pairs], jnp.int32)


def _attn_fwd(q_pre, tables, kv, kr):
    s_len = q_pre.shape[0]
    t = _attn_tile(s_len)
    nb = s_len // t
    hp = ATTN_HEADS_PER_STEP
    wide = hp * HEAD_PAD
    q_tab, k_tab = _causal_pairs(nb, k_major=False)

    def body(qt, kt, qp_ref, cos_ref, sina_ref, sinb_ref, kv_ref, kr_ref, o_ref, lse_ref, q_ref, m_s, acc_s):
        pair = pl.program_id(1)
        qi, ki = qt[pair], kt[pair]

        @pl.when(ki == 0)
        def _():
            m_s[...] = jnp.full(m_s.shape, -jnp.inf, F32)
            acc_s[...] = jnp.zeros(acc_s.shape, F32)
            (rotated,) = _f_rotq(qp_ref[...].astype(F32), cos_ref[...], sina_ref[...], sinb_ref[...])
            q_ref[...] = rotated.astype(q_ref.dtype)

        def step(diagonal):
            for h in range(hp):
                lanes = slice(h * HEAD_PAD, (h + 1) * HEAD_PAD)
                for r0, nr, nk in _sub_blocks(t, diagonal):
                    rows = slice(r0, r0 + nr)
                    kvv = kv_ref[:nk, lanes]
                    s = _scores(q_ref[rows, lanes], _keys(kvv, kr_ref[:nk, :]), diagonal)
                    m_old = m_s[h, rows]
                    m_new = jnp.maximum(m_old, jnp.max(s, axis=-1, keepdims=True))
                    alpha = jnp.exp2(m_old - m_new)
                    p = jnp.exp2(s - jnp.tile(m_new, (1, s.shape[1] // HEAD_PAD)))
                    lane = lax.broadcasted_iota(jnp.int32, kvv.shape, 1)
                    ones_and_values = jnp.where(lane < QK_NOPE, jnp.ones_like(kvv), kvv)
                    acc_s[rows, lanes] = alpha * acc_s[rows, lanes] + lax.dot_general(
                        p.astype(BF16), ones_and_values, NN, preferred_element_type=F32)
                    m_s[h, rows] = m_new

        @pl.when(ki < qi)
        def _():
            step(False)

        @pl.when(ki == qi)
        def _():
            step(True)
            lane = lax.broadcasted_iota(jnp.int32, (t, HEAD_PAD), 1)
            outs = []
            for h in range(hp):
                acc = acc_s[:, h * HEAD_PAD:(h + 1) * HEAD_PAD]
                total = acc[:, :1]
                outs.append(acc / total)
                lse_ref[h] = m_s[h][:, :1] + jnp.log(total) * LOG2_E
            o_ref[...] = jnp.where(lane >= QK_NOPE, outs[0], pltpu.roll(outs[1], QK_NOPE, 1)).astype(o_ref.dtype)

    q_rows = lambda h, p, qt, kt: (qt[p], 0)
    grid_spec = pltpu.PrefetchScalarGridSpec(
        num_scalar_prefetch=2, grid=(N_HEADS // hp, q_tab.shape[0]),
        in_specs=[pl.BlockSpec((t, wide), lambda h, p, qt, kt: (qt[p], h)),
                  pl.BlockSpec((t, HEAD_PAD), q_rows), pl.BlockSpec((t, HEAD_PAD), q_rows),
                  pl.BlockSpec((t, HEAD_PAD), q_rows),
                  pl.BlockSpec((t, wide), lambda h, p, qt, kt: (kt[p], h)),
                  pl.BlockSpec((t, HEAD_PAD), lambda h, p, qt, kt: (kt[p], 0))],
        out_specs=[pl.BlockSpec((t, HEAD_PAD), lambda h, p, qt, kt: (qt[p], h)),
                   pl.BlockSpec((hp, t, 1), lambda h, p, qt, kt: (h, qt[p], 0)),
                   pl.BlockSpec((t, wide), lambda h, p, qt, kt: (qt[p], h))],
        scratch_shapes=[pltpu.VMEM((hp, t, HEAD_PAD), F32), pltpu.VMEM((t, wide), F32)])
    return pl.pallas_call(
        body, name="attn_fwd", grid_spec=grid_spec,
        out_shape=[jax.ShapeDtypeStruct((s_len, N_HEADS // hp * HEAD_PAD), BF16),
                   jax.ShapeDtypeStruct((N_HEADS, s_len, 1), F32),
                   jax.ShapeDtypeStruct((s_len, N_HEADS * HEAD_PAD), BF16)],
        compiler_params=_cparams(("arbitrary", "arbitrary")),
    )(q_tab, k_tab, q_pre, *tables, kv, kr)


def _attn_bwd(q, kv, kr, o, lse, do):
    s_len = q.shape[0]
    t = _attn_tile(s_len)
    nb = s_len // t
    hp = ATTN_HEADS_PER_STEP
    wide = hp * HEAD_PAD
    q_tab, k_tab = _causal_pairs(nb, k_major=True)

    def body(qt, kt, q_ref, kv_ref, kr_ref, o_ref, lse_ref, do_ref, dq_ref, dkv_ref, dkr_ref, dk_s, dv_s, dq_s):
        g, pair = pl.program_id(0), pl.program_id(1)
        qb, kb = qt[pair], kt[pair]

        @pl.when(jnp.logical_and(g == 0, pair == 0))
        def _():
            dkr_ref[...] = jnp.zeros(dkr_ref.shape, F32)

        @pl.when(pair == 0)
        def _():
            dq_s[...] = jnp.zeros(dq_s.shape, F32)

        @pl.when(qb == kb)
        def _():
            dk_s[...] = jnp.zeros(dk_s.shape, F32)
            dv_s[...] = jnp.zeros(dv_s.shape, F32)

        def step(diagonal):
            for h in range(hp):
                lanes = slice(h * HEAD_PAD, (h + 1) * HEAD_PAD)
                for r0, nr, nk in _sub_blocks(t, diagonal):
                    rows, keys = slice(r0, r0 + nr), slice(0, nk)
                    qv, kvv = q_ref[rows, lanes], kv_ref[keys, lanes]
                    pair_do = do_ref[rows, :].astype(F32)
                    lane = lax.broadcasted_iota(jnp.int32, pair_do.shape, 1)
                    mine = (lane >= QK_NOPE) if h == 0 else (lane < QK_NOPE)
                    placed = pair_do if h == 0 else pltpu.roll(pair_do, QK_NOPE, 1)
                    dov = jnp.where(lane >= QK_NOPE, placed, 0.0).astype(BF16)
                    delta = jnp.sum(jnp.where(mine, pair_do * o_ref[rows, :].astype(F32), 0.0), axis=-1, keepdims=True)
                    kc = _keys(kvv, kr_ref[keys, :])
                    p = jnp.exp2(_scores(qv, kc, diagonal) - lse_ref[h, rows])
                    dp = lax.dot_general(dov, kvv, NT, preferred_element_type=F32)
                    ds = p * (dp - delta)
                    dv_s[keys, lanes] += lax.dot_general(p.astype(BF16), dov, TN, preferred_element_type=F32)
                    dk_s[keys, lanes] += lax.dot_general(ds.astype(BF16), qv, TN, preferred_element_type=F32)
                    q_rows = pl.ds(pl.multiple_of(qb * t + r0, nr), nr)
                    dq_s[q_rows, lanes] += lax.dot_general(ds.astype(BF16), kc, NN, preferred_element_type=F32)

        @pl.when(qb > kb)
        def _():
            step(False)

        @pl.when(qb == kb)
        def _():
            step(True)

        @pl.when(qb == nb - 1)
        def _():
            lane = lax.broadcasted_iota(jnp.int32, (t, HEAD_PAD), 1)
            rows = pl.ds(pl.multiple_of(kb * t, t), t)
            for h in range(hp):
                lanes = slice(h * HEAD_PAD, (h + 1) * HEAD_PAD)
                dk = dk_s[:, lanes] * LN_2
                dkv_ref[:, lanes] = jnp.where(lane < QK_NOPE, dk, dv_s[:, lanes]).astype(dkv_ref.dtype)
                dkr_ref[rows, :] += jnp.where(lane >= QK_NOPE, dk, 0.0)

        @pl.when(pair == q_tab.shape[0] - 1)
        def _():
            dq_ref[...] = dq_s[...].astype(dq_ref.dtype)

    all_lanes = N_HEADS * HEAD_PAD
    qmap = lambda h, p, qt, kt: (qt[p], h)
    kmap = lambda h, p, qt, kt: (kt[p], h)
    grid_spec = pltpu.PrefetchScalarGridSpec(
        num_scalar_prefetch=2, grid=(N_HEADS // hp, q_tab.shape[0]),
        in_specs=[pl.BlockSpec((t, wide), qmap),
                  pl.BlockSpec((t, wide), kmap),
                  pl.BlockSpec((t, HEAD_PAD), lambda h, p, qt, kt: (kt[p], 0)),
                  pl.BlockSpec((t, HEAD_PAD), qmap),
                  pl.BlockSpec((hp, t, 1), lambda h, p, qt, kt: (h, qt[p], 0)),
                  pl.BlockSpec((t, HEAD_PAD), qmap)],
        out_specs=[pl.BlockSpec((s_len, wide), lambda h, p, qt, kt: (0, h)),
                   pl.BlockSpec((t, wide), kmap),
                   pl.BlockSpec((s_len, HEAD_PAD), lambda h, p, qt, kt: (0, 0))],
        scratch_shapes=[pltpu.VMEM((t, wide), F32), pltpu.VMEM((t, wide), F32), pltpu.VMEM((s_len, wide), F32)])
    return pl.pallas_call(
        body, name="attn_bwd", grid_spec=grid_spec,
        out_shape=[jax.ShapeDtypeStruct((s_len, all_lanes), BF16),
                   jax.ShapeDtypeStruct((s_len, all_lanes), BF16),
                   jax.ShapeDtypeStruct((s_len, HEAD_PAD), F32)],
        compiler_params=_cparams(("arbitrary", "arbitrary")),
    )(q_tab, k_tab, q, kv, kr, o, lse, do)


def _adamw(name, w, g, m, v):
    rows, cols = w.shape
    tr = _div_tile(rows, max(SUBLANES, ADAM_BLOCK_BYTES // (4 * cols)), SUBLANES)

    def body(w_ref, g_ref, m_ref, v_ref, d_ref, nm_ref, nv_ref):
        gv = g_ref[...]
        nm = ADAM_B1 * m_ref[...] + (1.0 - ADAM_B1) * gv
        nv = ADAM_B2 * v_ref[...] + (1.0 - ADAM_B2) * jnp.square(gv)
        m_hat = nm / (1.0 - ADAM_B1 ** ADAM_STEP)
        v_hat = nv / (1.0 - ADAM_B2 ** ADAM_STEP)
        d_ref[...] = -ADAM_LR * (m_hat / (jnp.sqrt(v_hat) + ADAM_EPS) + ADAM_WD * w_ref[...])
        nm_ref[...] = nm
        nv_ref[...] = nv

    spec = pl.BlockSpec((tr, cols), lambda i: (i, 0))
    return pl.pallas_call(
        body, name=name, grid=(rows // tr,), in_specs=[spec] * 4, out_specs=[spec] * 3,
        out_shape=[jax.ShapeDtypeStruct((rows, cols), F32)] * 3,
        compiler_params=_cparams(("parallel",)),
    )(w, g, m, v)


ALL7 = (1, 2, 3, 4, 5, 6, 7)
CHIPS = (2, 4, 6)


def _all_gather(name, src, masks):
    many = isinstance(src, (list, tuple))
    srcs = list(src) if many else [src]
    n = len(srcs)
    bits = 0
    for m in masks:
        bits |= m
    nslots = {7: 8, 6: 4}[bits]
    nm = len(masks)

    def slot_of(x, y, c):
        return {7: 4 * x + 2 * y + c, 6: 2 * x + y}[bits]

    def body(*refs):
        src_refs, out_refs = refs[:n], refs[n:2 * n]
        send_sems, recv_sems, local_sems = refs[2 * n:]
        x, y, c = lax.axis_index("x"), lax.axis_index("y"), lax.axis_index("c")
        mine = slot_of(x, y, c)
        owns = [pltpu.make_async_copy(s, o.at[mine], local_sems.at[k])
                for k, (s, o) in enumerate(zip(src_refs, out_refs))]
        for own in owns:
            own.start()
        copies = []
        for i, m in enumerate(masks):
            peer = _peer(x, y, c, m)
            for k, (s, o) in enumerate(zip(src_refs, out_refs)):
                sem = k * nm + i
                copies.append((
                    pltpu.make_async_remote_copy(
                        src_ref=s, dst_ref=o.at[mine], send_sem=send_sems.at[sem], recv_sem=recv_sems.at[sem],
                        device_id=peer, device_id_type=MESH),
                    pltpu.make_async_remote_copy(
                        src_ref=s, dst_ref=o.at[slot_of(*peer)], send_sem=send_sems.at[sem],
                        recv_sem=recv_sems.at[sem], device_id=peer, device_id_type=MESH)))
        for send, _ in copies:
            send.start()
        for _, arrival in copies:
            arrival.wait_recv()
        for send, _ in copies:
            send.wait_send()
        for own in owns:
            own.wait()

    hbm = pl.BlockSpec(memory_space=pl.ANY)
    outs = pl.pallas_call(
        body, name=name, in_specs=[hbm] * n, out_specs=[hbm] * n,
        out_shape=[jax.ShapeDtypeStruct((nslots,) + tuple(s.shape), s.dtype) for s in srcs],
        scratch_shapes=[pltpu.SemaphoreType.DMA((n * nm,)), pltpu.SemaphoreType.DMA((n * nm,)),
                        pltpu.SemaphoreType.DMA((n,))],
    )(*srcs)
    return list(outs) if many else outs[0]


def _peer(x, y, c, m):
    return (1 - x if m & 4 else x, 1 - y if m & 2 else y, 1 - c if m & 1 else c)


def _comm_call(name, emit, srcs, out_shapes, n_sems, in_place=False):
    n = len(srcs)

    def body(*refs):
        src_refs, out_refs = refs[:n], refs[n:n + len(out_shapes)]
        send_sems, recv_sems = refs[-2], refs[-1]

        def copy(src, dst, i, peer):
            return pltpu.make_async_remote_copy(src_ref=src, dst_ref=dst, send_sem=send_sems.at[i],
                                                recv_sem=recv_sems.at[i], device_id=peer, device_id_type=MESH)

        emit(lax.axis_index("x"), lax.axis_index("y"), lax.axis_index("c"), src_refs, out_refs, copy)

    hbm = pl.BlockSpec(memory_space=pl.ANY)
    return pl.pallas_call(
        body, name=name, in_specs=[hbm] * n, out_specs=[hbm] * len(out_shapes), out_shape=out_shapes,
        scratch_shapes=[pltpu.SemaphoreType.DMA((n_sems,)), pltpu.SemaphoreType.DMA((n_sems,))],
        input_output_aliases={i: i for i in range(n)} if in_place else {},
    )(*srcs)


HBM_SPEC = pl.BlockSpec(memory_space=pltpu.HBM)
SEM_SPEC = pl.BlockSpec(memory_space=pltpu.SEMAPHORE)
DATAFLOW = pltpu.SideEffectType.DATAFLOW_SIDE_EFFECTING


def _chip_copies(srcs, lands, send_sems, recv_sems, mode):
    x, y, c = lax.axis_index("x"), lax.axis_index("y"), lax.axis_index("c")
    chip = 2 * x + y
    sends, arrivals = [], []
    if mode == "pair":
        for k, (s, l) in enumerate(zip(srcs, lands)):
            for group in (sends, arrivals):
                group.append(pltpu.make_async_remote_copy(
                    src_ref=s.at[:, 1 - c], dst_ref=l, send_sem=send_sems.at[3 * k], recv_sem=recv_sems.at[3 * k],
                    device_id=(x, y, 1 - c), device_id_type=MESH))
        return sends, arrivals
    for j, m in enumerate(CHIPS):
        px, py, _ = _peer(x, y, c, m)
        theirs = 2 * px + py
        for k, (s, l) in enumerate(zip(srcs, lands)):
            if mode == "gather":
                src, dst, got = s.at[c], l.at[chip, c], l.at[theirs, c]
            else:
                src, dst, got = s.at[theirs], l.at[chip], l.at[theirs]
            for to, group in ((dst, sends), (got, arrivals)):
                group.append(pltpu.make_async_remote_copy(
                    src_ref=src, dst_ref=to, send_sem=send_sems.at[3 * k + j], recv_sem=recv_sems.at[3 * k + j],
                    device_id=(px, py, c), device_id_type=MESH))
    return sends, arrivals


def _split_start(name, srcs, land_shapes, mode, after):
    n = len(srcs)

    def body(*refs):
        sends, _ = _chip_copies(refs[:n], refs[n:2 * n], refs[2 * n + 1], refs[2 * n + 2], mode)
        for cp in sends:
            cp.start()
        token = refs[-1]
        token[...] = jnp.zeros(token.shape, token.dtype)

    hbm = lambda a: pltpu.with_memory_space_constraint(a, pltpu.HBM)
    lands = [hbm(lax.empty(s.shape, s.dtype)) for s in land_shapes]
    bufs = [pltpu.HBM(a.shape, a.dtype) for a in list(srcs) + lands]
    res = pl.pallas_call(
        body, name=name,
        out_shape=(pltpu.SemaphoreType.DMA((3 * n,)), pltpu.SemaphoreType.DMA((3 * n,)), *bufs,
                   jax.ShapeDtypeStruct((SUBLANES, LANE), F32)),
        in_specs=[HBM_SPEC] * (2 * n) + [pl.BlockSpec(memory_space=pl.ANY)],
        out_specs=[SEM_SPEC, SEM_SPEC] + [HBM_SPEC] * (2 * n) + [pl.BlockSpec(memory_space=pltpu.VMEM)],
        input_output_aliases={i: 2 + i for i in range(2 * n)},
        compiler_params=pltpu.CompilerParams(has_side_effects=DATAFLOW),
    )(*[hbm(s) for s in srcs], *lands, after)
    return res[0], res[1], res[2:2 + n], res[2 + n:2 + 2 * n], res[-1]


def _split_wait(name, send_sems, recv_sems, srcs, lands, mode, after):
    n = len(srcs)

    def body(*refs):
        sends, arrivals = _chip_copies(refs[:n], refs[n:2 * n], refs[2 * n], refs[2 * n + 1], mode)
        for cp in sends:
            cp.wait_send()
        for cp in arrivals:
            cp.wait_recv()

    res = pl.pallas_call(
        body, name=name,
        out_shape=tuple(pltpu.HBM(a.shape, a.dtype) for a in list(srcs) + list(lands)),
        in_specs=[HBM_SPEC] * (2 * n) + [SEM_SPEC, SEM_SPEC, pl.BlockSpec(memory_space=pl.ANY)],
        out_specs=[HBM_SPEC] * (2 * n),
        input_output_aliases={i: i for i in range(2 * n)},
        compiler_params=pltpu.CompilerParams(has_side_effects=DATAFLOW),
    )(*srcs, *lands, send_sems, recv_sems, after)
    return res[n:]


def _relay_sibling(lands):
    def emit(x, y, c, srcs, outs, copy):
        sib = (x, y, 1 - c)
        sends, arrivals = [], []
        for j, m in enumerate(CHIPS):
            px, py, _ = _peer(x, y, c, m)
            theirs = 2 * px + py
            for k, (s, o) in enumerate(zip(srcs, outs)):
                sends.append(copy(s.at[theirs, c], o.at[theirs, c], 3 * k + j, sib))
                arrivals.append(copy(s.at[theirs, c], o.at[theirs, 1 - c], 3 * k + j, sib))
        for cp in sends:
            cp.start()
        for cp in arrivals:
            cp.wait_recv()
        for cp in sends:
            cp.wait_send()

    shapes = [jax.ShapeDtypeStruct(l.shape, l.dtype) for l in lands]
    return _comm_call("relay_weights", emit, lands, shapes, 3 * len(lands), in_place=True)


def _gather_weights(halves):
    n = len(halves)

    def emit(x, y, c, srcs, outs, copy):
        chip = 2 * x + y
        sib = (x, y, 1 - c)
        first, relay, landed, relayed = [], [], [], []
        for j, m in enumerate(CHIPS):
            px, py, _ = _peer(x, y, c, m)
            theirs = 2 * px + py
            for k in range(n):
                i = 6 * k + j
                first.append(copy(srcs[k].at[c], outs[k].at[chip, c], i, (px, py, c)))
                landed.append(copy(srcs[k].at[c], outs[k].at[theirs, c], i, (px, py, c)))
                relay.append(copy(outs[k].at[theirs, c], outs[k].at[theirs, c], i + 3, sib))
                relayed.append(copy(outs[k].at[theirs, 1 - c], outs[k].at[theirs, 1 - c], i + 3, sib))
        for cp in first:
            cp.start()
        for arrival, onward in zip(landed, relay):
            arrival.wait_recv()
            onward.start()
        for arrival in relayed:
            arrival.wait_recv()
        for cp in first + relay:
            cp.wait_send()

    shapes = [jax.ShapeDtypeStruct((4,) + h.shape, h.dtype) for h in halves]
    return _comm_call("gather_weights", emit, halves, shapes, 6 * n)


def _pair_exchange(name, chunks):
    def emit(x, y, c, srcs, outs, copy):
        sib = (x, y, 1 - c)
        sends = [copy(s.at[:, 1 - c], o, k, sib) for k, (s, o) in enumerate(zip(srcs, outs))]
        for cp in sends:
            cp.start()
        for cp in sends:
            cp.wait_recv()
        for cp in sends:
            cp.wait_send()

    shapes = [jax.ShapeDtypeStruct((4,) + g.shape[2:], g.dtype) for g in chunks]
    return _comm_call(name, emit, chunks, shapes, len(chunks))


def _share_sibling(name, parts):
    def emit(x, y, c, srcs, outs, copy):
        sib = (x, y, 1 - c)
        sends = [copy(s, o.at[c], k, sib) for k, (s, o) in enumerate(zip(srcs, outs))]
        arrivals = [copy(s, o.at[1 - c], k, sib) for k, (s, o) in enumerate(zip(srcs, outs))]
        for cp in sends:
            cp.start()
        for cp in arrivals:
            cp.wait_recv()
        for cp in sends:
            cp.wait_send()

    shapes = [jax.ShapeDtypeStruct((2,) + p.shape, p.dtype) for p in parts]
    return _comm_call(name, emit, parts, shapes, len(parts))


def _reduce_pair(name, chunk, from_sibling, core):
    n, _, h, cols = chunk.shape
    rt = _div_tile(h, max(2 * SUBLANES, REDUCE_BLOCK_BYTES // (4 * cols)), 2 * SUBLANES)

    def body(core_ref, a_ref, b_ref, o_ref):
        o_ref[...] = (a_ref[...] + b_ref[...]).astype(o_ref.dtype)

    grid_spec = pltpu.PrefetchScalarGridSpec(
        num_scalar_prefetch=1, grid=(n, h // rt),
        in_specs=[pl.BlockSpec((None, None, rt, cols), lambda s, i, core_ref: (s, core_ref[0], i, 0)),
                  pl.BlockSpec((None, rt, cols), lambda s, i, core_ref: (s, i, 0))],
        out_specs=pl.BlockSpec((None, rt, cols), lambda s, i, core_ref: (s, i, 0)))
    return pl.pallas_call(
        body, name=name, grid_spec=grid_spec, out_shape=jax.ShapeDtypeStruct((n, h, cols), BF16),
        compiler_params=_cparams(("parallel", "parallel")),
    )(core, chunk, from_sibling)


def _reduce_quad(name, q, after=None):
    _, h, cols = q.shape
    rt = _div_tile(h, max(2 * SUBLANES, REDUCE_BLOCK_BYTES // (4 * cols)), 2 * SUBLANES)

    def body(q_ref, *rest):
        v = q_ref[...].astype(F32)
        rest[-1][...] = ((v[0] + v[1]) + v[2]) + v[3]

    held = [] if after is None else [after]
    return pl.pallas_call(
        body, name=name, grid=(h // rt,),
        in_specs=[pl.BlockSpec((4, rt, cols), lambda i: (0, i, 0))] + [pl.BlockSpec(memory_space=pl.ANY)] * len(held),
        out_specs=pl.BlockSpec((rt, cols), lambda i: (i, 0)),
        out_shape=jax.ShapeDtypeStruct((h, cols), F32),
        compiler_params=_cparams(("parallel",)),
    )(q, *held)


def _unshard(seg, kind):
    n, r, c = seg.shape
    if kind == "col":
        return seg.transpose(1, 0, 2).reshape(r, n * c)
    return seg.reshape(n * r, c)


def _pad_rows(flat, rows):
    n, ln = flat.shape
    return jnp.pad(flat, ((0, 0), (0, rows * PACK_COLS - ln))).reshape(n, rows, PACK_COLS)


def _block_diag_pairs(w):
    n2, bs, _ = w.shape
    eye = jnp.eye(2, dtype=w.dtype)
    z = w.reshape(n2 // 2, 2, bs, 1, bs) * eye[None, :, None, :, None]
    return z.reshape(n2 // 2, 2 * bs, 2 * bs).transpose(1, 0, 2).reshape(2 * bs, n2 * bs)


def _block_diag_pairs_t(d, bs=64):
    n = d.shape[1] // (2 * bs)
    z = d.reshape(2 * bs, n, 2 * bs).transpose(1, 0, 2).reshape(n, 2, bs, 2, bs)
    return jnp.stack([z[:, 0, :, 0, :], z[:, 1, :, 1, :]], axis=1).reshape(2 * n, bs, bs)


BIG = (("w_in", "col"), ("w_uq", "col"), ("w_ukv", "col"), ("w_proj_rnn", "row"), ("w_proj_mla", "row"),
       ("w_out", "row"), ("w_up", "col"), ("w_down", "row"))
FIRST_USED = ("w_in", "w_uq", "w_ukv")
CONVS = (("conv_w", "col"), ("ffn_conv_w", "col"))
SMALL = ("b_ada", "norm1_g", "conv_b", "w_gate_a", "b_gate_a", "w_gate_x", "b_gate_x", "lru_param",
         "q_norm_g", "kv_norm_g", "norm2_g", "ffn_conv_b", "final_g")
WEIGHTS = ("w_ada", "b_ada", "norm1_g", "w_in", "conv_w", "conv_b", "w_gate_a", "b_gate_a", "w_gate_x",
           "b_gate_x", "lru_param", "q_norm_g", "w_uq", "kv_norm_g", "w_ukv", "w_proj_rnn", "w_proj_mla",
           "w_out", "norm2_g", "w_up", "ffn_conv_w", "ffn_conv_b", "w_down", "final_g")


def _step(x, c, positions, w, m_in, v_in, loss_target):
    s_len, d = x.shape[1], x.shape[2]
    x2d = x[0]
    tgt = loss_target[0]
    xi, yi, ci = lax.axis_index("x"), lax.axis_index("y"), lax.axis_index("c")
    chip = 2 * xi + yi
    me = 2 * chip + ci
    tile = min(ROW_TILE, s_len)
    nt = s_len // tile

    local2d = {k: w[k][0] for k, _ in BIG + CONVS}
    kinds = dict(BIG)
    halves_bf = {k: local2d[k].astype(BF16).reshape(2, local2d[k].shape[0] // 2, local2d[k].shape[1]) for k, _ in BIG}
    first_names = [k for k, _ in BIG if k in FIRST_USED]
    later_names = [k for k, _ in BIG if k not in FIRST_USED]
    full = {}

    def assemble(k, g):
        g = lax.dynamic_update_index_in_dim(g, halves_bf[k][None], chip, 0).reshape((4,) + local2d[k].shape)
        if k == "w_up":
            full["w_up_gate"], full["w_up_val"] = _unshard(g[:2], kinds[k]), _unshard(g[2:], kinds[k])
        else:
            full[k] = _unshard(g, kinds[k])

    first_got = _gather_weights([halves_bf[k] for k in first_names])
    for k, g in zip(first_names, first_got):
        assemble(k, g)
    conv_flat = jnp.concatenate([local2d[k].reshape(-1) for k, _ in CONVS])
    conv_rows = -(-conv_flat.shape[0] // PACK_COLS)
    c_gathered, conv_all = _all_gather("gather_c_conv_w", [c, _pad_rows(conv_flat[None], conv_rows)[0]], ALL7)
    conv_all = conv_all[0::2].reshape(4, -1)
    off = 0
    for k, kind in CONVS:
        r, cc = local2d[k].shape
        full[k] = _unshard(conv_all[:, off:off + r * cc].reshape(4, r, cc), kind)
        off += r * cc

    d_rnn = w["conv_b"].shape[1]
    n_q, n_kv = w["q_norm_g"].shape[1], w["kv_norm_g"].shape[1]
    w_in = full["w_in"]
    o1, o2, o3 = d_rnn + n_q, d_rnn + n_q + n_kv, d_rnn + n_q + n_kv + QK_ROPE
    w_rnn = w_in[:, :d_rnn]
    zpad = lambda n: jnp.zeros((d, n), BF16)
    w_qkv = jnp.concatenate([w_in[:, d_rnn:o2], zpad(QK_NOPE), w_in[:, o2:o3], zpad(LANE - QK_NOPE - QK_ROPE)], axis=1)
    w_g = w_in[:, o3:]
    hd = QK_NOPE + QK_ROPE
    w_uq = jnp.pad(full["w_uq"].reshape(n_q, N_HEADS, hd), ((0, 0), (0, 0), (0, HEAD_PAD - hd))).reshape(n_q, -1)
    w_ukv = full["w_ukv"]
    v_head = w_ukv.shape[1] // N_HEADS - QK_NOPE
    d_ff = w["ffn_conv_b"].shape[1] // 2
    ffn_cw_gate, ffn_cw_val = full["ffn_conv_w"][:, :d_ff], full["ffn_conv_w"][:, d_ff:]
    ffn_cb_gate, ffn_cb_val = w["ffn_conv_b"][:, :d_ff], w["ffn_conv_b"][:, d_ff:]
    conv_w, conv_b = full["conv_w"], w["conv_b"]
    wa_bd = _block_diag_pairs(w["w_gate_a"][0])
    wx_bd = _block_diag_pairs(w["w_gate_x"][0])

    c_all = c_gathered.reshape(8, d)
    c_rows = LANE
    (c_act,) = _tiled("silu_c", lambda v: (_silu(v),), 1, [(jnp.pad(c_all, ((0, c_rows - 8), (0, 0))), (c_rows, d), "full")],
                      [((c_rows, d), F32, (c_rows, d), "full")])
    w_ada = w["w_ada"][0]
    n_mod = w_ada.shape[1]
    b_loc = lax.dynamic_slice_in_dim(w["b_ada"], chip * n_mod, n_mod, axis=1)
    mod_loc = _mm("ada_fwd", c_act, w_ada, add=jnp.broadcast_to(b_loc, (c_rows, n_mod)))
    mod_all = _all_gather("gather_mod", mod_loc[:8], CHIPS)
    mod = lax.dynamic_index_in_dim(mod_all, me, 1, keepdims=False).reshape(1, -1)
    shift1, scale1, gate1, shift2, scale2, gate2 = [mod[:, i * d:(i + 1) * d] for i in range(6)]

    small_done = (mod[:, :1] + conv_all[:1, :1] + first_got[0][0, 0, :1, :1].astype(F32))
    later_flight = _split_start(
        "gather_later_start", [halves_bf[k] for k in later_names],
        [jax.ShapeDtypeStruct((4,) + halves_bf[k].shape, BF16) for k in later_names], "gather", after=small_done)

    half = QK_ROPE // 2
    inv_freq = ROPE_THETA ** (-jnp.arange(half, dtype=F32) / half)
    ang = positions[0].astype(F32)[:, None] * inv_freq
    cos, sin = jnp.cos(ang), jnp.sin(ang)
    one, zero = jnp.ones((s_len, QK_NOPE), F32), jnp.zeros((s_len, half), F32)
    tail = jnp.zeros((s_len, LANE - QK_NOPE - QK_ROPE), F32)
    cos_f = jnp.concatenate([one, cos, cos, tail + 1.0], axis=1)
    sin_a = jnp.concatenate([one * 0.0, -sin, zero, tail], axis=1)
    sin_b = jnp.concatenate([one * 0.0, zero, sin, tail], axis=1)
    reset = (positions[0] == 0).astype(F32)[:, None]
    tabs = [(cos_f, (tile, LANE), "row"), (sin_a, (tile, LANE), "row"), (sin_b, (tile, LANE), "row")]

    def rowspec(a):
        return (a, (tile, a.shape[1]), "row")

    def full2(a):
        return (a, a.shape, "full")

    def rowout(cols, dt):
        return ((s_len, cols), dt, (tile, cols), "row")

    def accout(a):
        return (a.shape, F32, a.shape, "acc")

    norm1_g = w["norm1_g"] + later_flight[4][:1, :1]
    norm2_g, final_g = w["norm2_g"], w["final_g"].reshape(1, d)
    ln1_in = [rowspec(x2d), full2(norm1_g), full2(scale1), full2(shift1)]
    big_tile = min(WIDE_ROW_TILE, s_len)
    (h1,) = _tiled("ln1", _f_ln, nt, ln1_in, [rowout(d, BF16)], row_tile=big_tile)
    x_rnn = _mm("in_rnn", h1, w_rnn, out_dtype=BF16)
    qkv = _mm("in_qkv", h1, w_qkv)
    gates = _mm("in_gates", h1, w_g, out_dtype=BF16)

    ct = LANE
    n_ct = d_rnn // ct
    colspec = lambda a, width=ct: (a, (a.shape[0], width), "col")
    lru_in = [colspec(x_rnn), colspec(conv_w), colspec(conv_b), colspec(wa_bd), colspec(w["b_gate_a"]),
              colspec(wx_bd), colspec(w["b_gate_x"]), colspec(w["lru_param"]), full2(reset)]
    y_rnn, h_rnn = _tiled("lru_fwd", _f_lru_fwd, n_ct, lru_in,
                          [((s_len, d_rnn), BF16, (s_len, ct), "col"), ((s_len, d_rnn), F32, (s_len, ct), "col")])

    qkv_in = [rowspec(qkv)] + tabs + [full2(w["q_norm_g"]), full2(w["kv_norm_g"])]
    qn, kvn, kr = _tiled("qkv_norm", _f_qkv, nt, qkv_in, [rowout(n_q, BF16), rowout(n_kv, BF16), rowout(LANE, BF16)],
                         row_tile=big_tile)
    q_pre = _mm("up_q", qn, w_uq, out_dtype=BF16)
    kv = _mm("up_kv", kvn, w_ukv, out_dtype=BF16)
    o_mla, lse, q_cat = _attn_fwd(q_pre, (cos_f, sin_a, sin_b), kv, kr)

    send_sems, recv_sems, flown, landed, _ = later_flight
    landed = _split_wait("gather_later_wait", send_sems, recv_sems, flown, landed, "gather", after=o_mla)
    for k, g in zip(later_names, _relay_sibling(landed)):
        assemble(k, g)
    w_pr = full["w_proj_rnn"]
    assert ATTN_HEADS_PER_STEP == 2 and 2 * v_head == HEAD_PAD
    swap_pairs = lambda a: a.reshape(N_HEADS // 2, 2, v_head, d)[:, ::-1].reshape(-1, d)
    w_pm = swap_pairs(full["w_proj_mla"])
    w_out = full["w_out"]
    w_up_gate, w_up_val = full["w_up_gate"], full["w_up_val"]
    w_down = full["w_down"]

    p_rnn = _mm("proj_rnn", y_rnn, w_pr, out_dtype=BF16)
    p_mla = _mm("proj_mla", o_mla, w_pm, out_dtype=BF16)
    merge_in = [rowspec(gates), rowspec(p_rnn), rowspec(p_mla)]
    (merged,) = _tiled("merge", _f_merge, nt, merge_in, [rowout(d, BF16)], row_tile=big_tile)
    o_tok = _mm("out_proj", merged, w_out)
    res_in = [rowspec(x2d), rowspec(o_tok), full2(gate1), full2(norm2_g), full2(scale2), full2(shift2)]
    x1, h2 = _tiled("res_ln2", _f_res_ln, nt, res_in, [rowout(d, F32), rowout(d, BF16)], row_tile=big_tile)
    u_gate = _mm("ffn_up_gate", h2, w_up_gate, out_dtype=BF16)
    u_val = _mm("ffn_up_val", h2, w_up_val, out_dtype=BF16)
    n_ft = d_ff // LANE
    ffn_in = [colspec(a) for a in (u_gate, u_val, ffn_cw_gate, ffn_cw_val, ffn_cb_gate, ffn_cb_val)]
    (act,) = _tiled("ffn_conv", _f_ffn, n_ft, ffn_in, [((s_len, d_ff), BF16, (s_len, LANE), "col")])
    f_tok = _mm("ffn_down", act, w_down)

    loss_in = [rowspec(x1), rowspec(f_tok), rowspec(tgt), full2(gate2), full2(final_g)]
    dx1, df, loss_row, d_gate2, d_final_g = _tiled(
        "loss", _f_loss_and_grads, nt, loss_in,
        [rowout(d, F32), rowout(d, BF16), ((1, LANE), F32, (1, LANE), "acc"), accout(gate2), accout(final_g)],
        row_tile=big_tile)
    loss = lax.psum(loss_row[0, 0], ("x", "y", "c"))

    d_act = _mm("ffn_down_dx", df, w_down, tb=True, out_dtype=BF16)
    g_w_down = _mm("ffn_down_dw", act, df, ta=True)
    taps = ffn_cw_gate.shape[0]
    du_gate, du_val, g_cw_gate, g_cw_val, g_cb_gate, g_cb_val = _tiled(
        "ffn_conv_bwd", _vjp_of(_f_ffn, 6, (0, 1, 2, 3, 4, 5)), n_ft, ffn_in + [colspec(d_act)],
        [((s_len, d_ff), BF16, (s_len, LANE), "col")] * 2 + [((taps, d_ff), F32, (taps, LANE), "col")] * 2
        + [((1, d_ff), F32, (1, LANE), "col")] * 2)
    dh2 = _mm("ffn_up_gate_dx", du_gate, w_up_gate, tb=True)
    dh2 = _mm("ffn_up_val_dx", du_val, w_up_val, tb=True, add=dh2, out_dtype=BF16)
    g_w_up_halves = [_mm("ffn_up_gate_dw", h2, du_gate, ta=True), _mm("ffn_up_val_dw", h2, du_val, ta=True)]
    g_ffn_cw = jnp.concatenate([g_cw_gate, g_cw_val], axis=1)
    g_ffn_cb = jnp.concatenate([g_cb_gate, g_cb_val], axis=1)

    def chunked(k, gk):
        r, cc = local2d[k].shape
        if kinds[k] == "col":
            gk = gk.reshape(r, 4, cc).transpose(1, 0, 2)
        return gk.reshape(4, 2, r // 2, cc)

    r_up, c_up = local2d["w_up"].shape
    up_chunks = jnp.concatenate([g.reshape(r_up, 2, c_up).transpose(1, 0, 2) for g in g_w_up_halves], axis=0)
    ffn_chunks = {"w_up": up_chunks.reshape(4, 2, r_up // 2, c_up), "w_down": chunked("w_down", g_w_down)}
    ffn_names = [k for k in later_names if k in ffn_chunks]
    ffn_pair_flight = _split_start(
        "reduce_pair_ffn_start", [ffn_chunks[k] for k in ffn_names],
        [jax.ShapeDtypeStruct((4,) + ffn_chunks[k].shape[2:], F32) for k in ffn_names], "pair",
        after=ffn_chunks[ffn_names[-1]])
    gate1_held = gate1 + ffn_pair_flight[4][:1, :1]

    res_bwd = _vjp_of(_f_res_ln, 6, (0, 1, 2, 3, 4, 5))
    dx_res, do_tok, d_gate1, g_norm2, d_scale2, d_shift2 = _tiled(
        "res_ln2_bwd", res_bwd, nt, res_in[:2] + [full2(gate1_held)] + res_in[3:] + [rowspec(dx1), rowspec(dh2)],
        [rowout(d, F32), rowout(d, BF16), accout(gate1), accout(norm2_g), accout(scale2), accout(shift2)],
        row_tile=big_tile)
    d_merged = _mm("out_proj_dx", do_tok, w_out, tb=True, out_dtype=BF16)
    g_w_out = _mm("out_proj_dw", merged, do_tok, ta=True)
    d_gates, dp_rnn, dp_mla = _tiled(
        "merge_bwd", _f_merge_bwd, nt, merge_in + [rowspec(d_merged)],
        [rowout(gates.shape[1], BF16), rowout(d, BF16), rowout(d, BF16)], row_tile=big_tile)
    dy_rnn = _mm("proj_rnn_dx", dp_rnn, w_pr, tb=True, out_dtype=BF16)
    g_w_pr = _mm("proj_rnn_dw", y_rnn, dp_rnn, ta=True)
    do_mla = _mm("proj_mla_dx", dp_mla, w_pm, tb=True, out_dtype=BF16)
    g_w_pm = _mm("proj_mla_dw", o_mla, dp_mla, ta=True)

    core = ci.astype(jnp.int32).reshape(1)

    def pair_sums(tag, names, chunks):
        received = _pair_exchange("reduce_pair_exchange_" + tag, chunks)
        return [_reduce_pair("reduce_pair_" + k, ck, got, core) for k, ck, got in zip(names, chunks, received)]

    g_later = {"w_proj_rnn": g_w_pr, "w_proj_mla": swap_pairs(g_w_pm), "w_out": g_w_out}
    send_sems, recv_sems, flown, landed, _ = ffn_pair_flight
    ffn_received = _split_wait("reduce_pair_ffn_wait", send_sems, recv_sems, flown, landed, "pair", after=g_w_pm)
    sums = {k: _reduce_pair("reduce_pair_" + k, ffn_chunks[k], got, core) for k, got in zip(ffn_names, ffn_received)}
    other_names = [k for k in later_names if k not in ffn_chunks]
    sums.update(zip(other_names, pair_sums("ready", other_names, [chunked(k, g_later[k]) for k in other_names])))
    sums_ready = [sums[k] for k in later_names]
    ready_flight = _split_start(
        "reduce_ready_start", sums_ready, [jax.ShapeDtypeStruct(s.shape, s.dtype) for s in sums_ready], "alltoall",
        after=sums_ready[0])
    kr_held = kr + ready_flight[4][:1, :].astype(BF16)

    dq_cat, dkv, dkr = _attn_bwd(q_cat, kv, kr_held, o_mla, lse, do_mla)
    (dq_pre,) = _tiled("rot_q_bwd", _f_rotq_bwd, nt, tabs + [rowspec(dq_cat)],
                       [rowout(q_pre.shape[1], BF16)], row_tile=big_tile)
    dqn = _mm("up_q_dx", dq_pre, w_uq, tb=True, out_dtype=BF16)
    g_w_uq = _mm("up_q_dw", qn, dq_pre, ta=True)
    dkvn = _mm("up_kv_dx", dkv, w_ukv, tb=True, out_dtype=BF16)
    g_w_ukv = _mm("up_kv_dw", kvn, dkv, ta=True)
    dqkv, g_q_norm, g_kv_norm = _tiled(
        "qkv_norm_bwd", _f_qkv_bwd, nt, qkv_in + [rowspec(dqn), rowspec(dkvn), rowspec(dkr)],
        [rowout(qkv.shape[1], BF16), accout(w["q_norm_g"]), accout(w["kv_norm_g"])], row_tile=big_tile)

    lru_out = [((s_len, d_rnn), BF16, (s_len, ct), "col")]
    for a in (conv_w, conv_b, wa_bd, w["b_gate_a"], wx_bd, w["b_gate_x"], w["lru_param"]):
        lru_out.append((a.shape, F32, (a.shape[0], ct), "col"))
    dx_rnn, g_conv_w, g_conv_b, g_wa_bd, g_b_a, g_wx_bd, g_b_x, g_lru = _tiled(
        "lru_bwd", _f_lru_bwd, n_ct, lru_in + [colspec(h_rnn), colspec(dy_rnn)], lru_out)

    dh1 = _mm("in_gates_dx", d_gates, w_g, tb=True)
    dh1 = _mm("in_qkv_dx", dqkv, w_qkv, tb=True, add=dh1)
    dh1 = _mm("in_rnn_dx", dx_rnn, w_rnn, tb=True, add=dh1)
    g_w_rnn = _mm("in_rnn_dw", h1, dx_rnn, ta=True)
    g_w_qkv = _mm("in_qkv_dw", h1, dqkv, ta=True)
    g_w_g = _mm("in_gates_dw", h1, d_gates, ta=True)

    g_first = {
        "w_in": jnp.concatenate([g_w_rnn, g_w_qkv[:, :n_q + n_kv],
                                 g_w_qkv[:, n_q + n_kv + QK_NOPE:n_q + n_kv + QK_NOPE + QK_ROPE], g_w_g], axis=1),
        "w_uq": g_w_uq.reshape(n_q, N_HEADS, HEAD_PAD)[:, :, :hd].reshape(n_q, -1),
        "w_ukv": g_w_ukv,
    }
    first_chunks = [chunked(k, g_first[k]) for k in first_names]
    first_pair_flight = _split_start(
        "reduce_pair_first_start", first_chunks,
        [jax.ShapeDtypeStruct((4,) + ck.shape[2:], F32) for ck in first_chunks], "pair", after=first_chunks[0])

    ln_bwd = _vjp_of(_f_ln, 4, (0, 1, 2, 3))

    def ln1_bwd(xv, gv, sc, sh, dxr, dh):
        dx, dg, dsc, dsh = ln_bwd(xv, gv, sc, sh, dh)
        return dx + dxr, dg, dsc, dsh

    ln1_held = [ln1_in[0], full2(norm1_g + first_pair_flight[4][:1, :1])] + ln1_in[2:]
    grad_x, g_norm1, d_scale1, d_shift1 = _tiled(
        "ln1_bwd", ln1_bwd, nt, ln1_held + [rowspec(dx_res), rowspec(dh1)],
        [rowout(d, F32), accout(norm1_g), accout(scale1), accout(shift1)], row_tile=big_tile)

    dmod = jnp.concatenate([d_shift1, d_scale1, d_gate1, d_shift2, d_scale2, d_gate2], axis=1)
    dmod_all = _all_gather("gather_dmod", dmod, ALL7).reshape(8, -1)
    dmod_loc = lax.dynamic_slice_in_dim(dmod_all, chip * n_mod, n_mod, axis=1)
    g_w_ada = _mm("ada_dw", c_act, jnp.pad(dmod_loc, ((0, c_rows - 8), (0, 0))), ta=True)

    g_convs = {"conv_w": g_conv_w, "ffn_conv_w": g_ffn_cw}
    g_small = {
        "b_ada": dmod, "norm1_g": g_norm1, "conv_b": g_conv_b,
        "w_gate_a": _block_diag_pairs_t(g_wa_bd)[None], "b_gate_a": g_b_a,
        "w_gate_x": _block_diag_pairs_t(g_wx_bd)[None], "b_gate_x": g_b_x, "lru_param": g_lru,
        "q_norm_g": g_q_norm, "kv_norm_g": g_kv_norm, "norm2_g": g_norm2,
        "ffn_conv_b": g_ffn_cb, "final_g": d_final_g.reshape(w["final_g"].shape),
    }

    small_flat = jnp.concatenate([g_small[k].reshape(-1) for k in SMALL] + [g_convs[k].reshape(-1) for k, _ in CONVS])
    small_rows = -(-small_flat.shape[0] // (8 * PACK_COLS * PACK_ROW_UNIT)) * PACK_ROW_UNIT
    small_chunk = _pad_rows(small_flat[None], 8 * small_rows).reshape(4, 2, small_rows, PACK_COLS)
    last_names = first_names + ["small"]
    send_sems, recv_sems, flown, landed, _ = first_pair_flight
    first_received = _split_wait("reduce_pair_first_wait", send_sems, recv_sems, flown, landed, "pair", after=small_chunk)
    sums_last = [_reduce_pair("reduce_pair_" + k, ck, got, core)
                 for k, ck, got in zip(first_names, first_chunks, first_received)]
    sums_last += pair_sums("small", ["small"], [small_chunk])
    send_sems, recv_sems, flown, landed, _ = ready_flight
    quads_ready = _split_wait("reduce_ready_wait", send_sems, recv_sems, flown, landed, "alltoall", after=grad_x)
    last_flight = _split_start(
        "reduce_last_start", sums_last, [jax.ShapeDtypeStruct(s.shape, s.dtype) for s in sums_last], "alltoall",
        after=quads_ready[0])
    grads = {"w_ada": g_w_ada[None]}
    delta, new_m, new_v = {}, {}, {}

    def adamw(k):
        shp = w[k].shape
        flip = len(shp) == 3 and shp[-1] % LANE != 0 and shp[-2] % LANE == 0
        view = (lambda a: jnp.swapaxes(a, 1, 2)) if flip else (lambda a: a)
        two_d = (-1, view(w[k]).shape[-1]) if len(shp) > 1 else (1, -1)
        dk, mk, vk = _adamw("adamw_" + k, *[view(a).reshape(two_d) for a in (w[k], grads[k], m_in[k], v_in[k])])
        back = lambda a: view(a.reshape(view(w[k]).shape))
        delta[k], new_m[k], new_v[k] = back(dk), back(mk), back(vk)

    def finish(tag, names, quads, sums, after):
        reduced = {}
        for k, quad, ps in zip(names, quads, sums):
            quad = lax.dynamic_update_index_in_dim(quad, lax.dynamic_index_in_dim(ps, chip, 0, keepdims=True), chip, 0)
            reduced[k] = _reduce_quad("reduce_quad_" + k, quad, after)
        big = [k for k in names if k != "small"]
        for k, both in zip(big, _share_sibling("share_sibling_" + tag, [reduced[k] for k in big])):
            grads[k] = lax.dynamic_update_index_in_dim(both, reduced[k][None], ci, 0).reshape(w[k].shape)
        return reduced

    finish("ready", later_names, quads_ready, sums_ready, after=last_flight[4])
    for k in later_names + ["w_ada"]:
        adamw(k)
    send_sems, recv_sems, flown, landed, _ = last_flight
    quads_last = _split_wait("reduce_last_wait", send_sems, recv_sems, flown, landed, "alltoall",
                             after=delta[later_names[-1]])
    reduced = finish("last", last_names, quads_last, sums_last, after=None)
    small_grad = _all_gather("share_small", reduced["small"], ALL7).reshape(-1)
    off = 0
    for k in SMALL:
        grads[k] = small_grad[off:off + w[k].size].reshape(w[k].shape)
        off += w[k].size
    for k, _ in CONVS:
        r, cc = local2d[k].shape
        whole = small_grad[off:off + 4 * r * cc].reshape(r, 4 * cc)
        grads[k] = lax.dynamic_slice_in_dim(whole, chip * cc, cc, axis=1)[None]
        off += 4 * r * cc
    for k in WEIGHTS:
        if k not in delta:
            adamw(k)

    return (loss, grad_x[None], *[grads[k] for k in WEIGHTS], *[delta[k] for k in WEIGHTS],
            *[new_m[k] for k in WEIGHTS], *[new_v[k] for k in WEIGHTS])


def kernel(x, c, positions, w_ada, b_ada, norm1_g, w_in, conv_w, conv_b, w_gate_a, b_gate_a, w_gate_x, b_gate_x, lru_param, q_norm_g, w_uq, kv_norm_g, w_ukv, w_proj_rnn, w_proj_mla, w_out, norm2_g, w_up, ffn_conv_w, ffn_conv_b, w_down, final_g, loss_target, m_w_ada, m_b_ada, m_norm1_g, m_w_in, m_conv_w, m_conv_b, m_w_gate_a, m_b_gate_a, m_w_gate_x, m_b_gate_x, m_lru_param, m_q_norm_g, m_w_uq, m_kv_norm_g, m_w_ukv, m_w_proj_rnn, m_w_proj_mla, m_w_out, m_norm2_g, m_w_up, m_ffn_conv_w, m_ffn_conv_b, m_w_down, m_final_g, v_w_ada, v_b_ada, v_norm1_g, v_w_in, v_conv_w, v_conv_b, v_w_gate_a, v_b_gate_a, v_w_gate_x, v_b_gate_x, v_lru_param, v_q_norm_g, v_w_uq, v_kv_norm_g, v_w_ukv, v_w_proj_rnn, v_w_proj_mla, v_w_out, v_norm2_g, v_w_up, v_ffn_conv_w, v_ffn_conv_b, v_w_down, v_final_g):
    given = dict(locals())
    w = {k: given[k] for k in WEIGHTS}
    m_in = {k: given["m_" + k] for k in WEIGHTS}
    v_in = {k: given["v_" + k] for k in WEIGHTS}
    return _step(x, c, positions, w, m_in, v_in, loss_target)
```
